```python
import math
import jax, jax.numpy as jnp
from jax import lax
import numpy as np

D_MODEL = 1024
BATCH = 4
SEQ = 4096
DEPTH = 1

HEAD_DIM = 64
HEADS_A = 8
HEADS_B = 8
WIDTH_A = HEADS_A * HEAD_DIM
WIDTH_B = HEADS_B * HEAD_DIM
MIX_WIDTH = WIDTH_A + WIDTH_B
IN_WIDTH = 3 * WIDTH_A + 3 * WIDTH_B + HEADS_B
DILATIONS = ((128, 1), (512, 4), (2048, 16))
ROT_DIM = HEAD_DIM // 4
ROPE_THETA = 500000.0
Q_BLOCK = 128
N_GROUPS = 4
EXPERTS_PER_GROUP = 4
N_EXPERTS = N_GROUPS * EXPERTS_PER_GROUP
TOP_K = 2
D_EXPERT = 512
NORM_EPS = 1e-6
NEG_INF = -1e30

kernel_name = "hybrid_dilated_fox_hiermoe_layer"


def rmsnorm(x, g):
    xf = x.astype(jnp.float32)
    y = xf * lax.rsqrt(jnp.mean(xf * xf, axis=-1, keepdims=True) + NORM_EPS)
    return (y * g.astype(jnp.float32)).astype(x.dtype)


def partial_rotary(t, seq_len):
    pos = jnp.arange(seq_len, dtype=jnp.float32)
    inv_freq = 1.0 / (ROPE_THETA ** (jnp.arange(0, ROT_DIM, 2, dtype=jnp.float32) / ROT_DIM))
    ang = pos[:, None] * inv_freq[None, :]
    ang = jnp.concatenate([ang, ang], axis=-1)[:, None, :]
    cos, sin = jnp.cos(ang).astype(t.dtype), jnp.sin(ang).astype(t.dtype)
    tr, tp = t[..., :ROT_DIM], t[..., ROT_DIM:]
    half = ROT_DIM // 2
    rot_half = jnp.concatenate([-tr[..., half:], tr[..., :half]], axis=-1)
    return jnp.concatenate([tr * cos + rot_half * sin, tp], axis=-1)


def dilated_branch(q, k, v, window, dilation):
    b, s_len, h, d = q.shape
    w = window // dilation
    sub_len = s_len // dilation
    nb = -(-sub_len // w)
    pad_len = nb * w

    def split(t):
        t = t.reshape(b, sub_len, dilation, h, d).transpose(0, 2, 3, 1, 4)
        t = jnp.pad(t, ((0, 0), (0, 0), (0, 0), (0, pad_len - sub_len), (0, 0)))
        return t.reshape(b, dilation, h, nb, w, d)

    def with_prev(t):
        prev = jnp.pad(t[:, :, :, :-1], ((0, 0), (0, 0), (0, 0), (1, 0), (0, 0), (0, 0)))
        return jnp.concatenate([prev, t], axis=4)

    qb = split(q)
    kk, vv = with_prev(split(k)), with_prev(split(v))
    sc = jnp.einsum('brhnqd,brhnkd->brhnqk', qb, kk, preferred_element_type=jnp.float32)
    qi = jnp.arange(w)[:, None]
    kj = jnp.arange(2 * w)[None, :]
    dist = qi + w - kj
    band = (dist >= 0) & (dist <= w)
    valid_prev = (jnp.arange(nb)[:, None, None] > 0) | (kj[None] >= w)
    mask = band[None] & valid_prev
    sc = jnp.where(mask, sc, NEG_INF)
    m = jnp.max(sc, axis=-1, keepdims=True)
    p = jnp.exp(sc - m)
    l = jnp.sum(p, axis=-1, keepdims=True)
    o = jnp.einsum('brhnqk,brhnkd->brhnqd', (p / l).astype(v.dtype), vv)
    lse = (m + jnp.log(l))[..., 0]
    o = o.reshape(b, dilation, h, pad_len, d)[:, :, :, :sub_len]
    o = o.transpose(0, 3, 1, 2, 4).reshape(b, s_len, h, d)
    lse = lse.reshape(b, dilation, h, pad_len)[..., :sub_len]
    lse = lse.transpose(0, 3, 1, 2).reshape(b, s_len, h)
    return o, lse


def dilated_attention(q, k, v):
    outs, lses = [], []
    for window, dilation in DILATIONS:
        o, lse = dilated_branch(q, k, v, window, dilation)
        outs.append(o)
        lses.append(lse)
    wts = jax.nn.softmax(jnp.stack(lses, axis=0), axis=0)
    o = jnp.sum(wts[..., None] * jnp.stack(outs, axis=0).astype(jnp.float32), axis=0)
    return o.astype(q.dtype)


def forgetting_attention(q, k, v, log_f):
    b, s_len, h, d = q.shape
    c = jnp.cumsum(log_f, axis=1).transpose(0, 2, 1)
    qh, kh, vh = (t.transpose(0, 2, 1, 3) for t in (q, k, v))
    pos = jnp.arange(s_len)
    n_blocks = s_len // Q_BLOCK

    def block(i):
        start = i * Q_BLOCK
        qb = lax.dynamic_slice_in_dim(qh, start, Q_BLOCK, axis=2)
        cq = lax.dynamic_slice_in_dim(c, start, Q_BLOCK, axis=2)
        sc = jnp.einsum('bhqd,bhkd->bhqk', qb, kh, preferred_element_type=jnp.float32)
        sc = sc + cq[..., :, None] - c[:, :, None, :]
        qpos = start + jnp.arange(Q_BLOCK)
        sc = jnp.where(pos[None, :] <= qpos[:, None], sc, NEG_INF)
        p = jax.nn.softmax(sc, axis=-1)
        return jnp.einsum('bhqk,bhkd->bhqd', p.astype(vh.dtype), vh)

    o = lax.map(block, jnp.arange(n_blocks))
    return o.transpose(1, 0, 3, 2, 4).reshape(b, s_len, h, d)


def hierarchical_moe(h, w_group, w_expert, w_gate_e, w_up_e, w_down_e):
    b, s_len, dm = h.shape
    n = b * s_len
    h2 = h.reshape(n, dm)
    g_logits = jnp.einsum('nd,dg->ng', h2, w_group).astype(jnp.float32)
    p_group = jax.nn.softmax(g_logits, axis=-1)
    g_star = jnp.argmax(g_logits, axis=-1)
    p_top = jnp.take_along_axis(p_group, g_star[:, None], axis=1)[:, 0]
    e_logits = jnp.einsum('nd,de->ne', h2, w_expert).astype(jnp.float32)
    e_logits = e_logits.reshape(n, N_GROUPS, EXPERTS_PER_GROUP)
    sel = jnp.take_along_axis(e_logits, g_star[:, None, None], axis=1)[:, 0]
    top_v, top_i = lax.top_k(sel, TOP_K)
    wk = jax.nn.softmax(top_v, axis=-1) * p_top[:, None]
    eid = g_star[:, None] * EXPERTS_PER_GROUP + top_i
    comb = jnp.sum(jax.nn.one_hot(eid, N_EXPERTS, dtype=jnp.float32) * wk[..., None], axis=1)
    comb = comb.astype(h2.dtype)
    y = jnp.zeros_like(h2)
    for e in range(N_EXPERTS):
        he = jax.nn.silu(h2 @ w_gate_e[e]) * (h2 @ w_up_e[e])
        y = y + comb[:, e:e + 1] * (he @ w_down_e[e])
    return y.reshape(b, s_len, dm)


def setup_inputs(seed: int = 0) -> dict:
    key = jax.random.key(seed)
    ks = jax.random.split(key, 16)
    f32 = jnp.float32
    x = jax.random.normal(ks[0], (BATCH, SEQ, D_MODEL), f32)
    attn_norm = 1.0 + 0.05 * jax.random.normal(ks[1], (DEPTH, D_MODEL), f32)
    w_in = jax.random.normal(ks[2], (DEPTH, D_MODEL, IN_WIDTH), f32) * D_MODEL ** -0.5
    b_forget = 4.0 + 0.5 * jax.random.normal(ks[3], (DEPTH, HEADS_B), f32)
    w_out = jax.random.normal(ks[4], (DEPTH, MIX_WIDTH, D_MODEL), f32) * MIX_WIDTH ** -0.5
    ffn_norm = 1.0 + 0.05 * jax.random.normal(ks[5], (DEPTH, D_MODEL), f32)
    w_group = jax.random.normal(ks[6], (DEPTH, D_MODEL, N_GROUPS), f32) * D_MODEL ** -0.5
    w_expert = jax.random.normal(ks[7], (DEPTH, D_MODEL, N_EXPERTS), f32) * D_MODEL ** -0.5
    w_gate_e = jax.random.normal(ks[8], (DEPTH, N_EXPERTS, D_MODEL, D_EXPERT), f32) * D_MODEL ** -0.5
    w_up_e = jax.random.normal(ks[9], (DEPTH, N_EXPERTS, D_MODEL, D_EXPERT), f32) * D_MODEL ** -0.5
    w_down_e = jax.random.normal(ks[10], (DEPTH, N_EXPERTS, D_EXPERT, D_MODEL), f32) * D_EXPERT ** -0.5
    final_norm = 1.0 + 0.05 * jax.random.normal(ks[11], (D_MODEL,), f32)
    return {"x": x, "attn_norm": attn_norm, "w_in": w_in, "b_forget": b_forget,
            "w_out": w_out, "ffn_norm": ffn_norm, "w_group": w_group,
            "w_expert": w_expert, "w_gate_e": w_gate_e, "w_up_e": w_up_e,
            "w_down_e": w_down_e, "final_norm": final_norm}


def reference(x, attn_norm, w_in, b_forget, w_out, ffn_norm, w_group, w_expert,
              w_gate_e, w_up_e, w_down_e, final_norm):
    b, s_len, _ = x.shape
    scale = HEAD_DIM ** -0.5
    offs = np.cumsum([0, WIDTH_A, WIDTH_A, WIDTH_A, WIDTH_B, WIDTH_B, WIDTH_B, HEADS_B])
    for layer in range(DEPTH):
        h = rmsnorm(x, attn_norm[layer])
        proj = jnp.einsum('bsd,de->bse', h, w_in[layer])
        parts = [proj[..., offs[i]:offs[i + 1]] for i in range(7)]
        qa, ka, va = (t.reshape(b, s_len, HEADS_A, HEAD_DIM) for t in parts[0:3])
        qb, kb, vb = (t.reshape(b, s_len, HEADS_B, HEAD_DIM) for t in parts[3:6])
        qa = partial_rotary(qa, s_len) * scale
        ka = partial_rotary(ka, s_len)
        out_a = dilated_attention(qa, ka, va)
        log_f = jax.nn.log_sigmoid(parts[6].astype(jnp.float32) + b_forget[layer].astype(jnp.float32))
        out_b = forgetting_attention(qb * scale, kb, vb, log_f)
        mixed = jnp.concatenate([out_a.reshape(b, s_len, WIDTH_A),
                                 out_b.reshape(b, s_len, WIDTH_B)], axis=-1)
        x = x + jnp.einsum('bse,ed->bsd', mixed, w_out[layer])
        h2 = rmsnorm(x, ffn_norm[layer])
        x = x + hierarchical_moe(h2, w_group[layer], w_expert[layer], w_gate_e[layer],
                                 w_up_e[layer], w_down_e[layer])
    return rmsnorm(x, final_norm)
```

```python
import functools

import jax
import jax.numpy as jnp
from jax import lax
from jax.experimental import pallas as pl
from jax.experimental.pallas import tpu as pltpu

F32 = jnp.float32
BF16 = jnp.bfloat16

D_MODEL = 1024
HEAD_DIM = 64
HEADS_A = 8
HEADS_B = 8
WIDTH_A = HEADS_A * HEAD_DIM
WIDTH_B = HEADS_B * HEAD_DIM
QKV_WIDTH = 3 * WIDTH_A
DILATIONS = ((128, 1), (512, 4), (2048, 16))
ROT_DIM = HEAD_DIM // 4
ROT_HALF = ROT_DIM // 2
ROPE_THETA = 500000.0
N_GROUPS = 4
EXPERTS_PER_GROUP = 4
N_EXPERTS = N_GROUPS * EXPERTS_PER_GROUP
D_EXPERT = 512
NORM_EPS = 1e-6
NEG_INF = -1e30

LANES = 128
HEAD_PAIRS = WIDTH_A // LANES
WIN = 128
MAX_DIL = 16
PERM = MAX_DIL * MAX_DIL
ROUTER_OFF = N_GROUPS
VMEM_LIMIT = 56 * 1024 * 1024


def _split3(x):
    hi = x.astype(BF16)
    r1 = x - hi.astype(F32)
    mid = r1.astype(BF16)
    lo = (r1 - mid.astype(F32)).astype(BF16)
    return hi, mid, lo


def _inproj_kernel(x_ref, g_ref, w_ref, cos_ref, sa_ref, sb_ref, bf_ref,
                   qkva_ref, qkvb_ref, c_ref, h_scr, carry_scr, *, tiles_per_seq):
    i = pl.program_id(0)
    tm = x_ref.shape[0]
    x = x_ref[...]
    ms = jnp.mean(x * x, axis=-1, keepdims=True)
    h_scr[...] = (x * lax.rsqrt(ms + NORM_EPS) * g_ref[...]).astype(BF16)
    cosv = cos_ref[...]
    sa = sa_ref[...]
    sb = sb_ref[...]
    for g in range(6):
        p = jnp.dot(h_scr[...], w_ref[:, g * WIDTH_A:(g + 1) * WIDTH_A], preferred_element_type=F32)
        dst = qkva_ref if g < 3 else qkvb_ref
        col0 = (g % 3) * WIDTH_A
        if g < 2:
            for k in range(HEAD_PAIRS):
                t = p[:, k * LANES:(k + 1) * LANES]
                t = (t * cosv + pltpu.roll(t, LANES - ROT_HALF, 1) * sa + pltpu.roll(t, ROT_HALF, 1) * sb)
                dst[:, col0 + k * LANES:col0 + (k + 1) * LANES] = t.astype(BF16)
        else:
            dst[:, col0:col0 + WIDTH_A] = p.astype(BF16)
    z = jnp.dot(h_scr[...], w_ref[:, 2 * QKV_WIDTH:2 * QKV_WIDTH + LANES], preferred_element_type=F32) + bf_ref[...]
    lf = jnp.minimum(z, 0.0) - jnp.log1p(jnp.exp(-jnp.abs(z)))
    lane = lax.broadcasted_iota(jnp.int32, (tm, LANES), 1)
    lf = jnp.where(lane < HEADS_B, lf, 0.0)
    hi, mid, lo = _split3(lf)
    tri = (lax.broadcasted_iota(jnp.int32, (tm, tm), 0) >= lax.broadcasted_iota(jnp.int32, (tm, tm), 1)).astype(BF16)
    cs = (jnp.dot(tri, hi, preferred_element_type=F32) + jnp.dot(tri, mid, preferred_element_type=F32)
          + jnp.dot(tri, lo, preferred_element_type=F32))

    @pl.when(i % tiles_per_seq == 0)
    def _():
        carry_scr[...] = jnp.zeros_like(carry_scr)

    c = cs + carry_scr[0:1, :]
    c_ref[...] = c
    carry_scr[...] = jnp.broadcast_to(c[tm - 1:tm, :], carry_scr.shape)


def _inproj(x2, g, w, cosv, sa, sb, bf, *, seq, tm):
    n = x2.shape[0]
    kern = functools.partial(_inproj_kernel, tiles_per_seq=seq // tm)
    tps = seq // tm
    return pl.pallas_call(
        kern,
        grid=(n // tm,),
        in_specs=[
            pl.BlockSpec((tm, D_MODEL), lambda i: (i, 0)),
            pl.BlockSpec((1, D_MODEL), lambda i: (0, 0)),
            pl.BlockSpec(w.shape, lambda i: (0, 0)),
            pl.BlockSpec((tm, LANES), lambda i: (i % tps, 0)),
            pl.BlockSpec((tm, LANES), lambda i: (i % tps, 0)),
            pl.BlockSpec((tm, LANES), lambda i: (i % tps, 0)),
            pl.BlockSpec((1, LANES), lambda i: (0, 0)),
        ],
        out_specs=[
            pl.BlockSpec((tm, QKV_WIDTH), lambda i: (i, 0)),
            pl.BlockSpec((tm, QKV_WIDTH), lambda i: (i, 0)),
            pl.BlockSpec((tm, LANES), lambda i: (i, 0)),
        ],
        out_shape=[
            jax.ShapeDtypeStruct((n, QKV_WIDTH), BF16),
            jax.ShapeDtypeStruct((n, QKV_WIDTH), BF16),
            jax.ShapeDtypeStruct((n, LANES), F32),
        ],
        scratch_shapes=[pltpu.VMEM((tm, D_MODEL), BF16), pltpu.VMEM((8, LANES), F32)],
        compiler_params=pltpu.CompilerParams(dimension_semantics=("arbitrary",), vmem_limit_bytes=VMEM_LIMIT),
        name="inproj",
    )(x2, g, w, cosv, sa, sb, bf)


def _head_column(c_all, lane, head):
    return jnp.sum(jnp.where(lane == head, c_all, 0.0), axis=1, keepdims=True)


def _fox_kernel(q_ref, k_ref, v_ref, cq_ref, ck_ref, o_ref, k0_scr, k1_scr, *, tq, seq, chunk):
    hp = pl.program_id(1)
    qi = pl.program_id(2)

    def augmented(vals, c_all, lane, hl, key_side):
        csel = _head_column(c_all, lane, 2 * hp + hl)
        hi, mid, lo = (t.astype(F32) for t in _split3(csel))
        base = HEAD_DIM if hl == 0 else 0
        ones = (lane >= base + (0 if key_side else 3)) & (lane < base + (3 if key_side else 6))
        o = 3 if key_side else 0
        sgn = -1.0 if key_side else 1.0
        aug = jnp.where(lane == base + o, sgn * hi,
                        jnp.where(lane == base + o + 1, sgn * mid,
                                  jnp.where(lane == base + o + 2, sgn * lo,
                                            jnp.where(ones, 1.0, 0.0))))
        own = (lane < HEAD_DIM) if hl == 0 else (lane >= HEAD_DIM)
        return jnp.where(own, vals, aug).astype(BF16)

    @pl.when(qi == 0)
    def _build_keys():
        lane = lax.broadcasted_iota(jnp.int32, (chunk, LANES), 1)

        def body(ci, carry):
            r0 = pl.multiple_of(ci * chunk, chunk)
            kc = k_ref[0, pl.ds(r0, chunk), :].astype(F32)
            cc = ck_ref[0, pl.ds(r0, chunk), :]
            k0_scr[pl.ds(r0, chunk), :] = augmented(kc, cc, lane, 0, True)
            k1_scr[pl.ds(r0, chunk), :] = augmented(kc, cc, lane, 1, True)
            return carry

        lax.fori_loop(0, seq // chunk, body, 0)

    lane_q = lax.broadcasted_iota(jnp.int32, (tq, LANES), 1)
    q2 = q_ref[0].astype(F32)
    cq = cq_ref[0]
    qh0 = augmented(q2, cq, lane_q, 0, False)
    qh1 = augmented(q2, cq, lane_q, 1, False)
    causal = lax.broadcasted_iota(jnp.int32, (tq, tq), 0) >= lax.broadcasted_iota(jnp.int32, (tq, tq), 1)

    def step(j, carry, masked):
        r0 = pl.multiple_of(j * tq, tq)
        vb = v_ref[0, pl.ds(r0, tq), :]
        new = []
        for hl, (qh, kscr) in enumerate(((qh0, k0_scr), (qh1, k1_scr))):
            m, l, a = carry[3 * hl:3 * hl + 3]
            kb = kscr[pl.ds(r0, tq), :]
            s = lax.dot_general(qh, kb, (((1,), (1,)), ((), ())), preferred_element_type=F32)
            if masked:
                s = jnp.where(causal, s, NEG_INF)
            mn = jnp.maximum(m, jnp.max(s, axis=1, keepdims=True))
            alpha = jnp.exp(m - mn)
            p = jnp.exp(s - mn)
            l = alpha * l + jnp.sum(p, axis=1, keepdims=True)
            a = alpha * a + jnp.dot(p.astype(BF16), vb, preferred_element_type=F32)
            new += [mn, l, a]
        return tuple(new)

    init = (jnp.full((tq, 1), NEG_INF, F32), jnp.zeros((tq, 1), F32), jnp.zeros((tq, LANES), F32)) * 2
    carry = lax.fori_loop(0, qi, lambda j, c: step(j, c, False), init)
    m0, l0, a0, m1, l1, a1 = step(qi, carry, True)
    o_ref[0] = jnp.where(lane_q < HEAD_DIM, a0 / l0, a1 / l1).astype(o_ref.dtype)


def _fox(qkvb, c, *, tq, chunk=512):
    b, seq, _ = qkvb.shape
    kern = functools.partial(_fox_kernel, tq=tq, seq=seq, chunk=chunk)
    return pl.pallas_call(
        kern,
        grid=(b, HEAD_PAIRS, seq // tq),
        in_specs=[
            pl.BlockSpec((1, tq, LANES), lambda bi, hp, qi: (bi, qi, hp)),
            pl.BlockSpec((1, seq, LANES), lambda bi, hp, qi: (bi, 0, HEAD_PAIRS + hp)),
            pl.BlockSpec((1, seq, LANES), lambda bi, hp, qi: (bi, 0, 2 * HEAD_PAIRS + hp)),
            pl.BlockSpec((1, tq, LANES), lambda bi, hp, qi: (bi, qi, 0)),
            pl.BlockSpec((1, seq, LANES), lambda bi, hp, qi: (bi, 0, 0)),
        ],
        out_specs=pl.BlockSpec((1, tq, LANES), lambda bi, hp, qi: (bi, qi, hp)),
        out_shape=jax.ShapeDtypeStruct((b, seq, WIDTH_B), BF16),
        scratch_shapes=[pltpu.VMEM((seq, LANES), BF16), pltpu.VMEM((seq, LANES), BF16)],
        compiler_params=pltpu.CompilerParams(
            dimension_semantics=("arbitrary", "arbitrary", "arbitrary"), vmem_limit_bytes=VMEM_LIMIT),
        name="fox",
    )(qkvb, qkvb, qkvb, c, c)


def _dilated_kernel(q_ref, k_ref, v_ref, o_ref, q16, k16, v16, m_s, l_s, acc_s, on_s, lse_s, *, seq):
    sub = seq // MAX_DIL
    own0 = lax.broadcasted_iota(jnp.int32, (WIN, LANES), 1) < HEAD_DIM
    ri = lax.broadcasted_iota(jnp.int32, (PERM, PERM), 0)
    ci = lax.broadcasted_iota(jnp.int32, (PERM, PERM), 1)
    perm = (((ri >> 4) == (ci & 15)) & ((ri & 15) == (ci >> 4))).astype(BF16)

    qa = lax.broadcasted_iota(jnp.int32, (WIN, 2 * WIN), 0)
    kb_i = lax.broadcasted_iota(jnp.int32, (WIN, 2 * WIN), 1)
    cur = kb_i >= WIN
    kin = kb_i & (WIN - 1)
    dist_nat = qa - kin + jnp.where(cur, 0, WIN)
    band_nat = (dist_nat >= 0) & (dist_nat <= WIN)
    sq = ((qa & 31) << 2) + (qa >> 5)
    sk = ((kin & 31) << 2) + (kin >> 5)
    dist_4 = sq - sk + jnp.where(cur, 0, WIN)
    band_4 = (dist_4 >= 0) & (dist_4 <= WIN)

    def attend(qb, kb, vb, mask, state):
        qf = qb.astype(F32)
        new = []
        for hl in range(2):
            m, l, a = state[3 * hl:3 * hl + 3]
            own = own0 if hl == 0 else jnp.logical_not(own0)
            qh = jnp.where(own, qf, 0.0).astype(BF16)
            s = lax.dot_general(qh, kb, (((1,), (1,)), ((), ())), preferred_element_type=F32)
            s = jnp.where(mask, s, NEG_INF)
            mn = jnp.maximum(m, jnp.max(s, axis=1, keepdims=True))
            alpha = jnp.exp(m - mn)
            p = jnp.exp(s - mn)
            l = alpha * l + jnp.sum(p, axis=1, keepdims=True)
            a = alpha * a + jnp.dot(p.astype(BF16), vb, preferred_element_type=F32)
            new += [mn, l, a]
        return new

    def pack_state(st):
        m0, l0, a0, m1, l1, a1 = st
        return (jnp.where(own0, m0, m1), jnp.where(own0, l0, l1), jnp.where(own0, a0, a1))

    def deint(blk, carry):
        r0 = pl.multiple_of(blk * PERM, PERM)
        j0 = pl.multiple_of(blk * MAX_DIL, MAX_DIL)
        for src, dst in ((q_ref, q16), (k_ref, k16), (v_ref, v16)):
            y = jnp.dot(perm, src[0, pl.ds(r0, PERM), :], preferred_element_type=F32).astype(BF16)
            for r in range(MAX_DIL):
                dst[r, pl.ds(j0, MAX_DIL), :] = y[r * MAX_DIL:(r + 1) * MAX_DIL, :]
        return carry

    lax.fori_loop(0, seq // PERM, deint, 0)

    fresh = [jnp.full((WIN, 1), NEG_INF, F32), jnp.zeros((WIN, 1), F32), jnp.zeros((WIN, LANES), F32)] * 2
    nb16 = sub // WIN

    def d16_body(t, carry):
        r = t // nb16
        n = t % nb16
        c0 = pl.multiple_of(n * WIN, WIN)
        p0 = pl.multiple_of(jnp.maximum(n - 1, 0) * WIN, WIN)
        qb = q16[r, pl.ds(c0, WIN), :]
        kb = jnp.concatenate([k16[r, pl.ds(p0, WIN), :], k16[r, pl.ds(c0, WIN), :]], axis=0)
        vb = jnp.concatenate([v16[r, pl.ds(p0, WIN), :], v16[r, pl.ds(c0, WIN), :]], axis=0)
        mask = band_nat & (cur | (n > 0))
        mm, ll, aa = pack_state(attend(qb, kb, vb, mask, fresh))
        row = pl.multiple_of(r * sub + c0, WIN)
        m_s[pl.ds(row, WIN), :] = mm
        l_s[pl.ds(row, WIN), :] = ll
        acc_s[pl.ds(row, WIN), :] = aa
        return carry

    lax.fori_loop(0, MAX_DIL * nb16, d16_body, 0)

    nb4 = (seq // 4) // WIN
    ch = WIN // 4

    def d4_body(t, carry):
        r4 = t // nb4
        n = t % nb4
        c0 = pl.multiple_of(n * ch, ch)
        p0 = pl.multiple_of(jnp.maximum(n - 1, 0) * ch, ch)

        def gather(ref, j0):
            return [ref[r4 + 4 * q, pl.ds(j0, ch), :] for q in range(4)]

        qb = jnp.concatenate(gather(q16, c0), axis=0)
        kb = jnp.concatenate(gather(k16, p0) + gather(k16, c0), axis=0)
        vb = jnp.concatenate(gather(v16, p0) + gather(v16, c0), axis=0)
        rows = [pl.multiple_of((r4 + 4 * q) * sub + c0, ch) for q in range(4)]
        mm = jnp.concatenate([m_s[pl.ds(rw, ch), :] for rw in rows], axis=0)
        ll = jnp.concatenate([l_s[pl.ds(rw, ch), :] for rw in rows], axis=0)
        aa = jnp.concatenate([acc_s[pl.ds(rw, ch), :] for rw in rows], axis=0)
        state = [mm[:, 0:1], ll[:, 0:1], aa, mm[:, HEAD_DIM:HEAD_DIM + 1], ll[:, HEAD_DIM:HEAD_DIM + 1], aa]
        mask = band_4 & (cur | (n > 0))
        mm, ll, aa = pack_state(attend(qb, kb, vb, mask, state))
        for q, rw in enumerate(rows):
            m_s[pl.ds(rw, ch), :] = mm[q * ch:(q + 1) * ch, :]
            l_s[pl.ds(rw, ch), :] = ll[q * ch:(q + 1) * ch, :]
            acc_s[pl.ds(rw, ch), :] = aa[q * ch:(q + 1) * ch, :]
        return carry

    lax.fori_loop(0, 4 * nb4, d4_body, 0)

    def renat(blk, carry):
        j0 = pl.multiple_of(blk * MAX_DIL, MAX_DIL)
        r0 = pl.multiple_of(blk * PERM, PERM)
        rows = [pl.multiple_of(r * sub + j0, MAX_DIL) for r in range(MAX_DIL)]
        mm = jnp.concatenate([m_s[pl.ds(rw, MAX_DIL), :] for rw in rows], axis=0)
        ll = jnp.concatenate([l_s[pl.ds(rw, MAX_DIL), :] for rw in rows], axis=0)
        aa = jnp.concatenate([acc_s[pl.ds(rw, MAX_DIL), :] for rw in rows], axis=0)
        o = (aa / ll).astype(BF16)
        hi, mid, lo = _split3(mm + jnp.log(ll))
        on_s[pl.ds(r0, PERM), :] = jnp.dot(perm, o, preferred_element_type=F32)
        lse_s[pl.ds(r0, PERM), :] = (jnp.dot(perm, hi, preferred_element_type=F32)
                                     + jnp.dot(perm, mid, preferred_element_type=F32)
                                     + jnp.dot(perm, lo, preferred_element_type=F32))
        return carry

    lax.fori_loop(0, seq // PERM, renat, 0)

    def d1_body(n, carry):
        c0 = pl.multiple_of(n * WIN, WIN)
        p0 = pl.multiple_of(jnp.maximum(n - 1, 0) * WIN, WIN)
        qb = q_ref[0, pl.ds(c0, WIN), :]
        kb = jnp.concatenate([k_ref[0, pl.ds(p0, WIN), :], k_ref[0, pl.ds(c0, WIN), :]], axis=0)
        vb = jnp.concatenate([v_ref[0, pl.ds(p0, WIN), :], v_ref[0, pl.ds(c0, WIN), :]], axis=0)
        lse = lse_s[pl.ds(c0, WIN), :]
        o = on_s[pl.ds(c0, WIN), :]
        one = jnp.ones((WIN, 1), F32)
        state = [lse[:, 0:1], one, o, lse[:, HEAD_DIM:HEAD_DIM + 1], one, o]
        mask = band_nat & (cur | (n > 0))
        m0, l0, a0, m1, l1, a1 = attend(qb, kb, vb, mask, state)
        o_ref[0, pl.ds(c0, WIN), :] = jnp.where(own0, a0 / l0, a1 / l1).astype(o_ref.dtype)
        return carry

    lax.fori_loop(0, seq // WIN, d1_body, 0)


def _dilated(qkva):
    b, seq, _ = qkva.shape
    sub = seq // MAX_DIL
    kern = functools.partial(_dilated_kernel, seq=seq)
    return pl.pallas_call(
        kern,
        grid=(b, HEAD_PAIRS),
        in_specs=[
            pl.BlockSpec((1, seq, LANES), lambda bi, hp: (bi, 0, hp)),
            pl.BlockSpec((1, seq, LANES), lambda bi, hp: (bi, 0, HEAD_PAIRS + hp)),
            pl.BlockSpec((1, seq, LANES), lambda bi, hp: (bi, 0, 2 * HEAD_PAIRS + hp)),
        ],
        out_specs=pl.BlockSpec((1, seq, LANES), lambda bi, hp: (bi, 0, hp)),
        out_shape=jax.ShapeDtypeStruct((b, seq, WIDTH_A), BF16),
        scratch_shapes=[
            pltpu.VMEM((MAX_DIL, sub, LANES), BF16),
            pltpu.VMEM((MAX_DIL, sub, LANES), BF16),
            pltpu.VMEM((MAX_DIL, sub, LANES), BF16),
            pltpu.VMEM((seq, LANES), F32),
            pltpu.VMEM((seq, LANES), F32),
            pltpu.VMEM((seq, LANES), F32),
            pltpu.VMEM((seq, LANES), F32),
            pltpu.VMEM((seq, LANES), F32),
        ],
        compiler_params=pltpu.CompilerParams(
            dimension_semantics=("arbitrary", "arbitrary"), vmem_limit_bytes=VMEM_LIMIT),
        name="dilated",
    )(qkva, qkva, qkva)


def _outproj_kernel(oa_ref, ob_ref, x_ref, wo_ref, g_ref, wr_ref, x1_ref, h2_ref, comb_ref):
    tm = x_ref.shape[0]
    x1 = (x_ref[...]
          + jnp.dot(oa_ref[...], wo_ref[0:WIDTH_A, :], preferred_element_type=F32)
          + jnp.dot(ob_ref[...], wo_ref[WIDTH_A:WIDTH_A + WIDTH_B, :], preferred_element_type=F32))
    x1_ref[...] = x1
    ms = jnp.mean(x1 * x1, axis=-1, keepdims=True)
    h2 = x1 * lax.rsqrt(ms + NORM_EPS) * g_ref[...]
    h2_ref[...] = h2.astype(BF16)
    logits = jnp.dot(h2, wr_ref[...], preferred_element_type=F32, precision=lax.Precision.HIGHEST)
    lane = lax.broadcasted_iota(jnp.int32, (tm, LANES), 1)
    lane_f = lane.astype(F32)
    big = float(LANES)
    gmask = lane < N_GROUPS
    gl = jnp.where(gmask, logits, NEG_INF)
    gmax = jnp.max(gl, axis=1, keepdims=True)
    gsum = jnp.sum(jnp.where(gmask, jnp.exp(gl - gmax), 0.0), axis=1, keepdims=True)
    p_top = 1.0 / gsum
    g_star = jnp.min(jnp.where(gmask & (gl == gmax), lane_f, big), axis=1, keepdims=True)
    lo_lane = ROUTER_OFF + EXPERTS_PER_GROUP * g_star
    emask = (lane_f >= lo_lane) & (lane_f < lo_lane + EXPERTS_PER_GROUP)
    sel = jnp.where(emask, logits, NEG_INF)
    v1 = jnp.max(sel, axis=1, keepdims=True)
    i1 = jnp.min(jnp.where(emask & (sel == v1), lane_f, big), axis=1, keepdims=True)
    rest = emask & (lane_f != i1)
    sel2 = jnp.where(rest, logits, NEG_INF)
    v2 = jnp.max(sel2, axis=1, keepdims=True)
    i2 = jnp.min(jnp.where(rest & (sel2 == v2), lane_f, big), axis=1, keepdims=True)
    e2 = jnp.exp(v2 - v1)
    w1 = p_top / (1.0 + e2)
    w2 = p_top * e2 / (1.0 + e2)
    comb_ref[...] = jnp.where(lane_f == i1, w1, jnp.where(lane_f == i2, w2, 0.0))


def _outproj(oa, ob, x2, wo, g, wr, *, tm):
    n = x2.shape[0]
    return pl.pallas_call(
        _outproj_kernel,
        grid=(n // tm,),
        in_specs=[
            pl.BlockSpec((tm, WIDTH_A), lambda i: (i, 0)),
            pl.BlockSpec((tm, WIDTH_B), lambda i: (i, 0)),
            pl.BlockSpec((tm, D_MODEL), lambda i: (i, 0)),
            pl.BlockSpec(wo.shape, lambda i: (0, 0)),
            pl.BlockSpec((1, D_MODEL), lambda i: (0, 0)),
            pl.BlockSpec(wr.shape, lambda i: (0, 0)),
        ],
        out_specs=[
            pl.BlockSpec((tm, D_MODEL), lambda i: (i, 0)),
            pl.BlockSpec((tm, D_MODEL), lambda i: (i, 0)),
            pl.BlockSpec((tm, LANES), lambda i: (i, 0)),
        ],
        out_shape=[
            jax.ShapeDtypeStruct((n, D_MODEL), F32),
            jax.ShapeDtypeStruct((n, D_MODEL), BF16),
            jax.ShapeDtypeStruct((n, LANES), F32),
        ],
        compiler_params=pltpu.CompilerParams(dimension_semantics=("arbitrary",), vmem_limit_bytes=VMEM_LIMIT),
        name="outproj",
    )(oa, ob, x2, wo, g, wr)


def _moe_kernel(h2_ref, comb_ref, x1_ref, wg_ref, wu_ref, wd_ref, g_ref, out_ref, acc_scr):
    e = pl.program_id(1)
    tm = h2_ref.shape[0]

    @pl.when(e == 0)
    def _():
        acc_scr[...] = jnp.zeros_like(acc_scr)

    h2 = h2_ref[...]
    gate = jnp.dot(h2, wg_ref[0], preferred_element_type=F32)
    up = jnp.dot(h2, wu_ref[0], preferred_element_type=F32)
    he = (gate / (1.0 + jnp.exp(-gate)) * up).astype(BF16)
    y = jnp.dot(he, wd_ref[0], preferred_element_type=F32)
    lane = lax.broadcasted_iota(jnp.int32, (tm, LANES), 1)
    ce = jnp.sum(jnp.where(lane == ROUTER_OFF + e, comb_ref[...], 0.0), axis=1, keepdims=True)
    acc_scr[...] += ce * y

    @pl.when(e == N_EXPERTS - 1)
    def _():
        x2 = x1_ref[...] + acc_scr[...]
        ms = jnp.mean(x2 * x2, axis=-1, keepdims=True)
        out_ref[...] = x2 * lax.rsqrt(ms + NORM_EPS) * g_ref[...]


def _moe(h2, comb, x1, wg, wu, wd, g, *, tm):
    n = h2.shape[0]
    return pl.pallas_call(
        _moe_kernel,
        grid=(n // tm, N_EXPERTS),
        in_specs=[
            pl.BlockSpec((tm, D_MODEL), lambda i, e: (i, 0)),
            pl.BlockSpec((tm, LANES), lambda i, e: (i, 0)),
            pl.BlockSpec((tm, D_MODEL), lambda i, e: (i, 0)),
            pl.BlockSpec((1, D_MODEL, D_EXPERT), lambda i, e: (e, 0, 0)),
            pl.BlockSpec((1, D_MODEL, D_EXPERT), lambda i, e: (e, 0, 0)),
            pl.BlockSpec((1, D_EXPERT, D_MODEL), lambda i, e: (e, 0, 0)),
            pl.BlockSpec((1, D_MODEL), lambda i, e: (0, 0)),
        ],
        out_specs=pl.BlockSpec((tm, D_MODEL), lambda i, e: (i, 0)),
        out_shape=jax.ShapeDtypeStruct((n, D_MODEL), F32),
        scratch_shapes=[pltpu.VMEM((tm, D_MODEL), F32)],
        compiler_params=pltpu.CompilerParams(
            dimension_semantics=("arbitrary", "arbitrary"), vmem_limit_bytes=VMEM_LIMIT),
        name="moe",
    )(h2, comb, x1, wg, wu, wd, g)


def _rotary_tables(seq):
    pos = jnp.arange(seq, dtype=F32)
    inv_freq = 1.0 / (ROPE_THETA ** (jnp.arange(0, ROT_DIM, 2, dtype=F32) / ROT_DIM))
    ang = pos[:, None] * inv_freq[None, :]
    cos, sin = jnp.cos(ang), jnp.sin(ang)
    zeros = jnp.zeros((seq, HEAD_DIM - ROT_DIM), F32)
    zh = jnp.zeros((seq, ROT_HALF), F32)
    cos_h = jnp.concatenate([cos, cos, jnp.ones((seq, HEAD_DIM - ROT_DIM), F32)], axis=1)
    sa_h = jnp.concatenate([-sin, zh, zeros], axis=1)
    sb_h = jnp.concatenate([zh, sin, zeros], axis=1)
    tile = lambda t: jnp.concatenate([t, t], axis=1)
    return tile(cos_h), tile(sa_h), tile(sb_h)


def kernel(x, attn_norm, w_in, b_forget, w_out, ffn_norm, w_group, w_expert, w_gate_e, w_up_e, w_down_e, final_norm):
    b, seq, d = x.shape
    assert d == D_MODEL and w_in.shape[0] == 1, "single-layer block"
    n = b * seq
    scale = HEAD_DIM ** -0.5
    col_scale = jnp.ones((2 * QKV_WIDTH + HEADS_B,), F32)
    col_scale = col_scale.at[0:WIDTH_A].set(scale).at[QKV_WIDTH:QKV_WIDTH + WIDTH_B].set(scale)
    w = jnp.pad(w_in[0] * col_scale[None, :], ((0, 0), (0, LANES - HEADS_B))).astype(BF16)
    bf = jnp.pad(b_forget[0].astype(F32), (0, LANES - HEADS_B))[None, :]
    cosv, sa, sb = _rotary_tables(seq)
    x2 = x.reshape(n, d)

    qkva, qkvb, c = _inproj(x2, attn_norm[0][None, :], w, cosv, sa, sb, bf, seq=seq, tm=512)
    out_a = _dilated(qkva.reshape(b, seq, QKV_WIDTH))
    out_b = _fox(qkvb.reshape(b, seq, QKV_WIDTH), c.reshape(b, seq, LANES), tq=256)

    wr = jnp.pad(jnp.concatenate([w_group[0], w_expert[0]], axis=1).astype(F32),
                 ((0, 0), (0, LANES - N_GROUPS - N_EXPERTS)))
    x1, h2, comb = _outproj(out_a.reshape(n, WIDTH_A), out_b.reshape(n, WIDTH_B), x2,
                            w_out[0].astype(BF16), ffn_norm[0][None, :], wr, tm=512)
    out = _moe(h2, comb, x1, w_gate_e[0].astype(BF16), w_up_e[0].astype(BF16), w_down_e[0].astype(BF16),
               final_norm[None, :], tm=1024)
    return out.reshape(b, seq, d)
```

```python
import functools

import jax
import jax.numpy as jnp
from jax import lax
from jax.experimental import pallas as pl
from jax.experimental.pallas import tpu as pltpu

F32 = jnp.float32
BF16 = jnp.bfloat16

D_MODEL = 1024
HEAD_DIM = 64
HEADS_A = 8
HEADS_B = 8
WIDTH_A = HEADS_A * HEAD_DIM
WIDTH_B = HEADS_B * HEAD_DIM
QKV_WIDTH = 3 * WIDTH_A
DILATIONS = ((128, 1), (512, 4), (2048, 16))
ROT_DIM = HEAD_DIM // 4
ROT_HALF = ROT_DIM // 2
ROPE_THETA = 500000.0
N_GROUPS = 4
EXPERTS_PER_GROUP = 4
N_EXPERTS = N_GROUPS * EXPERTS_PER_GROUP
D_EXPERT = 512
NORM_EPS = 1e-6
NEG_INF = -1e30

LANES = 128
HEAD_PAIRS = WIDTH_A // LANES
WIN = 128
MAX_DIL = 16
PERM = MAX_DIL * MAX_DIL
KV_BLOCK = 256
ROUTER_OFF = N_GROUPS
VMEM_LIMIT = 56 * 1024 * 1024
LOG2E = 1.4426950408889634
BOUND_SLACK = 1.0 + 2.0 ** -7
BOUND_EPS = 2.0 ** -7
GAP_MAX = 64.0


def _split3(x):
    hi = x.astype(BF16)
    r1 = x - hi.astype(F32)
    mid = r1.astype(BF16)
    lo = (r1 - mid.astype(F32)).astype(BF16)
    return hi, mid, lo


def _inproj_kernel(x_ref, g_ref, w_ref, cos_ref, sa_ref, sb_ref, bf_ref,
                   qkva_ref, qkvb_ref, c_ref, h_scr, carry_scr, *, tiles_per_seq):
    i = pl.program_id(0)
    tm = x_ref.shape[0]
    x = x_ref[...]
    ms = jnp.mean(x * x, axis=-1, keepdims=True)
    h_scr[...] = (x * lax.rsqrt(ms + NORM_EPS) * g_ref[...]).astype(BF16)
    cosv = cos_ref[...]
    sa = sa_ref[...]
    sb = sb_ref[...]
    for g in range(6):
        p = jnp.dot(h_scr[...], w_ref[:, g * WIDTH_A:(g + 1) * WIDTH_A], preferred_element_type=F32)
        dst = qkva_ref if g < 3 else qkvb_ref
        col0 = (g % 3) * WIDTH_A
        if g < 2:
            for k in range(HEAD_PAIRS):
                t = p[:, k * LANES:(k + 1) * LANES]
                t = (t * cosv + pltpu.roll(t, LANES - ROT_HALF, 1) * sa + pltpu.roll(t, ROT_HALF, 1) * sb)
                dst[:, col0 + k * LANES:col0 + (k + 1) * LANES] = t.astype(BF16)
        else:
            dst[:, col0:col0 + WIDTH_A] = p.astype(BF16)
    z = jnp.dot(h_scr[...], w_ref[:, 2 * QKV_WIDTH:2 * QKV_WIDTH + LANES], preferred_element_type=F32) + bf_ref[...]
    lf = jnp.minimum(z, 0.0) - jnp.log1p(jnp.exp(-jnp.abs(z)))
    lane = lax.broadcasted_iota(jnp.int32, (tm, LANES), 1)
    lf = jnp.where(lane < HEADS_B, lf * LOG2E, 0.0)
    hi, mid, lo = _split3(lf)
    tri = (lax.broadcasted_iota(jnp.int32, (tm, tm), 0) >= lax.broadcasted_iota(jnp.int32, (tm, tm), 1)).astype(BF16)
    cs = (jnp.dot(tri, hi, preferred_element_type=F32) + jnp.dot(tri, mid, preferred_element_type=F32)
          + jnp.dot(tri, lo, preferred_element_type=F32))

    @pl.when(i % tiles_per_seq == 0)
    def _():
        carry_scr[...] = jnp.zeros_like(carry_scr)

    c = cs + carry_scr[0:1, :]
    c_ref[...] = c
    carry_scr[...] = jnp.broadcast_to(c[tm - 1:tm, :], carry_scr.shape)


def _inproj(x2, g, w, cosv, sa, sb, bf, *, seq, tm):
    n = x2.shape[0]
    kern = functools.partial(_inproj_kernel, tiles_per_seq=seq // tm)
    tps = seq // tm
    return pl.pallas_call(
        kern,
        grid=(n // tm,),
        in_specs=[
            pl.BlockSpec((tm, D_MODEL), lambda i: (i, 0)),
            pl.BlockSpec((1, D_MODEL), lambda i: (0, 0)),
            pl.BlockSpec(w.shape, lambda i: (0, 0)),
            pl.BlockSpec((tm, LANES), lambda i: (i % tps, 0)),
            pl.BlockSpec((tm, LANES), lambda i: (i % tps, 0)),
            pl.BlockSpec((tm, LANES), lambda i: (i % tps, 0)),
            pl.BlockSpec((1, LANES), lambda i: (0, 0)),
        ],
        out_specs=[
            pl.BlockSpec((tm, QKV_WIDTH), lambda i: (i, 0)),
            pl.BlockSpec((tm, QKV_WIDTH), lambda i: (i, 0)),
            pl.BlockSpec((tm, LANES), lambda i: (i, 0)),
        ],
        out_shape=[
            jax.ShapeDtypeStruct((n, QKV_WIDTH), BF16),
            jax.ShapeDtypeStruct((n, QKV_WIDTH), BF16),
            jax.ShapeDtypeStruct((n, LANES), F32),
        ],
        scratch_shapes=[pltpu.VMEM((tm, D_MODEL), BF16), pltpu.VMEM((8, LANES), F32)],
        compiler_params=pltpu.CompilerParams(dimension_semantics=("arbitrary",), vmem_limit_bytes=VMEM_LIMIT),
        name="inproj",
    )(x2, g, w, cosv, sa, sb, bf)


def _head_column(c_all, lane, head):
    return jnp.sum(jnp.where(lane == head, c_all, 0.0), axis=1, keepdims=True)


def _own_lanes(lane, hl):
    return (lane < HEAD_DIM) if hl == 0 else (lane >= HEAD_DIM)


def _own_sum(vals, lane, hl):
    return jnp.sum(jnp.where(_own_lanes(lane, hl), vals, 0.0), axis=1, keepdims=True)


def _norm_bound(qsq, ksq_max):
    return jnp.sqrt(qsq * ksq_max) * BOUND_SLACK + BOUND_EPS


def _fox_kernel(q_ref, k_ref, v_ref, cq_ref, ck_ref, o_ref,
                k0_scr, k1_scr, v0_scr, v1_scr, ksq_scr, acc0_scr, acc1_scr, s_scr, *, tq, seq, chunk):
    hp = pl.program_id(1)
    qi = pl.program_id(2)
    k_scr = (k0_scr, k1_scr)
    v_scr = (v0_scr, v1_scr)
    acc_scr = (acc0_scr, acc1_scr)

    def augmented(vals, cterm, lane, hl, key_side):
        hi, mid, lo = (t.astype(F32) for t in _split3(cterm))
        base = HEAD_DIM if hl == 0 else 0
        ones = (lane >= base + (0 if key_side else 3)) & (lane < base + (3 if key_side else 6))
        o = 3 if key_side else 0
        sgn = -1.0 if key_side else 1.0
        aug = jnp.where(lane == base + o, sgn * hi,
                        jnp.where(lane == base + o + 1, sgn * mid,
                                  jnp.where(lane == base + o + 2, sgn * lo,
                                            jnp.where(ones, 1.0, 0.0))))
        return jnp.where(_own_lanes(lane, hl), vals, aug).astype(BF16)

    @pl.when(qi == 0)
    def _build_keys():
        lane = lax.broadcasted_iota(jnp.int32, (chunk, LANES), 1)

        def body(ci, ksq):
            r0 = pl.multiple_of(ci * chunk, chunk)
            kc = k_ref[0, pl.ds(r0, chunk), :].astype(F32)
            vc = v_ref[0, pl.ds(r0, chunk), :].astype(F32)
            cc = ck_ref[0, pl.ds(r0, chunk), :]
            new = []
            for hl in range(2):
                k_scr[hl][pl.ds(r0, chunk), :] = augmented(kc, _head_column(cc, lane, 2 * hp + hl), lane, hl, True)
                one_lane = HEAD_DIM if hl == 0 else 0
                v_scr[hl][pl.ds(r0, chunk), :] = jnp.where(
                    _own_lanes(lane, hl), vc, jnp.where(lane == one_lane, 1.0, 0.0)).astype(BF16)
                new.append(jnp.maximum(ksq[hl], jnp.max(_own_sum(kc * kc, lane, hl), axis=0, keepdims=True)))
            return tuple(new)

        zero = jnp.zeros((1, 1), F32)
        ksq0, ksq1 = lax.fori_loop(0, seq // chunk, body, (zero, zero))
        lane8 = lax.broadcasted_iota(jnp.int32, ksq_scr.shape, 1)
        ksq_scr[...] = jnp.where(lane8 < HEAD_DIM, ksq0, ksq1)

    lane_q = lax.broadcasted_iota(jnp.int32, (tq, LANES), 1)
    q2 = q_ref[0].astype(F32)
    cq = cq_ref[0]
    kd = k_ref[0, pl.ds(pl.multiple_of(qi * tq, tq), tq), :].astype(F32)
    ksq_row = ksq_scr[0:1, :]
    qh = []
    gap = None
    for hl in range(2):
        bound = _norm_bound(_own_sum(q2 * q2, lane_q, hl), ksq_row[:, hl * HEAD_DIM:hl * HEAD_DIM + 1])
        diag = _own_sum(q2 * kd, lane_q, hl)
        g = jnp.max(bound - diag)
        gap = g if gap is None else jnp.maximum(gap, g)
        qh.append(augmented(q2, _head_column(cq, lane_q, 2 * hp + hl) - bound, lane_q, hl, False))
    n_diag = tq // KV_BLOCK
    n_full = qi * n_diag
    row_minus_col = (lax.broadcasted_iota(jnp.int32, (tq, KV_BLOCK), 0)
                     - lax.broadcasted_iota(jnp.int32, (tq, KV_BLOCK), 1))

    def causal_mask(s, d):
        return jnp.where(row_minus_col >= d * KV_BLOCK, s, NEG_INF)

    def logits(hl, j):
        r0 = pl.multiple_of(j * KV_BLOCK, KV_BLOCK)
        return lax.dot_general(qh[hl], k_scr[hl][pl.ds(r0, KV_BLOCK), :], (((1,), (1,)), ((), ())),
                               preferred_element_type=F32)

    def values(hl, j):
        return v_scr[hl][pl.ds(pl.multiple_of(j * KV_BLOCK, KV_BLOCK), KV_BLOCK), :]

    def finish(a0, a1):
        l0 = a0[:, HEAD_DIM:HEAD_DIM + 1]
        l1 = a1[:, 0:1]
        o_ref[0] = jnp.where(lane_q < HEAD_DIM, a0 / l0, a1 / l1).astype(o_ref.dtype)

    @pl.when(gap <= GAP_MAX)
    def _fast():
        acc0_scr[...] = jnp.zeros_like(acc0_scr)
        acc1_scr[...] = jnp.zeros_like(acc1_scr)

        def produce(j, slot):
            for hl in range(2):
                s_scr[slot, hl] = logits(hl, j)

        def consume(j, slot, diag):
            for hl in range(2):
                s = s_scr[slot, hl]
                if diag is not None:
                    s = causal_mask(s, diag)
                acc_scr[hl][...] += jnp.dot(jnp.exp2(s).astype(BF16), values(hl, j), preferred_element_type=F32)

        produce(0, 0)

        def body(j, carry):
            slot = j & 1
            consume(j, slot, None)
            produce(j + 1, 1 - slot)
            return carry

        lax.fori_loop(0, n_full, body, 0)
        for d in range(n_diag):
            j = n_full + d
            consume(j, j & 1, d)
            if d + 1 < n_diag:
                produce(j + 1, (j + 1) & 1)
        finish(acc0_scr[...], acc1_scr[...])

    @pl.when(gap > GAP_MAX)
    def _general():
        def step(j, carry, diag):
            new = []
            for hl in range(2):
                m, a = carry[2 * hl:2 * hl + 2]
                s = logits(hl, j)
                if diag is not None:
                    s = causal_mask(s, diag)
                mn = jnp.maximum(m, jnp.max(s, axis=1, keepdims=True))
                a = jnp.exp2(m - mn) * a + jnp.dot(jnp.exp2(s - mn).astype(BF16), values(hl, j),
                                                  preferred_element_type=F32)
                new += [mn, a]
            return tuple(new)

        carry = (jnp.full((tq, 1), NEG_INF, F32), jnp.zeros((tq, LANES), F32)) * 2
        carry = lax.fori_loop(0, n_full, lambda j, c: step(j, c, None), carry)
        for d in range(n_diag):
            carry = step(n_full + d, carry, d)
        finish(carry[1], carry[3])


def _fox(qkvb, c, *, tq, chunk=512):
    b, seq, _ = qkvb.shape
    kern = functools.partial(_fox_kernel, tq=tq, seq=seq, chunk=chunk)
    return pl.pallas_call(
        kern,
        grid=(b, HEAD_PAIRS, seq // tq),
        in_specs=[
            pl.BlockSpec((1, tq, LANES), lambda bi, hp, qi: (bi, qi, hp)),
            pl.BlockSpec((1, seq, LANES), lambda bi, hp, qi: (bi, 0, HEAD_PAIRS + hp)),
            pl.BlockSpec((1, seq, LANES), lambda bi, hp, qi: (bi, 0, 2 * HEAD_PAIRS + hp)),
            pl.BlockSpec((1, tq, LANES), lambda bi, hp, qi: (bi, qi, 0)),
            pl.BlockSpec((1, seq, LANES), lambda bi, hp, qi: (bi, 0, 0)),
        ],
        out_specs=pl.BlockSpec((1, tq, LANES), lambda bi, hp, qi: (bi, qi, hp)),
        out_shape=jax.ShapeDtypeStruct((b, seq, WIDTH_B), BF16),
        scratch_shapes=[pltpu.VMEM((seq, LANES), BF16), pltpu.VMEM((seq, LANES), BF16),
                        pltpu.VMEM((seq, LANES), BF16), pltpu.VMEM((seq, LANES), BF16),
                        pltpu.VMEM((8, LANES), F32),
                        pltpu.VMEM((tq, LANES), F32), pltpu.VMEM((tq, LANES), F32),
                        pltpu.VMEM((2, 2, tq, KV_BLOCK), F32)],
        compiler_params=pltpu.CompilerParams(
            dimension_semantics=("arbitrary", "arbitrary", "arbitrary"), vmem_limit_bytes=VMEM_LIMIT),
        name="fox",
    )(qkvb, qkvb, qkvb, c, c)


def _dilated_kernel(q_ref, k_ref, v_ref, o_ref, q16, k16, v16, m_s, l_s, acc_s, on_s, lse_s, *, seq):
    sub = seq // MAX_DIL
    own0 = lax.broadcasted_iota(jnp.int32, (WIN, LANES), 1) < HEAD_DIM
    ri = lax.broadcasted_iota(jnp.int32, (PERM, PERM), 0)
    ci = lax.broadcasted_iota(jnp.int32, (PERM, PERM), 1)
    perm = (((ri >> 4) == (ci & 15)) & ((ri & 15) == (ci >> 4))).astype(BF16)

    qa = lax.broadcasted_iota(jnp.int32, (WIN, 2 * WIN), 0)
    kb_i = lax.broadcasted_iota(jnp.int32, (WIN, 2 * WIN), 1)
    cur = kb_i >= WIN
    kin = kb_i & (WIN - 1)
    dist_nat = qa - kin + jnp.where(cur, 0, WIN)
    band_nat = (dist_nat >= 0) & (dist_nat <= WIN)
    sq = ((qa & 31) << 2) + (qa >> 5)
    sk = ((kin & 31) << 2) + (kin >> 5)
    dist_4 = sq - sk + jnp.where(cur, 0, WIN)
    band_4 = (dist_4 >= 0) & (dist_4 <= WIN)

    def attend(qb, kb, vb, mask, state):
        qf = qb.astype(F32)
        new = []
        for hl in range(2):
            m, l, a = state[3 * hl:3 * hl + 3]
            own = own0 if hl == 0 else jnp.logical_not(own0)
            qh = jnp.where(own, qf, 0.0).astype(BF16)
            s = lax.dot_general(qh, kb, (((1,), (1,)), ((), ())), preferred_element_type=F32)
            s = jnp.where(mask, s, NEG_INF)
            mn = jnp.maximum(m, jnp.max(s, axis=1, keepdims=True))
            alpha = jnp.exp(m - mn)
            p = jnp.exp(s - mn)
            l = alpha * l + jnp.sum(p, axis=1, keepdims=True)
            a = alpha * a + jnp.dot(p.astype(BF16), vb, preferred_element_type=F32)
            new += [mn, l, a]
        return new

    def pack_state(st):
        m0, l0, a0, m1, l1, a1 = st
        return (jnp.where(own0, m0, m1), jnp.where(own0, l0, l1), jnp.where(own0, a0, a1))

    def deint(blk, carry):
        r0 = pl.multiple_of(blk * PERM, PERM)
        j0 = pl.multiple_of(blk * MAX_DIL, MAX_DIL)
        for src, dst in ((q_ref, q16), (k_ref, k16), (v_ref, v16)):
            y = jnp.dot(perm, src[0, pl.ds(r0, PERM), :], preferred_element_type=F32).astype(BF16)
            for r in range(MAX_DIL):
                dst[r, pl.ds(j0, MAX_DIL), :] = y[r * MAX_DIL:(r + 1) * MAX_DIL, :]
        return carry

    lax.fori_loop(0, seq // PERM, deint, 0)

    fresh = [jnp.full((WIN, 1), NEG_INF, F32), jnp.zeros((WIN, 1), F32), jnp.zeros((WIN, LANES), F32)] * 2
    nb16 = sub // WIN

    def d16_body(t, carry):
        r = t // nb16
        n = t % nb16
        c0 = pl.multiple_of(n * WIN, WIN)
        p0 = pl.multiple_of(jnp.maximum(n - 1, 0) * WIN, WIN)
        qb = q16[r, pl.ds(c0, WIN), :]
        kb = jnp.concatenate([k16[r, pl.ds(p0, WIN), :], k16[r, pl.ds(c0, WIN), :]], axis=0)
        vb = jnp.concatenate([v16[r, pl.ds(p0, WIN), :], v16[r, pl.ds(c0, WIN), :]], axis=0)
        mask = band_nat & (cur | (n > 0))
        mm, ll, aa = pack_state(attend(qb, kb, vb, mask, fresh))
        row = pl.multiple_of(r * sub + c0, WIN)
        m_s[pl.ds(row, WIN), :] = mm
        l_s[pl.ds(row, WIN), :] = ll
        acc_s[pl.ds(row, WIN), :] = aa
        return carry

    lax.fori_loop(0, MAX_DIL * nb16, d16_body, 0)

    nb4 = (seq // 4) // WIN
    ch = WIN // 4

    def d4_body(t, carry):
        r4 = t // nb4
        n = t % nb4
        c0 = pl.multiple_of(n * ch, ch)
        p0 = pl.multiple_of(jnp.maximum(n - 1, 0) * ch, ch)

        def gather(ref, j0):
            return [ref[r4 + 4 * q, pl.ds(j0, ch), :] for q in range(4)]

        qb = jnp.concatenate(gather(q16, c0), axis=0)
        kb = jnp.concatenate(gather(k16, p0) + gather(k16, c0), axis=0)
        vb = jnp.concatenate(gather(v16, p0) + gather(v16, c0), axis=0)
        rows = [pl.multiple_of((r4 + 4 * q) * sub + c0, ch) for q in range(4)]
        mm = jnp.concatenate([m_s[pl.ds(rw, ch), :] for rw in rows], axis=0)
        ll = jnp.concatenate([l_s[pl.ds(rw, ch), :] for rw in rows], axis=0)
        aa = jnp.concatenate([acc_s[pl.ds(rw, ch), :] for rw in rows], axis=0)
        state = [mm[:, 0:1], ll[:, 0:1], aa, mm[:, HEAD_DIM:HEAD_DIM + 1], ll[:, HEAD_DIM:HEAD_DIM + 1], aa]
        mask = band_4 & (cur | (n > 0))
        mm, ll, aa = pack_state(attend(qb, kb, vb, mask, state))
        for q, rw in enumerate(rows):
            m_s[pl.ds(rw, ch), :] = mm[q * ch:(q + 1) * ch, :]
            l_s[pl.ds(rw, ch), :] = ll[q * ch:(q + 1) * ch, :]
            acc_s[pl.ds(rw, ch), :] = aa[q * ch:(q + 1) * ch, :]
        return carry

    lax.fori_loop(0, 4 * nb4, d4_body, 0)

    def renat(blk, carry):
        j0 = pl.multiple_of(blk * MAX_DIL, MAX_DIL)
        r0 = pl.multiple_of(blk * PERM, PERM)
        rows = [pl.multiple_of(r * sub + j0, MAX_DIL) for r in range(MAX_DIL)]
        mm = jnp.concatenate([m_s[pl.ds(rw, MAX_DIL), :] for rw in rows], axis=0)
        ll = jnp.concatenate([l_s[pl.ds(rw, MAX_DIL), :] for rw in rows], axis=0)
        aa = jnp.concatenate([acc_s[pl.ds(rw, MAX_DIL), :] for rw in rows], axis=0)
        o = (aa / ll).astype(BF16)
        hi, mid, lo = _split3(mm + jnp.log(ll))
        on_s[pl.ds(r0, PERM), :] = jnp.dot(perm, o, preferred_element_type=F32)
        lse_s[pl.ds(r0, PERM), :] = (jnp.dot(perm, hi, preferred_element_type=F32)
                                     + jnp.dot(perm, mid, preferred_element_type=F32)
                                     + jnp.dot(perm, lo, preferred_element_type=F32))
        return carry

    lax.fori_loop(0, seq // PERM, renat, 0)

    def d1_body(n, carry):
        c0 = pl.multiple_of(n * WIN, WIN)
        p0 = pl.multiple_of(jnp.maximum(n - 1, 0) * WIN, WIN)
        qb = q_ref[0, pl.ds(c0, WIN), :]
        kb = jnp.concatenate([k_ref[0, pl.ds(p0, WIN), :], k_ref[0, pl.ds(c0, WIN), :]], axis=0)
        vb = jnp.concatenate([v_ref[0, pl.ds(p0, WIN), :], v_ref[0, pl.ds(c0, WIN), :]], axis=0)
        lse = lse_s[pl.ds(c0, WIN), :]
        o = on_s[pl.ds(c0, WIN), :]
        one = jnp.ones((WIN, 1), F32)
        state = [lse[:, 0:1], one, o, lse[:, HEAD_DIM:HEAD_DIM + 1], one, o]
        mask = band_nat & (cur | (n > 0))
        m0, l0, a0, m1, l1, a1 = attend(qb, kb, vb, mask, state)
        o_ref[0, pl.ds(c0, WIN), :] = jnp.where(own0, a0 / l0, a1 / l1).astype(o_ref.dtype)
        return carry

    lax.fori_loop(0, seq // WIN, d1_body, 0)


def _dilated(qkva):
    b, seq, _ = qkva.shape
    sub = seq // MAX_DIL
    kern = functools.partial(_dilated_kernel, seq=seq)
    return pl.pallas_call(
        kern,
        grid=(b, HEAD_PAIRS),
        in_specs=[
            pl.BlockSpec((1, seq, LANES), lambda bi, hp: (bi, 0, hp)),
            pl.BlockSpec((1, seq, LANES), lambda bi, hp: (bi, 0, HEAD_PAIRS + hp)),
            pl.BlockSpec((1, seq, LANES), lambda bi, hp: (bi, 0, 2 * HEAD_PAIRS + hp)),
        ],
        out_specs=pl.BlockSpec((1, seq, LANES), lambda bi, hp: (bi, 0, hp)),
        out_shape=jax.ShapeDtypeStruct((b, seq, WIDTH_A), BF16),
        scratch_shapes=[
            pltpu.VMEM((MAX_DIL, sub, LANES), BF16),
            pltpu.VMEM((MAX_DIL, sub, LANES), BF16),
            pltpu.VMEM((MAX_DIL, sub, LANES), BF16),
            pltpu.VMEM((seq, LANES), F32),
            pltpu.VMEM((seq, LANES), F32),
            pltpu.VMEM((seq, LANES), F32),
            pltpu.VMEM((seq, LANES), F32),
            pltpu.VMEM((seq, LANES), F32),
        ],
        compiler_params=pltpu.CompilerParams(
            dimension_semantics=("arbitrary", "arbitrary"), vmem_limit_bytes=VMEM_LIMIT),
        name="dilated",
    )(qkva, qkva, qkva)


def _outproj_kernel(oa_ref, ob_ref, x_ref, wo_ref, g_ref, wr_ref, x1_ref, h2_ref, comb_ref):
    tm = x_ref.shape[0]
    x1 = (x_ref[...]
          + jnp.dot(oa_ref[...], wo_ref[0:WIDTH_A, :], preferred_element_type=F32)
          + jnp.dot(ob_ref[...], wo_ref[WIDTH_A:WIDTH_A + WIDTH_B, :], preferred_element_type=F32))
    x1_ref[...] = x1
    ms = jnp.mean(x1 * x1, axis=-1, keepdims=True)
    h2 = x1 * lax.rsqrt(ms + NORM_EPS) * g_ref[...]
    h2_ref[...] = h2.astype(BF16)
    logits = jnp.dot(h2, wr_ref[...], preferred_element_type=F32, precision=lax.Precision.HIGHEST)
    lane = lax.broadcasted_iota(jnp.int32, (tm, LANES), 1)
    lane_f = lane.astype(F32)
    big = float(LANES)
    gmask = lane < N_GROUPS
    gl = jnp.where(gmask, logits, NEG_INF)
    gmax = jnp.max(gl, axis=1, keepdims=True)
    gsum = jnp.sum(jnp.where(gmask, jnp.exp(gl - gmax), 0.0), axis=1, keepdims=True)
    p_top = 1.0 / gsum
    g_star = jnp.min(jnp.where(gmask & (gl == gmax), lane_f, big), axis=1, keepdims=True)
    lo_lane = ROUTER_OFF + EXPERTS_PER_GROUP * g_star
    emask = (lane_f >= lo_lane) & (lane_f < lo_lane + EXPERTS_PER_GROUP)
    sel = jnp.where(emask, logits, NEG_INF)
    v1 = jnp.max(sel, axis=1, keepdims=True)
    i1 = jnp.min(jnp.where(emask & (sel == v1), lane_f, big), axis=1, keepdims=True)
    rest = emask & (lane_f != i1)
    sel2 = jnp.where(rest, logits, NEG_INF)
    v2 = jnp.max(sel2, axis=1, keepdims=True)
    i2 = jnp.min(jnp.where(rest & (sel2 == v2), lane_f, big), axis=1, keepdims=True)
    e2 = jnp.exp(v2 - v1)
    w1 = p_top / (1.0 + e2)
    w2 = p_top * e2 / (1.0 + e2)
    comb_ref[...] = jnp.where(lane_f == i1, w1, jnp.where(lane_f == i2, w2, 0.0))


def _outproj(oa, ob, x2, wo, g, wr, *, tm):
    n = x2.shape[0]
    return pl.pallas_call(
        _outproj_kernel,
        grid=(n // tm,),
        in_specs=[
            pl.BlockSpec((tm, WIDTH_A), lambda i: (i, 0)),
            pl.BlockSpec((tm, WIDTH_B), lambda i: (i, 0)),
            pl.BlockSpec((tm, D_MODEL), lambda i: (i, 0)),
            pl.BlockSpec(wo.shape, lambda i: (0, 0)),
            pl.BlockSpec((1, D_MODEL), lambda i: (0, 0)),
            pl.BlockSpec(wr.shape, lambda i: (0, 0)),
        ],
        out_specs=[
            pl.BlockSpec((tm, D_MODEL), lambda i: (i, 0)),
            pl.BlockSpec((tm, D_MODEL), lambda i: (i, 0)),
            pl.BlockSpec((tm, LANES), lambda i: (i, 0)),
        ],
        out_shape=[
            jax.ShapeDtypeStruct((n, D_MODEL), F32),
            jax.ShapeDtypeStruct((n, D_MODEL), BF16),
            jax.ShapeDtypeStruct((n, LANES), F32),
        ],
        compiler_params=pltpu.CompilerParams(dimension_semantics=("arbitrary",), vmem_limit_bytes=VMEM_LIMIT),
        name="outproj",
    )(oa, ob, x2, wo, g, wr)


def _moe_kernel(h2_ref, comb_ref, x1_ref, wg_ref, wu_ref, wd_ref, g_ref, out_ref, acc_scr):
    e = pl.program_id(1)
    tm = h2_ref.shape[0]

    @pl.when(e == 0)
    def _():
        acc_scr[...] = jnp.zeros_like(acc_scr)

    h2 = h2_ref[...]
    gate = jnp.dot(h2, wg_ref[0], preferred_element_type=F32)
    up = jnp.dot(h2, wu_ref[0], preferred_element_type=F32)
    he = (gate / (1.0 + jnp.exp(-gate)) * up).astype(BF16)
    y = jnp.dot(he, wd_ref[0], preferred_element_type=F32)
    lane = lax.broadcasted_iota(jnp.int32, (tm, LANES), 1)
    ce = jnp.sum(jnp.where(lane == ROUTER_OFF + e, comb_ref[...], 0.0), axis=1, keepdims=True)
    acc_scr[...] += ce * y

    @pl.when(e == N_EXPERTS - 1)
    def _():
        x2 = x1_ref[...] + acc_scr[...]
        ms = jnp.mean(x2 * x2, axis=-1, keepdims=True)
        out_ref[...] = x2 * lax.rsqrt(ms + NORM_EPS) * g_ref[...]


def _moe(h2, comb, x1, wg, wu, wd, g, *, tm):
    n = h2.shape[0]
    return pl.pallas_call(
        _moe_kernel,
        grid=(n // tm, N_EXPERTS),
        in_specs=[
            pl.BlockSpec((tm, D_MODEL), lambda i, e: (i, 0)),
            pl.BlockSpec((tm, LANES), lambda i, e: (i, 0)),
            pl.BlockSpec((tm, D_MODEL), lambda i, e: (i, 0)),
            pl.BlockSpec((1, D_MODEL, D_EXPERT), lambda i, e: (e, 0, 0)),
            pl.BlockSpec((1, D_MODEL, D_EXPERT), lambda i, e: (e, 0, 0)),
            pl.BlockSpec((1, D_EXPERT, D_MODEL), lambda i, e: (e, 0, 0)),
            pl.BlockSpec((1, D_MODEL), lambda i, e: (0, 0)),
        ],
        out_specs=pl.BlockSpec((tm, D_MODEL), lambda i, e: (i, 0)),
        out_shape=jax.ShapeDtypeStruct((n, D_MODEL), F32),
        scratch_shapes=[pltpu.VMEM((tm, D_MODEL), F32)],
        compiler_params=pltpu.CompilerParams(
            dimension_semantics=("arbitrary", "arbitrary"), vmem_limit_bytes=VMEM_LIMIT),
        name="moe",
    )(h2, comb, x1, wg, wu, wd, g)


def _rotary_tables(seq):
    pos = jnp.arange(seq, dtype=F32)
    inv_freq = 1.0 / (ROPE_THETA ** (jnp.arange(0, ROT_DIM, 2, dtype=F32) / ROT_DIM))
    ang = pos[:, None] * inv_freq[None, :]
    cos, sin = jnp.cos(ang), jnp.sin(ang)
    zeros = jnp.zeros((seq, HEAD_DIM - ROT_DIM), F32)
    zh = jnp.zeros((seq, ROT_HALF), F32)
    cos_h = jnp.concatenate([cos, cos, jnp.ones((seq, HEAD_DIM - ROT_DIM), F32)], axis=1)
    sa_h = jnp.concatenate([-sin, zh, zeros], axis=1)
    sb_h = jnp.concatenate([zh, sin, zeros], axis=1)
    tile = lambda t: jnp.concatenate([t, t], axis=1)
    return tile(cos_h), tile(sa_h), tile(sb_h)


def kernel(x, attn_norm, w_in, b_forget, w_out, ffn_norm, w_group, w_expert, w_gate_e, w_up_e, w_down_e, final_norm):
    b, seq, d = x.shape
    assert d == D_MODEL and w_in.shape[0] == 1, "single-layer block"
    n = b * seq
    scale = HEAD_DIM ** -0.5
    col_scale = jnp.ones((2 * QKV_WIDTH + HEADS_B,), F32)
    col_scale = col_scale.at[0:WIDTH_A].set(scale).at[QKV_WIDTH:QKV_WIDTH + WIDTH_B].set(scale * LOG2E)
    w = jnp.pad(w_in[0] * col_scale[None, :], ((0, 0), (0, LANES - HEADS_B))).astype(BF16)
    bf = jnp.pad(b_forget[0].astype(F32), (0, LANES - HEADS_B))[None, :]
    cosv, sa, sb = _rotary_tables(seq)
    x2 = x.reshape(n, d)

    qkva, qkvb, c = _inproj(x2, attn_norm[0][None, :], w, cosv, sa, sb, bf, seq=seq, tm=512)
    out_a = _dilated(qkva.reshape(b, seq, QKV_WIDTH))
    out_b = _fox(qkvb.reshape(b, seq, QKV_WIDTH), c.reshape(b, seq, LANES), tq=512)

    wr = jnp.pad(jnp.concatenate([w_group[0], w_expert[0]], axis=1).astype(F32),
                 ((0, 0), (0, LANES - N_GROUPS - N_EXPERTS)))
    x1, h2, comb = _outproj(out_a.reshape(n, WIDTH_A), out_b.reshape(n, WIDTH_B), x2,
                            w_out[0].astype(BF16), ffn_norm[0][None, :], wr, tm=512)
    out = _moe(h2, comb, x1, w_gate_e[0].astype(BF16), w_up_e[0].astype(BF16), w_down_e[0].astype(BF16),
               final_norm[None, :], tm=1024)
    return out.reshape(b, seq, d)
```

```python
import functools

import jax
import jax.numpy as jnp
from jax import lax
from jax.experimental import pallas as pl
from jax.experimental.pallas import tpu as pltpu

F32 = jnp.float32
BF16 = jnp.bfloat16

D_MODEL = 1024
HEAD_DIM = 64
HEADS_A = 8
HEADS_B = 8
WIDTH_A = HEADS_A * HEAD_DIM
WIDTH_B = HEADS_B * HEAD_DIM
QKV_WIDTH = 3 * WIDTH_A
DILATIONS = ((128, 1), (512, 4), (2048, 16))
ROT_DIM = HEAD_DIM // 4
ROT_HALF = ROT_DIM // 2
ROPE_THETA = 500000.0
N_GROUPS = 4
EXPERTS_PER_GROUP = 4
N_EXPERTS = N_GROUPS * EXPERTS_PER_GROUP
D_EXPERT = 512
NORM_EPS = 1e-6
NEG_INF = -1e30

LANES = 128
HEAD_PAIRS = WIDTH_A // LANES
WIN = 128
MAX_DIL = 16
PERM = MAX_DIL * MAX_DIL
KV_BLOCK = 256
ROUTER_OFF = N_GROUPS
VMEM_LIMIT = 56 * 1024 * 1024
LOG2E = 1.4426950408889634
BOUND_SLACK = 1.0 + 2.0 ** -7
BOUND_EPS = 2.0 ** -7
GAP_MAX = 64.0


def _split3(x):
    hi = x.astype(BF16)
    r1 = x - hi.astype(F32)
    mid = r1.astype(BF16)
    lo = (r1 - mid.astype(F32)).astype(BF16)
    return hi, mid, lo


def _inproj_kernel(x_ref, g_ref, w_ref, cos_ref, sa_ref, sb_ref, bf_ref,
                   qkva_ref, qkvb_ref, c_ref, h_scr, carry_scr, *, tiles_per_seq):
    i = pl.program_id(0)
    tm = x_ref.shape[0]
    x = x_ref[...]
    ms = jnp.mean(x * x, axis=-1, keepdims=True)
    h_scr[...] = (x * lax.rsqrt(ms + NORM_EPS) * g_ref[...]).astype(BF16)
    cosv = cos_ref[...]
    sa = sa_ref[...]
    sb = sb_ref[...]
    for g in range(6):
        p = jnp.dot(h_scr[...], w_ref[:, g * WIDTH_A:(g + 1) * WIDTH_A], preferred_element_type=F32)
        dst = qkva_ref if g < 3 else qkvb_ref
        col0 = (g % 3) * WIDTH_A
        if g < 2:
            for k in range(HEAD_PAIRS):
                t = p[:, k * LANES:(k + 1) * LANES]
                t = (t * cosv + pltpu.roll(t, LANES - ROT_HALF, 1) * sa + pltpu.roll(t, ROT_HALF, 1) * sb)
                dst[:, col0 + k * LANES:col0 + (k + 1) * LANES] = t.astype(BF16)
        else:
            dst[:, col0:col0 + WIDTH_A] = p.astype(BF16)
    z = jnp.dot(h_scr[...], w_ref[:, 2 * QKV_WIDTH:2 * QKV_WIDTH + LANES], preferred_element_type=F32) + bf_ref[...]
    lf = jnp.minimum(z, 0.0) - jnp.log1p(jnp.exp(-jnp.abs(z)))
    lane = lax.broadcasted_iota(jnp.int32, (tm, LANES), 1)
    lf = jnp.where(lane < HEADS_B, lf * LOG2E, 0.0)
    hi, mid, lo = _split3(lf)
    tri = (lax.broadcasted_iota(jnp.int32, (tm, tm), 0) >= lax.broadcasted_iota(jnp.int32, (tm, tm), 1)).astype(BF16)
    cs = (jnp.dot(tri, hi, preferred_element_type=F32) + jnp.dot(tri, mid, preferred_element_type=F32)
          + jnp.dot(tri, lo, preferred_element_type=F32))

    @pl.when(i % tiles_per_seq == 0)
    def _():
        carry_scr[...] = jnp.zeros_like(carry_scr)

    c = cs + carry_scr[0:1, :]
    c_ref[...] = c
    carry_scr[...] = jnp.broadcast_to(c[tm - 1:tm, :], carry_scr.shape)


def _inproj(x2, g, w, cosv, sa, sb, bf, *, seq, tm):
    n = x2.shape[0]
    kern = functools.partial(_inproj_kernel, tiles_per_seq=seq // tm)
    tps = seq // tm
    return pl.pallas_call(
        kern,
        grid=(n // tm,),
        in_specs=[
            pl.BlockSpec((tm, D_MODEL), lambda i: (i, 0)),
            pl.BlockSpec((1, D_MODEL), lambda i: (0, 0)),
            pl.BlockSpec(w.shape, lambda i: (0, 0)),
            pl.BlockSpec((tm, LANES), lambda i: (i % tps, 0)),
            pl.BlockSpec((tm, LANES), lambda i: (i % tps, 0)),
            pl.BlockSpec((tm, LANES), lambda i: (i % tps, 0)),
            pl.BlockSpec((1, LANES), lambda i: (0, 0)),
        ],
        out_specs=[
            pl.BlockSpec((tm, QKV_WIDTH), lambda i: (i, 0)),
            pl.BlockSpec((tm, QKV_WIDTH), lambda i: (i, 0)),
            pl.BlockSpec((tm, LANES), lambda i: (i, 0)),
        ],
        out_shape=[
            jax.ShapeDtypeStruct((n, QKV_WIDTH), BF16),
            jax.ShapeDtypeStruct((n, QKV_WIDTH), BF16),
            jax.ShapeDtypeStruct((n, LANES), F32),
        ],
        scratch_shapes=[pltpu.VMEM((tm, D_MODEL), BF16), pltpu.VMEM((8, LANES), F32)],
        compiler_params=pltpu.CompilerParams(dimension_semantics=("arbitrary",), vmem_limit_bytes=VMEM_LIMIT),
        name="inproj",
    )(x2, g, w, cosv, sa, sb, bf)


def _head_column(c_all, lane, head):
    return jnp.sum(jnp.where(lane == head, c_all, 0.0), axis=1, keepdims=True)


def _own_lanes(lane, hl):
    return (lane < HEAD_DIM) if hl == 0 else (lane >= HEAD_DIM)


def _own_sum(vals, lane, hl):
    return jnp.sum(jnp.where(_own_lanes(lane, hl), vals, 0.0), axis=1, keepdims=True)


def _norm_bound(qsq, ksq_max):
    return jnp.sqrt(qsq * ksq_max) * BOUND_SLACK + BOUND_EPS


def _fox_kernel(q_ref, k_ref, v_ref, cq_ref, ck_ref, o_ref,
                k0_scr, k1_scr, v0_scr, v1_scr, ksq_scr, acc0_scr, acc1_scr, s_scr, *, tq, seq, chunk):
    hp = pl.program_id(1)
    qi = pl.program_id(2)
    k_scr = (k0_scr, k1_scr)
    v_scr = (v0_scr, v1_scr)
    acc_scr = (acc0_scr, acc1_scr)

    def augmented(vals, cterm, lane, hl, key_side):
        hi, mid, lo = (t.astype(F32) for t in _split3(cterm))
        base = HEAD_DIM if hl == 0 else 0
        ones = (lane >= base + (0 if key_side else 3)) & (lane < base + (3 if key_side else 6))
        o = 3 if key_side else 0
        sgn = -1.0 if key_side else 1.0
        aug = jnp.where(lane == base + o, sgn * hi,
                        jnp.where(lane == base + o + 1, sgn * mid,
                                  jnp.where(lane == base + o + 2, sgn * lo,
                                            jnp.where(ones, 1.0, 0.0))))
        return jnp.where(_own_lanes(lane, hl), vals, aug).astype(BF16)

    @pl.when(qi == 0)
    def _build_keys():
        lane = lax.broadcasted_iota(jnp.int32, (chunk, LANES), 1)

        def body(ci, ksq):
            r0 = pl.multiple_of(ci * chunk, chunk)
            kc = k_ref[0, pl.ds(r0, chunk), :].astype(F32)
            vc = v_ref[0, pl.ds(r0, chunk), :].astype(F32)
            cc = ck_ref[0, pl.ds(r0, chunk), :]
            new = []
            for hl in range(2):
                k_scr[hl][pl.ds(r0, chunk), :] = augmented(kc, _head_column(cc, lane, 2 * hp + hl), lane, hl, True)
                one_lane = HEAD_DIM if hl == 0 else 0
                v_scr[hl][pl.ds(r0, chunk), :] = jnp.where(
                    _own_lanes(lane, hl), vc, jnp.where(lane == one_lane, 1.0, 0.0)).astype(BF16)
                new.append(jnp.maximum(ksq[hl], jnp.max(_own_sum(kc * kc, lane, hl), axis=0, keepdims=True)))
            return tuple(new)

        zero = jnp.zeros((1, 1), F32)
        ksq0, ksq1 = lax.fori_loop(0, seq // chunk, body, (zero, zero))
        lane8 = lax.broadcasted_iota(jnp.int32, ksq_scr.shape, 1)
        ksq_scr[...] = jnp.where(lane8 < HEAD_DIM, ksq0, ksq1)

    lane_q = lax.broadcasted_iota(jnp.int32, (tq, LANES), 1)
    q2 = q_ref[0].astype(F32)
    cq = cq_ref[0]
    kd = k_ref[0, pl.ds(pl.multiple_of(qi * tq, tq), tq), :].astype(F32)
    ksq_row = ksq_scr[0:1, :]
    qh = []
    gap = None
    for hl in range(2):
        bound = _norm_bound(_own_sum(q2 * q2, lane_q, hl), ksq_row[:, hl * HEAD_DIM:hl * HEAD_DIM + 1])
        diag = _own_sum(q2 * kd, lane_q, hl)
        g = jnp.max(bound - diag)
        gap = g if gap is None else jnp.maximum(gap, g)
        qh.append(augmented(q2, _head_column(cq, lane_q, 2 * hp + hl) - bound, lane_q, hl, False))
    n_diag = tq // KV_BLOCK
    n_full = qi * n_diag
    row_minus_col = (lax.broadcasted_iota(jnp.int32, (tq, KV_BLOCK), 0)
                     - lax.broadcasted_iota(jnp.int32, (tq, KV_BLOCK), 1))

    def causal_mask(s, d):
        return jnp.where(row_minus_col >= d * KV_BLOCK, s, NEG_INF)

    def logits(hl, j):
        r0 = pl.multiple_of(j * KV_BLOCK, KV_BLOCK)
        return lax.dot_general(qh[hl], k_scr[hl][pl.ds(r0, KV_BLOCK), :], (((1,), (1,)), ((), ())),
                               preferred_element_type=F32)

    def values(hl, j):
        return v_scr[hl][pl.ds(pl.multiple_of(j * KV_BLOCK, KV_BLOCK), KV_BLOCK), :]

    def finish(a0, a1):
        l0 = a0[:, HEAD_DIM:HEAD_DIM + 1]
        l1 = a1[:, 0:1]
        o_ref[0] = jnp.where(lane_q < HEAD_DIM, a0 / l0, a1 / l1).astype(o_ref.dtype)

    @pl.when(gap <= GAP_MAX)
    def _fast():
        acc0_scr[...] = jnp.zeros_like(acc0_scr)
        acc1_scr[...] = jnp.zeros_like(acc1_scr)

        def produce(j, slot):
            for hl in range(2):
                s_scr[slot, hl] = logits(hl, j)

        def consume(j, slot, diag):
            for hl in range(2):
                s = s_scr[slot, hl]
                if diag is not None:
                    s = causal_mask(s, diag)
                acc_scr[hl][...] += jnp.dot(jnp.exp2(s).astype(BF16), values(hl, j), preferred_element_type=F32)

        produce(0, 0)

        def body(j, carry):
            slot = j & 1
            consume(j, slot, None)
            produce(j + 1, 1 - slot)
            return carry

        lax.fori_loop(0, n_full, body, 0)
        for d in range(n_diag):
            j = n_full + d
            consume(j, j & 1, d)
            if d + 1 < n_diag:
                produce(j + 1, (j + 1) & 1)
        finish(acc0_scr[...], acc1_scr[...])

    @pl.when(gap > GAP_MAX)
    def _general():
        def step(j, carry, diag):
            new = []
            for hl in range(2):
                m, a = carry[2 * hl:2 * hl + 2]
                s = logits(hl, j)
                if diag is not None:
                    s = causal_mask(s, diag)
                mn = jnp.maximum(m, jnp.max(s, axis=1, keepdims=True))
                a = jnp.exp2(m - mn) * a + jnp.dot(jnp.exp2(s - mn).astype(BF16), values(hl, j),
                                                  preferred_element_type=F32)
                new += [mn, a]
            return tuple(new)

        carry = (jnp.full((tq, 1), NEG_INF, F32), jnp.zeros((tq, LANES), F32)) * 2
        carry = lax.fori_loop(0, n_full, lambda j, c: step(j, c, None), carry)
        for d in range(n_diag):
            carry = step(n_full + d, carry, d)
        finish(carry[1], carry[3])


def _fox(qkvb, c, *, tq, chunk=512):
    b, seq, _ = qkvb.shape
    kern = functools.partial(_fox_kernel, tq=tq, seq=seq, chunk=chunk)
    return pl.pallas_call(
        kern,
        grid=(b, HEAD_PAIRS, seq // tq),
        in_specs=[
            pl.BlockSpec((1, tq, LANES), lambda bi, hp, qi: (bi, qi, hp)),
            pl.BlockSpec((1, seq, LANES), lambda bi, hp, qi: (bi, 0, HEAD_PAIRS + hp)),
            pl.BlockSpec((1, seq, LANES), lambda bi, hp, qi: (bi, 0, 2 * HEAD_PAIRS + hp)),
            pl.BlockSpec((1, tq, LANES), lambda bi, hp, qi: (bi, qi, 0)),
            pl.BlockSpec((1, seq, LANES), lambda bi, hp, qi: (bi, 0, 0)),
        ],
        out_specs=pl.BlockSpec((1, tq, LANES), lambda bi, hp, qi: (bi, qi, hp)),
        out_shape=jax.ShapeDtypeStruct((b, seq, WIDTH_B), BF16),
        scratch_shapes=[pltpu.VMEM((seq, LANES), BF16), pltpu.VMEM((seq, LANES), BF16),
                        pltpu.VMEM((seq, LANES), BF16), pltpu.VMEM((seq, LANES), BF16),
                        pltpu.VMEM((8, LANES), F32),
                        pltpu.VMEM((tq, LANES), F32), pltpu.VMEM((tq, LANES), F32),
                        pltpu.VMEM((2, 2, tq, KV_BLOCK), F32)],
        compiler_params=pltpu.CompilerParams(
            dimension_semantics=("arbitrary", "arbitrary", "arbitrary"), vmem_limit_bytes=VMEM_LIMIT),
        name="fox",
    )(qkvb, qkvb, qkvb, c, c)


def _dilated_general(q_ref, k_ref, v_ref, o_ref, q16, k16, v16, m_s, l_s, acc_s, on_s, lse_s, *, seq):
    sub = seq // MAX_DIL
    own0 = lax.broadcasted_iota(jnp.int32, (WIN, LANES), 1) < HEAD_DIM
    ri = lax.broadcasted_iota(jnp.int32, (PERM, PERM), 0)
    ci = lax.broadcasted_iota(jnp.int32, (PERM, PERM), 1)
    perm = (((ri >> 4) == (ci & 15)) & ((ri & 15) == (ci >> 4))).astype(BF16)

    qa = lax.broadcasted_iota(jnp.int32, (WIN, 2 * WIN), 0)
    kb_i = lax.broadcasted_iota(jnp.int32, (WIN, 2 * WIN), 1)
    cur = kb_i >= WIN
    kin = kb_i & (WIN - 1)
    dist_nat = qa - kin + jnp.where(cur, 0, WIN)
    band_nat = (dist_nat >= 0) & (dist_nat <= WIN)
    sq = ((qa & 31) << 2) + (qa >> 5)
    sk = ((kin & 31) << 2) + (kin >> 5)
    dist_4 = sq - sk + jnp.where(cur, 0, WIN)
    band_4 = (dist_4 >= 0) & (dist_4 <= WIN)

    def attend(qb, kb, vb, mask, state):
        qf = qb.astype(F32)
        new = []
        for hl in range(2):
            m, l, a = state[3 * hl:3 * hl + 3]
            own = own0 if hl == 0 else jnp.logical_not(own0)
            qh = jnp.where(own, qf, 0.0).astype(BF16)
            s = lax.dot_general(qh, kb, (((1,), (1,)), ((), ())), preferred_element_type=F32)
            s = jnp.where(mask, s, NEG_INF)
            mn = jnp.maximum(m, jnp.max(s, axis=1, keepdims=True))
            alpha = jnp.exp2(m - mn)
            p = jnp.exp2(s - mn)
            l = alpha * l + jnp.sum(p, axis=1, keepdims=True)
            a = alpha * a + jnp.dot(p.astype(BF16), vb, preferred_element_type=F32)
            new += [mn, l, a]
        return new

    def pack_state(st):
        m0, l0, a0, m1, l1, a1 = st
        return (jnp.where(own0, m0, m1), jnp.where(own0, l0, l1), jnp.where(own0, a0, a1))

    def deint(blk, carry):
        r0 = pl.multiple_of(blk * PERM, PERM)
        j0 = pl.multiple_of(blk * MAX_DIL, MAX_DIL)
        for src, dst in ((q_ref, q16), (k_ref, k16), (v_ref, v16)):
            y = jnp.dot(perm, src[0, pl.ds(r0, PERM), :], preferred_element_type=F32).astype(BF16)
            for r in range(MAX_DIL):
                dst[r, pl.ds(j0, MAX_DIL), :] = y[r * MAX_DIL:(r + 1) * MAX_DIL, :]
        return carry

    lax.fori_loop(0, seq // PERM, deint, 0)

    fresh = [jnp.full((WIN, 1), NEG_INF, F32), jnp.zeros((WIN, 1), F32), jnp.zeros((WIN, LANES), F32)] * 2
    nb16 = sub // WIN

    def d16_body(t, carry):
        r = t // nb16
        n = t % nb16
        c0 = pl.multiple_of(n * WIN, WIN)
        p0 = pl.multiple_of(jnp.maximum(n - 1, 0) * WIN, WIN)
        qb = q16[r, pl.ds(c0, WIN), :]
        kb = jnp.concatenate([k16[r, pl.ds(p0, WIN), :], k16[r, pl.ds(c0, WIN), :]], axis=0)
        vb = jnp.concatenate([v16[r, pl.ds(p0, WIN), :], v16[r, pl.ds(c0, WIN), :]], axis=0)
        mask = band_nat & (cur | (n > 0))
        mm, ll, aa = pack_state(attend(qb, kb, vb, mask, fresh))
        row = pl.multiple_of(r * sub + c0, WIN)
        m_s[pl.ds(row, WIN), :] = mm
        l_s[pl.ds(row, WIN), :] = ll
        acc_s[pl.ds(row, WIN), :] = aa
        return carry

    lax.fori_loop(0, MAX_DIL * nb16, d16_body, 0)

    nb4 = (seq // 4) // WIN
    ch = WIN // 4

    def d4_body(t, carry):
        r4 = t // nb4
        n = t % nb4
        c0 = pl.multiple_of(n * ch, ch)
        p0 = pl.multiple_of(jnp.maximum(n - 1, 0) * ch, ch)

        def gather(ref, j0):
            return [ref[r4 + 4 * q, pl.ds(j0, ch), :] for q in range(4)]

        qb = jnp.concatenate(gather(q16, c0), axis=0)
        kb = jnp.concatenate(gather(k16, p0) + gather(k16, c0), axis=0)
        vb = jnp.concatenate(gather(v16, p0) + gather(v16, c0), axis=0)
        rows = [pl.multiple_of((r4 + 4 * q) * sub + c0, ch) for q in range(4)]
        mm = jnp.concatenate([m_s[pl.ds(rw, ch), :] for rw in rows], axis=0)
        ll = jnp.concatenate([l_s[pl.ds(rw, ch), :] for rw in rows], axis=0)
        aa = jnp.concatenate([acc_s[pl.ds(rw, ch), :] for rw in rows], axis=0)
        state = [mm[:, 0:1], ll[:, 0:1], aa, mm[:, HEAD_DIM:HEAD_DIM + 1], ll[:, HEAD_DIM:HEAD_DIM + 1], aa]
        mask = band_4 & (cur | (n > 0))
        mm, ll, aa = pack_state(attend(qb, kb, vb, mask, state))
        for q, rw in enumerate(rows):
            m_s[pl.ds(rw, ch), :] = mm[q * ch:(q + 1) * ch, :]
            l_s[pl.ds(rw, ch), :] = ll[q * ch:(q + 1) * ch, :]
            acc_s[pl.ds(rw, ch), :] = aa[q * ch:(q + 1) * ch, :]
        return carry

    lax.fori_loop(0, 4 * nb4, d4_body, 0)

    def renat(blk, carry):
        j0 = pl.multiple_of(blk * MAX_DIL, MAX_DIL)
        r0 = pl.multiple_of(blk * PERM, PERM)
        rows = [pl.multiple_of(r * sub + j0, MAX_DIL) for r in range(MAX_DIL)]
        mm = jnp.concatenate([m_s[pl.ds(rw, MAX_DIL), :] for rw in rows], axis=0)
        ll = jnp.concatenate([l_s[pl.ds(rw, MAX_DIL), :] for rw in rows], axis=0)
        aa = jnp.concatenate([acc_s[pl.ds(rw, MAX_DIL), :] for rw in rows], axis=0)
        o = (aa / ll).astype(BF16)
        hi, mid, lo = _split3(mm + jnp.log2(ll))
        on_s[pl.ds(r0, PERM), :] = jnp.dot(perm, o, preferred_element_type=F32)
        lse_s[pl.ds(r0, PERM), :] = (jnp.dot(perm, hi, preferred_element_type=F32)
                                     + jnp.dot(perm, mid, preferred_element_type=F32)
                                     + jnp.dot(perm, lo, preferred_element_type=F32))
        return carry

    lax.fori_loop(0, seq // PERM, renat, 0)

    def d1_body(n, carry):
        c0 = pl.multiple_of(n * WIN, WIN)
        p0 = pl.multiple_of(jnp.maximum(n - 1, 0) * WIN, WIN)
        qb = q_ref[0, pl.ds(c0, WIN), :]
        kb = jnp.concatenate([k_ref[0, pl.ds(p0, WIN), :], k_ref[0, pl.ds(c0, WIN), :]], axis=0)
        vb = jnp.concatenate([v_ref[0, pl.ds(p0, WIN), :], v_ref[0, pl.ds(c0, WIN), :]], axis=0)
        lse = lse_s[pl.ds(c0, WIN), :]
        o = on_s[pl.ds(c0, WIN), :]
        one = jnp.ones((WIN, 1), F32)
        state = [lse[:, 0:1], one, o, lse[:, HEAD_DIM:HEAD_DIM + 1], one, o]
        mask = band_nat & (cur | (n > 0))
        m0, l0, a0, m1, l1, a1 = attend(qb, kb, vb, mask, state)
        o_ref[0, pl.ds(c0, WIN), :] = jnp.where(own0, a0 / l0, a1 / l1).astype(o_ref.dtype)
        return carry

    lax.fori_loop(0, seq // WIN, d1_body, 0)


def _dilated_fast(o_ref, qn, kn, vn, qr, kr, vr, fr, fn, s_scr, mask_scr, *, seq):
    sub = seq // MAX_DIL
    own0 = lax.broadcasted_iota(jnp.int32, (WIN, LANES), 1) < HEAD_DIM
    ri = lax.broadcasted_iota(jnp.int32, (PERM, PERM), 0)
    ci = lax.broadcasted_iota(jnp.int32, (PERM, PERM), 1)
    perm = (((ri >> 4) == (ci & 15)) & ((ri & 15) == (ci >> 4))).astype(BF16)

    qa = lax.broadcasted_iota(jnp.int32, (WIN, 2 * WIN), 0)
    kb_i = lax.broadcasted_iota(jnp.int32, (WIN, 2 * WIN), 1)
    cur = kb_i >= WIN
    kin = kb_i & (WIN - 1)
    dist_nat = qa - kin + jnp.where(cur, 0, WIN)
    band_nat = (dist_nat >= 0) & (dist_nat <= WIN)
    sq = ((qa & 31) << 2) + (qa >> 5)
    sk = ((kin & 31) << 2) + (kin >> 5)
    dist_4 = sq - sk + jnp.where(cur, 0, WIN)
    band_4 = (dist_4 >= 0) & (dist_4 <= WIN)
    for i, band in enumerate((band_nat, band_4)):
        mask_scr[2 * i] = jnp.where(band & cur, 1.0, 0.0).astype(BF16)
        mask_scr[2 * i + 1] = jnp.where(band, 1.0, 0.0).astype(BF16)

    def deint(blk, carry):
        r0 = pl.multiple_of(blk * PERM, PERM)
        j0 = pl.multiple_of(blk * MAX_DIL, MAX_DIL)
        for srcs, dsts in ((qn, qr), (kn, kr), (vn, vr)):
            for src, dst in zip(srcs, dsts):
                y = jnp.dot(perm, src[pl.ds(r0, PERM), :], preferred_element_type=F32).astype(BF16)
                for r in range(MAX_DIL):
                    dst[r, pl.ds(j0, MAX_DIL), :] = y[r * MAX_DIL:(r + 1) * MAX_DIL, :]
        return carry

    lax.fori_loop(0, seq // PERM, deint, 0)

    def run_branch(nblocks, block_in_seq, fetch, mask_base, sink):
        def produce(t, pair, u):
            for hl in range(2):
                qb, kb = fetch(t, hl, True)
                s_scr[pair, u, hl] = lax.dot_general(qb, kb, (((1,), (1,)), ((), ())), preferred_element_type=F32)

        def consume(t, pair, u):
            mk = mask_scr[mask_base + jnp.minimum(block_in_seq(t), 1)]
            pv = []
            for hl in range(2):
                p = jnp.exp2(s_scr[pair, u, hl]).astype(BF16) * mk
                pv.append(jnp.dot(p, fetch(t, hl, False), preferred_element_type=F32))
            sink(t, pv)

        produce(0, 0, 0)
        produce(1, 0, 1)

        def body(i, carry):
            pair = i & 1
            for u in range(2):
                consume(2 * i + u, pair, u)
            for u in range(2):
                produce(jnp.minimum(2 * i + 2 + u, nblocks - 1), 1 - pair, u)
            return carry

        lax.fori_loop(0, nblocks // 2, body, 0)

    def prev_cur(ref_block, n, size):
        c0 = pl.multiple_of(n * size, size)
        p0 = pl.multiple_of(jnp.maximum(n - 1, 0) * size, size)
        return ref_block(p0) + ref_block(c0)

    nb16 = sub // WIN

    def fetch16(t, hl, qk):
        r, n = t // nb16, t % nb16
        if qk:
            return (qr[hl][r, pl.ds(pl.multiple_of(n * WIN, WIN), WIN), :],
                    jnp.concatenate(prev_cur(lambda s: [kr[hl][r, pl.ds(s, WIN), :]], n, WIN), axis=0))
        return jnp.concatenate(prev_cur(lambda s: [vr[hl][r, pl.ds(s, WIN), :]], n, WIN), axis=0)

    def sink16(t, pv):
        row = pl.multiple_of(t * WIN, WIN)
        for hl in range(2):
            fr[hl][pl.ds(row, WIN), :] = pv[hl]

    run_branch(MAX_DIL * nb16, lambda t: t % nb16, fetch16, 0, sink16)

    nb4 = (seq // 4) // WIN
    ch = WIN // 4

    def fetch4(t, hl, qk):
        r4, n = t // nb4, t % nb4
        gather = lambda ref: (lambda s: [ref[r4 + 4 * q, pl.ds(s, ch), :] for q in range(4)])
        if qk:
            return (jnp.concatenate(gather(qr[hl])(pl.multiple_of(n * ch, ch)), axis=0),
                    jnp.concatenate(prev_cur(gather(kr[hl]), n, ch), axis=0))
        return jnp.concatenate(prev_cur(gather(vr[hl]), n, ch), axis=0)

    def sink4(t, pv):
        r4, n = t // nb4, t % nb4
        for q in range(4):
            row = pl.multiple_of((r4 + 4 * q) * sub + n * ch, ch)
            for hl in range(2):
                fr[hl][pl.ds(row, ch), :] += pv[hl][q * ch:(q + 1) * ch, :]

    run_branch(4 * nb4, lambda t: t % nb4, fetch4, 2, sink4)

    def renat(blk, carry):
        j0 = pl.multiple_of(blk * MAX_DIL, MAX_DIL)
        r0 = pl.multiple_of(blk * PERM, PERM)
        for hl in range(2):
            a = jnp.concatenate([fr[hl][pl.ds(pl.multiple_of(r * sub + j0, MAX_DIL), MAX_DIL), :]
                                 for r in range(MAX_DIL)], axis=0)
            hi, mid, lo = _split3(a)
            fn[hl][pl.ds(r0, PERM), :] = (jnp.dot(perm, hi, preferred_element_type=F32)
                                          + jnp.dot(perm, mid, preferred_element_type=F32)
                                          + jnp.dot(perm, lo, preferred_element_type=F32))
        return carry

    lax.fori_loop(0, seq // PERM, renat, 0)

    def fetch1(t, hl, qk):
        if qk:
            return (qn[hl][pl.ds(pl.multiple_of(t * WIN, WIN), WIN), :],
                    jnp.concatenate(prev_cur(lambda s: [kn[hl][pl.ds(s, WIN), :]], t, WIN), axis=0))
        return jnp.concatenate(prev_cur(lambda s: [vn[hl][pl.ds(s, WIN), :]], t, WIN), axis=0)

    def sink1(t, pv):
        row = pl.multiple_of(t * WIN, WIN)
        t0 = fn[0][pl.ds(row, WIN), :] + pv[0]
        t1 = fn[1][pl.ds(row, WIN), :] + pv[1]
        o_ref[0, pl.ds(row, WIN), :] = jnp.where(own0, t0 / t0[:, HEAD_DIM:HEAD_DIM + 1],
                                                 t1 / t1[:, 0:1]).astype(o_ref.dtype)

    run_branch(seq // WIN, lambda t: t, fetch1, 0, sink1)


def _dilated_kernel(q_ref, k_ref, v_ref, o_ref, qn0, qn1, kn0, kn1, vn0, vn1, qr0, qr1, kr0, kr1, vr0, vr1,
                    f0, f1, f2, f3, f4, s_scr, mask_scr, *, seq, chunk):
    lane = lax.broadcasted_iota(jnp.int32, (chunk, LANES), 1)
    qn, kn, vn = (qn0, qn1), (kn0, kn1), (vn0, vn1)

    def ksq_body(ci, ksq):
        kc = k_ref[0, pl.ds(pl.multiple_of(ci * chunk, chunk), chunk), :].astype(F32)
        return tuple(jnp.maximum(ksq[hl], jnp.max(_own_sum(kc * kc, lane, hl), axis=0, keepdims=True))
                     for hl in range(2))

    zero = jnp.zeros((1, 1), F32)
    ksq = lax.fori_loop(0, seq // chunk, ksq_body, (zero, zero))

    def build(ci, gap):
        rows = pl.ds(pl.multiple_of(ci * chunk, chunk), chunk)
        qc = q_ref[0, rows, :].astype(F32)
        kc = k_ref[0, rows, :].astype(F32)
        vc = v_ref[0, rows, :].astype(F32)
        for hl in range(2):
            own = _own_lanes(lane, hl)
            base = HEAD_DIM if hl == 0 else 0
            bound = _norm_bound(_own_sum(qc * qc, lane, hl), ksq[hl])
            gap = jnp.maximum(gap, jnp.max(bound - _own_sum(qc * kc, lane, hl), axis=0, keepdims=True))
            hi, mid, lo = (t.astype(F32) for t in _split3(-bound))
            qaug = jnp.where(lane == base, hi, jnp.where(lane == base + 1, mid, jnp.where(lane == base + 2, lo, 0.0)))
            qn[hl][rows, :] = jnp.where(own, qc, qaug).astype(BF16)
            kn[hl][rows, :] = jnp.where(own, kc, jnp.where((lane >= base) & (lane < base + 3), 1.0, 0.0)).astype(BF16)
            vn[hl][rows, :] = jnp.where(own, vc, jnp.where(lane == base, 1.0, 0.0)).astype(BF16)
        return gap

    gap = jnp.max(lax.fori_loop(0, seq // chunk, build, jnp.full((1, 1), NEG_INF, F32)))

    @pl.when(gap <= GAP_MAX)
    def _fast():
        _dilated_fast(o_ref, qn, kn, vn, (qr0, qr1), (kr0, kr1), (vr0, vr1), (f0, f1), (f2, f3),
                      s_scr, mask_scr, seq=seq)

    @pl.when(gap > GAP_MAX)
    def _general():
        _dilated_general(q_ref, k_ref, v_ref, o_ref, qr0, kr0, vr0, f0, f1, f2, f3, f4, seq=seq)


def _dilated(qkva, *, chunk=512):
    b, seq, _ = qkva.shape
    sub = seq // MAX_DIL
    kern = functools.partial(_dilated_kernel, seq=seq, chunk=chunk)
    return pl.pallas_call(
        kern,
        grid=(b, HEAD_PAIRS),
        in_specs=[
            pl.BlockSpec((1, seq, LANES), lambda bi, hp: (bi, 0, hp)),
            pl.BlockSpec((1, seq, LANES), lambda bi, hp: (bi, 0, HEAD_PAIRS + hp)),
            pl.BlockSpec((1, seq, LANES), lambda bi, hp: (bi, 0, 2 * HEAD_PAIRS + hp)),
        ],
        out_specs=pl.BlockSpec((1, seq, LANES), lambda bi, hp: (bi, 0, hp)),
        out_shape=jax.ShapeDtypeStruct((b, seq, WIDTH_A), BF16),
        scratch_shapes=(
            [pltpu.VMEM((seq, LANES), BF16)] * 6
            + [pltpu.VMEM((MAX_DIL, sub, LANES), BF16)] * 6
            + [pltpu.VMEM((seq, LANES), F32)] * 5
            + [pltpu.VMEM((2, 2, 2, WIN, 2 * WIN), F32), pltpu.VMEM((4, WIN, 2 * WIN), BF16)]
        ),
        compiler_params=pltpu.CompilerParams(
            dimension_semantics=("arbitrary", "arbitrary"), vmem_limit_bytes=VMEM_LIMIT),
        name="dilated",
    )(qkva, qkva, qkva)


def _outproj_kernel(oa_ref, ob_ref, x_ref, wo_ref, g_ref, wr_ref, x1_ref, h2_ref, comb_ref):
    tm = x_ref.shape[0]
    x1 = (x_ref[...]
          + jnp.dot(oa_ref[...], wo_ref[0:WIDTH_A, :], preferred_element_type=F32)
          + jnp.dot(ob_ref[...], wo_ref[WIDTH_A:WIDTH_A + WIDTH_B, :], preferred_element_type=F32))
    x1_ref[...] = x1
    ms = jnp.mean(x1 * x1, axis=-1, keepdims=True)
    h2 = x1 * lax.rsqrt(ms + NORM_EPS) * g_ref[...]
    h2_ref[...] = h2.astype(BF16)
    logits = jnp.dot(h2, wr_ref[...], preferred_element_type=F32, precision=lax.Precision.HIGHEST)
    lane = lax.broadcasted_iota(jnp.int32, (tm, LANES), 1)
    lane_f = lane.astype(F32)
    big = float(LANES)
    gmask = lane < N_GROUPS
    gl = jnp.where(gmask, logits, NEG_INF)
    gmax = jnp.max(gl, axis=1, keepdims=True)
    gsum = jnp.sum(jnp.where(gmask, jnp.exp(gl - gmax), 0.0), axis=1, keepdims=True)
    p_top = 1.0 / gsum
    g_star = jnp.min(jnp.where(gmask & (gl == gmax), lane_f, big), axis=1, keepdims=True)
    lo_lane = ROUTER_OFF + EXPERTS_PER_GROUP * g_star
    emask = (lane_f >= lo_lane) & (lane_f < lo_lane + EXPERTS_PER_GROUP)
    sel = jnp.where(emask, logits, NEG_INF)
    v1 = jnp.max(sel, axis=1, keepdims=True)
    i1 = jnp.min(jnp.where(emask & (sel == v1), lane_f, big), axis=1, keepdims=True)
    rest = emask & (lane_f != i1)
    sel2 = jnp.where(rest, logits, NEG_INF)
    v2 = jnp.max(sel2, axis=1, keepdims=True)
    i2 = jnp.min(jnp.where(rest & (sel2 == v2), lane_f, big), axis=1, keepdims=True)
    e2 = jnp.exp(v2 - v1)
    w1 = p_top / (1.0 + e2)
    w2 = p_top * e2 / (1.0 + e2)
    comb_ref[...] = jnp.where(lane_f == i1, w1, jnp.where(lane_f == i2, w2, 0.0))


def _outproj(oa, ob, x2, wo, g, wr, *, tm):
    n = x2.shape[0]
    return pl.pallas_call(
        _outproj_kernel,
        grid=(n // tm,),
        in_specs=[
            pl.BlockSpec((tm, WIDTH_A), lambda i: (i, 0)),
            pl.BlockSpec((tm, WIDTH_B), lambda i: (i, 0)),
            pl.BlockSpec((tm, D_MODEL), lambda i: (i, 0)),
            pl.BlockSpec(wo.shape, lambda i: (0, 0)),
            pl.BlockSpec((1, D_MODEL), lambda i: (0, 0)),
            pl.BlockSpec(wr.shape, lambda i: (0, 0)),
        ],
        out_specs=[
            pl.BlockSpec((tm, D_MODEL), lambda i: (i, 0)),
            pl.BlockSpec((tm, D_MODEL), lambda i: (i, 0)),
            pl.BlockSpec((tm, LANES), lambda i: (i, 0)),
        ],
        out_shape=[
            jax.ShapeDtypeStruct((n, D_MODEL), F32),
            jax.ShapeDtypeStruct((n, D_MODEL), BF16),
            jax.ShapeDtypeStruct((n, LANES), F32),
        ],
        compiler_params=pltpu.CompilerParams(dimension_semantics=("arbitrary",), vmem_limit_bytes=VMEM_LIMIT),
        name="outproj",
    )(oa, ob, x2, wo, g, wr)


def _moe_kernel(h2_ref, comb_ref, x1_ref, wg_ref, wu_ref, wd_ref, g_ref, out_ref, acc_scr):
    e = pl.program_id(1)
    tm = h2_ref.shape[0]

    @pl.when(e == 0)
    def _():
        acc_scr[...] = jnp.zeros_like(acc_scr)

    h2 = h2_ref[...]
    gate = jnp.dot(h2, wg_ref[0], preferred_element_type=F32)
    up = jnp.dot(h2, wu_ref[0], preferred_element_type=F32)
    he = (gate / (1.0 + jnp.exp(-gate)) * up).astype(BF16)
    y = jnp.dot(he, wd_ref[0], preferred_element_type=F32)
    lane = lax.broadcasted_iota(jnp.int32, (tm, LANES), 1)
    ce = jnp.sum(jnp.where(lane == ROUTER_OFF + e, comb_ref[...], 0.0), axis=1, keepdims=True)
    acc_scr[...] += ce * y

    @pl.when(e == N_EXPERTS - 1)
    def _():
        x2 = x1_ref[...] + acc_scr[...]
        ms = jnp.mean(x2 * x2, axis=-1, keepdims=True)
        out_ref[...] = x2 * lax.rsqrt(ms + NORM_EPS) * g_ref[...]


def _moe(h2, comb, x1, wg, wu, wd, g, *, tm):
    n = h2.shape[0]
    return pl.pallas_call(
        _moe_kernel,
        grid=(n // tm, N_EXPERTS),
        in_specs=[
            pl.BlockSpec((tm, D_MODEL), lambda i, e: (i, 0)),
            pl.BlockSpec((tm, LANES), lambda i, e: (i, 0)),
            pl.BlockSpec((tm, D_MODEL), lambda i, e: (i, 0)),
            pl.BlockSpec((1, D_MODEL, D_EXPERT), lambda i, e: (e, 0, 0)),
            pl.BlockSpec((1, D_MODEL, D_EXPERT), lambda i, e: (e, 0, 0)),
            pl.BlockSpec((1, D_EXPERT, D_MODEL), lambda i, e: (e, 0, 0)),
            pl.BlockSpec((1, D_MODEL), lambda i, e: (0, 0)),
        ],
        out_specs=pl.BlockSpec((tm, D_MODEL), lambda i, e: (i, 0)),
        out_shape=jax.ShapeDtypeStruct((n, D_MODEL), F32),
        scratch_shapes=[pltpu.VMEM((tm, D_MODEL), F32)],
        compiler_params=pltpu.CompilerParams(
            dimension_semantics=("arbitrary", "arbitrary"), vmem_limit_bytes=VMEM_LIMIT),
        name="moe",
    )(h2, comb, x1, wg, wu, wd, g)


def _rotary_tables(seq):
    pos = jnp.arange(seq, dtype=F32)
    inv_freq = 1.0 / (ROPE_THETA ** (jnp.arange(0, ROT_DIM, 2, dtype=F32) / ROT_DIM))
    ang = pos[:, None] * inv_freq[None, :]
    cos, sin = jnp.cos(ang), jnp.sin(ang)
    zeros = jnp.zeros((seq, HEAD_DIM - ROT_DIM), F32)
    zh = jnp.zeros((seq, ROT_HALF), F32)
    cos_h = jnp.concatenate([cos, cos, jnp.ones((seq, HEAD_DIM - ROT_DIM), F32)], axis=1)
    sa_h = jnp.concatenate([-sin, zh, zeros], axis=1)
    sb_h = jnp.concatenate([zh, sin, zeros], axis=1)
    tile = lambda t: jnp.concatenate([t, t], axis=1)
    return tile(cos_h), tile(sa_h), tile(sb_h)


def kernel(x, attn_norm, w_in, b_forget, w_out, ffn_norm, w_group, w_expert, w_gate_e, w_up_e, w_down_e, final_norm):
    b, seq, d = x.shape
    assert d == D_MODEL and w_in.shape[0] == 1, "single-layer block"
    n = b * seq
    scale = HEAD_DIM ** -0.5 * LOG2E
    col_scale = jnp.ones((2 * QKV_WIDTH + HEADS_B,), F32)
    col_scale = col_scale.at[0:WIDTH_A].set(scale).at[QKV_WIDTH:QKV_WIDTH + WIDTH_B].set(scale)
    w = jnp.pad(w_in[0] * col_scale[None, :], ((0, 0), (0, LANES - HEADS_B))).astype(BF16)
    bf = jnp.pad(b_forget[0].astype(F32), (0, LANES - HEADS_B))[None, :]
    cosv, sa, sb = _rotary_tables(seq)
    x2 = x.reshape(n, d)

    qkva, qkvb, c = _inproj(x2, attn_norm[0][None, :], w, cosv, sa, sb, bf, seq=seq, tm=512)
    out_a = _dilated(qkva.reshape(b, seq, QKV_WIDTH))
    out_b = _fox(qkvb.reshape(b, seq, QKV_WIDTH), c.reshape(b, seq, LANES), tq=512)

    wr = jnp.pad(jnp.concatenate([w_group[0], w_expert[0]], axis=1).astype(F32),
                 ((0, 0), (0, LANES - N_GROUPS - N_EXPERTS)))
    x1, h2, comb = _outproj(out_a.reshape(n, WIDTH_A), out_b.reshape(n, WIDTH_B), x2,
                            w_out[0].astype(BF16), ffn_norm[0][None, :], wr, tm=512)
    out = _moe(h2, comb, x1, w_gate_e[0].astype(BF16), w_up_e[0].astype(BF16), w_down_e[0].astype(BF16),
               final_norm[None, :], tm=1024)
    return out.reshape(b, seq, d)
```

```python
import functools

import jax
import jax.numpy as jnp
from jax import lax
from jax.experimental import pallas as pl
from jax.experimental.pallas import tpu as pltpu

F32 = jnp.float32
BF16 = jnp.bfloat16

D_MODEL = 1024
HEAD_DIM = 64
HEADS_A = 8
HEADS_B = 8
WIDTH_A = HEADS_A * HEAD_DIM
WIDTH_B = HEADS_B * HEAD_DIM
QKV_WIDTH = 3 * WIDTH_A
DILATIONS = ((128, 1), (512, 4), (2048, 16))
ROT_DIM = HEAD_DIM // 4
ROT_HALF = ROT_DIM // 2
ROPE_THETA = 500000.0
N_GROUPS = 4
EXPERTS_PER_GROUP = 4
N_EXPERTS = N_GROUPS * EXPERTS_PER_GROUP
D_EXPERT = 512
NORM_EPS = 1e-6
NEG_INF = -1e30

LANES = 128
HEAD_PAIRS = WIDTH_A // LANES
WIN = 128
MAX_DIL = 16
PERM = MAX_DIL * MAX_DIL
KV_BLOCK = 256
N_PAIRS = EXPERTS_PER_GROUP * (EXPERTS_PER_GROUP - 1) // 2
N_CLASSES = N_GROUPS * N_PAIRS
MOE_TILE = 256
EXT_WIDTH = D_MODEL + LANES
DMA_UNROLL = 8
ROUTER_OFF = N_GROUPS
VMEM_LIMIT = 56 * 1024 * 1024
LOG2E = 1.4426950408889634
BOUND_SLACK = 1.0 + 2.0 ** -7
BOUND_EPS = 2.0 ** -7
GAP_MAX = 64.0


def _split3(x):
    hi = x.astype(BF16)
    r1 = x - hi.astype(F32)
    mid = r1.astype(BF16)
    lo = (r1 - mid.astype(F32)).astype(BF16)
    return hi, mid, lo


def _inproj_kernel(x_ref, g_ref, w_ref, cos_ref, sa_ref, sb_ref, bf_ref,
                   qkva_ref, qkvb_ref, c_ref, h_scr, carry_scr, *, tiles_per_seq):
    i = pl.program_id(0)
    tm = x_ref.shape[0]
    x = x_ref[...]
    ms = jnp.mean(x * x, axis=-1, keepdims=True)
    h_scr[...] = (x * lax.rsqrt(ms + NORM_EPS) * g_ref[...]).astype(BF16)
    cosv = cos_ref[...]
    sa = sa_ref[...]
    sb = sb_ref[...]
    for g in range(6):
        p = jnp.dot(h_scr[...], w_ref[:, g * WIDTH_A:(g + 1) * WIDTH_A], preferred_element_type=F32)
        dst = qkva_ref if g < 3 else qkvb_ref
        col0 = (g % 3) * WIDTH_A
        if g < 2:
            for k in range(HEAD_PAIRS):
                t = p[:, k * LANES:(k + 1) * LANES]
                t = (t * cosv + pltpu.roll(t, LANES - ROT_HALF, 1) * sa + pltpu.roll(t, ROT_HALF, 1) * sb)
                dst[:, col0 + k * LANES:col0 + (k + 1) * LANES] = t.astype(BF16)
        else:
            dst[:, col0:col0 + WIDTH_A] = p.astype(BF16)
    z = jnp.dot(h_scr[...], w_ref[:, 2 * QKV_WIDTH:2 * QKV_WIDTH + LANES], preferred_element_type=F32) + bf_ref[...]
    lf = jnp.minimum(z, 0.0) - jnp.log1p(jnp.exp(-jnp.abs(z)))
    lane = lax.broadcasted_iota(jnp.int32, (tm, LANES), 1)
    lf = jnp.where(lane < HEADS_B, lf * LOG2E, 0.0)
    hi, mid, lo = _split3(lf)
    tri = (lax.broadcasted_iota(jnp.int32, (tm, tm), 0) >= lax.broadcasted_iota(jnp.int32, (tm, tm), 1)).astype(BF16)
    cs = (jnp.dot(tri, hi, preferred_element_type=F32) + jnp.dot(tri, mid, preferred_element_type=F32)
          + jnp.dot(tri, lo, preferred_element_type=F32))

    @pl.when(i % tiles_per_seq == 0)
    def _():
        carry_scr[...] = jnp.zeros_like(carry_scr)

    c = cs + carry_scr[0:1, :]
    c_ref[...] = c
    carry_scr[...] = jnp.broadcast_to(c[tm - 1:tm, :], carry_scr.shape)


def _inproj(x2, g, w, cosv, sa, sb, bf, *, seq, tm):
    n = x2.shape[0]
    kern = functools.partial(_inproj_kernel, tiles_per_seq=seq // tm)
    tps = seq // tm
    return pl.pallas_call(
        kern,
        grid=(n // tm,),
        in_specs=[
            pl.BlockSpec((tm, D_MODEL), lambda i: (i, 0)),
            pl.BlockSpec((1, D_MODEL), lambda i: (0, 0)),
            pl.BlockSpec(w.shape, lambda i: (0, 0)),
            pl.BlockSpec((tm, LANES), lambda i: (i % tps, 0)),
            pl.BlockSpec((tm, LANES), lambda i: (i % tps, 0)),
            pl.BlockSpec((tm, LANES), lambda i: (i % tps, 0)),
            pl.BlockSpec((1, LANES), lambda i: (0, 0)),
        ],
        out_specs=[
            pl.BlockSpec((tm, QKV_WIDTH), lambda i: (i, 0)),
            pl.BlockSpec((tm, QKV_WIDTH), lambda i: (i, 0)),
            pl.BlockSpec((tm, LANES), lambda i: (i, 0)),
        ],
        out_shape=[
            jax.ShapeDtypeStruct((n, QKV_WIDTH), BF16),
            jax.ShapeDtypeStruct((n, QKV_WIDTH), BF16),
            jax.ShapeDtypeStruct((n, LANES), F32),
        ],
        scratch_shapes=[pltpu.VMEM((tm, D_MODEL), BF16), pltpu.VMEM((8, LANES), F32)],
        compiler_params=pltpu.CompilerParams(dimension_semantics=("arbitrary",), vmem_limit_bytes=VMEM_LIMIT),
        name="inproj",
    )(x2, g, w, cosv, sa, sb, bf)


def _head_column(c_all, lane, head):
    return jnp.sum(jnp.where(lane == head, c_all, 0.0), axis=1, keepdims=True)


def _own_lanes(lane, hl):
    return (lane < HEAD_DIM) if hl == 0 else (lane >= HEAD_DIM)


def _own_sum(vals, lane, hl):
    return jnp.sum(jnp.where(_own_lanes(lane, hl), vals, 0.0), axis=1, keepdims=True)


def _norm_bound(qsq, ksq_max):
    return jnp.sqrt(qsq * ksq_max) * BOUND_SLACK + BOUND_EPS


def _fox_kernel(q_ref, k_ref, v_ref, cq_ref, ck_ref, o_ref,
                k0_scr, k1_scr, v0_scr, v1_scr, ksq_scr, acc0_scr, acc1_scr, s_scr, *, tq, seq, chunk):
    hp = pl.program_id(1)
    qi = pl.program_id(2)
    k_scr = (k0_scr, k1_scr)
    v_scr = (v0_scr, v1_scr)
    acc_scr = (acc0_scr, acc1_scr)

    def augmented(vals, cterm, lane, hl, key_side):
        hi, mid, lo = (t.astype(F32) for t in _split3(cterm))
        base = HEAD_DIM if hl == 0 else 0
        ones = (lane >= base + (0 if key_side else 3)) & (lane < base + (3 if key_side else 6))
        o = 3 if key_side else 0
        sgn = -1.0 if key_side else 1.0
        aug = jnp.where(lane == base + o, sgn * hi,
                        jnp.where(lane == base + o + 1, sgn * mid,
                                  jnp.where(lane == base + o + 2, sgn * lo,
                                            jnp.where(ones, 1.0, 0.0))))
        return jnp.where(_own_lanes(lane, hl), vals, aug).astype(BF16)

    @pl.when(qi == 0)
    def _build_keys():
        lane = lax.broadcasted_iota(jnp.int32, (chunk, LANES), 1)

        def body(ci, ksq):
            r0 = pl.multiple_of(ci * chunk, chunk)
            kc = k_ref[0, pl.ds(r0, chunk), :].astype(F32)
            vc = v_ref[0, pl.ds(r0, chunk), :].astype(F32)
            cc = ck_ref[0, pl.ds(r0, chunk), :]
            new = []
            for hl in range(2):
                k_scr[hl][pl.ds(r0, chunk), :] = augmented(kc, _head_column(cc, lane, 2 * hp + hl), lane, hl, True)
                one_lane = HEAD_DIM if hl == 0 else 0
                v_scr[hl][pl.ds(r0, chunk), :] = jnp.where(
                    _own_lanes(lane, hl), vc, jnp.where(lane == one_lane, 1.0, 0.0)).astype(BF16)
                new.append(jnp.maximum(ksq[hl], jnp.max(_own_sum(kc * kc, lane, hl), axis=0, keepdims=True)))
            return tuple(new)

        zero = jnp.zeros((1, 1), F32)
        ksq0, ksq1 = lax.fori_loop(0, seq // chunk, body, (zero, zero))
        lane8 = lax.broadcasted_iota(jnp.int32, ksq_scr.shape, 1)
        ksq_scr[...] = jnp.where(lane8 < HEAD_DIM, ksq0, ksq1)

    lane_q = lax.broadcasted_iota(jnp.int32, (tq, LANES), 1)
    q2 = q_ref[0].astype(F32)
    cq = cq_ref[0]
    kd = k_ref[0, pl.ds(pl.multiple_of(qi * tq, tq), tq), :].astype(F32)
    ksq_row = ksq_scr[0:1, :]
    qh = []
    gap = None
    for hl in range(2):
        bound = _norm_bound(_own_sum(q2 * q2, lane_q, hl), ksq_row[:, hl * HEAD_DIM:hl * HEAD_DIM + 1])
        diag = _own_sum(q2 * kd, lane_q, hl)
        g = jnp.max(bound - diag)
        gap = g if gap is None else jnp.maximum(gap, g)
        qh.append(augmented(q2, _head_column(cq, lane_q, 2 * hp + hl) - bound, lane_q, hl, False))
    n_diag = tq // KV_BLOCK
    n_full = qi * n_diag
    row_minus_col = (lax.broadcasted_iota(jnp.int32, (tq, KV_BLOCK), 0)
                     - lax.broadcasted_iota(jnp.int32, (tq, KV_BLOCK), 1))

    def causal_mask(s, d):
        return jnp.where(row_minus_col >= d * KV_BLOCK, s, NEG_INF)

    def logits(hl, j):
        r0 = pl.multiple_of(j * KV_BLOCK, KV_BLOCK)
        return lax.dot_general(qh[hl], k_scr[hl][pl.ds(r0, KV_BLOCK), :], (((1,), (1,)), ((), ())),
                               preferred_element_type=F32)

    def values(hl, j):
        return v_scr[hl][pl.ds(pl.multiple_of(j * KV_BLOCK, KV_BLOCK), KV_BLOCK), :]

    def finish(a0, a1):
        l0 = a0[:, HEAD_DIM:HEAD_DIM + 1]
        l1 = a1[:, 0:1]
        o_ref[0] = jnp.where(lane_q < HEAD_DIM, a0 / l0, a1 / l1).astype(o_ref.dtype)

    @pl.when(gap <= GAP_MAX)
    def _fast():
        acc0_scr[...] = jnp.zeros_like(acc0_scr)
        acc1_scr[...] = jnp.zeros_like(acc1_scr)

        def produce(j, slot):
            for hl in range(2):
                s_scr[slot, hl] = logits(hl, j)

        def consume(j, slot, diag):
            for hl in range(2):
                s = s_scr[slot, hl]
                if diag is not None:
                    s = causal_mask(s, diag)
                acc_scr[hl][...] += jnp.dot(jnp.exp2(s).astype(BF16), values(hl, j), preferred_element_type=F32)

        produce(0, 0)

        def body(j, carry):
            slot = j & 1
            consume(j, slot, None)
            produce(j + 1, 1 - slot)
            return carry

        lax.fori_loop(0, n_full, body, 0)
        for d in range(n_diag):
            j = n_full + d
            consume(j, j & 1, d)
            if d + 1 < n_diag:
                produce(j + 1, (j + 1) & 1)
        finish(acc0_scr[...], acc1_scr[...])

    @pl.when(gap > GAP_MAX)
    def _general():
        def step(j, carry, diag):
            new = []
            for hl in range(2):
                m, a = carry[2 * hl:2 * hl + 2]
                s = logits(hl, j)
                if diag is not None:
                    s = causal_mask(s, diag)
                mn = jnp.maximum(m, jnp.max(s, axis=1, keepdims=True))
                a = jnp.exp2(m - mn) * a + jnp.dot(jnp.exp2(s - mn).astype(BF16), values(hl, j),
                                                  preferred_element_type=F32)
                new += [mn, a]
            return tuple(new)

        carry = (jnp.full((tq, 1), NEG_INF, F32), jnp.zeros((tq, LANES), F32)) * 2
        carry = lax.fori_loop(0, n_full, lambda j, c: step(j, c, None), carry)
        for d in range(n_diag):
            carry = step(n_full + d, carry, d)
        finish(carry[1], carry[3])


def _fox(qkvb, c, *, tq, chunk=512):
    b, seq, _ = qkvb.shape
    kern = functools.partial(_fox_kernel, tq=tq, seq=seq, chunk=chunk)
    return pl.pallas_call(
        kern,
        grid=(b, HEAD_PAIRS, seq // tq),
        in_specs=[
            pl.BlockSpec((1, tq, LANES), lambda bi, hp, qi: (bi, qi, hp)),
            pl.BlockSpec((1, seq, LANES), lambda bi, hp, qi: (bi, 0, HEAD_PAIRS + hp)),
            pl.BlockSpec((1, seq, LANES), lambda bi, hp, qi: (bi, 0, 2 * HEAD_PAIRS + hp)),
            pl.BlockSpec((1, tq, LANES), lambda bi, hp, qi: (bi, qi, 0)),
            pl.BlockSpec((1, seq, LANES), lambda bi, hp, qi: (bi, 0, 0)),
        ],
        out_specs=pl.BlockSpec((1, tq, LANES), lambda bi, hp, qi: (bi, qi, hp)),
        out_shape=jax.ShapeDtypeStruct((b, seq, WIDTH_B), BF16),
        scratch_shapes=[pltpu.VMEM((seq, LANES), BF16), pltpu.VMEM((seq, LANES), BF16),
                        pltpu.VMEM((seq, LANES), BF16), pltpu.VMEM((seq, LANES), BF16),
                        pltpu.VMEM((8, LANES), F32),
                        pltpu.VMEM((tq, LANES), F32), pltpu.VMEM((tq, LANES), F32),
                        pltpu.VMEM((2, 2, tq, KV_BLOCK), F32)],
        compiler_params=pltpu.CompilerParams(
            dimension_semantics=("arbitrary", "arbitrary", "arbitrary"), vmem_limit_bytes=VMEM_LIMIT),
        name="fox",
    )(qkvb, qkvb, qkvb, c, c)


def _dilated_general(q_ref, k_ref, v_ref, o_ref, q16, k16, v16, m_s, l_s, acc_s, on_s, lse_s, *, seq):
    sub = seq // MAX_DIL
    own0 = lax.broadcasted_iota(jnp.int32, (WIN, LANES), 1) < HEAD_DIM
    ri = lax.broadcasted_iota(jnp.int32, (PERM, PERM), 0)
    ci = lax.broadcasted_iota(jnp.int32, (PERM, PERM), 1)
    perm = (((ri >> 4) == (ci & 15)) & ((ri & 15) == (ci >> 4))).astype(BF16)

    qa = lax.broadcasted_iota(jnp.int32, (WIN, 2 * WIN), 0)
    kb_i = lax.broadcasted_iota(jnp.int32, (WIN, 2 * WIN), 1)
    cur = kb_i >= WIN
    kin = kb_i & (WIN - 1)
    dist_nat = qa - kin + jnp.where(cur, 0, WIN)
    band_nat = (dist_nat >= 0) & (dist_nat <= WIN)
    sq = ((qa & 31) << 2) + (qa >> 5)
    sk = ((kin & 31) << 2) + (kin >> 5)
    dist_4 = sq - sk + jnp.where(cur, 0, WIN)
    band_4 = (dist_4 >= 0) & (dist_4 <= WIN)

    def attend(qb, kb, vb, mask, state):
        qf = qb.astype(F32)
        new = []
        for hl in range(2):
            m, l, a = state[3 * hl:3 * hl + 3]
            own = own0 if hl == 0 else jnp.logical_not(own0)
            qh = jnp.where(own, qf, 0.0).astype(BF16)
            s = lax.dot_general(qh, kb, (((1,), (1,)), ((), ())), preferred_element_type=F32)
            s = jnp.where(mask, s, NEG_INF)
            mn = jnp.maximum(m, jnp.max(s, axis=1, keepdims=True))
            alpha = jnp.exp2(m - mn)
            p = jnp.exp2(s - mn)
            l = alpha * l + jnp.sum(p, axis=1, keepdims=True)
            a = alpha * a + jnp.dot(p.astype(BF16), vb, preferred_element_type=F32)
            new += [mn, l, a]
        return new

    def pack_state(st):
        m0, l0, a0, m1, l1, a1 = st
        return (jnp.where(own0, m0, m1), jnp.where(own0, l0, l1), jnp.where(own0, a0, a1))

    def deint(blk, carry):
        r0 = pl.multiple_of(blk * PERM, PERM)
        j0 = pl.multiple_of(blk * MAX_DIL, MAX_DIL)
        for src, dst in ((q_ref, q16), (k_ref, k16), (v_ref, v16)):
            y = jnp.dot(perm, src[0, pl.ds(r0, PERM), :], preferred_element_type=F32).astype(BF16)
            for r in range(MAX_DIL):
                dst[r, pl.ds(j0, MAX_DIL), :] = y[r * MAX_DIL:(r + 1) * MAX_DIL, :]
        return carry

    lax.fori_loop(0, seq // PERM, deint, 0)

    fresh = [jnp.full((WIN, 1), NEG_INF, F32), jnp.zeros((WIN, 1), F32), jnp.zeros((WIN, LANES), F32)] * 2
    nb16 = sub // WIN

    def d16_body(t, carry):
        r = t // nb16
        n = t % nb16
        c0 = pl.multiple_of(n * WIN, WIN)
        p0 = pl.multiple_of(jnp.maximum(n - 1, 0) * WIN, WIN)
        qb = q16[r, pl.ds(c0, WIN), :]
        kb = jnp.concatenate([k16[r, pl.ds(p0, WIN), :], k16[r, pl.ds(c0, WIN), :]], axis=0)
        vb = jnp.concatenate([v16[r, pl.ds(p0, WIN), :], v16[r, pl.ds(c0, WIN), :]], axis=0)
        mask = band_nat & (cur | (n > 0))
        mm, ll, aa = pack_state(attend(qb, kb, vb, mask, fresh))
        row = pl.multiple_of(r * sub + c0, WIN)
        m_s[pl.ds(row, WIN), :] = mm
        l_s[pl.ds(row, WIN), :] = ll
        acc_s[pl.ds(row, WIN), :] = aa
        return carry

    lax.fori_loop(0, MAX_DIL * nb16, d16_body, 0)

    nb4 = (seq // 4) // WIN
    ch = WIN // 4

    def d4_body(t, carry):
        r4 = t // nb4
        n = t % nb4
        c0 = pl.multiple_of(n * ch, ch)
        p0 = pl.multiple_of(jnp.maximum(n - 1, 0) * ch, ch)

        def gather(ref, j0):
            return [ref[r4 + 4 * q, pl.ds(j0, ch), :] for q in range(4)]

        qb = jnp.concatenate(gather(q16, c0), axis=0)
        kb = jnp.concatenate(gather(k16, p0) + gather(k16, c0), axis=0)
        vb = jnp.concatenate(gather(v16, p0) + gather(v16, c0), axis=0)
        rows = [pl.multiple_of((r4 + 4 * q) * sub + c0, ch) for q in range(4)]
        mm = jnp.concatenate([m_s[pl.ds(rw, ch), :] for rw in rows], axis=0)
        ll = jnp.concatenate([l_s[pl.ds(rw, ch), :] for rw in rows], axis=0)
        aa = jnp.concatenate([acc_s[pl.ds(rw, ch), :] for rw in rows], axis=0)
        state = [mm[:, 0:1], ll[:, 0:1], aa, mm[:, HEAD_DIM:HEAD_DIM + 1], ll[:, HEAD_DIM:HEAD_DIM + 1], aa]
        mask = band_4 & (cur | (n > 0))
        mm, ll, aa = pack_state(attend(qb, kb, vb, mask, state))
        for q, rw in enumerate(rows):
            m_s[pl.ds(rw, ch), :] = mm[q * ch:(q + 1) * ch, :]
            l_s[pl.ds(rw, ch), :] = ll[q * ch:(q + 1) * ch, :]
            acc_s[pl.ds(rw, ch), :] = aa[q * ch:(q + 1) * ch, :]
        return carry

    lax.fori_loop(0, 4 * nb4, d4_body, 0)

    def renat(blk, carry):
        j0 = pl.multiple_of(blk * MAX_DIL, MAX_DIL)
        r0 = pl.multiple_of(blk * PERM, PERM)
        rows = [pl.multiple_of(r * sub + j0, MAX_DIL) for r in range(MAX_DIL)]
        mm = jnp.concatenate([m_s[pl.ds(rw, MAX_DIL), :] for rw in rows], axis=0)
        ll = jnp.concatenate([l_s[pl.ds(rw, MAX_DIL), :] for rw in rows], axis=0)
        aa = jnp.concatenate([acc_s[pl.ds(rw, MAX_DIL), :] for rw in rows], axis=0)
        o = (aa / ll).astype(BF16)
        hi, mid, lo = _split3(mm + jnp.log2(ll))
        on_s[pl.ds(r0, PERM), :] = jnp.dot(perm, o, preferred_element_type=F32)
        lse_s[pl.ds(r0, PERM), :] = (jnp.dot(perm, hi, preferred_element_type=F32)
                                     + jnp.dot(perm, mid, preferred_element_type=F32)
                                     + jnp.dot(perm, lo, preferred_element_type=F32))
        return carry

    lax.fori_loop(0, seq // PERM, renat, 0)

    def d1_body(n, carry):
        c0 = pl.multiple_of(n * WIN, WIN)
        p0 = pl.multiple_of(jnp.maximum(n - 1, 0) * WIN, WIN)
        qb = q_ref[0, pl.ds(c0, WIN), :]
        kb = jnp.concatenate([k_ref[0, pl.ds(p0, WIN), :], k_ref[0, pl.ds(c0, WIN), :]], axis=0)
        vb = jnp.concatenate([v_ref[0, pl.ds(p0, WIN), :], v_ref[0, pl.ds(c0, WIN), :]], axis=0)
        lse = lse_s[pl.ds(c0, WIN), :]
        o = on_s[pl.ds(c0, WIN), :]
        one = jnp.ones((WIN, 1), F32)
        state = [lse[:, 0:1], one, o, lse[:, HEAD_DIM:HEAD_DIM + 1], one, o]
        mask = band_nat & (cur | (n > 0))
        m0, l0, a0, m1, l1, a1 = attend(qb, kb, vb, mask, state)
        o_ref[0, pl.ds(c0, WIN), :] = jnp.where(own0, a0 / l0, a1 / l1).astype(o_ref.dtype)
        return carry

    lax.fori_loop(0, seq // WIN, d1_body, 0)


def _dilated_fast(o_ref, qn, kn, vn, qr, kr, vr, fr, fn, s_scr, mask_scr, *, seq):
    sub = seq // MAX_DIL
    own0 = lax.broadcasted_iota(jnp.int32, (WIN, LANES), 1) < HEAD_DIM
    ri = lax.broadcasted_iota(jnp.int32, (PERM, PERM), 0)
    ci = lax.broadcasted_iota(jnp.int32, (PERM, PERM), 1)
    perm = (((ri >> 4) == (ci & 15)) & ((ri & 15) == (ci >> 4))).astype(BF16)

    qa = lax.broadcasted_iota(jnp.int32, (WIN, 2 * WIN), 0)
    kb_i = lax.broadcasted_iota(jnp.int32, (WIN, 2 * WIN), 1)
    cur = kb_i >= WIN
    kin = kb_i & (WIN - 1)
    dist_nat = qa - kin + jnp.where(cur, 0, WIN)
    band_nat = (dist_nat >= 0) & (dist_nat <= WIN)
    sq = ((qa & 31) << 2) + (qa >> 5)
    sk = ((kin & 31) << 2) + (kin >> 5)
    dist_4 = sq - sk + jnp.where(cur, 0, WIN)
    band_4 = (dist_4 >= 0) & (dist_4 <= WIN)
    for i, band in enumerate((band_nat, band_4)):
        mask_scr[2 * i] = jnp.where(band & cur, 1.0, 0.0).astype(BF16)
        mask_scr[2 * i + 1] = jnp.where(band, 1.0, 0.0).astype(BF16)

    def deint(blk, carry):
        r0 = pl.multiple_of(blk * PERM, PERM)
        j0 = pl.multiple_of(blk * MAX_DIL, MAX_DIL)
        for srcs, dsts in ((qn, qr), (kn, kr), (vn, vr)):
            for src, dst in zip(srcs, dsts):
                y = jnp.dot(perm, src[pl.ds(r0, PERM), :], preferred_element_type=F32).astype(BF16)
                for r in range(MAX_DIL):
                    dst[r, pl.ds(j0, MAX_DIL), :] = y[r * MAX_DIL:(r + 1) * MAX_DIL, :]
        return carry

    lax.fori_loop(0, seq // PERM, deint, 0)

    def run_branch(nblocks, block_in_seq, fetch, mask_base, sink):
        def produce(t, pair, u):
            for hl in range(2):
                qb, kb = fetch(t, hl, True)
                s_scr[pair, u, hl] = lax.dot_general(qb, kb, (((1,), (1,)), ((), ())), preferred_element_type=F32)

        def consume(t, pair, u):
            mk = mask_scr[mask_base + jnp.minimum(block_in_seq(t), 1)]
            pv = []
            for hl in range(2):
                p = jnp.exp2(s_scr[pair, u, hl]).astype(BF16) * mk
                pv.append(jnp.dot(p, fetch(t, hl, False), preferred_element_type=F32))
            sink(t, pv)

        produce(0, 0, 0)
        produce(1, 0, 1)

        def body(i, carry):
            pair = i & 1
            for u in range(2):
                consume(2 * i + u, pair, u)
            for u in range(2):
                produce(jnp.minimum(2 * i + 2 + u, nblocks - 1), 1 - pair, u)
            return carry

        lax.fori_loop(0, nblocks // 2, body, 0)

    def prev_cur(ref_block, n, size):
        c0 = pl.multiple_of(n * size, size)
        p0 = pl.multiple_of(jnp.maximum(n - 1, 0) * size, size)
        return ref_block(p0) + ref_block(c0)

    nb16 = sub // WIN

    def fetch16(t, hl, qk):
        r, n = t // nb16, t % nb16
        if qk:
            return (qr[hl][r, pl.ds(pl.multiple_of(n * WIN, WIN), WIN), :],
                    jnp.concatenate(prev_cur(lambda s: [kr[hl][r, pl.ds(s, WIN), :]], n, WIN), axis=0))
        return jnp.concatenate(prev_cur(lambda s: [vr[hl][r, pl.ds(s, WIN), :]], n, WIN), axis=0)

    def sink16(t, pv):
        row = pl.multiple_of(t * WIN, WIN)
        for hl in range(2):
            fr[hl][pl.ds(row, WIN), :] = pv[hl]

    run_branch(MAX_DIL * nb16, lambda t: t % nb16, fetch16, 0, sink16)

    nb4 = (seq // 4) // WIN
    ch = WIN // 4

    def fetch4(t, hl, qk):
        r4, n = t // nb4, t % nb4
        gather = lambda ref: (lambda s: [ref[r4 + 4 * q, pl.ds(s, ch), :] for q in range(4)])
        if qk:
            return (jnp.concatenate(gather(qr[hl])(pl.multiple_of(n * ch, ch)), axis=0),
                    jnp.concatenate(prev_cur(gather(kr[hl]), n, ch), axis=0))
        return jnp.concatenate(prev_cur(gather(vr[hl]), n, ch), axis=0)

    def sink4(t, pv):
        r4, n = t // nb4, t % nb4
        for q in range(4):
            row = pl.multiple_of((r4 + 4 * q) * sub + n * ch, ch)
            for hl in range(2):
                fr[hl][pl.ds(row, ch), :] += pv[hl][q * ch:(q + 1) * ch, :]

    run_branch(4 * nb4, lambda t: t % nb4, fetch4, 2, sink4)

    def renat(blk, carry):
        j0 = pl.multiple_of(blk * MAX_DIL, MAX_DIL)
        r0 = pl.multiple_of(blk * PERM, PERM)
        for hl in range(2):
            a = jnp.concatenate([fr[hl][pl.ds(pl.multiple_of(r * sub + j0, MAX_DIL), MAX_DIL), :]
                                 for r in range(MAX_DIL)], axis=0)
            hi, mid, lo = _split3(a)
            fn[hl][pl.ds(r0, PERM), :] = (jnp.dot(perm, hi, preferred_element_type=F32)
                                          + jnp.dot(perm, mid, preferred_element_type=F32)
                                          + jnp.dot(perm, lo, preferred_element_type=F32))
        return carry

    lax.fori_loop(0, seq // PERM, renat, 0)

    def fetch1(t, hl, qk):
        if qk:
            return (qn[hl][pl.ds(pl.multiple_of(t * WIN, WIN), WIN), :],
                    jnp.concatenate(prev_cur(lambda s: [kn[hl][pl.ds(s, WIN), :]], t, WIN), axis=0))
        return jnp.concatenate(prev_cur(lambda s: [vn[hl][pl.ds(s, WIN), :]], t, WIN), axis=0)

    def sink1(t, pv):
        row = pl.multiple_of(t * WIN, WIN)
        t0 = fn[0][pl.ds(row, WIN), :] + pv[0]
        t1 = fn[1][pl.ds(row, WIN), :] + pv[1]
        o_ref[0, pl.ds(row, WIN), :] = jnp.where(own0, t0 / t0[:, HEAD_DIM:HEAD_DIM + 1],
                                                 t1 / t1[:, 0:1]).astype(o_ref.dtype)

    run_branch(seq // WIN, lambda t: t, fetch1, 0, sink1)


def _dilated_kernel(q_ref, k_ref, v_ref, o_ref, qn0, qn1, kn0, kn1, vn0, vn1, qr0, qr1, kr0, kr1, vr0, vr1,
                    f0, f1, f2, f3, f4, s_scr, mask_scr, *, seq, chunk):
    lane = lax.broadcasted_iota(jnp.int32, (chunk, LANES), 1)
    qn, kn, vn = (qn0, qn1), (kn0, kn1), (vn0, vn1)

    def ksq_body(ci, ksq):
        kc = k_ref[0, pl.ds(pl.multiple_of(ci * chunk, chunk), chunk), :].astype(F32)
        return tuple(jnp.maximum(ksq[hl], jnp.max(_own_sum(kc * kc, lane, hl), axis=0, keepdims=True))
                     for hl in range(2))

    zero = jnp.zeros((1, 1), F32)
    ksq = lax.fori_loop(0, seq // chunk, ksq_body, (zero, zero))

    def build(ci, gap):
        rows = pl.ds(pl.multiple_of(ci * chunk, chunk), chunk)
        qc = q_ref[0, rows, :].astype(F32)
        kc = k_ref[0, rows, :].astype(F32)
        vc = v_ref[0, rows, :].astype(F32)
        for hl in range(2):
            own = _own_lanes(lane, hl)
            base = HEAD_DIM if hl == 0 else 0
            bound = _norm_bound(_own_sum(qc * qc, lane, hl), ksq[hl])
            gap = jnp.maximum(gap, jnp.max(bound - _own_sum(qc * kc, lane, hl), axis=0, keepdims=True))
            hi, mid, lo = (t.astype(F32) for t in _split3(-bound))
            qaug = jnp.where(lane == base, hi, jnp.where(lane == base + 1, mid, jnp.where(lane == base + 2, lo, 0.0)))
            qn[hl][rows, :] = jnp.where(own, qc, qaug).astype(BF16)
            kn[hl][rows, :] = jnp.where(own, kc, jnp.where((lane >= base) & (lane < base + 3), 1.0, 0.0)).astype(BF16)
            vn[hl][rows, :] = jnp.where(own, vc, jnp.where(lane == base, 1.0, 0.0)).astype(BF16)
        return gap

    gap = jnp.max(lax.fori_loop(0, seq // chunk, build, jnp.full((1, 1), NEG_INF, F32)))

    @pl.when(gap <= GAP_MAX)
    def _fast():
        _dilated_fast(o_ref, qn, kn, vn, (qr0, qr1), (kr0, kr1), (vr0, vr1), (f0, f1), (f2, f3),
                      s_scr, mask_scr, seq=seq)

    @pl.when(gap > GAP_MAX)
    def _general():
        _dilated_general(q_ref, k_ref, v_ref, o_ref, qr0, kr0, vr0, f0, f1, f2, f3, f4, seq=seq)


def _dilated(qkva, *, chunk=512):
    b, seq, _ = qkva.shape
    sub = seq // MAX_DIL
    kern = functools.partial(_dilated_kernel, seq=seq, chunk=chunk)
    return pl.pallas_call(
        kern,
        grid=(b, HEAD_PAIRS),
        in_specs=[
            pl.BlockSpec((1, seq, LANES), lambda bi, hp: (bi, 0, hp)),
            pl.BlockSpec((1, seq, LANES), lambda bi, hp: (bi, 0, HEAD_PAIRS + hp)),
            pl.BlockSpec((1, seq, LANES), lambda bi, hp: (bi, 0, 2 * HEAD_PAIRS + hp)),
        ],
        out_specs=pl.BlockSpec((1, seq, LANES), lambda bi, hp: (bi, 0, hp)),
        out_shape=jax.ShapeDtypeStruct((b, seq, WIDTH_A), BF16),
        scratch_shapes=(
            [pltpu.VMEM((seq, LANES), BF16)] * 6
            + [pltpu.VMEM((MAX_DIL, sub, LANES), BF16)] * 6
            + [pltpu.VMEM((seq, LANES), F32)] * 5
            + [pltpu.VMEM((2, 2, 2, WIN, 2 * WIN), F32), pltpu.VMEM((4, WIN, 2 * WIN), BF16)]
        ),
        compiler_params=pltpu.CompilerParams(
            dimension_semantics=("arbitrary", "arbitrary"), vmem_limit_bytes=VMEM_LIMIT),
        name="dilated",
    )(qkva, qkva, qkva)


def _outproj_kernel(oa_ref, ob_ref, x_ref, wo_ref, g_ref, wr_ref, x1_ref, h2_ref, cls_ref):
    tm = x_ref.shape[0]
    x1 = (x_ref[...]
          + jnp.dot(oa_ref[...], wo_ref[0:WIDTH_A, :], preferred_element_type=F32)
          + jnp.dot(ob_ref[...], wo_ref[WIDTH_A:WIDTH_A + WIDTH_B, :], preferred_element_type=F32))
    x1_ref[...] = x1
    ms = jnp.mean(x1 * x1, axis=-1, keepdims=True)
    h2 = x1 * lax.rsqrt(ms + NORM_EPS) * g_ref[...]
    h2_ref[:, 0:D_MODEL] = h2
    logits = jnp.dot(h2, wr_ref[...], preferred_element_type=F32, precision=lax.Precision.HIGHEST)
    lane = lax.broadcasted_iota(jnp.int32, (tm, LANES), 1)
    lane_f = lane.astype(F32)
    big = float(LANES)
    gmask = lane < N_GROUPS
    gl = jnp.where(gmask, logits, NEG_INF)
    gmax = jnp.max(gl, axis=1, keepdims=True)
    gsum = jnp.sum(jnp.where(gmask, jnp.exp(gl - gmax), 0.0), axis=1, keepdims=True)
    p_top = 1.0 / gsum
    g_star = jnp.min(jnp.where(gmask & (gl == gmax), lane_f, big), axis=1, keepdims=True)
    lo_lane = ROUTER_OFF + EXPERTS_PER_GROUP * g_star
    emask = (lane_f >= lo_lane) & (lane_f < lo_lane + EXPERTS_PER_GROUP)
    sel = jnp.where(emask, logits, NEG_INF)
    v1 = jnp.max(sel, axis=1, keepdims=True)
    i1 = jnp.min(jnp.where(emask & (sel == v1), lane_f, big), axis=1, keepdims=True)
    rest = emask & (lane_f != i1)
    sel2 = jnp.where(rest, logits, NEG_INF)
    v2 = jnp.max(sel2, axis=1, keepdims=True)
    i2 = jnp.min(jnp.where(rest & (sel2 == v2), lane_f, big), axis=1, keepdims=True)
    e2 = jnp.exp(v2 - v1)
    w1 = p_top / (1.0 + e2)
    w2 = p_top * e2 / (1.0 + e2)
    e1 = i1 - lo_lane
    e2x = i2 - lo_lane
    a = jnp.minimum(e1, e2x)
    b = jnp.maximum(e1, e2x)
    cls = g_star * N_PAIRS + a * (7.0 - a) * 0.5 + (b - a - 1.0)
    wa = jnp.where(e1 < e2x, w1, w2)
    wb = jnp.where(e1 < e2x, w2, w1)
    h2_ref[:, D_MODEL:EXT_WIDTH] = jnp.where(lane == 0, wa, jnp.where(lane == 1, wb, 0.0))
    row = lax.broadcasted_iota(jnp.int32, (tm, LANES), 0)
    spread = jnp.where(lane == (row & (LANES - 1)), cls, 0.0)
    cls_ref[0] = jnp.sum(spread.reshape(tm // LANES, LANES, LANES), axis=1).astype(jnp.int32)


def _outproj(oa, ob, x2, wo, g, wr, *, tm):
    n = x2.shape[0]
    return pl.pallas_call(
        _outproj_kernel,
        grid=(n // tm,),
        in_specs=[
            pl.BlockSpec((tm, WIDTH_A), lambda i: (i, 0)),
            pl.BlockSpec((tm, WIDTH_B), lambda i: (i, 0)),
            pl.BlockSpec((tm, D_MODEL), lambda i: (i, 0)),
            pl.BlockSpec(wo.shape, lambda i: (0, 0)),
            pl.BlockSpec((1, D_MODEL), lambda i: (0, 0)),
            pl.BlockSpec(wr.shape, lambda i: (0, 0)),
        ],
        out_specs=[
            pl.BlockSpec((tm, D_MODEL), lambda i: (i, 0)),
            pl.BlockSpec((tm, EXT_WIDTH), lambda i: (i, 0)),
            pl.BlockSpec((1, tm // LANES, LANES), lambda i: (i, 0, 0)),
        ],
        out_shape=[
            jax.ShapeDtypeStruct((n, D_MODEL), F32),
            jax.ShapeDtypeStruct((n, EXT_WIDTH), F32),
            jax.ShapeDtypeStruct((n // tm, tm // LANES, LANES), jnp.int32),
        ],
        compiler_params=pltpu.CompilerParams(dimension_semantics=("arbitrary",), vmem_limit_bytes=VMEM_LIMIT),
        name="outproj",
    )(oa, ob, x2, wo, g, wr)


def _sort_kernel(cls_ref, pos_ref, tile_ref, ends_ref, *, tile):
    cls = cls_ref[...]
    rows = cls.shape[0]
    upper = (lax.broadcasted_iota(jnp.int32, (LANES, LANES), 0)
             <= lax.broadcasted_iota(jnp.int32, (LANES, LANES), 1)).astype(BF16)
    earlier_rows = (lax.broadcasted_iota(jnp.int32, (rows, rows), 0)
                    > lax.broadcasted_iota(jnp.int32, (rows, rows), 1)).astype(BF16)
    ones = jnp.ones((LANES, LANES), BF16)
    lane8 = lax.broadcasted_iota(jnp.int32, tile_ref.shape, 1)
    tile_start = lane8.astype(F32) * tile
    off = jnp.zeros((1, 1), F32)
    pos = jnp.zeros(cls.shape, F32)
    tile_cls = jnp.zeros(tile_ref.shape, F32)
    ends = jnp.zeros(tile_ref.shape, F32)
    for c in range(N_CLASSES):
        hit = cls == c
        hot = jnp.where(hit, 1.0, 0.0).astype(BF16)
        in_row = jnp.dot(hot, upper, preferred_element_type=F32)
        row_tot = jnp.dot(hot, ones, preferred_element_type=F32)
        before = jnp.dot(earlier_rows, row_tot.astype(BF16), preferred_element_type=F32)
        count = before[rows - 1:rows, 0:1] + row_tot[rows - 1:rows, 0:1]
        pos = pos + jnp.where(hit, in_row - 1.0 + before + off, 0.0)
        off = off + jnp.ceil(count / tile) * tile
        tile_cls = tile_cls + jnp.where(tile_start >= off, 1.0, 0.0)
        ends = jnp.where(lane8 == c, off, ends)
    pos_ref[...] = pos.astype(jnp.int32)
    tile_ref[...] = tile_cls.astype(jnp.int32)
    ends_ref[...] = ends.astype(jnp.int32)


def _sort(cls2d, *, tile):
    meta = jax.ShapeDtypeStruct((8, LANES), jnp.int32)
    return pl.pallas_call(
        functools.partial(_sort_kernel, tile=tile),
        out_shape=[jax.ShapeDtypeStruct(cls2d.shape, jnp.int32), meta, meta],
        compiler_params=pltpu.CompilerParams(vmem_limit_bytes=VMEM_LIMIT),
        name="moe_sort",
    )(cls2d)


def _row_copy(src, src_row, dst, dst_row, sem, rows=1):
    return pltpu.make_async_copy(src.at[pl.ds(src_row, rows)], dst.at[pl.ds(dst_row, rows)], sem)


def _dispatch_kernel(pos_ref, ends_ref, h2_hbm, xs_hbm, zero_scr, sem, *, ch, tile):
    g = pl.program_id(0)

    @pl.when(g == 0)
    def _zero_tails():
        zero_scr[...] = jnp.zeros_like(zero_scr)
        n_tiles = xs_hbm.shape[0] // tile
        min_used = n_tiles - N_CLASSES
        total = ends_ref[0, N_CLASSES - 1]
        for phase in ("start", "wait"):
            def zero_tile(row0):
                cp = pltpu.make_async_copy(zero_scr, xs_hbm.at[pl.ds(pl.multiple_of(row0, tile), tile)], sem)
                cp.start() if phase == "start" else cp.wait()

            for c in range(N_CLASSES):
                end = ends_ref[0, c]
                prev = ends_ref[0, c - 1] if c else 0
                pl.when(end > prev)(functools.partial(zero_tile, end - tile))
                pl.when((min_used + c) * tile >= total)(functools.partial(zero_tile, (min_used + c) * tile))

    def body(i, carry):
        for u in range(DMA_UNROLL):
            k = i * DMA_UNROLL + u
            _row_copy(h2_hbm, g * ch + k, xs_hbm, pos_ref[0, 0, k], sem).start()
        return carry

    lax.fori_loop(0, ch // DMA_UNROLL, body, 0)
    _row_copy(h2_hbm, 0, xs_hbm, 0, sem, rows=ch).wait()


def _dispatch(pos3, ends, h2ext, *, n_rows, tile):
    n_chunks, _, ch = pos3.shape
    return pl.pallas_call(
        functools.partial(_dispatch_kernel, ch=ch, tile=tile),
        grid=(n_chunks,),
        in_specs=[
            pl.BlockSpec((1, 1, ch), lambda g: (g, 0, 0), memory_space=pltpu.SMEM),
            pl.BlockSpec(ends.shape, lambda g: (0, 0), memory_space=pltpu.SMEM),
            pl.BlockSpec(memory_space=pl.ANY),
        ],
        out_specs=pl.BlockSpec(memory_space=pl.ANY),
        out_shape=jax.ShapeDtypeStruct((n_rows, EXT_WIDTH), F32),
        scratch_shapes=[pltpu.VMEM((tile, EXT_WIDTH), F32), pltpu.SemaphoreType.DMA(())],
        compiler_params=pltpu.CompilerParams(dimension_semantics=("arbitrary",), vmem_limit_bytes=VMEM_LIMIT),
        name="moe_dispatch",
    )(pos3, ends, h2ext)


def _experts_kernel(ea_ref, eb_ref, used_ref, xs_ref, wga, wua, wda, wgb, wub, wdb, ys_ref):
    del ea_ref, eb_ref

    @pl.when(pl.program_id(0) < used_ref[0])
    def _():
        x = xs_ref[:, 0:D_MODEL].astype(BF16)
        wts = xs_ref[:, D_MODEL:EXT_WIDTH]
        y = None
        for col, (wg, wu, wd) in enumerate(((wga, wua, wda), (wgb, wub, wdb))):
            gate = jnp.dot(x, wg[0], preferred_element_type=F32)
            up = jnp.dot(x, wu[0], preferred_element_type=F32)
            he = (gate / (1.0 + jnp.exp(-gate)) * up * wts[:, col:col + 1]).astype(BF16)
            t = jnp.dot(he, wd[0], preferred_element_type=F32)
            y = t if y is None else y + t
        ys_ref[...] = y

    @pl.when(pl.program_id(0) >= used_ref[0])
    def _():
        ys_ref[...] = jnp.zeros_like(ys_ref)


def _experts(ea, eb, used, xs, wg, wu, wd, *, tile):
    n_tiles = xs.shape[0] // tile
    rows = lambda j, ea, eb, used: (jnp.minimum(j, used[0] - 1), 0)
    first = lambda j, ea, eb, used: (ea[j], 0, 0)
    second = lambda j, ea, eb, used: (eb[j], 0, 0)
    up_spec = lambda m: pl.BlockSpec((1, D_MODEL, D_EXPERT), m)
    down_spec = lambda m: pl.BlockSpec((1, D_EXPERT, D_MODEL), m)
    return pl.pallas_call(
        _experts_kernel,
        grid_spec=pltpu.PrefetchScalarGridSpec(
            num_scalar_prefetch=3,
            grid=(n_tiles,),
            in_specs=[pl.BlockSpec((tile, EXT_WIDTH), rows),
                      up_spec(first), up_spec(first), down_spec(first),
                      up_spec(second), up_spec(second), down_spec(second)],
            out_specs=pl.BlockSpec((tile, D_MODEL), lambda j, ea, eb, used: (j, 0)),
        ),
        out_shape=jax.ShapeDtypeStruct((xs.shape[0], D_MODEL), F32),
        compiler_params=pltpu.CompilerParams(dimension_semantics=("arbitrary",), vmem_limit_bytes=VMEM_LIMIT),
        name="moe_experts",
    )(ea, eb, used, xs, wg, wu, wd, wg, wu, wd)


def _combine_kernel(pos_ref, x1_ref, g_ref, ys_hbm, out_ref, y_scr, sem):
    tm = x1_ref.shape[0]

    def body(i, carry):
        for u in range(DMA_UNROLL):
            k = i * DMA_UNROLL + u
            _row_copy(ys_hbm, pos_ref[0, 0, k], y_scr, k, sem).start()
        return carry

    lax.fori_loop(0, tm // DMA_UNROLL, body, 0)
    _row_copy(ys_hbm, 0, y_scr, 0, sem, rows=tm).wait()
    x2 = x1_ref[...] + y_scr[...]
    ms = jnp.mean(x2 * x2, axis=-1, keepdims=True)
    out_ref[...] = x2 * lax.rsqrt(ms + NORM_EPS) * g_ref[...]


def _combine(pos3, x1, g, ys):
    n_chunks, _, tm = pos3.shape
    return pl.pallas_call(
        _combine_kernel,
        grid=(n_chunks,),
        in_specs=[
            pl.BlockSpec((1, 1, tm), lambda i: (i, 0, 0), memory_space=pltpu.SMEM),
            pl.BlockSpec((tm, D_MODEL), lambda i: (i, 0)),
            pl.BlockSpec((1, D_MODEL), lambda i: (0, 0)),
            pl.BlockSpec(memory_space=pl.ANY),
        ],
        out_specs=pl.BlockSpec((tm, D_MODEL), lambda i: (i, 0)),
        out_shape=jax.ShapeDtypeStruct(x1.shape, F32),
        scratch_shapes=[pltpu.VMEM((tm, D_MODEL), F32), pltpu.SemaphoreType.DMA(())],
        compiler_params=pltpu.CompilerParams(dimension_semantics=("arbitrary",), vmem_limit_bytes=VMEM_LIMIT),
        name="moe_combine",
    )(pos3, x1, g, ys)


def _moe(h2ext, cls3, x1, wg, wu, wd, g, *, tile=MOE_TILE, dispatch_chunk=2048, combine_chunk=512):
    n = x1.shape[0]
    n_tiles = n // tile + N_CLASSES
    assert n_tiles <= LANES
    pos, tile_cls, ends = _sort(cls3.reshape(n // LANES, LANES), tile=tile)
    xs = _dispatch(pos.reshape(n // dispatch_chunk, 1, dispatch_chunk), ends, h2ext, n_rows=n_tiles * tile, tile=tile)
    used = ends[0, N_CLASSES - 1:N_CLASSES] // tile
    tc = tile_cls[0, :n_tiles]
    tc = jnp.where(jnp.arange(n_tiles) < used[0], tc, tc[jnp.maximum(used[0] - 1, 0)])
    pair_a = jnp.array([0, 0, 0, 1, 1, 2], jnp.int32)
    pair_b = jnp.array([1, 2, 3, 2, 3, 3], jnp.int32)
    ea = (tc // N_PAIRS) * EXPERTS_PER_GROUP + pair_a[tc % N_PAIRS]
    eb = (tc // N_PAIRS) * EXPERTS_PER_GROUP + pair_b[tc % N_PAIRS]
    ys = _experts(ea, eb, used, xs, wg, wu, wd, tile=tile)
    return _combine(pos.reshape(n // combine_chunk, 1, combine_chunk), x1, g, ys)


def _rotary_tables(seq):
    pos = jnp.arange(seq, dtype=F32)
    inv_freq = 1.0 / (ROPE_THETA ** (jnp.arange(0, ROT_DIM, 2, dtype=F32) / ROT_DIM))
    ang = pos[:, None] * inv_freq[None, :]
    cos, sin = jnp.cos(ang), jnp.sin(ang)
    zeros = jnp.zeros((seq, HEAD_DIM - ROT_DIM), F32)
    zh = jnp.zeros((seq, ROT_HALF), F32)
    cos_h = jnp.concatenate([cos, cos, jnp.ones((seq, HEAD_DIM - ROT_DIM), F32)], axis=1)
    sa_h = jnp.concatenate([-sin, zh, zeros], axis=1)
    sb_h = jnp.concatenate([zh, sin, zeros], axis=1)
    tile = lambda t: jnp.concatenate([t, t], axis=1)
    return tile(cos_h), tile(sa_h), tile(sb_h)


def kernel(x, attn_norm, w_in, b_forget, w_out, ffn_norm, w_group, w_expert, w_gate_e, w_up_e, w_down_e, final_norm):
    b, seq, d = x.shape
    assert d == D_MODEL and w_in.shape[0] == 1, "single-layer block"
    n = b * seq
    scale = HEAD_DIM ** -0.5 * LOG2E
    col_scale = jnp.ones((2 * QKV_WIDTH + HEADS_B,), F32)
    col_scale = col_scale.at[0:WIDTH_A].set(scale).at[QKV_WIDTH:QKV_WIDTH + WIDTH_B].set(scale)
    w = jnp.pad(w_in[0] * col_scale[None, :], ((0, 0), (0, LANES - HEADS_B))).astype(BF16)
    bf = jnp.pad(b_forget[0].astype(F32), (0, LANES - HEADS_B))[None, :]
    cosv, sa, sb = _rotary_tables(seq)
    x2 = x.reshape(n, d)

    qkva, qkvb, c = _inproj(x2, attn_norm[0][None, :], w, cosv, sa, sb, bf, seq=seq, tm=512)
    out_a = _dilated(qkva.reshape(b, seq, QKV_WIDTH))
    out_b = _fox(qkvb.reshape(b, seq, QKV_WIDTH), c.reshape(b, seq, LANES), tq=512)

    wr = jnp.pad(jnp.concatenate([w_group[0], w_expert[0]], axis=1).astype(F32),
                 ((0, 0), (0, LANES - N_GROUPS - N_EXPERTS)))
    x1, h2ext, cls3 = _outproj(out_a.reshape(n, WIDTH_A), out_b.reshape(n, WIDTH_B), x2,
                               w_out[0].astype(BF16), ffn_norm[0][None, :], wr, tm=512)
    out = _moe(h2ext, cls3, x1, w_gate_e[0].astype(BF16), w_up_e[0].astype(BF16), w_down_e[0].astype(BF16),
               final_norm[None, :])
    return out.reshape(b, seq, d)
```

```python
import functools

import jax
import jax.numpy as jnp
from jax import lax
from jax.experimental import pallas as pl
from jax.experimental.pallas import tpu as pltpu

F32 = jnp.float32
BF16 = jnp.bfloat16

D_MODEL = 1024
HEAD_DIM = 64
HEADS_A = 8
HEADS_B = 8
WIDTH_A = HEADS_A * HEAD_DIM
WIDTH_B = HEADS_B * HEAD_DIM
QKV_WIDTH = 3 * WIDTH_A
DILATIONS = ((128, 1), (512, 4), (2048, 16))
ROT_DIM = HEAD_DIM // 4
ROT_HALF = ROT_DIM // 2
ROPE_THETA = 500000.0
N_GROUPS = 4
EXPERTS_PER_GROUP = 4
N_EXPERTS = N_GROUPS * EXPERTS_PER_GROUP
D_EXPERT = 512
NORM_EPS = 1e-6
NEG_INF = -1e30

LANES = 128
HEAD_PAIRS = WIDTH_A // LANES
WIN = 128
MAX_DIL = 16
PERM = MAX_DIL * MAX_DIL
KV_BLOCK = 256
N_PAIRS = EXPERTS_PER_GROUP * (EXPERTS_PER_GROUP - 1) // 2
N_CLASSES = N_GROUPS * N_PAIRS
MOE_TILE = 256
EXT_WIDTH = D_MODEL + LANES
DMA_UNROLL = 8
ROUTER_OFF = N_GROUPS
VMEM_LIMIT = 56 * 1024 * 1024
LOG2E = 1.4426950408889634
BOUND_SLACK = 1.0 + 2.0 ** -7
BOUND_EPS = 2.0 ** -7
GAP_MAX = 64.0


def _split3(x):
    hi = x.astype(BF16)
    r1 = x - hi.astype(F32)
    mid = r1.astype(BF16)
    lo = (r1 - mid.astype(F32)).astype(BF16)
    return hi, mid, lo


def _inproj_kernel(x_ref, g_ref, w_ref, cos_ref, sa_ref, sb_ref, bf_ref,
                   qkva_ref, qkvb_ref, c_ref, h_scr, carry_scr, *, tiles_per_seq):
    i = pl.program_id(0)
    tm = x_ref.shape[0]
    x = x_ref[...]
    ms = jnp.mean(x * x, axis=-1, keepdims=True)
    h_scr[...] = (x * lax.rsqrt(ms + NORM_EPS) * g_ref[...]).astype(BF16)
    cosv = cos_ref[...]
    sa = sa_ref[...]
    sb = sb_ref[...]
    for g in range(6):
        p = jnp.dot(h_scr[...], w_ref[:, g * WIDTH_A:(g + 1) * WIDTH_A], preferred_element_type=F32)
        dst = qkva_ref if g < 3 else qkvb_ref
        col0 = (g % 3) * WIDTH_A
        if g < 2:
            for k in range(HEAD_PAIRS):
                t = p[:, k * LANES:(k + 1) * LANES]
                t = (t * cosv + pltpu.roll(t, LANES - ROT_HALF, 1) * sa + pltpu.roll(t, ROT_HALF, 1) * sb)
                dst[:, col0 + k * LANES:col0 + (k + 1) * LANES] = t.astype(BF16)
        else:
            dst[:, col0:col0 + WIDTH_A] = p.astype(BF16)
    z = jnp.dot(h_scr[...], w_ref[:, 2 * QKV_WIDTH:2 * QKV_WIDTH + LANES], preferred_element_type=F32) + bf_ref[...]
    lf = jnp.minimum(z, 0.0) - jnp.log1p(jnp.exp(-jnp.abs(z)))
    lane = lax.broadcasted_iota(jnp.int32, (tm, LANES), 1)
    lf = jnp.where(lane < HEADS_B, lf * LOG2E, 0.0)
    hi, mid, lo = _split3(lf)
    tri = (lax.broadcasted_iota(jnp.int32, (tm, tm), 0) >= lax.broadcasted_iota(jnp.int32, (tm, tm), 1)).astype(BF16)
    cs = (jnp.dot(tri, hi, preferred_element_type=F32) + jnp.dot(tri, mid, preferred_element_type=F32)
          + jnp.dot(tri, lo, preferred_element_type=F32))

    @pl.when(i % tiles_per_seq == 0)
    def _():
        carry_scr[...] = jnp.zeros_like(carry_scr)

    c = cs + carry_scr[0:1, :]
    c_ref[...] = c
    carry_scr[...] = jnp.broadcast_to(c[tm - 1:tm, :], carry_scr.shape)


def _inproj(x2, g, w, cosv, sa, sb, bf, *, seq, tm):
    n = x2.shape[0]
    kern = functools.partial(_inproj_kernel, tiles_per_seq=seq // tm)
    tps = seq // tm
    return pl.pallas_call(
        kern,
        grid=(n // tm,),
        in_specs=[
            pl.BlockSpec((tm, D_MODEL), lambda i: (i, 0)),
            pl.BlockSpec((1, D_MODEL), lambda i: (0, 0)),
            pl.BlockSpec(w.shape, lambda i: (0, 0)),
            pl.BlockSpec((tm, LANES), lambda i: (i % tps, 0)),
            pl.BlockSpec((tm, LANES), lambda i: (i % tps, 0)),
            pl.BlockSpec((tm, LANES), lambda i: (i % tps, 0)),
            pl.BlockSpec((1, LANES), lambda i: (0, 0)),
        ],
        out_specs=[
            pl.BlockSpec((tm, QKV_WIDTH), lambda i: (i, 0)),
            pl.BlockSpec((tm, QKV_WIDTH), lambda i: (i, 0)),
            pl.BlockSpec((tm, LANES), lambda i: (i, 0)),
        ],
        out_shape=[
            jax.ShapeDtypeStruct((n, QKV_WIDTH), BF16),
            jax.ShapeDtypeStruct((n, QKV_WIDTH), BF16),
            jax.ShapeDtypeStruct((n, LANES), F32),
        ],
        scratch_shapes=[pltpu.VMEM((tm, D_MODEL), BF16), pltpu.VMEM((8, LANES), F32)],
        compiler_params=pltpu.CompilerParams(dimension_semantics=("arbitrary",), vmem_limit_bytes=VMEM_LIMIT),
        name="inproj",
    )(x2, g, w, cosv, sa, sb, bf)


def _head_column(c_all, lane, head):
    return jnp.sum(jnp.where(lane == head, c_all, 0.0), axis=1, keepdims=True)


def _own_lanes(lane, hl):
    return (lane < HEAD_DIM) if hl == 0 else (lane >= HEAD_DIM)


def _own_sum(vals, lane, hl):
    return jnp.sum(jnp.where(_own_lanes(lane, hl), vals, 0.0), axis=1, keepdims=True)


def _norm_bound(qsq, ksq_max):
    return jnp.sqrt(qsq * ksq_max) * BOUND_SLACK + BOUND_EPS


def _fox_kernel(q_ref, k_ref, v_ref, cq_ref, ck_ref, o_ref,
                k0_scr, k1_scr, v0_scr, v1_scr, ksq_scr, acc0_scr, acc1_scr, s_scr, *, tq, seq, chunk):
    hp = pl.program_id(1)
    qi = pl.program_id(2)
    k_scr = (k0_scr, k1_scr)
    v_scr = (v0_scr, v1_scr)
    acc_scr = (acc0_scr, acc1_scr)

    def augmented(vals, cterm, lane, hl, key_side):
        hi, mid, lo = (t.astype(F32) for t in _split3(cterm))
        base = HEAD_DIM if hl == 0 else 0
        ones = (lane >= base + (0 if key_side else 3)) & (lane < base + (3 if key_side else 6))
        o = 3 if key_side else 0
        sgn = -1.0 if key_side else 1.0
        aug = jnp.where(lane == base + o, sgn * hi,
                        jnp.where(lane == base + o + 1, sgn * mid,
                                  jnp.where(lane == base + o + 2, sgn * lo,
                                            jnp.where(ones, 1.0, 0.0))))
        return jnp.where(_own_lanes(lane, hl), vals, aug).astype(BF16)

    @pl.when(qi == 0)
    def _build_keys():
        lane = lax.broadcasted_iota(jnp.int32, (chunk, LANES), 1)

        def body(ci, ksq):
            r0 = pl.multiple_of(ci * chunk, chunk)
            kc = k_ref[0, pl.ds(r0, chunk), :].astype(F32)
            vc = v_ref[0, pl.ds(r0, chunk), :].astype(F32)
            cc = ck_ref[0, pl.ds(r0, chunk), :]
            new = []
            for hl in range(2):
                k_scr[hl][pl.ds(r0, chunk), :] = augmented(kc, _head_column(cc, lane, 2 * hp + hl), lane, hl, True)
                one_lane = HEAD_DIM if hl == 0 else 0
                v_scr[hl][pl.ds(r0, chunk), :] = jnp.where(
                    _own_lanes(lane, hl), vc, jnp.where(lane == one_lane, 1.0, 0.0)).astype(BF16)
                new.append(jnp.maximum(ksq[hl], jnp.max(_own_sum(kc * kc, lane, hl), axis=0, keepdims=True)))
            return tuple(new)

        zero = jnp.zeros((1, 1), F32)
        ksq0, ksq1 = lax.fori_loop(0, seq // chunk, body, (zero, zero))
        lane8 = lax.broadcasted_iota(jnp.int32, ksq_scr.shape, 1)
        ksq_scr[...] = jnp.where(lane8 < HEAD_DIM, ksq0, ksq1)

    lane_q = lax.broadcasted_iota(jnp.int32, (tq, LANES), 1)
    q2 = q_ref[0].astype(F32)
    cq = cq_ref[0]
    kd = k_ref[0, pl.ds(pl.multiple_of(qi * tq, tq), tq), :].astype(F32)
    ksq_row = ksq_scr[0:1, :]
    qh = []
    gap = None
    for hl in range(2):
        bound = _norm_bound(_own_sum(q2 * q2, lane_q, hl), ksq_row[:, hl * HEAD_DIM:hl * HEAD_DIM + 1])
        diag = _own_sum(q2 * kd, lane_q, hl)
        g = jnp.max(bound - diag)
        gap = g if gap is None else jnp.maximum(gap, g)
        qh.append(augmented(q2, _head_column(cq, lane_q, 2 * hp + hl) - bound, lane_q, hl, False))
    n_diag = tq // KV_BLOCK
    n_full = qi * n_diag
    row_minus_col = (lax.broadcasted_iota(jnp.int32, (tq, KV_BLOCK), 0)
                     - lax.broadcasted_iota(jnp.int32, (tq, KV_BLOCK), 1))

    def causal_mask(s, d):
        return jnp.where(row_minus_col >= d * KV_BLOCK, s, NEG_INF)

    def logits(hl, j):
        r0 = pl.multiple_of(j * KV_BLOCK, KV_BLOCK)
        return lax.dot_general(qh[hl], k_scr[hl][pl.ds(r0, KV_BLOCK), :], (((1,), (1,)), ((), ())),
                               preferred_element_type=F32)

    def values(hl, j):
        return v_scr[hl][pl.ds(pl.multiple_of(j * KV_BLOCK, KV_BLOCK), KV_BLOCK), :]

    def finish(a0, a1):
        l0 = a0[:, HEAD_DIM:HEAD_DIM + 1]
        l1 = a1[:, 0:1]
        o_ref[0] = jnp.where(lane_q < HEAD_DIM, a0 / l0, a1 / l1).astype(o_ref.dtype)

    @pl.when(gap <= GAP_MAX)
    def _fast():
        acc0_scr[...] = jnp.zeros_like(acc0_scr)
        acc1_scr[...] = jnp.zeros_like(acc1_scr)

        def produce(j, slot):
            for hl in range(2):
                s_scr[slot, hl] = logits(hl, j)

        def consume(j, slot, diag):
            for hl in range(2):
                s = s_scr[slot, hl]
                if diag is not None:
                    s = causal_mask(s, diag)
                acc_scr[hl][...] += jnp.dot(jnp.exp2(s).astype(BF16), values(hl, j), preferred_element_type=F32)

        produce(0, 0)

        def body(j, carry):
            slot = j & 1
            consume(j, slot, None)
            produce(j + 1, 1 - slot)
            return carry

        lax.fori_loop(0, n_full, body, 0)
        for d in range(n_diag):
            j = n_full + d
            consume(j, j & 1, d)
            if d + 1 < n_diag:
                produce(j + 1, (j + 1) & 1)
        finish(acc0_scr[...], acc1_scr[...])

    @pl.when(gap > GAP_MAX)
    def _general():
        def step(j, carry, diag):
            new = []
            for hl in range(2):
                m, a = carry[2 * hl:2 * hl + 2]
                s = logits(hl, j)
                if diag is not None:
                    s = causal_mask(s, diag)
                mn = jnp.maximum(m, jnp.max(s, axis=1, keepdims=True))
                a = jnp.exp2(m - mn) * a + jnp.dot(jnp.exp2(s - mn).astype(BF16), values(hl, j),
                                                  preferred_element_type=F32)
                new += [mn, a]
            return tuple(new)

        carry = (jnp.full((tq, 1), NEG_INF, F32), jnp.zeros((tq, LANES), F32)) * 2
        carry = lax.fori_loop(0, n_full, lambda j, c: step(j, c, None), carry)
        for d in range(n_diag):
            carry = step(n_full + d, carry, d)
        finish(carry[1], carry[3])


def _fox(qkvb, c, *, tq, chunk=512):
    b, seq, _ = qkvb.shape
    kern = functools.partial(_fox_kernel, tq=tq, seq=seq, chunk=chunk)
    return pl.pallas_call(
        kern,
        grid=(b, HEAD_PAIRS, seq // tq),
        in_specs=[
            pl.BlockSpec((1, tq, LANES), lambda bi, hp, qi: (bi, qi, hp)),
            pl.BlockSpec((1, seq, LANES), lambda bi, hp, qi: (bi, 0, HEAD_PAIRS + hp)),
            pl.BlockSpec((1, seq, LANES), lambda bi, hp, qi: (bi, 0, 2 * HEAD_PAIRS + hp)),
            pl.BlockSpec((1, tq, LANES), lambda bi, hp, qi: (bi, qi, 0)),
            pl.BlockSpec((1, seq, LANES), lambda bi, hp, qi: (bi, 0, 0)),
        ],
        out_specs=pl.BlockSpec((1, tq, LANES), lambda bi, hp, qi: (bi, qi, hp)),
        out_shape=jax.ShapeDtypeStruct((b, seq, WIDTH_B), BF16),
        scratch_shapes=[pltpu.VMEM((seq, LANES), BF16), pltpu.VMEM((seq, LANES), BF16),
                        pltpu.VMEM((seq, LANES), BF16), pltpu.VMEM((seq, LANES), BF16),
                        pltpu.VMEM((8, LANES), F32),
                        pltpu.VMEM((tq, LANES), F32), pltpu.VMEM((tq, LANES), F32),
                        pltpu.VMEM((2, 2, tq, KV_BLOCK), F32)],
        compiler_params=pltpu.CompilerParams(
            dimension_semantics=("arbitrary", "arbitrary", "arbitrary"), vmem_limit_bytes=VMEM_LIMIT),
        name="fox",
    )(qkvb, qkvb, qkvb, c, c)


def _dilated_general(q_ref, k_ref, v_ref, o_ref, q16, k16, v16, m_s, l_s, acc_s, on_s, lse_s, *, seq):
    sub = seq // MAX_DIL
    own0 = lax.broadcasted_iota(jnp.int32, (WIN, LANES), 1) < HEAD_DIM
    ri = lax.broadcasted_iota(jnp.int32, (PERM, PERM), 0)
    ci = lax.broadcasted_iota(jnp.int32, (PERM, PERM), 1)
    perm = (((ri >> 4) == (ci & 15)) & ((ri & 15) == (ci >> 4))).astype(BF16)

    qa = lax.broadcasted_iota(jnp.int32, (WIN, 2 * WIN), 0)
    kb_i = lax.broadcasted_iota(jnp.int32, (WIN, 2 * WIN), 1)
    cur = kb_i >= WIN
    kin = kb_i & (WIN - 1)
    dist_nat = qa - kin + jnp.where(cur, 0, WIN)
    band_nat = (dist_nat >= 0) & (dist_nat <= WIN)
    sq = ((qa & 31) << 2) + (qa >> 5)
    sk = ((kin & 31) << 2) + (kin >> 5)
    dist_4 = sq - sk + jnp.where(cur, 0, WIN)
    band_4 = (dist_4 >= 0) & (dist_4 <= WIN)

    def attend(qb, kb, vb, mask, state):
        qf = qb.astype(F32)
        new = []
        for hl in range(2):
            m, l, a = state[3 * hl:3 * hl + 3]
            own = own0 if hl == 0 else jnp.logical_not(own0)
            qh = jnp.where(own, qf, 0.0).astype(BF16)
            s = lax.dot_general(qh, kb, (((1,), (1,)), ((), ())), preferred_element_type=F32)
            s = jnp.where(mask, s, NEG_INF)
            mn = jnp.maximum(m, jnp.max(s, axis=1, keepdims=True))
            alpha = jnp.exp2(m - mn)
            p = jnp.exp2(s - mn)
            l = alpha * l + jnp.sum(p, axis=1, keepdims=True)
            a = alpha * a + jnp.dot(p.astype(BF16), vb, preferred_element_type=F32)
            new += [mn, l, a]
        return new

    def pack_state(st):
        m0, l0, a0, m1, l1, a1 = st
        return (jnp.where(own0, m0, m1), jnp.where(own0, l0, l1), jnp.where(own0, a0, a1))

    def deint(blk, carry):
        r0 = pl.multiple_of(blk * PERM, PERM)
        j0 = pl.multiple_of(blk * MAX_DIL, MAX_DIL)
        for src, dst in ((q_ref, q16), (k_ref, k16), (v_ref, v16)):
            y = jnp.dot(perm, src[0, pl.ds(r0, PERM), :], preferred_element_type=F32).astype(BF16)
            for r in range(MAX_DIL):
                dst[r, pl.ds(j0, MAX_DIL), :] = y[r * MAX_DIL:(r + 1) * MAX_DIL, :]
        return carry

    lax.fori_loop(0, seq // PERM, deint, 0)

    fresh = [jnp.full((WIN, 1), NEG_INF, F32), jnp.zeros((WIN, 1), F32), jnp.zeros((WIN, LANES), F32)] * 2
    nb16 = sub // WIN

    def d16_body(t, carry):
        r = t // nb16
        n = t % nb16
        c0 = pl.multiple_of(n * WIN, WIN)
        p0 = pl.multiple_of(jnp.maximum(n - 1, 0) * WIN, WIN)
        qb = q16[r, pl.ds(c0, WIN), :]
        kb = jnp.concatenate([k16[r, pl.ds(p0, WIN), :], k16[r, pl.ds(c0, WIN), :]], axis=0)
        vb = jnp.concatenate([v16[r, pl.ds(p0, WIN), :], v16[r, pl.ds(c0, WIN), :]], axis=0)
        mask = band_nat & (cur | (n > 0))
        mm, ll, aa = pack_state(attend(qb, kb, vb, mask, fresh))
        row = pl.multiple_of(r * sub + c0, WIN)
        m_s[pl.ds(row, WIN), :] = mm
        l_s[pl.ds(row, WIN), :] = ll
        acc_s[pl.ds(row, WIN), :] = aa
        return carry

    lax.fori_loop(0, MAX_DIL * nb16, d16_body, 0)

    nb4 = (seq // 4) // WIN
    ch = WIN // 4

    def d4_body(t, carry):
        r4 = t // nb4
        n = t % nb4
        c0 = pl.multiple_of(n * ch, ch)
        p0 = pl.multiple_of(jnp.maximum(n - 1, 0) * ch, ch)

        def gather(ref, j0):
            return [ref[r4 + 4 * q, pl.ds(j0, ch), :] for q in range(4)]

        qb = jnp.concatenate(gather(q16, c0), axis=0)
        kb = jnp.concatenate(gather(k16, p0) + gather(k16, c0), axis=0)
        vb = jnp.concatenate(gather(v16, p0) + gather(v16, c0), axis=0)
        rows = [pl.multiple_of((r4 + 4 * q) * sub + c0, ch) for q in range(4)]
        mm = jnp.concatenate([m_s[pl.ds(rw, ch), :] for rw in rows], axis=0)
        ll = jnp.concatenate([l_s[pl.ds(rw, ch), :] for rw in rows], axis=0)
        aa = jnp.concatenate([acc_s[pl.ds(rw, ch), :] for rw in rows], axis=0)
        state = [mm[:, 0:1], ll[:, 0:1], aa, mm[:, HEAD_DIM:HEAD_DIM + 1], ll[:, HEAD_DIM:HEAD_DIM + 1], aa]
        mask = band_4 & (cur | (n > 0))
        mm, ll, aa = pack_state(attend(qb, kb, vb, mask, state))
        for q, rw in enumerate(rows):
            m_s[pl.ds(rw, ch), :] = mm[q * ch:(q + 1) * ch, :]
            l_s[pl.ds(rw, ch), :] = ll[q * ch:(q + 1) * ch, :]
            acc_s[pl.ds(rw, ch), :] = aa[q * ch:(q + 1) * ch, :]
        return carry

    lax.fori_loop(0, 4 * nb4, d4_body, 0)

    def renat(blk, carry):
        j0 = pl.multiple_of(blk * MAX_DIL, MAX_DIL)
        r0 = pl.multiple_of(blk * PERM, PERM)
        rows = [pl.multiple_of(r * sub + j0, MAX_DIL) for r in range(MAX_DIL)]
        mm = jnp.concatenate([m_s[pl.ds(rw, MAX_DIL), :] for rw in rows], axis=0)
        ll = jnp.concatenate([l_s[pl.ds(rw, MAX_DIL), :] for rw in rows], axis=0)
        aa = jnp.concatenate([acc_s[pl.ds(rw, MAX_DIL), :] for rw in rows], axis=0)
        o = (aa / ll).astype(BF16)
        hi, mid, lo = _split3(mm + jnp.log2(ll))
        on_s[pl.ds(r0, PERM), :] = jnp.dot(perm, o, preferred_element_type=F32)
        lse_s[pl.ds(r0, PERM), :] = (jnp.dot(perm, hi, preferred_element_type=F32)
                                     + jnp.dot(perm, mid, preferred_element_type=F32)
                                     + jnp.dot(perm, lo, preferred_element_type=F32))
        return carry

    lax.fori_loop(0, seq // PERM, renat, 0)

    def d1_body(n, carry):
        c0 = pl.multiple_of(n * WIN, WIN)
        p0 = pl.multiple_of(jnp.maximum(n - 1, 0) * WIN, WIN)
        qb = q_ref[0, pl.ds(c0, WIN), :]
        kb = jnp.concatenate([k_ref[0, pl.ds(p0, WIN), :], k_ref[0, pl.ds(c0, WIN), :]], axis=0)
        vb = jnp.concatenate([v_ref[0, pl.ds(p0, WIN), :], v_ref[0, pl.ds(c0, WIN), :]], axis=0)
        lse = lse_s[pl.ds(c0, WIN), :]
        o = on_s[pl.ds(c0, WIN), :]
        one = jnp.ones((WIN, 1), F32)
        state = [lse[:, 0:1], one, o, lse[:, HEAD_DIM:HEAD_DIM + 1], one, o]
        mask = band_nat & (cur | (n > 0))
        m0, l0, a0, m1, l1, a1 = attend(qb, kb, vb, mask, state)
        o_ref[0, pl.ds(c0, WIN), :] = jnp.where(own0, a0 / l0, a1 / l1).astype(o_ref.dtype)
        return carry

    lax.fori_loop(0, seq // WIN, d1_body, 0)


def _dilated_fast(o_ref, qn, kn, vn, qr, kr, vr, fr, fn, s_scr, mask_scr, *, seq):
    sub = seq // MAX_DIL
    own0 = lax.broadcasted_iota(jnp.int32, (WIN, LANES), 1) < HEAD_DIM
    ri = lax.broadcasted_iota(jnp.int32, (PERM, PERM), 0)
    ci = lax.broadcasted_iota(jnp.int32, (PERM, PERM), 1)
    perm = (((ri >> 4) == (ci & 15)) & ((ri & 15) == (ci >> 4))).astype(BF16)

    qa = lax.broadcasted_iota(jnp.int32, (WIN, 2 * WIN), 0)
    kb_i = lax.broadcasted_iota(jnp.int32, (WIN, 2 * WIN), 1)
    cur = kb_i >= WIN
    kin = kb_i & (WIN - 1)
    dist_nat = qa - kin + jnp.where(cur, 0, WIN)
    band_nat = (dist_nat >= 0) & (dist_nat <= WIN)
    sq = ((qa & 31) << 2) + (qa >> 5)
    sk = ((kin & 31) << 2) + (kin >> 5)
    dist_4 = sq - sk + jnp.where(cur, 0, WIN)
    band_4 = (dist_4 >= 0) & (dist_4 <= WIN)
    for i, band in enumerate((band_nat, band_4)):
        mask_scr[2 * i] = jnp.where(band & cur, 1.0, 0.0).astype(BF16)
        mask_scr[2 * i + 1] = jnp.where(band, 1.0, 0.0).astype(BF16)

    def deint(blk, carry):
        r0 = pl.multiple_of(blk * PERM, PERM)
        j0 = pl.multiple_of(blk * MAX_DIL, MAX_DIL)
        for srcs, dsts in ((qn, qr), (kn, kr), (vn, vr)):
            for src, dst in zip(srcs, dsts):
                y = jnp.dot(perm, src[pl.ds(r0, PERM), :], preferred_element_type=F32).astype(BF16)
                for r in range(MAX_DIL):
                    dst[r, pl.ds(j0, MAX_DIL), :] = y[r * MAX_DIL:(r + 1) * MAX_DIL, :]
        return carry

    lax.fori_loop(0, seq // PERM, deint, 0)

    def run_branch(nblocks, block_in_seq, fetch, mask_base, sink):
        def produce(t, pair, u):
            for hl in range(2):
                qb, kb = fetch(t, hl, True)
                s_scr[pair, u, hl] = lax.dot_general(qb, kb, (((1,), (1,)), ((), ())), preferred_element_type=F32)

        def consume(t, pair, u):
            mk = mask_scr[mask_base + jnp.minimum(block_in_seq(t), 1)]
            pv = []
            for hl in range(2):
                p = jnp.exp2(s_scr[pair, u, hl]).astype(BF16) * mk
                pv.append(jnp.dot(p, fetch(t, hl, False), preferred_element_type=F32))
            sink(t, pv)

        produce(0, 0, 0)
        produce(1, 0, 1)

        def body(i, carry):
            pair = i & 1
            for u in range(2):
                consume(2 * i + u, pair, u)
            for u in range(2):
                produce(jnp.minimum(2 * i + 2 + u, nblocks - 1), 1 - pair, u)
            return carry

        lax.fori_loop(0, nblocks // 2, body, 0)

    def prev_cur(ref_block, n, size):
        c0 = pl.multiple_of(n * size, size)
        p0 = pl.multiple_of(jnp.maximum(n - 1, 0) * size, size)
        return ref_block(p0) + ref_block(c0)

    nb16 = sub // WIN

    def fetch16(t, hl, qk):
        r, n = t // nb16, t % nb16
        if qk:
            return (qr[hl][r, pl.ds(pl.multiple_of(n * WIN, WIN), WIN), :],
                    jnp.concatenate(prev_cur(lambda s: [kr[hl][r, pl.ds(s, WIN), :]], n, WIN), axis=0))
        return jnp.concatenate(prev_cur(lambda s: [vr[hl][r, pl.ds(s, WIN), :]], n, WIN), axis=0)

    def sink16(t, pv):
        row = pl.multiple_of(t * WIN, WIN)
        for hl in range(2):
            fr[hl][pl.ds(row, WIN), :] = pv[hl]

    run_branch(MAX_DIL * nb16, lambda t: t % nb16, fetch16, 0, sink16)

    nb4 = (seq // 4) // WIN
    ch = WIN // 4

    def fetch4(t, hl, qk):
        r4, n = t // nb4, t % nb4
        gather = lambda ref: (lambda s: [ref[r4 + 4 * q, pl.ds(s, ch), :] for q in range(4)])
        if qk:
            return (jnp.concatenate(gather(qr[hl])(pl.multiple_of(n * ch, ch)), axis=0),
                    jnp.concatenate(prev_cur(gather(kr[hl]), n, ch), axis=0))
        return jnp.concatenate(prev_cur(gather(vr[hl]), n, ch), axis=0)

    def sink4(t, pv):
        r4, n = t // nb4, t % nb4
        for q in range(4):
            row = pl.multiple_of((r4 + 4 * q) * sub + n * ch, ch)
            for hl in range(2):
                fr[hl][pl.ds(row, ch), :] += pv[hl][q * ch:(q + 1) * ch, :]

    run_branch(4 * nb4, lambda t: t % nb4, fetch4, 2, sink4)

    def renat(blk, carry):
        j0 = pl.multiple_of(blk * MAX_DIL, MAX_DIL)
        r0 = pl.multiple_of(blk * PERM, PERM)
        for hl in range(2):
            a = jnp.concatenate([fr[hl][pl.ds(pl.multiple_of(r * sub + j0, MAX_DIL), MAX_DIL), :]
                                 for r in range(MAX_DIL)], axis=0)
            hi, mid, lo = _split3(a)
            fn[hl][pl.ds(r0, PERM), :] = (jnp.dot(perm, hi, preferred_element_type=F32)
                                          + jnp.dot(perm, mid, preferred_element_type=F32)
                                          + jnp.dot(perm, lo, preferred_element_type=F32))
        return carry

    lax.fori_loop(0, seq // PERM, renat, 0)

    def fetch1(t, hl, qk):
        if qk:
            return (qn[hl][pl.ds(pl.multiple_of(t * WIN, WIN), WIN), :],
                    jnp.concatenate(prev_cur(lambda s: [kn[hl][pl.ds(s, WIN), :]], t, WIN), axis=0))
        return jnp.concatenate(prev_cur(lambda s: [vn[hl][pl.ds(s, WIN), :]], t, WIN), axis=0)

    def sink1(t, pv):
        row = pl.multiple_of(t * WIN, WIN)
        t0 = fn[0][pl.ds(row, WIN), :] + pv[0]
        t1 = fn[1][pl.ds(row, WIN), :] + pv[1]
        o_ref[0, pl.ds(row, WIN), :] = jnp.where(own0, t0 / t0[:, HEAD_DIM:HEAD_DIM + 1],
                                                 t1 / t1[:, 0:1]).astype(o_ref.dtype)

    run_branch(seq // WIN, lambda t: t, fetch1, 0, sink1)


def _dilated_kernel(q_ref, k_ref, v_ref, o_ref, qn0, qn1, kn0, kn1, vn0, vn1, qr0, qr1, kr0, kr1, vr0, vr1,
                    f0, f1, f2, f3, f4, s_scr, mask_scr, *, seq, chunk):
    lane = lax.broadcasted_iota(jnp.int32, (chunk, LANES), 1)
    qn, kn, vn = (qn0, qn1), (kn0, kn1), (vn0, vn1)

    def ksq_body(ci, ksq):
        kc = k_ref[0, pl.ds(pl.multiple_of(ci * chunk, chunk), chunk), :].astype(F32)
        return tuple(jnp.maximum(ksq[hl], jnp.max(_own_sum(kc * kc, lane, hl), axis=0, keepdims=True))
                     for hl in range(2))

    zero = jnp.zeros((1, 1), F32)
    ksq = lax.fori_loop(0, seq // chunk, ksq_body, (zero, zero))

    def build(ci, gap):
        rows = pl.ds(pl.multiple_of(ci * chunk, chunk), chunk)
        qc = q_ref[0, rows, :].astype(F32)
        kc = k_ref[0, rows, :].astype(F32)
        vc = v_ref[0, rows, :].astype(F32)
        for hl in range(2):
            own = _own_lanes(lane, hl)
            base = HEAD_DIM if hl == 0 else 0
            bound = _norm_bound(_own_sum(qc * qc, lane, hl), ksq[hl])
            gap = jnp.maximum(gap, jnp.max(bound - _own_sum(qc * kc, lane, hl), axis=0, keepdims=True))
            hi, mid, lo = (t.astype(F32) for t in _split3(-bound))
            qaug = jnp.where(lane == base, hi, jnp.where(lane == base + 1, mid, jnp.where(lane == base + 2, lo, 0.0)))
            qn[hl][rows, :] = jnp.where(own, qc, qaug).astype(BF16)
            kn[hl][rows, :] = jnp.where(own, kc, jnp.where((lane >= base) & (lane < base + 3), 1.0, 0.0)).astype(BF16)
            vn[hl][rows, :] = jnp.where(own, vc, jnp.where(lane == base, 1.0, 0.0)).astype(BF16)
        return gap

    gap = jnp.max(lax.fori_loop(0, seq // chunk, build, jnp.full((1, 1), NEG_INF, F32)))

    @pl.when(gap <= GAP_MAX)
    def _fast():
        _dilated_fast(o_ref, qn, kn, vn, (qr0, qr1), (kr0, kr1), (vr0, vr1), (f0, f1), (f2, f3),
                      s_scr, mask_scr, seq=seq)

    @pl.when(gap > GAP_MAX)
    def _general():
        _dilated_general(q_ref, k_ref, v_ref, o_ref, qr0, kr0, vr0, f0, f1, f2, f3, f4, seq=seq)


def _dilated(qkva, *, chunk=512):
    b, seq, _ = qkva.shape
    sub = seq // MAX_DIL
    kern = functools.partial(_dilated_kernel, seq=seq, chunk=chunk)
    return pl.pallas_call(
        kern,
        grid=(b, HEAD_PAIRS),
        in_specs=[
            pl.BlockSpec((1, seq, LANES), lambda bi, hp: (bi, 0, hp)),
            pl.BlockSpec((1, seq, LANES), lambda bi, hp: (bi, 0, HEAD_PAIRS + hp)),
            pl.BlockSpec((1, seq, LANES), lambda bi, hp: (bi, 0, 2 * HEAD_PAIRS + hp)),
        ],
        out_specs=pl.BlockSpec((1, seq, LANES), lambda bi, hp: (bi, 0, hp)),
        out_shape=jax.ShapeDtypeStruct((b, seq, WIDTH_A), BF16),
        scratch_shapes=(
            [pltpu.VMEM((seq, LANES), BF16)] * 6
            + [pltpu.VMEM((MAX_DIL, sub, LANES), BF16)] * 6
            + [pltpu.VMEM((seq, LANES), F32)] * 5
            + [pltpu.VMEM((2, 2, 2, WIN, 2 * WIN), F32), pltpu.VMEM((4, WIN, 2 * WIN), BF16)]
        ),
        compiler_params=pltpu.CompilerParams(
            dimension_semantics=("arbitrary", "arbitrary"), vmem_limit_bytes=VMEM_LIMIT),
        name="dilated",
    )(qkva, qkva, qkva)


def _outproj_kernel(oa_ref, ob_ref, x_ref, wo_ref, g_ref, wr_ref, x1_ref, h2_ref, cls_ref):
    tm = x_ref.shape[0]
    x1 = (x_ref[...]
          + jnp.dot(oa_ref[...], wo_ref[0:WIDTH_A, :], preferred_element_type=F32)
          + jnp.dot(ob_ref[...], wo_ref[WIDTH_A:WIDTH_A + WIDTH_B, :], preferred_element_type=F32))
    x1_ref[...] = x1
    ms = jnp.mean(x1 * x1, axis=-1, keepdims=True)
    h2 = x1 * lax.rsqrt(ms + NORM_EPS) * g_ref[...]
    h2_ref[:, 0:D_MODEL] = h2
    logits = jnp.dot(h2, wr_ref[...], preferred_element_type=F32, precision=lax.Precision.HIGHEST)
    lane = lax.broadcasted_iota(jnp.int32, (tm, LANES), 1)
    lane_f = lane.astype(F32)
    big = float(LANES)
    gmask = lane < N_GROUPS
    gl = jnp.where(gmask, logits, NEG_INF)
    gmax = jnp.max(gl, axis=1, keepdims=True)
    gsum = jnp.sum(jnp.where(gmask, jnp.exp(gl - gmax), 0.0), axis=1, keepdims=True)
    p_top = 1.0 / gsum
    g_star = jnp.min(jnp.where(gmask & (gl == gmax), lane_f, big), axis=1, keepdims=True)
    lo_lane = ROUTER_OFF + EXPERTS_PER_GROUP * g_star
    emask = (lane_f >= lo_lane) & (lane_f < lo_lane + EXPERTS_PER_GROUP)
    sel = jnp.where(emask, logits, NEG_INF)
    v1 = jnp.max(sel, axis=1, keepdims=True)
    i1 = jnp.min(jnp.where(emask & (sel == v1), lane_f, big), axis=1, keepdims=True)
    rest = emask & (lane_f != i1)
    sel2 = jnp.where(rest, logits, NEG_INF)
    v2 = jnp.max(sel2, axis=1, keepdims=True)
    i2 = jnp.min(jnp.where(rest & (sel2 == v2), lane_f, big), axis=1, keepdims=True)
    e2 = jnp.exp(v2 - v1)
    w1 = p_top / (1.0 + e2)
    w2 = p_top * e2 / (1.0 + e2)
    e1 = i1 - lo_lane
    e2x = i2 - lo_lane
    a = jnp.minimum(e1, e2x)
    b = jnp.maximum(e1, e2x)
    cls = g_star * N_PAIRS + a * (7.0 - a) * 0.5 + (b - a - 1.0)
    wa = jnp.where(e1 < e2x, w1, w2)
    wb = jnp.where(e1 < e2x, w2, w1)
    h2_ref[:, D_MODEL:EXT_WIDTH] = jnp.where(lane == 0, wa, jnp.where(lane == 1, wb, 0.0))
    row = lax.broadcasted_iota(jnp.int32, (tm, LANES), 0)
    spread = jnp.where(lane == (row & (LANES - 1)), cls, 0.0)
    cls_ref[0] = jnp.sum(spread.reshape(tm // LANES, LANES, LANES), axis=1).astype(jnp.int32)


def _outproj(oa, ob, x2, wo, g, wr, *, tm):
    n = x2.shape[0]
    return pl.pallas_call(
        _outproj_kernel,
        grid=(n // tm,),
        in_specs=[
            pl.BlockSpec((tm, WIDTH_A), lambda i: (i, 0)),
            pl.BlockSpec((tm, WIDTH_B), lambda i: (i, 0)),
            pl.BlockSpec((tm, D_MODEL), lambda i: (i, 0)),
            pl.BlockSpec(wo.shape, lambda i: (0, 0)),
            pl.BlockSpec((1, D_MODEL), lambda i: (0, 0)),
            pl.BlockSpec(wr.shape, lambda i: (0, 0)),
        ],
        out_specs=[
            pl.BlockSpec((tm, D_MODEL), lambda i: (i, 0)),
            pl.BlockSpec((tm, EXT_WIDTH), lambda i: (i, 0)),
            pl.BlockSpec((1, tm // LANES, LANES), lambda i: (i, 0, 0)),
        ],
        out_shape=[
            jax.ShapeDtypeStruct((n, D_MODEL), F32),
            jax.ShapeDtypeStruct((n, EXT_WIDTH), F32),
            jax.ShapeDtypeStruct((n // tm, tm // LANES, LANES), jnp.int32),
        ],
        compiler_params=pltpu.CompilerParams(dimension_semantics=("arbitrary",), vmem_limit_bytes=VMEM_LIMIT),
        name="outproj",
    )(oa, ob, x2, wo, g, wr)


def _sort_kernel(cls_ref, pos_ref, tile_ref, ends_ref, *, tile):
    cls = cls_ref[...]
    rows = cls.shape[0]
    upper = (lax.broadcasted_iota(jnp.int32, (LANES, LANES), 0)
             <= lax.broadcasted_iota(jnp.int32, (LANES, LANES), 1)).astype(BF16)
    earlier_rows = (lax.broadcasted_iota(jnp.int32, (rows, rows), 0)
                    > lax.broadcasted_iota(jnp.int32, (rows, rows), 1)).astype(BF16)
    ones = jnp.ones((LANES, LANES), BF16)
    lane8 = lax.broadcasted_iota(jnp.int32, tile_ref.shape, 1)
    tile_start = lane8.astype(F32) * tile
    off = jnp.zeros((1, 1), F32)
    pos = jnp.zeros(cls.shape, F32)
    tile_cls = jnp.zeros(tile_ref.shape, F32)
    ends = jnp.zeros(tile_ref.shape, F32)
    for c in range(N_CLASSES):
        hit = cls == c
        hot = jnp.where(hit, 1.0, 0.0).astype(BF16)
        in_row = jnp.dot(hot, upper, preferred_element_type=F32)
        row_tot = jnp.dot(hot, ones, preferred_element_type=F32)
        before = jnp.dot(earlier_rows, row_tot.astype(BF16), preferred_element_type=F32)
        count = before[rows - 1:rows, 0:1] + row_tot[rows - 1:rows, 0:1]
        pos = pos + jnp.where(hit, in_row - 1.0 + before + off, 0.0)
        off = off + jnp.ceil(count / tile) * tile
        tile_cls = tile_cls + jnp.where(tile_start >= off, 1.0, 0.0)
        ends = jnp.where(lane8 == c, off, ends)
    pos_ref[...] = pos.astype(jnp.int32)
    tile_ref[...] = tile_cls.astype(jnp.int32)
    ends_ref[...] = ends.astype(jnp.int32)


def _sort(cls2d, *, tile):
    meta = jax.ShapeDtypeStruct((8, LANES), jnp.int32)
    return pl.pallas_call(
        functools.partial(_sort_kernel, tile=tile),
        out_shape=[jax.ShapeDtypeStruct(cls2d.shape, jnp.int32), meta, meta],
        compiler_params=pltpu.CompilerParams(vmem_limit_bytes=VMEM_LIMIT),
        name="moe_sort",
    )(cls2d)


def _row_copy(src, src_row, dst, dst_row, sem, rows=1):
    return pltpu.make_async_copy(src.at[pl.ds(src_row, rows)], dst.at[pl.ds(dst_row, rows)], sem)


def _dispatch_kernel(pos_ref, ends_ref, h2_ref, xs_hbm, zero_scr, sem, *, ch, tile):
    g = pl.program_id(0)

    @pl.when(g == 0)
    def _zero_tails():
        zero_scr[...] = jnp.zeros_like(zero_scr)
        n_tiles = xs_hbm.shape[0] // tile
        min_used = n_tiles - N_CLASSES
        total = ends_ref[0, N_CLASSES - 1]
        for phase in ("start", "wait"):
            def zero_tile(row0):
                cp = pltpu.make_async_copy(zero_scr, xs_hbm.at[pl.ds(pl.multiple_of(row0, tile), tile)], sem)
                cp.start() if phase == "start" else cp.wait()

            for c in range(N_CLASSES):
                end = ends_ref[0, c]
                prev = ends_ref[0, c - 1] if c else 0
                pl.when(end > prev)(functools.partial(zero_tile, end - tile))
                pl.when((min_used + c) * tile >= total)(functools.partial(zero_tile, (min_used + c) * tile))

    def body(i, carry):
        for u in range(DMA_UNROLL):
            k = i * DMA_UNROLL + u
            _row_copy(h2_ref, k, xs_hbm, pos_ref[0, 0, k], sem).start()
        return carry

    lax.fori_loop(0, ch // DMA_UNROLL, body, 0)
    _row_copy(h2_ref, 0, xs_hbm, 0, sem, rows=ch).wait()


def _dispatch(pos3, ends, h2ext, *, n_rows, tile):
    n_chunks, _, ch = pos3.shape
    return pl.pallas_call(
        functools.partial(_dispatch_kernel, ch=ch, tile=tile),
        grid=(n_chunks,),
        in_specs=[
            pl.BlockSpec((1, 1, ch), lambda g: (g, 0, 0), memory_space=pltpu.SMEM),
            pl.BlockSpec(ends.shape, lambda g: (0, 0), memory_space=pltpu.SMEM),
            pl.BlockSpec((ch, EXT_WIDTH), lambda g: (g, 0)),
        ],
        out_specs=pl.BlockSpec(memory_space=pl.ANY),
        out_shape=jax.ShapeDtypeStruct((n_rows, EXT_WIDTH), F32),
        scratch_shapes=[pltpu.VMEM((tile, EXT_WIDTH), F32), pltpu.SemaphoreType.DMA(())],
        compiler_params=pltpu.CompilerParams(dimension_semantics=("arbitrary",), vmem_limit_bytes=VMEM_LIMIT),
        name="moe_dispatch",
    )(pos3, ends, h2ext)


def _experts_kernel(ea_ref, eb_ref, used_ref, xs_ref, wga, wua, wda, wgb, wub, wdb, ys_ref):
    del ea_ref, eb_ref

    @pl.when(pl.program_id(0) < used_ref[0])
    def _():
        x = xs_ref[:, 0:D_MODEL].astype(BF16)
        wts = xs_ref[:, D_MODEL:EXT_WIDTH]
        y = None
        for col, (wg, wu, wd) in enumerate(((wga, wua, wda), (wgb, wub, wdb))):
            gate = jnp.dot(x, wg[0], preferred_element_type=F32)
            up = jnp.dot(x, wu[0], preferred_element_type=F32)
            he = (gate / (1.0 + jnp.exp(-gate)) * up * wts[:, col:col + 1]).astype(BF16)
            t = jnp.dot(he, wd[0], preferred_element_type=F32)
            y = t if y is None else y + t
        ys_ref[...] = y

    @pl.when(pl.program_id(0) >= used_ref[0])
    def _():
        ys_ref[...] = jnp.zeros_like(ys_ref)


def _experts(ea, eb, used, xs, wg, wu, wd, *, tile):
    n_tiles = xs.shape[0] // tile
    rows = lambda j, ea, eb, used: (jnp.minimum(j, used[0] - 1), 0)
    first = lambda j, ea, eb, used: (ea[j], 0, 0)
    second = lambda j, ea, eb, used: (eb[j], 0, 0)
    up_spec = lambda m: pl.BlockSpec((1, D_MODEL, D_EXPERT), m)
    down_spec = lambda m: pl.BlockSpec((1, D_EXPERT, D_MODEL), m)
    return pl.pallas_call(
        _experts_kernel,
        grid_spec=pltpu.PrefetchScalarGridSpec(
            num_scalar_prefetch=3,
            grid=(n_tiles,),
            in_specs=[pl.BlockSpec((tile, EXT_WIDTH), rows),
                      up_spec(first), up_spec(first), down_spec(first),
                      up_spec(second), up_spec(second), down_spec(second)],
            out_specs=pl.BlockSpec((tile, D_MODEL), lambda j, ea, eb, used: (j, 0)),
        ),
        out_shape=jax.ShapeDtypeStruct((xs.shape[0], D_MODEL), F32),
        compiler_params=pltpu.CompilerParams(dimension_semantics=("arbitrary",), vmem_limit_bytes=VMEM_LIMIT),
        name="moe_experts",
    )(ea, eb, used, xs, wg, wu, wd, wg, wu, wd)


def _combine_kernel(pos_ref, x1_ref, g_ref, ys_hbm, out_ref, y_scr, sem):
    tm = x1_ref.shape[0]

    def body(i, carry):
        for u in range(DMA_UNROLL):
            k = i * DMA_UNROLL + u
            _row_copy(ys_hbm, pos_ref[0, 0, k], y_scr, k, sem).start()
        return carry

    lax.fori_loop(0, tm // DMA_UNROLL, body, 0)
    _row_copy(ys_hbm, 0, y_scr, 0, sem, rows=tm).wait()
    x2 = x1_ref[...] + y_scr[...]
    ms = jnp.mean(x2 * x2, axis=-1, keepdims=True)
    out_ref[...] = x2 * lax.rsqrt(ms + NORM_EPS) * g_ref[...]


def _combine(pos3, x1, g, ys):
    n_chunks, _, tm = pos3.shape
    return pl.pallas_call(
        _combine_kernel,
        grid=(n_chunks,),
        in_specs=[
            pl.BlockSpec((1, 1, tm), lambda i: (i, 0, 0), memory_space=pltpu.SMEM),
            pl.BlockSpec((tm, D_MODEL), lambda i: (i, 0)),
            pl.BlockSpec((1, D_MODEL), lambda i: (0, 0)),
            pl.BlockSpec(memory_space=pl.ANY),
        ],
        out_specs=pl.BlockSpec((tm, D_MODEL), lambda i: (i, 0)),
        out_shape=jax.ShapeDtypeStruct(x1.shape, F32),
        scratch_shapes=[pltpu.VMEM((tm, D_MODEL), F32), pltpu.SemaphoreType.DMA(())],
        compiler_params=pltpu.CompilerParams(dimension_semantics=("arbitrary",), vmem_limit_bytes=VMEM_LIMIT),
        name="moe_combine",
    )(pos3, x1, g, ys)


def _moe(h2ext, cls3, x1, wg, wu, wd, g, *, tile=MOE_TILE, dispatch_chunk=1024, combine_chunk=512):
    n = x1.shape[0]
    n_tiles = n // tile + N_CLASSES
    assert n_tiles <= LANES
    pos, tile_cls, ends = _sort(cls3.reshape(n // LANES, LANES), tile=tile)
    xs = _dispatch(pos.reshape(n // dispatch_chunk, 1, dispatch_chunk), ends, h2ext, n_rows=n_tiles * tile, tile=tile)
    used = ends[0, N_CLASSES - 1:N_CLASSES] // tile
    tc = tile_cls[0, :n_tiles]
    tc = jnp.where(jnp.arange(n_tiles) < used[0], tc, tc[jnp.maximum(used[0] - 1, 0)])
    pair_a = jnp.array([0, 0, 0, 1, 1, 2], jnp.int32)
    pair_b = jnp.array([1, 2, 3, 2, 3, 3], jnp.int32)
    ea = (tc // N_PAIRS) * EXPERTS_PER_GROUP + pair_a[tc % N_PAIRS]
    eb = (tc // N_PAIRS) * EXPERTS_PER_GROUP + pair_b[tc % N_PAIRS]
    ys = _experts(ea, eb, used, xs, wg, wu, wd, tile=tile)
    return _combine(pos.reshape(n // combine_chunk, 1, combine_chunk), x1, g, ys)


def _rotary_tables(seq):
    pos = jnp.arange(seq, dtype=F32)
    inv_freq = 1.0 / (ROPE_THETA ** (jnp.arange(0, ROT_DIM, 2, dtype=F32) / ROT_DIM))
    ang = pos[:, None] * inv_freq[None, :]
    cos, sin = jnp.cos(ang), jnp.sin(ang)
    zeros = jnp.zeros((seq, HEAD_DIM - ROT_DIM), F32)
    zh = jnp.zeros((seq, ROT_HALF), F32)
    cos_h = jnp.concatenate([cos, cos, jnp.ones((seq, HEAD_DIM - ROT_DIM), F32)], axis=1)
    sa_h = jnp.concatenate([-sin, zh, zeros], axis=1)
    sb_h = jnp.concatenate([zh, sin, zeros], axis=1)
    tile = lambda t: jnp.concatenate([t, t], axis=1)
    return tile(cos_h), tile(sa_h), tile(sb_h)


def kernel(x, attn_norm, w_in, b_forget, w_out, ffn_norm, w_group, w_expert, w_gate_e, w_up_e, w_down_e, final_norm):
    b, seq, d = x.shape
    assert d == D_MODEL and w_in.shape[0] == 1, "single-layer block"
    n = b * seq
    scale = HEAD_DIM ** -0.5 * LOG2E
    col_scale = jnp.ones((2 * QKV_WIDTH + HEADS_B,), F32)
    col_scale = col_scale.at[0:WIDTH_A].set(scale).at[QKV_WIDTH:QKV_WIDTH + WIDTH_B].set(scale)
    w = jnp.pad(w_in[0] * col_scale[None, :], ((0, 0), (0, LANES - HEADS_B))).astype(BF16)
    bf = jnp.pad(b_forget[0].astype(F32), (0, LANES - HEADS_B))[None, :]
    cosv, sa, sb = _rotary_tables(seq)
    x2 = x.reshape(n, d)

    qkva, qkvb, c = _inproj(x2, attn_norm[0][None, :], w, cosv, sa, sb, bf, seq=seq, tm=512)
    out_a = _dilated(qkva.reshape(b, seq, QKV_WIDTH))
    out_b = _fox(qkvb.reshape(b, seq, QKV_WIDTH), c.reshape(b, seq, LANES), tq=512)

    wr = jnp.pad(jnp.concatenate([w_group[0], w_expert[0]], axis=1).astype(F32),
                 ((0, 0), (0, LANES - N_GROUPS - N_EXPERTS)))
    x1, h2ext, cls3 = _outproj(out_a.reshape(n, WIDTH_A), out_b.reshape(n, WIDTH_B), x2,
                               w_out[0].astype(BF16), ffn_norm[0][None, :], wr, tm=512)
    out = _moe(h2ext, cls3, x1, w_gate_e[0].astype(BF16), w_up_e[0].astype(BF16), w_down_e[0].astype(BF16),
               final_norm[None, :])
    return out.reshape(b, seq, d)
```

```python
import functools

import jax
import jax.numpy as jnp
import numpy as np
from jax import lax
from jax.experimental import pallas as pl
from jax.experimental.pallas import tpu as pltpu

F32 = jnp.float32
BF16 = jnp.bfloat16

D_MODEL = 1024
HEAD_DIM = 64
HEADS_A = 8
HEADS_B = 8
WIDTH_A = HEADS_A * HEAD_DIM
WIDTH_B = HEADS_B * HEAD_DIM
QKV_WIDTH = 3 * WIDTH_A
DILATIONS = ((128, 1), (512, 4), (2048, 16))
ROT_DIM = HEAD_DIM // 4
ROT_HALF = ROT_DIM // 2
ROPE_THETA = 500000.0
N_GROUPS = 4
EXPERTS_PER_GROUP = 4
N_EXPERTS = N_GROUPS * EXPERTS_PER_GROUP
D_EXPERT = 512
NORM_EPS = 1e-6
NEG_INF = -1e30

LANES = 128
HEAD_PAIRS = WIDTH_A // LANES
WIN = 128
MAX_DIL = 16
PERM = MAX_DIL * MAX_DIL
KV_BLOCK = 256
N_PAIRS = EXPERTS_PER_GROUP * (EXPERTS_PER_GROUP - 1) // 2
N_CLASSES = N_GROUPS * N_PAIRS
MOE_TILE = 256
EXT_WIDTH = D_MODEL + LANES
DMA_UNROLL = 8
DMA_THREADS = 2
ROUTER_OFF = N_GROUPS
VMEM_LIMIT = 56 * 1024 * 1024
LOG2E = 1.4426950408889634
BOUND_SLACK = 1.0 + 2.0 ** -7
BOUND_EPS = 2.0 ** -7
GAP_MAX = 64.0


def _split3(x):
    hi = x.astype(BF16)
    r1 = x - hi.astype(F32)
    mid = r1.astype(BF16)
    lo = (r1 - mid.astype(F32)).astype(BF16)
    return hi, mid, lo


def _inproj_kernel(x_ref, g_ref, w_ref, cos_ref, sa_ref, sb_ref, bf_ref,
                   qkva_ref, qkvb_ref, c_ref, h_scr, carry_scr, *, tiles_per_seq):
    i = pl.program_id(0)
    tm = x_ref.shape[0]
    x = x_ref[...]
    ms = jnp.mean(x * x, axis=-1, keepdims=True)
    h_scr[...] = (x * lax.rsqrt(ms + NORM_EPS) * g_ref[...]).astype(BF16)
    cosv = cos_ref[...]
    sa = sa_ref[...]
    sb = sb_ref[...]
    for g in range(6):
        p = jnp.dot(h_scr[...], w_ref[:, g * WIDTH_A:(g + 1) * WIDTH_A], preferred_element_type=F32)
        dst = qkva_ref if g < 3 else qkvb_ref
        col0 = (g % 3) * WIDTH_A
        if g < 2:
            for k in range(HEAD_PAIRS):
                t = p[:, k * LANES:(k + 1) * LANES]
                t = (t * cosv + pltpu.roll(t, LANES - ROT_HALF, 1) * sa + pltpu.roll(t, ROT_HALF, 1) * sb)
                dst[:, col0 + k * LANES:col0 + (k + 1) * LANES] = t.astype(BF16)
        else:
            dst[:, col0:col0 + WIDTH_A] = p.astype(BF16)
    z = jnp.dot(h_scr[...], w_ref[:, 2 * QKV_WIDTH:2 * QKV_WIDTH + LANES], preferred_element_type=F32) + bf_ref[...]
    lf = jnp.minimum(z, 0.0) - jnp.log1p(jnp.exp(-jnp.abs(z)))
    lane = lax.broadcasted_iota(jnp.int32, (tm, LANES), 1)
    lf = jnp.where(lane < HEADS_B, lf * LOG2E, 0.0)
    hi, mid, lo = _split3(lf)
    tri = (lax.broadcasted_iota(jnp.int32, (tm, tm), 0) >= lax.broadcasted_iota(jnp.int32, (tm, tm), 1)).astype(BF16)
    cs = (jnp.dot(tri, hi, preferred_element_type=F32) + jnp.dot(tri, mid, preferred_element_type=F32)
          + jnp.dot(tri, lo, preferred_element_type=F32))

    @pl.when(i % tiles_per_seq == 0)
    def _():
        carry_scr[...] = jnp.zeros_like(carry_scr)

    c = cs + carry_scr[0:1, :]
    c_ref[...] = c
    carry_scr[...] = jnp.broadcast_to(c[tm - 1:tm, :], carry_scr.shape)


def _inproj(x2, g, w, cosv, sa, sb, bf, *, seq, tm):
    n = x2.shape[0]
    kern = functools.partial(_inproj_kernel, tiles_per_seq=seq // tm)
    tps = seq // tm
    return pl.pallas_call(
        kern,
        grid=(n // tm,),
        in_specs=[
            pl.BlockSpec((tm, D_MODEL), lambda i: (i, 0)),
            pl.BlockSpec((1, D_MODEL), lambda i: (0, 0)),
            pl.BlockSpec(w.shape, lambda i: (0, 0)),
            pl.BlockSpec((tm, LANES), lambda i: (i % tps, 0)),
            pl.BlockSpec((tm, LANES), lambda i: (i % tps, 0)),
            pl.BlockSpec((tm, LANES), lambda i: (i % tps, 0)),
            pl.BlockSpec((1, LANES), lambda i: (0, 0)),
        ],
        out_specs=[
            pl.BlockSpec((tm, QKV_WIDTH), lambda i: (i, 0)),
            pl.BlockSpec((tm, QKV_WIDTH), lambda i: (i, 0)),
            pl.BlockSpec((tm, LANES), lambda i: (i, 0)),
        ],
        out_shape=[
            jax.ShapeDtypeStruct((n, QKV_WIDTH), BF16),
            jax.ShapeDtypeStruct((n, QKV_WIDTH), BF16),
            jax.ShapeDtypeStruct((n, LANES), F32),
        ],
        scratch_shapes=[pltpu.VMEM((tm, D_MODEL), BF16), pltpu.VMEM((8, LANES), F32)],
        compiler_params=pltpu.CompilerParams(dimension_semantics=("arbitrary",), vmem_limit_bytes=VMEM_LIMIT),
        name="inproj",
    )(x2, g, w, cosv, sa, sb, bf)


def _head_column(c_all, lane, head):
    return jnp.sum(jnp.where(lane == head, c_all, 0.0), axis=1, keepdims=True)


def _own_lanes(lane, hl):
    return (lane < HEAD_DIM) if hl == 0 else (lane >= HEAD_DIM)


def _own_sum(vals, lane, hl):
    return jnp.sum(jnp.where(_own_lanes(lane, hl), vals, 0.0), axis=1, keepdims=True)


def _norm_bound(qsq, ksq_max):
    return jnp.sqrt(qsq * ksq_max) * BOUND_SLACK + BOUND_EPS


def _fox_kernel(q_ref, k_ref, v_ref, cq_ref, ck_ref, o_ref,
                k0_scr, k1_scr, v0_scr, v1_scr, ksq_scr, acc0_scr, acc1_scr, s_scr, *, tq, seq, chunk):
    hp = pl.program_id(1)
    qi = pl.program_id(2)
    k_scr = (k0_scr, k1_scr)
    v_scr = (v0_scr, v1_scr)
    acc_scr = (acc0_scr, acc1_scr)

    def augmented(vals, cterm, lane, hl, key_side):
        hi, mid, lo = (t.astype(F32) for t in _split3(cterm))
        base = HEAD_DIM if hl == 0 else 0
        ones = (lane >= base + (0 if key_side else 3)) & (lane < base + (3 if key_side else 6))
        o = 3 if key_side else 0
        sgn = -1.0 if key_side else 1.0
        aug = jnp.where(lane == base + o, sgn * hi,
                        jnp.where(lane == base + o + 1, sgn * mid,
                                  jnp.where(lane == base + o + 2, sgn * lo,
                                            jnp.where(ones, 1.0, 0.0))))
        return jnp.where(_own_lanes(lane, hl), vals, aug).astype(BF16)

    @pl.when(qi == 0)
    def _build_keys():
        lane = lax.broadcasted_iota(jnp.int32, (chunk, LANES), 1)

        def body(ci, ksq):
            r0 = pl.multiple_of(ci * chunk, chunk)
            kc = k_ref[0, pl.ds(r0, chunk), :].astype(F32)
            vc = v_ref[0, pl.ds(r0, chunk), :].astype(F32)
            cc = ck_ref[0, pl.ds(r0, chunk), :]
            new = []
            for hl in range(2):
                k_scr[hl][pl.ds(r0, chunk), :] = augmented(kc, _head_column(cc, lane, 2 * hp + hl), lane, hl, True)
                one_lane = HEAD_DIM if hl == 0 else 0
                v_scr[hl][pl.ds(r0, chunk), :] = jnp.where(
                    _own_lanes(lane, hl), vc, jnp.where(lane == one_lane, 1.0, 0.0)).astype(BF16)
                new.append(jnp.maximum(ksq[hl], jnp.max(_own_sum(kc * kc, lane, hl), axis=0, keepdims=True)))
            return tuple(new)

        zero = jnp.zeros((1, 1), F32)
        ksq0, ksq1 = lax.fori_loop(0, seq // chunk, body, (zero, zero))
        lane8 = lax.broadcasted_iota(jnp.int32, ksq_scr.shape, 1)
        ksq_scr[...] = jnp.where(lane8 < HEAD_DIM, ksq0, ksq1)

    lane_q = lax.broadcasted_iota(jnp.int32, (tq, LANES), 1)
    q2 = q_ref[0].astype(F32)
    cq = cq_ref[0]
    kd = k_ref[0, pl.ds(pl.multiple_of(qi * tq, tq), tq), :].astype(F32)
    ksq_row = ksq_scr[0:1, :]
    qh = []
    gap = None
    for hl in range(2):
        bound = _norm_bound(_own_sum(q2 * q2, lane_q, hl), ksq_row[:, hl * HEAD_DIM:hl * HEAD_DIM + 1])
        diag = _own_sum(q2 * kd, lane_q, hl)
        g = jnp.max(bound - diag)
        gap = g if gap is None else jnp.maximum(gap, g)
        qh.append(augmented(q2, _head_column(cq, lane_q, 2 * hp + hl) - bound, lane_q, hl, False))
    n_diag = tq // KV_BLOCK
    n_full = qi * n_diag
    row_minus_col = (lax.broadcasted_iota(jnp.int32, (tq, KV_BLOCK), 0)
                     - lax.broadcasted_iota(jnp.int32, (tq, KV_BLOCK), 1))

    def causal_mask(s, d):
        return jnp.where(row_minus_col >= d * KV_BLOCK, s, NEG_INF)

    def logits(hl, j):
        r0 = pl.multiple_of(j * KV_BLOCK, KV_BLOCK)
        return lax.dot_general(qh[hl], k_scr[hl][pl.ds(r0, KV_BLOCK), :], (((1,), (1,)), ((), ())),
                               preferred_element_type=F32)

    def values(hl, j):
        return v_scr[hl][pl.ds(pl.multiple_of(j * KV_BLOCK, KV_BLOCK), KV_BLOCK), :]

    def finish(a0, a1):
        l0 = a0[:, HEAD_DIM:HEAD_DIM + 1]
        l1 = a1[:, 0:1]
        o_ref[0] = jnp.where(lane_q < HEAD_DIM, a0 / l0, a1 / l1).astype(o_ref.dtype)

    @pl.when(gap <= GAP_MAX)
    def _fast():
        acc0_scr[...] = jnp.zeros_like(acc0_scr)
        acc1_scr[...] = jnp.zeros_like(acc1_scr)

        def produce(j, slot):
            for hl in range(2):
                s_scr[slot, hl] = logits(hl, j)

        def consume(j, slot, diag):
            for hl in range(2):
                s = s_scr[slot, hl]
                if diag is not None:
                    s = causal_mask(s, diag)
                acc_scr[hl][...] += jnp.dot(jnp.exp2(s).astype(BF16), values(hl, j), preferred_element_type=F32)

        for d in range(n_diag):
            produce(d, d)

        def body(i, carry):
            for d in range(n_diag):
                j = i * n_diag + d
                consume(j, d, None)
                produce(j + n_diag, d)
            return carry

        lax.fori_loop(0, qi, body, 0)
        for d in range(n_diag):
            consume(n_full + d, d, d)
        finish(acc0_scr[...], acc1_scr[...])

    @pl.when(gap > GAP_MAX)
    def _general():
        def step(j, carry, diag):
            new = []
            for hl in range(2):
                m, a = carry[2 * hl:2 * hl + 2]
                s = logits(hl, j)
                if diag is not None:
                    s = causal_mask(s, diag)
                mn = jnp.maximum(m, jnp.max(s, axis=1, keepdims=True))
                a = jnp.exp2(m - mn) * a + jnp.dot(jnp.exp2(s - mn).astype(BF16), values(hl, j),
                                                  preferred_element_type=F32)
                new += [mn, a]
            return tuple(new)

        carry = (jnp.full((tq, 1), NEG_INF, F32), jnp.zeros((tq, LANES), F32)) * 2
        carry = lax.fori_loop(0, n_full, lambda j, c: step(j, c, None), carry)
        for d in range(n_diag):
            carry = step(n_full + d, carry, d)
        finish(carry[1], carry[3])


def _fox(qkvb, c, *, tq, chunk=512):
    b, seq, _ = qkvb.shape
    kern = functools.partial(_fox_kernel, tq=tq, seq=seq, chunk=chunk)
    return pl.pallas_call(
        kern,
        grid=(b, HEAD_PAIRS, seq // tq),
        in_specs=[
            pl.BlockSpec((1, tq, LANES), lambda bi, hp, qi: (bi, qi, hp)),
            pl.BlockSpec((1, seq, LANES), lambda bi, hp, qi: (bi, 0, HEAD_PAIRS + hp)),
            pl.BlockSpec((1, seq, LANES), lambda bi, hp, qi: (bi, 0, 2 * HEAD_PAIRS + hp)),
            pl.BlockSpec((1, tq, LANES), lambda bi, hp, qi: (bi, qi, 0)),
            pl.BlockSpec((1, seq, LANES), lambda bi, hp, qi: (bi, 0, 0)),
        ],
        out_specs=pl.BlockSpec((1, tq, LANES), lambda bi, hp, qi: (bi, qi, hp)),
        out_shape=jax.ShapeDtypeStruct((b, seq, WIDTH_B), BF16),
        scratch_shapes=[pltpu.VMEM((seq, LANES), BF16), pltpu.VMEM((seq, LANES), BF16),
                        pltpu.VMEM((seq, LANES), BF16), pltpu.VMEM((seq, LANES), BF16),
                        pltpu.VMEM((8, LANES), F32),
                        pltpu.VMEM((tq, LANES), F32), pltpu.VMEM((tq, LANES), F32),
                        pltpu.VMEM((tq // KV_BLOCK, 2, tq, KV_BLOCK), F32)],
        compiler_params=pltpu.CompilerParams(
            dimension_semantics=("arbitrary", "arbitrary", "arbitrary"), vmem_limit_bytes=VMEM_LIMIT),
        name="fox",
    )(qkvb, qkvb, qkvb, c, c)


def _dilated_general(q_ref, k_ref, v_ref, o_ref, q16, k16, v16, m_s, l_s, acc_s, on_s, lse_s, *, seq):
    sub = seq // MAX_DIL
    own0 = lax.broadcasted_iota(jnp.int32, (WIN, LANES), 1) < HEAD_DIM
    ri = lax.broadcasted_iota(jnp.int32, (PERM, PERM), 0)
    ci = lax.broadcasted_iota(jnp.int32, (PERM, PERM), 1)
    perm = (((ri >> 4) == (ci & 15)) & ((ri & 15) == (ci >> 4))).astype(BF16)

    qa = lax.broadcasted_iota(jnp.int32, (WIN, 2 * WIN), 0)
    kb_i = lax.broadcasted_iota(jnp.int32, (WIN, 2 * WIN), 1)
    cur = kb_i >= WIN
    kin = kb_i & (WIN - 1)
    dist_nat = qa - kin + jnp.where(cur, 0, WIN)
    band_nat = (dist_nat >= 0) & (dist_nat <= WIN)
    sq = ((qa & 31) << 2) + (qa >> 5)
    sk = ((kin & 31) << 2) + (kin >> 5)
    dist_4 = sq - sk + jnp.where(cur, 0, WIN)
    band_4 = (dist_4 >= 0) & (dist_4 <= WIN)

    def attend(qb, kb, vb, mask, state):
        qf = qb.astype(F32)
        new = []
        for hl in range(2):
            m, l, a = state[3 * hl:3 * hl + 3]
            own = own0 if hl == 0 else jnp.logical_not(own0)
            qh = jnp.where(own, qf, 0.0).astype(BF16)
            s = lax.dot_general(qh, kb, (((1,), (1,)), ((), ())), preferred_element_type=F32)
            s = jnp.where(mask, s, NEG_INF)
            mn = jnp.maximum(m, jnp.max(s, axis=1, keepdims=True))
            alpha = jnp.exp2(m - mn)
            p = jnp.exp2(s - mn)
            l = alpha * l + jnp.sum(p, axis=1, keepdims=True)
            a = alpha * a + jnp.dot(p.astype(BF16), vb, preferred_element_type=F32)
            new += [mn, l, a]
        return new

    def pack_state(st):
        m0, l0, a0, m1, l1, a1 = st
        return (jnp.where(own0, m0, m1), jnp.where(own0, l0, l1), jnp.where(own0, a0, a1))

    def deint(blk, carry):
        r0 = pl.multiple_of(blk * PERM, PERM)
        j0 = pl.multiple_of(blk * MAX_DIL, MAX_DIL)
        for src, dst in ((q_ref, q16), (k_ref, k16), (v_ref, v16)):
            y = jnp.dot(perm, src[0, pl.ds(r0, PERM), :], preferred_element_type=F32).astype(BF16)
            for r in range(MAX_DIL):
                dst[r, pl.ds(j0, MAX_DIL), :] = y[r * MAX_DIL:(r + 1) * MAX_DIL, :]
        return carry

    lax.fori_loop(0, seq // PERM, deint, 0)

    fresh = [jnp.full((WIN, 1), NEG_INF, F32), jnp.zeros((WIN, 1), F32), jnp.zeros((WIN, LANES), F32)] * 2
    nb16 = sub // WIN

    def d16_body(t, carry):
        r = t // nb16
        n = t % nb16
        c0 = pl.multiple_of(n * WIN, WIN)
        p0 = pl.multiple_of(jnp.maximum(n - 1, 0) * WIN, WIN)
        qb = q16[r, pl.ds(c0, WIN), :]
        kb = jnp.concatenate([k16[r, pl.ds(p0, WIN), :], k16[r, pl.ds(c0, WIN), :]], axis=0)
        vb = jnp.concatenate([v16[r, pl.ds(p0, WIN), :], v16[r, pl.ds(c0, WIN), :]], axis=0)
        mask = band_nat & (cur | (n > 0))
        mm, ll, aa = pack_state(attend(qb, kb, vb, mask, fresh))
        row = pl.multiple_of(r * sub + c0, WIN)
        m_s[pl.ds(row, WIN), :] = mm
        l_s[pl.ds(row, WIN), :] = ll
        acc_s[pl.ds(row, WIN), :] = aa
        return carry

    lax.fori_loop(0, MAX_DIL * nb16, d16_body, 0)

    nb4 = (seq // 4) // WIN
    ch = WIN // 4

    def d4_body(t, carry):
        r4 = t // nb4
        n = t % nb4
        c0 = pl.multiple_of(n * ch, ch)
        p0 = pl.multiple_of(jnp.maximum(n - 1, 0) * ch, ch)

        def gather(ref, j0):
            return [ref[r4 + 4 * q, pl.ds(j0, ch), :] for q in range(4)]

        qb = jnp.concatenate(gather(q16, c0), axis=0)
        kb = jnp.concatenate(gather(k16, p0) + gather(k16, c0), axis=0)
        vb = jnp.concatenate(gather(v16, p0) + gather(v16, c0), axis=0)
        rows = [pl.multiple_of((r4 + 4 * q) * sub + c0, ch) for q in range(4)]
        mm = jnp.concatenate([m_s[pl.ds(rw, ch), :] for rw in rows], axis=0)
        ll = jnp.concatenate([l_s[pl.ds(rw, ch), :] for rw in rows], axis=0)
        aa = jnp.concatenate([acc_s[pl.ds(rw, ch), :] for rw in rows], axis=0)
        state = [mm[:, 0:1], ll[:, 0:1], aa, mm[:, HEAD_DIM:HEAD_DIM + 1], ll[:, HEAD_DIM:HEAD_DIM + 1], aa]
        mask = band_4 & (cur | (n > 0))
        mm, ll, aa = pack_state(attend(qb, kb, vb, mask, state))
        for q, rw in enumerate(rows):
            m_s[pl.ds(rw, ch), :] = mm[q * ch:(q + 1) * ch, :]
            l_s[pl.ds(rw, ch), :] = ll[q * ch:(q + 1) * ch, :]
            acc_s[pl.ds(rw, ch), :] = aa[q * ch:(q + 1) * ch, :]
        return carry

    lax.fori_loop(0, 4 * nb4, d4_body, 0)

    def renat(blk, carry):
        j0 = pl.multiple_of(blk * MAX_DIL, MAX_DIL)
        r0 = pl.multiple_of(blk * PERM, PERM)
        rows = [pl.multiple_of(r * sub + j0, MAX_DIL) for r in range(MAX_DIL)]
        mm = jnp.concatenate([m_s[pl.ds(rw, MAX_DIL), :] for rw in rows], axis=0)
        ll = jnp.concatenate([l_s[pl.ds(rw, MAX_DIL), :] for rw in rows], axis=0)
        aa = jnp.concatenate([acc_s[pl.ds(rw, MAX_DIL), :] for rw in rows], axis=0)
        o = (aa / ll).astype(BF16)
        hi, mid, lo = _split3(mm + jnp.log2(ll))
        on_s[pl.ds(r0, PERM), :] = jnp.dot(perm, o, preferred_element_type=F32)
        lse_s[pl.ds(r0, PERM), :] = (jnp.dot(perm, hi, preferred_element_type=F32)
                                     + jnp.dot(perm, mid, preferred_element_type=F32)
                                     + jnp.dot(perm, lo, preferred_element_type=F32))
        return carry

    lax.fori_loop(0, seq // PERM, renat, 0)

    def d1_body(n, carry):
        c0 = pl.multiple_of(n * WIN, WIN)
        p0 = pl.multiple_of(jnp.maximum(n - 1, 0) * WIN, WIN)
        qb = q_ref[0, pl.ds(c0, WIN), :]
        kb = jnp.concatenate([k_ref[0, pl.ds(p0, WIN), :], k_ref[0, pl.ds(c0, WIN), :]], axis=0)
        vb = jnp.concatenate([v_ref[0, pl.ds(p0, WIN), :], v_ref[0, pl.ds(c0, WIN), :]], axis=0)
        lse = lse_s[pl.ds(c0, WIN), :]
        o = on_s[pl.ds(c0, WIN), :]
        one = jnp.ones((WIN, 1), F32)
        state = [lse[:, 0:1], one, o, lse[:, HEAD_DIM:HEAD_DIM + 1], one, o]
        mask = band_nat & (cur | (n > 0))
        m0, l0, a0, m1, l1, a1 = attend(qb, kb, vb, mask, state)
        o_ref[0, pl.ds(c0, WIN), :] = jnp.where(own0, a0 / l0, a1 / l1).astype(o_ref.dtype)
        return carry

    lax.fori_loop(0, seq // WIN, d1_body, 0)


def _dilated_fast(o_ref, qn, kn, vn, qr, kr, vr, fr, fn, s_scr, mask_scr, *, seq):
    sub = seq // MAX_DIL
    own0 = lax.broadcasted_iota(jnp.int32, (WIN, LANES), 1) < HEAD_DIM
    ri = lax.broadcasted_iota(jnp.int32, (PERM, PERM), 0)
    ci = lax.broadcasted_iota(jnp.int32, (PERM, PERM), 1)
    perm = (((ri >> 4) == (ci & 15)) & ((ri & 15) == (ci >> 4))).astype(BF16)

    qa = lax.broadcasted_iota(jnp.int32, (WIN, 2 * WIN), 0)
    kb_i = lax.broadcasted_iota(jnp.int32, (WIN, 2 * WIN), 1)
    cur = kb_i >= WIN
    kin = kb_i & (WIN - 1)
    dist_nat = qa - kin + jnp.where(cur, 0, WIN)
    band_nat = (dist_nat >= 0) & (dist_nat <= WIN)
    sq = ((qa & 31) << 2) + (qa >> 5)
    sk = ((kin & 31) << 2) + (kin >> 5)
    dist_4 = sq - sk + jnp.where(cur, 0, WIN)
    band_4 = (dist_4 >= 0) & (dist_4 <= WIN)
    for i, band in enumerate((band_nat, band_4)):
        mask_scr[2 * i] = jnp.where(band & cur, 1.0, 0.0).astype(BF16)
        mask_scr[2 * i + 1] = jnp.where(band, 1.0, 0.0).astype(BF16)

    def deint(blk, carry):
        r0 = pl.multiple_of(blk * PERM, PERM)
        j0 = pl.multiple_of(blk * MAX_DIL, MAX_DIL)
        for srcs, dsts in ((qn, qr), (kn, kr), (vn, vr)):
            for src, dst in zip(srcs, dsts):
                y = jnp.dot(perm, src[pl.ds(r0, PERM), :], preferred_element_type=F32).astype(BF16)
                for r in range(MAX_DIL):
                    dst[r, pl.ds(j0, MAX_DIL), :] = y[r * MAX_DIL:(r + 1) * MAX_DIL, :]
        return carry

    lax.fori_loop(0, seq // PERM, deint, 0)

    def run_branch(nblocks, block_in_seq, fetch, mask_base, sink):
        def produce(t, pair, u):
            for hl in range(2):
                qb, kb = fetch(t, hl, True)
                s_scr[pair, u, hl] = lax.dot_general(qb, kb, (((1,), (1,)), ((), ())), preferred_element_type=F32)

        def consume(t, pair, u):
            mk = mask_scr[mask_base + jnp.minimum(block_in_seq(t), 1)]
            pv = []
            for hl in range(2):
                p = jnp.exp2(s_scr[pair, u, hl]).astype(BF16) * mk
                pv.append(jnp.dot(p, fetch(t, hl, False), preferred_element_type=F32))
            sink(t, pv)

        produce(0, 0, 0)
        produce(1, 0, 1)

        def body(i, carry):
            pair = i & 1
            for u in range(2):
                consume(2 * i + u, pair, u)
            for u in range(2):
                produce(jnp.minimum(2 * i + 2 + u, nblocks - 1), 1 - pair, u)
            return carry

        lax.fori_loop(0, nblocks // 2, body, 0)

    def prev_cur(ref_block, n, size):
        c0 = pl.multiple_of(n * size, size)
        p0 = pl.multiple_of(jnp.maximum(n - 1, 0) * size, size)
        return ref_block(p0) + ref_block(c0)

    nb16 = sub // WIN

    def fetch16(t, hl, qk):
        r, n = t // nb16, t % nb16
        if qk:
            return (qr[hl][r, pl.ds(pl.multiple_of(n * WIN, WIN), WIN), :],
                    jnp.concatenate(prev_cur(lambda s: [kr[hl][r, pl.ds(s, WIN), :]], n, WIN), axis=0))
        return jnp.concatenate(prev_cur(lambda s: [vr[hl][r, pl.ds(s, WIN), :]], n, WIN), axis=0)

    def sink16(t, pv):
        row = pl.multiple_of(t * WIN, WIN)
        for hl in range(2):
            fr[hl][pl.ds(row, WIN), :] = pv[hl]

    run_branch(MAX_DIL * nb16, lambda t: t % nb16, fetch16, 0, sink16)

    nb4 = (seq // 4) // WIN
    ch = WIN // 4

    def fetch4(t, hl, qk):
        r4, n = t // nb4, t % nb4
        gather = lambda ref: (lambda s: [ref[r4 + 4 * q, pl.ds(s, ch), :] for q in range(4)])
        if qk:
            return (jnp.concatenate(gather(qr[hl])(pl.multiple_of(n * ch, ch)), axis=0),
                    jnp.concatenate(prev_cur(gather(kr[hl]), n, ch), axis=0))
        return jnp.concatenate(prev_cur(gather(vr[hl]), n, ch), axis=0)

    def sink4(t, pv):
        r4, n = t // nb4, t % nb4
        for q in range(4):
            row = pl.multiple_of((r4 + 4 * q) * sub + n * ch, ch)
            for hl in range(2):
                fr[hl][pl.ds(row, ch), :] += pv[hl][q * ch:(q + 1) * ch, :]

    run_branch(4 * nb4, lambda t: t % nb4, fetch4, 2, sink4)

    def renat(blk, carry):
        j0 = pl.multiple_of(blk * MAX_DIL, MAX_DIL)
        r0 = pl.multiple_of(blk * PERM, PERM)
        for hl in range(2):
            a = jnp.concatenate([fr[hl][pl.ds(pl.multiple_of(r * sub + j0, MAX_DIL), MAX_DIL), :]
                                 for r in range(MAX_DIL)], axis=0)
            hi, mid, lo = _split3(a)
            fn[hl][pl.ds(r0, PERM), :] = (jnp.dot(perm, hi, preferred_element_type=F32)
                                          + jnp.dot(perm, mid, preferred_element_type=F32)
                                          + jnp.dot(perm, lo, preferred_element_type=F32))
        return carry

    lax.fori_loop(0, seq // PERM, renat, 0)

    def fetch1(t, hl, qk):
        if qk:
            return (qn[hl][pl.ds(pl.multiple_of(t * WIN, WIN), WIN), :],
                    jnp.concatenate(prev_cur(lambda s: [kn[hl][pl.ds(s, WIN), :]], t, WIN), axis=0))
        return jnp.concatenate(prev_cur(lambda s: [vn[hl][pl.ds(s, WIN), :]], t, WIN), axis=0)

    def sink1(t, pv):
        row = pl.multiple_of(t * WIN, WIN)
        t0 = fn[0][pl.ds(row, WIN), :] + pv[0]
        t1 = fn[1][pl.ds(row, WIN), :] + pv[1]
        o_ref[0, pl.ds(row, WIN), :] = jnp.where(own0, t0 / t0[:, HEAD_DIM:HEAD_DIM + 1],
                                                 t1 / t1[:, 0:1]).astype(o_ref.dtype)

    run_branch(seq // WIN, lambda t: t, fetch1, 0, sink1)


def _dilated_kernel(q_ref, k_ref, v_ref, o_ref, qn0, qn1, kn0, kn1, vn0, vn1, qr0, qr1, kr0, kr1, vr0, vr1,
                    f0, f1, f2, f3, f4, s_scr, mask_scr, *, seq, chunk):
    lane = lax.broadcasted_iota(jnp.int32, (chunk, LANES), 1)
    qn, kn, vn = (qn0, qn1), (kn0, kn1), (vn0, vn1)

    def ksq_body(ci, ksq):
        kc = k_ref[0, pl.ds(pl.multiple_of(ci * chunk, chunk), chunk), :].astype(F32)
        return tuple(jnp.maximum(ksq[hl], jnp.max(_own_sum(kc * kc, lane, hl), axis=0, keepdims=True))
                     for hl in range(2))

    zero = jnp.zeros((1, 1), F32)
    ksq = lax.fori_loop(0, seq // chunk, ksq_body, (zero, zero))

    def build(ci, gap):
        rows = pl.ds(pl.multiple_of(ci * chunk, chunk), chunk)
        qc = q_ref[0, rows, :].astype(F32)
        kc = k_ref[0, rows, :].astype(F32)
        vc = v_ref[0, rows, :].astype(F32)
        for hl in range(2):
            own = _own_lanes(lane, hl)
            base = HEAD_DIM if hl == 0 else 0
            bound = _norm_bound(_own_sum(qc * qc, lane, hl), ksq[hl])
            gap = jnp.maximum(gap, jnp.max(bound - _own_sum(qc * kc, lane, hl), axis=0, keepdims=True))
            hi, mid, lo = (t.astype(F32) for t in _split3(-bound))
            qaug = jnp.where(lane == base, hi, jnp.where(lane == base + 1, mid, jnp.where(lane == base + 2, lo, 0.0)))
            qn[hl][rows, :] = jnp.where(own, qc, qaug).astype(BF16)
            kn[hl][rows, :] = jnp.where(own, kc, jnp.where((lane >= base) & (lane < base + 3), 1.0, 0.0)).astype(BF16)
            vn[hl][rows, :] = jnp.where(own, vc, jnp.where(lane == base, 1.0, 0.0)).astype(BF16)
        return gap

    gap = jnp.max(lax.fori_loop(0, seq // chunk, build, jnp.full((1, 1), NEG_INF, F32)))

    @pl.when(gap <= GAP_MAX)
    def _fast():
        _dilated_fast(o_ref, qn, kn, vn, (qr0, qr1), (kr0, kr1), (vr0, vr1), (f0, f1), (f2, f3),
                      s_scr, mask_scr, seq=seq)

    @pl.when(gap > GAP_MAX)
    def _general():
        _dilated_general(q_ref, k_ref, v_ref, o_ref, qr0, kr0, vr0, f0, f1, f2, f3, f4, seq=seq)


def _dilated(qkva, *, chunk=512):
    b, seq, _ = qkva.shape
    sub = seq // MAX_DIL
    kern = functools.partial(_dilated_kernel, seq=seq, chunk=chunk)
    return pl.pallas_call(
        kern,
        grid=(b, HEAD_PAIRS),
        in_specs=[
            pl.BlockSpec((1, seq, LANES), lambda bi, hp: (bi, 0, hp)),
            pl.BlockSpec((1, seq, LANES), lambda bi, hp: (bi, 0, HEAD_PAIRS + hp)),
            pl.BlockSpec((1, seq, LANES), lambda bi, hp: (bi, 0, 2 * HEAD_PAIRS + hp)),
        ],
        out_specs=pl.BlockSpec((1, seq, LANES), lambda bi, hp: (bi, 0, hp)),
        out_shape=jax.ShapeDtypeStruct((b, seq, WIDTH_A), BF16),
        scratch_shapes=(
            [pltpu.VMEM((seq, LANES), BF16)] * 6
            + [pltpu.VMEM((MAX_DIL, sub, LANES), BF16)] * 6
            + [pltpu.VMEM((seq, LANES), F32)] * 5
            + [pltpu.VMEM((2, 2, 2, WIN, 2 * WIN), F32), pltpu.VMEM((4, WIN, 2 * WIN), BF16)]
        ),
        compiler_params=pltpu.CompilerParams(
            dimension_semantics=("arbitrary", "arbitrary"), vmem_limit_bytes=VMEM_LIMIT),
        name="dilated",
    )(qkva, qkva, qkva)


def _outproj_kernel(oa_ref, ob_ref, x_ref, wo_ref, g_ref, wr_ref, x1_ref, h2_ref, cls_ref):
    tm = x_ref.shape[0]
    x1 = (x_ref[...]
          + jnp.dot(oa_ref[...], wo_ref[0:WIDTH_A, :], preferred_element_type=F32)
          + jnp.dot(ob_ref[...], wo_ref[WIDTH_A:WIDTH_A + WIDTH_B, :], preferred_element_type=F32))
    x1_ref[...] = x1
    ms = jnp.mean(x1 * x1, axis=-1, keepdims=True)
    h2 = x1 * lax.rsqrt(ms + NORM_EPS) * g_ref[...]
    h2_ref[:, 0:D_MODEL] = h2
    logits = jnp.dot(h2, wr_ref[...], preferred_element_type=F32, precision=lax.Precision.HIGHEST)
    lane = lax.broadcasted_iota(jnp.int32, (tm, LANES), 1)
    lane_f = lane.astype(F32)
    big = float(LANES)
    gmask = lane < N_GROUPS
    gl = jnp.where(gmask, logits, NEG_INF)
    gmax = jnp.max(gl, axis=1, keepdims=True)
    gsum = jnp.sum(jnp.where(gmask, jnp.exp(gl - gmax), 0.0), axis=1, keepdims=True)
    p_top = 1.0 / gsum
    g_star = jnp.min(jnp.where(gmask & (gl == gmax), lane_f, big), axis=1, keepdims=True)
    lo_lane = ROUTER_OFF + EXPERTS_PER_GROUP * g_star
    emask = (lane_f >= lo_lane) & (lane_f < lo_lane + EXPERTS_PER_GROUP)
    sel = jnp.where(emask, logits, NEG_INF)
    v1 = jnp.max(sel, axis=1, keepdims=True)
    i1 = jnp.min(jnp.where(emask & (sel == v1), lane_f, big), axis=1, keepdims=True)
    rest = emask & (lane_f != i1)
    sel2 = jnp.where(rest, logits, NEG_INF)
    v2 = jnp.max(sel2, axis=1, keepdims=True)
    i2 = jnp.min(jnp.where(rest & (sel2 == v2), lane_f, big), axis=1, keepdims=True)
    e2 = jnp.exp(v2 - v1)
    w1 = p_top / (1.0 + e2)
    w2 = p_top * e2 / (1.0 + e2)
    e1 = i1 - lo_lane
    e2x = i2 - lo_lane
    a = jnp.minimum(e1, e2x)
    b = jnp.maximum(e1, e2x)
    cls = g_star * N_PAIRS + a * (7.0 - a) * 0.5 + (b - a - 1.0)
    wa = jnp.where(e1 < e2x, w1, w2)
    wb = jnp.where(e1 < e2x, w2, w1)
    h2_ref[:, D_MODEL:EXT_WIDTH] = jnp.where(lane == 0, wa, jnp.where(lane == 1, wb, 0.0))
    row = lax.broadcasted_iota(jnp.int32, (tm, LANES), 0)
    spread = jnp.where(lane == (row & (LANES - 1)), cls, 0.0)
    cls_ref[0] = jnp.sum(spread.reshape(tm // LANES, LANES, LANES), axis=1).astype(jnp.int32)


def _outproj(oa, ob, x2, wo, g, wr, *, tm):
    n = x2.shape[0]
    return pl.pallas_call(
        _outproj_kernel,
        grid=(n // tm,),
        in_specs=[
            pl.BlockSpec((tm, WIDTH_A), lambda i: (i, 0)),
            pl.BlockSpec((tm, WIDTH_B), lambda i: (i, 0)),
            pl.BlockSpec((tm, D_MODEL), lambda i: (i, 0)),
            pl.BlockSpec(wo.shape, lambda i: (0, 0)),
            pl.BlockSpec((1, D_MODEL), lambda i: (0, 0)),
            pl.BlockSpec(wr.shape, lambda i: (0, 0)),
        ],
        out_specs=[
            pl.BlockSpec((tm, D_MODEL), lambda i: (i, 0)),
            pl.BlockSpec((tm, EXT_WIDTH), lambda i: (i, 0)),
            pl.BlockSpec((1, tm // LANES, LANES), lambda i: (i, 0, 0)),
        ],
        out_shape=[
            jax.ShapeDtypeStruct((n, D_MODEL), F32),
            jax.ShapeDtypeStruct((n, EXT_WIDTH), F32),
            jax.ShapeDtypeStruct((n // tm, tm // LANES, LANES), jnp.int32),
        ],
        compiler_params=pltpu.CompilerParams(dimension_semantics=("arbitrary",), vmem_limit_bytes=VMEM_LIMIT),
        name="outproj",
    )(oa, ob, x2, wo, g, wr)


def _sort_kernel(cls_ref, pos_ref, tile_ref, ends_ref, *, tile):
    cls = cls_ref[...]
    rows = cls.shape[0]
    upper = (lax.broadcasted_iota(jnp.int32, (LANES, LANES), 0)
             <= lax.broadcasted_iota(jnp.int32, (LANES, LANES), 1)).astype(BF16)
    earlier_rows = (lax.broadcasted_iota(jnp.int32, (rows, rows), 0)
                    > lax.broadcasted_iota(jnp.int32, (rows, rows), 1)).astype(BF16)
    ones = jnp.ones((LANES, LANES), BF16)
    lane8 = lax.broadcasted_iota(jnp.int32, tile_ref.shape, 1)
    tile_start = lane8.astype(F32) * tile
    off = jnp.zeros((1, 1), F32)
    pos = jnp.zeros(cls.shape, F32)
    tile_cls = jnp.zeros(tile_ref.shape, F32)
    ends = jnp.zeros(tile_ref.shape, F32)
    for c in range(N_CLASSES):
        hit = cls == c
        hot = jnp.where(hit, 1.0, 0.0).astype(BF16)
        in_row = jnp.dot(hot, upper, preferred_element_type=F32)
        row_tot = jnp.dot(hot, ones, preferred_element_type=F32)
        before = jnp.dot(earlier_rows, row_tot.astype(BF16), preferred_element_type=F32)
        count = before[rows - 1:rows, 0:1] + row_tot[rows - 1:rows, 0:1]
        pos = pos + jnp.where(hit, in_row - 1.0 + before + off, 0.0)
        off = off + jnp.ceil(count / tile) * tile
        tile_cls = tile_cls + jnp.where(tile_start >= off, 1.0, 0.0)
        ends = jnp.where(lane8 == c, off, ends)
    pos_ref[...] = pos.astype(jnp.int32)
    tile_ref[...] = tile_cls.astype(jnp.int32)
    ends_ref[...] = ends.astype(jnp.int32)


def _sort(cls2d, *, tile):
    meta = jax.ShapeDtypeStruct((8, LANES), jnp.int32)
    return pl.pallas_call(
        functools.partial(_sort_kernel, tile=tile),
        out_shape=[jax.ShapeDtypeStruct(cls2d.shape, jnp.int32), meta, meta],
        compiler_params=pltpu.CompilerParams(vmem_limit_bytes=VMEM_LIMIT),
        name="moe_sort",
    )(cls2d)


def _row_copy(src, src_row, dst, dst_row, sem, rows=1):
    return pltpu.make_async_copy(src.at[pl.ds(src_row, rows)], dst.at[pl.ds(dst_row, rows)], sem)


def _dispatch_kernel(pos_ref, ends_ref, h2_ref, xs_hbm, zero_scr, sem, *, ch, tile):
    g = pl.program_id(0)

    @pl.when(g == 0)
    def _zero_tails():
        zero_scr[...] = jnp.zeros_like(zero_scr)
        n_tiles = xs_hbm.shape[0] // tile
        min_used = n_tiles - N_CLASSES
        total = ends_ref[0, N_CLASSES - 1]
        for phase in ("start", "wait"):
            def zero_tile(row0):
                cp = pltpu.make_async_copy(zero_scr, xs_hbm.at[pl.ds(pl.multiple_of(row0, tile), tile)], sem)
                cp.start() if phase == "start" else cp.wait()

            for c in range(N_CLASSES):
                end = ends_ref[0, c]
                prev = ends_ref[0, c - 1] if c else 0
                pl.when(end > prev)(functools.partial(zero_tile, end - tile))
                pl.when((min_used + c) * tile >= total)(functools.partial(zero_tile, (min_used + c) * tile))

    def body(i, carry):
        for u in range(DMA_UNROLL):
            k = i * DMA_UNROLL + u
            _row_copy(h2_ref, k, xs_hbm, pos_ref[0, 0, k], sem).start(priority=u % DMA_THREADS)
        return carry

    lax.fori_loop(0, ch // DMA_UNROLL, body, 0)
    _row_copy(h2_ref, 0, xs_hbm, 0, sem, rows=ch).wait()


def _dispatch(pos3, ends, h2ext, *, n_rows, tile):
    n_chunks, _, ch = pos3.shape
    return pl.pallas_call(
        functools.partial(_dispatch_kernel, ch=ch, tile=tile),
        grid=(n_chunks,),
        in_specs=[
            pl.BlockSpec((1, 1, ch), lambda g: (g, 0, 0), memory_space=pltpu.SMEM),
            pl.BlockSpec(ends.shape, lambda g: (0, 0), memory_space=pltpu.SMEM),
            pl.BlockSpec((ch, EXT_WIDTH), lambda g: (g, 0)),
        ],
        out_specs=pl.BlockSpec(memory_space=pl.ANY),
        out_shape=jax.ShapeDtypeStruct((n_rows, EXT_WIDTH), F32),
        scratch_shapes=[pltpu.VMEM((tile, EXT_WIDTH), F32), pltpu.SemaphoreType.DMA(())],
        compiler_params=pltpu.CompilerParams(dimension_semantics=("arbitrary",), vmem_limit_bytes=VMEM_LIMIT),
        name="moe_dispatch",
    )(pos3, ends, h2ext)


def _experts_kernel(ea_ref, eb_ref, used_ref, xs_ref, wga, wua, wda, wgb, wub, wdb, ys_ref):
    del ea_ref, eb_ref

    @pl.when(pl.program_id(0) < used_ref[0])
    def _():
        x = xs_ref[:, 0:D_MODEL].astype(BF16)
        wts = xs_ref[:, D_MODEL:EXT_WIDTH]
        y = None
        for col, (wg, wu, wd) in enumerate(((wga, wua, wda), (wgb, wub, wdb))):
            gate = jnp.dot(x, wg[0], preferred_element_type=F32)
            up = jnp.dot(x, wu[0], preferred_element_type=F32)
            he = (gate / (1.0 + jnp.exp(-gate)) * up * wts[:, col:col + 1]).astype(BF16)
            t = jnp.dot(he, wd[0], preferred_element_type=F32)
            y = t if y is None else y + t
        ys_ref[...] = y

    @pl.when(pl.program_id(0) >= used_ref[0])
    def _():
        ys_ref[...] = jnp.zeros_like(ys_ref)


def _experts(ea, eb, used, xs, wg, wu, wd, *, tile):
    n_tiles = xs.shape[0] // tile
    rows = lambda j, ea, eb, used: (jnp.minimum(j, used[0] - 1), 0)
    first = lambda j, ea, eb, used: (ea[j], 0, 0)
    second = lambda j, ea, eb, used: (eb[j], 0, 0)
    up_spec = lambda m: pl.BlockSpec((1, D_MODEL, D_EXPERT), m)
    down_spec = lambda m: pl.BlockSpec((1, D_EXPERT, D_MODEL), m)
    return pl.pallas_call(
        _experts_kernel,
        grid_spec=pltpu.PrefetchScalarGridSpec(
            num_scalar_prefetch=3,
            grid=(n_tiles,),
            in_specs=[pl.BlockSpec((tile, EXT_WIDTH), rows),
                      up_spec(first), up_spec(first), down_spec(first),
                      up_spec(second), up_spec(second), down_spec(second)],
            out_specs=pl.BlockSpec((tile, D_MODEL), lambda j, ea, eb, used: (j, 0)),
        ),
        out_shape=jax.ShapeDtypeStruct((xs.shape[0], D_MODEL), F32),
        compiler_params=pltpu.CompilerParams(dimension_semantics=("arbitrary",), vmem_limit_bytes=VMEM_LIMIT),
        name="moe_experts",
    )(ea, eb, used, xs, wg, wu, wd, wg, wu, wd)


def _combine_kernel(pos_ref, x1_ref, g_ref, ys_hbm, out_ref, y_scr, sem):
    tm = x1_ref.shape[0]

    def body(i, carry):
        for u in range(DMA_UNROLL):
            k = i * DMA_UNROLL + u
            _row_copy(ys_hbm, pos_ref[0, 0, k], y_scr, k, sem).start(priority=u % DMA_THREADS)
        return carry

    lax.fori_loop(0, tm // DMA_UNROLL, body, 0)
    _row_copy(ys_hbm, 0, y_scr, 0, sem, rows=tm).wait()
    x2 = x1_ref[...] + y_scr[...]
    ms = jnp.mean(x2 * x2, axis=-1, keepdims=True)
    out_ref[...] = x2 * lax.rsqrt(ms + NORM_EPS) * g_ref[...]


def _combine(pos3, x1, g, ys):
    n_chunks, _, tm = pos3.shape
    return pl.pallas_call(
        _combine_kernel,
        grid=(n_chunks,),
        in_specs=[
            pl.BlockSpec((1, 1, tm), lambda i: (i, 0, 0), memory_space=pltpu.SMEM),
            pl.BlockSpec((tm, D_MODEL), lambda i: (i, 0)),
            pl.BlockSpec((1, D_MODEL), lambda i: (0, 0)),
            pl.BlockSpec(memory_space=pl.ANY),
        ],
        out_specs=pl.BlockSpec((tm, D_MODEL), lambda i: (i, 0)),
        out_shape=jax.ShapeDtypeStruct(x1.shape, F32),
        scratch_shapes=[pltpu.VMEM((tm, D_MODEL), F32), pltpu.SemaphoreType.DMA(())],
        compiler_params=pltpu.CompilerParams(dimension_semantics=("arbitrary",), vmem_limit_bytes=VMEM_LIMIT),
        name="moe_combine",
    )(pos3, x1, g, ys)


def _moe(h2ext, cls3, x1, wg, wu, wd, g, *, tile=MOE_TILE, dispatch_chunk=1024, combine_chunk=512):
    n = x1.shape[0]
    n_tiles = n // tile + N_CLASSES
    assert n_tiles <= LANES
    pos, tile_cls, ends = _sort(cls3.reshape(n // LANES, LANES), tile=tile)
    xs = _dispatch(pos.reshape(n // dispatch_chunk, 1, dispatch_chunk), ends, h2ext, n_rows=n_tiles * tile, tile=tile)
    used = ends[0, N_CLASSES - 1:N_CLASSES] // tile
    tc = tile_cls[0, :n_tiles]
    tc = jnp.where(jnp.arange(n_tiles) < used[0], tc, tc[jnp.maximum(used[0] - 1, 0)])
    pair_a = jnp.array([0, 0, 0, 1, 1, 2], jnp.int32)
    pair_b = jnp.array([1, 2, 3, 2, 3, 3], jnp.int32)
    ea = (tc // N_PAIRS) * EXPERTS_PER_GROUP + pair_a[tc % N_PAIRS]
    eb = (tc // N_PAIRS) * EXPERTS_PER_GROUP + pair_b[tc % N_PAIRS]
    ys = _experts(ea, eb, used, xs, wg, wu, wd, tile=tile)
    return _combine(pos.reshape(n // combine_chunk, 1, combine_chunk), x1, g, ys)


def _rotary_tables(seq):
    pos = np.arange(seq, dtype=np.float64)
    inv_freq = 1.0 / (ROPE_THETA ** (np.arange(0, ROT_DIM, 2, dtype=np.float64) / ROT_DIM))
    ang = pos[:, None] * inv_freq[None, :]
    cos, sin = np.cos(ang), np.sin(ang)
    zeros = np.zeros((seq, HEAD_DIM - ROT_DIM))
    zh = np.zeros((seq, ROT_HALF))
    cos_h = np.concatenate([cos, cos, np.ones((seq, HEAD_DIM - ROT_DIM))], axis=1)
    sa_h = np.concatenate([-sin, zh, zeros], axis=1)
    sb_h = np.concatenate([zh, sin, zeros], axis=1)
    tile = lambda t: jnp.asarray(np.concatenate([t, t], axis=1), F32)
    return tile(cos_h), tile(sa_h), tile(sb_h)


def kernel(x, attn_norm, w_in, b_forget, w_out, ffn_norm, w_group, w_expert, w_gate_e, w_up_e, w_down_e, final_norm):
    b, seq, d = x.shape
    assert d == D_MODEL and w_in.shape[0] == 1, "single-layer block"
    n = b * seq
    scale = HEAD_DIM ** -0.5 * LOG2E
    col_scale = jnp.ones((2 * QKV_WIDTH + HEADS_B,), F32)
    col_scale = col_scale.at[0:WIDTH_A].set(scale).at[QKV_WIDTH:QKV_WIDTH + WIDTH_B].set(scale)
    w = jnp.pad(w_in[0] * col_scale[None, :], ((0, 0), (0, LANES - HEADS_B))).astype(BF16)
    bf = jnp.pad(b_forget[0].astype(F32), (0, LANES - HEADS_B))[None, :]
    cosv, sa, sb = _rotary_tables(seq)
    x2 = x.reshape(n, d)

    qkva, qkvb, c = _inproj(x2, attn_norm[0][None, :], w, cosv, sa, sb, bf, seq=seq, tm=512)
    out_a = _dilated(qkva.reshape(b, seq, QKV_WIDTH))
    out_b = _fox(qkvb.reshape(b, seq, QKV_WIDTH), c.reshape(b, seq, LANES), tq=512)

    wr = jnp.pad(jnp.concatenate([w_group[0], w_expert[0]], axis=1).astype(F32),
                 ((0, 0), (0, LANES - N_GROUPS - N_EXPERTS)))
    x1, h2ext, cls3 = _outproj(out_a.reshape(n, WIDTH_A), out_b.reshape(n, WIDTH_B), x2,
                               w_out[0].astype(BF16), ffn_norm[0][None, :], wr, tm=512)
    out = _moe(h2ext, cls3, x1, w_gate_e[0].astype(BF16), w_up_e[0].astype(BF16), w_down_e[0].astype(BF16),
               final_norm[None, :])
    return out.reshape(b, seq, d)
```

```python
import functools

import jax
import jax.numpy as jnp
import numpy as np
from jax import lax
from jax.experimental import pallas as pl
from jax.experimental.pallas import tpu as pltpu

F32 = jnp.float32
BF16 = jnp.bfloat16

D_MODEL = 1024
HEAD_DIM = 64
HEADS_A = 8
HEADS_B = 8
WIDTH_A = HEADS_A * HEAD_DIM
WIDTH_B = HEADS_B * HEAD_DIM
QKV_WIDTH = 3 * WIDTH_A
DILATIONS = ((128, 1), (512, 4), (2048, 16))
ROT_DIM = HEAD_DIM // 4
ROT_HALF = ROT_DIM // 2
ROPE_THETA = 500000.0
N_GROUPS = 4
EXPERTS_PER_GROUP = 4
N_EXPERTS = N_GROUPS * EXPERTS_PER_GROUP
D_EXPERT = 512
NORM_EPS = 1e-6
NEG_INF = -1e30

LANES = 128
HEAD_PAIRS = WIDTH_A // LANES
WIN = 128
MAX_DIL = 16
PERM = MAX_DIL * MAX_DIL
KV_BLOCK = 256
N_PAIRS = EXPERTS_PER_GROUP * (EXPERTS_PER_GROUP - 1) // 2
N_CLASSES = N_GROUPS * N_PAIRS
MOE_TILE = 256
EXT_WIDTH = D_MODEL + LANES
LOGIT_MAX = 60.0
BRANCH_UNROLL = 4
DMA_THREADS = 2
ROUTER_OFF = N_GROUPS
VMEM_LIMIT = 56 * 1024 * 1024
LOG2E = 1.4426950408889634
BOUND_SLACK = 1.0 + 2.0 ** -7
BOUND_EPS = 2.0 ** -7
GAP_MAX = 64.0


def _split3(x):
    hi = x.astype(BF16)
    r1 = x - hi.astype(F32)
    mid = r1.astype(BF16)
    lo = (r1 - mid.astype(F32)).astype(BF16)
    return hi, mid, lo


def _inproj_kernel(x_ref, g_ref, w_ref, cos_ref, sa_ref, sb_ref, bf_ref,
                   qkva_ref, qkvb_ref, c_ref, h_scr, carry_scr, *, tiles_per_seq):
    i = pl.program_id(0)
    tm = x_ref.shape[0]
    x = x_ref[...]
    ms = jnp.mean(x * x, axis=-1, keepdims=True)
    h_scr[...] = (x * lax.rsqrt(ms + NORM_EPS) * g_ref[...]).astype(BF16)
    cosv = cos_ref[...]
    sa = sa_ref[...]
    sb = sb_ref[...]
    for g in range(6):
        p = jnp.dot(h_scr[...], w_ref[:, g * WIDTH_A:(g + 1) * WIDTH_A], preferred_element_type=F32)
        dst = qkva_ref if g < 3 else qkvb_ref
        col0 = (g % 3) * WIDTH_A
        if g < 2:
            for k in range(HEAD_PAIRS):
                t = p[:, k * LANES:(k + 1) * LANES]
                t = (t * cosv + pltpu.roll(t, LANES - ROT_HALF, 1) * sa + pltpu.roll(t, ROT_HALF, 1) * sb)
                dst[:, col0 + k * LANES:col0 + (k + 1) * LANES] = t.astype(BF16)
        else:
            dst[:, col0:col0 + WIDTH_A] = p.astype(BF16)
    z = jnp.dot(h_scr[...], w_ref[:, 2 * QKV_WIDTH:2 * QKV_WIDTH + LANES], preferred_element_type=F32) + bf_ref[...]
    lf = jnp.minimum(z, 0.0) - jnp.log1p(jnp.exp(-jnp.abs(z)))
    lane = lax.broadcasted_iota(jnp.int32, (tm, LANES), 1)
    lf = jnp.where(lane < HEADS_B, lf * LOG2E, 0.0)
    hi, mid, lo = _split3(lf)
    tri = (lax.broadcasted_iota(jnp.int32, (tm, tm), 0) >= lax.broadcasted_iota(jnp.int32, (tm, tm), 1)).astype(BF16)
    cs = (jnp.dot(tri, hi, preferred_element_type=F32) + jnp.dot(tri, mid, preferred_element_type=F32)
          + jnp.dot(tri, lo, preferred_element_type=F32))

    @pl.when(i % tiles_per_seq == 0)
    def _():
        carry_scr[...] = jnp.zeros_like(carry_scr)

    c = cs + carry_scr[0:1, :]
    c_ref[...] = c
    carry_scr[...] = jnp.broadcast_to(c[tm - 1:tm, :], carry_scr.shape)


def _inproj(x2, g, w, cosv, sa, sb, bf, *, seq, tm):
    n = x2.shape[0]
    kern = functools.partial(_inproj_kernel, tiles_per_seq=seq // tm)
    tps = seq // tm
    return pl.pallas_call(
        kern,
        grid=(n // tm,),
        in_specs=[
            pl.BlockSpec((tm, D_MODEL), lambda i: (i, 0)),
            pl.BlockSpec((1, D_MODEL), lambda i: (0, 0)),
            pl.BlockSpec(w.shape, lambda i: (0, 0)),
            pl.BlockSpec((tm, LANES), lambda i: (i % tps, 0)),
            pl.BlockSpec((tm, LANES), lambda i: (i % tps, 0)),
            pl.BlockSpec((tm, LANES), lambda i: (i % tps, 0)),
            pl.BlockSpec((1, LANES), lambda i: (0, 0)),
        ],
        out_specs=[
            pl.BlockSpec((tm, QKV_WIDTH), lambda i: (i, 0)),
            pl.BlockSpec((tm, QKV_WIDTH), lambda i: (i, 0)),
            pl.BlockSpec((tm, LANES), lambda i: (i, 0)),
        ],
        out_shape=[
            jax.ShapeDtypeStruct((n, QKV_WIDTH), BF16),
            jax.ShapeDtypeStruct((n, QKV_WIDTH), BF16),
            jax.ShapeDtypeStruct((n, LANES), F32),
        ],
        scratch_shapes=[pltpu.VMEM((tm, D_MODEL), BF16), pltpu.VMEM((8, LANES), F32)],
        compiler_params=pltpu.CompilerParams(dimension_semantics=("arbitrary",), vmem_limit_bytes=VMEM_LIMIT),
        name="inproj",
    )(x2, g, w, cosv, sa, sb, bf)


def _head_column(c_all, lane, head):
    return jnp.sum(jnp.where(lane == head, c_all, 0.0), axis=1, keepdims=True)


def _own_lanes(lane, hl):
    return (lane < HEAD_DIM) if hl == 0 else (lane >= HEAD_DIM)


def _own_sum(vals, lane, hl):
    return jnp.sum(jnp.where(_own_lanes(lane, hl), vals, 0.0), axis=1, keepdims=True)


def _norm_bound(qsq, ksq_max):
    return jnp.sqrt(qsq * ksq_max) * BOUND_SLACK + BOUND_EPS


def _fox_kernel(q_ref, k_ref, v_ref, cq_ref, ck_ref, o_ref,
                k0_scr, k1_scr, v0_scr, v1_scr, ksq_scr, acc0_scr, acc1_scr, s_scr, *, tq, seq, chunk):
    hp = pl.program_id(1)
    qi = pl.program_id(2)
    k_scr = (k0_scr, k1_scr)
    v_scr = (v0_scr, v1_scr)
    acc_scr = (acc0_scr, acc1_scr)

    def augmented(vals, cterm, lane, hl, key_side):
        hi, mid, lo = (t.astype(F32) for t in _split3(cterm))
        base = HEAD_DIM if hl == 0 else 0
        ones = (lane >= base + (0 if key_side else 3)) & (lane < base + (3 if key_side else 6))
        o = 3 if key_side else 0
        sgn = -1.0 if key_side else 1.0
        aug = jnp.where(lane == base + o, sgn * hi,
                        jnp.where(lane == base + o + 1, sgn * mid,
                                  jnp.where(lane == base + o + 2, sgn * lo,
                                            jnp.where(ones, 1.0, 0.0))))
        return jnp.where(_own_lanes(lane, hl), vals, aug).astype(BF16)

    @pl.when(qi == 0)
    def _build_keys():
        lane = lax.broadcasted_iota(jnp.int32, (chunk, LANES), 1)

        def body(ci, ksq):
            r0 = pl.multiple_of(ci * chunk, chunk)
            kc = k_ref[0, pl.ds(r0, chunk), :].astype(F32)
            vc = v_ref[0, pl.ds(r0, chunk), :].astype(F32)
            cc = ck_ref[0, pl.ds(r0, chunk), :]
            new = []
            for hl in range(2):
                k_scr[hl][pl.ds(r0, chunk), :] = augmented(kc, _head_column(cc, lane, 2 * hp + hl), lane, hl, True)
                one_lane = HEAD_DIM if hl == 0 else 0
                v_scr[hl][pl.ds(r0, chunk), :] = jnp.where(
                    _own_lanes(lane, hl), vc, jnp.where(lane == one_lane, 1.0, 0.0)).astype(BF16)
                new.append(jnp.maximum(ksq[hl], jnp.max(_own_sum(kc * kc, lane, hl), axis=0, keepdims=True)))
            return tuple(new)

        zero = jnp.zeros((1, 1), F32)
        ksq0, ksq1 = lax.fori_loop(0, seq // chunk, body, (zero, zero))
        lane8 = lax.broadcasted_iota(jnp.int32, ksq_scr.shape, 1)
        ksq_scr[...] = jnp.where(lane8 < HEAD_DIM, ksq0, ksq1)

    lane_q = lax.broadcasted_iota(jnp.int32, (tq, LANES), 1)
    q2 = q_ref[0].astype(F32)
    cq = cq_ref[0]
    kd = k_ref[0, pl.ds(pl.multiple_of(qi * tq, tq), tq), :].astype(F32)
    ksq_row = ksq_scr[0:1, :]
    qh = []
    gap = None
    for hl in range(2):
        bound = _norm_bound(_own_sum(q2 * q2, lane_q, hl), ksq_row[:, hl * HEAD_DIM:hl * HEAD_DIM + 1])
        diag = _own_sum(q2 * kd, lane_q, hl)
        g = jnp.max(bound - diag)
        gap = g if gap is None else jnp.maximum(gap, g)
        qh.append(augmented(q2, _head_column(cq, lane_q, 2 * hp + hl) - bound, lane_q, hl, False))
    n_diag = tq // KV_BLOCK
    n_full = qi * n_diag
    row_minus_col = (lax.broadcasted_iota(jnp.int32, (tq, KV_BLOCK), 0)
                     - lax.broadcasted_iota(jnp.int32, (tq, KV_BLOCK), 1))

    def causal_mask(s, d):
        return jnp.where(row_minus_col >= d * KV_BLOCK, s, NEG_INF)

    def logits(hl, j):
        r0 = pl.multiple_of(j * KV_BLOCK, KV_BLOCK)
        return lax.dot_general(qh[hl], k_scr[hl][pl.ds(r0, KV_BLOCK), :], (((1,), (1,)), ((), ())),
                               preferred_element_type=F32)

    def values(hl, j):
        return v_scr[hl][pl.ds(pl.multiple_of(j * KV_BLOCK, KV_BLOCK), KV_BLOCK), :]

    def finish(a0, a1):
        l0 = a0[:, HEAD_DIM:HEAD_DIM + 1]
        l1 = a1[:, 0:1]
        o_ref[0] = jnp.where(lane_q < HEAD_DIM, a0 / l0, a1 / l1).astype(o_ref.dtype)

    @pl.when(gap <= GAP_MAX)
    def _fast():
        acc0_scr[...] = jnp.zeros_like(acc0_scr)
        acc1_scr[...] = jnp.zeros_like(acc1_scr)

        def produce(j, slot):
            for hl in range(2):
                s_scr[slot, hl] = logits(hl, j)

        def consume(j, slot, diag):
            for hl in range(2):
                s = s_scr[slot, hl]
                if diag is not None:
                    s = causal_mask(s, diag)
                acc_scr[hl][...] += jnp.dot(jnp.exp2(s).astype(BF16), values(hl, j), preferred_element_type=F32)

        for d in range(n_diag):
            produce(d, d)

        def body(i, carry):
            for d in range(n_diag):
                j = i * n_diag + d
                consume(j, d, None)
                produce(j + n_diag, d)
            return carry

        lax.fori_loop(0, qi, body, 0)
        for d in range(n_diag):
            consume(n_full + d, d, d)
        finish(acc0_scr[...], acc1_scr[...])

    @pl.when(gap > GAP_MAX)
    def _general():
        def step(j, carry, diag):
            new = []
            for hl in range(2):
                m, a = carry[2 * hl:2 * hl + 2]
                s = logits(hl, j)
                if diag is not None:
                    s = causal_mask(s, diag)
                mn = jnp.maximum(m, jnp.max(s, axis=1, keepdims=True))
                a = jnp.exp2(m - mn) * a + jnp.dot(jnp.exp2(s - mn).astype(BF16), values(hl, j),
                                                  preferred_element_type=F32)
                new += [mn, a]
            return tuple(new)

        carry = (jnp.full((tq, 1), NEG_INF, F32), jnp.zeros((tq, LANES), F32)) * 2
        carry = lax.fori_loop(0, n_full, lambda j, c: step(j, c, None), carry)
        for d in range(n_diag):
            carry = step(n_full + d, carry, d)
        finish(carry[1], carry[3])


def _fox(qkvb, c, *, tq, chunk=512):
    b, seq, _ = qkvb.shape
    kern = functools.partial(_fox_kernel, tq=tq, seq=seq, chunk=chunk)
    return pl.pallas_call(
        kern,
        grid=(b, HEAD_PAIRS, seq // tq),
        in_specs=[
            pl.BlockSpec((1, tq, LANES), lambda bi, hp, qi: (bi, qi, hp)),
            pl.BlockSpec((1, seq, LANES), lambda bi, hp, qi: (bi, 0, HEAD_PAIRS + hp)),
            pl.BlockSpec((1, seq, LANES), lambda bi, hp, qi: (bi, 0, 2 * HEAD_PAIRS + hp)),
            pl.BlockSpec((1, tq, LANES), lambda bi, hp, qi: (bi, qi, 0)),
            pl.BlockSpec((1, seq, LANES), lambda bi, hp, qi: (bi, 0, 0)),
        ],
        out_specs=pl.BlockSpec((1, tq, LANES), lambda bi, hp, qi: (bi, qi, hp)),
        out_shape=jax.ShapeDtypeStruct((b, seq, WIDTH_B), BF16),
        scratch_shapes=[pltpu.VMEM((seq, LANES), BF16), pltpu.VMEM((seq, LANES), BF16),
                        pltpu.VMEM((seq, LANES), BF16), pltpu.VMEM((seq, LANES), BF16),
                        pltpu.VMEM((8, LANES), F32),
                        pltpu.VMEM((tq, LANES), F32), pltpu.VMEM((tq, LANES), F32),
                        pltpu.VMEM((tq // KV_BLOCK, 2, tq, KV_BLOCK), F32)],
        compiler_params=pltpu.CompilerParams(
            dimension_semantics=("arbitrary", "arbitrary", "arbitrary"), vmem_limit_bytes=VMEM_LIMIT),
        name="fox",
    )(qkvb, qkvb, qkvb, c, c)


def _dilated_general(q_ref, k_ref, v_ref, o_ref, q16, k16, v16, m_s, l_s, acc_s, on_s, lse_s, *, seq):
    sub = seq // MAX_DIL
    own0 = lax.broadcasted_iota(jnp.int32, (WIN, LANES), 1) < HEAD_DIM
    ri = lax.broadcasted_iota(jnp.int32, (PERM, PERM), 0)
    ci = lax.broadcasted_iota(jnp.int32, (PERM, PERM), 1)
    perm = (((ri >> 4) == (ci & 15)) & ((ri & 15) == (ci >> 4))).astype(BF16)

    qa = lax.broadcasted_iota(jnp.int32, (WIN, 2 * WIN), 0)
    kb_i = lax.broadcasted_iota(jnp.int32, (WIN, 2 * WIN), 1)
    cur = kb_i >= WIN
    kin = kb_i & (WIN - 1)
    dist_nat = qa - kin + jnp.where(cur, 0, WIN)
    band_nat = (dist_nat >= 0) & (dist_nat <= WIN)
    sq = ((qa & 31) << 2) + (qa >> 5)
    sk = ((kin & 31) << 2) + (kin >> 5)
    dist_4 = sq - sk + jnp.where(cur, 0, WIN)
    band_4 = (dist_4 >= 0) & (dist_4 <= WIN)

    def attend(qb, kb, vb, mask, state):
        qf = qb.astype(F32)
        new = []
        for hl in range(2):
            m, l, a = state[3 * hl:3 * hl + 3]
            own = own0 if hl == 0 else jnp.logical_not(own0)
            qh = jnp.where(own, qf, 0.0).astype(BF16)
            s = lax.dot_general(qh, kb, (((1,), (1,)), ((), ())), preferred_element_type=F32)
            s = jnp.where(mask, s, NEG_INF)
            mn = jnp.maximum(m, jnp.max(s, axis=1, keepdims=True))
            alpha = jnp.exp2(m - mn)
            p = jnp.exp2(s - mn)
            l = alpha * l + jnp.sum(p, axis=1, keepdims=True)
            a = alpha * a + jnp.dot(p.astype(BF16), vb, preferred_element_type=F32)
            new += [mn, l, a]
        return new

    def pack_state(st):
        m0, l0, a0, m1, l1, a1 = st
        return (jnp.where(own0, m0, m1), jnp.where(own0, l0, l1), jnp.where(own0, a0, a1))

    def deint(blk, carry):
        r0 = pl.multiple_of(blk * PERM, PERM)
        j0 = pl.multiple_of(blk * MAX_DIL, MAX_DIL)
        for src, dst in ((q_ref, q16), (k_ref, k16), (v_ref, v16)):
            y = jnp.dot(perm, src[0, pl.ds(r0, PERM), :], preferred_element_type=F32).astype(BF16)
            for r in range(MAX_DIL):
                dst[r, pl.ds(j0, MAX_DIL), :] = y[r * MAX_DIL:(r + 1) * MAX_DIL, :]
        return carry

    lax.fori_loop(0, seq // PERM, deint, 0)

    fresh = [jnp.full((WIN, 1), NEG_INF, F32), jnp.zeros((WIN, 1), F32), jnp.zeros((WIN, LANES), F32)] * 2
    nb16 = sub // WIN

    def d16_body(t, carry):
        r = t // nb16
        n = t % nb16
        c0 = pl.multiple_of(n * WIN, WIN)
        p0 = pl.multiple_of(jnp.maximum(n - 1, 0) * WIN, WIN)
        qb = q16[r, pl.ds(c0, WIN), :]
        kb = jnp.concatenate([k16[r, pl.ds(p0, WIN), :], k16[r, pl.ds(c0, WIN), :]], axis=0)
        vb = jnp.concatenate([v16[r, pl.ds(p0, WIN), :], v16[r, pl.ds(c0, WIN), :]], axis=0)
        mask = band_nat & (cur | (n > 0))
        mm, ll, aa = pack_state(attend(qb, kb, vb, mask, fresh))
        row = pl.multiple_of(r * sub + c0, WIN)
        m_s[pl.ds(row, WIN), :] = mm
        l_s[pl.ds(row, WIN), :] = ll
        acc_s[pl.ds(row, WIN), :] = aa
        return carry

    lax.fori_loop(0, MAX_DIL * nb16, d16_body, 0)

    nb4 = (seq // 4) // WIN
    ch = WIN // 4

    def d4_body(t, carry):
        r4 = t // nb4
        n = t % nb4
        c0 = pl.multiple_of(n * ch, ch)
        p0 = pl.multiple_of(jnp.maximum(n - 1, 0) * ch, ch)

        def gather(ref, j0):
            return [ref[r4 + 4 * q, pl.ds(j0, ch), :] for q in range(4)]

        qb = jnp.concatenate(gather(q16, c0), axis=0)
        kb = jnp.concatenate(gather(k16, p0) + gather(k16, c0), axis=0)
        vb = jnp.concatenate(gather(v16, p0) + gather(v16, c0), axis=0)
        rows = [pl.multiple_of((r4 + 4 * q) * sub + c0, ch) for q in range(4)]
        mm = jnp.concatenate([m_s[pl.ds(rw, ch), :] for rw in rows], axis=0)
        ll = jnp.concatenate([l_s[pl.ds(rw, ch), :] for rw in rows], axis=0)
        aa = jnp.concatenate([acc_s[pl.ds(rw, ch), :] for rw in rows], axis=0)
        state = [mm[:, 0:1], ll[:, 0:1], aa, mm[:, HEAD_DIM:HEAD_DIM + 1], ll[:, HEAD_DIM:HEAD_DIM + 1], aa]
        mask = band_4 & (cur | (n > 0))
        mm, ll, aa = pack_state(attend(qb, kb, vb, mask, state))
        for q, rw in enumerate(rows):
            m_s[pl.ds(rw, ch), :] = mm[q * ch:(q + 1) * ch, :]
            l_s[pl.ds(rw, ch), :] = ll[q * ch:(q + 1) * ch, :]
            acc_s[pl.ds(rw, ch), :] = aa[q * ch:(q + 1) * ch, :]
        return carry

    lax.fori_loop(0, 4 * nb4, d4_body, 0)

    def renat(blk, carry):
        j0 = pl.multiple_of(blk * MAX_DIL, MAX_DIL)
        r0 = pl.multiple_of(blk * PERM, PERM)
        rows = [pl.multiple_of(r * sub + j0, MAX_DIL) for r in range(MAX_DIL)]
        mm = jnp.concatenate([m_s[pl.ds(rw, MAX_DIL), :] for rw in rows], axis=0)
        ll = jnp.concatenate([l_s[pl.ds(rw, MAX_DIL), :] for rw in rows], axis=0)
        aa = jnp.concatenate([acc_s[pl.ds(rw, MAX_DIL), :] for rw in rows], axis=0)
        o = (aa / ll).astype(BF16)
        hi, mid, lo = _split3(mm + jnp.log2(ll))
        on_s[pl.ds(r0, PERM), :] = jnp.dot(perm, o, preferred_element_type=F32)
        lse_s[pl.ds(r0, PERM), :] = (jnp.dot(perm, hi, preferred_element_type=F32)
                                     + jnp.dot(perm, mid, preferred_element_type=F32)
                                     + jnp.dot(perm, lo, preferred_element_type=F32))
        return carry

    lax.fori_loop(0, seq // PERM, renat, 0)

    def d1_body(n, carry):
        c0 = pl.multiple_of(n * WIN, WIN)
        p0 = pl.multiple_of(jnp.maximum(n - 1, 0) * WIN, WIN)
        qb = q_ref[0, pl.ds(c0, WIN), :]
        kb = jnp.concatenate([k_ref[0, pl.ds(p0, WIN), :], k_ref[0, pl.ds(c0, WIN), :]], axis=0)
        vb = jnp.concatenate([v_ref[0, pl.ds(p0, WIN), :], v_ref[0, pl.ds(c0, WIN), :]], axis=0)
        lse = lse_s[pl.ds(c0, WIN), :]
        o = on_s[pl.ds(c0, WIN), :]
        one = jnp.ones((WIN, 1), F32)
        state = [lse[:, 0:1], one, o, lse[:, HEAD_DIM:HEAD_DIM + 1], one, o]
        mask = band_nat & (cur | (n > 0))
        m0, l0, a0, m1, l1, a1 = attend(qb, kb, vb, mask, state)
        o_ref[0, pl.ds(c0, WIN), :] = jnp.where(own0, a0 / l0, a1 / l1).astype(o_ref.dtype)
        return carry

    lax.fori_loop(0, seq // WIN, d1_body, 0)


def _dilated_fast(o_ref, qn, kn, vn, qr, kr, vr, fr, fn, s_scr, mask_scr, *, seq):
    sub = seq // MAX_DIL
    own0 = lax.broadcasted_iota(jnp.int32, (WIN, LANES), 1) < HEAD_DIM
    head_mask = [jnp.where(own0, 1.0, 0.0).astype(BF16), jnp.where(own0, 0.0, 1.0).astype(BF16)]
    ri = lax.broadcasted_iota(jnp.int32, (PERM, PERM), 0)
    ci = lax.broadcasted_iota(jnp.int32, (PERM, PERM), 1)
    perm = (((ri >> 4) == (ci & 15)) & ((ri & 15) == (ci >> 4))).astype(BF16)

    qa = lax.broadcasted_iota(jnp.int32, (WIN, 2 * WIN), 0)
    kb_i = lax.broadcasted_iota(jnp.int32, (WIN, 2 * WIN), 1)
    cur = kb_i >= WIN
    kin = kb_i & (WIN - 1)
    dist_nat = qa - kin + jnp.where(cur, 0, WIN)
    band_nat = (dist_nat >= 0) & (dist_nat <= WIN)
    sq = ((qa & 31) << 2) + (qa >> 5)
    sk = ((kin & 31) << 2) + (kin >> 5)
    dist_4 = sq - sk + jnp.where(cur, 0, WIN)
    band_4 = (dist_4 >= 0) & (dist_4 <= WIN)
    for i, band in enumerate((band_nat, band_4)):
        mask_scr[2 * i] = jnp.where(band & cur, 1.0, 0.0).astype(BF16)
        mask_scr[2 * i + 1] = jnp.where(band, 1.0, 0.0).astype(BF16)

    def deint(blk, carry):
        r0 = pl.multiple_of(blk * PERM, PERM)
        j0 = pl.multiple_of(blk * MAX_DIL, MAX_DIL)
        for src, dst in ((qn[0], qr[0]), (kn[0], kr[0]), (vn[0], vr[0]), (vn[1], vr[1])):
            y = jnp.dot(perm, src[pl.ds(r0, PERM), :], preferred_element_type=F32).astype(BF16)
            for r in range(MAX_DIL):
                dst[r, pl.ds(j0, MAX_DIL), :] = y[r * MAX_DIL:(r + 1) * MAX_DIL, :]
        return carry

    lax.fori_loop(0, seq // PERM, deint, 0)

    def run_branch(nblocks, block_in_seq, fetch, mask_base, sink):
        def produce(t, pair, u):
            for hl in range(2):
                qb, kb = fetch(t, hl, True)
                s_scr[pair, u, hl] = lax.dot_general(qb * head_mask[hl], kb, (((1,), (1,)), ((), ())),
                                                     preferred_element_type=F32)

        def consume(t, pair, u):
            mk = mask_scr[mask_base + jnp.minimum(block_in_seq(t), 1)]
            pv = []
            for hl in range(2):
                p = jnp.exp2(s_scr[pair, u, hl]).astype(BF16) * mk
                pv.append(jnp.dot(p, fetch(t, hl, False), preferred_element_type=F32))
            sink(t, pv)

        for u in range(BRANCH_UNROLL):
            produce(u, 0, u)

        def body(i, carry):
            pair = i & 1
            for u in range(BRANCH_UNROLL):
                consume(BRANCH_UNROLL * i + u, pair, u)
            for u in range(BRANCH_UNROLL):
                produce(jnp.minimum(BRANCH_UNROLL * (i + 1) + u, nblocks - 1), 1 - pair, u)
            return carry

        lax.fori_loop(0, nblocks // BRANCH_UNROLL, body, 0)

    def prev_cur(ref_block, n, size):
        c0 = pl.multiple_of(n * size, size)
        p0 = pl.multiple_of(jnp.maximum(n - 1, 0) * size, size)
        return ref_block(p0) + ref_block(c0)

    nb16 = sub // WIN

    def fetch16(t, hl, qk):
        r, n = t // nb16, t % nb16
        if qk:
            return (qr[hl][r, pl.ds(pl.multiple_of(n * WIN, WIN), WIN), :],
                    jnp.concatenate(prev_cur(lambda s: [kr[hl][r, pl.ds(s, WIN), :]], n, WIN), axis=0))
        return jnp.concatenate(prev_cur(lambda s: [vr[hl][r, pl.ds(s, WIN), :]], n, WIN), axis=0)

    def sink16(t, pv):
        row = pl.multiple_of(t * WIN, WIN)
        for hl in range(2):
            fr[hl][pl.ds(row, WIN), :] = pv[hl]

    run_branch(MAX_DIL * nb16, lambda t: t % nb16, fetch16, 0, sink16)

    nb4 = (seq // 4) // WIN
    ch = WIN // 4

    def fetch4(t, hl, qk):
        r4, n = t // nb4, t % nb4
        gather = lambda ref: (lambda s: [ref[r4 + 4 * q, pl.ds(s, ch), :] for q in range(4)])
        if qk:
            return (jnp.concatenate(gather(qr[hl])(pl.multiple_of(n * ch, ch)), axis=0),
                    jnp.concatenate(prev_cur(gather(kr[hl]), n, ch), axis=0))
        return jnp.concatenate(prev_cur(gather(vr[hl]), n, ch), axis=0)

    def sink4(t, pv):
        r4, n = t // nb4, t % nb4
        for q in range(4):
            row = pl.multiple_of((r4 + 4 * q) * sub + n * ch, ch)
            for hl in range(2):
                fr[hl][pl.ds(row, ch), :] += pv[hl][q * ch:(q + 1) * ch, :]

    run_branch(4 * nb4, lambda t: t % nb4, fetch4, 2, sink4)

    def renat(blk, carry):
        j0 = pl.multiple_of(blk * MAX_DIL, MAX_DIL)
        r0 = pl.multiple_of(blk * PERM, PERM)
        for hl in range(2):
            a = jnp.concatenate([fr[hl][pl.ds(pl.multiple_of(r * sub + j0, MAX_DIL), MAX_DIL), :]
                                 for r in range(MAX_DIL)], axis=0)
            hi = a.astype(BF16)
            lo = (a - hi.astype(F32)).astype(BF16)
            fn[hl][pl.ds(r0, PERM), :] = (jnp.dot(perm, hi, preferred_element_type=F32)
                                          + jnp.dot(perm, lo, preferred_element_type=F32))
        return carry

    lax.fori_loop(0, seq // PERM, renat, 0)

    def fetch1(t, hl, qk):
        if qk:
            return (qn[hl][pl.ds(pl.multiple_of(t * WIN, WIN), WIN), :],
                    jnp.concatenate(prev_cur(lambda s: [kn[hl][pl.ds(s, WIN), :]], t, WIN), axis=0))
        return jnp.concatenate(prev_cur(lambda s: [vn[hl][pl.ds(s, WIN), :]], t, WIN), axis=0)

    def sink1(t, pv):
        row = pl.multiple_of(t * WIN, WIN)
        t0 = fn[0][pl.ds(row, WIN), :] + pv[0]
        t1 = fn[1][pl.ds(row, WIN), :] + pv[1]
        o_ref[0, pl.ds(row, WIN), :] = jnp.where(own0, t0 / t0[:, HEAD_DIM:HEAD_DIM + 1],
                                                 t1 / t1[:, 0:1]).astype(o_ref.dtype)

    run_branch(seq // WIN, lambda t: t, fetch1, 0, sink1)


def _dilated_kernel(q_ref, k_ref, v_ref, o_ref, vn0, vn1, qr, kr, vr0, vr1,
                    f0, f1, f2, f3, f4, s_scr, mask_scr, *, seq, chunk):
    lane = lax.broadcasted_iota(jnp.int32, (chunk, LANES), 1)
    vn = (vn0, vn1)

    def stats(ci, sq_max):
        rows = pl.ds(pl.multiple_of(ci * chunk, chunk), chunk)
        qc = q_ref[0, rows, :].astype(F32)
        kc = k_ref[0, rows, :].astype(F32)
        vc = v_ref[0, rows, :].astype(F32)
        new = []
        for hl in range(2):
            ones_lane = HEAD_DIM if hl == 0 else 0
            vn[hl][rows, :] = jnp.where(_own_lanes(lane, hl), vc, jnp.where(lane == ones_lane, 1.0, 0.0)).astype(BF16)
            for i, t in enumerate((qc, kc)):
                new.append(jnp.maximum(sq_max[2 * hl + i], jnp.max(_own_sum(t * t, lane, hl), axis=0, keepdims=True)))
        return tuple(new)

    sq_max = lax.fori_loop(0, seq // chunk, stats, (jnp.zeros((1, 1), F32),) * 4)
    worst = jnp.max(jnp.maximum(_norm_bound(sq_max[0], sq_max[1]), _norm_bound(sq_max[2], sq_max[3])))

    @pl.when(worst <= LOGIT_MAX)
    def _fast():
        qn, kn = q_ref.at[0], k_ref.at[0]
        _dilated_fast(o_ref, (qn, qn), (kn, kn), vn, (qr, qr), (kr, kr), (vr0, vr1), (f0, f1), (f2, f3),
                      s_scr, mask_scr, seq=seq)

    @pl.when(worst > LOGIT_MAX)
    def _general():
        _dilated_general(q_ref, k_ref, v_ref, o_ref, qr, kr, vr0, f0, f1, f2, f3, f4, seq=seq)


def _dilated(qkva, *, chunk=512):
    b, seq, _ = qkva.shape
    sub = seq // MAX_DIL
    kern = functools.partial(_dilated_kernel, seq=seq, chunk=chunk)
    return pl.pallas_call(
        kern,
        grid=(b, HEAD_PAIRS),
        in_specs=[
            pl.BlockSpec((1, seq, LANES), lambda bi, hp: (bi, 0, hp)),
            pl.BlockSpec((1, seq, LANES), lambda bi, hp: (bi, 0, HEAD_PAIRS + hp)),
            pl.BlockSpec((1, seq, LANES), lambda bi, hp: (bi, 0, 2 * HEAD_PAIRS + hp)),
        ],
        out_specs=pl.BlockSpec((1, seq, LANES), lambda bi, hp: (bi, 0, hp)),
        out_shape=jax.ShapeDtypeStruct((b, seq, WIDTH_A), BF16),
        scratch_shapes=(
            [pltpu.VMEM((seq, LANES), BF16)] * 2
            + [pltpu.VMEM((MAX_DIL, sub, LANES), BF16)] * 4
            + [pltpu.VMEM((seq, LANES), F32)] * 5
            + [pltpu.VMEM((2, BRANCH_UNROLL, 2, WIN, 2 * WIN), F32), pltpu.VMEM((4, WIN, 2 * WIN), BF16)]
        ),
        compiler_params=pltpu.CompilerParams(
            dimension_semantics=("arbitrary", "arbitrary"), vmem_limit_bytes=VMEM_LIMIT),
        name="dilated",
    )(qkva, qkva, qkva)


def _outproj_kernel(oa_ref, ob_ref, x_ref, wo_ref, g_ref, wr_ref, x1_ref, h2_ref, cls_ref):
    tm = x_ref.shape[0]
    x1 = (x_ref[...]
          + jnp.dot(oa_ref[...], wo_ref[0:WIDTH_A, :], preferred_element_type=F32)
          + jnp.dot(ob_ref[...], wo_ref[WIDTH_A:WIDTH_A + WIDTH_B, :], preferred_element_type=F32))
    x1_ref[...] = x1
    ms = jnp.mean(x1 * x1, axis=-1, keepdims=True)
    h2 = x1 * lax.rsqrt(ms + NORM_EPS) * g_ref[...]
    h2_ref[:, 0:D_MODEL] = h2
    logits = jnp.dot(h2, wr_ref[...], preferred_element_type=F32, precision=lax.Precision.HIGHEST)
    lane = lax.broadcasted_iota(jnp.int32, (tm, LANES), 1)
    lane_f = lane.astype(F32)
    big = float(LANES)
    gmask = lane < N_GROUPS
    gl = jnp.where(gmask, logits, NEG_INF)
    gmax = jnp.max(gl, axis=1, keepdims=True)
    gsum = jnp.sum(jnp.where(gmask, jnp.exp(gl - gmax), 0.0), axis=1, keepdims=True)
    p_top = 1.0 / gsum
    g_star = jnp.min(jnp.where(gmask & (gl == gmax), lane_f, big), axis=1, keepdims=True)
    lo_lane = ROUTER_OFF + EXPERTS_PER_GROUP * g_star
    emask = (lane_f >= lo_lane) & (lane_f < lo_lane + EXPERTS_PER_GROUP)
    sel = jnp.where(emask, logits, NEG_INF)
    v1 = jnp.max(sel, axis=1, keepdims=True)
    i1 = jnp.min(jnp.where(emask & (sel == v1), lane_f, big), axis=1, keepdims=True)
    rest = emask & (lane_f != i1)
    sel2 = jnp.where(rest, logits, NEG_INF)
    v2 = jnp.max(sel2, axis=1, keepdims=True)
    i2 = jnp.min(jnp.where(rest & (sel2 == v2), lane_f, big), axis=1, keepdims=True)
    e2 = jnp.exp(v2 - v1)
    w1 = p_top / (1.0 + e2)
    w2 = p_top * e2 / (1.0 + e2)
    e1 = i1 - lo_lane
    e2x = i2 - lo_lane
    a = jnp.minimum(e1, e2x)
    b = jnp.maximum(e1, e2x)
    cls = g_star * N_PAIRS + a * (7.0 - a) * 0.5 + (b - a - 1.0)
    wa = jnp.where(e1 < e2x, w1, w2)
    wb = jnp.where(e1 < e2x, w2, w1)
    h2_ref[:, D_MODEL:EXT_WIDTH] = jnp.where(lane == 0, wa, jnp.where(lane == 1, wb, 0.0))
    row = lax.broadcasted_iota(jnp.int32, (tm, LANES), 0)
    spread = jnp.where(lane == (row & (LANES - 1)), cls, 0.0)
    cls_ref[0] = jnp.sum(spread.reshape(tm // LANES, LANES, LANES), axis=1).astype(jnp.int32)


def _outproj(oa, ob, x2, wo, g, wr, *, tm):
    n = x2.shape[0]
    return pl.pallas_call(
        _outproj_kernel,
        grid=(n // tm,),
        in_specs=[
            pl.BlockSpec((tm, WIDTH_A), lambda i: (i, 0)),
            pl.BlockSpec((tm, WIDTH_B), lambda i: (i, 0)),
            pl.BlockSpec((tm, D_MODEL), lambda i: (i, 0)),
            pl.BlockSpec(wo.shape, lambda i: (0, 0)),
            pl.BlockSpec((1, D_MODEL), lambda i: (0, 0)),
            pl.BlockSpec(wr.shape, lambda i: (0, 0)),
        ],
        out_specs=[
            pl.BlockSpec((tm, D_MODEL), lambda i: (i, 0)),
            pl.BlockSpec((tm, EXT_WIDTH), lambda i: (i, 0)),
            pl.BlockSpec((1, tm // LANES, LANES), lambda i: (i, 0, 0)),
        ],
        out_shape=[
            jax.ShapeDtypeStruct((n, D_MODEL), F32),
            jax.ShapeDtypeStruct((n, EXT_WIDTH), F32),
            jax.ShapeDtypeStruct((n // tm, tm // LANES, LANES), jnp.int32),
        ],
        compiler_params=pltpu.CompilerParams(dimension_semantics=("arbitrary",), vmem_limit_bytes=VMEM_LIMIT),
        name="outproj",
    )(oa, ob, x2, wo, g, wr)


def _sort_kernel(cls_ref, pos_ref, tile_ref, ends_ref, *, tile):
    cls = cls_ref[...]
    rows = cls.shape[0]
    upper = (lax.broadcasted_iota(jnp.int32, (LANES, LANES), 0)
             <= lax.broadcasted_iota(jnp.int32, (LANES, LANES), 1)).astype(BF16)
    earlier_rows = (lax.broadcasted_iota(jnp.int32, (rows, rows), 0)
                    > lax.broadcasted_iota(jnp.int32, (rows, rows), 1)).astype(BF16)
    ones = jnp.ones((LANES, LANES), BF16)
    lane8 = lax.broadcasted_iota(jnp.int32, tile_ref.shape, 1)
    tile_start = lane8.astype(F32) * tile
    off = jnp.zeros((1, 1), F32)
    pos = jnp.zeros(cls.shape, F32)
    tile_cls = jnp.zeros(tile_ref.shape, F32)
    ends = jnp.zeros(tile_ref.shape, F32)
    for c in range(N_CLASSES):
        hit = cls == c
        hot = jnp.where(hit, 1.0, 0.0).astype(BF16)
        in_row = jnp.dot(hot, upper, preferred_element_type=F32)
        row_tot = jnp.dot(hot, ones, preferred_element_type=F32)
        before = jnp.dot(earlier_rows, row_tot.astype(BF16), preferred_element_type=F32)
        count = before[rows - 1:rows, 0:1] + row_tot[rows - 1:rows, 0:1]
        pos = pos + jnp.where(hit, in_row - 1.0 + before + off, 0.0)
        off = off + jnp.ceil(count / tile) * tile
        tile_cls = tile_cls + jnp.where(tile_start >= off, 1.0, 0.0)
        ends = jnp.where(lane8 == c, off, ends)
    pos_ref[...] = pos.astype(jnp.int32)
    tile_ref[...] = tile_cls.astype(jnp.int32)
    ends_ref[...] = ends.astype(jnp.int32)


def _sort(cls2d, *, tile):
    meta = jax.ShapeDtypeStruct((8, LANES), jnp.int32)
    return pl.pallas_call(
        functools.partial(_sort_kernel, tile=tile),
        out_shape=[jax.ShapeDtypeStruct(cls2d.shape, jnp.int32), meta, meta],
        compiler_params=pltpu.CompilerParams(vmem_limit_bytes=VMEM_LIMIT),
        name="moe_sort",
    )(cls2d)


def _row_copy(src, src_row, dst, dst_row, sem, rows=1):
    return pltpu.make_async_copy(src.at[pl.ds(src_row, rows)], dst.at[pl.ds(dst_row, rows)], sem)


def _dispatch_kernel(pos_ref, ends_ref, h2_ref, xs_hbm, zero_scr, sem, *, ch, tile):
    g = pl.program_id(0)

    @pl.when(g == 0)
    def _zero_tails():
        zero_scr[...] = jnp.zeros_like(zero_scr)
        n_tiles = xs_hbm.shape[0] // tile
        min_used = n_tiles - N_CLASSES
        total = ends_ref[0, N_CLASSES - 1]
        for phase in ("start", "wait"):
            def zero_tile(row0):
                cp = pltpu.make_async_copy(zero_scr, xs_hbm.at[pl.ds(pl.multiple_of(row0, tile), tile)], sem)
                cp.start() if phase == "start" else cp.wait()

            for c in range(N_CLASSES):
                end = ends_ref[0, c]
                prev = ends_ref[0, c - 1] if c else 0
                pl.when(end > prev)(functools.partial(zero_tile, end - tile))
                pl.when((min_used + c) * tile >= total)(functools.partial(zero_tile, (min_used + c) * tile))

    for k in range(ch):
        _row_copy(h2_ref, k, xs_hbm, pos_ref[0, 0, k], sem).start(priority=k % DMA_THREADS)
    _row_copy(h2_ref, 0, xs_hbm, 0, sem, rows=ch).wait()


def _dispatch(pos3, ends, h2ext, *, n_rows, tile):
    n_chunks, _, ch = pos3.shape
    return pl.pallas_call(
        functools.partial(_dispatch_kernel, ch=ch, tile=tile),
        grid=(n_chunks,),
        in_specs=[
            pl.BlockSpec((1, 1, ch), lambda g: (g, 0, 0), memory_space=pltpu.SMEM),
            pl.BlockSpec(ends.shape, lambda g: (0, 0), memory_space=pltpu.SMEM),
            pl.BlockSpec((ch, EXT_WIDTH), lambda g: (g, 0)),
        ],
        out_specs=pl.BlockSpec(memory_space=pl.ANY),
        out_shape=jax.ShapeDtypeStruct((n_rows, EXT_WIDTH), F32),
        scratch_shapes=[pltpu.VMEM((tile, EXT_WIDTH), F32), pltpu.SemaphoreType.DMA(())],
        compiler_params=pltpu.CompilerParams(dimension_semantics=("arbitrary",), vmem_limit_bytes=VMEM_LIMIT),
        name="moe_dispatch",
    )(pos3, ends, h2ext)


def _experts_kernel(ea_ref, eb_ref, used_ref, xs_ref, wga, wua, wda, wgb, wub, wdb, ys_ref):
    del ea_ref, eb_ref

    @pl.when(pl.program_id(0) < used_ref[0])
    def _():
        x = xs_ref[:, 0:D_MODEL].astype(BF16)
        wts = xs_ref[:, D_MODEL:EXT_WIDTH]
        y = None
        for col, (wg, wu, wd) in enumerate(((wga, wua, wda), (wgb, wub, wdb))):
            gate = jnp.dot(x, wg[0], preferred_element_type=F32)
            up = jnp.dot(x, wu[0], preferred_element_type=F32)
            he = (gate / (1.0 + jnp.exp(-gate)) * up * wts[:, col:col + 1]).astype(BF16)
            t = jnp.dot(he, wd[0], preferred_element_type=F32)
            y = t if y is None else y + t
        ys_ref[...] = y

    @pl.when(pl.program_id(0) >= used_ref[0])
    def _():
        ys_ref[...] = jnp.zeros_like(ys_ref)


def _experts(ea, eb, used, xs, wg, wu, wd, *, tile):
    n_tiles = xs.shape[0] // tile
    rows = lambda j, ea, eb, used: (jnp.minimum(j, used[0] - 1), 0)
    first = lambda j, ea, eb, used: (ea[j], 0, 0)
    second = lambda j, ea, eb, used: (eb[j], 0, 0)
    up_spec = lambda m: pl.BlockSpec((1, D_MODEL, D_EXPERT), m)
    down_spec = lambda m: pl.BlockSpec((1, D_EXPERT, D_MODEL), m)
    return pl.pallas_call(
        _experts_kernel,
        grid_spec=pltpu.PrefetchScalarGridSpec(
            num_scalar_prefetch=3,
            grid=(n_tiles,),
            in_specs=[pl.BlockSpec((tile, EXT_WIDTH), rows),
                      up_spec(first), up_spec(first), down_spec(first),
                      up_spec(second), up_spec(second), down_spec(second)],
            out_specs=pl.BlockSpec((tile, D_MODEL), lambda j, ea, eb, used: (j, 0)),
        ),
        out_shape=jax.ShapeDtypeStruct((xs.shape[0], D_MODEL), F32),
        compiler_params=pltpu.CompilerParams(dimension_semantics=("arbitrary",), vmem_limit_bytes=VMEM_LIMIT),
        name="moe_experts",
    )(ea, eb, used, xs, wg, wu, wd, wg, wu, wd)


def _combine_kernel(pos_ref, x1_ref, g_ref, ys_hbm, out_ref, y_scr, sem):
    tm = x1_ref.shape[0]

    for k in range(tm):
        _row_copy(ys_hbm, pos_ref[0, 0, k], y_scr, k, sem).start(priority=k % DMA_THREADS)
    _row_copy(ys_hbm, 0, y_scr, 0, sem, rows=tm).wait()
    x2 = x1_ref[...] + y_scr[...]
    ms = jnp.mean(x2 * x2, axis=-1, keepdims=True)
    out_ref[...] = x2 * lax.rsqrt(ms + NORM_EPS) * g_ref[...]


def _combine(pos3, x1, g, ys):
    n_chunks, _, tm = pos3.shape
    return pl.pallas_call(
        _combine_kernel,
        grid=(n_chunks,),
        in_specs=[
            pl.BlockSpec((1, 1, tm), lambda i: (i, 0, 0), memory_space=pltpu.SMEM),
            pl.BlockSpec((tm, D_MODEL), lambda i: (i, 0)),
            pl.BlockSpec((1, D_MODEL), lambda i: (0, 0)),
            pl.BlockSpec(memory_space=pl.ANY),
        ],
        out_specs=pl.BlockSpec((tm, D_MODEL), lambda i: (i, 0)),
        out_shape=jax.ShapeDtypeStruct(x1.shape, F32),
        scratch_shapes=[pltpu.VMEM((tm, D_MODEL), F32), pltpu.SemaphoreType.DMA(())],
        compiler_params=pltpu.CompilerParams(dimension_semantics=("arbitrary",), vmem_limit_bytes=VMEM_LIMIT),
        name="moe_combine",
    )(pos3, x1, g, ys)


def _moe(h2ext, cls3, x1, wg, wu, wd, g, *, tile=MOE_TILE, dispatch_chunk=512, combine_chunk=512):
    n = x1.shape[0]
    n_tiles = n // tile + N_CLASSES
    assert n_tiles <= LANES
    pos, tile_cls, ends = _sort(cls3.reshape(n // LANES, LANES), tile=tile)
    xs = _dispatch(pos.reshape(n // dispatch_chunk, 1, dispatch_chunk), ends, h2ext, n_rows=n_tiles * tile, tile=tile)
    used = ends[0, N_CLASSES - 1:N_CLASSES] // tile
    tc = tile_cls[0, :n_tiles]
    tc = jnp.where(jnp.arange(n_tiles) < used[0], tc, tc[jnp.maximum(used[0] - 1, 0)])
    pair_a = jnp.array([0, 0, 0, 1, 1, 2], jnp.int32)
    pair_b = jnp.array([1, 2, 3, 2, 3, 3], jnp.int32)
    ea = (tc // N_PAIRS) * EXPERTS_PER_GROUP + pair_a[tc % N_PAIRS]
    eb = (tc // N_PAIRS) * EXPERTS_PER_GROUP + pair_b[tc % N_PAIRS]
    ys = _experts(ea, eb, used, xs, wg, wu, wd, tile=tile)
    return _combine(pos.reshape(n // combine_chunk, 1, combine_chunk), x1, g, ys)


def _rotary_tables(seq):
    pos = np.arange(seq, dtype=np.float64)
    inv_freq = 1.0 / (ROPE_THETA ** (np.arange(0, ROT_DIM, 2, dtype=np.float64) / ROT_DIM))
    ang = pos[:, None] * inv_freq[None, :]
    cos, sin = np.cos(ang), np.sin(ang)
    zeros = np.zeros((seq, HEAD_DIM - ROT_DIM))
    zh = np.zeros((seq, ROT_HALF))
    cos_h = np.concatenate([cos, cos, np.ones((seq, HEAD_DIM - ROT_DIM))], axis=1)
    sa_h = np.concatenate([-sin, zh, zeros], axis=1)
    sb_h = np.concatenate([zh, sin, zeros], axis=1)
    tile = lambda t: jnp.asarray(np.concatenate([t, t], axis=1), F32)
    return tile(cos_h), tile(sa_h), tile(sb_h)


def kernel(x, attn_norm, w_in, b_forget, w_out, ffn_norm, w_group, w_expert, w_gate_e, w_up_e, w_down_e, final_norm):
    b, seq, d = x.shape
    assert d == D_MODEL and w_in.shape[0] == 1, "single-layer block"
    n = b * seq
    scale = HEAD_DIM ** -0.5 * LOG2E
    col_scale = jnp.ones((2 * QKV_WIDTH + HEADS_B,), F32)
    col_scale = col_scale.at[0:WIDTH_A].set(scale).at[QKV_WIDTH:QKV_WIDTH + WIDTH_B].set(scale)
    w = jnp.pad(w_in[0] * col_scale[None, :], ((0, 0), (0, LANES - HEADS_B))).astype(BF16)
    bf = jnp.pad(b_forget[0].astype(F32), (0, LANES - HEADS_B))[None, :]
    cosv, sa, sb = _rotary_tables(seq)
    x2 = x.reshape(n, d)

    qkva, qkvb, c = _inproj(x2, attn_norm[0][None, :], w, cosv, sa, sb, bf, seq=seq, tm=512)
    out_a = _dilated(qkva.reshape(b, seq, QKV_WIDTH))
    out_b = _fox(qkvb.reshape(b, seq, QKV_WIDTH), c.reshape(b, seq, LANES), tq=512)

    wr = jnp.pad(jnp.concatenate([w_group[0], w_expert[0]], axis=1).astype(F32),
                 ((0, 0), (0, LANES - N_GROUPS - N_EXPERTS)))
    x1, h2ext, cls3 = _outproj(out_a.reshape(n, WIDTH_A), out_b.reshape(n, WIDTH_B), x2,
                               w_out[0].astype(BF16), ffn_norm[0][None, :], wr, tm=512)
    out = _moe(h2ext, cls3, x1, w_gate_e[0].astype(BF16), w_up_e[0].astype(BF16), w_down_e[0].astype(BF16),
               final_norm[None, :])
    return out.reshape(b, seq, d)
```

```python
import functools

import jax
import jax.numpy as jnp
import numpy as np
from jax import lax
from jax.experimental import pallas as pl
from jax.experimental.pallas import tpu as pltpu

F32 = jnp.float32
BF16 = jnp.bfloat16

D_MODEL = 1024
HEAD_DIM = 64
HEADS_A = 8
HEADS_B = 8
WIDTH_A = HEADS_A * HEAD_DIM
WIDTH_B = HEADS_B * HEAD_DIM
QKV_WIDTH = 3 * WIDTH_A
DILATIONS = ((128, 1), (512, 4), (2048, 16))
ROT_DIM = HEAD_DIM // 4
ROT_HALF = ROT_DIM // 2
ROPE_THETA = 500000.0
N_GROUPS = 4
EXPERTS_PER_GROUP = 4
N_EXPERTS = N_GROUPS * EXPERTS_PER_GROUP
D_EXPERT = 512
NORM_EPS = 1e-6
NEG_INF = -1e30

LANES = 128
HEAD_PAIRS = WIDTH_A // LANES
WIN = 128
MAX_DIL = 16
PERM = MAX_DIL * MAX_DIL
KV_BLOCK = 256
N_PAIRS = EXPERTS_PER_GROUP * (EXPERTS_PER_GROUP - 1) // 2
N_CLASSES = N_GROUPS * N_PAIRS
MOE_TILE = 256
EXT_WIDTH = D_MODEL + LANES
LOGIT_MAX = 60.0
BRANCH_UNROLL = 4
DMA_THREADS = 2
ROUTER_OFF = N_GROUPS
VMEM_LIMIT = 56 * 1024 * 1024
LOG2E = 1.4426950408889634
BOUND_SLACK = 1.0 + 2.0 ** -7
BOUND_EPS = 2.0 ** -7


def _split3(x):
    hi = x.astype(BF16)
    r1 = x - hi.astype(F32)
    mid = r1.astype(BF16)
    lo = (r1 - mid.astype(F32)).astype(BF16)
    return hi, mid, lo


def _inproj_kernel(x_ref, g_ref, w_ref, cos_ref, sa_ref, sb_ref, bf_ref,
                   qkva_ref, qkvb_ref, c_ref, h_scr, carry_scr, *, tiles_per_seq):
    i = pl.program_id(0)
    tm = x_ref.shape[0]
    x = x_ref[...]
    ms = jnp.mean(x * x, axis=-1, keepdims=True)
    h_scr[...] = (x * lax.rsqrt(ms + NORM_EPS) * g_ref[...]).astype(BF16)
    cosv = cos_ref[...]
    sa = sa_ref[...]
    sb = sb_ref[...]
    for g in range(6):
        p = jnp.dot(h_scr[...], w_ref[:, g * WIDTH_A:(g + 1) * WIDTH_A], preferred_element_type=F32)
        dst = qkva_ref if g < 3 else qkvb_ref
        col0 = (g % 3) * WIDTH_A
        if g < 2:
            for k in range(HEAD_PAIRS):
                t = p[:, k * LANES:(k + 1) * LANES]
                t = (t * cosv + pltpu.roll(t, LANES - ROT_HALF, 1) * sa + pltpu.roll(t, ROT_HALF, 1) * sb)
                dst[:, col0 + k * LANES:col0 + (k + 1) * LANES] = t.astype(BF16)
        else:
            dst[:, col0:col0 + WIDTH_A] = p.astype(BF16)
    z = jnp.dot(h_scr[...], w_ref[:, 2 * QKV_WIDTH:2 * QKV_WIDTH + LANES], preferred_element_type=F32) + bf_ref[...]
    lf = jnp.minimum(z, 0.0) - jnp.log1p(jnp.exp(-jnp.abs(z)))
    lane = lax.broadcasted_iota(jnp.int32, (tm, LANES), 1)
    lf = jnp.where(lane < HEADS_B, lf * LOG2E, 0.0)
    hi, mid, lo = _split3(lf)
    tri = (lax.broadcasted_iota(jnp.int32, (tm, tm), 0) >= lax.broadcasted_iota(jnp.int32, (tm, tm), 1)).astype(BF16)
    cs = (jnp.dot(tri, hi, preferred_element_type=F32) + jnp.dot(tri, mid, preferred_element_type=F32)
          + jnp.dot(tri, lo, preferred_element_type=F32))

    @pl.when(i % tiles_per_seq == 0)
    def _():
        carry_scr[...] = jnp.zeros_like(carry_scr)

    c = cs + carry_scr[0:1, :]
    c_ref[...] = c
    carry_scr[...] = jnp.broadcast_to(c[tm - 1:tm, :], carry_scr.shape)


def _inproj(x2, g, w, cosv, sa, sb, bf, *, seq, tm):
    n = x2.shape[0]
    kern = functools.partial(_inproj_kernel, tiles_per_seq=seq // tm)
    tps = seq // tm
    return pl.pallas_call(
        kern,
        grid=(n // tm,),
        in_specs=[
            pl.BlockSpec((tm, D_MODEL), lambda i: (i, 0)),
            pl.BlockSpec((1, D_MODEL), lambda i: (0, 0)),
            pl.BlockSpec(w.shape, lambda i: (0, 0)),
            pl.BlockSpec((tm, LANES), lambda i: (i % tps, 0)),
            pl.BlockSpec((tm, LANES), lambda i: (i % tps, 0)),
            pl.BlockSpec((tm, LANES), lambda i: (i % tps, 0)),
            pl.BlockSpec((1, LANES), lambda i: (0, 0)),
        ],
        out_specs=[
            pl.BlockSpec((tm, QKV_WIDTH), lambda i: (i, 0)),
            pl.BlockSpec((tm, QKV_WIDTH), lambda i: (i, 0)),
            pl.BlockSpec((tm, LANES), lambda i: (i, 0)),
        ],
        out_shape=[
            jax.ShapeDtypeStruct((n, QKV_WIDTH), BF16),
            jax.ShapeDtypeStruct((n, QKV_WIDTH), BF16),
            jax.ShapeDtypeStruct((n, LANES), F32),
        ],
        scratch_shapes=[pltpu.VMEM((tm, D_MODEL), BF16), pltpu.VMEM((8, LANES), F32)],
        compiler_params=pltpu.CompilerParams(dimension_semantics=("arbitrary",), vmem_limit_bytes=VMEM_LIMIT),
        name="inproj",
    )(x2, g, w, cosv, sa, sb, bf)


def _head_column(c_all, lane, head):
    return jnp.sum(jnp.where(lane == head, c_all, 0.0), axis=1, keepdims=True)


def _own_lanes(lane, hl):
    return (lane < HEAD_DIM) if hl == 0 else (lane >= HEAD_DIM)


def _own_sum(vals, lane, hl):
    return jnp.sum(jnp.where(_own_lanes(lane, hl), vals, 0.0), axis=1, keepdims=True)


def _norm_bound(qsq, ksq_max):
    return jnp.sqrt(qsq * ksq_max) * BOUND_SLACK + BOUND_EPS


def _fox_kernel(q_ref, k_ref, v_ref, c_ref, o_ref,
                q0_scr, q1_scr, k0_scr, k1_scr, v0_scr, v1_scr, bound_scr, acc0_scr, acc1_scr, s_scr,
                *, tq, seq, chunk):
    hp = pl.program_id(1)
    qi = pl.program_id(2)
    q_scr = (q0_scr, q1_scr)
    k_scr = (k0_scr, k1_scr)
    v_scr = (v0_scr, v1_scr)
    acc_scr = (acc0_scr, acc1_scr)

    def augmented(vals, cterm, lane, hl, key_side):
        hi, mid, lo = (t.astype(F32) for t in _split3(cterm))
        base = HEAD_DIM if hl == 0 else 0
        ones = (lane >= base + (0 if key_side else 3)) & (lane < base + (3 if key_side else 6))
        o = 3 if key_side else 0
        sgn = -1.0 if key_side else 1.0
        aug = jnp.where(lane == base + o, sgn * hi,
                        jnp.where(lane == base + o + 1, sgn * mid,
                                  jnp.where(lane == base + o + 2, sgn * lo,
                                            jnp.where(ones, 1.0, 0.0))))
        return jnp.where(_own_lanes(lane, hl), vals, aug).astype(BF16)

    @pl.when(qi == 0)
    def _build():
        lane = lax.broadcasted_iota(jnp.int32, (chunk, LANES), 1)

        def body(ci, sq_max):
            rows = pl.ds(pl.multiple_of(ci * chunk, chunk), chunk)
            qc = q_ref[0, rows, :].astype(F32)
            kc = k_ref[0, rows, :].astype(F32)
            vc = v_ref[0, rows, :].astype(F32)
            cc = c_ref[0, rows, :]
            new = []
            for hl in range(2):
                ccol = _head_column(cc, lane, 2 * hp + hl)
                q_scr[hl][rows, :] = augmented(qc, ccol, lane, hl, False)
                k_scr[hl][rows, :] = augmented(kc, ccol, lane, hl, True)
                one_lane = HEAD_DIM if hl == 0 else 0
                v_scr[hl][rows, :] = jnp.where(
                    _own_lanes(lane, hl), vc, jnp.where(lane == one_lane, 1.0, 0.0)).astype(BF16)
                for i, t in enumerate((qc, kc)):
                    new.append(jnp.maximum(sq_max[2 * hl + i],
                                           jnp.max(_own_sum(t * t, lane, hl), axis=0, keepdims=True)))
            return tuple(new)

        sq_max = lax.fori_loop(0, seq // chunk, body, (jnp.zeros((1, 1), F32),) * 4)
        worst = jnp.maximum(_norm_bound(sq_max[0], sq_max[1]), _norm_bound(sq_max[2], sq_max[3]))
        bound_scr[...] = jnp.broadcast_to(worst, bound_scr.shape)

    lane_q = lax.broadcasted_iota(jnp.int32, (tq, LANES), 1)
    q_rows = pl.ds(pl.multiple_of(qi * tq, tq), tq)
    worst = jnp.max(bound_scr[...])
    n_diag = tq // KV_BLOCK
    n_full = qi * n_diag
    row_minus_col = (lax.broadcasted_iota(jnp.int32, (tq, KV_BLOCK), 0)
                     - lax.broadcasted_iota(jnp.int32, (tq, KV_BLOCK), 1))

    def causal_mask(s, d):
        return jnp.where(row_minus_col >= d * KV_BLOCK, s, NEG_INF)

    def logits(hl, j):
        r0 = pl.multiple_of(j * KV_BLOCK, KV_BLOCK)
        return lax.dot_general(q_scr[hl][q_rows, :], k_scr[hl][pl.ds(r0, KV_BLOCK), :], (((1,), (1,)), ((), ())),
                               preferred_element_type=F32)

    def values(hl, j):
        return v_scr[hl][pl.ds(pl.multiple_of(j * KV_BLOCK, KV_BLOCK), KV_BLOCK), :]

    def finish(a0, a1):
        l0 = a0[:, HEAD_DIM:HEAD_DIM + 1]
        l1 = a1[:, 0:1]
        o_ref[0] = jnp.where(lane_q < HEAD_DIM, a0 / l0, a1 / l1).astype(o_ref.dtype)

    @pl.when(worst <= LOGIT_MAX)
    def _fast():
        acc0_scr[...] = jnp.zeros_like(acc0_scr)
        acc1_scr[...] = jnp.zeros_like(acc1_scr)

        def produce(j, slot):
            for hl in range(2):
                s_scr[slot, hl] = logits(hl, j)

        def consume(j, slot, diag):
            for hl in range(2):
                s = s_scr[slot, hl]
                if diag is not None:
                    s = causal_mask(s, diag)
                acc_scr[hl][...] += jnp.dot(jnp.exp2(s).astype(BF16), values(hl, j), preferred_element_type=F32)

        for d in range(n_diag):
            produce(d, d)

        def body(i, carry):
            for d in range(n_diag):
                j = i * n_diag + d
                consume(j, d, None)
                produce(j + n_diag, d)
            return carry

        lax.fori_loop(0, qi, body, 0)
        for d in range(n_diag):
            consume(n_full + d, d, d)
        finish(acc0_scr[...], acc1_scr[...])

    @pl.when(worst > LOGIT_MAX)
    def _general():
        def step(j, carry, diag):
            new = []
            for hl in range(2):
                m, a = carry[2 * hl:2 * hl + 2]
                s = logits(hl, j)
                if diag is not None:
                    s = causal_mask(s, diag)
                mn = jnp.maximum(m, jnp.max(s, axis=1, keepdims=True))
                a = jnp.exp2(m - mn) * a + jnp.dot(jnp.exp2(s - mn).astype(BF16), values(hl, j),
                                                  preferred_element_type=F32)
                new += [mn, a]
            return tuple(new)

        carry = (jnp.full((tq, 1), NEG_INF, F32), jnp.zeros((tq, LANES), F32)) * 2
        carry = lax.fori_loop(0, n_full, lambda j, c: step(j, c, None), carry)
        for d in range(n_diag):
            carry = step(n_full + d, carry, d)
        finish(carry[1], carry[3])


def _fox(qkvb, c, *, tq, chunk=512):
    b, seq, _ = qkvb.shape
    kern = functools.partial(_fox_kernel, tq=tq, seq=seq, chunk=chunk)
    return pl.pallas_call(
        kern,
        grid=(b, HEAD_PAIRS, seq // tq),
        in_specs=[
            pl.BlockSpec((1, seq, LANES), lambda bi, hp, qi: (bi, 0, hp)),
            pl.BlockSpec((1, seq, LANES), lambda bi, hp, qi: (bi, 0, HEAD_PAIRS + hp)),
            pl.BlockSpec((1, seq, LANES), lambda bi, hp, qi: (bi, 0, 2 * HEAD_PAIRS + hp)),
            pl.BlockSpec((1, seq, LANES), lambda bi, hp, qi: (bi, 0, 0)),
        ],
        out_specs=pl.BlockSpec((1, tq, LANES), lambda bi, hp, qi: (bi, qi, hp)),
        out_shape=jax.ShapeDtypeStruct((b, seq, WIDTH_B), BF16),
        scratch_shapes=[pltpu.VMEM((seq, LANES), BF16)] * 6 + [
                        pltpu.VMEM((8, LANES), F32),
                        pltpu.VMEM((tq, LANES), F32), pltpu.VMEM((tq, LANES), F32),
                        pltpu.VMEM((tq // KV_BLOCK, 2, tq, KV_BLOCK), F32)],
        compiler_params=pltpu.CompilerParams(
            dimension_semantics=("arbitrary", "arbitrary", "arbitrary"), vmem_limit_bytes=VMEM_LIMIT),
        name="fox",
    )(qkvb, qkvb, qkvb, c)


def _dilated_general(q_ref, k_ref, v_ref, o_ref, q16, k16, v16, m_s, l_s, acc_s, on_s, lse_s, *, seq):
    sub = seq // MAX_DIL
    own0 = lax.broadcasted_iota(jnp.int32, (WIN, LANES), 1) < HEAD_DIM
    ri = lax.broadcasted_iota(jnp.int32, (PERM, PERM), 0)
    ci = lax.broadcasted_iota(jnp.int32, (PERM, PERM), 1)
    perm = (((ri >> 4) == (ci & 15)) & ((ri & 15) == (ci >> 4))).astype(BF16)

    qa = lax.broadcasted_iota(jnp.int32, (WIN, 2 * WIN), 0)
    kb_i = lax.broadcasted_iota(jnp.int32, (WIN, 2 * WIN), 1)
    cur = kb_i >= WIN
    kin = kb_i & (WIN - 1)
    dist_nat = qa - kin + jnp.where(cur, 0, WIN)
    band_nat = (dist_nat >= 0) & (dist_nat <= WIN)
    sq = ((qa & 31) << 2) + (qa >> 5)
    sk = ((kin & 31) << 2) + (kin >> 5)
    dist_4 = sq - sk + jnp.where(cur, 0, WIN)
    band_4 = (dist_4 >= 0) & (dist_4 <= WIN)

    def attend(qb, kb, vb, mask, state):
        qf = qb.astype(F32)
        new = []
        for hl in range(2):
            m, l, a = state[3 * hl:3 * hl + 3]
            own = own0 if hl == 0 else jnp.logical_not(own0)
            qh = jnp.where(own, qf, 0.0).astype(BF16)
            s = lax.dot_general(qh, kb, (((1,), (1,)), ((), ())), preferred_element_type=F32)
            s = jnp.where(mask, s, NEG_INF)
            mn = jnp.maximum(m, jnp.max(s, axis=1, keepdims=True))
            alpha = jnp.exp2(m - mn)
            p = jnp.exp2(s - mn)
            l = alpha * l + jnp.sum(p, axis=1, keepdims=True)
            a = alpha * a + jnp.dot(p.astype(BF16), vb, preferred_element_type=F32)
            new += [mn, l, a]
        return new

    def pack_state(st):
        m0, l0, a0, m1, l1, a1 = st
        return (jnp.where(own0, m0, m1), jnp.where(own0, l0, l1), jnp.where(own0, a0, a1))

    def deint(blk, carry):
        r0 = pl.multiple_of(blk * PERM, PERM)
        j0 = pl.multiple_of(blk * MAX_DIL, MAX_DIL)
        for src, dst in ((q_ref, q16), (k_ref, k16), (v_ref, v16)):
            y = jnp.dot(perm, src[0, pl.ds(r0, PERM), :], preferred_element_type=F32).astype(BF16)
            for r in range(MAX_DIL):
                dst[r, pl.ds(j0, MAX_DIL), :] = y[r * MAX_DIL:(r + 1) * MAX_DIL, :]
        return carry

    lax.fori_loop(0, seq // PERM, deint, 0)

    fresh = [jnp.full((WIN, 1), NEG_INF, F32), jnp.zeros((WIN, 1), F32), jnp.zeros((WIN, LANES), F32)] * 2
    nb16 = sub // WIN

    def d16_body(t, carry):
        r = t // nb16
        n = t % nb16
        c0 = pl.multiple_of(n * WIN, WIN)
        p0 = pl.multiple_of(jnp.maximum(n - 1, 0) * WIN, WIN)
        qb = q16[r, pl.ds(c0, WIN), :]
        kb = jnp.concatenate([k16[r, pl.ds(p0, WIN), :], k16[r, pl.ds(c0, WIN), :]], axis=0)
        vb = jnp.concatenate([v16[r, pl.ds(p0, WIN), :], v16[r, pl.ds(c0, WIN), :]], axis=0)
        mask = band_nat & (cur | (n > 0))
        mm, ll, aa = pack_state(attend(qb, kb, vb, mask, fresh))
        row = pl.multiple_of(r * sub + c0, WIN)
        m_s[pl.ds(row, WIN), :] = mm
        l_s[pl.ds(row, WIN), :] = ll
        acc_s[pl.ds(row, WIN), :] = aa
        return carry

    lax.fori_loop(0, MAX_DIL * nb16, d16_body, 0)

    nb4 = (seq // 4) // WIN
    ch = WIN // 4

    def d4_body(t, carry):
        r4 = t // nb4
        n = t % nb4
        c0 = pl.multiple_of(n * ch, ch)
        p0 = pl.multiple_of(jnp.maximum(n - 1, 0) * ch, ch)

        def gather(ref, j0):
            return [ref[r4 + 4 * q, pl.ds(j0, ch), :] for q in range(4)]

        qb = jnp.concatenate(gather(q16, c0), axis=0)
        kb = jnp.concatenate(gather(k16, p0) + gather(k16, c0), axis=0)
        vb = jnp.concatenate(gather(v16, p0) + gather(v16, c0), axis=0)
        rows = [pl.multiple_of((r4 + 4 * q) * sub + c0, ch) for q in range(4)]
        mm = jnp.concatenate([m_s[pl.ds(rw, ch), :] for rw in rows], axis=0)
        ll = jnp.concatenate([l_s[pl.ds(rw, ch), :] for rw in rows], axis=0)
        aa = jnp.concatenate([acc_s[pl.ds(rw, ch), :] for rw in rows], axis=0)
        state = [mm[:, 0:1], ll[:, 0:1], aa, mm[:, HEAD_DIM:HEAD_DIM + 1], ll[:, HEAD_DIM:HEAD_DIM + 1], aa]
        mask = band_4 & (cur | (n > 0))
        mm, ll, aa = pack_state(attend(qb, kb, vb, mask, state))
        for q, rw in enumerate(rows):
            m_s[pl.ds(rw, ch), :] = mm[q * ch:(q + 1) * ch, :]
            l_s[pl.ds(rw, ch), :] = ll[q * ch:(q + 1) * ch, :]
            acc_s[pl.ds(rw, ch), :] = aa[q * ch:(q + 1) * ch, :]
        return carry

    lax.fori_loop(0, 4 * nb4, d4_body, 0)

    def renat(blk, carry):
        j0 = pl.multiple_of(blk * MAX_DIL, MAX_DIL)
        r0 = pl.multiple_of(blk * PERM, PERM)
        rows = [pl.multiple_of(r * sub + j0, MAX_DIL) for r in range(MAX_DIL)]
        mm = jnp.concatenate([m_s[pl.ds(rw, MAX_DIL), :] for rw in rows], axis=0)
        ll = jnp.concatenate([l_s[pl.ds(rw, MAX_DIL), :] for rw in rows], axis=0)
        aa = jnp.concatenate([acc_s[pl.ds(rw, MAX_DIL), :] for rw in rows], axis=0)
        o = (aa / ll).astype(BF16)
        hi, mid, lo = _split3(mm + jnp.log2(ll))
        on_s[pl.ds(r0, PERM), :] = jnp.dot(perm, o, preferred_element_type=F32)
        lse_s[pl.ds(r0, PERM), :] = (jnp.dot(perm, hi, preferred_element_type=F32)
                                     + jnp.dot(perm, mid, preferred_element_type=F32)
                                     + jnp.dot(perm, lo, preferred_element_type=F32))
        return carry

    lax.fori_loop(0, seq // PERM, renat, 0)

    def d1_body(n, carry):
        c0 = pl.multiple_of(n * WIN, WIN)
        p0 = pl.multiple_of(jnp.maximum(n - 1, 0) * WIN, WIN)
        qb = q_ref[0, pl.ds(c0, WIN), :]
        kb = jnp.concatenate([k_ref[0, pl.ds(p0, WIN), :], k_ref[0, pl.ds(c0, WIN), :]], axis=0)
        vb = jnp.concatenate([v_ref[0, pl.ds(p0, WIN), :], v_ref[0, pl.ds(c0, WIN), :]], axis=0)
        lse = lse_s[pl.ds(c0, WIN), :]
        o = on_s[pl.ds(c0, WIN), :]
        one = jnp.ones((WIN, 1), F32)
        state = [lse[:, 0:1], one, o, lse[:, HEAD_DIM:HEAD_DIM + 1], one, o]
        mask = band_nat & (cur | (n > 0))
        m0, l0, a0, m1, l1, a1 = attend(qb, kb, vb, mask, state)
        o_ref[0, pl.ds(c0, WIN), :] = jnp.where(own0, a0 / l0, a1 / l1).astype(o_ref.dtype)
        return carry

    lax.fori_loop(0, seq // WIN, d1_body, 0)


def _dilated_fast(o_ref, qn, kn, vn, qr, kr, vr, fr, fn, s_scr, mask_scr, *, seq):
    sub = seq // MAX_DIL
    own0 = lax.broadcasted_iota(jnp.int32, (WIN, LANES), 1) < HEAD_DIM
    head_mask = [jnp.where(own0, 1.0, 0.0).astype(BF16), jnp.where(own0, 0.0, 1.0).astype(BF16)]
    ri = lax.broadcasted_iota(jnp.int32, (PERM, PERM), 0)
    ci = lax.broadcasted_iota(jnp.int32, (PERM, PERM), 1)
    perm = (((ri >> 4) == (ci & 15)) & ((ri & 15) == (ci >> 4))).astype(BF16)

    qa = lax.broadcasted_iota(jnp.int32, (WIN, 2 * WIN), 0)
    kb_i = lax.broadcasted_iota(jnp.int32, (WIN, 2 * WIN), 1)
    cur = kb_i >= WIN
    kin = kb_i & (WIN - 1)
    dist_nat = qa - kin + jnp.where(cur, 0, WIN)
    band_nat = (dist_nat >= 0) & (dist_nat <= WIN)
    sq = ((qa & 31) << 2) + (qa >> 5)
    sk = ((kin & 31) << 2) + (kin >> 5)
    dist_4 = sq - sk + jnp.where(cur, 0, WIN)
    band_4 = (dist_4 >= 0) & (dist_4 <= WIN)
    for i, band in enumerate((band_nat, band_4)):
        mask_scr[2 * i] = jnp.where(band & cur, 1.0, 0.0).astype(BF16)
        mask_scr[2 * i + 1] = jnp.where(band, 1.0, 0.0).astype(BF16)

    def deint(blk, carry):
        r0 = pl.multiple_of(blk * PERM, PERM)
        j0 = pl.multiple_of(blk * MAX_DIL, MAX_DIL)
        for src, dst in ((qn[0], qr[0]), (kn[0], kr[0]), (vn[0], vr[0]), (vn[1], vr[1])):
            y = jnp.dot(perm, src[pl.ds(r0, PERM), :], preferred_element_type=F32).astype(BF16)
            for r in range(MAX_DIL):
                dst[r, pl.ds(j0, MAX_DIL), :] = y[r * MAX_DIL:(r + 1) * MAX_DIL, :]
        return carry

    lax.fori_loop(0, seq // PERM, deint, 0)

    def run_branch(nblocks, block_in_seq, fetch, mask_base, sink):
        def produce(t, pair, u):
            for hl in range(2):
                qb, kb = fetch(t, hl, True)
                s_scr[pair, u, hl] = lax.dot_general(qb * head_mask[hl], kb, (((1,), (1,)), ((), ())),
                                                     preferred_element_type=F32)

        def consume(t, pair, u):
            mk = mask_scr[mask_base + jnp.minimum(block_in_seq(t), 1)]
            pv = []
            for hl in range(2):
                p = jnp.exp2(s_scr[pair, u, hl]).astype(BF16) * mk
                pv.append(jnp.dot(p, fetch(t, hl, False), preferred_element_type=F32))
            sink(t, pv)

        for u in range(BRANCH_UNROLL):
            produce(u, 0, u)

        def body(i, carry):
            pair = i & 1
            for u in range(BRANCH_UNROLL):
                consume(BRANCH_UNROLL * i + u, pair, u)
            for u in range(BRANCH_UNROLL):
                produce(jnp.minimum(BRANCH_UNROLL * (i + 1) + u, nblocks - 1), 1 - pair, u)
            return carry

        lax.fori_loop(0, nblocks // BRANCH_UNROLL, body, 0)

    def prev_cur(ref_block, n, size):
        c0 = pl.multiple_of(n * size, size)
        p0 = pl.multiple_of(jnp.maximum(n - 1, 0) * size, size)
        return ref_block(p0) + ref_block(c0)

    nb16 = sub // WIN

    def fetch16(t, hl, qk):
        r, n = t // nb16, t % nb16
        if qk:
            return (qr[hl][r, pl.ds(pl.multiple_of(n * WIN, WIN), WIN), :],
                    jnp.concatenate(prev_cur(lambda s: [kr[hl][r, pl.ds(s, WIN), :]], n, WIN), axis=0))
        return jnp.concatenate(prev_cur(lambda s: [vr[hl][r, pl.ds(s, WIN), :]], n, WIN), axis=0)

    def sink16(t, pv):
        row = pl.multiple_of(t * WIN, WIN)
        for hl in range(2):
            fr[hl][pl.ds(row, WIN), :] = pv[hl]

    run_branch(MAX_DIL * nb16, lambda t: t % nb16, fetch16, 0, sink16)

    nb4 = (seq // 4) // WIN
    ch = WIN // 4

    def fetch4(t, hl, qk):
        r4, n = t // nb4, t % nb4
        gather = lambda ref: (lambda s: [ref[r4 + 4 * q, pl.ds(s, ch), :] for q in range(4)])
        if qk:
            return (jnp.concatenate(gather(qr[hl])(pl.multiple_of(n * ch, ch)), axis=0),
                    jnp.concatenate(prev_cur(gather(kr[hl]), n, ch), axis=0))
        return jnp.concatenate(prev_cur(gather(vr[hl]), n, ch), axis=0)

    def sink4(t, pv):
        r4, n = t // nb4, t % nb4
        for q in range(4):
            row = pl.multiple_of((r4 + 4 * q) * sub + n * ch, ch)
            for hl in range(2):
                fr[hl][pl.ds(row, ch), :] += pv[hl][q * ch:(q + 1) * ch, :]

    run_branch(4 * nb4, lambda t: t % nb4, fetch4, 2, sink4)

    def renat(blk, carry):
        j0 = pl.multiple_of(blk * MAX_DIL, MAX_DIL)
        r0 = pl.multiple_of(blk * PERM, PERM)
        for hl in range(2):
            a = jnp.concatenate([fr[hl][pl.ds(pl.multiple_of(r * sub + j0, MAX_DIL), MAX_DIL), :]
                                 for r in range(MAX_DIL)], axis=0)
            hi = a.astype(BF16)
            lo = (a - hi.astype(F32)).astype(BF16)
            fn[hl][pl.ds(r0, PERM), :] = (jnp.dot(perm, hi, preferred_element_type=F32)
                                          + jnp.dot(perm, lo, preferred_element_type=F32))
        return carry

    lax.fori_loop(0, seq // PERM, renat, 0)

    def fetch1(t, hl, qk):
        if qk:
            return (qn[hl][pl.ds(pl.multiple_of(t * WIN, WIN), WIN), :],
                    jnp.concatenate(prev_cur(lambda s: [kn[hl][pl.ds(s, WIN), :]], t, WIN), axis=0))
        return jnp.concatenate(prev_cur(lambda s: [vn[hl][pl.ds(s, WIN), :]], t, WIN), axis=0)

    def sink1(t, pv):
        row = pl.multiple_of(t * WIN, WIN)
        t0 = fn[0][pl.ds(row, WIN), :] + pv[0]
        t1 = fn[1][pl.ds(row, WIN), :] + pv[1]
        o_ref[0, pl.ds(row, WIN), :] = jnp.where(own0, t0 / t0[:, HEAD_DIM:HEAD_DIM + 1],
                                                 t1 / t1[:, 0:1]).astype(o_ref.dtype)

    run_branch(seq // WIN, lambda t: t, fetch1, 0, sink1)


def _dilated_kernel(q_ref, k_ref, v_ref, o_ref, vn0, vn1, qr, kr, vr0, vr1,
                    f0, f1, f2, f3, f4, s_scr, mask_scr, *, seq, chunk):
    lane = lax.broadcasted_iota(jnp.int32, (chunk, LANES), 1)
    vn = (vn0, vn1)

    def stats(ci, sq_max):
        rows = pl.ds(pl.multiple_of(ci * chunk, chunk), chunk)
        qc = q_ref[0, rows, :].astype(F32)
        kc = k_ref[0, rows, :].astype(F32)
        vc = v_ref[0, rows, :].astype(F32)
        new = []
        for hl in range(2):
            ones_lane = HEAD_DIM if hl == 0 else 0
            vn[hl][rows, :] = jnp.where(_own_lanes(lane, hl), vc, jnp.where(lane == ones_lane, 1.0, 0.0)).astype(BF16)
            for i, t in enumerate((qc, kc)):
                new.append(jnp.maximum(sq_max[2 * hl + i], jnp.max(_own_sum(t * t, lane, hl), axis=0, keepdims=True)))
        return tuple(new)

    sq_max = lax.fori_loop(0, seq // chunk, stats, (jnp.zeros((1, 1), F32),) * 4)
    worst = jnp.max(jnp.maximum(_norm_bound(sq_max[0], sq_max[1]), _norm_bound(sq_max[2], sq_max[3])))

    @pl.when(worst <= LOGIT_MAX)
    def _fast():
        qn, kn = q_ref.at[0], k_ref.at[0]
        _dilated_fast(o_ref, (qn, qn), (kn, kn), vn, (qr, qr), (kr, kr), (vr0, vr1), (f0, f1), (f2, f3),
                      s_scr, mask_scr, seq=seq)

    @pl.when(worst > LOGIT_MAX)
    def _general():
        _dilated_general(q_ref, k_ref, v_ref, o_ref, qr, kr, vr0, f0, f1, f2, f3, f4, seq=seq)


def _dilated(qkva, *, chunk=512):
    b, seq, _ = qkva.shape
    sub = seq // MAX_DIL
    kern = functools.partial(_dilated_kernel, seq=seq, chunk=chunk)
    return pl.pallas_call(
        kern,
        grid=(b, HEAD_PAIRS),
        in_specs=[
            pl.BlockSpec((1, seq, LANES), lambda bi, hp: (bi, 0, hp)),
            pl.BlockSpec((1, seq, LANES), lambda bi, hp: (bi, 0, HEAD_PAIRS + hp)),
            pl.BlockSpec((1, seq, LANES), lambda bi, hp: (bi, 0, 2 * HEAD_PAIRS + hp)),
        ],
        out_specs=pl.BlockSpec((1, seq, LANES), lambda bi, hp: (bi, 0, hp)),
        out_shape=jax.ShapeDtypeStruct((b, seq, WIDTH_A), BF16),
        scratch_shapes=(
            [pltpu.VMEM((seq, LANES), BF16)] * 2
            + [pltpu.VMEM((MAX_DIL, sub, LANES), BF16)] * 4
            + [pltpu.VMEM((seq, LANES), F32)] * 5
            + [pltpu.VMEM((2, BRANCH_UNROLL, 2, WIN, 2 * WIN), F32), pltpu.VMEM((4, WIN, 2 * WIN), BF16)]
        ),
        compiler_params=pltpu.CompilerParams(
            dimension_semantics=("arbitrary", "arbitrary"), vmem_limit_bytes=VMEM_LIMIT),
        name="dilated",
    )(qkva, qkva, qkva)


def _outproj_kernel(oa_ref, ob_ref, x_ref, wo_ref, g_ref, wr_ref, x1_ref, h2_ref, cls_ref):
    tm = x_ref.shape[0]
    x1 = (x_ref[...]
          + jnp.dot(oa_ref[...], wo_ref[0:WIDTH_A, :], preferred_element_type=F32)
          + jnp.dot(ob_ref[...], wo_ref[WIDTH_A:WIDTH_A + WIDTH_B, :], preferred_element_type=F32))
    x1_ref[...] = x1
    ms = jnp.mean(x1 * x1, axis=-1, keepdims=True)
    h2 = x1 * lax.rsqrt(ms + NORM_EPS) * g_ref[...]
    h2_ref[:, 0:D_MODEL] = h2
    h2_hi = h2.astype(BF16)
    h2_lo = (h2 - h2_hi.astype(F32)).astype(BF16)
    both = jnp.dot(h2_hi, wr_ref[...], preferred_element_type=F32)
    logits = (both[:, 0:LANES] + both[:, LANES:2 * LANES]
              + jnp.dot(h2_lo, wr_ref[:, 0:LANES], preferred_element_type=F32))
    lane = lax.broadcasted_iota(jnp.int32, (tm, LANES), 1)
    lane_f = lane.astype(F32)
    big = float(LANES)
    gmask = lane < N_GROUPS
    gl = jnp.where(gmask, logits, NEG_INF)
    gmax = jnp.max(gl, axis=1, keepdims=True)
    gsum = jnp.sum(jnp.where(gmask, jnp.exp(gl - gmax), 0.0), axis=1, keepdims=True)
    p_top = 1.0 / gsum
    g_star = jnp.min(jnp.where(gmask & (gl == gmax), lane_f, big), axis=1, keepdims=True)
    lo_lane = ROUTER_OFF + EXPERTS_PER_GROUP * g_star
    emask = (lane_f >= lo_lane) & (lane_f < lo_lane + EXPERTS_PER_GROUP)
    sel = jnp.where(emask, logits, NEG_INF)
    v1 = jnp.max(sel, axis=1, keepdims=True)
    i1 = jnp.min(jnp.where(emask & (sel == v1), lane_f, big), axis=1, keepdims=True)
    rest = emask & (lane_f != i1)
    sel2 = jnp.where(rest, logits, NEG_INF)
    v2 = jnp.max(sel2, axis=1, keepdims=True)
    i2 = jnp.min(jnp.where(rest & (sel2 == v2), lane_f, big), axis=1, keepdims=True)
    e2 = jnp.exp(v2 - v1)
    w1 = p_top / (1.0 + e2)
    w2 = p_top * e2 / (1.0 + e2)
    e1 = i1 - lo_lane
    e2x = i2 - lo_lane
    a = jnp.minimum(e1, e2x)
    b = jnp.maximum(e1, e2x)
    cls = g_star * N_PAIRS + a * (7.0 - a) * 0.5 + (b - a - 1.0)
    wa = jnp.where(e1 < e2x, w1, w2)
    wb = jnp.where(e1 < e2x, w2, w1)
    h2_ref[:, D_MODEL:EXT_WIDTH] = jnp.where(lane == 0, wa, jnp.where(lane == 1, wb, 0.0))
    row = lax.broadcasted_iota(jnp.int32, (tm, LANES), 0)
    spread = jnp.where(lane == (row & (LANES - 1)), cls, 0.0)
    cls_ref[0] = jnp.sum(spread.reshape(tm // LANES, LANES, LANES), axis=1).astype(jnp.int32)


def _outproj(oa, ob, x2, wo, g, wr, *, tm):
    n = x2.shape[0]
    return pl.pallas_call(
        _outproj_kernel,
        grid=(n // tm,),
        in_specs=[
            pl.BlockSpec((tm, WIDTH_A), lambda i: (i, 0)),
            pl.BlockSpec((tm, WIDTH_B), lambda i: (i, 0)),
            pl.BlockSpec((tm, D_MODEL), lambda i: (i, 0)),
            pl.BlockSpec(wo.shape, lambda i: (0, 0)),
            pl.BlockSpec((1, D_MODEL), lambda i: (0, 0)),
            pl.BlockSpec(wr.shape, lambda i: (0, 0)),
        ],
        out_specs=[
            pl.BlockSpec((tm, D_MODEL), lambda i: (i, 0)),
            pl.BlockSpec((tm, EXT_WIDTH), lambda i: (i, 0)),
            pl.BlockSpec((1, tm // LANES, LANES), lambda i: (i, 0, 0)),
        ],
        out_shape=[
            jax.ShapeDtypeStruct((n, D_MODEL), F32),
            jax.ShapeDtypeStruct((n, EXT_WIDTH), F32),
            jax.ShapeDtypeStruct((n // tm, tm // LANES, LANES), jnp.int32),
        ],
        compiler_params=pltpu.CompilerParams(dimension_semantics=("arbitrary",), vmem_limit_bytes=VMEM_LIMIT),
        name="outproj",
    )(oa, ob, x2, wo, g, wr)


def _sort_kernel(cls_ref, pos_ref, tile_ref, ends_ref, *, tile):
    cls = cls_ref[...]
    rows = cls.shape[0]
    upper = (lax.broadcasted_iota(jnp.int32, (LANES, LANES), 0)
             <= lax.broadcasted_iota(jnp.int32, (LANES, LANES), 1)).astype(BF16)
    earlier_rows = (lax.broadcasted_iota(jnp.int32, (rows, rows), 0)
                    > lax.broadcasted_iota(jnp.int32, (rows, rows), 1)).astype(BF16)
    ones = jnp.ones((LANES, LANES), BF16)
    lane8 = lax.broadcasted_iota(jnp.int32, tile_ref.shape, 1)
    tile_start = lane8.astype(F32) * tile
    off = jnp.zeros((1, 1), F32)
    pos = jnp.zeros(cls.shape, F32)
    tile_cls = jnp.zeros(tile_ref.shape, F32)
    ends = jnp.zeros(tile_ref.shape, F32)
    for c in range(N_CLASSES):
        hit = cls == c
        hot = jnp.where(hit, 1.0, 0.0).astype(BF16)
        in_row = jnp.dot(hot, upper, preferred_element_type=F32)
        row_tot = jnp.dot(hot, ones, preferred_element_type=F32)
        before = jnp.dot(earlier_rows, row_tot.astype(BF16), preferred_element_type=F32)
        count = before[rows - 1:rows, 0:1] + row_tot[rows - 1:rows, 0:1]
        pos = pos + jnp.where(hit, in_row - 1.0 + before + off, 0.0)
        off = off + jnp.ceil(count / tile) * tile
        tile_cls = tile_cls + jnp.where(tile_start >= off, 1.0, 0.0)
        ends = jnp.where(lane8 == c, off, ends)
    pos_ref[...] = pos.astype(jnp.int32)
    tile_ref[...] = tile_cls.astype(jnp.int32)
    ends_ref[...] = ends.astype(jnp.int32)


def _sort(cls2d, *, tile):
    meta = jax.ShapeDtypeStruct((8, LANES), jnp.int32)
    return pl.pallas_call(
        functools.partial(_sort_kernel, tile=tile),
        out_shape=[jax.ShapeDtypeStruct(cls2d.shape, jnp.int32), meta, meta],
        compiler_params=pltpu.CompilerParams(vmem_limit_bytes=VMEM_LIMIT),
        name="moe_sort",
    )(cls2d)


def _row_copy(src, src_row, dst, dst_row, sem, rows=1):
    return pltpu.make_async_copy(src.at[pl.ds(src_row, rows)], dst.at[pl.ds(dst_row, rows)], sem)


def _dispatch_kernel(pos_ref, ends_ref, h2_ref, xs_hbm, zero_scr, sem, *, ch, tile):
    g = pl.program_id(0)

    @pl.when(g == 0)
    def _zero_tails():
        zero_scr[...] = jnp.zeros_like(zero_scr)
        n_tiles = xs_hbm.shape[0] // tile
        min_used = n_tiles - N_CLASSES
        total = ends_ref[0, N_CLASSES - 1]
        for phase in ("start", "wait"):
            def zero_tile(row0):
                cp = pltpu.make_async_copy(zero_scr, xs_hbm.at[pl.ds(pl.multiple_of(row0, tile), tile)], sem)
                cp.start() if phase == "start" else cp.wait()

            for c in range(N_CLASSES):
                end = ends_ref[0, c]
                prev = ends_ref[0, c - 1] if c else 0
                pl.when(end > prev)(functools.partial(zero_tile, end - tile))
                pl.when((min_used + c) * tile >= total)(functools.partial(zero_tile, (min_used + c) * tile))

    for k in range(ch):
        _row_copy(h2_ref, k, xs_hbm, pos_ref[0, 0, k], sem).start(priority=k % DMA_THREADS)
    _row_copy(h2_ref, 0, xs_hbm, 0, sem, rows=ch).wait()


def _dispatch(pos3, ends, h2ext, *, n_rows, tile):
    n_chunks, _, ch = pos3.shape
    return pl.pallas_call(
        functools.partial(_dispatch_kernel, ch=ch, tile=tile),
        grid=(n_chunks,),
        in_specs=[
            pl.BlockSpec((1, 1, ch), lambda g: (g, 0, 0), memory_space=pltpu.SMEM),
            pl.BlockSpec(ends.shape, lambda g: (0, 0), memory_space=pltpu.SMEM),
            pl.BlockSpec((ch, EXT_WIDTH), lambda g: (g, 0)),
        ],
        out_specs=pl.BlockSpec(memory_space=pl.ANY),
        out_shape=jax.ShapeDtypeStruct((n_rows, EXT_WIDTH), F32),
        scratch_shapes=[pltpu.VMEM((tile, EXT_WIDTH), F32), pltpu.SemaphoreType.DMA(())],
        compiler_params=pltpu.CompilerParams(dimension_semantics=("arbitrary",), vmem_limit_bytes=VMEM_LIMIT),
        name="moe_dispatch",
    )(pos3, ends, h2ext)


def _experts_kernel(ea_ref, eb_ref, used_ref, xs_ref, wga, wua, wda, wgb, wub, wdb, ys_ref):
    del ea_ref, eb_ref

    @pl.when(pl.program_id(0) < used_ref[0])
    def _():
        x = xs_ref[:, 0:D_MODEL].astype(BF16)
        wts = xs_ref[:, D_MODEL:EXT_WIDTH]
        y = None
        for col, (wg, wu, wd) in enumerate(((wga, wua, wda), (wgb, wub, wdb))):
            gate = jnp.dot(x, wg[0].astype(BF16), preferred_element_type=F32)
            up = jnp.dot(x, wu[0].astype(BF16), preferred_element_type=F32)
            he = (gate / (1.0 + jnp.exp(-gate)) * up * wts[:, col:col + 1]).astype(BF16)
            t = jnp.dot(he, wd[0].astype(BF16), preferred_element_type=F32)
            y = t if y is None else y + t
        ys_ref[...] = y

    @pl.when(pl.program_id(0) >= used_ref[0])
    def _():
        ys_ref[...] = jnp.zeros_like(ys_ref)


def _experts(ea, eb, used, xs, wg, wu, wd, *, tile):
    n_tiles = xs.shape[0] // tile
    rows = lambda j, ea, eb, used: (jnp.minimum(j, used[0] - 1), 0)
    first = lambda j, ea, eb, used: (ea[j], 0, 0)
    second = lambda j, ea, eb, used: (eb[j], 0, 0)
    up_spec = lambda m: pl.BlockSpec((1, D_MODEL, D_EXPERT), m)
    down_spec = lambda m: pl.BlockSpec((1, D_EXPERT, D_MODEL), m)
    return pl.pallas_call(
        _experts_kernel,
        grid_spec=pltpu.PrefetchScalarGridSpec(
            num_scalar_prefetch=3,
            grid=(n_tiles,),
            in_specs=[pl.BlockSpec((tile, EXT_WIDTH), rows),
                      up_spec(first), up_spec(first), down_spec(first),
                      up_spec(second), up_spec(second), down_spec(second)],
            out_specs=pl.BlockSpec((tile, D_MODEL), lambda j, ea, eb, used: (j, 0)),
        ),
        out_shape=jax.ShapeDtypeStruct((xs.shape[0], D_MODEL), F32),
        compiler_params=pltpu.CompilerParams(dimension_semantics=("arbitrary",), vmem_limit_bytes=VMEM_LIMIT),
        name="moe_experts",
    )(ea, eb, used, xs, wg, wu, wd, wg, wu, wd)


def _combine_kernel(pos_ref, x1_ref, g_ref, ys_hbm, out_ref, y_scr, sem):
    tm = x1_ref.shape[0]

    for k in range(tm):
        _row_copy(ys_hbm, pos_ref[0, 0, k], y_scr, k, sem).start(priority=k % DMA_THREADS)
    _row_copy(ys_hbm, 0, y_scr, 0, sem, rows=tm).wait()
    x2 = x1_ref[...] + y_scr[...]
    ms = jnp.mean(x2 * x2, axis=-1, keepdims=True)
    out_ref[...] = x2 * lax.rsqrt(ms + NORM_EPS) * g_ref[...]


def _combine(pos3, x1, g, ys):
    n_chunks, _, tm = pos3.shape
    return pl.pallas_call(
        _combine_kernel,
        grid=(n_chunks,),
        in_specs=[
            pl.BlockSpec((1, 1, tm), lambda i: (i, 0, 0), memory_space=pltpu.SMEM),
            pl.BlockSpec((tm, D_MODEL), lambda i: (i, 0)),
            pl.BlockSpec((1, D_MODEL), lambda i: (0, 0)),
            pl.BlockSpec(memory_space=pl.ANY),
        ],
        out_specs=pl.BlockSpec((tm, D_MODEL), lambda i: (i, 0)),
        out_shape=jax.ShapeDtypeStruct(x1.shape, F32),
        scratch_shapes=[pltpu.VMEM((tm, D_MODEL), F32), pltpu.SemaphoreType.DMA(())],
        compiler_params=pltpu.CompilerParams(dimension_semantics=("arbitrary",), vmem_limit_bytes=VMEM_LIMIT),
        name="moe_combine",
    )(pos3, x1, g, ys)


def _moe(h2ext, cls3, x1, wg, wu, wd, g, *, tile=MOE_TILE, dispatch_chunk=512, combine_chunk=512):
    n = x1.shape[0]
    n_tiles = n // tile + N_CLASSES
    assert n_tiles <= LANES
    pos, tile_cls, ends = _sort(cls3.reshape(n // LANES, LANES), tile=tile)
    xs = _dispatch(pos.reshape(n // dispatch_chunk, 1, dispatch_chunk), ends, h2ext, n_rows=n_tiles * tile, tile=tile)
    used = ends[0, N_CLASSES - 1:N_CLASSES] // tile
    tc = tile_cls[0, :n_tiles]
    tc = jnp.where(jnp.arange(n_tiles) < used[0], tc, tc[jnp.maximum(used[0] - 1, 0)])
    pair_a = jnp.array([0, 0, 0, 1, 1, 2], jnp.int32)
    pair_b = jnp.array([1, 2, 3, 2, 3, 3], jnp.int32)
    ea = (tc // N_PAIRS) * EXPERTS_PER_GROUP + pair_a[tc % N_PAIRS]
    eb = (tc // N_PAIRS) * EXPERTS_PER_GROUP + pair_b[tc % N_PAIRS]
    ys = _experts(ea, eb, used, xs, wg, wu, wd, tile=tile)
    return _combine(pos.reshape(n // combine_chunk, 1, combine_chunk), x1, g, ys)


def _rotary_tables(seq):
    pos = np.arange(seq, dtype=np.float64)
    inv_freq = 1.0 / (ROPE_THETA ** (np.arange(0, ROT_DIM, 2, dtype=np.float64) / ROT_DIM))
    ang = pos[:, None] * inv_freq[None, :]
    cos, sin = np.cos(ang), np.sin(ang)
    zeros = np.zeros((seq, HEAD_DIM - ROT_DIM))
    zh = np.zeros((seq, ROT_HALF))
    cos_h = np.concatenate([cos, cos, np.ones((seq, HEAD_DIM - ROT_DIM))], axis=1)
    sa_h = np.concatenate([-sin, zh, zeros], axis=1)
    sb_h = np.concatenate([zh, sin, zeros], axis=1)
    tile = lambda t: jnp.asarray(np.concatenate([t, t], axis=1), F32)
    return tile(cos_h), tile(sa_h), tile(sb_h)


def kernel(x, attn_norm, w_in, b_forget, w_out, ffn_norm, w_group, w_expert, w_gate_e, w_up_e, w_down_e, final_norm):
    b, seq, d = x.shape
    assert d == D_MODEL and w_in.shape[0] == 1, "single-layer block"
    n = b * seq
    scale = HEAD_DIM ** -0.5 * LOG2E
    col_scale = jnp.ones((2 * QKV_WIDTH + HEADS_B,), F32)
    col_scale = col_scale.at[0:WIDTH_A].set(scale).at[QKV_WIDTH:QKV_WIDTH + WIDTH_B].set(scale)
    w = jnp.pad(w_in[0] * col_scale[None, :], ((0, 0), (0, LANES - HEADS_B))).astype(BF16)
    bf = jnp.pad(b_forget[0].astype(F32), (0, LANES - HEADS_B))[None, :]
    cosv, sa, sb = _rotary_tables(seq)
    x2 = x.reshape(n, d)

    qkva, qkvb, c = _inproj(x2, attn_norm[0][None, :], w, cosv, sa, sb, bf, seq=seq, tm=512)
    out_a = _dilated(qkva.reshape(b, seq, QKV_WIDTH))
    out_b = _fox(qkvb.reshape(b, seq, QKV_WIDTH), c.reshape(b, seq, LANES), tq=512)

    wr = jnp.pad(jnp.concatenate([w_group[0], w_expert[0]], axis=1).astype(F32),
                 ((0, 0), (0, LANES - N_GROUPS - N_EXPERTS)))
    wr_hi = wr.astype(BF16)
    wr = jnp.concatenate([wr_hi, (wr - wr_hi.astype(F32)).astype(BF16)], axis=1)
    x1, h2ext, cls3 = _outproj(out_a.reshape(n, WIDTH_A), out_b.reshape(n, WIDTH_B), x2,
                               w_out[0].astype(BF16), ffn_norm[0][None, :], wr, tm=512)
    out = _moe(h2ext, cls3, x1, w_gate_e[0], w_up_e[0], w_down_e[0],
               final_norm[None, :])
    return out.reshape(b, seq, d)
```

```python
import functools

import jax
import jax.numpy as jnp
import numpy as np
from jax import lax
from jax.experimental import pallas as pl
from jax.experimental.pallas import tpu as pltpu

F32 = jnp.float32
BF16 = jnp.bfloat16

D_MODEL = 1024
HEAD_DIM = 64
HEADS_A = 8
HEADS_B = 8
WIDTH_A = HEADS_A * HEAD_DIM
WIDTH_B = HEADS_B * HEAD_DIM
QKV_WIDTH = 3 * WIDTH_A
DILATIONS = ((128, 1), (512, 4), (2048, 16))
ROT_DIM = HEAD_DIM // 4
ROT_HALF = ROT_DIM // 2
ROPE_THETA = 500000.0
N_GROUPS = 4
EXPERTS_PER_GROUP = 4
N_EXPERTS = N_GROUPS * EXPERTS_PER_GROUP
D_EXPERT = 512
NORM_EPS = 1e-6
NEG_INF = -1e30

LANES = 128
HEAD_PAIRS = WIDTH_A // LANES
WIN = 128
MAX_DIL = 16
PERM = MAX_DIL * MAX_DIL
KV_BLOCK = 256
N_PAIRS = EXPERTS_PER_GROUP * (EXPERTS_PER_GROUP - 1) // 2
N_CLASSES = N_GROUPS * N_PAIRS
MOE_TILE = 256
EXT_WIDTH = D_MODEL + LANES
LOGIT_MAX = 60.0
BRANCH_UNROLL = 4
DMA_THREADS = 2
ROUTER_OFF = N_GROUPS
VMEM_LIMIT = 56 * 1024 * 1024
LOG2E = 1.4426950408889634
BOUND_SLACK = 1.0 + 2.0 ** -6
BOUND_EPS = 2.0 ** -7


def _split3(x):
    hi = x.astype(BF16)
    r1 = x - hi.astype(F32)
    mid = r1.astype(BF16)
    lo = (r1 - mid.astype(F32)).astype(BF16)
    return hi, mid, lo


def _inproj_kernel(x_ref, g_ref, w_ref, cos_ref, sa_ref, sb_ref, bf_ref,
                   qkva_ref, qkvb_ref, c_ref, h_scr, carry_scr, *, tiles_per_seq):
    i = pl.program_id(0)
    tm = x_ref.shape[0]
    x = x_ref[...]
    ms = jnp.mean(x * x, axis=-1, keepdims=True)
    h_scr[...] = (x * lax.rsqrt(ms + NORM_EPS) * g_ref[...]).astype(BF16)
    cosv = cos_ref[...]
    sa = sa_ref[...]
    sb = sb_ref[...]
    for g in range(6):
        p = jnp.dot(h_scr[...], w_ref[:, g * WIDTH_A:(g + 1) * WIDTH_A], preferred_element_type=F32)
        dst = qkva_ref if g < 3 else qkvb_ref
        col0 = (g % 3) * WIDTH_A
        if g < 2:
            for k in range(HEAD_PAIRS):
                t = p[:, k * LANES:(k + 1) * LANES]
                t = (t * cosv + pltpu.roll(t, LANES - ROT_HALF, 1) * sa + pltpu.roll(t, ROT_HALF, 1) * sb)
                dst[:, col0 + k * LANES:col0 + (k + 1) * LANES] = t.astype(BF16)
        else:
            dst[:, col0:col0 + WIDTH_A] = p.astype(BF16)
    z = jnp.dot(h_scr[...], w_ref[:, 2 * QKV_WIDTH:2 * QKV_WIDTH + LANES], preferred_element_type=F32) + bf_ref[...]
    lf = jnp.minimum(z, 0.0) - jnp.log1p(jnp.exp(-jnp.abs(z)))
    lane = lax.broadcasted_iota(jnp.int32, (tm, LANES), 1)
    lf = jnp.where(lane < HEADS_B, lf * LOG2E, 0.0)
    hi, mid, lo = _split3(lf)
    tri = (lax.broadcasted_iota(jnp.int32, (tm, tm), 0) >= lax.broadcasted_iota(jnp.int32, (tm, tm), 1)).astype(BF16)
    cs = (jnp.dot(tri, hi, preferred_element_type=F32) + jnp.dot(tri, mid, preferred_element_type=F32)
          + jnp.dot(tri, lo, preferred_element_type=F32))

    @pl.when(i % tiles_per_seq == 0)
    def _():
        carry_scr[...] = jnp.zeros_like(carry_scr)

    c = cs + carry_scr[0:1, :]
    c_ref[...] = c
    carry_scr[...] = jnp.broadcast_to(c[tm - 1:tm, :], carry_scr.shape)


def _inproj(x2, g, w, cosv, sa, sb, bf, *, seq, tm):
    n = x2.shape[0]
    kern = functools.partial(_inproj_kernel, tiles_per_seq=seq // tm)
    tps = seq // tm
    return pl.pallas_call(
        kern,
        grid=(n // tm,),
        in_specs=[
            pl.BlockSpec((tm, D_MODEL), lambda i: (i, 0)),
            pl.BlockSpec((1, D_MODEL), lambda i: (0, 0)),
            pl.BlockSpec(w.shape, lambda i: (0, 0)),
            pl.BlockSpec((tm, LANES), lambda i: (i % tps, 0)),
            pl.BlockSpec((tm, LANES), lambda i: (i % tps, 0)),
            pl.BlockSpec((tm, LANES), lambda i: (i % tps, 0)),
            pl.BlockSpec((1, LANES), lambda i: (0, 0)),
        ],
        out_specs=[
            pl.BlockSpec((tm, QKV_WIDTH), lambda i: (i, 0)),
            pl.BlockSpec((tm, QKV_WIDTH), lambda i: (i, 0)),
            pl.BlockSpec((tm, LANES), lambda i: (i, 0)),
        ],
        out_shape=[
            jax.ShapeDtypeStruct((n, QKV_WIDTH), BF16),
            jax.ShapeDtypeStruct((n, QKV_WIDTH), BF16),
            jax.ShapeDtypeStruct((n, LANES), F32),
        ],
        scratch_shapes=[pltpu.VMEM((tm, D_MODEL), BF16), pltpu.VMEM((8, LANES), F32)],
        compiler_params=pltpu.CompilerParams(dimension_semantics=("arbitrary",), vmem_limit_bytes=VMEM_LIMIT),
        name="inproj",
    )(x2, g, w, cosv, sa, sb, bf)


def _head_column(c_all, lane, head):
    return jnp.sum(jnp.where(lane == head, c_all, 0.0), axis=1, keepdims=True)


def _own_lanes(lane, hl):
    return (lane < HEAD_DIM) if hl == 0 else (lane >= HEAD_DIM)


def _own_sum(vals, lane, hl):
    return jnp.sum(jnp.where(_own_lanes(lane, hl), vals, 0.0), axis=1, keepdims=True)


def _head_sum_selector():
    lane = lax.broadcasted_iota(jnp.int32, (LANES, LANES), 0)
    col = lax.broadcasted_iota(jnp.int32, (LANES, LANES), 1)
    return jnp.where(((col == 0) & (lane < HEAD_DIM)) | ((col == 1) & (lane >= HEAD_DIM)), 1.0, 0.0).astype(BF16)


def _max_sq_norm(t, selector):
    return jnp.max(jnp.dot((t * t).astype(BF16), selector, preferred_element_type=F32), axis=0, keepdims=True)


def _norm_bound(qsq, ksq_max):
    return jnp.sqrt(qsq * ksq_max) * BOUND_SLACK + BOUND_EPS


def _fox_kernel(q_ref, k_ref, v_ref, c_ref, o_ref,
                q0_scr, q1_scr, k0_scr, k1_scr, v0_scr, v1_scr, bound_scr, acc0_scr, acc1_scr, s_scr,
                *, tq, seq, chunk):
    hp = pl.program_id(1)
    qi = pl.program_id(2)
    q_scr = (q0_scr, q1_scr)
    k_scr = (k0_scr, k1_scr)
    v_scr = (v0_scr, v1_scr)
    acc_scr = (acc0_scr, acc1_scr)

    def augmented(vals, cterm, lane, hl, key_side):
        hi, mid, lo = (t.astype(F32) for t in _split3(cterm))
        base = HEAD_DIM if hl == 0 else 0
        ones = (lane >= base + (0 if key_side else 3)) & (lane < base + (3 if key_side else 6))
        o = 3 if key_side else 0
        sgn = -1.0 if key_side else 1.0
        aug = jnp.where(lane == base + o, sgn * hi,
                        jnp.where(lane == base + o + 1, sgn * mid,
                                  jnp.where(lane == base + o + 2, sgn * lo,
                                            jnp.where(ones, 1.0, 0.0))))
        return jnp.where(_own_lanes(lane, hl), vals, aug).astype(BF16)

    @pl.when(qi == 0)
    def _build():
        lane = lax.broadcasted_iota(jnp.int32, (chunk, LANES), 1)

        def body(ci, sq_max):
            rows = pl.ds(pl.multiple_of(ci * chunk, chunk), chunk)
            qc = q_ref[0, rows, :].astype(F32)
            kc = k_ref[0, rows, :].astype(F32)
            vc = v_ref[0, rows, :].astype(F32)
            cc = c_ref[0, rows, :]
            for hl in range(2):
                ccol = _head_column(cc, lane, 2 * hp + hl)
                q_scr[hl][rows, :] = augmented(qc, ccol, lane, hl, False)
                k_scr[hl][rows, :] = augmented(kc, ccol, lane, hl, True)
                one_lane = HEAD_DIM if hl == 0 else 0
                v_scr[hl][rows, :] = jnp.where(
                    _own_lanes(lane, hl), vc, jnp.where(lane == one_lane, 1.0, 0.0)).astype(BF16)
            return (jnp.maximum(sq_max[0], _max_sq_norm(qc, selector)),
                    jnp.maximum(sq_max[1], _max_sq_norm(kc, selector)))

        selector = _head_sum_selector()
        sq_max = lax.fori_loop(0, seq // chunk, body, (jnp.zeros((1, LANES), F32),) * 2)
        bound_scr[...] = jnp.broadcast_to(_norm_bound(sq_max[0], sq_max[1]), bound_scr.shape)

    lane_q = lax.broadcasted_iota(jnp.int32, (tq, LANES), 1)
    q_rows = pl.ds(pl.multiple_of(qi * tq, tq), tq)
    worst = jnp.max(bound_scr[...])
    n_diag = tq // KV_BLOCK
    n_full = qi * n_diag
    row_minus_col = (lax.broadcasted_iota(jnp.int32, (tq, KV_BLOCK), 0)
                     - lax.broadcasted_iota(jnp.int32, (tq, KV_BLOCK), 1))

    def causal_mask(s, d):
        return jnp.where(row_minus_col >= d * KV_BLOCK, s, NEG_INF)

    def logits(hl, j):
        r0 = pl.multiple_of(j * KV_BLOCK, KV_BLOCK)
        return lax.dot_general(q_scr[hl][q_rows, :], k_scr[hl][pl.ds(r0, KV_BLOCK), :], (((1,), (1,)), ((), ())),
                               preferred_element_type=F32)

    def values(hl, j):
        return v_scr[hl][pl.ds(pl.multiple_of(j * KV_BLOCK, KV_BLOCK), KV_BLOCK), :]

    def finish(a0, a1):
        l0 = a0[:, HEAD_DIM:HEAD_DIM + 1]
        l1 = a1[:, 0:1]
        o_ref[0] = jnp.where(lane_q < HEAD_DIM, a0 / l0, a1 / l1).astype(o_ref.dtype)

    @pl.when(worst <= LOGIT_MAX)
    def _fast():
        acc0_scr[...] = jnp.zeros_like(acc0_scr)
        acc1_scr[...] = jnp.zeros_like(acc1_scr)

        def produce(j, slot):
            for hl in range(2):
                s_scr[slot, hl] = logits(hl, j)

        def consume(j, slot, diag):
            for hl in range(2):
                s = s_scr[slot, hl]
                if diag is not None:
                    s = causal_mask(s, diag)
                acc_scr[hl][...] += jnp.dot(jnp.exp2(s).astype(BF16), values(hl, j), preferred_element_type=F32)

        for d in range(n_diag):
            produce(d, d)

        def body(i, carry):
            for d in range(n_diag):
                j = i * n_diag + d
                consume(j, d, None)
                produce(j + n_diag, d)
            return carry

        lax.fori_loop(0, qi, body, 0)
        for d in range(n_diag):
            consume(n_full + d, d, d)
        finish(acc0_scr[...], acc1_scr[...])

    @pl.when(worst > LOGIT_MAX)
    def _general():
        def step(j, carry, diag):
            new = []
            for hl in range(2):
                m, a = carry[2 * hl:2 * hl + 2]
                s = logits(hl, j)
                if diag is not None:
                    s = causal_mask(s, diag)
                mn = jnp.maximum(m, jnp.max(s, axis=1, keepdims=True))
                a = jnp.exp2(m - mn) * a + jnp.dot(jnp.exp2(s - mn).astype(BF16), values(hl, j),
                                                  preferred_element_type=F32)
                new += [mn, a]
            return tuple(new)

        carry = (jnp.full((tq, 1), NEG_INF, F32), jnp.zeros((tq, LANES), F32)) * 2
        carry = lax.fori_loop(0, n_full, lambda j, c: step(j, c, None), carry)
        for d in range(n_diag):
            carry = step(n_full + d, carry, d)
        finish(carry[1], carry[3])


def _fox(qkvb, c, *, tq, chunk=512):
    b, seq, _ = qkvb.shape
    kern = functools.partial(_fox_kernel, tq=tq, seq=seq, chunk=chunk)
    return pl.pallas_call(
        kern,
        grid=(b, HEAD_PAIRS, seq // tq),
        in_specs=[
            pl.BlockSpec((1, seq, LANES), lambda bi, hp, qi: (bi, 0, hp)),
            pl.BlockSpec((1, seq, LANES), lambda bi, hp, qi: (bi, 0, HEAD_PAIRS + hp)),
            pl.BlockSpec((1, seq, LANES), lambda bi, hp, qi: (bi, 0, 2 * HEAD_PAIRS + hp)),
            pl.BlockSpec((1, seq, LANES), lambda bi, hp, qi: (bi, 0, 0)),
        ],
        out_specs=pl.BlockSpec((1, tq, LANES), lambda bi, hp, qi: (bi, qi, hp)),
        out_shape=jax.ShapeDtypeStruct((b, seq, WIDTH_B), BF16),
        scratch_shapes=[pltpu.VMEM((seq, LANES), BF16)] * 6 + [
                        pltpu.VMEM((8, LANES), F32),
                        pltpu.VMEM((tq, LANES), F32), pltpu.VMEM((tq, LANES), F32),
                        pltpu.VMEM((tq // KV_BLOCK, 2, tq, KV_BLOCK), F32)],
        compiler_params=pltpu.CompilerParams(
            dimension_semantics=("arbitrary", "arbitrary", "arbitrary"), vmem_limit_bytes=VMEM_LIMIT),
        name="fox",
    )(qkvb, qkvb, qkvb, c)


def _dilated_general(q_ref, k_ref, v_ref, o_ref, q16, k16, v16, m_s, l_s, acc_s, on_s, lse_s, *, seq):
    sub = seq // MAX_DIL
    own0 = lax.broadcasted_iota(jnp.int32, (WIN, LANES), 1) < HEAD_DIM
    ri = lax.broadcasted_iota(jnp.int32, (PERM, PERM), 0)
    ci = lax.broadcasted_iota(jnp.int32, (PERM, PERM), 1)
    perm = (((ri >> 4) == (ci & 15)) & ((ri & 15) == (ci >> 4))).astype(BF16)

    qa = lax.broadcasted_iota(jnp.int32, (WIN, 2 * WIN), 0)
    kb_i = lax.broadcasted_iota(jnp.int32, (WIN, 2 * WIN), 1)
    cur = kb_i >= WIN
    kin = kb_i & (WIN - 1)
    dist_nat = qa - kin + jnp.where(cur, 0, WIN)
    band_nat = (dist_nat >= 0) & (dist_nat <= WIN)
    sq = ((qa & 31) << 2) + (qa >> 5)
    sk = ((kin & 31) << 2) + (kin >> 5)
    dist_4 = sq - sk + jnp.where(cur, 0, WIN)
    band_4 = (dist_4 >= 0) & (dist_4 <= WIN)

    def attend(qb, kb, vb, mask, state):
        qf = qb.astype(F32)
        new = []
        for hl in range(2):
            m, l, a = state[3 * hl:3 * hl + 3]
            own = own0 if hl == 0 else jnp.logical_not(own0)
            qh = jnp.where(own, qf, 0.0).astype(BF16)
            s = lax.dot_general(qh, kb, (((1,), (1,)), ((), ())), preferred_element_type=F32)
            s = jnp.where(mask, s, NEG_INF)
            mn = jnp.maximum(m, jnp.max(s, axis=1, keepdims=True))
            alpha = jnp.exp2(m - mn)
            p = jnp.exp2(s - mn)
            l = alpha * l + jnp.sum(p, axis=1, keepdims=True)
            a = alpha * a + jnp.dot(p.astype(BF16), vb, preferred_element_type=F32)
            new += [mn, l, a]
        return new

    def pack_state(st):
        m0, l0, a0, m1, l1, a1 = st
        return (jnp.where(own0, m0, m1), jnp.where(own0, l0, l1), jnp.where(own0, a0, a1))

    def deint(blk, carry):
        r0 = pl.multiple_of(blk * PERM, PERM)
        j0 = pl.multiple_of(blk * MAX_DIL, MAX_DIL)
        for src, dst in ((q_ref, q16), (k_ref, k16), (v_ref, v16)):
            y = jnp.dot(perm, src[0, pl.ds(r0, PERM), :], preferred_element_type=F32).astype(BF16)
            for r in range(MAX_DIL):
                dst[r, pl.ds(j0, MAX_DIL), :] = y[r * MAX_DIL:(r + 1) * MAX_DIL, :]
        return carry

    lax.fori_loop(0, seq // PERM, deint, 0)

    fresh = [jnp.full((WIN, 1), NEG_INF, F32), jnp.zeros((WIN, 1), F32), jnp.zeros((WIN, LANES), F32)] * 2
    nb16 = sub // WIN

    def d16_body(t, carry):
        r = t // nb16
        n = t % nb16
        c0 = pl.multiple_of(n * WIN, WIN)
        p0 = pl.multiple_of(jnp.maximum(n - 1, 0) * WIN, WIN)
        qb = q16[r, pl.ds(c0, WIN), :]
        kb = jnp.concatenate([k16[r, pl.ds(p0, WIN), :], k16[r, pl.ds(c0, WIN), :]], axis=0)
        vb = jnp.concatenate([v16[r, pl.ds(p0, WIN), :], v16[r, pl.ds(c0, WIN), :]], axis=0)
        mask = band_nat & (cur | (n > 0))
        mm, ll, aa = pack_state(attend(qb, kb, vb, mask, fresh))
        row = pl.multiple_of(r * sub + c0, WIN)
        m_s[pl.ds(row, WIN), :] = mm
        l_s[pl.ds(row, WIN), :] = ll
        acc_s[pl.ds(row, WIN), :] = aa
        return carry

    lax.fori_loop(0, MAX_DIL * nb16, d16_body, 0)

    nb4 = (seq // 4) // WIN
    ch = WIN // 4

    def d4_body(t, carry):
        r4 = t // nb4
        n = t % nb4
        c0 = pl.multiple_of(n * ch, ch)
        p0 = pl.multiple_of(jnp.maximum(n - 1, 0) * ch, ch)

        def gather(ref, j0):
            return [ref[r4 + 4 * q, pl.ds(j0, ch), :] for q in range(4)]

        qb = jnp.concatenate(gather(q16, c0), axis=0)
        kb = jnp.concatenate(gather(k16, p0) + gather(k16, c0), axis=0)
        vb = jnp.concatenate(gather(v16, p0) + gather(v16, c0), axis=0)
        rows = [pl.multiple_of((r4 + 4 * q) * sub + c0, ch) for q in range(4)]
        mm = jnp.concatenate([m_s[pl.ds(rw, ch), :] for rw in rows], axis=0)
        ll = jnp.concatenate([l_s[pl.ds(rw, ch), :] for rw in rows], axis=0)
        aa = jnp.concatenate([acc_s[pl.ds(rw, ch), :] for rw in rows], axis=0)
        state = [mm[:, 0:1], ll[:, 0:1], aa, mm[:, HEAD_DIM:HEAD_DIM + 1], ll[:, HEAD_DIM:HEAD_DIM + 1], aa]
        mask = band_4 & (cur | (n > 0))
        mm, ll, aa = pack_state(attend(qb, kb, vb, mask, state))
        for q, rw in enumerate(rows):
            m_s[pl.ds(rw, ch), :] = mm[q * ch:(q + 1) * ch, :]
            l_s[pl.ds(rw, ch), :] = ll[q * ch:(q + 1) * ch, :]
            acc_s[pl.ds(rw, ch), :] = aa[q * ch:(q + 1) * ch, :]
        return carry

    lax.fori_loop(0, 4 * nb4, d4_body, 0)

    def renat(blk, carry):
        j0 = pl.multiple_of(blk * MAX_DIL, MAX_DIL)
        r0 = pl.multiple_of(blk * PERM, PERM)
        rows = [pl.multiple_of(r * sub + j0, MAX_DIL) for r in range(MAX_DIL)]
        mm = jnp.concatenate([m_s[pl.ds(rw, MAX_DIL), :] for rw in rows], axis=0)
        ll = jnp.concatenate([l_s[pl.ds(rw, MAX_DIL), :] for rw in rows], axis=0)
        aa = jnp.concatenate([acc_s[pl.ds(rw, MAX_DIL), :] for rw in rows], axis=0)
        o = (aa / ll).astype(BF16)
        hi, mid, lo = _split3(mm + jnp.log2(ll))
        on_s[pl.ds(r0, PERM), :] = jnp.dot(perm, o, preferred_element_type=F32)
        lse_s[pl.ds(r0, PERM), :] = (jnp.dot(perm, hi, preferred_element_type=F32)
                                     + jnp.dot(perm, mid, preferred_element_type=F32)
                                     + jnp.dot(perm, lo, preferred_element_type=F32))
        return carry

    lax.fori_loop(0, seq // PERM, renat, 0)

    def d1_body(n, carry):
        c0 = pl.multiple_of(n * WIN, WIN)
        p0 = pl.multiple_of(jnp.maximum(n - 1, 0) * WIN, WIN)
        qb = q_ref[0, pl.ds(c0, WIN), :]
        kb = jnp.concatenate([k_ref[0, pl.ds(p0, WIN), :], k_ref[0, pl.ds(c0, WIN), :]], axis=0)
        vb = jnp.concatenate([v_ref[0, pl.ds(p0, WIN), :], v_ref[0, pl.ds(c0, WIN), :]], axis=0)
        lse = lse_s[pl.ds(c0, WIN), :]
        o = on_s[pl.ds(c0, WIN), :]
        one = jnp.ones((WIN, 1), F32)
        state = [lse[:, 0:1], one, o, lse[:, HEAD_DIM:HEAD_DIM + 1], one, o]
        mask = band_nat & (cur | (n > 0))
        m0, l0, a0, m1, l1, a1 = attend(qb, kb, vb, mask, state)
        o_ref[0, pl.ds(c0, WIN), :] = jnp.where(own0, a0 / l0, a1 / l1).astype(o_ref.dtype)
        return carry

    lax.fori_loop(0, seq // WIN, d1_body, 0)


def _dilated_fast(o_ref, qn, kn, vn, qr, kr, vr, fr, fn, s_scr, mask_scr, *, seq):
    sub = seq // MAX_DIL
    own0 = lax.broadcasted_iota(jnp.int32, (WIN, LANES), 1) < HEAD_DIM
    head_mask = [jnp.where(own0, 1.0, 0.0).astype(BF16), jnp.where(own0, 0.0, 1.0).astype(BF16)]
    ri = lax.broadcasted_iota(jnp.int32, (PERM, PERM), 0)
    ci = lax.broadcasted_iota(jnp.int32, (PERM, PERM), 1)
    perm = (((ri >> 4) == (ci & 15)) & ((ri & 15) == (ci >> 4))).astype(BF16)

    qa = lax.broadcasted_iota(jnp.int32, (WIN, 2 * WIN), 0)
    kb_i = lax.broadcasted_iota(jnp.int32, (WIN, 2 * WIN), 1)
    cur = kb_i >= WIN
    kin = kb_i & (WIN - 1)
    dist_nat = qa - kin + jnp.where(cur, 0, WIN)
    band_nat = (dist_nat >= 0) & (dist_nat <= WIN)
    sq = ((qa & 31) << 2) + (qa >> 5)
    sk = ((kin & 31) << 2) + (kin >> 5)
    dist_4 = sq - sk + jnp.where(cur, 0, WIN)
    band_4 = (dist_4 >= 0) & (dist_4 <= WIN)
    for i, band in enumerate((band_nat, band_4)):
        mask_scr[2 * i] = jnp.where(band & cur, 1.0, 0.0).astype(BF16)
        mask_scr[2 * i + 1] = jnp.where(band, 1.0, 0.0).astype(BF16)

    def deint(i, carry):
        for u in range(2):
            blk = 2 * i + u
            r0 = pl.multiple_of(blk * PERM, PERM)
            j0 = pl.multiple_of(blk * MAX_DIL, MAX_DIL)
            for srcs, dsts in (((qn[0], kn[0]), (qr[0], kr[0])), (vn, vr)):
                both = jnp.concatenate([src[pl.ds(r0, PERM), :] for src in srcs], axis=1)
                y = jnp.dot(perm, both, preferred_element_type=F32).astype(BF16)
                for r in range(MAX_DIL):
                    for half, dst in enumerate(dsts):
                        dst[r, pl.ds(j0, MAX_DIL), :] = y[r * MAX_DIL:(r + 1) * MAX_DIL, half * LANES:(half + 1) * LANES]
        return carry

    lax.fori_loop(0, seq // PERM // 2, deint, 0)

    def run_branch(nblocks, block_in_seq, fetch, mask_base, sink):
        def produce(t, pair, u):
            for hl in range(2):
                qb, kb = fetch(t, hl, True)
                s_scr[pair, u, hl] = lax.dot_general(qb * head_mask[hl], kb, (((1,), (1,)), ((), ())),
                                                     preferred_element_type=F32)

        def consume(t, pair, u):
            mk = mask_scr[mask_base + jnp.minimum(block_in_seq(t), 1)]
            pv = []
            for hl in range(2):
                p = jnp.exp2(s_scr[pair, u, hl]).astype(BF16) * mk
                pv.append(jnp.dot(p, fetch(t, hl, False), preferred_element_type=F32))
            sink(t, pv)

        for u in range(BRANCH_UNROLL):
            produce(u, 0, u)

        def body(i, carry):
            pair = i & 1
            for u in range(BRANCH_UNROLL):
                consume(BRANCH_UNROLL * i + u, pair, u)
            for u in range(BRANCH_UNROLL):
                produce(jnp.minimum(BRANCH_UNROLL * (i + 1) + u, nblocks - 1), 1 - pair, u)
            return carry

        lax.fori_loop(0, nblocks // BRANCH_UNROLL, body, 0)

    def prev_cur(ref_block, n, size):
        c0 = pl.multiple_of(n * size, size)
        p0 = pl.multiple_of(jnp.maximum(n - 1, 0) * size, size)
        return ref_block(p0) + ref_block(c0)

    nb16 = sub // WIN

    def fetch16(t, hl, qk):
        r, n = t // nb16, t % nb16
        if qk:
            return (qr[hl][r, pl.ds(pl.multiple_of(n * WIN, WIN), WIN), :],
                    jnp.concatenate(prev_cur(lambda s: [kr[hl][r, pl.ds(s, WIN), :]], n, WIN), axis=0))
        return jnp.concatenate(prev_cur(lambda s: [vr[hl][r, pl.ds(s, WIN), :]], n, WIN), axis=0)

    def sink16(t, pv):
        row = pl.multiple_of(t * WIN, WIN)
        for hl in range(2):
            fr[hl][pl.ds(row, WIN), :] = pv[hl]

    run_branch(MAX_DIL * nb16, lambda t: t % nb16, fetch16, 0, sink16)

    nb4 = (seq // 4) // WIN
    ch = WIN // 4

    def fetch4(t, hl, qk):
        r4, n = t // nb4, t % nb4
        gather = lambda ref: (lambda s: [ref[r4 + 4 * q, pl.ds(s, ch), :] for q in range(4)])
        if qk:
            return (jnp.concatenate(gather(qr[hl])(pl.multiple_of(n * ch, ch)), axis=0),
                    jnp.concatenate(prev_cur(gather(kr[hl]), n, ch), axis=0))
        return jnp.concatenate(prev_cur(gather(vr[hl]), n, ch), axis=0)

    def sink4(t, pv):
        r4, n = t // nb4, t % nb4
        for q in range(4):
            row = pl.multiple_of((r4 + 4 * q) * sub + n * ch, ch)
            for hl in range(2):
                fr[hl][pl.ds(row, ch), :] += pv[hl][q * ch:(q + 1) * ch, :]

    run_branch(4 * nb4, lambda t: t % nb4, fetch4, 2, sink4)

    def renat(i, carry):
        for u in range(2):
            blk = 2 * i + u
            j0 = pl.multiple_of(blk * MAX_DIL, MAX_DIL)
            r0 = pl.multiple_of(blk * PERM, PERM)
            for hl in range(2):
                a = jnp.concatenate([fr[hl][pl.ds(pl.multiple_of(r * sub + j0, MAX_DIL), MAX_DIL), :]
                                     for r in range(MAX_DIL)], axis=0)
                hi = a.astype(BF16)
                lo = (a - hi.astype(F32)).astype(BF16)
                y = jnp.dot(perm, jnp.concatenate([hi, lo], axis=1), preferred_element_type=F32)
                fn[hl][pl.ds(r0, PERM), :] = y[:, 0:LANES] + y[:, LANES:2 * LANES]
        return carry

    lax.fori_loop(0, seq // PERM // 2, renat, 0)

    def fetch1(t, hl, qk):
        if qk:
            return (qn[hl][pl.ds(pl.multiple_of(t * WIN, WIN), WIN), :],
                    jnp.concatenate(prev_cur(lambda s: [kn[hl][pl.ds(s, WIN), :]], t, WIN), axis=0))
        return jnp.concatenate(prev_cur(lambda s: [vn[hl][pl.ds(s, WIN), :]], t, WIN), axis=0)

    def sink1(t, pv):
        row = pl.multiple_of(t * WIN, WIN)
        t0 = fn[0][pl.ds(row, WIN), :] + pv[0]
        t1 = fn[1][pl.ds(row, WIN), :] + pv[1]
        o_ref[0, pl.ds(row, WIN), :] = jnp.where(own0, t0 / t0[:, HEAD_DIM:HEAD_DIM + 1],
                                                 t1 / t1[:, 0:1]).astype(o_ref.dtype)

    run_branch(seq // WIN, lambda t: t, fetch1, 0, sink1)


def _dilated_kernel(q_ref, k_ref, v_ref, o_ref, vn0, vn1, qr, kr, vr0, vr1,
                    f0, f1, f2, f3, f4, s_scr, mask_scr, *, seq, chunk):
    lane = lax.broadcasted_iota(jnp.int32, (chunk, LANES), 1)
    vn = (vn0, vn1)

    def stats(ci, sq_max):
        rows = pl.ds(pl.multiple_of(ci * chunk, chunk), chunk)
        qc = q_ref[0, rows, :].astype(F32)
        kc = k_ref[0, rows, :].astype(F32)
        vc = v_ref[0, rows, :].astype(F32)
        for hl in range(2):
            ones_lane = HEAD_DIM if hl == 0 else 0
            vn[hl][rows, :] = jnp.where(_own_lanes(lane, hl), vc, jnp.where(lane == ones_lane, 1.0, 0.0)).astype(BF16)
        return (jnp.maximum(sq_max[0], _max_sq_norm(qc, selector)),
                jnp.maximum(sq_max[1], _max_sq_norm(kc, selector)))

    selector = _head_sum_selector()
    sq_max = lax.fori_loop(0, seq // chunk, stats, (jnp.zeros((1, LANES), F32),) * 2)
    worst = jnp.max(_norm_bound(sq_max[0], sq_max[1]))

    @pl.when(worst <= LOGIT_MAX)
    def _fast():
        qn, kn = q_ref.at[0], k_ref.at[0]
        _dilated_fast(o_ref, (qn, qn), (kn, kn), vn, (qr, qr), (kr, kr), (vr0, vr1), (f0, f1), (f2, f3),
                      s_scr, mask_scr, seq=seq)

    @pl.when(worst > LOGIT_MAX)
    def _general():
        _dilated_general(q_ref, k_ref, v_ref, o_ref, qr, kr, vr0, f0, f1, f2, f3, f4, seq=seq)


def _dilated(qkva, *, chunk=512):
    b, seq, _ = qkva.shape
    sub = seq // MAX_DIL
    kern = functools.partial(_dilated_kernel, seq=seq, chunk=chunk)
    return pl.pallas_call(
        kern,
        grid=(b, HEAD_PAIRS),
        in_specs=[
            pl.BlockSpec((1, seq, LANES), lambda bi, hp: (bi, 0, hp)),
            pl.BlockSpec((1, seq, LANES), lambda bi, hp: (bi, 0, HEAD_PAIRS + hp)),
            pl.BlockSpec((1, seq, LANES), lambda bi, hp: (bi, 0, 2 * HEAD_PAIRS + hp)),
        ],
        out_specs=pl.BlockSpec((1, seq, LANES), lambda bi, hp: (bi, 0, hp)),
        out_shape=jax.ShapeDtypeStruct((b, seq, WIDTH_A), BF16),
        scratch_shapes=(
            [pltpu.VMEM((seq, LANES), BF16)] * 2
            + [pltpu.VMEM((MAX_DIL, sub, LANES), BF16)] * 4
            + [pltpu.VMEM((seq, LANES), F32)] * 5
            + [pltpu.VMEM((2, BRANCH_UNROLL, 2, WIN, 2 * WIN), F32), pltpu.VMEM((4, WIN, 2 * WIN), BF16)]
        ),
        compiler_params=pltpu.CompilerParams(
            dimension_semantics=("arbitrary", "arbitrary"), vmem_limit_bytes=VMEM_LIMIT),
        name="dilated",
    )(qkva, qkva, qkva)


def _outproj_kernel(oa_ref, ob_ref, x_ref, wo_ref, g_ref, wr_ref, x1_ref, h2_ref, cls_ref):
    tm = x_ref.shape[0]
    x1 = (x_ref[...]
          + jnp.dot(oa_ref[...], wo_ref[0:WIDTH_A, :], preferred_element_type=F32)
          + jnp.dot(ob_ref[...], wo_ref[WIDTH_A:WIDTH_A + WIDTH_B, :], preferred_element_type=F32))
    x1_ref[...] = x1
    ms = jnp.mean(x1 * x1, axis=-1, keepdims=True)
    h2 = x1 * lax.rsqrt(ms + NORM_EPS) * g_ref[...]
    h2_ref[:, 0:D_MODEL] = h2
    h2_hi = h2.astype(BF16)
    h2_lo = (h2 - h2_hi.astype(F32)).astype(BF16)
    both = jnp.dot(h2_hi, wr_ref[...], preferred_element_type=F32)
    logits = (both[:, 0:LANES] + both[:, LANES:2 * LANES]
              + jnp.dot(h2_lo, wr_ref[:, 0:LANES], preferred_element_type=F32))
    lane = lax.broadcasted_iota(jnp.int32, (tm, LANES), 1)
    lane_f = lane.astype(F32)
    big = float(LANES)
    gmask = lane < N_GROUPS
    gl = jnp.where(gmask, logits, NEG_INF)
    gmax = jnp.max(gl, axis=1, keepdims=True)
    gsum = jnp.sum(jnp.where(gmask, jnp.exp(gl - gmax), 0.0), axis=1, keepdims=True)
    p_top = 1.0 / gsum
    g_star = jnp.min(jnp.where(gmask & (gl == gmax), lane_f, big), axis=1, keepdims=True)
    lo_lane = ROUTER_OFF + EXPERTS_PER_GROUP * g_star
    emask = (lane_f >= lo_lane) & (lane_f < lo_lane + EXPERTS_PER_GROUP)
    sel = jnp.where(emask, logits, NEG_INF)
    v1 = jnp.max(sel, axis=1, keepdims=True)
    i1 = jnp.min(jnp.where(emask & (sel == v1), lane_f, big), axis=1, keepdims=True)
    rest = emask & (lane_f != i1)
    sel2 = jnp.where(rest, logits, NEG_INF)
    v2 = jnp.max(sel2, axis=1, keepdims=True)
    i2 = jnp.min(jnp.where(rest & (sel2 == v2), lane_f, big), axis=1, keepdims=True)
    e2 = jnp.exp(v2 - v1)
    w1 = p_top / (1.0 + e2)
    w2 = p_top * e2 / (1.0 + e2)
    e1 = i1 - lo_lane
    e2x = i2 - lo_lane
    a = jnp.minimum(e1, e2x)
    b = jnp.maximum(e1, e2x)
    cls = g_star * N_PAIRS + a * (7.0 - a) * 0.5 + (b - a - 1.0)
    wa = jnp.where(e1 < e2x, w1, w2)
    wb = jnp.where(e1 < e2x, w2, w1)
    h2_ref[:, D_MODEL:EXT_WIDTH] = jnp.where(lane == 0, wa, jnp.where(lane == 1, wb, 0.0))
    row = lax.broadcasted_iota(jnp.int32, (tm, LANES), 0)
    spread = jnp.where(lane == (row & (LANES - 1)), cls, 0.0)
    cls_ref[0] = jnp.sum(spread.reshape(tm // LANES, LANES, LANES), axis=1).astype(jnp.int32)


def _outproj(oa, ob, x2, wo, g, wr, *, tm):
    n = x2.shape[0]
    return pl.pallas_call(
        _outproj_kernel,
        grid=(n // tm,),
        in_specs=[
            pl.BlockSpec((tm, WIDTH_A), lambda i: (i, 0)),
            pl.BlockSpec((tm, WIDTH_B), lambda i: (i, 0)),
            pl.BlockSpec((tm, D_MODEL), lambda i: (i, 0)),
            pl.BlockSpec(wo.shape, lambda i: (0, 0)),
            pl.BlockSpec((1, D_MODEL), lambda i: (0, 0)),
            pl.BlockSpec(wr.shape, lambda i: (0, 0)),
        ],
        out_specs=[
            pl.BlockSpec((tm, D_MODEL), lambda i: (i, 0)),
            pl.BlockSpec((tm, EXT_WIDTH), lambda i: (i, 0)),
            pl.BlockSpec((1, tm // LANES, LANES), lambda i: (i, 0, 0)),
        ],
        out_shape=[
            jax.ShapeDtypeStruct((n, D_MODEL), F32),
            jax.ShapeDtypeStruct((n, EXT_WIDTH), F32),
            jax.ShapeDtypeStruct((n // tm, tm // LANES, LANES), jnp.int32),
        ],
        compiler_params=pltpu.CompilerParams(dimension_semantics=("arbitrary",), vmem_limit_bytes=VMEM_LIMIT),
        name="outproj",
    )(oa, ob, x2, wo, g, wr)


def _sort_kernel(cls_ref, pos_ref, tile_ref, ends_ref, *, tile):
    cls = cls_ref[...]
    rows = cls.shape[0]
    upper = (lax.broadcasted_iota(jnp.int32, (LANES, LANES), 0)
             <= lax.broadcasted_iota(jnp.int32, (LANES, LANES), 1)).astype(BF16)
    earlier_rows = (lax.broadcasted_iota(jnp.int32, (rows, rows), 0)
                    > lax.broadcasted_iota(jnp.int32, (rows, rows), 1)).astype(BF16)
    ones = jnp.ones((LANES, LANES), BF16)
    lane8 = lax.broadcasted_iota(jnp.int32, tile_ref.shape, 1)
    tile_start = lane8.astype(F32) * tile
    off = jnp.zeros((1, 1), F32)
    pos = jnp.zeros(cls.shape, F32)
    tile_cls = jnp.zeros(tile_ref.shape, F32)
    ends = jnp.zeros(tile_ref.shape, F32)
    for c in range(N_CLASSES):
        hit = cls == c
        hot = jnp.where(hit, 1.0, 0.0).astype(BF16)
        in_row = jnp.dot(hot, upper, preferred_element_type=F32)
        row_tot = jnp.dot(hot, ones, preferred_element_type=F32)
        before = jnp.dot(earlier_rows, row_tot.astype(BF16), preferred_element_type=F32)
        count = before[rows - 1:rows, 0:1] + row_tot[rows - 1:rows, 0:1]
        pos = pos + jnp.where(hit, in_row - 1.0 + before + off, 0.0)
        off = off + jnp.ceil(count / tile) * tile
        tile_cls = tile_cls + jnp.where(tile_start >= off, 1.0, 0.0)
        ends = jnp.where(lane8 == c, off, ends)
    pos_ref[...] = pos.astype(jnp.int32)
    tile_ref[...] = tile_cls.astype(jnp.int32)
    ends_ref[...] = ends.astype(jnp.int32)


def _sort(cls2d, *, tile):
    meta = jax.ShapeDtypeStruct((8, LANES), jnp.int32)
    return pl.pallas_call(
        functools.partial(_sort_kernel, tile=tile),
        out_shape=[jax.ShapeDtypeStruct(cls2d.shape, jnp.int32), meta, meta],
        compiler_params=pltpu.CompilerParams(vmem_limit_bytes=VMEM_LIMIT),
        name="moe_sort",
    )(cls2d)


def _row_copy(src, src_row, dst, dst_row, sem, rows=1):
    return pltpu.make_async_copy(src.at[pl.ds(src_row, rows)], dst.at[pl.ds(dst_row, rows)], sem)


def _dispatch_kernel(pos_ref, ends_ref, h2_ref, xs_hbm, zero_scr, sem, *, ch, tile):
    g = pl.program_id(0)

    @pl.when(g == 0)
    def _zero_tails():
        zero_scr[...] = jnp.zeros_like(zero_scr)
        n_tiles = xs_hbm.shape[0] // tile
        min_used = n_tiles - N_CLASSES
        total = ends_ref[0, N_CLASSES - 1]
        for phase in ("start", "wait"):
            def zero_tile(row0):
                cp = pltpu.make_async_copy(zero_scr, xs_hbm.at[pl.ds(pl.multiple_of(row0, tile), tile)], sem)
                cp.start() if phase == "start" else cp.wait()

            for c in range(N_CLASSES):
                end = ends_ref[0, c]
                prev = ends_ref[0, c - 1] if c else 0
                pl.when(end > prev)(functools.partial(zero_tile, end - tile))
                pl.when((min_used + c) * tile >= total)(functools.partial(zero_tile, (min_used + c) * tile))

    for k in range(ch):
        _row_copy(h2_ref, k, xs_hbm, pos_ref[0, 0, k], sem).start(priority=k % DMA_THREADS)
    _row_copy(h2_ref, 0, xs_hbm, 0, sem, rows=ch).wait()


def _dispatch(pos3, ends, h2ext, *, n_rows, tile):
    n_chunks, _, ch = pos3.shape
    return pl.pallas_call(
        functools.partial(_dispatch_kernel, ch=ch, tile=tile),
        grid=(n_chunks,),
        in_specs=[
            pl.BlockSpec((1, 1, ch), lambda g: (g, 0, 0), memory_space=pltpu.SMEM),
            pl.BlockSpec(ends.shape, lambda g: (0, 0), memory_space=pltpu.SMEM),
            pl.BlockSpec((ch, EXT_WIDTH), lambda g: (g, 0)),
        ],
        out_specs=pl.BlockSpec(memory_space=pl.ANY),
        out_shape=jax.ShapeDtypeStruct((n_rows, EXT_WIDTH), F32),
        scratch_shapes=[pltpu.VMEM((tile, EXT_WIDTH), F32), pltpu.SemaphoreType.DMA(())],
        compiler_params=pltpu.CompilerParams(dimension_semantics=("arbitrary",), vmem_limit_bytes=VMEM_LIMIT),
        name="moe_dispatch",
    )(pos3, ends, h2ext)


def _experts_kernel(ea_ref, eb_ref, used_ref, xs_ref, wga, wua, wda, wgb, wub, wdb, ys_ref):
    del ea_ref, eb_ref

    @pl.when(pl.program_id(0) < used_ref[0])
    def _():
        x = xs_ref[:, 0:D_MODEL].astype(BF16)
        wts = xs_ref[:, D_MODEL:EXT_WIDTH]
        y = None
        for col, (wg, wu, wd) in enumerate(((wga, wua, wda), (wgb, wub, wdb))):
            gate = jnp.dot(x, wg[0].astype(BF16), preferred_element_type=F32)
            up = jnp.dot(x, wu[0].astype(BF16), preferred_element_type=F32)
            he = (gate / (1.0 + jnp.exp(-gate)) * up * wts[:, col:col + 1]).astype(BF16)
            t = jnp.dot(he, wd[0].astype(BF16), preferred_element_type=F32)
            y = t if y is None else y + t
        ys_ref[...] = y

    @pl.when(pl.program_id(0) >= used_ref[0])
    def _():
        ys_ref[...] = jnp.zeros_like(ys_ref)


def _experts(ea, eb, used, xs, wg, wu, wd, *, tile):
    n_tiles = xs.shape[0] // tile
    rows = lambda j, ea, eb, used: (jnp.minimum(j, used[0] - 1), 0)
    first = lambda j, ea, eb, used: (ea[j], 0, 0)
    second = lambda j, ea, eb, used: (eb[j], 0, 0)
    up_spec = lambda m: pl.BlockSpec((1, D_MODEL, D_EXPERT), m)
    down_spec = lambda m: pl.BlockSpec((1, D_EXPERT, D_MODEL), m)
    return pl.pallas_call(
        _experts_kernel,
        grid_spec=pltpu.PrefetchScalarGridSpec(
            num_scalar_prefetch=3,
            grid=(n_tiles,),
            in_specs=[pl.BlockSpec((tile, EXT_WIDTH), rows),
                      up_spec(first), up_spec(first), down_spec(first),
                      up_spec(second), up_spec(second), down_spec(second)],
            out_specs=pl.BlockSpec((tile, D_MODEL), lambda j, ea, eb, used: (j, 0)),
        ),
        out_shape=jax.ShapeDtypeStruct((xs.shape[0], D_MODEL), F32),
        compiler_params=pltpu.CompilerParams(dimension_semantics=("arbitrary",), vmem_limit_bytes=VMEM_LIMIT),
        name="moe_experts",
    )(ea, eb, used, xs, wg, wu, wd, wg, wu, wd)


def _combine_kernel(pos_ref, next_pos_ref, x1_ref, g_ref, ys_hbm, out_ref, y_scr, sem):
    i = pl.program_id(0)
    tm = x1_ref.shape[0]
    slot = i & 1

    def gather(p_ref, s):
        for k in range(tm):
            _row_copy(ys_hbm, p_ref[0, 0, k], y_scr.at[s], k, sem.at[s]).start(priority=k % DMA_THREADS)

    @pl.when(i == 0)
    def _():
        gather(pos_ref, 0)

    @pl.when(i + 1 < pl.num_programs(0))
    def _():
        gather(next_pos_ref, 1 - slot)

    _row_copy(ys_hbm, 0, y_scr.at[slot], 0, sem.at[slot], rows=tm).wait()
    x2 = x1_ref[...] + y_scr[slot]
    ms = jnp.mean(x2 * x2, axis=-1, keepdims=True)
    out_ref[...] = x2 * lax.rsqrt(ms + NORM_EPS) * g_ref[...]


def _combine(pos3, x1, g, ys):
    n_chunks, _, tm = pos3.shape
    return pl.pallas_call(
        _combine_kernel,
        grid=(n_chunks,),
        in_specs=[
            pl.BlockSpec((1, 1, tm), lambda i: (i, 0, 0), memory_space=pltpu.SMEM),
            pl.BlockSpec((1, 1, tm), lambda i: (jnp.minimum(i + 1, n_chunks - 1), 0, 0), memory_space=pltpu.SMEM),
            pl.BlockSpec((tm, D_MODEL), lambda i: (i, 0)),
            pl.BlockSpec((1, D_MODEL), lambda i: (0, 0)),
            pl.BlockSpec(memory_space=pl.ANY),
        ],
        out_specs=pl.BlockSpec((tm, D_MODEL), lambda i: (i, 0)),
        out_shape=jax.ShapeDtypeStruct(x1.shape, F32),
        scratch_shapes=[pltpu.VMEM((2, tm, D_MODEL), F32), pltpu.SemaphoreType.DMA((2,))],
        compiler_params=pltpu.CompilerParams(dimension_semantics=("arbitrary",), vmem_limit_bytes=VMEM_LIMIT),
        name="moe_combine",
    )(pos3, pos3, x1, g, ys)


def _moe(h2ext, cls3, x1, wg, wu, wd, g, *, tile=MOE_TILE, dispatch_chunk=512, combine_chunk=512):
    n = x1.shape[0]
    n_tiles = n // tile + N_CLASSES
    assert n_tiles <= LANES
    pos, tile_cls, ends = _sort(cls3.reshape(n // LANES, LANES), tile=tile)
    xs = _dispatch(pos.reshape(n // dispatch_chunk, 1, dispatch_chunk), ends, h2ext, n_rows=n_tiles * tile, tile=tile)
    used = ends[0, N_CLASSES - 1:N_CLASSES] // tile
    tc = tile_cls[0, :n_tiles]
    tc = jnp.where(jnp.arange(n_tiles) < used[0], tc, tc[jnp.maximum(used[0] - 1, 0)])
    pair_a = jnp.array([0, 0, 0, 1, 1, 2], jnp.int32)
    pair_b = jnp.array([1, 2, 3, 2, 3, 3], jnp.int32)
    ea = (tc // N_PAIRS) * EXPERTS_PER_GROUP + pair_a[tc % N_PAIRS]
    eb = (tc // N_PAIRS) * EXPERTS_PER_GROUP + pair_b[tc % N_PAIRS]
    ys = _experts(ea, eb, used, xs, wg, wu, wd, tile=tile)
    return _combine(pos.reshape(n // combine_chunk, 1, combine_chunk), x1, g, ys)


def _rotary_tables(seq):
    pos = np.arange(seq, dtype=np.float64)
    inv_freq = 1.0 / (ROPE_THETA ** (np.arange(0, ROT_DIM, 2, dtype=np.float64) / ROT_DIM))
    ang = pos[:, None] * inv_freq[None, :]
    cos, sin = np.cos(ang), np.sin(ang)
    zeros = np.zeros((seq, HEAD_DIM - ROT_DIM))
    zh = np.zeros((seq, ROT_HALF))
    cos_h = np.concatenate([cos, cos, np.ones((seq, HEAD_DIM - ROT_DIM))], axis=1)
    sa_h = np.concatenate([-sin, zh, zeros], axis=1)
    sb_h = np.concatenate([zh, sin, zeros], axis=1)
    tile = lambda t: jnp.asarray(np.concatenate([t, t], axis=1), F32)
    return tile(cos_h), tile(sa_h), tile(sb_h)


def kernel(x, attn_norm, w_in, b_forget, w_out, ffn_norm, w_group, w_expert, w_gate_e, w_up_e, w_down_e, final_norm):
    b, seq, d = x.shape
    assert d == D_MODEL and w_in.shape[0] == 1, "single-layer block"
    n = b * seq
    scale = HEAD_DIM ** -0.5 * LOG2E
    col_scale = jnp.ones((2 * QKV_WIDTH + HEADS_B,), F32)
    col_scale = col_scale.at[0:WIDTH_A].set(scale).at[QKV_WIDTH:QKV_WIDTH + WIDTH_B].set(scale)
    w = jnp.pad(w_in[0] * col_scale[None, :], ((0, 0), (0, LANES - HEADS_B))).astype(BF16)
    bf = jnp.pad(b_forget[0].astype(F32), (0, LANES - HEADS_B))[None, :]
    cosv, sa, sb = _rotary_tables(seq)
    x2 = x.reshape(n, d)

    qkva, qkvb, c = _inproj(x2, attn_norm[0][None, :], w, cosv, sa, sb, bf, seq=seq, tm=512)
    out_a = _dilated(qkva.reshape(b, seq, QKV_WIDTH))
    out_b = _fox(qkvb.reshape(b, seq, QKV_WIDTH), c.reshape(b, seq, LANES), tq=512)

    wr = jnp.pad(jnp.concatenate([w_group[0], w_expert[0]], axis=1).astype(F32),
                 ((0, 0), (0, LANES - N_GROUPS - N_EXPERTS)))
    wr_hi = wr.astype(BF16)
    wr = jnp.concatenate([wr_hi, (wr - wr_hi.astype(F32)).astype(BF16)], axis=1)
    x1, h2ext, cls3 = _outproj(out_a.reshape(n, WIDTH_A), out_b.reshape(n, WIDTH_B), x2,
                               w_out[0].astype(BF16), ffn_norm[0][None, :], wr, tm=512)
    out = _moe(h2ext, cls3, x1, w_gate_e[0], w_up_e[0], w_down_e[0],
               final_norm[None, :])
    return out.reshape(b, seq, d)
```

```python
import functools

import jax
import jax.numpy as jnp
import numpy as np
from jax import lax
from jax.experimental import pallas as pl
from jax.experimental.pallas import tpu as pltpu

F32 = jnp.float32
BF16 = jnp.bfloat16

D_MODEL = 1024
HEAD_DIM = 64
HEADS_A = 8
HEADS_B = 8
WIDTH_A = HEADS_A * HEAD_DIM
WIDTH_B = HEADS_B * HEAD_DIM
QKV_WIDTH = 3 * WIDTH_A
DILATIONS = ((128, 1), (512, 4), (2048, 16))
ROT_DIM = HEAD_DIM // 4
ROT_HALF = ROT_DIM // 2
ROPE_THETA = 500000.0
N_GROUPS = 4
EXPERTS_PER_GROUP = 4
N_EXPERTS = N_GROUPS * EXPERTS_PER_GROUP
D_EXPERT = 512
NORM_EPS = 1e-6
NEG_INF = -1e30

LANES = 128
HEAD_PAIRS = WIDTH_A // LANES
WIN = 128
MAX_DIL = 16
PERM = MAX_DIL * MAX_DIL
KV_BLOCK = 256
N_PAIRS = EXPERTS_PER_GROUP * (EXPERTS_PER_GROUP - 1) // 2
N_CLASSES = N_GROUPS * N_PAIRS
MOE_TILE = 256
EXT_WIDTH = D_MODEL + LANES
LOGIT_MAX = 60.0
BRANCH_UNROLL = 8
DMA_THREADS = 2
ROUTER_OFF = N_GROUPS
VMEM_LIMIT = 56 * 1024 * 1024
LOG2E = 1.4426950408889634
BOUND_SLACK = 1.0 + 2.0 ** -6
BOUND_EPS = 2.0 ** -7


def _split3(x):
    hi = x.astype(BF16)
    r1 = x - hi.astype(F32)
    mid = r1.astype(BF16)
    lo = (r1 - mid.astype(F32)).astype(BF16)
    return hi, mid, lo


def _inproj_kernel(x_ref, g_ref, w_ref, cos_ref, sa_ref, sb_ref, bf_ref,
                   qkva_ref, qkvb_ref, c_ref, h_scr, carry_scr, *, tiles_per_seq):
    i = pl.program_id(0)
    tm = x_ref.shape[0]
    x = x_ref[...]
    ms = jnp.mean(x * x, axis=-1, keepdims=True)
    h_scr[...] = (x * lax.rsqrt(ms + NORM_EPS) * g_ref[...]).astype(BF16)
    cosv = cos_ref[...]
    sa = sa_ref[...]
    sb = sb_ref[...]
    for g in range(6):
        p = jnp.dot(h_scr[...], w_ref[:, g * WIDTH_A:(g + 1) * WIDTH_A], preferred_element_type=F32)
        dst = qkva_ref if g < 3 else qkvb_ref
        col0 = (g % 3) * WIDTH_A
        if g < 2:
            for k in range(HEAD_PAIRS):
                t = p[:, k * LANES:(k + 1) * LANES]
                t = (t * cosv + pltpu.roll(t, LANES - ROT_HALF, 1) * sa + pltpu.roll(t, ROT_HALF, 1) * sb)
                dst[:, col0 + k * LANES:col0 + (k + 1) * LANES] = t.astype(BF16)
        else:
            dst[:, col0:col0 + WIDTH_A] = p.astype(BF16)
    z = jnp.dot(h_scr[...], w_ref[:, 2 * QKV_WIDTH:2 * QKV_WIDTH + LANES], preferred_element_type=F32) + bf_ref[...]
    lf = jnp.minimum(z, 0.0) - jnp.log1p(jnp.exp(-jnp.abs(z)))
    lane = lax.broadcasted_iota(jnp.int32, (tm, LANES), 1)
    lf = jnp.where(lane < HEADS_B, lf * LOG2E, 0.0)
    hi, mid, lo = _split3(lf)
    tri = (lax.broadcasted_iota(jnp.int32, (tm, tm), 0) >= lax.broadcasted_iota(jnp.int32, (tm, tm), 1)).astype(BF16)
    cs = (jnp.dot(tri, hi, preferred_element_type=F32) + jnp.dot(tri, mid, preferred_element_type=F32)
          + jnp.dot(tri, lo, preferred_element_type=F32))

    @pl.when(i % tiles_per_seq == 0)
    def _():
        carry_scr[...] = jnp.zeros_like(carry_scr)

    c = cs + carry_scr[0:1, :]
    c_ref[...] = c
    carry_scr[...] = jnp.broadcast_to(c[tm - 1:tm, :], carry_scr.shape)


def _inproj(x2, g, w, cosv, sa, sb, bf, *, seq, tm):
    n = x2.shape[0]
    kern = functools.partial(_inproj_kernel, tiles_per_seq=seq // tm)
    tps = seq // tm
    return pl.pallas_call(
        kern,
        grid=(n // tm,),
        in_specs=[
            pl.BlockSpec((tm, D_MODEL), lambda i: (i, 0)),
            pl.BlockSpec((1, D_MODEL), lambda i: (0, 0)),
            pl.BlockSpec(w.shape, lambda i: (0, 0)),
            pl.BlockSpec((tm, LANES), lambda i: (i % tps, 0)),
            pl.BlockSpec((tm, LANES), lambda i: (i % tps, 0)),
            pl.BlockSpec((tm, LANES), lambda i: (i % tps, 0)),
            pl.BlockSpec((1, LANES), lambda i: (0, 0)),
        ],
        out_specs=[
            pl.BlockSpec((tm, QKV_WIDTH), lambda i: (i, 0)),
            pl.BlockSpec((tm, QKV_WIDTH), lambda i: (i, 0)),
            pl.BlockSpec((tm, LANES), lambda i: (i, 0)),
        ],
        out_shape=[
            jax.ShapeDtypeStruct((n, QKV_WIDTH), BF16),
            jax.ShapeDtypeStruct((n, QKV_WIDTH), BF16),
            jax.ShapeDtypeStruct((n, LANES), F32),
        ],
        scratch_shapes=[pltpu.VMEM((tm, D_MODEL), BF16), pltpu.VMEM((8, LANES), F32)],
        compiler_params=pltpu.CompilerParams(dimension_semantics=("arbitrary",), vmem_limit_bytes=VMEM_LIMIT),
        name="inproj",
    )(x2, g, w, cosv, sa, sb, bf)


def _head_column(c_all, lane, head):
    return jnp.sum(jnp.where(lane == head, c_all, 0.0), axis=1, keepdims=True)


def _own_lanes(lane, hl):
    return (lane < HEAD_DIM) if hl == 0 else (lane >= HEAD_DIM)


def _own_sum(vals, lane, hl):
    return jnp.sum(jnp.where(_own_lanes(lane, hl), vals, 0.0), axis=1, keepdims=True)


def _head_sum_selector():
    lane = lax.broadcasted_iota(jnp.int32, (LANES, LANES), 0)
    col = lax.broadcasted_iota(jnp.int32, (LANES, LANES), 1)
    return jnp.where(((col == 0) & (lane < HEAD_DIM)) | ((col == 1) & (lane >= HEAD_DIM)), 1.0, 0.0).astype(BF16)


def _max_sq_norm(t, selector):
    return jnp.max(jnp.dot((t * t).astype(BF16), selector, preferred_element_type=F32), axis=0, keepdims=True)


def _norm_bound(qsq, ksq_max):
    return jnp.sqrt(qsq * ksq_max) * BOUND_SLACK + BOUND_EPS


def _fox_kernel(q_ref, k_ref, v_ref, c_ref, o_ref,
                q0_scr, q1_scr, k0_scr, k1_scr, v0_scr, v1_scr, bound_scr, acc0_scr, acc1_scr, s_scr,
                *, tq, seq, chunk):
    hp = pl.program_id(1)
    qi = pl.program_id(2)
    q_scr = (q0_scr, q1_scr)
    k_scr = (k0_scr, k1_scr)
    v_scr = (v0_scr, v1_scr)
    acc_scr = (acc0_scr, acc1_scr)

    def augmented(vals, cterm, lane, hl, key_side):
        hi, mid, lo = (t.astype(F32) for t in _split3(cterm))
        base = HEAD_DIM if hl == 0 else 0
        ones = (lane >= base + (0 if key_side else 3)) & (lane < base + (3 if key_side else 6))
        o = 3 if key_side else 0
        sgn = -1.0 if key_side else 1.0
        aug = jnp.where(lane == base + o, sgn * hi,
                        jnp.where(lane == base + o + 1, sgn * mid,
                                  jnp.where(lane == base + o + 2, sgn * lo,
                                            jnp.where(ones, 1.0, 0.0))))
        return jnp.where(_own_lanes(lane, hl), vals, aug).astype(BF16)

    @pl.when(qi == 0)
    def _build():
        lane = lax.broadcasted_iota(jnp.int32, (chunk, LANES), 1)

        def body(ci, sq_max):
            rows = pl.ds(pl.multiple_of(ci * chunk, chunk), chunk)
            qc = q_ref[0, rows, :].astype(F32)
            kc = k_ref[0, rows, :].astype(F32)
            vc = v_ref[0, rows, :].astype(F32)
            cc = c_ref[0, rows, :]
            for hl in range(2):
                ccol = _head_column(cc, lane, 2 * hp + hl)
                q_scr[hl][rows, :] = augmented(qc, ccol, lane, hl, False)
                k_scr[hl][rows, :] = augmented(kc, ccol, lane, hl, True)
                one_lane = HEAD_DIM if hl == 0 else 0
                v_scr[hl][rows, :] = jnp.where(
                    _own_lanes(lane, hl), vc, jnp.where(lane == one_lane, 1.0, 0.0)).astype(BF16)
            return (jnp.maximum(sq_max[0], _max_sq_norm(qc, selector)),
                    jnp.maximum(sq_max[1], _max_sq_norm(kc, selector)))

        selector = _head_sum_selector()
        sq_max = lax.fori_loop(0, seq // chunk, body, (jnp.zeros((1, LANES), F32),) * 2)
        bound_scr[...] = jnp.broadcast_to(_norm_bound(sq_max[0], sq_max[1]), bound_scr.shape)

    lane_q = lax.broadcasted_iota(jnp.int32, (tq, LANES), 1)
    q_rows = pl.ds(pl.multiple_of(qi * tq, tq), tq)
    worst = jnp.max(bound_scr[...])
    n_diag = tq // KV_BLOCK
    n_full = qi * n_diag
    row_minus_col = (lax.broadcasted_iota(jnp.int32, (tq, KV_BLOCK), 0)
                     - lax.broadcasted_iota(jnp.int32, (tq, KV_BLOCK), 1))

    def causal_mask(s, d, top=0):
        return jnp.where(row_minus_col[top:tq, :] >= d * KV_BLOCK, s, NEG_INF)

    def logits(hl, j):
        r0 = pl.multiple_of(j * KV_BLOCK, KV_BLOCK)
        return lax.dot_general(q_scr[hl][q_rows, :], k_scr[hl][pl.ds(r0, KV_BLOCK), :], (((1,), (1,)), ((), ())),
                               preferred_element_type=F32)

    def values(hl, j):
        return v_scr[hl][pl.ds(pl.multiple_of(j * KV_BLOCK, KV_BLOCK), KV_BLOCK), :]

    def finish(a0, a1):
        l0 = a0[:, HEAD_DIM:HEAD_DIM + 1]
        l1 = a1[:, 0:1]
        o_ref[0] = jnp.where(lane_q < HEAD_DIM, a0 / l0, a1 / l1).astype(o_ref.dtype)

    @pl.when(worst <= LOGIT_MAX)
    def _fast():
        acc0_scr[...] = jnp.zeros_like(acc0_scr)
        acc1_scr[...] = jnp.zeros_like(acc1_scr)

        def produce(j, slot):
            for hl in range(2):
                s_scr[slot, hl] = logits(hl, j)

        def consume(j, slot, diag):
            top = 0 if diag is None else diag * KV_BLOCK
            for hl in range(2):
                s = s_scr[slot, hl, top:tq, :]
                if diag is not None:
                    s = causal_mask(s, diag, top)
                acc_scr[hl][top:tq, :] += jnp.dot(jnp.exp2(s).astype(BF16), values(hl, j),
                                                  preferred_element_type=F32)

        for d in range(n_diag):
            produce(d, d)

        def body(i, carry):
            for d in range(n_diag):
                j = i * n_diag + d
                consume(j, d, None)
                produce(j + n_diag, d)
            return carry

        lax.fori_loop(0, qi, body, 0)
        for d in range(n_diag):
            consume(n_full + d, d, d)
        finish(acc0_scr[...], acc1_scr[...])

    @pl.when(worst > LOGIT_MAX)
    def _general():
        def step(j, carry, diag):
            new = []
            for hl in range(2):
                m, a = carry[2 * hl:2 * hl + 2]
                s = logits(hl, j)
                if diag is not None:
                    s = causal_mask(s, diag)
                mn = jnp.maximum(m, jnp.max(s, axis=1, keepdims=True))
                a = jnp.exp2(m - mn) * a + jnp.dot(jnp.exp2(s - mn).astype(BF16), values(hl, j),
                                                  preferred_element_type=F32)
                new += [mn, a]
            return tuple(new)

        carry = (jnp.full((tq, 1), NEG_INF, F32), jnp.zeros((tq, LANES), F32)) * 2
        carry = lax.fori_loop(0, n_full, lambda j, c: step(j, c, None), carry)
        for d in range(n_diag):
            carry = step(n_full + d, carry, d)
        finish(carry[1], carry[3])


def _fox(qkvb, c, *, tq, chunk=512):
    b, seq, _ = qkvb.shape
    kern = functools.partial(_fox_kernel, tq=tq, seq=seq, chunk=chunk)
    return pl.pallas_call(
        kern,
        grid=(b, HEAD_PAIRS, seq // tq),
        in_specs=[
            pl.BlockSpec((1, seq, LANES), lambda bi, hp, qi: (bi, 0, hp)),
            pl.BlockSpec((1, seq, LANES), lambda bi, hp, qi: (bi, 0, HEAD_PAIRS + hp)),
            pl.BlockSpec((1, seq, LANES), lambda bi, hp, qi: (bi, 0, 2 * HEAD_PAIRS + hp)),
            pl.BlockSpec((1, seq, LANES), lambda bi, hp, qi: (bi, 0, 0)),
        ],
        out_specs=pl.BlockSpec((1, tq, LANES), lambda bi, hp, qi: (bi, qi, hp)),
        out_shape=jax.ShapeDtypeStruct((b, seq, WIDTH_B), BF16),
        scratch_shapes=[pltpu.VMEM((seq, LANES), BF16)] * 6 + [
                        pltpu.VMEM((8, LANES), F32),
                        pltpu.VMEM((tq, LANES), F32), pltpu.VMEM((tq, LANES), F32),
                        pltpu.VMEM((tq // KV_BLOCK, 2, tq, KV_BLOCK), F32)],
        compiler_params=pltpu.CompilerParams(
            dimension_semantics=("arbitrary", "arbitrary", "arbitrary"), vmem_limit_bytes=VMEM_LIMIT),
        name="fox",
    )(qkvb, qkvb, qkvb, c)


def _dilated_general(q_ref, k_ref, v_ref, o_ref, q16, k16, v16, m_s, l_s, acc_s, on_s, lse_s, *, seq):
    sub = seq // MAX_DIL
    own0 = lax.broadcasted_iota(jnp.int32, (WIN, LANES), 1) < HEAD_DIM
    ri = lax.broadcasted_iota(jnp.int32, (PERM, PERM), 0)
    ci = lax.broadcasted_iota(jnp.int32, (PERM, PERM), 1)
    perm = (((ri >> 4) == (ci & 15)) & ((ri & 15) == (ci >> 4))).astype(BF16)

    qa = lax.broadcasted_iota(jnp.int32, (WIN, 2 * WIN), 0)
    kb_i = lax.broadcasted_iota(jnp.int32, (WIN, 2 * WIN), 1)
    cur = kb_i >= WIN
    kin = kb_i & (WIN - 1)
    dist_nat = qa - kin + jnp.where(cur, 0, WIN)
    band_nat = (dist_nat >= 0) & (dist_nat <= WIN)
    sq = ((qa & 31) << 2) + (qa >> 5)
    sk = ((kin & 31) << 2) + (kin >> 5)
    dist_4 = sq - sk + jnp.where(cur, 0, WIN)
    band_4 = (dist_4 >= 0) & (dist_4 <= WIN)

    def attend(qb, kb, vb, mask, state):
        qf = qb.astype(F32)
        new = []
        for hl in range(2):
            m, l, a = state[3 * hl:3 * hl + 3]
            own = own0 if hl == 0 else jnp.logical_not(own0)
            qh = jnp.where(own, qf, 0.0).astype(BF16)
            s = lax.dot_general(qh, kb, (((1,), (1,)), ((), ())), preferred_element_type=F32)
            s = jnp.where(mask, s, NEG_INF)
            mn = jnp.maximum(m, jnp.max(s, axis=1, keepdims=True))
            alpha = jnp.exp2(m - mn)
            p = jnp.exp2(s - mn)
            l = alpha * l + jnp.sum(p, axis=1, keepdims=True)
            a = alpha * a + jnp.dot(p.astype(BF16), vb, preferred_element_type=F32)
            new += [mn, l, a]
        return new

    def pack_state(st):
        m0, l0, a0, m1, l1, a1 = st
        return (jnp.where(own0, m0, m1), jnp.where(own0, l0, l1), jnp.where(own0, a0, a1))

    def deint(blk, carry):
        r0 = pl.multiple_of(blk * PERM, PERM)
        j0 = pl.multiple_of(blk * MAX_DIL, MAX_DIL)
        for src, dst in ((q_ref, q16), (k_ref, k16), (v_ref, v16)):
            y = jnp.dot(perm, src[0, pl.ds(r0, PERM), :], preferred_element_type=F32).astype(BF16)
            for r in range(MAX_DIL):
                dst[r, pl.ds(j0, MAX_DIL), :] = y[r * MAX_DIL:(r + 1) * MAX_DIL, :]
        return carry

    lax.fori_loop(0, seq // PERM, deint, 0)

    fresh = [jnp.full((WIN, 1), NEG_INF, F32), jnp.zeros((WIN, 1), F32), jnp.zeros((WIN, LANES), F32)] * 2
    nb16 = sub // WIN

    def d16_body(t, carry):
        r = t // nb16
        n = t % nb16
        c0 = pl.multiple_of(n * WIN, WIN)
        p0 = pl.multiple_of(jnp.maximum(n - 1, 0) * WIN, WIN)
        qb = q16[r, pl.ds(c0, WIN), :]
        kb = jnp.concatenate([k16[r, pl.ds(p0, WIN), :], k16[r, pl.ds(c0, WIN), :]], axis=0)
        vb = jnp.concatenate([v16[r, pl.ds(p0, WIN), :], v16[r, pl.ds(c0, WIN), :]], axis=0)
        mask = band_nat & (cur | (n > 0))
        mm, ll, aa = pack_state(attend(qb, kb, vb, mask, fresh))
        row = pl.multiple_of(r * sub + c0, WIN)
        m_s[pl.ds(row, WIN), :] = mm
        l_s[pl.ds(row, WIN), :] = ll
        acc_s[pl.ds(row, WIN), :] = aa
        return carry

    lax.fori_loop(0, MAX_DIL * nb16, d16_body, 0)

    nb4 = (seq // 4) // WIN
    ch = WIN // 4

    def d4_body(t, carry):
        r4 = t // nb4
        n = t % nb4
        c0 = pl.multiple_of(n * ch, ch)
        p0 = pl.multiple_of(jnp.maximum(n - 1, 0) * ch, ch)

        def gather(ref, j0):
            return [ref[r4 + 4 * q, pl.ds(j0, ch), :] for q in range(4)]

        qb = jnp.concatenate(gather(q16, c0), axis=0)
        kb = jnp.concatenate(gather(k16, p0) + gather(k16, c0), axis=0)
        vb = jnp.concatenate(gather(v16, p0) + gather(v16, c0), axis=0)
        rows = [pl.multiple_of((r4 + 4 * q) * sub + c0, ch) for q in range(4)]
        mm = jnp.concatenate([m_s[pl.ds(rw, ch), :] for rw in rows], axis=0)
        ll = jnp.concatenate([l_s[pl.ds(rw, ch), :] for rw in rows], axis=0)
        aa = jnp.concatenate([acc_s[pl.ds(rw, ch), :] for rw in rows], axis=0)
        state = [mm[:, 0:1], ll[:, 0:1], aa, mm[:, HEAD_DIM:HEAD_DIM + 1], ll[:, HEAD_DIM:HEAD_DIM + 1], aa]
        mask = band_4 & (cur | (n > 0))
        mm, ll, aa = pack_state(attend(qb, kb, vb, mask, state))
        for q, rw in enumerate(rows):
            m_s[pl.ds(rw, ch), :] = mm[q * ch:(q + 1) * ch, :]
            l_s[pl.ds(rw, ch), :] = ll[q * ch:(q + 1) * ch, :]
            acc_s[pl.ds(rw, ch), :] = aa[q * ch:(q + 1) * ch, :]
        return carry

    lax.fori_loop(0, 4 * nb4, d4_body, 0)

    def renat(blk, carry):
        j0 = pl.multiple_of(blk * MAX_DIL, MAX_DIL)
        r0 = pl.multiple_of(blk * PERM, PERM)
        rows = [pl.multiple_of(r * sub + j0, MAX_DIL) for r in range(MAX_DIL)]
        mm = jnp.concatenate([m_s[pl.ds(rw, MAX_DIL), :] for rw in rows], axis=0)
        ll = jnp.concatenate([l_s[pl.ds(rw, MAX_DIL), :] for rw in rows], axis=0)
        aa = jnp.concatenate([acc_s[pl.ds(rw, MAX_DIL), :] for rw in rows], axis=0)
        o = (aa / ll).astype(BF16)
        hi, mid, lo = _split3(mm + jnp.log2(ll))
        on_s[pl.ds(r0, PERM), :] = jnp.dot(perm, o, preferred_element_type=F32)
        lse_s[pl.ds(r0, PERM), :] = (jnp.dot(perm, hi, preferred_element_type=F32)
                                     + jnp.dot(perm, mid, preferred_element_type=F32)
                                     + jnp.dot(perm, lo, preferred_element_type=F32))
        return carry

    lax.fori_loop(0, seq // PERM, renat, 0)

    def d1_body(n, carry):
        c0 = pl.multiple_of(n * WIN, WIN)
        p0 = pl.multiple_of(jnp.maximum(n - 1, 0) * WIN, WIN)
        qb = q_ref[0, pl.ds(c0, WIN), :]
        kb = jnp.concatenate([k_ref[0, pl.ds(p0, WIN), :], k_ref[0, pl.ds(c0, WIN), :]], axis=0)
        vb = jnp.concatenate([v_ref[0, pl.ds(p0, WIN), :], v_ref[0, pl.ds(c0, WIN), :]], axis=0)
        lse = lse_s[pl.ds(c0, WIN), :]
        o = on_s[pl.ds(c0, WIN), :]
        one = jnp.ones((WIN, 1), F32)
        state = [lse[:, 0:1], one, o, lse[:, HEAD_DIM:HEAD_DIM + 1], one, o]
        mask = band_nat & (cur | (n > 0))
        m0, l0, a0, m1, l1, a1 = attend(qb, kb, vb, mask, state)
        o_ref[0, pl.ds(c0, WIN), :] = jnp.where(own0, a0 / l0, a1 / l1).astype(o_ref.dtype)
        return carry

    lax.fori_loop(0, seq // WIN, d1_body, 0)


def _dilated_fast(o_ref, qn, kn, vn, qr, kr, vr, fr, fn, s_scr, mask_scr, *, seq):
    sub = seq // MAX_DIL
    own0 = lax.broadcasted_iota(jnp.int32, (WIN, LANES), 1) < HEAD_DIM
    head_mask = [jnp.where(own0, 1.0, 0.0).astype(BF16), jnp.where(own0, 0.0, 1.0).astype(BF16)]
    ri = lax.broadcasted_iota(jnp.int32, (PERM, PERM), 0)
    ci = lax.broadcasted_iota(jnp.int32, (PERM, PERM), 1)
    perm = (((ri >> 4) == (ci & 15)) & ((ri & 15) == (ci >> 4))).astype(BF16)

    qa = lax.broadcasted_iota(jnp.int32, (WIN, 2 * WIN), 0)
    kb_i = lax.broadcasted_iota(jnp.int32, (WIN, 2 * WIN), 1)
    cur = kb_i >= WIN
    kin = kb_i & (WIN - 1)
    dist_nat = qa - kin + jnp.where(cur, 0, WIN)
    band_nat = (dist_nat >= 0) & (dist_nat <= WIN)
    sq = ((qa & 31) << 2) + (qa >> 5)
    sk = ((kin & 31) << 2) + (kin >> 5)
    dist_4 = sq - sk + jnp.where(cur, 0, WIN)
    band_4 = (dist_4 >= 0) & (dist_4 <= WIN)
    for i, band in enumerate((band_nat, band_4)):
        mask_scr[2 * i] = jnp.where(band & cur, 1.0, 0.0).astype(BF16)
        mask_scr[2 * i + 1] = jnp.where(band, 1.0, 0.0).astype(BF16)

    def deint(i, carry):
        for u in range(2):
            blk = 2 * i + u
            r0 = pl.multiple_of(blk * PERM, PERM)
            j0 = pl.multiple_of(blk * MAX_DIL, MAX_DIL)
            for srcs, dsts in (((qn[0], kn[0]), (qr[0], kr[0])), (vn, vr)):
                both = jnp.concatenate([src[pl.ds(r0, PERM), :] for src in srcs], axis=1)
                y = jnp.dot(perm, both, preferred_element_type=F32).astype(BF16)
                for r in range(MAX_DIL):
                    for half, dst in enumerate(dsts):
                        dst[r, pl.ds(j0, MAX_DIL), :] = y[r * MAX_DIL:(r + 1) * MAX_DIL, half * LANES:(half + 1) * LANES]
        return carry

    lax.fori_loop(0, seq // PERM // 2, deint, 0)

    def run_branch(nblocks, block_in_seq, fetch, mask_base, sink):
        def produce(t, pair, u):
            for hl in range(2):
                qb, kb = fetch(t, hl, True)
                s_scr[pair, u, hl] = lax.dot_general(qb * head_mask[hl], kb, (((1,), (1,)), ((), ())),
                                                     preferred_element_type=F32)

        def consume(t, pair, u):
            mk = mask_scr[mask_base + jnp.minimum(block_in_seq(t), 1)]
            pv = []
            for hl in range(2):
                p = jnp.exp2(s_scr[pair, u, hl]).astype(BF16) * mk
                pv.append(jnp.dot(p, fetch(t, hl, False), preferred_element_type=F32))
            sink(t, pv)

        for u in range(BRANCH_UNROLL):
            produce(u, 0, u)

        def body(i, carry):
            pair = i & 1
            for u in range(BRANCH_UNROLL):
                consume(BRANCH_UNROLL * i + u, pair, u)
            for u in range(BRANCH_UNROLL):
                produce(jnp.minimum(BRANCH_UNROLL * (i + 1) + u, nblocks - 1), 1 - pair, u)
            return carry

        lax.fori_loop(0, nblocks // BRANCH_UNROLL, body, 0)

    def prev_cur(ref_block, n, size):
        c0 = pl.multiple_of(n * size, size)
        p0 = pl.multiple_of(jnp.maximum(n - 1, 0) * size, size)
        return ref_block(p0) + ref_block(c0)

    nb16 = sub // WIN

    def fetch16(t, hl, qk):
        r, n = t // nb16, t % nb16
        if qk:
            return (qr[hl][r, pl.ds(pl.multiple_of(n * WIN, WIN), WIN), :],
                    jnp.concatenate(prev_cur(lambda s: [kr[hl][r, pl.ds(s, WIN), :]], n, WIN), axis=0))
        return jnp.concatenate(prev_cur(lambda s: [vr[hl][r, pl.ds(s, WIN), :]], n, WIN), axis=0)

    def sink16(t, pv):
        row = pl.multiple_of(t * WIN, WIN)
        for hl in range(2):
            fr[hl][pl.ds(row, WIN), :] = pv[hl]

    run_branch(MAX_DIL * nb16, lambda t: t % nb16, fetch16, 0, sink16)

    nb4 = (seq // 4) // WIN
    ch = WIN // 4

    def fetch4(t, hl, qk):
        r4, n = t // nb4, t % nb4
        gather = lambda ref: (lambda s: [ref[r4 + 4 * q, pl.ds(s, ch), :] for q in range(4)])
        if qk:
            return (jnp.concatenate(gather(qr[hl])(pl.multiple_of(n * ch, ch)), axis=0),
                    jnp.concatenate(prev_cur(gather(kr[hl]), n, ch), axis=0))
        return jnp.concatenate(prev_cur(gather(vr[hl]), n, ch), axis=0)

    def sink4(t, pv):
        r4, n = t // nb4, t % nb4
        for q in range(4):
            row = pl.multiple_of((r4 + 4 * q) * sub + n * ch, ch)
            for hl in range(2):
                fr[hl][pl.ds(row, ch), :] += pv[hl][q * ch:(q + 1) * ch, :]

    run_branch(4 * nb4, lambda t: t % nb4, fetch4, 2, sink4)

    def renat(i, carry):
        for u in range(2):
            blk = 2 * i + u
            j0 = pl.multiple_of(blk * MAX_DIL, MAX_DIL)
            r0 = pl.multiple_of(blk * PERM, PERM)
            for hl in range(2):
                a = jnp.concatenate([fr[hl][pl.ds(pl.multiple_of(r * sub + j0, MAX_DIL), MAX_DIL), :]
                                     for r in range(MAX_DIL)], axis=0)
                hi = a.astype(BF16)
                lo = (a - hi.astype(F32)).astype(BF16)
                y = jnp.dot(perm, jnp.concatenate([hi, lo], axis=1), preferred_element_type=F32)
                fn[hl][pl.ds(r0, PERM), :] = y[:, 0:LANES] + y[:, LANES:2 * LANES]
        return carry

    lax.fori_loop(0, seq // PERM // 2, renat, 0)

    def fetch1(t, hl, qk):
        if qk:
            return (qn[hl][pl.ds(pl.multiple_of(t * WIN, WIN), WIN), :],
                    jnp.concatenate(prev_cur(lambda s: [kn[hl][pl.ds(s, WIN), :]], t, WIN), axis=0))
        return jnp.concatenate(prev_cur(lambda s: [vn[hl][pl.ds(s, WIN), :]], t, WIN), axis=0)

    def sink1(t, pv):
        row = pl.multiple_of(t * WIN, WIN)
        t0 = fn[0][pl.ds(row, WIN), :] + pv[0]
        t1 = fn[1][pl.ds(row, WIN), :] + pv[1]
        o_ref[0, pl.ds(row, WIN), :] = jnp.where(own0, t0 / t0[:, HEAD_DIM:HEAD_DIM + 1],
                                                 t1 / t1[:, 0:1]).astype(o_ref.dtype)

    run_branch(seq // WIN, lambda t: t, fetch1, 0, sink1)


def _dilated_kernel(q_ref, k_ref, v_ref, o_ref, vn0, vn1, qr, kr, vr0, vr1,
                    f0, f1, f2, f3, f4, s_scr, mask_scr, *, seq, chunk):
    lane = lax.broadcasted_iota(jnp.int32, (chunk, LANES), 1)
    vn = (vn0, vn1)

    def stats(ci, sq_max):
        rows = pl.ds(pl.multiple_of(ci * chunk, chunk), chunk)
        qc = q_ref[0, rows, :].astype(F32)
        kc = k_ref[0, rows, :].astype(F32)
        vc = v_ref[0, rows, :].astype(F32)
        for hl in range(2):
            ones_lane = HEAD_DIM if hl == 0 else 0
            vn[hl][rows, :] = jnp.where(_own_lanes(lane, hl), vc, jnp.where(lane == ones_lane, 1.0, 0.0)).astype(BF16)
        return (jnp.maximum(sq_max[0], _max_sq_norm(qc, selector)),
                jnp.maximum(sq_max[1], _max_sq_norm(kc, selector)))

    selector = _head_sum_selector()
    sq_max = lax.fori_loop(0, seq // chunk, stats, (jnp.zeros((1, LANES), F32),) * 2)
    worst = jnp.max(_norm_bound(sq_max[0], sq_max[1]))

    @pl.when(worst <= LOGIT_MAX)
    def _fast():
        qn, kn = q_ref.at[0], k_ref.at[0]
        _dilated_fast(o_ref, (qn, qn), (kn, kn), vn, (qr, qr), (kr, kr), (vr0, vr1), (f0, f1), (f2, f3),
                      s_scr, mask_scr, seq=seq)

    @pl.when(worst > LOGIT_MAX)
    def _general():
        _dilated_general(q_ref, k_ref, v_ref, o_ref, qr, kr, vr0, f0, f1, f2, f3, f4, seq=seq)


def _dilated(qkva, *, chunk=512):
    b, seq, _ = qkva.shape
    sub = seq // MAX_DIL
    kern = functools.partial(_dilated_kernel, seq=seq, chunk=chunk)
    return pl.pallas_call(
        kern,
        grid=(b, HEAD_PAIRS),
        in_specs=[
            pl.BlockSpec((1, seq, LANES), lambda bi, hp: (bi, 0, hp)),
            pl.BlockSpec((1, seq, LANES), lambda bi, hp: (bi, 0, HEAD_PAIRS + hp)),
            pl.BlockSpec((1, seq, LANES), lambda bi, hp: (bi, 0, 2 * HEAD_PAIRS + hp)),
        ],
        out_specs=pl.BlockSpec((1, seq, LANES), lambda bi, hp: (bi, 0, hp)),
        out_shape=jax.ShapeDtypeStruct((b, seq, WIDTH_A), BF16),
        scratch_shapes=(
            [pltpu.VMEM((seq, LANES), BF16)] * 2
            + [pltpu.VMEM((MAX_DIL, sub, LANES), BF16)] * 4
            + [pltpu.VMEM((seq, LANES), F32)] * 5
            + [pltpu.VMEM((2, BRANCH_UNROLL, 2, WIN, 2 * WIN), F32), pltpu.VMEM((4, WIN, 2 * WIN), BF16)]
        ),
        compiler_params=pltpu.CompilerParams(
            dimension_semantics=("arbitrary", "arbitrary"), vmem_limit_bytes=VMEM_LIMIT),
        name="dilated",
    )(qkva, qkva, qkva)


def _outproj_kernel(oa_ref, ob_ref, x_ref, wo_ref, g_ref, wr_ref, x1_ref, h2_ref, cls_ref):
    tm = x_ref.shape[0]
    x1 = (x_ref[...]
          + jnp.dot(oa_ref[...], wo_ref[0:WIDTH_A, :], preferred_element_type=F32)
          + jnp.dot(ob_ref[...], wo_ref[WIDTH_A:WIDTH_A + WIDTH_B, :], preferred_element_type=F32))
    x1_ref[...] = x1
    ms = jnp.mean(x1 * x1, axis=-1, keepdims=True)
    h2 = x1 * lax.rsqrt(ms + NORM_EPS) * g_ref[...]
    h2_ref[:, 0:D_MODEL] = h2
    h2_hi = h2.astype(BF16)
    h2_lo = (h2 - h2_hi.astype(F32)).astype(BF16)
    both = jnp.dot(h2_hi, wr_ref[...], preferred_element_type=F32)
    logits = (both[:, 0:LANES] + both[:, LANES:2 * LANES]
              + jnp.dot(h2_lo, wr_ref[:, 0:LANES], preferred_element_type=F32))
    lane = lax.broadcasted_iota(jnp.int32, (tm, LANES), 1)
    lane_f = lane.astype(F32)
    big = float(LANES)
    gmask = lane < N_GROUPS
    gl = jnp.where(gmask, logits, NEG_INF)
    gmax = jnp.max(gl, axis=1, keepdims=True)
    gsum = jnp.sum(jnp.where(gmask, jnp.exp(gl - gmax), 0.0), axis=1, keepdims=True)
    p_top = 1.0 / gsum
    g_star = jnp.min(jnp.where(gmask & (gl == gmax), lane_f, big), axis=1, keepdims=True)
    lo_lane = ROUTER_OFF + EXPERTS_PER_GROUP * g_star
    emask = (lane_f >= lo_lane) & (lane_f < lo_lane + EXPERTS_PER_GROUP)
    sel = jnp.where(emask, logits, NEG_INF)
    v1 = jnp.max(sel, axis=1, keepdims=True)
    i1 = jnp.min(jnp.where(emask & (sel == v1), lane_f, big), axis=1, keepdims=True)
    rest = emask & (lane_f != i1)
    sel2 = jnp.where(rest, logits, NEG_INF)
    v2 = jnp.max(sel2, axis=1, keepdims=True)
    i2 = jnp.min(jnp.where(rest & (sel2 == v2), lane_f, big), axis=1, keepdims=True)
    e2 = jnp.exp(v2 - v1)
    w1 = p_top / (1.0 + e2)
    w2 = p_top * e2 / (1.0 + e2)
    e1 = i1 - lo_lane
    e2x = i2 - lo_lane
    a = jnp.minimum(e1, e2x)
    b = jnp.maximum(e1, e2x)
    cls = g_star * N_PAIRS + a * (7.0 - a) * 0.5 + (b - a - 1.0)
    wa = jnp.where(e1 < e2x, w1, w2)
    wb = jnp.where(e1 < e2x, w2, w1)
    h2_ref[:, D_MODEL:EXT_WIDTH] = jnp.where(lane == 0, wa, jnp.where(lane == 1, wb, 0.0))
    row = lax.broadcasted_iota(jnp.int32, (tm, LANES), 0)
    spread = jnp.where(lane == (row & (LANES - 1)), cls, 0.0)
    cls_ref[0] = jnp.sum(spread.reshape(tm // LANES, LANES, LANES), axis=1).astype(jnp.int32)


def _outproj(oa, ob, x2, wo, g, wr, *, tm):
    n = x2.shape[0]
    return pl.pallas_call(
        _outproj_kernel,
        grid=(n // tm,),
        in_specs=[
            pl.BlockSpec((tm, WIDTH_A), lambda i: (i, 0)),
            pl.BlockSpec((tm, WIDTH_B), lambda i: (i, 0)),
            pl.BlockSpec((tm, D_MODEL), lambda i: (i, 0)),
            pl.BlockSpec(wo.shape, lambda i: (0, 0)),
            pl.BlockSpec((1, D_MODEL), lambda i: (0, 0)),
            pl.BlockSpec(wr.shape, lambda i: (0, 0)),
        ],
        out_specs=[
            pl.BlockSpec((tm, D_MODEL), lambda i: (i, 0)),
            pl.BlockSpec((tm, EXT_WIDTH), lambda i: (i, 0)),
            pl.BlockSpec((1, tm // LANES, LANES), lambda i: (i, 0, 0)),
        ],
        out_shape=[
            jax.ShapeDtypeStruct((n, D_MODEL), F32),
            jax.ShapeDtypeStruct((n, EXT_WIDTH), F32),
            jax.ShapeDtypeStruct((n // tm, tm // LANES, LANES), jnp.int32),
        ],
        compiler_params=pltpu.CompilerParams(dimension_semantics=("arbitrary",), vmem_limit_bytes=VMEM_LIMIT),
        name="outproj",
    )(oa, ob, x2, wo, g, wr)


def _sort_kernel(cls_ref, pos_ref, tile_ref, ends_ref, *, tile):
    cls = cls_ref[...]
    rows = cls.shape[0]
    upper = (lax.broadcasted_iota(jnp.int32, (LANES, LANES), 0)
             <= lax.broadcasted_iota(jnp.int32, (LANES, LANES), 1)).astype(BF16)
    earlier_rows = (lax.broadcasted_iota(jnp.int32, (rows, rows), 0)
                    > lax.broadcasted_iota(jnp.int32, (rows, rows), 1)).astype(BF16)
    ones = jnp.ones((LANES, LANES), BF16)
    lane8 = lax.broadcasted_iota(jnp.int32, tile_ref.shape, 1)
    tile_start = lane8.astype(F32) * tile
    off = jnp.zeros((1, 1), F32)
    pos = jnp.zeros(cls.shape, F32)
    tile_cls = jnp.zeros(tile_ref.shape, F32)
    ends = jnp.zeros(tile_ref.shape, F32)
    for c in range(N_CLASSES):
        hit = cls == c
        hot = jnp.where(hit, 1.0, 0.0).astype(BF16)
        in_row = jnp.dot(hot, upper, preferred_element_type=F32)
        row_tot = jnp.dot(hot, ones, preferred_element_type=F32)
        before = jnp.dot(earlier_rows, row_tot.astype(BF16), preferred_element_type=F32)
        count = before[rows - 1:rows, 0:1] + row_tot[rows - 1:rows, 0:1]
        pos = pos + jnp.where(hit, in_row - 1.0 + before + off, 0.0)
        off = off + jnp.ceil(count / tile) * tile
        tile_cls = tile_cls + jnp.where(tile_start >= off, 1.0, 0.0)
        ends = jnp.where(lane8 == c, off, ends)
    pos_ref[...] = pos.astype(jnp.int32)
    tile_ref[...] = tile_cls.astype(jnp.int32)
    ends_ref[...] = ends.astype(jnp.int32)


def _sort(cls2d, *, tile):
    meta = jax.ShapeDtypeStruct((8, LANES), jnp.int32)
    return pl.pallas_call(
        functools.partial(_sort_kernel, tile=tile),
        out_shape=[jax.ShapeDtypeStruct(cls2d.shape, jnp.int32), meta, meta],
        compiler_params=pltpu.CompilerParams(vmem_limit_bytes=VMEM_LIMIT),
        name="moe_sort",
    )(cls2d)


def _row_copy(src, src_row, dst, dst_row, sem, rows=1):
    return pltpu.make_async_copy(src.at[pl.ds(src_row, rows)], dst.at[pl.ds(dst_row, rows)], sem)


def _dispatch_kernel(pos_ref, ends_ref, h2_ref, xs_hbm, zero_scr, sem, *, ch, tile):
    g = pl.program_id(0)

    @pl.when(g == 0)
    def _zero_tails():
        zero_scr[...] = jnp.zeros_like(zero_scr)
        n_tiles = xs_hbm.shape[0] // tile
        min_used = n_tiles - N_CLASSES
        total = ends_ref[0, N_CLASSES - 1]
        for phase in ("start", "wait"):
            def zero_tile(row0):
                cp = pltpu.make_async_copy(zero_scr, xs_hbm.at[pl.ds(pl.multiple_of(row0, tile), tile)], sem)
                cp.start() if phase == "start" else cp.wait()

            for c in range(N_CLASSES):
                end = ends_ref[0, c]
                prev = ends_ref[0, c - 1] if c else 0
                pl.when(end > prev)(functools.partial(zero_tile, end - tile))
                pl.when((min_used + c) * tile >= total)(functools.partial(zero_tile, (min_used + c) * tile))

    for k in range(ch):
        _row_copy(h2_ref, k, xs_hbm, pos_ref[0, 0, k], sem).start(priority=k % DMA_THREADS)
    _row_copy(h2_ref, 0, xs_hbm, 0, sem, rows=ch).wait()


def _dispatch(pos3, ends, h2ext, *, n_rows, tile):
    n_chunks, _, ch = pos3.shape
    return pl.pallas_call(
        functools.partial(_dispatch_kernel, ch=ch, tile=tile),
        grid=(n_chunks,),
        in_specs=[
            pl.BlockSpec((1, 1, ch), lambda g: (g, 0, 0), memory_space=pltpu.SMEM),
            pl.BlockSpec(ends.shape, lambda g: (0, 0), memory_space=pltpu.SMEM),
            pl.BlockSpec((ch, EXT_WIDTH), lambda g: (g, 0)),
        ],
        out_specs=pl.BlockSpec(memory_space=pl.ANY),
        out_shape=jax.ShapeDtypeStruct((n_rows, EXT_WIDTH), F32),
        scratch_shapes=[pltpu.VMEM((tile, EXT_WIDTH), F32), pltpu.SemaphoreType.DMA(())],
        compiler_params=pltpu.CompilerParams(dimension_semantics=("arbitrary",), vmem_limit_bytes=VMEM_LIMIT),
        name="moe_dispatch",
    )(pos3, ends, h2ext)


def _experts_kernel(ea_ref, eb_ref, used_ref, xs_ref, wga, wua, wda, wgb, wub, wdb, ys_ref):
    del ea_ref, eb_ref

    @pl.when(pl.program_id(0) < used_ref[0])
    def _():
        x = xs_ref[:, 0:D_MODEL].astype(BF16)
        wts = xs_ref[:, D_MODEL:EXT_WIDTH]
        y = None
        for col, (wg, wu, wd) in enumerate(((wga, wua, wda), (wgb, wub, wdb))):
            gate = jnp.dot(x, wg[0].astype(BF16), preferred_element_type=F32)
            up = jnp.dot(x, wu[0].astype(BF16), preferred_element_type=F32)
            he = (gate / (1.0 + jnp.exp(-gate)) * up * wts[:, col:col + 1]).astype(BF16)
            t = jnp.dot(he, wd[0].astype(BF16), preferred_element_type=F32)
            y = t if y is None else y + t
        ys_ref[...] = y

    @pl.when(pl.program_id(0) >= used_ref[0])
    def _():
        ys_ref[...] = jnp.zeros_like(ys_ref)


def _experts(ea, eb, used, xs, wg, wu, wd, *, tile):
    n_tiles = xs.shape[0] // tile
    rows = lambda j, ea, eb, used: (jnp.minimum(j, used[0] - 1), 0)
    first = lambda j, ea, eb, used: (ea[j], 0, 0)
    second = lambda j, ea, eb, used: (eb[j], 0, 0)
    up_spec = lambda m: pl.BlockSpec((1, D_MODEL, D_EXPERT), m)
    down_spec = lambda m: pl.BlockSpec((1, D_EXPERT, D_MODEL), m)
    return pl.pallas_call(
        _experts_kernel,
        grid_spec=pltpu.PrefetchScalarGridSpec(
            num_scalar_prefetch=3,
            grid=(n_tiles,),
            in_specs=[pl.BlockSpec((tile, EXT_WIDTH), rows),
                      up_spec(first), up_spec(first), down_spec(first),
                      up_spec(second), up_spec(second), down_spec(second)],
            out_specs=pl.BlockSpec((tile, D_MODEL), lambda j, ea, eb, used: (j, 0)),
        ),
        out_shape=jax.ShapeDtypeStruct((xs.shape[0], D_MODEL), F32),
        compiler_params=pltpu.CompilerParams(dimension_semantics=("arbitrary",), vmem_limit_bytes=VMEM_LIMIT),
        name="moe_experts",
    )(ea, eb, used, xs, wg, wu, wd, wg, wu, wd)


def _combine_kernel(pos_ref, next_pos_ref, x1_ref, g_ref, ys_hbm, out_ref, y_scr, sem):
    i = pl.program_id(0)
    tm = x1_ref.shape[0]
    slot = i & 1

    def gather(p_ref, s):
        for k in range(tm):
            _row_copy(ys_hbm, p_ref[0, 0, k], y_scr.at[s], k, sem.at[s]).start(priority=k % DMA_THREADS)

    @pl.when(i == 0)
    def _():
        gather(pos_ref, 0)

    @pl.when(i + 1 < pl.num_programs(0))
    def _():
        gather(next_pos_ref, 1 - slot)

    _row_copy(ys_hbm, 0, y_scr.at[slot], 0, sem.at[slot], rows=tm).wait()
    x2 = x1_ref[...] + y_scr[slot]
    ms = jnp.mean(x2 * x2, axis=-1, keepdims=True)
    out_ref[...] = x2 * lax.rsqrt(ms + NORM_EPS) * g_ref[...]


def _combine(pos3, x1, g, ys):
    n_chunks, _, tm = pos3.shape
    return pl.pallas_call(
        _combine_kernel,
        grid=(n_chunks,),
        in_specs=[
            pl.BlockSpec((1, 1, tm), lambda i: (i, 0, 0), memory_space=pltpu.SMEM),
            pl.BlockSpec((1, 1, tm), lambda i: (jnp.minimum(i + 1, n_chunks - 1), 0, 0), memory_space=pltpu.SMEM),
            pl.BlockSpec((tm, D_MODEL), lambda i: (i, 0)),
            pl.BlockSpec((1, D_MODEL), lambda i: (0, 0)),
            pl.BlockSpec(memory_space=pl.ANY),
        ],
        out_specs=pl.BlockSpec((tm, D_MODEL), lambda i: (i, 0)),
        out_shape=jax.ShapeDtypeStruct(x1.shape, F32),
        scratch_shapes=[pltpu.VMEM((2, tm, D_MODEL), F32), pltpu.SemaphoreType.DMA((2,))],
        compiler_params=pltpu.CompilerParams(dimension_semantics=("arbitrary",), vmem_limit_bytes=VMEM_LIMIT),
        name="moe_combine",
    )(pos3, pos3, x1, g, ys)


def _moe(h2ext, cls3, x1, wg, wu, wd, g, *, tile=MOE_TILE, dispatch_chunk=512, combine_chunk=512):
    n = x1.shape[0]
    n_tiles = n // tile + N_CLASSES
    assert n_tiles <= LANES
    pos, tile_cls, ends = _sort(cls3.reshape(n // LANES, LANES), tile=tile)
    xs = _dispatch(pos.reshape(n // dispatch_chunk, 1, dispatch_chunk), ends, h2ext, n_rows=n_tiles * tile, tile=tile)
    used = ends[0, N_CLASSES - 1:N_CLASSES] // tile
    tc = tile_cls[0, :n_tiles]
    tc = jnp.where(jnp.arange(n_tiles) < used[0], tc, tc[jnp.maximum(used[0] - 1, 0)])
    pair_a = jnp.array([0, 0, 0, 1, 1, 2], jnp.int32)
    pair_b = jnp.array([1, 2, 3, 2, 3, 3], jnp.int32)
    ea = (tc // N_PAIRS) * EXPERTS_PER_GROUP + pair_a[tc % N_PAIRS]
    eb = (tc // N_PAIRS) * EXPERTS_PER_GROUP + pair_b[tc % N_PAIRS]
    ys = _experts(ea, eb, used, xs, wg, wu, wd, tile=tile)
    return _combine(pos.reshape(n // combine_chunk, 1, combine_chunk), x1, g, ys)


def _rotary_tables(seq):
    pos = np.arange(seq, dtype=np.float64)
    inv_freq = 1.0 / (ROPE_THETA ** (np.arange(0, ROT_DIM, 2, dtype=np.float64) / ROT_DIM))
    ang = pos[:, None] * inv_freq[None, :]
    cos, sin = np.cos(ang), np.sin(ang)
    zeros = np.zeros((seq, HEAD_DIM - ROT_DIM))
    zh = np.zeros((seq, ROT_HALF))
    cos_h = np.concatenate([cos, cos, np.ones((seq, HEAD_DIM - ROT_DIM))], axis=1)
    sa_h = np.concatenate([-sin, zh, zeros], axis=1)
    sb_h = np.concatenate([zh, sin, zeros], axis=1)
    tile = lambda t: jnp.asarray(np.concatenate([t, t], axis=1), F32)
    return tile(cos_h), tile(sa_h), tile(sb_h)


def kernel(x, attn_norm, w_in, b_forget, w_out, ffn_norm, w_group, w_expert, w_gate_e, w_up_e, w_down_e, final_norm):
    b, seq, d = x.shape
    assert d == D_MODEL and w_in.shape[0] == 1, "single-layer block"
    n = b * seq
    scale = HEAD_DIM ** -0.5 * LOG2E
    col_scale = jnp.ones((2 * QKV_WIDTH + HEADS_B,), F32)
    col_scale = col_scale.at[0:WIDTH_A].set(scale).at[QKV_WIDTH:QKV_WIDTH + WIDTH_B].set(scale)
    w = jnp.pad(w_in[0] * col_scale[None, :], ((0, 0), (0, LANES - HEADS_B))).astype(BF16)
    bf = jnp.pad(b_forget[0].astype(F32), (0, LANES - HEADS_B))[None, :]
    cosv, sa, sb = _rotary_tables(seq)
    x2 = x.reshape(n, d)

    qkva, qkvb, c = _inproj(x2, attn_norm[0][None, :], w, cosv, sa, sb, bf, seq=seq, tm=512)
    out_a = _dilated(qkva.reshape(b, seq, QKV_WIDTH))
    out_b = _fox(qkvb.reshape(b, seq, QKV_WIDTH), c.reshape(b, seq, LANES), tq=1024)

    wr = jnp.pad(jnp.concatenate([w_group[0], w_expert[0]], axis=1).astype(F32),
                 ((0, 0), (0, LANES - N_GROUPS - N_EXPERTS)))
    wr_hi = wr.astype(BF16)
    wr = jnp.concatenate([wr_hi, (wr - wr_hi.astype(F32)).astype(BF16)], axis=1)
    x1, h2ext, cls3 = _outproj(out_a.reshape(n, WIDTH_A), out_b.reshape(n, WIDTH_B), x2,
                               w_out[0].astype(BF16), ffn_norm[0][None, :], wr, tm=512)
    out = _moe(h2ext, cls3, x1, w_gate_e[0], w_up_e[0], w_down_e[0],
               final_norm[None, :])
    return out.reshape(b, seq, d)
```

```python
import functools

import jax
import jax.numpy as jnp
import numpy as np
from jax import lax
from jax.experimental import pallas as pl
from jax.experimental.pallas import tpu as pltpu

F32 = jnp.float32
BF16 = jnp.bfloat16

D_MODEL = 1024
HEAD_DIM = 64
HEADS_A = 8
HEADS_B = 8
WIDTH_A = HEADS_A * HEAD_DIM
WIDTH_B = HEADS_B * HEAD_DIM
QKV_WIDTH = 3 * WIDTH_A
DILATIONS = ((128, 1), (512, 4), (2048, 16))
ROT_DIM = HEAD_DIM // 4
ROT_HALF = ROT_DIM // 2
ROPE_THETA = 500000.0
N_GROUPS = 4
EXPERTS_PER_GROUP = 4
N_EXPERTS = N_GROUPS * EXPERTS_PER_GROUP
D_EXPERT = 512
NORM_EPS = 1e-6
NEG_INF = -1e30

LANES = 128
HEAD_PAIRS = WIDTH_A // LANES
WIN = 128
MAX_DIL = 16
PERM = MAX_DIL * MAX_DIL
KV_BLOCK = 256
N_PAIRS = EXPERTS_PER_GROUP * (EXPERTS_PER_GROUP - 1) // 2
N_CLASSES = N_GROUPS * N_PAIRS
MOE_TILE = 256
EXT_WIDTH = D_MODEL + LANES
LOGIT_MAX = 60.0
BRANCH_UNROLL = 8
DMA_THREADS = 2
ROUTER_OFF = N_GROUPS
VMEM_LIMIT = 56 * 1024 * 1024
LOG2E = 1.4426950408889634
BOUND_SLACK = 1.0 + 2.0 ** -6
BOUND_EPS = 2.0 ** -7


def _split3(x):
    hi = x.astype(BF16)
    r1 = x - hi.astype(F32)
    mid = r1.astype(BF16)
    lo = (r1 - mid.astype(F32)).astype(BF16)
    return hi, mid, lo


def _inproj_kernel(x_ref, g_ref, w_ref, cos_ref, sa_ref, sb_ref, bf_ref,
                   qkva_ref, qkvb_ref, c_ref, h_scr, carry_scr, *, tiles_per_seq):
    i = pl.program_id(0)
    tm = x_ref.shape[0]
    x = x_ref[...]
    ms = jnp.mean(x * x, axis=-1, keepdims=True)
    h_scr[...] = (x * lax.rsqrt(ms + NORM_EPS) * g_ref[...]).astype(BF16)
    cosv = cos_ref[...]
    sa = sa_ref[...]
    sb = sb_ref[...]
    for g in range(6):
        p = jnp.dot(h_scr[...], w_ref[:, g * WIDTH_A:(g + 1) * WIDTH_A], preferred_element_type=F32)
        dst = qkva_ref if g < 3 else qkvb_ref
        col0 = (g % 3) * WIDTH_A
        if g < 2:
            for k in range(HEAD_PAIRS):
                t = p[:, k * LANES:(k + 1) * LANES]
                t = (t * cosv + pltpu.roll(t, LANES - ROT_HALF, 1) * sa + pltpu.roll(t, ROT_HALF, 1) * sb)
                dst[:, col0 + k * LANES:col0 + (k + 1) * LANES] = t.astype(BF16)
        else:
            dst[:, col0:col0 + WIDTH_A] = p.astype(BF16)
    z = jnp.dot(h_scr[...], w_ref[:, 2 * QKV_WIDTH:2 * QKV_WIDTH + LANES], preferred_element_type=F32) + bf_ref[...]
    lf = jnp.minimum(z, 0.0) - jnp.log1p(jnp.exp(-jnp.abs(z)))
    lane = lax.broadcasted_iota(jnp.int32, (tm, LANES), 1)
    lf = jnp.where(lane < HEADS_B, lf * LOG2E, 0.0)
    hi, mid, lo = _split3(lf)
    tri = (lax.broadcasted_iota(jnp.int32, (tm, tm), 0) >= lax.broadcasted_iota(jnp.int32, (tm, tm), 1)).astype(BF16)
    cs = (jnp.dot(tri, hi, preferred_element_type=F32) + jnp.dot(tri, mid, preferred_element_type=F32)
          + jnp.dot(tri, lo, preferred_element_type=F32))

    @pl.when(i % tiles_per_seq == 0)
    def _():
        carry_scr[...] = jnp.zeros_like(carry_scr)

    c = cs + carry_scr[0:1, :]
    c_ref[...] = c
    carry_scr[...] = jnp.broadcast_to(c[tm - 1:tm, :], carry_scr.shape)


def _inproj(x2, g, w, cosv, sa, sb, bf, *, seq, tm):
    n = x2.shape[0]
    kern = functools.partial(_inproj_kernel, tiles_per_seq=seq // tm)
    tps = seq // tm
    return pl.pallas_call(
        kern,
        grid=(n // tm,),
        in_specs=[
            pl.BlockSpec((tm, D_MODEL), lambda i: (i, 0)),
            pl.BlockSpec((1, D_MODEL), lambda i: (0, 0)),
            pl.BlockSpec(w.shape, lambda i: (0, 0)),
            pl.BlockSpec((tm, LANES), lambda i: (i % tps, 0)),
            pl.BlockSpec((tm, LANES), lambda i: (i % tps, 0)),
            pl.BlockSpec((tm, LANES), lambda i: (i % tps, 0)),
            pl.BlockSpec((1, LANES), lambda i: (0, 0)),
        ],
        out_specs=[
            pl.BlockSpec((tm, QKV_WIDTH), lambda i: (i, 0)),
            pl.BlockSpec((tm, QKV_WIDTH), lambda i: (i, 0)),
            pl.BlockSpec((tm, LANES), lambda i: (i, 0)),
        ],
        out_shape=[
            jax.ShapeDtypeStruct((n, QKV_WIDTH), BF16),
            jax.ShapeDtypeStruct((n, QKV_WIDTH), BF16),
            jax.ShapeDtypeStruct((n, LANES), F32),
        ],
        scratch_shapes=[pltpu.VMEM((tm, D_MODEL), BF16), pltpu.VMEM((8, LANES), F32)],
        compiler_params=pltpu.CompilerParams(dimension_semantics=("arbitrary",), vmem_limit_bytes=VMEM_LIMIT),
        name="inproj",
    )(x2, g, w, cosv, sa, sb, bf)


def _head_column(c_all, lane, head):
    return jnp.sum(jnp.where(lane == head, c_all, 0.0), axis=1, keepdims=True)


def _own_lanes(lane, hl):
    return (lane < HEAD_DIM) if hl == 0 else (lane >= HEAD_DIM)


def _own_sum(vals, lane, hl):
    return jnp.sum(jnp.where(_own_lanes(lane, hl), vals, 0.0), axis=1, keepdims=True)


def _head_sum_selector():
    lane = lax.broadcasted_iota(jnp.int32, (LANES, LANES), 0)
    col = lax.broadcasted_iota(jnp.int32, (LANES, LANES), 1)
    return jnp.where(((col == 0) & (lane < HEAD_DIM)) | ((col == 1) & (lane >= HEAD_DIM)), 1.0, 0.0).astype(BF16)


def _max_sq_norm(t, selector):
    return jnp.max(jnp.dot((t * t).astype(BF16), selector, preferred_element_type=F32), axis=0, keepdims=True)


def _norm_bound(qsq, ksq_max):
    return jnp.sqrt(qsq * ksq_max) * BOUND_SLACK + BOUND_EPS


def _fox_kernel(q_ref, k_ref, v_ref, c_ref, o_ref,
                q0_scr, q1_scr, k0_scr, k1_scr, v0_scr, v1_scr, bound_scr, acc0_scr, acc1_scr, s_scr,
                *, tq, seq, chunk):
    hp = pl.program_id(1)
    qi = pl.program_id(2)
    q_scr = (q0_scr, q1_scr)
    k_scr = (k0_scr, k1_scr)
    v_scr = (v0_scr, v1_scr)
    acc_scr = (acc0_scr, acc1_scr)

    def augmented(vals, cterm, lane, hl, key_side):
        hi, mid, lo = (t.astype(F32) for t in _split3(cterm))
        base = HEAD_DIM if hl == 0 else 0
        ones = (lane >= base + (0 if key_side else 3)) & (lane < base + (3 if key_side else 6))
        o = 3 if key_side else 0
        sgn = -1.0 if key_side else 1.0
        aug = jnp.where(lane == base + o, sgn * hi,
                        jnp.where(lane == base + o + 1, sgn * mid,
                                  jnp.where(lane == base + o + 2, sgn * lo,
                                            jnp.where(ones, 1.0, 0.0))))
        return jnp.where(_own_lanes(lane, hl), vals, aug).astype(BF16)

    @pl.when(qi == 0)
    def _build():
        lane = lax.broadcasted_iota(jnp.int32, (chunk, LANES), 1)

        def body(ci, sq_max):
            rows = pl.ds(pl.multiple_of(ci * chunk, chunk), chunk)
            qc = q_ref[0, rows, :].astype(F32)
            kc = k_ref[0, rows, :].astype(F32)
            vc = v_ref[0, rows, :].astype(F32)
            cc = c_ref[0, rows, :]
            for hl in range(2):
                ccol = _head_column(cc, lane, 2 * hp + hl)
                q_scr[hl][rows, :] = augmented(qc, ccol, lane, hl, False)
                k_scr[hl][rows, :] = augmented(kc, ccol, lane, hl, True)
                one_lane = HEAD_DIM if hl == 0 else 0
                v_scr[hl][rows, :] = jnp.where(
                    _own_lanes(lane, hl), vc, jnp.where(lane == one_lane, 1.0, 0.0)).astype(BF16)
            return (jnp.maximum(sq_max[0], _max_sq_norm(qc, selector)),
                    jnp.maximum(sq_max[1], _max_sq_norm(kc, selector)))

        selector = _head_sum_selector()
        sq_max = lax.fori_loop(0, seq // chunk, body, (jnp.zeros((1, LANES), F32),) * 2)
        bound_scr[...] = jnp.broadcast_to(_norm_bound(sq_max[0], sq_max[1]), bound_scr.shape)

    lane_q = lax.broadcasted_iota(jnp.int32, (tq, LANES), 1)
    q_rows = pl.ds(pl.multiple_of(qi * tq, tq), tq)
    worst = jnp.max(bound_scr[...])
    n_diag = tq // KV_BLOCK
    n_full = qi * n_diag
    row_minus_col = (lax.broadcasted_iota(jnp.int32, (tq, KV_BLOCK), 0)
                     - lax.broadcasted_iota(jnp.int32, (tq, KV_BLOCK), 1))

    def causal_mask(s, d, top=0):
        return jnp.where(row_minus_col[top:tq, :] >= d * KV_BLOCK, s, NEG_INF)

    def logits(hl, j, top=0):
        r0 = pl.multiple_of(j * KV_BLOCK, KV_BLOCK)
        rows = q_rows if top == 0 else pl.ds(pl.multiple_of(qi * tq + top, KV_BLOCK), tq - top)
        return lax.dot_general(q_scr[hl][rows, :], k_scr[hl][pl.ds(r0, KV_BLOCK), :], (((1,), (1,)), ((), ())),
                               preferred_element_type=F32)

    def values(hl, j):
        return v_scr[hl][pl.ds(pl.multiple_of(j * KV_BLOCK, KV_BLOCK), KV_BLOCK), :]

    def finish(a0, a1):
        l0 = a0[:, HEAD_DIM:HEAD_DIM + 1]
        l1 = a1[:, 0:1]
        o_ref[0] = jnp.where(lane_q < HEAD_DIM, a0 / l0, a1 / l1).astype(o_ref.dtype)

    @pl.when(worst <= LOGIT_MAX)
    def _fast():
        acc0_scr[...] = jnp.zeros_like(acc0_scr)
        acc1_scr[...] = jnp.zeros_like(acc1_scr)

        def produce(j, slot, top=0):
            for hl in range(2):
                s_scr[slot, hl, top:tq, :] = logits(hl, j, top)

        def consume(j, slot, diag):
            top = 0 if diag is None else diag * KV_BLOCK
            for hl in range(2):
                s = s_scr[slot, hl, top:tq, :]
                if diag is not None:
                    s = causal_mask(s, diag, top)
                acc_scr[hl][top:tq, :] += jnp.dot(jnp.exp2(s).astype(BF16), values(hl, j),
                                                  preferred_element_type=F32)

        def produce_diagonal(d):
            produce(n_full + d, d, d * KV_BLOCK)

        @pl.when(qi == 0)
        def _():
            for d in range(n_diag):
                produce_diagonal(d)

        @pl.when(qi > 0)
        def _():
            for d in range(n_diag):
                produce(d, d)

            def body(i, carry):
                for d in range(n_diag):
                    j = i * n_diag + d
                    consume(j, d, None)
                    produce(j + n_diag, d)
                return carry

            lax.fori_loop(0, qi - 1, body, 0)
            for d in range(n_diag):
                consume(n_full - n_diag + d, d, None)
                produce_diagonal(d)

        for d in range(n_diag):
            consume(n_full + d, d, d)
        finish(acc0_scr[...], acc1_scr[...])

    @pl.when(worst > LOGIT_MAX)
    def _general():
        def step(j, carry, diag):
            new = []
            for hl in range(2):
                m, a = carry[2 * hl:2 * hl + 2]
                s = logits(hl, j)
                if diag is not None:
                    s = causal_mask(s, diag)
                mn = jnp.maximum(m, jnp.max(s, axis=1, keepdims=True))
                a = jnp.exp2(m - mn) * a + jnp.dot(jnp.exp2(s - mn).astype(BF16), values(hl, j),
                                                  preferred_element_type=F32)
                new += [mn, a]
            return tuple(new)

        carry = (jnp.full((tq, 1), NEG_INF, F32), jnp.zeros((tq, LANES), F32)) * 2
        carry = lax.fori_loop(0, n_full, lambda j, c: step(j, c, None), carry)
        for d in range(n_diag):
            carry = step(n_full + d, carry, d)
        finish(carry[1], carry[3])


def _fox(qkvb, c, *, tq, chunk=512):
    b, seq, _ = qkvb.shape
    kern = functools.partial(_fox_kernel, tq=tq, seq=seq, chunk=chunk)
    return pl.pallas_call(
        kern,
        grid=(b, HEAD_PAIRS, seq // tq),
        in_specs=[
            pl.BlockSpec((1, seq, LANES), lambda bi, hp, qi: (bi, 0, hp)),
            pl.BlockSpec((1, seq, LANES), lambda bi, hp, qi: (bi, 0, HEAD_PAIRS + hp)),
            pl.BlockSpec((1, seq, LANES), lambda bi, hp, qi: (bi, 0, 2 * HEAD_PAIRS + hp)),
            pl.BlockSpec((1, seq, LANES), lambda bi, hp, qi: (bi, 0, 0)),
        ],
        out_specs=pl.BlockSpec((1, tq, LANES), lambda bi, hp, qi: (bi, qi, hp)),
        out_shape=jax.ShapeDtypeStruct((b, seq, WIDTH_B), BF16),
        scratch_shapes=[pltpu.VMEM((seq, LANES), BF16)] * 6 + [
                        pltpu.VMEM((8, LANES), F32),
                        pltpu.VMEM((tq, LANES), F32), pltpu.VMEM((tq, LANES), F32),
                        pltpu.VMEM((tq // KV_BLOCK, 2, tq, KV_BLOCK), F32)],
        compiler_params=pltpu.CompilerParams(
            dimension_semantics=("arbitrary", "arbitrary", "arbitrary"), vmem_limit_bytes=VMEM_LIMIT),
        name="fox",
    )(qkvb, qkvb, qkvb, c)


def _dilated_general(q_ref, k_ref, v_ref, o_ref, q16, k16, v16, m_s, l_s, acc_s, on_s, lse_s, *, seq):
    sub = seq // MAX_DIL
    own0 = lax.broadcasted_iota(jnp.int32, (WIN, LANES), 1) < HEAD_DIM
    ri = lax.broadcasted_iota(jnp.int32, (PERM, PERM), 0)
    ci = lax.broadcasted_iota(jnp.int32, (PERM, PERM), 1)
    perm = (((ri >> 4) == (ci & 15)) & ((ri & 15) == (ci >> 4))).astype(BF16)

    qa = lax.broadcasted_iota(jnp.int32, (WIN, 2 * WIN), 0)
    kb_i = lax.broadcasted_iota(jnp.int32, (WIN, 2 * WIN), 1)
    cur = kb_i >= WIN
    kin = kb_i & (WIN - 1)
    dist_nat = qa - kin + jnp.where(cur, 0, WIN)
    band_nat = (dist_nat >= 0) & (dist_nat <= WIN)
    sq = ((qa & 31) << 2) + (qa >> 5)
    sk = ((kin & 31) << 2) + (kin >> 5)
    dist_4 = sq - sk + jnp.where(cur, 0, WIN)
    band_4 = (dist_4 >= 0) & (dist_4 <= WIN)

    def attend(qb, kb, vb, mask, state):
        qf = qb.astype(F32)
        new = []
        for hl in range(2):
            m, l, a = state[3 * hl:3 * hl + 3]
            own = own0 if hl == 0 else jnp.logical_not(own0)
            qh = jnp.where(own, qf, 0.0).astype(BF16)
            s = lax.dot_general(qh, kb, (((1,), (1,)), ((), ())), preferred_element_type=F32)
            s = jnp.where(mask, s, NEG_INF)
            mn = jnp.maximum(m, jnp.max(s, axis=1, keepdims=True))
            alpha = jnp.exp2(m - mn)
            p = jnp.exp2(s - mn)
            l = alpha * l + jnp.sum(p, axis=1, keepdims=True)
            a = alpha * a + jnp.dot(p.astype(BF16), vb, preferred_element_type=F32)
            new += [mn, l, a]
        return new

    def pack_state(st):
        m0, l0, a0, m1, l1, a1 = st
        return (jnp.where(own0, m0, m1), jnp.where(own0, l0, l1), jnp.where(own0, a0, a1))

    def deint(blk, carry):
        r0 = pl.multiple_of(blk * PERM, PERM)
        j0 = pl.multiple_of(blk * MAX_DIL, MAX_DIL)
        for src, dst in ((q_ref, q16), (k_ref, k16), (v_ref, v16)):
            y = jnp.dot(perm, src[0, pl.ds(r0, PERM), :], preferred_element_type=F32).astype(BF16)
            for r in range(MAX_DIL):
                dst[r, pl.ds(j0, MAX_DIL), :] = y[r * MAX_DIL:(r + 1) * MAX_DIL, :]
        return carry

    lax.fori_loop(0, seq // PERM, deint, 0)

    fresh = [jnp.full((WIN, 1), NEG_INF, F32), jnp.zeros((WIN, 1), F32), jnp.zeros((WIN, LANES), F32)] * 2
    nb16 = sub // WIN

    def d16_body(t, carry):
        r = t // nb16
        n = t % nb16
        c0 = pl.multiple_of(n * WIN, WIN)
        p0 = pl.multiple_of(jnp.maximum(n - 1, 0) * WIN, WIN)
        qb = q16[r, pl.ds(c0, WIN), :]
        kb = jnp.concatenate([k16[r, pl.ds(p0, WIN), :], k16[r, pl.ds(c0, WIN), :]], axis=0)
        vb = jnp.concatenate([v16[r, pl.ds(p0, WIN), :], v16[r, pl.ds(c0, WIN), :]], axis=0)
        mask = band_nat & (cur | (n > 0))
        mm, ll, aa = pack_state(attend(qb, kb, vb, mask, fresh))
        row = pl.multiple_of(r * sub + c0, WIN)
        m_s[pl.ds(row, WIN), :] = mm
        l_s[pl.ds(row, WIN), :] = ll
        acc_s[pl.ds(row, WIN), :] = aa
        return carry

    lax.fori_loop(0, MAX_DIL * nb16, d16_body, 0)

    nb4 = (seq // 4) // WIN
    ch = WIN // 4

    def d4_body(t, carry):
        r4 = t // nb4
        n = t % nb4
        c0 = pl.multiple_of(n * ch, ch)
        p0 = pl.multiple_of(jnp.maximum(n - 1, 0) * ch, ch)

        def gather(ref, j0):
            return [ref[r4 + 4 * q, pl.ds(j0, ch), :] for q in range(4)]

        qb = jnp.concatenate(gather(q16, c0), axis=0)
        kb = jnp.concatenate(gather(k16, p0) + gather(k16, c0), axis=0)
        vb = jnp.concatenate(gather(v16, p0) + gather(v16, c0), axis=0)
        rows = [pl.multiple_of((r4 + 4 * q) * sub + c0, ch) for q in range(4)]
        mm = jnp.concatenate([m_s[pl.ds(rw, ch), :] for rw in rows], axis=0)
        ll = jnp.concatenate([l_s[pl.ds(rw, ch), :] for rw in rows], axis=0)
        aa = jnp.concatenate([acc_s[pl.ds(rw, ch), :] for rw in rows], axis=0)
        state = [mm[:, 0:1], ll[:, 0:1], aa, mm[:, HEAD_DIM:HEAD_DIM + 1], ll[:, HEAD_DIM:HEAD_DIM + 1], aa]
        mask = band_4 & (cur | (n > 0))
        mm, ll, aa = pack_state(attend(qb, kb, vb, mask, state))
        for q, rw in enumerate(rows):
            m_s[pl.ds(rw, ch), :] = mm[q * ch:(q + 1) * ch, :]
            l_s[pl.ds(rw, ch), :] = ll[q * ch:(q + 1) * ch, :]
            acc_s[pl.ds(rw, ch), :] = aa[q * ch:(q + 1) * ch, :]
        return carry

    lax.fori_loop(0, 4 * nb4, d4_body, 0)

    def renat(blk, carry):
        j0 = pl.multiple_of(blk * MAX_DIL, MAX_DIL)
        r0 = pl.multiple_of(blk * PERM, PERM)
        rows = [pl.multiple_of(r * sub + j0, MAX_DIL) for r in range(MAX_DIL)]
        mm = jnp.concatenate([m_s[pl.ds(rw, MAX_DIL), :] for rw in rows], axis=0)
        ll = jnp.concatenate([l_s[pl.ds(rw, MAX_DIL), :] for rw in rows], axis=0)
        aa = jnp.concatenate([acc_s[pl.ds(rw, MAX_DIL), :] for rw in rows], axis=0)
        o = (aa / ll).astype(BF16)
        hi, mid, lo = _split3(mm + jnp.log2(ll))
        on_s[pl.ds(r0, PERM), :] = jnp.dot(perm, o, preferred_element_type=F32)
        lse_s[pl.ds(r0, PERM), :] = (jnp.dot(perm, hi, preferred_element_type=F32)
                                     + jnp.dot(perm, mid, preferred_element_type=F32)
                                     + jnp.dot(perm, lo, preferred_element_type=F32))
        return carry

    lax.fori_loop(0, seq // PERM, renat, 0)

    def d1_body(n, carry):
        c0 = pl.multiple_of(n * WIN, WIN)
        p0 = pl.multiple_of(jnp.maximum(n - 1, 0) * WIN, WIN)
        qb = q_ref[0, pl.ds(c0, WIN), :]
        kb = jnp.concatenate([k_ref[0, pl.ds(p0, WIN), :], k_ref[0, pl.ds(c0, WIN), :]], axis=0)
        vb = jnp.concatenate([v_ref[0, pl.ds(p0, WIN), :], v_ref[0, pl.ds(c0, WIN), :]], axis=0)
        lse = lse_s[pl.ds(c0, WIN), :]
        o = on_s[pl.ds(c0, WIN), :]
        one = jnp.ones((WIN, 1), F32)
        state = [lse[:, 0:1], one, o, lse[:, HEAD_DIM:HEAD_DIM + 1], one, o]
        mask = band_nat & (cur | (n > 0))
        m0, l0, a0, m1, l1, a1 = attend(qb, kb, vb, mask, state)
        o_ref[0, pl.ds(c0, WIN), :] = jnp.where(own0, a0 / l0, a1 / l1).astype(o_ref.dtype)
        return carry

    lax.fori_loop(0, seq // WIN, d1_body, 0)


def _dilated_fast(o_ref, qn, kn, vn, qr, kr, vr, fr, fn, s_scr, mask_scr, *, seq):
    sub = seq // MAX_DIL
    own0 = lax.broadcasted_iota(jnp.int32, (WIN, LANES), 1) < HEAD_DIM
    head_mask = [jnp.where(own0, 1.0, 0.0).astype(BF16), jnp.where(own0, 0.0, 1.0).astype(BF16)]
    ri = lax.broadcasted_iota(jnp.int32, (PERM, PERM), 0)
    ci = lax.broadcasted_iota(jnp.int32, (PERM, PERM), 1)
    perm = (((ri >> 4) == (ci & 15)) & ((ri & 15) == (ci >> 4))).astype(BF16)

    qa = lax.broadcasted_iota(jnp.int32, (WIN, 2 * WIN), 0)
    kb_i = lax.broadcasted_iota(jnp.int32, (WIN, 2 * WIN), 1)
    cur = kb_i >= WIN
    kin = kb_i & (WIN - 1)
    dist_nat = qa - kin + jnp.where(cur, 0, WIN)
    band_nat = (dist_nat >= 0) & (dist_nat <= WIN)
    sq = ((qa & 31) << 2) + (qa >> 5)
    sk = ((kin & 31) << 2) + (kin >> 5)
    dist_4 = sq - sk + jnp.where(cur, 0, WIN)
    band_4 = (dist_4 >= 0) & (dist_4 <= WIN)
    for i, band in enumerate((band_nat, band_4)):
        mask_scr[2 * i] = jnp.where(band & cur, 1.0, 0.0).astype(BF16)
        mask_scr[2 * i + 1] = jnp.where(band, 1.0, 0.0).astype(BF16)

    def deint(i, carry):
        for u in range(2):
            blk = 2 * i + u
            r0 = pl.multiple_of(blk * PERM, PERM)
            j0 = pl.multiple_of(blk * MAX_DIL, MAX_DIL)
            for srcs, dsts in (((qn[0], kn[0]), (qr[0], kr[0])), (vn, vr)):
                both = jnp.concatenate([src[pl.ds(r0, PERM), :] for src in srcs], axis=1)
                y = jnp.dot(perm, both, preferred_element_type=F32).astype(BF16)
                for r in range(MAX_DIL):
                    for half, dst in enumerate(dsts):
                        dst[r, pl.ds(j0, MAX_DIL), :] = y[r * MAX_DIL:(r + 1) * MAX_DIL, half * LANES:(half + 1) * LANES]
        return carry

    lax.fori_loop(0, seq // PERM // 2, deint, 0)

    def run_branch(nblocks, block_in_seq, fetch, mask_base, sink):
        def produce(t, pair, u):
            for hl in range(2):
                qb, kb = fetch(t, hl, True)
                s_scr[pair, u, hl] = lax.dot_general(qb * head_mask[hl], kb, (((1,), (1,)), ((), ())),
                                                     preferred_element_type=F32)

        def consume(t, pair, u):
            mk = mask_scr[mask_base + jnp.minimum(block_in_seq(t), 1)]
            pv = []
            for hl in range(2):
                p = jnp.exp2(s_scr[pair, u, hl]).astype(BF16) * mk
                pv.append(jnp.dot(p, fetch(t, hl, False), preferred_element_type=F32))
            sink(t, pv)

        for u in range(BRANCH_UNROLL):
            produce(u, 0, u)

        def body(i, carry):
            pair = i & 1
            for u in range(BRANCH_UNROLL):
                consume(BRANCH_UNROLL * i + u, pair, u)
            for u in range(BRANCH_UNROLL):
                produce(jnp.minimum(BRANCH_UNROLL * (i + 1) + u, nblocks - 1), 1 - pair, u)
            return carry

        lax.fori_loop(0, nblocks // BRANCH_UNROLL, body, 0)

    def prev_cur(ref_block, n, size):
        c0 = pl.multiple_of(n * size, size)
        p0 = pl.multiple_of(jnp.maximum(n - 1, 0) * size, size)
        return ref_block(p0) + ref_block(c0)

    nb16 = sub // WIN

    def fetch16(t, hl, qk):
        r, n = t // nb16, t % nb16
        if qk:
            return (qr[hl][r, pl.ds(pl.multiple_of(n * WIN, WIN), WIN), :],
                    jnp.concatenate(prev_cur(lambda s: [kr[hl][r, pl.ds(s, WIN), :]], n, WIN), axis=0))
        return jnp.concatenate(prev_cur(lambda s: [vr[hl][r, pl.ds(s, WIN), :]], n, WIN), axis=0)

    def sink16(t, pv):
        row = pl.multiple_of(t * WIN, WIN)
        for hl in range(2):
            fr[hl][pl.ds(row, WIN), :] = pv[hl]

    run_branch(MAX_DIL * nb16, lambda t: t % nb16, fetch16, 0, sink16)

    nb4 = (seq // 4) // WIN
    ch = WIN // 4

    def fetch4(t, hl, qk):
        r4, n = t // nb4, t % nb4
        gather = lambda ref: (lambda s: [ref[r4 + 4 * q, pl.ds(s, ch), :] for q in range(4)])
        if qk:
            return (jnp.concatenate(gather(qr[hl])(pl.multiple_of(n * ch, ch)), axis=0),
                    jnp.concatenate(prev_cur(gather(kr[hl]), n, ch), axis=0))
        return jnp.concatenate(prev_cur(gather(vr[hl]), n, ch), axis=0)

    def sink4(t, pv):
        r4, n = t // nb4, t % nb4
        for q in range(4):
            row = pl.multiple_of((r4 + 4 * q) * sub + n * ch, ch)
            for hl in range(2):
                fr[hl][pl.ds(row, ch), :] += pv[hl][q * ch:(q + 1) * ch, :]

    run_branch(4 * nb4, lambda t: t % nb4, fetch4, 2, sink4)

    def renat(i, carry):
        for u in range(2):
            blk = 2 * i + u
            j0 = pl.multiple_of(blk * MAX_DIL, MAX_DIL)
            r0 = pl.multiple_of(blk * PERM, PERM)
            for hl in range(2):
                a = jnp.concatenate([fr[hl][pl.ds(pl.multiple_of(r * sub + j0, MAX_DIL), MAX_DIL), :]
                                     for r in range(MAX_DIL)], axis=0)
                hi = a.astype(BF16)
                lo = (a - hi.astype(F32)).astype(BF16)
                y = jnp.dot(perm, jnp.concatenate([hi, lo], axis=1), preferred_element_type=F32)
                fn[hl][pl.ds(r0, PERM), :] = y[:, 0:LANES] + y[:, LANES:2 * LANES]
        return carry

    lax.fori_loop(0, seq // PERM // 2, renat, 0)

    def fetch1(t, hl, qk):
        if qk:
            return (qn[hl][pl.ds(pl.multiple_of(t * WIN, WIN), WIN), :],
                    jnp.concatenate(prev_cur(lambda s: [kn[hl][pl.ds(s, WIN), :]], t, WIN), axis=0))
        return jnp.concatenate(prev_cur(lambda s: [vn[hl][pl.ds(s, WIN), :]], t, WIN), axis=0)

    def sink1(t, pv):
        row = pl.multiple_of(t * WIN, WIN)
        t0 = fn[0][pl.ds(row, WIN), :] + pv[0]
        t1 = fn[1][pl.ds(row, WIN), :] + pv[1]
        o_ref[0, pl.ds(row, WIN), :] = jnp.where(own0, t0 / t0[:, HEAD_DIM:HEAD_DIM + 1],
                                                 t1 / t1[:, 0:1]).astype(o_ref.dtype)

    run_branch(seq // WIN, lambda t: t, fetch1, 0, sink1)


def _dilated_kernel(q_ref, k_ref, v_ref, o_ref, vn0, vn1, qr, kr, vr0, vr1,
                    f0, f1, f2, f3, f4, s_scr, mask_scr, *, seq, chunk):
    lane = lax.broadcasted_iota(jnp.int32, (chunk, LANES), 1)
    vn = (vn0, vn1)

    def stats(ci, sq_max):
        rows = pl.ds(pl.multiple_of(ci * chunk, chunk), chunk)
        qc = q_ref[0, rows, :].astype(F32)
        kc = k_ref[0, rows, :].astype(F32)
        vc = v_ref[0, rows, :].astype(F32)
        for hl in range(2):
            ones_lane = HEAD_DIM if hl == 0 else 0
            vn[hl][rows, :] = jnp.where(_own_lanes(lane, hl), vc, jnp.where(lane == ones_lane, 1.0, 0.0)).astype(BF16)
        return (jnp.maximum(sq_max[0], _max_sq_norm(qc, selector)),
                jnp.maximum(sq_max[1], _max_sq_norm(kc, selector)))

    selector = _head_sum_selector()
    sq_max = lax.fori_loop(0, seq // chunk, stats, (jnp.zeros((1, LANES), F32),) * 2)
    worst = jnp.max(_norm_bound(sq_max[0], sq_max[1]))

    @pl.when(worst <= LOGIT_MAX)
    def _fast():
        qn, kn = q_ref.at[0], k_ref.at[0]
        _dilated_fast(o_ref, (qn, qn), (kn, kn), vn, (qr, qr), (kr, kr), (vr0, vr1), (f0, f1), (f2, f3),
                      s_scr, mask_scr, seq=seq)

    @pl.when(worst > LOGIT_MAX)
    def _general():
        _dilated_general(q_ref, k_ref, v_ref, o_ref, qr, kr, vr0, f0, f1, f2, f3, f4, seq=seq)


def _dilated(qkva, *, chunk=512):
    b, seq, _ = qkva.shape
    sub = seq // MAX_DIL
    kern = functools.partial(_dilated_kernel, seq=seq, chunk=chunk)
    return pl.pallas_call(
        kern,
        grid=(b, HEAD_PAIRS),
        in_specs=[
            pl.BlockSpec((1, seq, LANES), lambda bi, hp: (bi, 0, hp)),
            pl.BlockSpec((1, seq, LANES), lambda bi, hp: (bi, 0, HEAD_PAIRS + hp)),
            pl.BlockSpec((1, seq, LANES), lambda bi, hp: (bi, 0, 2 * HEAD_PAIRS + hp)),
        ],
        out_specs=pl.BlockSpec((1, seq, LANES), lambda bi, hp: (bi, 0, hp)),
        out_shape=jax.ShapeDtypeStruct((b, seq, WIDTH_A), BF16),
        scratch_shapes=(
            [pltpu.VMEM((seq, LANES), BF16)] * 2
            + [pltpu.VMEM((MAX_DIL, sub, LANES), BF16)] * 4
            + [pltpu.VMEM((seq, LANES), F32)] * 5
            + [pltpu.VMEM((2, BRANCH_UNROLL, 2, WIN, 2 * WIN), F32), pltpu.VMEM((4, WIN, 2 * WIN), BF16)]
        ),
        compiler_params=pltpu.CompilerParams(
            dimension_semantics=("arbitrary", "arbitrary"), vmem_limit_bytes=VMEM_LIMIT),
        name="dilated",
    )(qkva, qkva, qkva)


def _outproj_kernel(oa_ref, ob_ref, x_ref, wo_ref, g_ref, wr_ref, x1_ref, h2_ref, cls_ref):
    tm = x_ref.shape[0]
    x1 = (x_ref[...]
          + jnp.dot(oa_ref[...], wo_ref[0:WIDTH_A, :], preferred_element_type=F32)
          + jnp.dot(ob_ref[...], wo_ref[WIDTH_A:WIDTH_A + WIDTH_B, :], preferred_element_type=F32))
    x1_ref[...] = x1
    ms = jnp.mean(x1 * x1, axis=-1, keepdims=True)
    h2 = x1 * lax.rsqrt(ms + NORM_EPS) * g_ref[...]
    h2_ref[:, 0:D_MODEL] = h2
    h2_hi = h2.astype(BF16)
    h2_lo = (h2 - h2_hi.astype(F32)).astype(BF16)
    both = jnp.dot(h2_hi, wr_ref[...], preferred_element_type=F32)
    logits = (both[:, 0:LANES] + both[:, LANES:2 * LANES]
              + jnp.dot(h2_lo, wr_ref[:, 0:LANES], preferred_element_type=F32))
    lane = lax.broadcasted_iota(jnp.int32, (tm, LANES), 1)
    lane_f = lane.astype(F32)
    big = float(LANES)
    gmask = lane < N_GROUPS
    gl = jnp.where(gmask, logits, NEG_INF)
    gmax = jnp.max(gl, axis=1, keepdims=True)
    gsum = jnp.sum(jnp.where(gmask, jnp.exp(gl - gmax), 0.0), axis=1, keepdims=True)
    p_top = 1.0 / gsum
    g_star = jnp.min(jnp.where(gmask & (gl == gmax), lane_f, big), axis=1, keepdims=True)
    lo_lane = ROUTER_OFF + EXPERTS_PER_GROUP * g_star
    emask = (lane_f >= lo_lane) & (lane_f < lo_lane + EXPERTS_PER_GROUP)
    sel = jnp.where(emask, logits, NEG_INF)
    v1 = jnp.max(sel, axis=1, keepdims=True)
    i1 = jnp.min(jnp.where(emask & (sel == v1), lane_f, big), axis=1, keepdims=True)
    rest = emask & (lane_f != i1)
    sel2 = jnp.where(rest, logits, NEG_INF)
    v2 = jnp.max(sel2, axis=1, keepdims=True)
    i2 = jnp.min(jnp.where(rest & (sel2 == v2), lane_f, big), axis=1, keepdims=True)
    e2 = jnp.exp(v2 - v1)
    w1 = p_top / (1.0 + e2)
    w2 = p_top * e2 / (1.0 + e2)
    e1 = i1 - lo_lane
    e2x = i2 - lo_lane
    a = jnp.minimum(e1, e2x)
    b = jnp.maximum(e1, e2x)
    cls = g_star * N_PAIRS + a * (7.0 - a) * 0.5 + (b - a - 1.0)
    wa = jnp.where(e1 < e2x, w1, w2)
    wb = jnp.where(e1 < e2x, w2, w1)
    h2_ref[:, D_MODEL:EXT_WIDTH] = jnp.where(lane == 0, wa, jnp.where(lane == 1, wb, 0.0))
    row = lax.broadcasted_iota(jnp.int32, (tm, LANES), 0)
    spread = jnp.where(lane == (row & (LANES - 1)), cls, 0.0)
    cls_ref[0] = jnp.sum(spread.reshape(tm // LANES, LANES, LANES), axis=1).astype(jnp.int32)


def _outproj(oa, ob, x2, wo, g, wr, *, tm):
    n = x2.shape[0]
    return pl.pallas_call(
        _outproj_kernel,
        grid=(n // tm,),
        in_specs=[
            pl.BlockSpec((tm, WIDTH_A), lambda i: (i, 0)),
            pl.BlockSpec((tm, WIDTH_B), lambda i: (i, 0)),
            pl.BlockSpec((tm, D_MODEL), lambda i: (i, 0)),
            pl.BlockSpec(wo.shape, lambda i: (0, 0)),
            pl.BlockSpec((1, D_MODEL), lambda i: (0, 0)),
            pl.BlockSpec(wr.shape, lambda i: (0, 0)),
        ],
        out_specs=[
            pl.BlockSpec((tm, D_MODEL), lambda i: (i, 0)),
            pl.BlockSpec((tm, EXT_WIDTH), lambda i: (i, 0)),
            pl.BlockSpec((1, tm // LANES, LANES), lambda i: (i, 0, 0)),
        ],
        out_shape=[
            jax.ShapeDtypeStruct((n, D_MODEL), F32),
            jax.ShapeDtypeStruct((n, EXT_WIDTH), F32),
            jax.ShapeDtypeStruct((n // tm, tm // LANES, LANES), jnp.int32),
        ],
        compiler_params=pltpu.CompilerParams(dimension_semantics=("arbitrary",), vmem_limit_bytes=VMEM_LIMIT),
        name="outproj",
    )(oa, ob, x2, wo, g, wr)


def _sort_kernel(cls_ref, pos_ref, tile_ref, ends_ref, *, tile):
    cls = cls_ref[...]
    rows = cls.shape[0]
    upper = (lax.broadcasted_iota(jnp.int32, (LANES, LANES), 0)
             <= lax.broadcasted_iota(jnp.int32, (LANES, LANES), 1)).astype(BF16)
    earlier_rows = (lax.broadcasted_iota(jnp.int32, (rows, rows), 0)
                    > lax.broadcasted_iota(jnp.int32, (rows, rows), 1)).astype(BF16)
    ones = jnp.ones((LANES, LANES), BF16)
    lane8 = lax.broadcasted_iota(jnp.int32, tile_ref.shape, 1)
    tile_start = lane8.astype(F32) * tile
    off = jnp.zeros((1, 1), F32)
    pos = jnp.zeros(cls.shape, F32)
    tile_cls = jnp.zeros(tile_ref.shape, F32)
    ends = jnp.zeros(tile_ref.shape, F32)
    for c in range(N_CLASSES):
        hit = cls == c
        hot = jnp.where(hit, 1.0, 0.0).astype(BF16)
        in_row = jnp.dot(hot, upper, preferred_element_type=F32)
        row_tot = jnp.dot(hot, ones, preferred_element_type=F32)
        before = jnp.dot(earlier_rows, row_tot.astype(BF16), preferred_element_type=F32)
        count = before[rows - 1:rows, 0:1] + row_tot[rows - 1:rows, 0:1]
        pos = pos + jnp.where(hit, in_row - 1.0 + before + off, 0.0)
        off = off + jnp.ceil(count / tile) * tile
        tile_cls = tile_cls + jnp.where(tile_start >= off, 1.0, 0.0)
        ends = jnp.where(lane8 == c, off, ends)
    pos_ref[...] = pos.astype(jnp.int32)
    tile_ref[...] = tile_cls.astype(jnp.int32)
    ends_ref[...] = ends.astype(jnp.int32)


def _sort(cls2d, *, tile):
    meta = jax.ShapeDtypeStruct((8, LANES), jnp.int32)
    return pl.pallas_call(
        functools.partial(_sort_kernel, tile=tile),
        out_shape=[jax.ShapeDtypeStruct(cls2d.shape, jnp.int32), meta, meta],
        compiler_params=pltpu.CompilerParams(vmem_limit_bytes=VMEM_LIMIT),
        name="moe_sort",
    )(cls2d)


def _row_copy(src, src_row, dst, dst_row, sem, rows=1):
    return pltpu.make_async_copy(src.at[pl.ds(src_row, rows)], dst.at[pl.ds(dst_row, rows)], sem)


def _dispatch_kernel(pos_ref, ends_ref, h2_ref, wg_ref, wu_ref, wd_ref, xs_hbm, wg_out, wu_out, wd_out,
                     zero_scr, sem, *, ch, tile):
    g = pl.program_id(0)
    for src, dst in ((wg_ref, wg_out), (wu_ref, wu_out), (wd_ref, wd_out)):
        dst[...] = src[...].astype(dst.dtype)

    @pl.when(g == 0)
    def _zero_tails():
        zero_scr[...] = jnp.zeros_like(zero_scr)
        n_tiles = xs_hbm.shape[0] // tile
        min_used = n_tiles - N_CLASSES
        total = ends_ref[0, N_CLASSES - 1]
        for phase in ("start", "wait"):
            def zero_tile(row0):
                cp = pltpu.make_async_copy(zero_scr, xs_hbm.at[pl.ds(pl.multiple_of(row0, tile), tile)], sem)
                cp.start() if phase == "start" else cp.wait()

            for c in range(N_CLASSES):
                end = ends_ref[0, c]
                prev = ends_ref[0, c - 1] if c else 0
                pl.when(end > prev)(functools.partial(zero_tile, end - tile))
                pl.when((min_used + c) * tile >= total)(functools.partial(zero_tile, (min_used + c) * tile))

    for k in range(ch):
        _row_copy(h2_ref, k, xs_hbm, pos_ref[0, 0, k], sem).start(priority=k % DMA_THREADS)
    _row_copy(h2_ref, 0, xs_hbm, 0, sem, rows=ch).wait()


def _dispatch(pos3, ends, h2ext, wg, wu, wd, *, n_rows, tile):
    n_chunks, _, ch = pos3.shape
    parts = n_chunks // N_EXPERTS
    assert parts * N_EXPERTS == n_chunks and D_MODEL % parts == 0 and D_EXPERT % parts == 0
    slice_spec = lambda rows, cols: pl.BlockSpec((1, rows // parts, cols), lambda g: (g // parts, g % parts, 0))
    w_specs = [slice_spec(D_MODEL, D_EXPERT), slice_spec(D_MODEL, D_EXPERT), slice_spec(D_EXPERT, D_MODEL)]
    return pl.pallas_call(
        functools.partial(_dispatch_kernel, ch=ch, tile=tile),
        grid=(n_chunks,),
        in_specs=[
            pl.BlockSpec((1, 1, ch), lambda g: (g, 0, 0), memory_space=pltpu.SMEM),
            pl.BlockSpec(ends.shape, lambda g: (0, 0), memory_space=pltpu.SMEM),
            pl.BlockSpec((ch, EXT_WIDTH), lambda g: (g, 0)),
        ] + w_specs,
        out_specs=[pl.BlockSpec(memory_space=pl.ANY)] + w_specs,
        out_shape=[jax.ShapeDtypeStruct((n_rows, EXT_WIDTH), F32)]
        + [jax.ShapeDtypeStruct(w.shape, BF16) for w in (wg, wu, wd)],
        scratch_shapes=[pltpu.VMEM((tile, EXT_WIDTH), F32), pltpu.SemaphoreType.DMA(())],
        compiler_params=pltpu.CompilerParams(dimension_semantics=("arbitrary",), vmem_limit_bytes=VMEM_LIMIT),
        name="moe_dispatch",
    )(pos3, ends, h2ext, wg, wu, wd)


def _experts_kernel(ea_ref, eb_ref, used_ref, xs_ref, wga, wua, wda, wgb, wub, wdb, ys_ref):
    del ea_ref, eb_ref

    @pl.when(pl.program_id(0) < used_ref[0])
    def _():
        x = xs_ref[:, 0:D_MODEL].astype(BF16)
        wts = xs_ref[:, D_MODEL:EXT_WIDTH]
        y = None
        for col, (wg, wu, wd) in enumerate(((wga, wua, wda), (wgb, wub, wdb))):
            gate = jnp.dot(x, wg[0], preferred_element_type=F32)
            up = jnp.dot(x, wu[0], preferred_element_type=F32)
            he = (gate / (1.0 + jnp.exp(-gate)) * up * wts[:, col:col + 1]).astype(BF16)
            t = jnp.dot(he, wd[0], preferred_element_type=F32)
            y = t if y is None else y + t
        ys_ref[...] = y

    @pl.when(pl.program_id(0) >= used_ref[0])
    def _():
        ys_ref[...] = jnp.zeros_like(ys_ref)


def _experts(ea, eb, used, xs, wg, wu, wd, *, tile):
    n_tiles = xs.shape[0] // tile
    rows = lambda j, ea, eb, used: (jnp.minimum(j, used[0] - 1), 0)
    first = lambda j, ea, eb, used: (ea[j], 0, 0)
    second = lambda j, ea, eb, used: (eb[j], 0, 0)
    up_spec = lambda m: pl.BlockSpec((1, D_MODEL, D_EXPERT), m)
    down_spec = lambda m: pl.BlockSpec((1, D_EXPERT, D_MODEL), m)
    return pl.pallas_call(
        _experts_kernel,
        grid_spec=pltpu.PrefetchScalarGridSpec(
            num_scalar_prefetch=3,
            grid=(n_tiles,),
            in_specs=[pl.BlockSpec((tile, EXT_WIDTH), rows),
                      up_spec(first), up_spec(first), down_spec(first),
                      up_spec(second), up_spec(second), down_spec(second)],
            out_specs=pl.BlockSpec((tile, D_MODEL), lambda j, ea, eb, used: (j, 0)),
        ),
        out_shape=jax.ShapeDtypeStruct((xs.shape[0], D_MODEL), F32),
        compiler_params=pltpu.CompilerParams(dimension_semantics=("arbitrary",), vmem_limit_bytes=VMEM_LIMIT),
        name="moe_experts",
    )(ea, eb, used, xs, wg, wu, wd, wg, wu, wd)


def _combine_kernel(pos_ref, next_pos_ref, x1_ref, g_ref, ys_hbm, out_ref, y_scr, sem):
    i = pl.program_id(0)
    tm = x1_ref.shape[0]
    slot = i & 1

    def gather(p_ref, s):
        for k in range(tm):
            _row_copy(ys_hbm, p_ref[0, 0, k], y_scr.at[s], k, sem.at[s]).start(priority=k % DMA_THREADS)

    @pl.when(i == 0)
    def _():
        gather(pos_ref, 0)

    @pl.when(i + 1 < pl.num_programs(0))
    def _():
        gather(next_pos_ref, 1 - slot)

    _row_copy(ys_hbm, 0, y_scr.at[slot], 0, sem.at[slot], rows=tm).wait()
    x2 = x1_ref[...] + y_scr[slot]
    ms = jnp.mean(x2 * x2, axis=-1, keepdims=True)
    out_ref[...] = x2 * lax.rsqrt(ms + NORM_EPS) * g_ref[...]


def _combine(pos3, x1, g, ys):
    n_chunks, _, tm = pos3.shape
    return pl.pallas_call(
        _combine_kernel,
        grid=(n_chunks,),
        in_specs=[
            pl.BlockSpec((1, 1, tm), lambda i: (i, 0, 0), memory_space=pltpu.SMEM),
            pl.BlockSpec((1, 1, tm), lambda i: (jnp.minimum(i + 1, n_chunks - 1), 0, 0), memory_space=pltpu.SMEM),
            pl.BlockSpec((tm, D_MODEL), lambda i: (i, 0)),
            pl.BlockSpec((1, D_MODEL), lambda i: (0, 0)),
            pl.BlockSpec(memory_space=pl.ANY),
        ],
        out_specs=pl.BlockSpec((tm, D_MODEL), lambda i: (i, 0)),
        out_shape=jax.ShapeDtypeStruct(x1.shape, F32),
        scratch_shapes=[pltpu.VMEM((2, tm, D_MODEL), F32), pltpu.SemaphoreType.DMA((2,))],
        compiler_params=pltpu.CompilerParams(dimension_semantics=("arbitrary",), vmem_limit_bytes=VMEM_LIMIT),
        name="moe_combine",
    )(pos3, pos3, x1, g, ys)


def _moe(h2ext, cls3, x1, wg, wu, wd, g, *, tile=MOE_TILE, dispatch_chunk=512, combine_chunk=512):
    n = x1.shape[0]
    n_tiles = n // tile + N_CLASSES
    assert n_tiles <= LANES
    pos, tile_cls, ends = _sort(cls3.reshape(n // LANES, LANES), tile=tile)
    xs, wg, wu, wd = _dispatch(pos.reshape(n // dispatch_chunk, 1, dispatch_chunk), ends, h2ext, wg, wu, wd,
                               n_rows=n_tiles * tile, tile=tile)
    used = ends[0, N_CLASSES - 1:N_CLASSES] // tile
    tc = tile_cls[0, :n_tiles]
    tc = jnp.where(jnp.arange(n_tiles) < used[0], tc, tc[jnp.maximum(used[0] - 1, 0)])
    pair_a = jnp.array([0, 0, 0, 1, 1, 2], jnp.int32)
    pair_b = jnp.array([1, 2, 3, 2, 3, 3], jnp.int32)
    ea = (tc // N_PAIRS) * EXPERTS_PER_GROUP + pair_a[tc % N_PAIRS]
    eb = (tc // N_PAIRS) * EXPERTS_PER_GROUP + pair_b[tc % N_PAIRS]
    ys = _experts(ea, eb, used, xs, wg, wu, wd, tile=tile)
    return _combine(pos.reshape(n // combine_chunk, 1, combine_chunk), x1, g, ys)


def _rotary_tables(seq):
    pos = np.arange(seq, dtype=np.float64)
    inv_freq = 1.0 / (ROPE_THETA ** (np.arange(0, ROT_DIM, 2, dtype=np.float64) / ROT_DIM))
    ang = pos[:, None] * inv_freq[None, :]
    cos, sin = np.cos(ang), np.sin(ang)
    zeros = np.zeros((seq, HEAD_DIM - ROT_DIM))
    zh = np.zeros((seq, ROT_HALF))
    cos_h = np.concatenate([cos, cos, np.ones((seq, HEAD_DIM - ROT_DIM))], axis=1)
    sa_h = np.concatenate([-sin, zh, zeros], axis=1)
    sb_h = np.concatenate([zh, sin, zeros], axis=1)
    tile = lambda t: jnp.asarray(np.concatenate([t, t], axis=1), F32)
    return tile(cos_h), tile(sa_h), tile(sb_h)


def kernel(x, attn_norm, w_in, b_forget, w_out, ffn_norm, w_group, w_expert, w_gate_e, w_up_e, w_down_e, final_norm):
    b, seq, d = x.shape
    assert d == D_MODEL and w_in.shape[0] == 1, "single-layer block"
    n = b * seq
    scale = HEAD_DIM ** -0.5 * LOG2E
    col_scale = jnp.ones((2 * QKV_WIDTH + HEADS_B,), F32)
    col_scale = col_scale.at[0:WIDTH_A].set(scale).at[QKV_WIDTH:QKV_WIDTH + WIDTH_B].set(scale)
    w = jnp.pad(w_in[0] * col_scale[None, :], ((0, 0), (0, LANES - HEADS_B))).astype(BF16)
    bf = jnp.pad(b_forget[0].astype(F32), (0, LANES - HEADS_B))[None, :]
    cosv, sa, sb = _rotary_tables(seq)
    x2 = x.reshape(n, d)

    qkva, qkvb, c = _inproj(x2, attn_norm[0][None, :], w, cosv, sa, sb, bf, seq=seq, tm=512)
    out_a = _dilated(qkva.reshape(b, seq, QKV_WIDTH))
    out_b = _fox(qkvb.reshape(b, seq, QKV_WIDTH), c.reshape(b, seq, LANES), tq=1024)

    wr = jnp.pad(jnp.concatenate([w_group[0], w_expert[0]], axis=1).astype(F32),
                 ((0, 0), (0, LANES - N_GROUPS - N_EXPERTS)))
    wr_hi = wr.astype(BF16)
    wr = jnp.concatenate([wr_hi, (wr - wr_hi.astype(F32)).astype(BF16)], axis=1)
    x1, h2ext, cls3 = _outproj(out_a.reshape(n, WIDTH_A), out_b.reshape(n, WIDTH_B), x2,
                               w_out[0].astype(BF16), ffn_norm[0][None, :], wr, tm=512)
    out = _moe(h2ext, cls3, x1, w_gate_e[0], w_up_e[0], w_down_e[0],
               final_norm[None, :])
    return out.reshape(b, seq, d)
```

```python
import functools

import jax
import jax.numpy as jnp
import numpy as np
from jax import lax
from jax.experimental import pallas as pl
from jax.experimental.pallas import tpu as pltpu

F32 = jnp.float32
BF16 = jnp.bfloat16

D_MODEL = 1024
HEAD_DIM = 64
HEADS_A = 8
HEADS_B = 8
WIDTH_A = HEADS_A * HEAD_DIM
WIDTH_B = HEADS_B * HEAD_DIM
QKV_WIDTH = 3 * WIDTH_A
DILATIONS = ((128, 1), (512, 4), (2048, 16))
ROT_DIM = HEAD_DIM // 4
ROT_HALF = ROT_DIM // 2
ROPE_THETA = 500000.0
N_GROUPS = 4
EXPERTS_PER_GROUP = 4
N_EXPERTS = N_GROUPS * EXPERTS_PER_GROUP
D_EXPERT = 512
NORM_EPS = 1e-6
NEG_INF = -1e30

LANES = 128
HEAD_PAIRS = WIDTH_A // LANES
WIN = 128
MAX_DIL = 16
PERM = MAX_DIL * MAX_DIL
KV_BLOCK = 256
N_PAIRS = EXPERTS_PER_GROUP * (EXPERTS_PER_GROUP - 1) // 2
N_CLASSES = N_GROUPS * N_PAIRS
MOE_TILE = 256
EXT_WIDTH = D_MODEL + LANES
LOGIT_MAX = 60.0
BRANCH_UNROLL = 8
DMA_THREADS = 2
ROUTER_OFF = N_GROUPS
VMEM_LIMIT = 56 * 1024 * 1024
LOG2E = 1.4426950408889634
BOUND_SLACK = 1.0 + 2.0 ** -6
BOUND_EPS = 2.0 ** -7


def _split3(x):
    hi = x.astype(BF16)
    r1 = x - hi.astype(F32)
    mid = r1.astype(BF16)
    lo = (r1 - mid.astype(F32)).astype(BF16)
    return hi, mid, lo


def _inproj_kernel(x_ref, g_ref, w_ref, cos_ref, sa_ref, sb_ref, bf_ref,
                   qkva_ref, qkvb_ref, c_ref, h_scr, carry_scr, *, tiles_per_seq):
    i = pl.program_id(0)
    tm = x_ref.shape[0]
    x = x_ref[...]
    ms = jnp.mean(x * x, axis=-1, keepdims=True)
    h_scr[...] = (x * lax.rsqrt(ms + NORM_EPS) * g_ref[...]).astype(BF16)
    cosv = cos_ref[...]
    sa = sa_ref[...]
    sb = sb_ref[...]
    for g in range(6):
        p = jnp.dot(h_scr[...], w_ref[:, g * WIDTH_A:(g + 1) * WIDTH_A], preferred_element_type=F32)
        dst = qkva_ref if g < 3 else qkvb_ref
        col0 = (g % 3) * WIDTH_A
        if g < 2:
            for k in range(HEAD_PAIRS):
                t = p[:, k * LANES:(k + 1) * LANES]
                t = (t * cosv + pltpu.roll(t, LANES - ROT_HALF, 1) * sa + pltpu.roll(t, ROT_HALF, 1) * sb)
                dst[:, col0 + k * LANES:col0 + (k + 1) * LANES] = t.astype(BF16)
        else:
            dst[:, col0:col0 + WIDTH_A] = p.astype(BF16)
    z = jnp.dot(h_scr[...], w_ref[:, 2 * QKV_WIDTH:2 * QKV_WIDTH + LANES], preferred_element_type=F32) + bf_ref[...]
    lf = jnp.minimum(z, 0.0) - jnp.log1p(jnp.exp(-jnp.abs(z)))
    lane = lax.broadcasted_iota(jnp.int32, (tm, LANES), 1)
    lf = jnp.where(lane < HEADS_B, lf * LOG2E, 0.0)
    hi, mid, lo = _split3(lf)
    tri = (lax.broadcasted_iota(jnp.int32, (tm, tm), 0) >= lax.broadcasted_iota(jnp.int32, (tm, tm), 1)).astype(BF16)
    cs = (jnp.dot(tri, hi, preferred_element_type=F32) + jnp.dot(tri, mid, preferred_element_type=F32)
          + jnp.dot(tri, lo, preferred_element_type=F32))

    @pl.when(i % tiles_per_seq == 0)
    def _():
        carry_scr[...] = jnp.zeros_like(carry_scr)

    c = cs + carry_scr[0:1, :]
    c_ref[...] = c
    carry_scr[...] = jnp.broadcast_to(c[tm - 1:tm, :], carry_scr.shape)


def _inproj(x2, g, w, cosv, sa, sb, bf, *, seq, tm):
    n = x2.shape[0]
    kern = functools.partial(_inproj_kernel, tiles_per_seq=seq // tm)
    tps = seq // tm
    return pl.pallas_call(
        kern,
        grid=(n // tm,),
        in_specs=[
            pl.BlockSpec((tm, D_MODEL), lambda i: (i, 0)),
            pl.BlockSpec((1, D_MODEL), lambda i: (0, 0)),
            pl.BlockSpec(w.shape, lambda i: (0, 0)),
            pl.BlockSpec((tm, LANES), lambda i: (i % tps, 0)),
            pl.BlockSpec((tm, LANES), lambda i: (i % tps, 0)),
            pl.BlockSpec((tm, LANES), lambda i: (i % tps, 0)),
            pl.BlockSpec((1, LANES), lambda i: (0, 0)),
        ],
        out_specs=[
            pl.BlockSpec((tm, QKV_WIDTH), lambda i: (i, 0)),
            pl.BlockSpec((tm, QKV_WIDTH), lambda i: (i, 0)),
            pl.BlockSpec((tm, LANES), lambda i: (i, 0)),
        ],
        out_shape=[
            jax.ShapeDtypeStruct((n, QKV_WIDTH), BF16),
            jax.ShapeDtypeStruct((n, QKV_WIDTH), BF16),
            jax.ShapeDtypeStruct((n, LANES), F32),
        ],
        scratch_shapes=[pltpu.VMEM((tm, D_MODEL), BF16), pltpu.VMEM((8, LANES), F32)],
        compiler_params=pltpu.CompilerParams(dimension_semantics=("arbitrary",), vmem_limit_bytes=VMEM_LIMIT),
        name="inproj",
    )(x2, g, w, cosv, sa, sb, bf)


def _head_column(c_all, lane, head):
    return jnp.sum(jnp.where(lane == head, c_all, 0.0), axis=1, keepdims=True)


def _own_lanes(lane, hl):
    return (lane < HEAD_DIM) if hl == 0 else (lane >= HEAD_DIM)


def _own_sum(vals, lane, hl):
    return jnp.sum(jnp.where(_own_lanes(lane, hl), vals, 0.0), axis=1, keepdims=True)


def _head_sum_selector():
    lane = lax.broadcasted_iota(jnp.int32, (LANES, LANES), 0)
    col = lax.broadcasted_iota(jnp.int32, (LANES, LANES), 1)
    return jnp.where(((col == 0) & (lane < HEAD_DIM)) | ((col == 1) & (lane >= HEAD_DIM)), 1.0, 0.0).astype(BF16)


def _max_sq_norm(t, selector):
    return jnp.max(jnp.dot((t * t).astype(BF16), selector, preferred_element_type=F32), axis=0, keepdims=True)


def _norm_bound(qsq, ksq_max):
    return jnp.sqrt(qsq * ksq_max) * BOUND_SLACK + BOUND_EPS


def _fox_kernel(q_ref, k_ref, v_ref, c_ref, wg_ref, wu_ref, wd_ref, o_ref, wg_out, wu_out, wd_out,
                q0_scr, q1_scr, k0_scr, k1_scr, v0_scr, v1_scr, bound_scr, acc0_scr, acc1_scr, s_scr,
                *, tq, seq, chunk):
    hp = pl.program_id(1)
    qi = pl.program_id(2)
    for src, dst in ((wg_ref, wg_out), (wu_ref, wu_out), (wd_ref, wd_out)):
        dst[...] = src[...].astype(dst.dtype)
    q_scr = (q0_scr, q1_scr)
    k_scr = (k0_scr, k1_scr)
    v_scr = (v0_scr, v1_scr)
    acc_scr = (acc0_scr, acc1_scr)

    def augmented(vals, cterm, lane, hl, key_side):
        hi, mid, lo = (t.astype(F32) for t in _split3(cterm))
        base = HEAD_DIM if hl == 0 else 0
        ones = (lane >= base + (0 if key_side else 3)) & (lane < base + (3 if key_side else 6))
        o = 3 if key_side else 0
        sgn = -1.0 if key_side else 1.0
        aug = jnp.where(lane == base + o, sgn * hi,
                        jnp.where(lane == base + o + 1, sgn * mid,
                                  jnp.where(lane == base + o + 2, sgn * lo,
                                            jnp.where(ones, 1.0, 0.0))))
        return jnp.where(_own_lanes(lane, hl), vals, aug).astype(BF16)

    @pl.when(qi == 0)
    def _build():
        lane = lax.broadcasted_iota(jnp.int32, (chunk, LANES), 1)

        def body(ci, sq_max):
            rows = pl.ds(pl.multiple_of(ci * chunk, chunk), chunk)
            qc = q_ref[0, rows, :].astype(F32)
            kc = k_ref[0, rows, :].astype(F32)
            vc = v_ref[0, rows, :].astype(F32)
            cc = c_ref[0, rows, :]
            for hl in range(2):
                ccol = _head_column(cc, lane, 2 * hp + hl)
                q_scr[hl][rows, :] = augmented(qc, ccol, lane, hl, False)
                k_scr[hl][rows, :] = augmented(kc, ccol, lane, hl, True)
                one_lane = HEAD_DIM if hl == 0 else 0
                v_scr[hl][rows, :] = jnp.where(
                    _own_lanes(lane, hl), vc, jnp.where(lane == one_lane, 1.0, 0.0)).astype(BF16)
            return (jnp.maximum(sq_max[0], _max_sq_norm(qc, selector)),
                    jnp.maximum(sq_max[1], _max_sq_norm(kc, selector)))

        selector = _head_sum_selector()
        sq_max = lax.fori_loop(0, seq // chunk, body, (jnp.zeros((1, LANES), F32),) * 2)
        bound_scr[...] = jnp.broadcast_to(_norm_bound(sq_max[0], sq_max[1]), bound_scr.shape)

    lane_q = lax.broadcasted_iota(jnp.int32, (tq, LANES), 1)
    q_rows = pl.ds(pl.multiple_of(qi * tq, tq), tq)
    worst = jnp.max(bound_scr[...])
    n_diag = tq // KV_BLOCK
    n_full = qi * n_diag
    row_minus_col = (lax.broadcasted_iota(jnp.int32, (tq, KV_BLOCK), 0)
                     - lax.broadcasted_iota(jnp.int32, (tq, KV_BLOCK), 1))

    def causal_mask(s, d, top=0):
        return jnp.where(row_minus_col[top:tq, :] >= d * KV_BLOCK, s, NEG_INF)

    def logits(hl, j, top=0):
        r0 = pl.multiple_of(j * KV_BLOCK, KV_BLOCK)
        rows = q_rows if top == 0 else pl.ds(pl.multiple_of(qi * tq + top, KV_BLOCK), tq - top)
        return lax.dot_general(q_scr[hl][rows, :], k_scr[hl][pl.ds(r0, KV_BLOCK), :], (((1,), (1,)), ((), ())),
                               preferred_element_type=F32)

    def values(hl, j):
        return v_scr[hl][pl.ds(pl.multiple_of(j * KV_BLOCK, KV_BLOCK), KV_BLOCK), :]

    def finish(a0, a1):
        l0 = a0[:, HEAD_DIM:HEAD_DIM + 1]
        l1 = a1[:, 0:1]
        o_ref[0] = jnp.where(lane_q < HEAD_DIM, a0 / l0, a1 / l1).astype(o_ref.dtype)

    @pl.when(worst <= LOGIT_MAX)
    def _fast():
        acc0_scr[...] = jnp.zeros_like(acc0_scr)
        acc1_scr[...] = jnp.zeros_like(acc1_scr)

        def produce(j, slot, top=0):
            for hl in range(2):
                s_scr[slot, hl, top:tq, :] = logits(hl, j, top)

        def consume(j, slot, diag):
            top = 0 if diag is None else diag * KV_BLOCK
            for hl in range(2):
                s = s_scr[slot, hl, top:tq, :]
                if diag is not None:
                    s = causal_mask(s, diag, top)
                acc_scr[hl][top:tq, :] += jnp.dot(jnp.exp2(s).astype(BF16), values(hl, j),
                                                  preferred_element_type=F32)

        def produce_diagonal(d):
            produce(n_full + d, d, d * KV_BLOCK)

        @pl.when(qi == 0)
        def _():
            for d in range(n_diag):
                produce_diagonal(d)

        @pl.when(qi > 0)
        def _():
            for d in range(n_diag):
                produce(d, d)

            def body(i, carry):
                for d in range(n_diag):
                    j = i * n_diag + d
                    consume(j, d, None)
                    produce(j + n_diag, d)
                return carry

            lax.fori_loop(0, qi - 1, body, 0)
            for d in range(n_diag):
                consume(n_full - n_diag + d, d, None)
                produce_diagonal(d)

        for d in range(n_diag):
            consume(n_full + d, d, d)
        finish(acc0_scr[...], acc1_scr[...])

    @pl.when(worst > LOGIT_MAX)
    def _general():
        def step(j, carry, diag):
            new = []
            for hl in range(2):
                m, a = carry[2 * hl:2 * hl + 2]
                s = logits(hl, j)
                if diag is not None:
                    s = causal_mask(s, diag)
                mn = jnp.maximum(m, jnp.max(s, axis=1, keepdims=True))
                a = jnp.exp2(m - mn) * a + jnp.dot(jnp.exp2(s - mn).astype(BF16), values(hl, j),
                                                  preferred_element_type=F32)
                new += [mn, a]
            return tuple(new)

        carry = (jnp.full((tq, 1), NEG_INF, F32), jnp.zeros((tq, LANES), F32)) * 2
        carry = lax.fori_loop(0, n_full, lambda j, c: step(j, c, None), carry)
        for d in range(n_diag):
            carry = step(n_full + d, carry, d)
        finish(carry[1], carry[3])


def _fox(qkvb, c, wg, wu, wd, *, tq, chunk=512):
    b, seq, _ = qkvb.shape
    nq = seq // tq
    steps = b * HEAD_PAIRS * nq
    parts = steps // N_EXPERTS
    assert parts * N_EXPERTS == steps and D_MODEL % parts == 0 and D_EXPERT % parts == 0

    def slice_spec(rows, cols):
        def index(bi, hp, qi):
            step = (bi * HEAD_PAIRS + hp) * nq + qi
            return (step // parts, step % parts, 0)
        return pl.BlockSpec((1, rows // parts, cols), index)

    w_specs = [slice_spec(D_MODEL, D_EXPERT), slice_spec(D_MODEL, D_EXPERT), slice_spec(D_EXPERT, D_MODEL)]
    kern = functools.partial(_fox_kernel, tq=tq, seq=seq, chunk=chunk)
    return pl.pallas_call(
        kern,
        grid=(b, HEAD_PAIRS, nq),
        in_specs=[
            pl.BlockSpec((1, seq, LANES), lambda bi, hp, qi: (bi, 0, hp)),
            pl.BlockSpec((1, seq, LANES), lambda bi, hp, qi: (bi, 0, HEAD_PAIRS + hp)),
            pl.BlockSpec((1, seq, LANES), lambda bi, hp, qi: (bi, 0, 2 * HEAD_PAIRS + hp)),
            pl.BlockSpec((1, seq, LANES), lambda bi, hp, qi: (bi, 0, 0)),
        ] + w_specs,
        out_specs=[pl.BlockSpec((1, tq, LANES), lambda bi, hp, qi: (bi, qi, hp))] + w_specs,
        out_shape=[jax.ShapeDtypeStruct((b, seq, WIDTH_B), BF16)]
        + [jax.ShapeDtypeStruct(w.shape, BF16) for w in (wg, wu, wd)],
        scratch_shapes=[pltpu.VMEM((seq, LANES), BF16)] * 6 + [
                        pltpu.VMEM((8, LANES), F32),
                        pltpu.VMEM((tq, LANES), F32), pltpu.VMEM((tq, LANES), F32),
                        pltpu.VMEM((tq // KV_BLOCK, 2, tq, KV_BLOCK), F32)],
        compiler_params=pltpu.CompilerParams(
            dimension_semantics=("arbitrary", "arbitrary", "arbitrary"), vmem_limit_bytes=VMEM_LIMIT),
        name="fox",
    )(qkvb, qkvb, qkvb, c, wg, wu, wd)


def _dilated_general(q_ref, k_ref, v_ref, o_ref, q16, k16, v16, m_s, l_s, acc_s, on_s, lse_s, *, seq):
    sub = seq // MAX_DIL
    own0 = lax.broadcasted_iota(jnp.int32, (WIN, LANES), 1) < HEAD_DIM
    ri = lax.broadcasted_iota(jnp.int32, (PERM, PERM), 0)
    ci = lax.broadcasted_iota(jnp.int32, (PERM, PERM), 1)
    perm = (((ri >> 4) == (ci & 15)) & ((ri & 15) == (ci >> 4))).astype(BF16)

    qa = lax.broadcasted_iota(jnp.int32, (WIN, 2 * WIN), 0)
    kb_i = lax.broadcasted_iota(jnp.int32, (WIN, 2 * WIN), 1)
    cur = kb_i >= WIN
    kin = kb_i & (WIN - 1)
    dist_nat = qa - kin + jnp.where(cur, 0, WIN)
    band_nat = (dist_nat >= 0) & (dist_nat <= WIN)
    sq = ((qa & 31) << 2) + (qa >> 5)
    sk = ((kin & 31) << 2) + (kin >> 5)
    dist_4 = sq - sk + jnp.where(cur, 0, WIN)
    band_4 = (dist_4 >= 0) & (dist_4 <= WIN)

    def attend(qb, kb, vb, mask, state):
        qf = qb.astype(F32)
        new = []
        for hl in range(2):
            m, l, a = state[3 * hl:3 * hl + 3]
            own = own0 if hl == 0 else jnp.logical_not(own0)
            qh = jnp.where(own, qf, 0.0).astype(BF16)
            s = lax.dot_general(qh, kb, (((1,), (1,)), ((), ())), preferred_element_type=F32)
            s = jnp.where(mask, s, NEG_INF)
            mn = jnp.maximum(m, jnp.max(s, axis=1, keepdims=True))
            alpha = jnp.exp2(m - mn)
            p = jnp.exp2(s - mn)
            l = alpha * l + jnp.sum(p, axis=1, keepdims=True)
            a = alpha * a + jnp.dot(p.astype(BF16), vb, preferred_element_type=F32)
            new += [mn, l, a]
        return new

    def pack_state(st):
        m0, l0, a0, m1, l1, a1 = st
        return (jnp.where(own0, m0, m1), jnp.where(own0, l0, l1), jnp.where(own0, a0, a1))

    def deint(blk, carry):
        r0 = pl.multiple_of(blk * PERM, PERM)
        j0 = pl.multiple_of(blk * MAX_DIL, MAX_DIL)
        for src, dst in ((q_ref, q16), (k_ref, k16), (v_ref, v16)):
            y = jnp.dot(perm, src[0, pl.ds(r0, PERM), :], preferred_element_type=F32).astype(BF16)
            for r in range(MAX_DIL):
                dst[r, pl.ds(j0, MAX_DIL), :] = y[r * MAX_DIL:(r + 1) * MAX_DIL, :]
        return carry

    lax.fori_loop(0, seq // PERM, deint, 0)

    fresh = [jnp.full((WIN, 1), NEG_INF, F32), jnp.zeros((WIN, 1), F32), jnp.zeros((WIN, LANES), F32)] * 2
    nb16 = sub // WIN

    def d16_body(t, carry):
        r = t // nb16
        n = t % nb16
        c0 = pl.multiple_of(n * WIN, WIN)
        p0 = pl.multiple_of(jnp.maximum(n - 1, 0) * WIN, WIN)
        qb = q16[r, pl.ds(c0, WIN), :]
        kb = jnp.concatenate([k16[r, pl.ds(p0, WIN), :], k16[r, pl.ds(c0, WIN), :]], axis=0)
        vb = jnp.concatenate([v16[r, pl.ds(p0, WIN), :], v16[r, pl.ds(c0, WIN), :]], axis=0)
        mask = band_nat & (cur | (n > 0))
        mm, ll, aa = pack_state(attend(qb, kb, vb, mask, fresh))
        row = pl.multiple_of(r * sub + c0, WIN)
        m_s[pl.ds(row, WIN), :] = mm
        l_s[pl.ds(row, WIN), :] = ll
        acc_s[pl.ds(row, WIN), :] = aa
        return carry

    lax.fori_loop(0, MAX_DIL * nb16, d16_body, 0)

    nb4 = (seq // 4) // WIN
    ch = WIN // 4

    def d4_body(t, carry):
        r4 = t // nb4
        n = t % nb4
        c0 = pl.multiple_of(n * ch, ch)
        p0 = pl.multiple_of(jnp.maximum(n - 1, 0) * ch, ch)

        def gather(ref, j0):
            return [ref[r4 + 4 * q, pl.ds(j0, ch), :] for q in range(4)]

        qb = jnp.concatenate(gather(q16, c0), axis=0)
        kb = jnp.concatenate(gather(k16, p0) + gather(k16, c0), axis=0)
        vb = jnp.concatenate(gather(v16, p0) + gather(v16, c0), axis=0)
        rows = [pl.multiple_of((r4 + 4 * q) * sub + c0, ch) for q in range(4)]
        mm = jnp.concatenate([m_s[pl.ds(rw, ch), :] for rw in rows], axis=0)
        ll = jnp.concatenate([l_s[pl.ds(rw, ch), :] for rw in rows], axis=0)
        aa = jnp.concatenate([acc_s[pl.ds(rw, ch), :] for rw in rows], axis=0)
        state = [mm[:, 0:1], ll[:, 0:1], aa, mm[:, HEAD_DIM:HEAD_DIM + 1], ll[:, HEAD_DIM:HEAD_DIM + 1], aa]
        mask = band_4 & (cur | (n > 0))
        mm, ll, aa = pack_state(attend(qb, kb, vb, mask, state))
        for q, rw in enumerate(rows):
            m_s[pl.ds(rw, ch), :] = mm[q * ch:(q + 1) * ch, :]
            l_s[pl.ds(rw, ch), :] = ll[q * ch:(q + 1) * ch, :]
            acc_s[pl.ds(rw, ch), :] = aa[q * ch:(q + 1) * ch, :]
        return carry

    lax.fori_loop(0, 4 * nb4, d4_body, 0)

    def renat(blk, carry):
        j0 = pl.multiple_of(blk * MAX_DIL, MAX_DIL)
        r0 = pl.multiple_of(blk * PERM, PERM)
        rows = [pl.multiple_of(r * sub + j0, MAX_DIL) for r in range(MAX_DIL)]
        mm = jnp.concatenate([m_s[pl.ds(rw, MAX_DIL), :] for rw in rows], axis=0)
        ll = jnp.concatenate([l_s[pl.ds(rw, MAX_DIL), :] for rw in rows], axis=0)
        aa = jnp.concatenate([acc_s[pl.ds(rw, MAX_DIL), :] for rw in rows], axis=0)
        o = (aa / ll).astype(BF16)
        hi, mid, lo = _split3(mm + jnp.log2(ll))
        on_s[pl.ds(r0, PERM), :] = jnp.dot(perm, o, preferred_element_type=F32)
        lse_s[pl.ds(r0, PERM), :] = (jnp.dot(perm, hi, preferred_element_type=F32)
                                     + jnp.dot(perm, mid, preferred_element_type=F32)
                                     + jnp.dot(perm, lo, preferred_element_type=F32))
        return carry

    lax.fori_loop(0, seq // PERM, renat, 0)

    def d1_body(n, carry):
        c0 = pl.multiple_of(n * WIN, WIN)
        p0 = pl.multiple_of(jnp.maximum(n - 1, 0) * WIN, WIN)
        qb = q_ref[0, pl.ds(c0, WIN), :]
        kb = jnp.concatenate([k_ref[0, pl.ds(p0, WIN), :], k_ref[0, pl.ds(c0, WIN), :]], axis=0)
        vb = jnp.concatenate([v_ref[0, pl.ds(p0, WIN), :], v_ref[0, pl.ds(c0, WIN), :]], axis=0)
        lse = lse_s[pl.ds(c0, WIN), :]
        o = on_s[pl.ds(c0, WIN), :]
        one = jnp.ones((WIN, 1), F32)
        state = [lse[:, 0:1], one, o, lse[:, HEAD_DIM:HEAD_DIM + 1], one, o]
        mask = band_nat & (cur | (n > 0))
        m0, l0, a0, m1, l1, a1 = attend(qb, kb, vb, mask, state)
        o_ref[0, pl.ds(c0, WIN), :] = jnp.where(own0, a0 / l0, a1 / l1).astype(o_ref.dtype)
        return carry

    lax.fori_loop(0, seq // WIN, d1_body, 0)


def _dilated_fast(o_ref, qn, kn, vn, qr, kr, vr, fr, fn, s_scr, mask_scr, *, seq):
    sub = seq // MAX_DIL
    own0 = lax.broadcasted_iota(jnp.int32, (WIN, LANES), 1) < HEAD_DIM
    head_mask = [jnp.where(own0, 1.0, 0.0).astype(BF16), jnp.where(own0, 0.0, 1.0).astype(BF16)]
    ri = lax.broadcasted_iota(jnp.int32, (PERM, PERM), 0)
    ci = lax.broadcasted_iota(jnp.int32, (PERM, PERM), 1)
    perm = (((ri >> 4) == (ci & 15)) & ((ri & 15) == (ci >> 4))).astype(BF16)

    qa = lax.broadcasted_iota(jnp.int32, (WIN, 2 * WIN), 0)
    kb_i = lax.broadcasted_iota(jnp.int32, (WIN, 2 * WIN), 1)
    cur = kb_i >= WIN
    kin = kb_i & (WIN - 1)
    dist_nat = qa - kin + jnp.where(cur, 0, WIN)
    band_nat = (dist_nat >= 0) & (dist_nat <= WIN)
    sq = ((qa & 31) << 2) + (qa >> 5)
    sk = ((kin & 31) << 2) + (kin >> 5)
    dist_4 = sq - sk + jnp.where(cur, 0, WIN)
    band_4 = (dist_4 >= 0) & (dist_4 <= WIN)
    for i, band in enumerate((band_nat, band_4)):
        mask_scr[2 * i] = jnp.where(band & cur, 1.0, 0.0).astype(BF16)
        mask_scr[2 * i + 1] = jnp.where(band, 1.0, 0.0).astype(BF16)

    def run_branch(nblocks, block_in_seq, fetch, mask_base, sink, before_trip=None):
        def produce(t, pair, u):
            for hl in range(2):
                qb, kb = fetch(t, hl, True)
                s_scr[pair, u, hl] = lax.dot_general(qb * head_mask[hl], kb, (((1,), (1,)), ((), ())),
                                                     preferred_element_type=F32)

        def consume(t, pair, u):
            mk = mask_scr[mask_base + jnp.minimum(block_in_seq(t), 1)]
            pv = []
            for hl in range(2):
                p = jnp.exp2(s_scr[pair, u, hl]).astype(BF16) * mk
                pv.append(jnp.dot(p, fetch(t, hl, False), preferred_element_type=F32))
            sink(t, pv)

        for u in range(BRANCH_UNROLL):
            produce(u, 0, u)

        def body(i, carry):
            pair = i & 1
            if before_trip is not None:
                before_trip(i)
            for u in range(BRANCH_UNROLL):
                consume(BRANCH_UNROLL * i + u, pair, u)
            for u in range(BRANCH_UNROLL):
                produce(jnp.minimum(BRANCH_UNROLL * (i + 1) + u, nblocks - 1), 1 - pair, u)
            return carry

        lax.fori_loop(0, nblocks // BRANCH_UNROLL, body, 0)

    def prev_cur(ref_block, n, size):
        c0 = pl.multiple_of(n * size, size)
        p0 = pl.multiple_of(jnp.maximum(n - 1, 0) * size, size)
        return ref_block(p0) + ref_block(c0)

    nb16 = sub // WIN

    def fetch16(t, hl, qk):
        r, n = t // nb16, t % nb16
        if qk:
            return (qr[hl][r, pl.ds(pl.multiple_of(n * WIN, WIN), WIN), :],
                    jnp.concatenate(prev_cur(lambda s: [kr[hl][r, pl.ds(s, WIN), :]], n, WIN), axis=0))
        return jnp.concatenate(prev_cur(lambda s: [vr[hl][r, pl.ds(s, WIN), :]], n, WIN), axis=0)

    def sink16(t, pv):
        row = pl.multiple_of(t * WIN, WIN)
        for hl in range(2):
            fr[hl][pl.ds(row, WIN), :] = pv[hl]

    run_branch(MAX_DIL * nb16, lambda t: t % nb16, fetch16, 0, sink16)

    nb4 = (seq // 4) // WIN
    ch = WIN // 4

    def fetch4(t, hl, qk):
        r4, n = t // nb4, t % nb4
        gather = lambda ref: (lambda s: [ref[r4 + 4 * q, pl.ds(s, ch), :] for q in range(4)])
        if qk:
            return (jnp.concatenate(gather(qr[hl])(pl.multiple_of(n * ch, ch)), axis=0),
                    jnp.concatenate(prev_cur(gather(kr[hl]), n, ch), axis=0))
        return jnp.concatenate(prev_cur(gather(vr[hl]), n, ch), axis=0)

    def sink4(t, pv):
        r4, n = t // nb4, t % nb4
        for q in range(4):
            row = pl.multiple_of((r4 + 4 * q) * sub + n * ch, ch)
            for hl in range(2):
                fr[hl][pl.ds(row, ch), :] += pv[hl][q * ch:(q + 1) * ch, :]

    run_branch(4 * nb4, lambda t: t % nb4, fetch4, 2, sink4)

    def renat(i):
        for u in range(BRANCH_UNROLL * WIN // PERM):
            blk = i * (BRANCH_UNROLL * WIN // PERM) + u
            j0 = pl.multiple_of(blk * MAX_DIL, MAX_DIL)
            r0 = pl.multiple_of(blk * PERM, PERM)
            for hl in range(2):
                a = jnp.concatenate([fr[hl][pl.ds(pl.multiple_of(r * sub + j0, MAX_DIL), MAX_DIL), :]
                                     for r in range(MAX_DIL)], axis=0)
                hi = a.astype(BF16)
                lo = (a - hi.astype(F32)).astype(BF16)
                y = jnp.dot(perm, jnp.concatenate([hi, lo], axis=1), preferred_element_type=F32)
                fn[hl][pl.ds(r0, PERM), :] = y[:, 0:LANES] + y[:, LANES:2 * LANES]

    def fetch1(t, hl, qk):
        if qk:
            return (qn[hl][pl.ds(pl.multiple_of(t * WIN, WIN), WIN), :],
                    jnp.concatenate(prev_cur(lambda s: [kn[hl][pl.ds(s, WIN), :]], t, WIN), axis=0))
        return jnp.concatenate(prev_cur(lambda s: [vn[hl][pl.ds(s, WIN), :]], t, WIN), axis=0)

    def sink1(t, pv):
        row = pl.multiple_of(t * WIN, WIN)
        t0 = fn[0][pl.ds(row, WIN), :] + pv[0]
        t1 = fn[1][pl.ds(row, WIN), :] + pv[1]
        o_ref[0, pl.ds(row, WIN), :] = jnp.where(own0, t0 / t0[:, HEAD_DIM:HEAD_DIM + 1],
                                                 t1 / t1[:, 0:1]).astype(o_ref.dtype)

    run_branch(seq // WIN, lambda t: t, fetch1, 0, sink1, before_trip=renat)


def _dilated_kernel(q_ref, k_ref, v_ref, o_ref, vn0, vn1, qr, kr, vr0, vr1,
                    f0, f1, f2, f3, f4, s_scr, mask_scr, *, seq, chunk):
    lane = lax.broadcasted_iota(jnp.int32, (chunk, LANES), 1)
    vn = (vn0, vn1)

    ri = lax.broadcasted_iota(jnp.int32, (PERM, PERM), 0)
    ci_ = lax.broadcasted_iota(jnp.int32, (PERM, PERM), 1)
    perm = (((ri >> 4) == (ci_ & 15)) & ((ri & 15) == (ci_ >> 4))).astype(BF16)

    def stats(ci, sq_max):
        rows = pl.ds(pl.multiple_of(ci * chunk, chunk), chunk)
        qb = q_ref[0, rows, :]
        kb = k_ref[0, rows, :]
        vc = v_ref[0, rows, :].astype(F32)
        vb = []
        for hl in range(2):
            ones_lane = HEAD_DIM if hl == 0 else 0
            vb.append(jnp.where(_own_lanes(lane, hl), vc, jnp.where(lane == ones_lane, 1.0, 0.0)).astype(BF16))
            vn[hl][rows, :] = vb[hl]
        for u in range(chunk // PERM):
            j0 = pl.multiple_of((ci * (chunk // PERM) + u) * MAX_DIL, MAX_DIL)
            for srcs, dsts in (((qb, kb), (qr, kr)), (vb, (vr0, vr1))):
                both = jnp.concatenate([src[u * PERM:(u + 1) * PERM, :] for src in srcs], axis=1)
                y = jnp.dot(perm, both, preferred_element_type=F32).astype(BF16)
                for r in range(MAX_DIL):
                    for half, dst in enumerate(dsts):
                        dst[r, pl.ds(j0, MAX_DIL), :] = y[r * MAX_DIL:(r + 1) * MAX_DIL, half * LANES:(half + 1) * LANES]
        return (jnp.maximum(sq_max[0], _max_sq_norm(qb.astype(F32), selector)),
                jnp.maximum(sq_max[1], _max_sq_norm(kb.astype(F32), selector)))

    selector = _head_sum_selector()
    sq_max = lax.fori_loop(0, seq // chunk, stats, (jnp.zeros((1, LANES), F32),) * 2)
    worst = jnp.max(_norm_bound(sq_max[0], sq_max[1]))

    @pl.when(worst <= LOGIT_MAX)
    def _fast():
        qn, kn = q_ref.at[0], k_ref.at[0]
        _dilated_fast(o_ref, (qn, qn), (kn, kn), vn, (qr, qr), (kr, kr), (vr0, vr1), (f0, f1), (f2, f3),
                      s_scr, mask_scr, seq=seq)

    @pl.when(worst > LOGIT_MAX)
    def _general():
        _dilated_general(q_ref, k_ref, v_ref, o_ref, qr, kr, vr0, f0, f1, f2, f3, f4, seq=seq)


def _dilated(qkva, *, chunk=512):
    b, seq, _ = qkva.shape
    sub = seq // MAX_DIL
    kern = functools.partial(_dilated_kernel, seq=seq, chunk=chunk)
    return pl.pallas_call(
        kern,
        grid=(b, HEAD_PAIRS),
        in_specs=[
            pl.BlockSpec((1, seq, LANES), lambda bi, hp: (bi, 0, hp)),
            pl.BlockSpec((1, seq, LANES), lambda bi, hp: (bi, 0, HEAD_PAIRS + hp)),
            pl.BlockSpec((1, seq, LANES), lambda bi, hp: (bi, 0, 2 * HEAD_PAIRS + hp)),
        ],
        out_specs=pl.BlockSpec((1, seq, LANES), lambda bi, hp: (bi, 0, hp)),
        out_shape=jax.ShapeDtypeStruct((b, seq, WIDTH_A), BF16),
        scratch_shapes=(
            [pltpu.VMEM((seq, LANES), BF16)] * 2
            + [pltpu.VMEM((MAX_DIL, sub, LANES), BF16)] * 4
            + [pltpu.VMEM((seq, LANES), F32)] * 5
            + [pltpu.VMEM((2, BRANCH_UNROLL, 2, WIN, 2 * WIN), F32), pltpu.VMEM((4, WIN, 2 * WIN), BF16)]
        ),
        compiler_params=pltpu.CompilerParams(
            dimension_semantics=("arbitrary", "arbitrary"), vmem_limit_bytes=VMEM_LIMIT),
        name="dilated",
    )(qkva, qkva, qkva)


def _outproj_kernel(oa_ref, ob_ref, x_ref, wo_ref, g_ref, wr_ref, x1_ref, h2_ref, cls_ref):
    tm = x_ref.shape[0]
    x1 = (x_ref[...]
          + jnp.dot(oa_ref[...], wo_ref[0:WIDTH_A, :], preferred_element_type=F32)
          + jnp.dot(ob_ref[...], wo_ref[WIDTH_A:WIDTH_A + WIDTH_B, :], preferred_element_type=F32))
    x1_ref[...] = x1
    ms = jnp.mean(x1 * x1, axis=-1, keepdims=True)
    h2 = x1 * lax.rsqrt(ms + NORM_EPS) * g_ref[...]
    h2_ref[:, 0:D_MODEL] = h2
    h2_hi = h2.astype(BF16)
    h2_lo = (h2 - h2_hi.astype(F32)).astype(BF16)
    both = jnp.dot(h2_hi, wr_ref[...], preferred_element_type=F32)
    logits = (both[:, 0:LANES] + both[:, LANES:2 * LANES]
              + jnp.dot(h2_lo, wr_ref[:, 0:LANES], preferred_element_type=F32))
    lane = lax.broadcasted_iota(jnp.int32, (tm, LANES), 1)
    lane_f = lane.astype(F32)
    big = float(LANES)
    gmask = lane < N_GROUPS
    gl = jnp.where(gmask, logits, NEG_INF)
    gmax = jnp.max(gl, axis=1, keepdims=True)
    gsum = jnp.sum(jnp.where(gmask, jnp.exp(gl - gmax), 0.0), axis=1, keepdims=True)
    p_top = 1.0 / gsum
    g_star = jnp.min(jnp.where(gmask & (gl == gmax), lane_f, big), axis=1, keepdims=True)
    lo_lane = ROUTER_OFF + EXPERTS_PER_GROUP * g_star
    emask = (lane_f >= lo_lane) & (lane_f < lo_lane + EXPERTS_PER_GROUP)
    sel = jnp.where(emask, logits, NEG_INF)
    v1 = jnp.max(sel, axis=1, keepdims=True)
    i1 = jnp.min(jnp.where(emask & (sel == v1), lane_f, big), axis=1, keepdims=True)
    rest = emask & (lane_f != i1)
    sel2 = jnp.where(rest, logits, NEG_INF)
    v2 = jnp.max(sel2, axis=1, keepdims=True)
    i2 = jnp.min(jnp.where(rest & (sel2 == v2), lane_f, big), axis=1, keepdims=True)
    e2 = jnp.exp(v2 - v1)
    w1 = p_top / (1.0 + e2)
    w2 = p_top * e2 / (1.0 + e2)
    e1 = i1 - lo_lane
    e2x = i2 - lo_lane
    a = jnp.minimum(e1, e2x)
    b = jnp.maximum(e1, e2x)
    cls = g_star * N_PAIRS + a * (7.0 - a) * 0.5 + (b - a - 1.0)
    wa = jnp.where(e1 < e2x, w1, w2)
    wb = jnp.where(e1 < e2x, w2, w1)
    h2_ref[:, D_MODEL:EXT_WIDTH] = jnp.where(lane == 0, wa, jnp.where(lane == 1, wb, 0.0))
    row = lax.broadcasted_iota(jnp.int32, (tm, LANES), 0)
    spread = jnp.where(lane == (row & (LANES - 1)), cls, 0.0)
    cls_ref[0] = jnp.sum(spread.reshape(tm // LANES, LANES, LANES), axis=1).astype(jnp.int32)


def _outproj(oa, ob, x2, wo, g, wr, *, tm):
    n = x2.shape[0]
    return pl.pallas_call(
        _outproj_kernel,
        grid=(n // tm,),
        in_specs=[
            pl.BlockSpec((tm, WIDTH_A), lambda i: (i, 0)),
            pl.BlockSpec((tm, WIDTH_B), lambda i: (i, 0)),
            pl.BlockSpec((tm, D_MODEL), lambda i: (i, 0)),
            pl.BlockSpec(wo.shape, lambda i: (0, 0)),
            pl.BlockSpec((1, D_MODEL), lambda i: (0, 0)),
            pl.BlockSpec(wr.shape, lambda i: (0, 0)),
        ],
        out_specs=[
            pl.BlockSpec((tm, D_MODEL), lambda i: (i, 0)),
            pl.BlockSpec((tm, EXT_WIDTH), lambda i: (i, 0)),
            pl.BlockSpec((1, tm // LANES, LANES), lambda i: (i, 0, 0)),
        ],
        out_shape=[
            jax.ShapeDtypeStruct((n, D_MODEL), F32),
            jax.ShapeDtypeStruct((n, EXT_WIDTH), F32),
            jax.ShapeDtypeStruct((n // tm, tm // LANES, LANES), jnp.int32),
        ],
        compiler_params=pltpu.CompilerParams(dimension_semantics=("arbitrary",), vmem_limit_bytes=VMEM_LIMIT),
        name="outproj",
    )(oa, ob, x2, wo, g, wr)


def _sort_kernel(cls_ref, pos_ref, tile_ref, ends_ref, *, tile):
    cls = cls_ref[...]
    rows = cls.shape[0]
    upper = (lax.broadcasted_iota(jnp.int32, (LANES, LANES), 0)
             <= lax.broadcasted_iota(jnp.int32, (LANES, LANES), 1)).astype(BF16)
    earlier_rows = (lax.broadcasted_iota(jnp.int32, (rows, rows), 0)
                    > lax.broadcasted_iota(jnp.int32, (rows, rows), 1)).astype(BF16)
    ones = jnp.ones((LANES, LANES), BF16)
    lane8 = lax.broadcasted_iota(jnp.int32, tile_ref.shape, 1)
    tile_start = lane8.astype(F32) * tile
    off = jnp.zeros((1, 1), F32)
    pos = jnp.zeros(cls.shape, F32)
    tile_cls = jnp.zeros(tile_ref.shape, F32)
    ends = jnp.zeros(tile_ref.shape, F32)
    for c in range(N_CLASSES):
        hit = cls == c
        hot = jnp.where(hit, 1.0, 0.0).astype(BF16)
        in_row = jnp.dot(hot, upper, preferred_element_type=F32)
        row_tot = jnp.dot(hot, ones, preferred_element_type=F32)
        before = jnp.dot(earlier_rows, row_tot.astype(BF16), preferred_element_type=F32)
        count = before[rows - 1:rows, 0:1] + row_tot[rows - 1:rows, 0:1]
        pos = pos + jnp.where(hit, in_row - 1.0 + before + off, 0.0)
        off = off + jnp.ceil(count / tile) * tile
        tile_cls = tile_cls + jnp.where(tile_start >= off, 1.0, 0.0)
        ends = jnp.where(lane8 == c, off, ends)
    pos_ref[...] = pos.astype(jnp.int32)
    tile_ref[...] = tile_cls.astype(jnp.int32)
    ends_ref[...] = ends.astype(jnp.int32)


def _sort(cls2d, *, tile):
    meta = jax.ShapeDtypeStruct((8, LANES), jnp.int32)
    return pl.pallas_call(
        functools.partial(_sort_kernel, tile=tile),
        out_shape=[jax.ShapeDtypeStruct(cls2d.shape, jnp.int32), meta, meta],
        compiler_params=pltpu.CompilerParams(vmem_limit_bytes=VMEM_LIMIT),
        name="moe_sort",
    )(cls2d)


def _row_copy(src, src_row, dst, dst_row, sem, rows=1):
    return pltpu.make_async_copy(src.at[pl.ds(src_row, rows)], dst.at[pl.ds(dst_row, rows)], sem)


def _dispatch_kernel(pos_ref, ends_ref, h2_ref, xs_hbm, zero_scr, sem, *, ch, tile):
    g = pl.program_id(0)

    @pl.when(g == 0)
    def _zero_tails():
        zero_scr[...] = jnp.zeros_like(zero_scr)
        n_tiles = xs_hbm.shape[0] // tile
        min_used = n_tiles - N_CLASSES
        total = ends_ref[0, N_CLASSES - 1]
        for phase in ("start", "wait"):
            def zero_tile(row0):
                cp = pltpu.make_async_copy(zero_scr, xs_hbm.at[pl.ds(pl.multiple_of(row0, tile), tile)], sem)
                cp.start() if phase == "start" else cp.wait()

            for c in range(N_CLASSES):
                end = ends_ref[0, c]
                prev = ends_ref[0, c - 1] if c else 0
                pl.when(end > prev)(functools.partial(zero_tile, end - tile))
                pl.when((min_used + c) * tile >= total)(functools.partial(zero_tile, (min_used + c) * tile))

    for k in range(ch):
        _row_copy(h2_ref, k, xs_hbm, pos_ref[0, 0, k], sem).start(priority=k % DMA_THREADS)
    _row_copy(h2_ref, 0, xs_hbm, 0, sem, rows=ch).wait()


def _dispatch(pos3, ends, h2ext, *, n_rows, tile):
    n_chunks, _, ch = pos3.shape
    return pl.pallas_call(
        functools.partial(_dispatch_kernel, ch=ch, tile=tile),
        grid=(n_chunks,),
        in_specs=[
            pl.BlockSpec((1, 1, ch), lambda g: (g, 0, 0), memory_space=pltpu.SMEM),
            pl.BlockSpec(ends.shape, lambda g: (0, 0), memory_space=pltpu.SMEM),
            pl.BlockSpec((ch, EXT_WIDTH), lambda g: (g, 0)),
        ],
        out_specs=pl.BlockSpec(memory_space=pl.ANY),
        out_shape=jax.ShapeDtypeStruct((n_rows, EXT_WIDTH), F32),
        scratch_shapes=[pltpu.VMEM((tile, EXT_WIDTH), F32), pltpu.SemaphoreType.DMA(())],
        compiler_params=pltpu.CompilerParams(dimension_semantics=("arbitrary",), vmem_limit_bytes=VMEM_LIMIT),
        name="moe_dispatch",
    )(pos3, ends, h2ext)


def _experts_kernel(ea_ref, eb_ref, used_ref, xs_ref, wga, wua, wda, wgb, wub, wdb, ys_ref):
    del ea_ref, eb_ref

    @pl.when(pl.program_id(0) < used_ref[0])
    def _():
        x = xs_ref[:, 0:D_MODEL].astype(BF16)
        wts = xs_ref[:, D_MODEL:EXT_WIDTH]
        y = None
        for col, (wg, wu, wd) in enumerate(((wga, wua, wda), (wgb, wub, wdb))):
            gate = jnp.dot(x, wg[0], preferred_element_type=F32)
            up = jnp.dot(x, wu[0], preferred_element_type=F32)
            he = (gate / (1.0 + jnp.exp(-gate)) * up * wts[:, col:col + 1]).astype(BF16)
            t = jnp.dot(he, wd[0], preferred_element_type=F32)
            y = t if y is None else y + t
        ys_ref[...] = y

    @pl.when(pl.program_id(0) >= used_ref[0])
    def _():
        ys_ref[...] = jnp.zeros_like(ys_ref)


def _experts(ea, eb, used, xs, wg, wu, wd, *, tile):
    n_tiles = xs.shape[0] // tile
    rows = lambda j, ea, eb, used: (jnp.minimum(j, used[0] - 1), 0)
    first = lambda j, ea, eb, used: (ea[j], 0, 0)
    second = lambda j, ea, eb, used: (eb[j], 0, 0)
    up_spec = lambda m: pl.BlockSpec((1, D_MODEL, D_EXPERT), m)
    down_spec = lambda m: pl.BlockSpec((1, D_EXPERT, D_MODEL), m)
    return pl.pallas_call(
        _experts_kernel,
        grid_spec=pltpu.PrefetchScalarGridSpec(
            num_scalar_prefetch=3,
            grid=(n_tiles,),
            in_specs=[pl.BlockSpec((tile, EXT_WIDTH), rows),
                      up_spec(first), up_spec(first), down_spec(first),
                      up_spec(second), up_spec(second), down_spec(second)],
            out_specs=pl.BlockSpec((tile, D_MODEL), lambda j, ea, eb, used: (j, 0)),
        ),
        out_shape=jax.ShapeDtypeStruct((xs.shape[0], D_MODEL), F32),
        compiler_params=pltpu.CompilerParams(dimension_semantics=("arbitrary",), vmem_limit_bytes=VMEM_LIMIT),
        name="moe_experts",
    )(ea, eb, used, xs, wg, wu, wd, wg, wu, wd)


def _combine_kernel(pos_ref, next_pos_ref, x1_ref, g_ref, ys_hbm, out_ref, y_scr, sem):
    i = pl.program_id(0)
    tm = x1_ref.shape[0]
    slot = i & 1

    def gather(p_ref, s):
        for k in range(tm):
            _row_copy(ys_hbm, p_ref[0, 0, k], y_scr.at[s], k, sem.at[s]).start(priority=k % DMA_THREADS)

    @pl.when(i == 0)
    def _():
        gather(pos_ref, 0)

    @pl.when(i + 1 < pl.num_programs(0))
    def _():
        gather(next_pos_ref, 1 - slot)

    _row_copy(ys_hbm, 0, y_scr.at[slot], 0, sem.at[slot], rows=tm).wait()
    x2 = x1_ref[...] + y_scr[slot]
    ms = jnp.mean(x2 * x2, axis=-1, keepdims=True)
    out_ref[...] = x2 * lax.rsqrt(ms + NORM_EPS) * g_ref[...]


def _combine(pos3, x1, g, ys):
    n_chunks, _, tm = pos3.shape
    return pl.pallas_call(
        _combine_kernel,
        grid=(n_chunks,),
        in_specs=[
            pl.BlockSpec((1, 1, tm), lambda i: (i, 0, 0), memory_space=pltpu.SMEM),
            pl.BlockSpec((1, 1, tm), lambda i: (jnp.minimum(i + 1, n_chunks - 1), 0, 0), memory_space=pltpu.SMEM),
            pl.BlockSpec((tm, D_MODEL), lambda i: (i, 0)),
            pl.BlockSpec((1, D_MODEL), lambda i: (0, 0)),
            pl.BlockSpec(memory_space=pl.ANY),
        ],
        out_specs=pl.BlockSpec((tm, D_MODEL), lambda i: (i, 0)),
        out_shape=jax.ShapeDtypeStruct(x1.shape, F32),
        scratch_shapes=[pltpu.VMEM((2, tm, D_MODEL), F32), pltpu.SemaphoreType.DMA((2,))],
        compiler_params=pltpu.CompilerParams(dimension_semantics=("arbitrary",), vmem_limit_bytes=VMEM_LIMIT),
        name="moe_combine",
    )(pos3, pos3, x1, g, ys)


def _moe(h2ext, cls3, x1, wg, wu, wd, g, *, tile=MOE_TILE, dispatch_chunk=512, combine_chunk=512):
    n = x1.shape[0]
    n_tiles = n // tile + N_CLASSES
    assert n_tiles <= LANES
    pos, tile_cls, ends = _sort(cls3.reshape(n // LANES, LANES), tile=tile)
    xs = _dispatch(pos.reshape(n // dispatch_chunk, 1, dispatch_chunk), ends, h2ext, n_rows=n_tiles * tile, tile=tile)
    used = ends[0, N_CLASSES - 1:N_CLASSES] // tile
    tc = tile_cls[0, :n_tiles]
    tc = jnp.where(jnp.arange(n_tiles) < used[0], tc, tc[jnp.maximum(used[0] - 1, 0)])
    pair_a = jnp.array([0, 0, 0, 1, 1, 2], jnp.int32)
    pair_b = jnp.array([1, 2, 3, 2, 3, 3], jnp.int32)
    ea = (tc // N_PAIRS) * EXPERTS_PER_GROUP + pair_a[tc % N_PAIRS]
    eb = (tc // N_PAIRS) * EXPERTS_PER_GROUP + pair_b[tc % N_PAIRS]
    ys = _experts(ea, eb, used, xs, wg, wu, wd, tile=tile)
    return _combine(pos.reshape(n // combine_chunk, 1, combine_chunk), x1, g, ys)


def _rotary_tables(seq):
    pos = np.arange(seq, dtype=np.float64)
    inv_freq = 1.0 / (ROPE_THETA ** (np.arange(0, ROT_DIM, 2, dtype=np.float64) / ROT_DIM))
    ang = pos[:, None] * inv_freq[None, :]
    cos, sin = np.cos(ang), np.sin(ang)
    zeros = np.zeros((seq, HEAD_DIM - ROT_DIM))
    zh = np.zeros((seq, ROT_HALF))
    cos_h = np.concatenate([cos, cos, np.ones((seq, HEAD_DIM - ROT_DIM))], axis=1)
    sa_h = np.concatenate([-sin, zh, zeros], axis=1)
    sb_h = np.concatenate([zh, sin, zeros], axis=1)
    tile = lambda t: jnp.asarray(np.concatenate([t, t], axis=1), F32)
    return tile(cos_h), tile(sa_h), tile(sb_h)


def kernel(x, attn_norm, w_in, b_forget, w_out, ffn_norm, w_group, w_expert, w_gate_e, w_up_e, w_down_e, final_norm):
    b, seq, d = x.shape
    assert d == D_MODEL and w_in.shape[0] == 1, "single-layer block"
    n = b * seq
    scale = HEAD_DIM ** -0.5 * LOG2E
    col_scale = jnp.ones((2 * QKV_WIDTH + HEADS_B,), F32)
    col_scale = col_scale.at[0:WIDTH_A].set(scale).at[QKV_WIDTH:QKV_WIDTH + WIDTH_B].set(scale)
    w = jnp.pad(w_in[0] * col_scale[None, :], ((0, 0), (0, LANES - HEADS_B))).astype(BF16)
    bf = jnp.pad(b_forget[0].astype(F32), (0, LANES - HEADS_B))[None, :]
    cosv, sa, sb = _rotary_tables(seq)
    x2 = x.reshape(n, d)

    qkva, qkvb, c = _inproj(x2, attn_norm[0][None, :], w, cosv, sa, sb, bf, seq=seq, tm=512)
    out_a = _dilated(qkva.reshape(b, seq, QKV_WIDTH))
    out_b, wg, wu, wd = _fox(qkvb.reshape(b, seq, QKV_WIDTH), c.reshape(b, seq, LANES),
                             w_gate_e[0], w_up_e[0], w_down_e[0], tq=1024)

    wr = jnp.pad(jnp.concatenate([w_group[0], w_expert[0]], axis=1).astype(F32),
                 ((0, 0), (0, LANES - N_GROUPS - N_EXPERTS)))
    wr_hi = wr.astype(BF16)
    wr = jnp.concatenate([wr_hi, (wr - wr_hi.astype(F32)).astype(BF16)], axis=1)
    x1, h2ext, cls3 = _outproj(out_a.reshape(n, WIDTH_A), out_b.reshape(n, WIDTH_B), x2,
                               w_out[0].astype(BF16), ffn_norm[0][None, :], wr, tm=512)
    out = _moe(h2ext, cls3, x1, wg, wu, wd, final_norm[None, :])
    return out.reshape(b, seq, d)
```

```python
import functools

import jax
import jax.numpy as jnp
import numpy as np
from jax import lax
from jax.experimental import pallas as pl
from jax.experimental.pallas import tpu as pltpu

F32 = jnp.float32
BF16 = jnp.bfloat16

D_MODEL = 1024
HEAD_DIM = 64
HEADS_A = 8
HEADS_B = 8
WIDTH_A = HEADS_A * HEAD_DIM
WIDTH_B = HEADS_B * HEAD_DIM
QKV_WIDTH = 3 * WIDTH_A
DILATIONS = ((128, 1), (512, 4), (2048, 16))
ROT_DIM = HEAD_DIM // 4
ROT_HALF = ROT_DIM // 2
ROPE_THETA = 500000.0
N_GROUPS = 4
EXPERTS_PER_GROUP = 4
N_EXPERTS = N_GROUPS * EXPERTS_PER_GROUP
D_EXPERT = 512
NORM_EPS = 1e-6
NEG_INF = -1e30

LANES = 128
HEAD_PAIRS = WIDTH_A // LANES
WIN = 128
MAX_DIL = 16
PERM = MAX_DIL * MAX_DIL
KV_BLOCK = 256
N_PAIRS = EXPERTS_PER_GROUP * (EXPERTS_PER_GROUP - 1) // 2
N_CLASSES = N_GROUPS * N_PAIRS
MOE_TILE = 256
EXT_WIDTH = D_MODEL + LANES
LOGIT_MAX = 60.0
BRANCH_UNROLL = 8
DMA_THREADS = 2
ROUTER_OFF = N_GROUPS
VMEM_LIMIT = 56 * 1024 * 1024
LOG2E = 1.4426950408889634
BOUND_SLACK = 1.0 + 2.0 ** -6
BOUND_EPS = 2.0 ** -7


def _split3(x):
    hi = x.astype(BF16)
    r1 = x - hi.astype(F32)
    mid = r1.astype(BF16)
    lo = (r1 - mid.astype(F32)).astype(BF16)
    return hi, mid, lo


def _inproj_kernel(x_ref, g_ref, w_ref, cos_ref, sa_ref, sb_ref, bf_ref,
                   qkva_ref, qkvb_ref, c_ref, h_scr, carry_scr, *, tiles_per_seq):
    i = pl.program_id(0)
    tm = x_ref.shape[0]
    x = x_ref[...]
    ms = jnp.mean(x * x, axis=-1, keepdims=True)
    h_scr[...] = (x * lax.rsqrt(ms + NORM_EPS) * g_ref[...]).astype(BF16)
    cosv = cos_ref[...]
    sa = sa_ref[...]
    sb = sb_ref[...]
    for g in range(6):
        p = jnp.dot(h_scr[...], w_ref[:, g * WIDTH_A:(g + 1) * WIDTH_A], preferred_element_type=F32)
        dst = qkva_ref if g < 3 else qkvb_ref
        col0 = (g % 3) * WIDTH_A
        if g < 2:
            for k in range(HEAD_PAIRS):
                t = p[:, k * LANES:(k + 1) * LANES]
                t = (t * cosv + pltpu.roll(t, LANES - ROT_HALF, 1) * sa + pltpu.roll(t, ROT_HALF, 1) * sb)
                dst[:, col0 + k * LANES:col0 + (k + 1) * LANES] = t.astype(BF16)
        else:
            dst[:, col0:col0 + WIDTH_A] = p.astype(BF16)
    z = jnp.dot(h_scr[...], w_ref[:, 2 * QKV_WIDTH:2 * QKV_WIDTH + LANES], preferred_element_type=F32) + bf_ref[...]
    lf = jnp.minimum(z, 0.0) - jnp.log1p(jnp.exp(-jnp.abs(z)))
    lane = lax.broadcasted_iota(jnp.int32, (tm, LANES), 1)
    lf = jnp.where(lane < HEADS_B, lf * LOG2E, 0.0)
    hi, mid, lo = _split3(lf)
    tri = (lax.broadcasted_iota(jnp.int32, (tm, tm), 0) >= lax.broadcasted_iota(jnp.int32, (tm, tm), 1)).astype(BF16)
    cs = (jnp.dot(tri, hi, preferred_element_type=F32) + jnp.dot(tri, mid, preferred_element_type=F32)
          + jnp.dot(tri, lo, preferred_element_type=F32))

    @pl.when(i % tiles_per_seq == 0)
    def _():
        carry_scr[...] = jnp.zeros_like(carry_scr)

    c = cs + carry_scr[0:1, :]
    c_ref[...] = c
    carry_scr[...] = jnp.broadcast_to(c[tm - 1:tm, :], carry_scr.shape)


def _inproj(x2, g, w, cosv, sa, sb, bf, *, seq, tm):
    n = x2.shape[0]
    kern = functools.partial(_inproj_kernel, tiles_per_seq=seq // tm)
    tps = seq // tm
    return pl.pallas_call(
        kern,
        grid=(n // tm,),
        in_specs=[
            pl.BlockSpec((tm, D_MODEL), lambda i: (i, 0)),
            pl.BlockSpec((1, D_MODEL), lambda i: (0, 0)),
            pl.BlockSpec(w.shape, lambda i: (0, 0)),
            pl.BlockSpec((tm, LANES), lambda i: (i % tps, 0)),
            pl.BlockSpec((tm, LANES), lambda i: (i % tps, 0)),
            pl.BlockSpec((tm, LANES), lambda i: (i % tps, 0)),
            pl.BlockSpec((1, LANES), lambda i: (0, 0)),
        ],
        out_specs=[
            pl.BlockSpec((tm, QKV_WIDTH), lambda i: (i, 0)),
            pl.BlockSpec((tm, QKV_WIDTH), lambda i: (i, 0)),
            pl.BlockSpec((tm, LANES), lambda i: (i, 0)),
        ],
        out_shape=[
            jax.ShapeDtypeStruct((n, QKV_WIDTH), BF16),
            jax.ShapeDtypeStruct((n, QKV_WIDTH), BF16),
            jax.ShapeDtypeStruct((n, LANES), F32),
        ],
        scratch_shapes=[pltpu.VMEM((tm, D_MODEL), BF16), pltpu.VMEM((8, LANES), F32)],
        compiler_params=pltpu.CompilerParams(dimension_semantics=("arbitrary",), vmem_limit_bytes=VMEM_LIMIT),
        name="inproj",
    )(x2, g, w, cosv, sa, sb, bf)


def _head_column(c_all, lane, head):
    return jnp.sum(jnp.where(lane == head, c_all, 0.0), axis=1, keepdims=True)


def _own_lanes(lane, hl):
    return (lane < HEAD_DIM) if hl == 0 else (lane >= HEAD_DIM)


def _head_sum_selector():
    lane = lax.broadcasted_iota(jnp.int32, (LANES, LANES), 0)
    col = lax.broadcasted_iota(jnp.int32, (LANES, LANES), 1)
    return jnp.where(((col == 0) & (lane < HEAD_DIM)) | ((col == 1) & (lane >= HEAD_DIM)), 1.0, 0.0).astype(BF16)


def _max_sq_norm(t, selector):
    return jnp.max(jnp.dot((t * t).astype(BF16), selector, preferred_element_type=F32), axis=0, keepdims=True)


def _norm_bound(qsq, ksq_max):
    return jnp.sqrt(qsq * ksq_max) * BOUND_SLACK + BOUND_EPS


def _fox_kernel(q_ref, k_ref, v_ref, c_ref, wg_ref, wu_ref, wd_ref, o_ref, wg_out, wu_out, wd_out,
                q0_scr, q1_scr, k0_scr, k1_scr, v0_scr, v1_scr, bound_scr, acc0_scr, acc1_scr, s_scr,
                *, tq, seq, chunk):
    hp = pl.program_id(1)
    qi = pl.program_id(2)
    for src, dst in ((wg_ref, wg_out), (wu_ref, wu_out), (wd_ref, wd_out)):
        dst[...] = src[...].astype(dst.dtype)
    q_scr = (q0_scr, q1_scr)
    k_scr = (k0_scr, k1_scr)
    v_scr = (v0_scr, v1_scr)
    acc_scr = (acc0_scr, acc1_scr)

    def augmented(vals, cterm, lane, hl, key_side):
        hi, mid, lo = (t.astype(F32) for t in _split3(cterm))
        base = HEAD_DIM if hl == 0 else 0
        ones = (lane >= base + (0 if key_side else 3)) & (lane < base + (3 if key_side else 6))
        o = 3 if key_side else 0
        sgn = -1.0 if key_side else 1.0
        aug = jnp.where(lane == base + o, sgn * hi,
                        jnp.where(lane == base + o + 1, sgn * mid,
                                  jnp.where(lane == base + o + 2, sgn * lo,
                                            jnp.where(ones, 1.0, 0.0))))
        return jnp.where(_own_lanes(lane, hl), vals, aug).astype(BF16)

    @pl.when(qi == 0)
    def _build():
        lane = lax.broadcasted_iota(jnp.int32, (chunk, LANES), 1)

        def body(ci, sq_max):
            rows = pl.ds(pl.multiple_of(ci * chunk, chunk), chunk)
            qc = q_ref[0, rows, :].astype(F32)
            kc = k_ref[0, rows, :].astype(F32)
            vc = v_ref[0, rows, :].astype(F32)
            cc = c_ref[0, rows, :]
            for hl in range(2):
                ccol = _head_column(cc, lane, 2 * hp + hl)
                q_scr[hl][rows, :] = augmented(qc, ccol, lane, hl, False)
                k_scr[hl][rows, :] = augmented(kc, ccol, lane, hl, True)
                one_lane = HEAD_DIM if hl == 0 else 0
                v_scr[hl][rows, :] = jnp.where(
                    _own_lanes(lane, hl), vc, jnp.where(lane == one_lane, 1.0, 0.0)).astype(BF16)
            return (jnp.maximum(sq_max[0], _max_sq_norm(qc, selector)),
                    jnp.maximum(sq_max[1], _max_sq_norm(kc, selector)))

        selector = _head_sum_selector()
        sq_max = lax.fori_loop(0, seq // chunk, body, (jnp.zeros((1, LANES), F32),) * 2)
        bound_scr[...] = jnp.broadcast_to(_norm_bound(sq_max[0], sq_max[1]), bound_scr.shape)

    lane_q = lax.broadcasted_iota(jnp.int32, (tq, LANES), 1)
    q_rows = pl.ds(pl.multiple_of(qi * tq, tq), tq)
    worst = jnp.max(bound_scr[...])
    n_diag = tq // KV_BLOCK
    n_full = qi * n_diag
    row_minus_col = (lax.broadcasted_iota(jnp.int32, (tq, KV_BLOCK), 0)
                     - lax.broadcasted_iota(jnp.int32, (tq, KV_BLOCK), 1))

    def causal_mask(s, d, top=0):
        return jnp.where(row_minus_col[top:tq, :] >= d * KV_BLOCK, s, NEG_INF)

    def logits(hl, j, top=0):
        r0 = pl.multiple_of(j * KV_BLOCK, KV_BLOCK)
        rows = q_rows if top == 0 else pl.ds(pl.multiple_of(qi * tq + top, KV_BLOCK), tq - top)
        return lax.dot_general(q_scr[hl][rows, :], k_scr[hl][pl.ds(r0, KV_BLOCK), :], (((1,), (1,)), ((), ())),
                               preferred_element_type=F32)

    def values(hl, j):
        return v_scr[hl][pl.ds(pl.multiple_of(j * KV_BLOCK, KV_BLOCK), KV_BLOCK), :]

    def finish(a0, a1):
        l0 = a0[:, HEAD_DIM:HEAD_DIM + 1]
        l1 = a1[:, 0:1]
        o_ref[0] = jnp.where(lane_q < HEAD_DIM, a0 / l0, a1 / l1).astype(o_ref.dtype)

    @pl.when(worst <= LOGIT_MAX)
    def _fast():
        acc0_scr[...] = jnp.zeros_like(acc0_scr)
        acc1_scr[...] = jnp.zeros_like(acc1_scr)

        def produce(j, slot, top=0):
            for hl in range(2):
                s_scr[slot, hl, top:tq, :] = logits(hl, j, top)

        def consume(j, slot, diag):
            top = 0 if diag is None else diag * KV_BLOCK
            for hl in range(2):
                s = s_scr[slot, hl, top:tq, :]
                if diag is not None:
                    s = causal_mask(s, diag, top)
                acc_scr[hl][top:tq, :] += jnp.dot(jnp.exp2(s).astype(BF16), values(hl, j),
                                                  preferred_element_type=F32)

        def produce_diagonal(d):
            produce(n_full + d, d, d * KV_BLOCK)

        @pl.when(qi == 0)
        def _():
            for d in range(n_diag):
                produce_diagonal(d)

        @pl.when(qi > 0)
        def _():
            for d in range(n_diag):
                produce(d, d)

            def body(i, carry):
                for d in range(n_diag):
                    j = i * n_diag + d
                    consume(j, d, None)
                    produce(j + n_diag, d)
                return carry

            lax.fori_loop(0, qi - 1, body, 0)
            for d in range(n_diag):
                consume(n_full - n_diag + d, d, None)
                produce_diagonal(d)

        for d in range(n_diag):
            consume(n_full + d, d, d)
        finish(acc0_scr[...], acc1_scr[...])

    @pl.when(worst > LOGIT_MAX)
    def _general():
        def step(j, carry, diag):
            new = []
            for hl in range(2):
                m, a = carry[2 * hl:2 * hl + 2]
                s = logits(hl, j)
                if diag is not None:
                    s = causal_mask(s, diag)
                mn = jnp.maximum(m, jnp.max(s, axis=1, keepdims=True))
                a = jnp.exp2(m - mn) * a + jnp.dot(jnp.exp2(s - mn).astype(BF16), values(hl, j),
                                                  preferred_element_type=F32)
                new += [mn, a]
            return tuple(new)

        carry = (jnp.full((tq, 1), NEG_INF, F32), jnp.zeros((tq, LANES), F32)) * 2
        carry = lax.fori_loop(0, n_full, lambda j, c: step(j, c, None), carry)
        for d in range(n_diag):
            carry = step(n_full + d, carry, d)
        finish(carry[1], carry[3])


def _fox(qkvb, c, wg, wu, wd, *, tq, chunk=512):
    b, seq, _ = qkvb.shape
    nq = seq // tq
    steps = b * HEAD_PAIRS * nq
    parts = steps // N_EXPERTS
    assert parts * N_EXPERTS == steps and D_MODEL % parts == 0 and D_EXPERT % parts == 0

    def slice_spec(rows, cols):
        def index(bi, hp, qi):
            step = (bi * HEAD_PAIRS + hp) * nq + qi
            return (step // parts, step % parts, 0)
        return pl.BlockSpec((1, rows // parts, cols), index)

    w_specs = [slice_spec(D_MODEL, D_EXPERT), slice_spec(D_MODEL, D_EXPERT), slice_spec(D_EXPERT, D_MODEL)]
    kern = functools.partial(_fox_kernel, tq=tq, seq=seq, chunk=chunk)
    return pl.pallas_call(
        kern,
        grid=(b, HEAD_PAIRS, nq),
        in_specs=[
            pl.BlockSpec((1, seq, LANES), lambda bi, hp, qi: (bi, 0, hp)),
            pl.BlockSpec((1, seq, LANES), lambda bi, hp, qi: (bi, 0, HEAD_PAIRS + hp)),
            pl.BlockSpec((1, seq, LANES), lambda bi, hp, qi: (bi, 0, 2 * HEAD_PAIRS + hp)),
            pl.BlockSpec((1, seq, LANES), lambda bi, hp, qi: (bi, 0, 0)),
        ] + w_specs,
        out_specs=[pl.BlockSpec((1, tq, LANES), lambda bi, hp, qi: (bi, qi, hp))] + w_specs,
        out_shape=[jax.ShapeDtypeStruct((b, seq, WIDTH_B), BF16)]
        + [jax.ShapeDtypeStruct(w.shape, BF16) for w in (wg, wu, wd)],
        scratch_shapes=[pltpu.VMEM((seq, LANES), BF16)] * 6 + [
                        pltpu.VMEM((8, LANES), F32),
                        pltpu.VMEM((tq, LANES), F32), pltpu.VMEM((tq, LANES), F32),
                        pltpu.VMEM((tq // KV_BLOCK, 2, tq, KV_BLOCK), F32)],
        compiler_params=pltpu.CompilerParams(
            dimension_semantics=("arbitrary", "arbitrary", "arbitrary"), vmem_limit_bytes=VMEM_LIMIT),
        name="fox",
    )(qkvb, qkvb, qkvb, c, wg, wu, wd)


def _dilated_general(q_ref, k_ref, v_ref, o_ref, q16, k16, v16, m_s, l_s, acc_s, on_s, lse_s, *, seq):
    sub = seq // MAX_DIL
    own0 = lax.broadcasted_iota(jnp.int32, (WIN, LANES), 1) < HEAD_DIM
    ri = lax.broadcasted_iota(jnp.int32, (PERM, PERM), 0)
    ci = lax.broadcasted_iota(jnp.int32, (PERM, PERM), 1)
    perm = (((ri >> 4) == (ci & 15)) & ((ri & 15) == (ci >> 4))).astype(BF16)

    qa = lax.broadcasted_iota(jnp.int32, (WIN, 2 * WIN), 0)
    kb_i = lax.broadcasted_iota(jnp.int32, (WIN, 2 * WIN), 1)
    cur = kb_i >= WIN
    kin = kb_i & (WIN - 1)
    dist_nat = qa - kin + jnp.where(cur, 0, WIN)
    band_nat = (dist_nat >= 0) & (dist_nat <= WIN)
    sq = ((qa & 31) << 2) + (qa >> 5)
    sk = ((kin & 31) << 2) + (kin >> 5)
    dist_4 = sq - sk + jnp.where(cur, 0, WIN)
    band_4 = (dist_4 >= 0) & (dist_4 <= WIN)

    def attend(qb, kb, vb, mask, state):
        qf = qb.astype(F32)
        new = []
        for hl in range(2):
            m, l, a = state[3 * hl:3 * hl + 3]
            own = own0 if hl == 0 else jnp.logical_not(own0)
            qh = jnp.where(own, qf, 0.0).astype(BF16)
            s = lax.dot_general(qh, kb, (((1,), (1,)), ((), ())), preferred_element_type=F32)
            s = jnp.where(mask, s, NEG_INF)
            mn = jnp.maximum(m, jnp.max(s, axis=1, keepdims=True))
            alpha = jnp.exp2(m - mn)
            p = jnp.exp2(s - mn)
            l = alpha * l + jnp.sum(p, axis=1, keepdims=True)
            a = alpha * a + jnp.dot(p.astype(BF16), vb, preferred_element_type=F32)
            new += [mn, l, a]
        return new

    def pack_state(st):
        m0, l0, a0, m1, l1, a1 = st
        return (jnp.where(own0, m0, m1), jnp.where(own0, l0, l1), jnp.where(own0, a0, a1))

    def deint(blk, carry):
        r0 = pl.multiple_of(blk * PERM, PERM)
        j0 = pl.multiple_of(blk * MAX_DIL, MAX_DIL)
        for src, dst in ((q_ref, q16), (k_ref, k16), (v_ref, v16)):
            y = jnp.dot(perm, src[0, pl.ds(r0, PERM), :], preferred_element_type=F32).astype(BF16)
            for r in range(MAX_DIL):
                dst[r, pl.ds(j0, MAX_DIL), :] = y[r * MAX_DIL:(r + 1) * MAX_DIL, :]
        return carry

    lax.fori_loop(0, seq // PERM, deint, 0)

    fresh = [jnp.full((WIN, 1), NEG_INF, F32), jnp.zeros((WIN, 1), F32), jnp.zeros((WIN, LANES), F32)] * 2
    nb16 = sub // WIN

    def d16_body(t, carry):
        r = t // nb16
        n = t % nb16
        c0 = pl.multiple_of(n * WIN, WIN)
        p0 = pl.multiple_of(jnp.maximum(n - 1, 0) * WIN, WIN)
        qb = q16[r, pl.ds(c0, WIN), :]
        kb = jnp.concatenate([k16[r, pl.ds(p0, WIN), :], k16[r, pl.ds(c0, WIN), :]], axis=0)
        vb = jnp.concatenate([v16[r, pl.ds(p0, WIN), :], v16[r, pl.ds(c0, WIN), :]], axis=0)
        mask = band_nat & (cur | (n > 0))
        mm, ll, aa = pack_state(attend(qb, kb, vb, mask, fresh))
        row = pl.multiple_of(r * sub + c0, WIN)
        m_s[pl.ds(row, WIN), :] = mm
        l_s[pl.ds(row, WIN), :] = ll
        acc_s[pl.ds(row, WIN), :] = aa
        return carry

    lax.fori_loop(0, MAX_DIL * nb16, d16_body, 0)

    nb4 = (seq // 4) // WIN
    ch = WIN // 4

    def d4_body(t, carry):
        r4 = t // nb4
        n = t % nb4
        c0 = pl.multiple_of(n * ch, ch)
        p0 = pl.multiple_of(jnp.maximum(n - 1, 0) * ch, ch)

        def gather(ref, j0):
            return [ref[r4 + 4 * q, pl.ds(j0, ch), :] for q in range(4)]

        qb = jnp.concatenate(gather(q16, c0), axis=0)
        kb = jnp.concatenate(gather(k16, p0) + gather(k16, c0), axis=0)
        vb = jnp.concatenate(gather(v16, p0) + gather(v16, c0), axis=0)
        rows = [pl.multiple_of((r4 + 4 * q) * sub + c0, ch) for q in range(4)]
        mm = jnp.concatenate([m_s[pl.ds(rw, ch), :] for rw in rows], axis=0)
        ll = jnp.concatenate([l_s[pl.ds(rw, ch), :] for rw in rows], axis=0)
        aa = jnp.concatenate([acc_s[pl.ds(rw, ch), :] for rw in rows], axis=0)
        state = [mm[:, 0:1], ll[:, 0:1], aa, mm[:, HEAD_DIM:HEAD_DIM + 1], ll[:, HEAD_DIM:HEAD_DIM + 1], aa]
        mask = band_4 & (cur | (n > 0))
        mm, ll, aa = pack_state(attend(qb, kb, vb, mask, state))
        for q, rw in enumerate(rows):
            m_s[pl.ds(rw, ch), :] = mm[q * ch:(q + 1) * ch, :]
            l_s[pl.ds(rw, ch), :] = ll[q * ch:(q + 1) * ch, :]
            acc_s[pl.ds(rw, ch), :] = aa[q * ch:(q + 1) * ch, :]
        return carry

    lax.fori_loop(0, 4 * nb4, d4_body, 0)

    def renat(blk, carry):
        j0 = pl.multiple_of(blk * MAX_DIL, MAX_DIL)
        r0 = pl.multiple_of(blk * PERM, PERM)
        rows = [pl.multiple_of(r * sub + j0, MAX_DIL) for r in range(MAX_DIL)]
        mm = jnp.concatenate([m_s[pl.ds(rw, MAX_DIL), :] for rw in rows], axis=0)
        ll = jnp.concatenate([l_s[pl.ds(rw, MAX_DIL), :] for rw in rows], axis=0)
        aa = jnp.concatenate([acc_s[pl.ds(rw, MAX_DIL), :] for rw in rows], axis=0)
        o = (aa / ll).astype(BF16)
        hi, mid, lo = _split3(mm + jnp.log2(ll))
        on_s[pl.ds(r0, PERM), :] = jnp.dot(perm, o, preferred_element_type=F32)
        lse_s[pl.ds(r0, PERM), :] = (jnp.dot(perm, hi, preferred_element_type=F32)
                                     + jnp.dot(perm, mid, preferred_element_type=F32)
                                     + jnp.dot(perm, lo, preferred_element_type=F32))
        return carry

    lax.fori_loop(0, seq // PERM, renat, 0)

    def d1_body(n, carry):
        c0 = pl.multiple_of(n * WIN, WIN)
        p0 = pl.multiple_of(jnp.maximum(n - 1, 0) * WIN, WIN)
        qb = q_ref[0, pl.ds(c0, WIN), :]
        kb = jnp.concatenate([k_ref[0, pl.ds(p0, WIN), :], k_ref[0, pl.ds(c0, WIN), :]], axis=0)
        vb = jnp.concatenate([v_ref[0, pl.ds(p0, WIN), :], v_ref[0, pl.ds(c0, WIN), :]], axis=0)
        lse = lse_s[pl.ds(c0, WIN), :]
        o = on_s[pl.ds(c0, WIN), :]
        one = jnp.ones((WIN, 1), F32)
        state = [lse[:, 0:1], one, o, lse[:, HEAD_DIM:HEAD_DIM + 1], one, o]
        mask = band_nat & (cur | (n > 0))
        m0, l0, a0, m1, l1, a1 = attend(qb, kb, vb, mask, state)
        o_ref[0, pl.ds(c0, WIN), :] = jnp.where(own0, a0 / l0, a1 / l1).astype(o_ref.dtype)
        return carry

    lax.fori_loop(0, seq // WIN, d1_body, 0)


def _dilated_fast(o_ref, qn, kn, vn, qr, kr, vr, fr, fn, s_scr, mask_scr, *, seq):
    sub = seq // MAX_DIL
    own0 = lax.broadcasted_iota(jnp.int32, (WIN, LANES), 1) < HEAD_DIM
    head_mask = [jnp.where(own0, 1.0, 0.0).astype(BF16), jnp.where(own0, 0.0, 1.0).astype(BF16)]
    ri = lax.broadcasted_iota(jnp.int32, (PERM, PERM), 0)
    ci = lax.broadcasted_iota(jnp.int32, (PERM, PERM), 1)
    perm = (((ri >> 4) == (ci & 15)) & ((ri & 15) == (ci >> 4))).astype(BF16)

    qa = lax.broadcasted_iota(jnp.int32, (WIN, 2 * WIN), 0)
    kb_i = lax.broadcasted_iota(jnp.int32, (WIN, 2 * WIN), 1)
    cur = kb_i >= WIN
    kin = kb_i & (WIN - 1)
    dist_nat = qa - kin + jnp.where(cur, 0, WIN)
    band_nat = (dist_nat >= 0) & (dist_nat <= WIN)
    sq = ((qa & 31) << 2) + (qa >> 5)
    sk = ((kin & 31) << 2) + (kin >> 5)
    dist_4 = sq - sk + jnp.where(cur, 0, WIN)
    band_4 = (dist_4 >= 0) & (dist_4 <= WIN)
    for i, band in enumerate((band_nat, band_4)):
        mask_scr[2 * i] = jnp.where(band & cur, 1.0, 0.0).astype(BF16)
        mask_scr[2 * i + 1] = jnp.where(band, 1.0, 0.0).astype(BF16)

    def run_branch(nblocks, block_in_seq, fetch, mask_base, sink, before_trip=None):
        def produce(t, pair, u):
            for hl in range(2):
                qb, kb = fetch(t, hl, True)
                s_scr[pair, u, hl] = lax.dot_general(qb * head_mask[hl], kb, (((1,), (1,)), ((), ())),
                                                     preferred_element_type=F32)

        def consume(t, pair, u):
            mk = mask_scr[mask_base + jnp.minimum(block_in_seq(t), 1)]
            pv = []
            for hl in range(2):
                p = jnp.exp2(s_scr[pair, u, hl]).astype(BF16) * mk
                pv.append(jnp.dot(p, fetch(t, hl, False), preferred_element_type=F32))
            sink(t, pv)

        for u in range(BRANCH_UNROLL):
            produce(u, 0, u)

        def body(i, carry):
            pair = i & 1
            if before_trip is not None:
                before_trip(i)
            for u in range(BRANCH_UNROLL):
                consume(BRANCH_UNROLL * i + u, pair, u)
            for u in range(BRANCH_UNROLL):
                produce(jnp.minimum(BRANCH_UNROLL * (i + 1) + u, nblocks - 1), 1 - pair, u)
            return carry

        lax.fori_loop(0, nblocks // BRANCH_UNROLL, body, 0)

    def prev_cur(ref_block, n, size):
        c0 = pl.multiple_of(n * size, size)
        p0 = pl.multiple_of(jnp.maximum(n - 1, 0) * size, size)
        return ref_block(p0) + ref_block(c0)

    nb16 = sub // WIN

    def fetch16(t, hl, qk):
        r, n = t // nb16, t % nb16
        if qk:
            return (qr[hl][r, pl.ds(pl.multiple_of(n * WIN, WIN), WIN), :],
                    jnp.concatenate(prev_cur(lambda s: [kr[hl][r, pl.ds(s, WIN), :]], n, WIN), axis=0))
        return jnp.concatenate(prev_cur(lambda s: [vr[hl][r, pl.ds(s, WIN), :]], n, WIN), axis=0)

    def sink16(t, pv):
        row = pl.multiple_of(t * WIN, WIN)
        for hl in range(2):
            fr[hl][pl.ds(row, WIN), :] = pv[hl]

    run_branch(MAX_DIL * nb16, lambda t: t % nb16, fetch16, 0, sink16)

    nb4 = (seq // 4) // WIN
    ch = WIN // 4

    def fetch4(t, hl, qk):
        r4, n = t // nb4, t % nb4
        gather = lambda ref: (lambda s: [ref[r4 + 4 * q, pl.ds(s, ch), :] for q in range(4)])
        if qk:
            return (jnp.concatenate(gather(qr[hl])(pl.multiple_of(n * ch, ch)), axis=0),
                    jnp.concatenate(prev_cur(gather(kr[hl]), n, ch), axis=0))
        return jnp.concatenate(prev_cur(gather(vr[hl]), n, ch), axis=0)

    def sink4(t, pv):
        r4, n = t // nb4, t % nb4
        for q in range(4):
            row = pl.multiple_of((r4 + 4 * q) * sub + n * ch, ch)
            for hl in range(2):
                fr[hl][pl.ds(row, ch), :] += pv[hl][q * ch:(q + 1) * ch, :]

    run_branch(4 * nb4, lambda t: t % nb4, fetch4, 2, sink4)

    def renat(i):
        for u in range(BRANCH_UNROLL * WIN // PERM):
            blk = i * (BRANCH_UNROLL * WIN // PERM) + u
            j0 = pl.multiple_of(blk * MAX_DIL, MAX_DIL)
            r0 = pl.multiple_of(blk * PERM, PERM)
            for hl in range(2):
                a = jnp.concatenate([fr[hl][pl.ds(pl.multiple_of(r * sub + j0, MAX_DIL), MAX_DIL), :]
                                     for r in range(MAX_DIL)], axis=0)
                hi = a.astype(BF16)
                lo = (a - hi.astype(F32)).astype(BF16)
                y = jnp.dot(perm, jnp.concatenate([hi, lo], axis=1), preferred_element_type=F32)
                fn[hl][pl.ds(r0, PERM), :] = y[:, 0:LANES] + y[:, LANES:2 * LANES]

    def fetch1(t, hl, qk):
        if qk:
            return (qn[hl][pl.ds(pl.multiple_of(t * WIN, WIN), WIN), :],
                    jnp.concatenate(prev_cur(lambda s: [kn[hl][pl.ds(s, WIN), :]], t, WIN), axis=0))
        return jnp.concatenate(prev_cur(lambda s: [vn[hl][pl.ds(s, WIN), :]], t, WIN), axis=0)

    def sink1(t, pv):
        row = pl.multiple_of(t * WIN, WIN)
        t0 = fn[0][pl.ds(row, WIN), :] + pv[0]
        t1 = fn[1][pl.ds(row, WIN), :] + pv[1]
        o_ref[0, pl.ds(row, WIN), :] = jnp.where(own0, t0 / t0[:, HEAD_DIM:HEAD_DIM + 1],
                                                 t1 / t1[:, 0:1]).astype(o_ref.dtype)

    run_branch(seq // WIN, lambda t: t, fetch1, 0, sink1, before_trip=renat)


def _dilated_kernel(q_ref, k_ref, v_ref, o_ref, vn0, vn1, qr, kr, vr0, vr1,
                    f0, f1, f2, f3, f4, s_scr, mask_scr, *, seq, chunk):
    lane = lax.broadcasted_iota(jnp.int32, (chunk, LANES), 1)
    vn = (vn0, vn1)

    ri = lax.broadcasted_iota(jnp.int32, (PERM, PERM), 0)
    ci_ = lax.broadcasted_iota(jnp.int32, (PERM, PERM), 1)
    perm = (((ri >> 4) == (ci_ & 15)) & ((ri & 15) == (ci_ >> 4))).astype(BF16)

    def stats(ci, sq_max):
        rows = pl.ds(pl.multiple_of(ci * chunk, chunk), chunk)
        qb = q_ref[0, rows, :]
        kb = k_ref[0, rows, :]
        vc = v_ref[0, rows, :].astype(F32)
        vb = []
        for hl in range(2):
            ones_lane = HEAD_DIM if hl == 0 else 0
            vb.append(jnp.where(_own_lanes(lane, hl), vc, jnp.where(lane == ones_lane, 1.0, 0.0)).astype(BF16))
            vn[hl][rows, :] = vb[hl]
        for u in range(chunk // PERM):
            j0 = pl.multiple_of((ci * (chunk // PERM) + u) * MAX_DIL, MAX_DIL)
            for srcs, dsts in (((qb, kb), (qr, kr)), (vb, (vr0, vr1))):
                both = jnp.concatenate([src[u * PERM:(u + 1) * PERM, :] for src in srcs], axis=1)
                y = jnp.dot(perm, both, preferred_element_type=F32).astype(BF16)
                for r in range(MAX_DIL):
                    for half, dst in enumerate(dsts):
                        dst[r, pl.ds(j0, MAX_DIL), :] = y[r * MAX_DIL:(r + 1) * MAX_DIL, half * LANES:(half + 1) * LANES]
        return (jnp.maximum(sq_max[0], _max_sq_norm(qb.astype(F32), selector)),
                jnp.maximum(sq_max[1], _max_sq_norm(kb.astype(F32), selector)))

    selector = _head_sum_selector()
    sq_max = lax.fori_loop(0, seq // chunk, stats, (jnp.zeros((1, LANES), F32),) * 2)
    worst = jnp.max(_norm_bound(sq_max[0], sq_max[1]))

    @pl.when(worst <= LOGIT_MAX)
    def _fast():
        qn, kn = q_ref.at[0], k_ref.at[0]
        _dilated_fast(o_ref, (qn, qn), (kn, kn), vn, (qr, qr), (kr, kr), (vr0, vr1), (f0, f1), (f2, f3),
                      s_scr, mask_scr, seq=seq)

    @pl.when(worst > LOGIT_MAX)
    def _general():
        _dilated_general(q_ref, k_ref, v_ref, o_ref, qr, kr, vr0, f0, f1, f2, f3, f4, seq=seq)


def _dilated(qkva, *, chunk=512):
    b, seq, _ = qkva.shape
    sub = seq // MAX_DIL
    kern = functools.partial(_dilated_kernel, seq=seq, chunk=chunk)
    return pl.pallas_call(
        kern,
        grid=(b, HEAD_PAIRS),
        in_specs=[
            pl.BlockSpec((1, seq, LANES), lambda bi, hp: (bi, 0, hp)),
            pl.BlockSpec((1, seq, LANES), lambda bi, hp: (bi, 0, HEAD_PAIRS + hp)),
            pl.BlockSpec((1, seq, LANES), lambda bi, hp: (bi, 0, 2 * HEAD_PAIRS + hp)),
        ],
        out_specs=pl.BlockSpec((1, seq, LANES), lambda bi, hp: (bi, 0, hp)),
        out_shape=jax.ShapeDtypeStruct((b, seq, WIDTH_A), BF16),
        scratch_shapes=(
            [pltpu.VMEM((seq, LANES), BF16)] * 2
            + [pltpu.VMEM((MAX_DIL, sub, LANES), BF16)] * 4
            + [pltpu.VMEM((seq, LANES), F32)] * 5
            + [pltpu.VMEM((2, BRANCH_UNROLL, 2, WIN, 2 * WIN), F32), pltpu.VMEM((4, WIN, 2 * WIN), BF16)]
        ),
        compiler_params=pltpu.CompilerParams(
            dimension_semantics=("arbitrary", "arbitrary"), vmem_limit_bytes=VMEM_LIMIT),
        name="dilated",
    )(qkva, qkva, qkva)


def _outproj_kernel(oa_ref, ob_ref, x_ref, wo_ref, g_ref, wr_ref, x1_ref, h2_ref, cls_ref):
    tm = x_ref.shape[0]
    x1 = (x_ref[...]
          + jnp.dot(oa_ref[...], wo_ref[0:WIDTH_A, :], preferred_element_type=F32)
          + jnp.dot(ob_ref[...], wo_ref[WIDTH_A:WIDTH_A + WIDTH_B, :], preferred_element_type=F32))
    x1_ref[...] = x1
    ms = jnp.mean(x1 * x1, axis=-1, keepdims=True)
    h2 = x1 * lax.rsqrt(ms + NORM_EPS) * g_ref[...]
    h2_ref[:, 0:D_MODEL] = h2
    h2_hi = h2.astype(BF16)
    h2_lo = (h2 - h2_hi.astype(F32)).astype(BF16)
    both = jnp.dot(h2_hi, wr_ref[...], preferred_element_type=F32)
    logits = (both[:, 0:LANES] + both[:, LANES:2 * LANES]
              + jnp.dot(h2_lo, wr_ref[:, 0:LANES], preferred_element_type=F32))
    lane = lax.broadcasted_iota(jnp.int32, (tm, LANES), 1)
    lane_f = lane.astype(F32)
    big = float(LANES)
    gmask = lane < N_GROUPS
    gl = jnp.where(gmask, logits, NEG_INF)
    gmax = jnp.max(gl, axis=1, keepdims=True)
    gsum = jnp.sum(jnp.where(gmask, jnp.exp(gl - gmax), 0.0), axis=1, keepdims=True)
    p_top = 1.0 / gsum
    g_star = jnp.min(jnp.where(gmask & (gl == gmax), lane_f, big), axis=1, keepdims=True)
    lo_lane = ROUTER_OFF + EXPERTS_PER_GROUP * g_star
    emask = (lane_f >= lo_lane) & (lane_f < lo_lane + EXPERTS_PER_GROUP)
    sel = jnp.where(emask, logits, NEG_INF)
    v1 = jnp.max(sel, axis=1, keepdims=True)
    i1 = jnp.min(jnp.where(emask & (sel == v1), lane_f, big), axis=1, keepdims=True)
    rest = emask & (lane_f != i1)
    sel2 = jnp.where(rest, logits, NEG_INF)
    v2 = jnp.max(sel2, axis=1, keepdims=True)
    i2 = jnp.min(jnp.where(rest & (sel2 == v2), lane_f, big), axis=1, keepdims=True)
    e2 = jnp.exp(v2 - v1)
    w1 = p_top / (1.0 + e2)
    w2 = p_top * e2 / (1.0 + e2)
    e1 = i1 - lo_lane
    e2x = i2 - lo_lane
    a = jnp.minimum(e1, e2x)
    b = jnp.maximum(e1, e2x)
    cls = g_star * N_PAIRS + a * (7.0 - a) * 0.5 + (b - a - 1.0)
    wa = jnp.where(e1 < e2x, w1, w2)
    wb = jnp.where(e1 < e2x, w2, w1)
    h2_ref[:, D_MODEL:EXT_WIDTH] = jnp.where(lane == 0, wa, jnp.where(lane == 1, wb, 0.0))
    row = lax.broadcasted_iota(jnp.int32, (tm, LANES), 0)
    spread = jnp.where(lane == (row & (LANES - 1)), cls, 0.0)
    cls_ref[0] = jnp.sum(spread.reshape(tm // LANES, LANES, LANES), axis=1).astype(jnp.int32)


def _outproj(oa, ob, x2, wo, g, wr, *, tm):
    n = x2.shape[0]
    return pl.pallas_call(
        _outproj_kernel,
        grid=(n // tm,),
        in_specs=[
            pl.BlockSpec((tm, WIDTH_A), lambda i: (i, 0)),
            pl.BlockSpec((tm, WIDTH_B), lambda i: (i, 0)),
            pl.BlockSpec((tm, D_MODEL), lambda i: (i, 0)),
            pl.BlockSpec(wo.shape, lambda i: (0, 0)),
            pl.BlockSpec((1, D_MODEL), lambda i: (0, 0)),
            pl.BlockSpec(wr.shape, lambda i: (0, 0)),
        ],
        out_specs=[
            pl.BlockSpec((tm, D_MODEL), lambda i: (i, 0)),
            pl.BlockSpec((tm, EXT_WIDTH), lambda i: (i, 0)),
            pl.BlockSpec((1, tm // LANES, LANES), lambda i: (i, 0, 0)),
        ],
        out_shape=[
            jax.ShapeDtypeStruct((n, D_MODEL), F32),
            jax.ShapeDtypeStruct((n, EXT_WIDTH), F32),
            jax.ShapeDtypeStruct((n // tm, tm // LANES, LANES), jnp.int32),
        ],
        compiler_params=pltpu.CompilerParams(dimension_semantics=("arbitrary",), vmem_limit_bytes=VMEM_LIMIT),
        name="outproj",
    )(oa, ob, x2, wo, g, wr)


def _sort_kernel(cls_ref, pos_ref, tile_ref, ends_ref, *, tile):
    cls = cls_ref[...]
    rows = cls.shape[0]
    upper = (lax.broadcasted_iota(jnp.int32, (LANES, LANES), 0)
             <= lax.broadcasted_iota(jnp.int32, (LANES, LANES), 1)).astype(BF16)
    earlier_rows = (lax.broadcasted_iota(jnp.int32, (rows, rows), 0)
                    > lax.broadcasted_iota(jnp.int32, (rows, rows), 1)).astype(BF16)
    ones = jnp.ones((LANES, LANES), BF16)
    lane8 = lax.broadcasted_iota(jnp.int32, tile_ref.shape, 1)
    tile_start = lane8.astype(F32) * tile
    off = jnp.zeros((1, 1), F32)
    pos = jnp.zeros(cls.shape, F32)
    tile_cls = jnp.zeros(tile_ref.shape, F32)
    ends = jnp.zeros(tile_ref.shape, F32)
    for c in range(N_CLASSES):
        hit = cls == c
        hot = jnp.where(hit, 1.0, 0.0).astype(BF16)
        in_row = jnp.dot(hot, upper, preferred_element_type=F32)
        row_tot = jnp.dot(hot, ones, preferred_element_type=F32)
        before = jnp.dot(earlier_rows, row_tot.astype(BF16), preferred_element_type=F32)
        count = before[rows - 1:rows, 0:1] + row_tot[rows - 1:rows, 0:1]
        pos = pos + jnp.where(hit, in_row - 1.0 + before + off, 0.0)
        off = off + jnp.ceil(count / tile) * tile
        tile_cls = tile_cls + jnp.where(tile_start >= off, 1.0, 0.0)
        ends = jnp.where(lane8 == c, off, ends)
    pos_ref[...] = pos.astype(jnp.int32)
    tile_ref[...] = tile_cls.astype(jnp.int32)
    ends_ref[...] = ends.astype(jnp.int32)


def _sort(cls2d, *, tile):
    meta = jax.ShapeDtypeStruct((8, LANES), jnp.int32)
    return pl.pallas_call(
        functools.partial(_sort_kernel, tile=tile),
        out_shape=[jax.ShapeDtypeStruct(cls2d.shape, jnp.int32), meta, meta],
        compiler_params=pltpu.CompilerParams(vmem_limit_bytes=VMEM_LIMIT),
        name="moe_sort",
    )(cls2d)


def _row_copy(src, src_row, dst, dst_row, sem, rows=1):
    return pltpu.make_async_copy(src.at[pl.ds(src_row, rows)], dst.at[pl.ds(dst_row, rows)], sem)


def _dispatch_kernel(pos_ref, ends_ref, h2_ref, xs_hbm, zero_scr, sem, *, ch, tile):
    g = pl.program_id(0)

    @pl.when(g == 0)
    def _zero_tails():
        zero_scr[...] = jnp.zeros_like(zero_scr)
        n_tiles = xs_hbm.shape[0] // tile
        min_used = n_tiles - N_CLASSES
        total = ends_ref[0, N_CLASSES - 1]
        for phase in ("start", "wait"):
            def zero_tile(row0):
                cp = pltpu.make_async_copy(zero_scr, xs_hbm.at[pl.ds(pl.multiple_of(row0, tile), tile)], sem)
                cp.start() if phase == "start" else cp.wait()

            for c in range(N_CLASSES):
                end = ends_ref[0, c]
                prev = ends_ref[0, c - 1] if c else 0
                pl.when(end > prev)(functools.partial(zero_tile, end - tile))
                pl.when((min_used + c) * tile >= total)(functools.partial(zero_tile, (min_used + c) * tile))

    for k in range(ch):
        _row_copy(h2_ref, k, xs_hbm, pos_ref[0, 0, k], sem).start(priority=k % DMA_THREADS)
    _row_copy(h2_ref, 0, xs_hbm, 0, sem, rows=ch).wait()


def _dispatch(pos3, ends, h2ext, *, n_rows, tile):
    n_chunks, _, ch = pos3.shape
    return pl.pallas_call(
        functools.partial(_dispatch_kernel, ch=ch, tile=tile),
        grid=(n_chunks,),
        in_specs=[
            pl.BlockSpec((1, 1, ch), lambda g: (g, 0, 0), memory_space=pltpu.SMEM),
            pl.BlockSpec(ends.shape, lambda g: (0, 0), memory_space=pltpu.SMEM),
            pl.BlockSpec((ch, EXT_WIDTH), lambda g: (g, 0)),
        ],
        out_specs=pl.BlockSpec(memory_space=pl.ANY),
        out_shape=jax.ShapeDtypeStruct((n_rows, EXT_WIDTH), F32),
        scratch_shapes=[pltpu.VMEM((tile, EXT_WIDTH), F32), pltpu.SemaphoreType.DMA(())],
        compiler_params=pltpu.CompilerParams(dimension_semantics=("arbitrary",), vmem_limit_bytes=VMEM_LIMIT),
        name="moe_dispatch",
    )(pos3, ends, h2ext)


def _experts_kernel(ea_ref, eb_ref, used_ref, xs_ref, wga, wua, wda, wgb, wub, wdb, ys_ref):
    del ea_ref, eb_ref

    @pl.when(pl.program_id(0) < used_ref[0])
    def _():
        x = xs_ref[:, 0:D_MODEL].astype(BF16)
        wts = xs_ref[:, D_MODEL:EXT_WIDTH]
        y = None
        for col, (wg, wu, wd) in enumerate(((wga, wua, wda), (wgb, wub, wdb))):
            gate = jnp.dot(x, wg[0], preferred_element_type=F32)
            up = jnp.dot(x, wu[0], preferred_element_type=F32)
            he = (gate / (1.0 + jnp.exp(-gate)) * up * wts[:, col:col + 1]).astype(BF16)
            t = jnp.dot(he, wd[0], preferred_element_type=F32)
            y = t if y is None else y + t
        ys_ref[...] = y

    @pl.when(pl.program_id(0) >= used_ref[0])
    def _():
        ys_ref[...] = jnp.zeros_like(ys_ref)


def _experts(ea, eb, used, xs, wg, wu, wd, *, tile):
    n_tiles = xs.shape[0] // tile
    rows = lambda j, ea, eb, used: (jnp.minimum(j, used[0] - 1), 0)
    first = lambda j, ea, eb, used: (ea[j], 0, 0)
    second = lambda j, ea, eb, used: (eb[j], 0, 0)
    up_spec = lambda m: pl.BlockSpec((1, D_MODEL, D_EXPERT), m)
    down_spec = lambda m: pl.BlockSpec((1, D_EXPERT, D_MODEL), m)
    return pl.pallas_call(
        _experts_kernel,
        grid_spec=pltpu.PrefetchScalarGridSpec(
            num_scalar_prefetch=3,
            grid=(n_tiles,),
            in_specs=[pl.BlockSpec((tile, EXT_WIDTH), rows),
                      up_spec(first), up_spec(first), down_spec(first),
                      up_spec(second), up_spec(second), down_spec(second)],
            out_specs=pl.BlockSpec((tile, D_MODEL), lambda j, ea, eb, used: (j, 0)),
        ),
        out_shape=jax.ShapeDtypeStruct((xs.shape[0], D_MODEL), F32),
        compiler_params=pltpu.CompilerParams(dimension_semantics=("arbitrary",), vmem_limit_bytes=VMEM_LIMIT),
        name="moe_experts",
    )(ea, eb, used, xs, wg, wu, wd, wg, wu, wd)


def _combine_kernel(pos_ref, next_pos_ref, x1_ref, g_ref, ys_hbm, out_ref, y_scr, sem):
    i = pl.program_id(0)
    tm = x1_ref.shape[0]
    slot = i & 1

    def gather(p_ref, s):
        for k in range(tm):
            _row_copy(ys_hbm, p_ref[0, 0, k], y_scr.at[s], k, sem.at[s]).start(priority=k % DMA_THREADS)

    @pl.when(i == 0)
    def _():
        gather(pos_ref, 0)

    @pl.when(i + 1 < pl.num_programs(0))
    def _():
        gather(next_pos_ref, 1 - slot)

    _row_copy(ys_hbm, 0, y_scr.at[slot], 0, sem.at[slot], rows=tm).wait()
    x2 = x1_ref[...] + y_scr[slot]
    ms = jnp.mean(x2 * x2, axis=-1, keepdims=True)
    out_ref[...] = x2 * lax.rsqrt(ms + NORM_EPS) * g_ref[...]


def _combine(pos3, x1, g, ys):
    n_chunks, _, tm = pos3.shape
    return pl.pallas_call(
        _combine_kernel,
        grid=(n_chunks,),
        in_specs=[
            pl.BlockSpec((1, 1, tm), lambda i: (i, 0, 0), memory_space=pltpu.SMEM),
            pl.BlockSpec((1, 1, tm), lambda i: (jnp.minimum(i + 1, n_chunks - 1), 0, 0), memory_space=pltpu.SMEM),
            pl.BlockSpec((tm, D_MODEL), lambda i: (i, 0)),
            pl.BlockSpec((1, D_MODEL), lambda i: (0, 0)),
            pl.BlockSpec(memory_space=pl.ANY),
        ],
        out_specs=pl.BlockSpec((tm, D_MODEL), lambda i: (i, 0)),
        out_shape=jax.ShapeDtypeStruct(x1.shape, F32),
        scratch_shapes=[pltpu.VMEM((2, tm, D_MODEL), F32), pltpu.SemaphoreType.DMA((2,))],
        compiler_params=pltpu.CompilerParams(dimension_semantics=("arbitrary",), vmem_limit_bytes=VMEM_LIMIT),
        name="moe_combine",
    )(pos3, pos3, x1, g, ys)


def _moe(h2ext, cls3, x1, wg, wu, wd, g, *, tile=MOE_TILE, dispatch_chunk=1024, combine_chunk=512):
    n = x1.shape[0]
    n_tiles = n // tile + N_CLASSES
    assert n_tiles <= LANES
    pos, tile_cls, ends = _sort(cls3.reshape(n // LANES, LANES), tile=tile)
    xs = _dispatch(pos.reshape(n // dispatch_chunk, 1, dispatch_chunk), ends, h2ext, n_rows=n_tiles * tile, tile=tile)
    used = ends[0, N_CLASSES - 1:N_CLASSES] // tile
    tc = tile_cls[0, :n_tiles]
    tc = jnp.where(jnp.arange(n_tiles) < used[0], tc, tc[jnp.maximum(used[0] - 1, 0)])
    pair_a = jnp.array([0, 0, 0, 1, 1, 2], jnp.int32)
    pair_b = jnp.array([1, 2, 3, 2, 3, 3], jnp.int32)
    ea = (tc // N_PAIRS) * EXPERTS_PER_GROUP + pair_a[tc % N_PAIRS]
    eb = (tc // N_PAIRS) * EXPERTS_PER_GROUP + pair_b[tc % N_PAIRS]
    ys = _experts(ea, eb, used, xs, wg, wu, wd, tile=tile)
    return _combine(pos.reshape(n // combine_chunk, 1, combine_chunk), x1, g, ys)


def _rotary_tables(seq):
    pos = np.arange(seq, dtype=np.float64)
    inv_freq = 1.0 / (ROPE_THETA ** (np.arange(0, ROT_DIM, 2, dtype=np.float64) / ROT_DIM))
    ang = pos[:, None] * inv_freq[None, :]
    cos, sin = np.cos(ang), np.sin(ang)
    zeros = np.zeros((seq, HEAD_DIM - ROT_DIM))
    zh = np.zeros((seq, ROT_HALF))
    cos_h = np.concatenate([cos, cos, np.ones((seq, HEAD_DIM - ROT_DIM))], axis=1)
    sa_h = np.concatenate([-sin, zh, zeros], axis=1)
    sb_h = np.concatenate([zh, sin, zeros], axis=1)
    tile = lambda t: jnp.asarray(np.concatenate([t, t], axis=1), F32)
    return tile(cos_h), tile(sa_h), tile(sb_h)


def kernel(x, attn_norm, w_in, b_forget, w_out, ffn_norm, w_group, w_expert, w_gate_e, w_up_e, w_down_e, final_norm):
    b, seq, d = x.shape
    assert d == D_MODEL and w_in.shape[0] == 1, "single-layer block"
    n = b * seq
    scale = HEAD_DIM ** -0.5 * LOG2E
    col_scale = np.ones((2 * QKV_WIDTH + LANES,), np.float32)
    col_scale[0:WIDTH_A] = scale
    col_scale[QKV_WIDTH:QKV_WIDTH + WIDTH_B] = scale
    w = (jnp.pad(w_in[0], ((0, 0), (0, LANES - HEADS_B))) * col_scale[None, :]).astype(BF16)
    bf = jnp.pad(b_forget[0].astype(F32), (0, LANES - HEADS_B))[None, :]
    cosv, sa, sb = _rotary_tables(seq)
    x2 = x.reshape(n, d)

    qkva, qkvb, c = _inproj(x2, attn_norm[0][None, :], w, cosv, sa, sb, bf, seq=seq, tm=512)
    out_a = _dilated(qkva.reshape(b, seq, QKV_WIDTH))
    out_b, wg, wu, wd = _fox(qkvb.reshape(b, seq, QKV_WIDTH), c.reshape(b, seq, LANES),
                             w_gate_e[0], w_up_e[0], w_down_e[0], tq=1024)

    wr = jnp.pad(jnp.concatenate([w_group[0], w_expert[0]], axis=1).astype(F32),
                 ((0, 0), (0, LANES - N_GROUPS - N_EXPERTS)))
    wr_hi = wr.astype(BF16)
    wr = jnp.concatenate([wr_hi, (wr - wr_hi.astype(F32)).astype(BF16)], axis=1)
    x1, h2ext, cls3 = _outproj(out_a.reshape(n, WIDTH_A), out_b.reshape(n, WIDTH_B), x2,
                               w_out[0].astype(BF16), ffn_norm[0][None, :], wr, tm=512)
    out = _moe(h2ext, cls3, x1, wg, wu, wd, final_norm[None, :])
    return out.reshape(b, seq, d)
```

```python
import functools

import jax
import jax.numpy as jnp
import numpy as np
from jax import lax
from jax.experimental import pallas as pl
from jax.experimental.pallas import tpu as pltpu

F32 = jnp.float32
BF16 = jnp.bfloat16

D_MODEL = 1024
HEAD_DIM = 64
HEADS_A = 8
HEADS_B = 8
WIDTH_A = HEADS_A * HEAD_DIM
WIDTH_B = HEADS_B * HEAD_DIM
QKV_WIDTH = 3 * WIDTH_A
DILATIONS = ((128, 1), (512, 4), (2048, 16))
ROT_DIM = HEAD_DIM // 4
ROT_HALF = ROT_DIM // 2
ROPE_THETA = 500000.0
N_GROUPS = 4
EXPERTS_PER_GROUP = 4
N_EXPERTS = N_GROUPS * EXPERTS_PER_GROUP
D_EXPERT = 512
NORM_EPS = 1e-6
NEG_INF = -1e30

LANES = 128
HEAD_PAIRS = WIDTH_A // LANES
WIN = 128
MAX_DIL = 16
PERM = MAX_DIL * MAX_DIL
KV_BLOCK = 256
N_PAIRS = EXPERTS_PER_GROUP * (EXPERTS_PER_GROUP - 1) // 2
N_CLASSES = N_GROUPS * N_PAIRS
PAIR_ORDER = ((0, 1), (0, 2), (1, 2), (1, 3), (0, 3), (2, 3))
assert len(PAIR_ORDER) == N_PAIRS
MOE_TILE = 256
EXT_WIDTH = D_MODEL + LANES
LOGIT_MAX = 60.0
BRANCH_UNROLL = 8
DMA_THREADS = 2
ROUTER_OFF = N_GROUPS
VMEM_LIMIT = 56 * 1024 * 1024
LOG2E = 1.4426950408889634
BOUND_SLACK = 1.0 + 2.0 ** -6
BOUND_EPS = 2.0 ** -7


def _split3(x):
    hi = x.astype(BF16)
    r1 = x - hi.astype(F32)
    mid = r1.astype(BF16)
    lo = (r1 - mid.astype(F32)).astype(BF16)
    return hi, mid, lo


def _inproj_kernel(x_ref, g_ref, w_ref, cos_ref, sa_ref, sb_ref, bf_ref,
                   qkva_ref, qkvb_ref, c_ref, h_scr, carry_scr, *, tiles_per_seq):
    i = pl.program_id(0)
    tm = x_ref.shape[0]
    x = x_ref[...]
    ms = jnp.mean(x * x, axis=-1, keepdims=True)
    h_scr[...] = (x * lax.rsqrt(ms + NORM_EPS) * g_ref[...]).astype(BF16)
    cosv = cos_ref[...]
    sa = sa_ref[...]
    sb = sb_ref[...]
    for g in range(6):
        p = jnp.dot(h_scr[...], w_ref[:, g * WIDTH_A:(g + 1) * WIDTH_A], preferred_element_type=F32)
        dst = qkva_ref if g < 3 else qkvb_ref
        col0 = (g % 3) * WIDTH_A
        if g < 2:
            for k in range(HEAD_PAIRS):
                t = p[:, k * LANES:(k + 1) * LANES]
                t = (t * cosv + pltpu.roll(t, LANES - ROT_HALF, 1) * sa + pltpu.roll(t, ROT_HALF, 1) * sb)
                dst[:, col0 + k * LANES:col0 + (k + 1) * LANES] = t.astype(BF16)
        else:
            dst[:, col0:col0 + WIDTH_A] = p.astype(BF16)
    z = jnp.dot(h_scr[...], w_ref[:, 2 * QKV_WIDTH:2 * QKV_WIDTH + LANES], preferred_element_type=F32) + bf_ref[...]
    lf = jnp.minimum(z, 0.0) - jnp.log1p(jnp.exp(-jnp.abs(z)))
    lane = lax.broadcasted_iota(jnp.int32, (tm, LANES), 1)
    lf = jnp.where(lane < HEADS_B, lf * LOG2E, 0.0)
    hi, mid, lo = _split3(lf)
    tri = (lax.broadcasted_iota(jnp.int32, (tm, tm), 0) >= lax.broadcasted_iota(jnp.int32, (tm, tm), 1)).astype(BF16)
    cs = (jnp.dot(tri, hi, preferred_element_type=F32) + jnp.dot(tri, mid, preferred_element_type=F32)
          + jnp.dot(tri, lo, preferred_element_type=F32))

    @pl.when(i % tiles_per_seq == 0)
    def _():
        carry_scr[...] = jnp.zeros_like(carry_scr)

    c = cs + carry_scr[0:1, :]
    c_ref[...] = c
    carry_scr[...] = jnp.broadcast_to(c[tm - 1:tm, :], carry_scr.shape)


def _inproj(x2, g, w, cosv, sa, sb, bf, *, seq, tm):
    n = x2.shape[0]
    kern = functools.partial(_inproj_kernel, tiles_per_seq=seq // tm)
    tps = seq // tm
    return pl.pallas_call(
        kern,
        grid=(n // tm,),
        in_specs=[
            pl.BlockSpec((tm, D_MODEL), lambda i: (i, 0)),
            pl.BlockSpec((1, D_MODEL), lambda i: (0, 0)),
            pl.BlockSpec(w.shape, lambda i: (0, 0)),
            pl.BlockSpec((tm, LANES), lambda i: (i % tps, 0)),
            pl.BlockSpec((tm, LANES), lambda i: (i % tps, 0)),
            pl.BlockSpec((tm, LANES), lambda i: (i % tps, 0)),
            pl.BlockSpec((1, LANES), lambda i: (0, 0)),
        ],
        out_specs=[
            pl.BlockSpec((tm, QKV_WIDTH), lambda i: (i, 0)),
            pl.BlockSpec((tm, QKV_WIDTH), lambda i: (i, 0)),
            pl.BlockSpec((tm, LANES), lambda i: (i, 0)),
        ],
        out_shape=[
            jax.ShapeDtypeStruct((n, QKV_WIDTH), BF16),
            jax.ShapeDtypeStruct((n, QKV_WIDTH), BF16),
            jax.ShapeDtypeStruct((n, LANES), F32),
        ],
        scratch_shapes=[pltpu.VMEM((tm, D_MODEL), BF16), pltpu.VMEM((8, LANES), F32)],
        compiler_params=pltpu.CompilerParams(dimension_semantics=("arbitrary",), vmem_limit_bytes=VMEM_LIMIT),
        name="inproj",
    )(x2, g, w, cosv, sa, sb, bf)


def _head_column(c_all, lane, head):
    return jnp.sum(jnp.where(lane == head, c_all, 0.0), axis=1, keepdims=True)


def _own_lanes(lane, hl):
    return (lane < HEAD_DIM) if hl == 0 else (lane >= HEAD_DIM)


def _head_sum_selector():
    lane = lax.broadcasted_iota(jnp.int32, (LANES, LANES), 0)
    col = lax.broadcasted_iota(jnp.int32, (LANES, LANES), 1)
    return jnp.where(((col == 0) & (lane < HEAD_DIM)) | ((col == 1) & (lane >= HEAD_DIM)), 1.0, 0.0).astype(BF16)


def _max_sq_norm(t, selector):
    return jnp.max(jnp.dot((t * t).astype(BF16), selector, preferred_element_type=F32), axis=0, keepdims=True)


def _norm_bound(qsq, ksq_max):
    return jnp.sqrt(qsq * ksq_max) * BOUND_SLACK + BOUND_EPS


def _fox_kernel(q_ref, k_ref, v_ref, c_ref, wg_ref, wu_ref, wd_ref, o_ref, wg_out, wu_out, wd_out,
                q0_scr, q1_scr, k0_scr, k1_scr, v0_scr, v1_scr, bound_scr, acc0_scr, acc1_scr, s_scr,
                *, tq, seq, chunk):
    hp = pl.program_id(1)
    qi = pl.program_id(2)
    for src, dst in ((wg_ref, wg_out), (wu_ref, wu_out), (wd_ref, wd_out)):
        dst[...] = src[...].astype(dst.dtype)
    q_scr = (q0_scr, q1_scr)
    k_scr = (k0_scr, k1_scr)
    v_scr = (v0_scr, v1_scr)
    acc_scr = (acc0_scr, acc1_scr)

    def augmented(vals, cterm, lane, hl, key_side):
        hi, mid, lo = (t.astype(F32) for t in _split3(cterm))
        base = HEAD_DIM if hl == 0 else 0
        ones = (lane >= base + (0 if key_side else 3)) & (lane < base + (3 if key_side else 6))
        o = 3 if key_side else 0
        sgn = -1.0 if key_side else 1.0
        aug = jnp.where(lane == base + o, sgn * hi,
                        jnp.where(lane == base + o + 1, sgn * mid,
                                  jnp.where(lane == base + o + 2, sgn * lo,
                                            jnp.where(ones, 1.0, 0.0))))
        return jnp.where(_own_lanes(lane, hl), vals, aug).astype(BF16)

    @pl.when(qi == 0)
    def _build():
        lane = lax.broadcasted_iota(jnp.int32, (chunk, LANES), 1)

        def body(ci, sq_max):
            rows = pl.ds(pl.multiple_of(ci * chunk, chunk), chunk)
            qc = q_ref[0, rows, :].astype(F32)
            kc = k_ref[0, rows, :].astype(F32)
            vc = v_ref[0, rows, :].astype(F32)
            cc = c_ref[0, rows, :]
            for hl in range(2):
                ccol = _head_column(cc, lane, 2 * hp + hl)
                q_scr[hl][rows, :] = augmented(qc, ccol, lane, hl, False)
                k_scr[hl][rows, :] = augmented(kc, ccol, lane, hl, True)
                one_lane = HEAD_DIM if hl == 0 else 0
                v_scr[hl][rows, :] = jnp.where(
                    _own_lanes(lane, hl), vc, jnp.where(lane == one_lane, 1.0, 0.0)).astype(BF16)
            return (jnp.maximum(sq_max[0], _max_sq_norm(qc, selector)),
                    jnp.maximum(sq_max[1], _max_sq_norm(kc, selector)))

        selector = _head_sum_selector()
        sq_max = lax.fori_loop(0, seq // chunk, body, (jnp.zeros((1, LANES), F32),) * 2)
        bound_scr[...] = jnp.broadcast_to(_norm_bound(sq_max[0], sq_max[1]), bound_scr.shape)

    lane_q = lax.broadcasted_iota(jnp.int32, (tq, LANES), 1)
    q_rows = pl.ds(pl.multiple_of(qi * tq, tq), tq)
    worst = jnp.max(bound_scr[...])
    n_diag = tq // KV_BLOCK
    n_full = qi * n_diag
    row_minus_col = (lax.broadcasted_iota(jnp.int32, (tq, KV_BLOCK), 0)
                     - lax.broadcasted_iota(jnp.int32, (tq, KV_BLOCK), 1))

    def causal_mask(s, d, top=0):
        return jnp.where(row_minus_col[top:tq, :] >= d * KV_BLOCK, s, NEG_INF)

    def logits(hl, j, top=0):
        r0 = pl.multiple_of(j * KV_BLOCK, KV_BLOCK)
        rows = q_rows if top == 0 else pl.ds(pl.multiple_of(qi * tq + top, KV_BLOCK), tq - top)
        return lax.dot_general(q_scr[hl][rows, :], k_scr[hl][pl.ds(r0, KV_BLOCK), :], (((1,), (1,)), ((), ())),
                               preferred_element_type=F32)

    def values(hl, j):
        return v_scr[hl][pl.ds(pl.multiple_of(j * KV_BLOCK, KV_BLOCK), KV_BLOCK), :]

    def finish(a0, a1):
        l0 = a0[:, HEAD_DIM:HEAD_DIM + 1]
        l1 = a1[:, 0:1]
        o_ref[0] = jnp.where(lane_q < HEAD_DIM, a0 / l0, a1 / l1).astype(o_ref.dtype)

    @pl.when(worst <= LOGIT_MAX)
    def _fast():
        acc0_scr[...] = jnp.zeros_like(acc0_scr)
        acc1_scr[...] = jnp.zeros_like(acc1_scr)

        def produce(j, slot, top=0):
            for hl in range(2):
                s_scr[slot, hl, top:tq, :] = logits(hl, j, top)

        def consume(j, slot, diag):
            top = 0 if diag is None else diag * KV_BLOCK
            for hl in range(2):
                s = s_scr[slot, hl, top:tq, :]
                if diag is not None:
                    s = causal_mask(s, diag, top)
                acc_scr[hl][top:tq, :] += jnp.dot(jnp.exp2(s).astype(BF16), values(hl, j),
                                                  preferred_element_type=F32)

        def produce_diagonal(d):
            produce(n_full + d, d, d * KV_BLOCK)

        @pl.when(qi == 0)
        def _():
            for d in range(n_diag):
                produce_diagonal(d)

        @pl.when(qi > 0)
        def _():
            for d in range(n_diag):
                produce(d, d)

            def body(i, carry):
                for d in range(n_diag):
                    j = i * n_diag + d
                    consume(j, d, None)
                    produce(j + n_diag, d)
                return carry

            lax.fori_loop(0, qi - 1, body, 0)
            for d in range(n_diag):
                consume(n_full - n_diag + d, d, None)
                produce_diagonal(d)

        for d in range(n_diag):
            consume(n_full + d, d, d)
        finish(acc0_scr[...], acc1_scr[...])

    @pl.when(worst > LOGIT_MAX)
    def _general():
        def step(j, carry, diag):
            new = []
            for hl in range(2):
                m, a = carry[2 * hl:2 * hl + 2]
                s = logits(hl, j)
                if diag is not None:
                    s = causal_mask(s, diag)
                mn = jnp.maximum(m, jnp.max(s, axis=1, keepdims=True))
                a = jnp.exp2(m - mn) * a + jnp.dot(jnp.exp2(s - mn).astype(BF16), values(hl, j),
                                                  preferred_element_type=F32)
                new += [mn, a]
            return tuple(new)

        carry = (jnp.full((tq, 1), NEG_INF, F32), jnp.zeros((tq, LANES), F32)) * 2
        carry = lax.fori_loop(0, n_full, lambda j, c: step(j, c, None), carry)
        for d in range(n_diag):
            carry = step(n_full + d, carry, d)
        finish(carry[1], carry[3])


def _fox(qkvb, c, wg, wu, wd, *, tq, chunk=512):
    b, seq, _ = qkvb.shape
    nq = seq // tq
    steps = b * HEAD_PAIRS * nq
    parts = steps // N_EXPERTS
    assert parts * N_EXPERTS == steps and D_MODEL % parts == 0 and D_EXPERT % parts == 0

    def slice_spec(rows, cols):
        def index(bi, hp, qi):
            step = (bi * HEAD_PAIRS + hp) * nq + qi
            return (step // parts, step % parts, 0)
        return pl.BlockSpec((1, rows // parts, cols), index)

    w_specs = [slice_spec(D_MODEL, D_EXPERT), slice_spec(D_MODEL, D_EXPERT), slice_spec(D_EXPERT, D_MODEL)]
    kern = functools.partial(_fox_kernel, tq=tq, seq=seq, chunk=chunk)
    return pl.pallas_call(
        kern,
        grid=(b, HEAD_PAIRS, nq),
        in_specs=[
            pl.BlockSpec((1, seq, LANES), lambda bi, hp, qi: (bi, 0, hp)),
            pl.BlockSpec((1, seq, LANES), lambda bi, hp, qi: (bi, 0, HEAD_PAIRS + hp)),
            pl.BlockSpec((1, seq, LANES), lambda bi, hp, qi: (bi, 0, 2 * HEAD_PAIRS + hp)),
            pl.BlockSpec((1, seq, LANES), lambda bi, hp, qi: (bi, 0, 0)),
        ] + w_specs,
        out_specs=[pl.BlockSpec((1, tq, LANES), lambda bi, hp, qi: (bi, qi, hp))] + w_specs,
        out_shape=[jax.ShapeDtypeStruct((b, seq, WIDTH_B), BF16)]
        + [jax.ShapeDtypeStruct(w.shape, BF16) for w in (wg, wu, wd)],
        scratch_shapes=[pltpu.VMEM((seq, LANES), BF16)] * 6 + [
                        pltpu.VMEM((8, LANES), F32),
                        pltpu.VMEM((tq, LANES), F32), pltpu.VMEM((tq, LANES), F32),
                        pltpu.VMEM((tq // KV_BLOCK, 2, tq, KV_BLOCK), F32)],
        compiler_params=pltpu.CompilerParams(
            dimension_semantics=("arbitrary", "arbitrary", "arbitrary"), vmem_limit_bytes=VMEM_LIMIT),
        name="fox",
    )(qkvb, qkvb, qkvb, c, wg, wu, wd)


def _dilated_general(q_ref, k_ref, v_ref, o_ref, q16, k16, v16, m_s, l_s, acc_s, on_s, lse_s, *, seq):
    sub = seq // MAX_DIL
    own0 = lax.broadcasted_iota(jnp.int32, (WIN, LANES), 1) < HEAD_DIM
    ri = lax.broadcasted_iota(jnp.int32, (PERM, PERM), 0)
    ci = lax.broadcasted_iota(jnp.int32, (PERM, PERM), 1)
    perm = (((ri >> 4) == (ci & 15)) & ((ri & 15) == (ci >> 4))).astype(BF16)

    qa = lax.broadcasted_iota(jnp.int32, (WIN, 2 * WIN), 0)
    kb_i = lax.broadcasted_iota(jnp.int32, (WIN, 2 * WIN), 1)
    cur = kb_i >= WIN
    kin = kb_i & (WIN - 1)
    dist_nat = qa - kin + jnp.where(cur, 0, WIN)
    band_nat = (dist_nat >= 0) & (dist_nat <= WIN)
    sq = ((qa & 31) << 2) + (qa >> 5)
    sk = ((kin & 31) << 2) + (kin >> 5)
    dist_4 = sq - sk + jnp.where(cur, 0, WIN)
    band_4 = (dist_4 >= 0) & (dist_4 <= WIN)

    def attend(qb, kb, vb, mask, state):
        qf = qb.astype(F32)
        new = []
        for hl in range(2):
            m, l, a = state[3 * hl:3 * hl + 3]
            own = own0 if hl == 0 else jnp.logical_not(own0)
            qh = jnp.where(own, qf, 0.0).astype(BF16)
            s = lax.dot_general(qh, kb, (((1,), (1,)), ((), ())), preferred_element_type=F32)
            s = jnp.where(mask, s, NEG_INF)
            mn = jnp.maximum(m, jnp.max(s, axis=1, keepdims=True))
            alpha = jnp.exp2(m - mn)
            p = jnp.exp2(s - mn)
            l = alpha * l + jnp.sum(p, axis=1, keepdims=True)
            a = alpha * a + jnp.dot(p.astype(BF16), vb, preferred_element_type=F32)
            new += [mn, l, a]
        return new

    def pack_state(st):
        m0, l0, a0, m1, l1, a1 = st
        return (jnp.where(own0, m0, m1), jnp.where(own0, l0, l1), jnp.where(own0, a0, a1))

    def deint(blk, carry):
        r0 = pl.multiple_of(blk * PERM, PERM)
        j0 = pl.multiple_of(blk * MAX_DIL, MAX_DIL)
        for src, dst in ((q_ref, q16), (k_ref, k16), (v_ref, v16)):
            y = jnp.dot(perm, src[0, pl.ds(r0, PERM), :], preferred_element_type=F32).astype(BF16)
            for r in range(MAX_DIL):
                dst[r, pl.ds(j0, MAX_DIL), :] = y[r * MAX_DIL:(r + 1) * MAX_DIL, :]
        return carry

    lax.fori_loop(0, seq // PERM, deint, 0)

    fresh = [jnp.full((WIN, 1), NEG_INF, F32), jnp.zeros((WIN, 1), F32), jnp.zeros((WIN, LANES), F32)] * 2
    nb16 = sub // WIN

    def d16_body(t, carry):
        r = t // nb16
        n = t % nb16
        c0 = pl.multiple_of(n * WIN, WIN)
        p0 = pl.multiple_of(jnp.maximum(n - 1, 0) * WIN, WIN)
        qb = q16[r, pl.ds(c0, WIN), :]
        kb = jnp.concatenate([k16[r, pl.ds(p0, WIN), :], k16[r, pl.ds(c0, WIN), :]], axis=0)
        vb = jnp.concatenate([v16[r, pl.ds(p0, WIN), :], v16[r, pl.ds(c0, WIN), :]], axis=0)
        mask = band_nat & (cur | (n > 0))
        mm, ll, aa = pack_state(attend(qb, kb, vb, mask, fresh))
        row = pl.multiple_of(r * sub + c0, WIN)
        m_s[pl.ds(row, WIN), :] = mm
        l_s[pl.ds(row, WIN), :] = ll
        acc_s[pl.ds(row, WIN), :] = aa
        return carry

    lax.fori_loop(0, MAX_DIL * nb16, d16_body, 0)

    nb4 = (seq // 4) // WIN
    ch = WIN // 4

    def d4_body(t, carry):
        r4 = t // nb4
        n = t % nb4
        c0 = pl.multiple_of(n * ch, ch)
        p0 = pl.multiple_of(jnp.maximum(n - 1, 0) * ch, ch)

        def gather(ref, j0):
            return [ref[r4 + 4 * q, pl.ds(j0, ch), :] for q in range(4)]

        qb = jnp.concatenate(gather(q16, c0), axis=0)
        kb = jnp.concatenate(gather(k16, p0) + gather(k16, c0), axis=0)
        vb = jnp.concatenate(gather(v16, p0) + gather(v16, c0), axis=0)
        rows = [pl.multiple_of((r4 + 4 * q) * sub + c0, ch) for q in range(4)]
        mm = jnp.concatenate([m_s[pl.ds(rw, ch), :] for rw in rows], axis=0)
        ll = jnp.concatenate([l_s[pl.ds(rw, ch), :] for rw in rows], axis=0)
        aa = jnp.concatenate([acc_s[pl.ds(rw, ch), :] for rw in rows], axis=0)
        state = [mm[:, 0:1], ll[:, 0:1], aa, mm[:, HEAD_DIM:HEAD_DIM + 1], ll[:, HEAD_DIM:HEAD_DIM + 1], aa]
        mask = band_4 & (cur | (n > 0))
        mm, ll, aa = pack_state(attend(qb, kb, vb, mask, state))
        for q, rw in enumerate(rows):
            m_s[pl.ds(rw, ch), :] = mm[q * ch:(q + 1) * ch, :]
            l_s[pl.ds(rw, ch), :] = ll[q * ch:(q + 1) * ch, :]
            acc_s[pl.ds(rw, ch), :] = aa[q * ch:(q + 1) * ch, :]
        return carry

    lax.fori_loop(0, 4 * nb4, d4_body, 0)

    def renat(blk, carry):
        j0 = pl.multiple_of(blk * MAX_DIL, MAX_DIL)
        r0 = pl.multiple_of(blk * PERM, PERM)
        rows = [pl.multiple_of(r * sub + j0, MAX_DIL) for r in range(MAX_DIL)]
        mm = jnp.concatenate([m_s[pl.ds(rw, MAX_DIL), :] for rw in rows], axis=0)
        ll = jnp.concatenate([l_s[pl.ds(rw, MAX_DIL), :] for rw in rows], axis=0)
        aa = jnp.concatenate([acc_s[pl.ds(rw, MAX_DIL), :] for rw in rows], axis=0)
        o = (aa / ll).astype(BF16)
        hi, mid, lo = _split3(mm + jnp.log2(ll))
        on_s[pl.ds(r0, PERM), :] = jnp.dot(perm, o, preferred_element_type=F32)
        lse_s[pl.ds(r0, PERM), :] = (jnp.dot(perm, hi, preferred_element_type=F32)
                                     + jnp.dot(perm, mid, preferred_element_type=F32)
                                     + jnp.dot(perm, lo, preferred_element_type=F32))
        return carry

    lax.fori_loop(0, seq // PERM, renat, 0)

    def d1_body(n, carry):
        c0 = pl.multiple_of(n * WIN, WIN)
        p0 = pl.multiple_of(jnp.maximum(n - 1, 0) * WIN, WIN)
        qb = q_ref[0, pl.ds(c0, WIN), :]
        kb = jnp.concatenate([k_ref[0, pl.ds(p0, WIN), :], k_ref[0, pl.ds(c0, WIN), :]], axis=0)
        vb = jnp.concatenate([v_ref[0, pl.ds(p0, WIN), :], v_ref[0, pl.ds(c0, WIN), :]], axis=0)
        lse = lse_s[pl.ds(c0, WIN), :]
        o = on_s[pl.ds(c0, WIN), :]
        one = jnp.ones((WIN, 1), F32)
        state = [lse[:, 0:1], one, o, lse[:, HEAD_DIM:HEAD_DIM + 1], one, o]
        mask = band_nat & (cur | (n > 0))
        m0, l0, a0, m1, l1, a1 = attend(qb, kb, vb, mask, state)
        o_ref[0, pl.ds(c0, WIN), :] = jnp.where(own0, a0 / l0, a1 / l1).astype(o_ref.dtype)
        return carry

    lax.fori_loop(0, seq // WIN, d1_body, 0)


def _dilated_fast(o_ref, qn, kn, vn, qr, kr, vr, fr, fn, s_scr, mask_scr, *, seq):
    sub = seq // MAX_DIL
    own0 = lax.broadcasted_iota(jnp.int32, (WIN, LANES), 1) < HEAD_DIM
    head_mask = [jnp.where(own0, 1.0, 0.0).astype(BF16), jnp.where(own0, 0.0, 1.0).astype(BF16)]
    ri = lax.broadcasted_iota(jnp.int32, (PERM, PERM), 0)
    ci = lax.broadcasted_iota(jnp.int32, (PERM, PERM), 1)
    perm = (((ri >> 4) == (ci & 15)) & ((ri & 15) == (ci >> 4))).astype(BF16)

    qa = lax.broadcasted_iota(jnp.int32, (WIN, 2 * WIN), 0)
    kb_i = lax.broadcasted_iota(jnp.int32, (WIN, 2 * WIN), 1)
    cur = kb_i >= WIN
    kin = kb_i & (WIN - 1)
    dist_nat = qa - kin + jnp.where(cur, 0, WIN)
    band_nat = (dist_nat >= 0) & (dist_nat <= WIN)
    sq = ((qa & 31) << 2) + (qa >> 5)
    sk = ((kin & 31) << 2) + (kin >> 5)
    dist_4 = sq - sk + jnp.where(cur, 0, WIN)
    band_4 = (dist_4 >= 0) & (dist_4 <= WIN)
    for i, band in enumerate((band_nat, band_4)):
        mask_scr[2 * i] = jnp.where(band & cur, 1.0, 0.0).astype(BF16)
        mask_scr[2 * i + 1] = jnp.where(band, 1.0, 0.0).astype(BF16)

    def run_branch(nblocks, block_in_seq, fetch, mask_base, sink, before_trip=None):
        def produce(t, pair, u):
            for hl in range(2):
                qb, kb = fetch(t, hl, True)
                s_scr[pair, u, hl] = lax.dot_general(qb * head_mask[hl], kb, (((1,), (1,)), ((), ())),
                                                     preferred_element_type=F32)

        def consume(t, pair, u):
            mk = mask_scr[mask_base + jnp.minimum(block_in_seq(t), 1)]
            pv = []
            for hl in range(2):
                p = jnp.exp2(s_scr[pair, u, hl]).astype(BF16) * mk
                pv.append(jnp.dot(p, fetch(t, hl, False), preferred_element_type=F32))
            sink(t, pv)

        for u in range(BRANCH_UNROLL):
            produce(u, 0, u)

        def body(i, carry):
            pair = i & 1
            if before_trip is not None:
                before_trip(i)
            for u in range(BRANCH_UNROLL):
                consume(BRANCH_UNROLL * i + u, pair, u)
            for u in range(BRANCH_UNROLL):
                produce(jnp.minimum(BRANCH_UNROLL * (i + 1) + u, nblocks - 1), 1 - pair, u)
            return carry

        lax.fori_loop(0, nblocks // BRANCH_UNROLL, body, 0)

    def prev_cur(ref_block, n, size):
        c0 = pl.multiple_of(n * size, size)
        p0 = pl.multiple_of(jnp.maximum(n - 1, 0) * size, size)
        return ref_block(p0) + ref_block(c0)

    nb16 = sub // WIN

    def fetch16(t, hl, qk):
        r, n = t // nb16, t % nb16
        if qk:
            return (qr[hl][r, pl.ds(pl.multiple_of(n * WIN, WIN), WIN), :],
                    jnp.concatenate(prev_cur(lambda s: [kr[hl][r, pl.ds(s, WIN), :]], n, WIN), axis=0))
        return jnp.concatenate(prev_cur(lambda s: [vr[hl][r, pl.ds(s, WIN), :]], n, WIN), axis=0)

    def sink16(t, pv):
        row = pl.multiple_of(t * WIN, WIN)
        for hl in range(2):
            fr[hl][pl.ds(row, WIN), :] = pv[hl]

    run_branch(MAX_DIL * nb16, lambda t: t % nb16, fetch16, 0, sink16)

    nb4 = (seq // 4) // WIN
    ch = WIN // 4

    def fetch4(t, hl, qk):
        r4, n = t // nb4, t % nb4
        gather = lambda ref: (lambda s: [ref[r4 + 4 * q, pl.ds(s, ch), :] for q in range(4)])
        if qk:
            return (jnp.concatenate(gather(qr[hl])(pl.multiple_of(n * ch, ch)), axis=0),
                    jnp.concatenate(prev_cur(gather(kr[hl]), n, ch), axis=0))
        return jnp.concatenate(prev_cur(gather(vr[hl]), n, ch), axis=0)

    def sink4(t, pv):
        r4, n = t // nb4, t % nb4
        for q in range(4):
            row = pl.multiple_of((r4 + 4 * q) * sub + n * ch, ch)
            for hl in range(2):
                fr[hl][pl.ds(row, ch), :] += pv[hl][q * ch:(q + 1) * ch, :]

    run_branch(4 * nb4, lambda t: t % nb4, fetch4, 2, sink4)

    def renat(i):
        for u in range(BRANCH_UNROLL * WIN // PERM):
            blk = i * (BRANCH_UNROLL * WIN // PERM) + u
            j0 = pl.multiple_of(blk * MAX_DIL, MAX_DIL)
            r0 = pl.multiple_of(blk * PERM, PERM)
            for hl in range(2):
                a = jnp.concatenate([fr[hl][pl.ds(pl.multiple_of(r * sub + j0, MAX_DIL), MAX_DIL), :]
                                     for r in range(MAX_DIL)], axis=0)
                hi = a.astype(BF16)
                lo = (a - hi.astype(F32)).astype(BF16)
                y = jnp.dot(perm, jnp.concatenate([hi, lo], axis=1), preferred_element_type=F32)
                fn[hl][pl.ds(r0, PERM), :] = y[:, 0:LANES] + y[:, LANES:2 * LANES]

    def fetch1(t, hl, qk):
        if qk:
            return (qn[hl][pl.ds(pl.multiple_of(t * WIN, WIN), WIN), :],
                    jnp.concatenate(prev_cur(lambda s: [kn[hl][pl.ds(s, WIN), :]], t, WIN), axis=0))
        return jnp.concatenate(prev_cur(lambda s: [vn[hl][pl.ds(s, WIN), :]], t, WIN), axis=0)

    def sink1(t, pv):
        row = pl.multiple_of(t * WIN, WIN)
        t0 = fn[0][pl.ds(row, WIN), :] + pv[0]
        t1 = fn[1][pl.ds(row, WIN), :] + pv[1]
        o_ref[0, pl.ds(row, WIN), :] = jnp.where(own0, t0 / t0[:, HEAD_DIM:HEAD_DIM + 1],
                                                 t1 / t1[:, 0:1]).astype(o_ref.dtype)

    run_branch(seq // WIN, lambda t: t, fetch1, 0, sink1, before_trip=renat)


def _dilated_kernel(q_ref, k_ref, v_ref, o_ref, vn0, vn1, qr, kr, vr0, vr1,
                    f0, f1, f2, f3, f4, s_scr, mask_scr, *, seq, chunk):
    lane = lax.broadcasted_iota(jnp.int32, (chunk, LANES), 1)
    vn = (vn0, vn1)

    ri = lax.broadcasted_iota(jnp.int32, (PERM, PERM), 0)
    ci_ = lax.broadcasted_iota(jnp.int32, (PERM, PERM), 1)
    perm = (((ri >> 4) == (ci_ & 15)) & ((ri & 15) == (ci_ >> 4))).astype(BF16)

    def stats(ci, sq_max):
        rows = pl.ds(pl.multiple_of(ci * chunk, chunk), chunk)
        qb = q_ref[0, rows, :]
        kb = k_ref[0, rows, :]
        vc = v_ref[0, rows, :].astype(F32)
        vb = []
        for hl in range(2):
            ones_lane = HEAD_DIM if hl == 0 else 0
            vb.append(jnp.where(_own_lanes(lane, hl), vc, jnp.where(lane == ones_lane, 1.0, 0.0)).astype(BF16))
            vn[hl][rows, :] = vb[hl]
        for u in range(chunk // PERM):
            j0 = pl.multiple_of((ci * (chunk // PERM) + u) * MAX_DIL, MAX_DIL)
            for srcs, dsts in (((qb, kb), (qr, kr)), (vb, (vr0, vr1))):
                both = jnp.concatenate([src[u * PERM:(u + 1) * PERM, :] for src in srcs], axis=1)
                y = jnp.dot(perm, both, preferred_element_type=F32).astype(BF16)
                for r in range(MAX_DIL):
                    for half, dst in enumerate(dsts):
                        dst[r, pl.ds(j0, MAX_DIL), :] = y[r * MAX_DIL:(r + 1) * MAX_DIL, half * LANES:(half + 1) * LANES]
        return (jnp.maximum(sq_max[0], _max_sq_norm(qb.astype(F32), selector)),
                jnp.maximum(sq_max[1], _max_sq_norm(kb.astype(F32), selector)))

    selector = _head_sum_selector()
    sq_max = lax.fori_loop(0, seq // chunk, stats, (jnp.zeros((1, LANES), F32),) * 2)
    worst = jnp.max(_norm_bound(sq_max[0], sq_max[1]))

    @pl.when(worst <= LOGIT_MAX)
    def _fast():
        qn, kn = q_ref.at[0], k_ref.at[0]
        _dilated_fast(o_ref, (qn, qn), (kn, kn), vn, (qr, qr), (kr, kr), (vr0, vr1), (f0, f1), (f2, f3),
                      s_scr, mask_scr, seq=seq)

    @pl.when(worst > LOGIT_MAX)
    def _general():
        _dilated_general(q_ref, k_ref, v_ref, o_ref, qr, kr, vr0, f0, f1, f2, f3, f4, seq=seq)


def _dilated(qkva, *, chunk=512):
    b, seq, _ = qkva.shape
    sub = seq // MAX_DIL
    kern = functools.partial(_dilated_kernel, seq=seq, chunk=chunk)
    return pl.pallas_call(
        kern,
        grid=(b, HEAD_PAIRS),
        in_specs=[
            pl.BlockSpec((1, seq, LANES), lambda bi, hp: (bi, 0, hp)),
            pl.BlockSpec((1, seq, LANES), lambda bi, hp: (bi, 0, HEAD_PAIRS + hp)),
            pl.BlockSpec((1, seq, LANES), lambda bi, hp: (bi, 0, 2 * HEAD_PAIRS + hp)),
        ],
        out_specs=pl.BlockSpec((1, seq, LANES), lambda bi, hp: (bi, 0, hp)),
        out_shape=jax.ShapeDtypeStruct((b, seq, WIDTH_A), BF16),
        scratch_shapes=(
            [pltpu.VMEM((seq, LANES), BF16)] * 2
            + [pltpu.VMEM((MAX_DIL, sub, LANES), BF16)] * 4
            + [pltpu.VMEM((seq, LANES), F32)] * 5
            + [pltpu.VMEM((2, BRANCH_UNROLL, 2, WIN, 2 * WIN), F32), pltpu.VMEM((4, WIN, 2 * WIN), BF16)]
        ),
        compiler_params=pltpu.CompilerParams(
            dimension_semantics=("arbitrary", "arbitrary"), vmem_limit_bytes=VMEM_LIMIT),
        name="dilated",
    )(qkva, qkva, qkva)


def _outproj_kernel(oa_ref, ob_ref, x_ref, wo_ref, g_ref, wr_ref, x1_ref, h2_ref, cls_ref):
    tm = x_ref.shape[0]
    x1 = (x_ref[...]
          + jnp.dot(oa_ref[...], wo_ref[0:WIDTH_A, :], preferred_element_type=F32)
          + jnp.dot(ob_ref[...], wo_ref[WIDTH_A:WIDTH_A + WIDTH_B, :], preferred_element_type=F32))
    x1_ref[...] = x1
    ms = jnp.mean(x1 * x1, axis=-1, keepdims=True)
    h2 = x1 * lax.rsqrt(ms + NORM_EPS) * g_ref[...]
    h2_ref[:, 0:D_MODEL] = h2
    h2_hi = h2.astype(BF16)
    h2_lo = (h2 - h2_hi.astype(F32)).astype(BF16)
    both = jnp.dot(h2_hi, wr_ref[...], preferred_element_type=F32)
    logits = (both[:, 0:LANES] + both[:, LANES:2 * LANES]
              + jnp.dot(h2_lo, wr_ref[:, 0:LANES], preferred_element_type=F32))
    lane = lax.broadcasted_iota(jnp.int32, (tm, LANES), 1)
    lane_f = lane.astype(F32)
    big = float(LANES)
    gmask = lane < N_GROUPS
    gl = jnp.where(gmask, logits, NEG_INF)
    gmax = jnp.max(gl, axis=1, keepdims=True)
    gsum = jnp.sum(jnp.where(gmask, jnp.exp(gl - gmax), 0.0), axis=1, keepdims=True)
    p_top = 1.0 / gsum
    g_star = jnp.min(jnp.where(gmask & (gl == gmax), lane_f, big), axis=1, keepdims=True)
    lo_lane = ROUTER_OFF + EXPERTS_PER_GROUP * g_star
    emask = (lane_f >= lo_lane) & (lane_f < lo_lane + EXPERTS_PER_GROUP)
    sel = jnp.where(emask, logits, NEG_INF)
    v1 = jnp.max(sel, axis=1, keepdims=True)
    i1 = jnp.min(jnp.where(emask & (sel == v1), lane_f, big), axis=1, keepdims=True)
    rest = emask & (lane_f != i1)
    sel2 = jnp.where(rest, logits, NEG_INF)
    v2 = jnp.max(sel2, axis=1, keepdims=True)
    i2 = jnp.min(jnp.where(rest & (sel2 == v2), lane_f, big), axis=1, keepdims=True)
    e2 = jnp.exp(v2 - v1)
    w1 = p_top / (1.0 + e2)
    w2 = p_top * e2 / (1.0 + e2)
    e1 = i1 - lo_lane
    e2x = i2 - lo_lane
    a = jnp.minimum(e1, e2x)
    b = jnp.maximum(e1, e2x)
    pair = jnp.zeros_like(a)
    for idx, (pa, pb) in enumerate(PAIR_ORDER):
        pair = jnp.where((a == pa) & (b == pb), float(idx), pair)
    cls = g_star * N_PAIRS + pair
    wa = jnp.where(e1 < e2x, w1, w2)
    wb = jnp.where(e1 < e2x, w2, w1)
    h2_ref[:, D_MODEL:EXT_WIDTH] = jnp.where(lane == 0, wa, jnp.where(lane == 1, wb, 0.0))
    row = lax.broadcasted_iota(jnp.int32, (tm, LANES), 0)
    spread = jnp.where(lane == (row & (LANES - 1)), cls, 0.0)
    cls_ref[0] = jnp.sum(spread.reshape(tm // LANES, LANES, LANES), axis=1).astype(jnp.int32)


def _outproj(oa, ob, x2, wo, g, wr, *, tm):
    n = x2.shape[0]
    return pl.pallas_call(
        _outproj_kernel,
        grid=(n // tm,),
        in_specs=[
            pl.BlockSpec((tm, WIDTH_A), lambda i: (i, 0)),
            pl.BlockSpec((tm, WIDTH_B), lambda i: (i, 0)),
            pl.BlockSpec((tm, D_MODEL), lambda i: (i, 0)),
            pl.BlockSpec(wo.shape, lambda i: (0, 0)),
            pl.BlockSpec((1, D_MODEL), lambda i: (0, 0)),
            pl.BlockSpec(wr.shape, lambda i: (0, 0)),
        ],
        out_specs=[
            pl.BlockSpec((tm, D_MODEL), lambda i: (i, 0)),
            pl.BlockSpec((tm, EXT_WIDTH), lambda i: (i, 0)),
            pl.BlockSpec((1, tm // LANES, LANES), lambda i: (i, 0, 0)),
        ],
        out_shape=[
            jax.ShapeDtypeStruct((n, D_MODEL), F32),
            jax.ShapeDtypeStruct((n, EXT_WIDTH), F32),
            jax.ShapeDtypeStruct((n // tm, tm // LANES, LANES), jnp.int32),
        ],
        compiler_params=pltpu.CompilerParams(dimension_semantics=("arbitrary",), vmem_limit_bytes=VMEM_LIMIT),
        name="outproj",
    )(oa, ob, x2, wo, g, wr)


def _sort_kernel(cls_ref, pos_ref, tile_ref, ends_ref, *, tile):
    cls = cls_ref[...]
    rows = cls.shape[0]
    upper = (lax.broadcasted_iota(jnp.int32, (LANES, LANES), 0)
             <= lax.broadcasted_iota(jnp.int32, (LANES, LANES), 1)).astype(BF16)
    earlier_rows = (lax.broadcasted_iota(jnp.int32, (rows, rows), 0)
                    > lax.broadcasted_iota(jnp.int32, (rows, rows), 1)).astype(BF16)
    ones = jnp.ones((LANES, LANES), BF16)
    lane8 = lax.broadcasted_iota(jnp.int32, tile_ref.shape, 1)
    tile_start = lane8.astype(F32) * tile
    off = jnp.zeros((1, 1), F32)
    pos = jnp.zeros(cls.shape, F32)
    tile_cls = jnp.zeros(tile_ref.shape, F32)
    ends = jnp.zeros(tile_ref.shape, F32)
    for c in range(N_CLASSES):
        hit = cls == c
        hot = jnp.where(hit, 1.0, 0.0).astype(BF16)
        in_row = jnp.dot(hot, upper, preferred_element_type=F32)
        row_tot = jnp.dot(hot, ones, preferred_element_type=F32)
        before = jnp.dot(earlier_rows, row_tot.astype(BF16), preferred_element_type=F32)
        count = before[rows - 1:rows, 0:1] + row_tot[rows - 1:rows, 0:1]
        pos = pos + jnp.where(hit, in_row - 1.0 + before + off, 0.0)
        off = off + jnp.ceil(count / tile) * tile
        tile_cls = tile_cls + jnp.where(tile_start >= off, 1.0, 0.0)
        ends = jnp.where(lane8 == c, off, ends)
    pos_ref[...] = pos.astype(jnp.int32)
    tile_ref[...] = tile_cls.astype(jnp.int32)
    ends_ref[...] = ends.astype(jnp.int32)


def _sort(cls2d, *, tile):
    meta = jax.ShapeDtypeStruct((8, LANES), jnp.int32)
    return pl.pallas_call(
        functools.partial(_sort_kernel, tile=tile),
        out_shape=[jax.ShapeDtypeStruct(cls2d.shape, jnp.int32), meta, meta],
        compiler_params=pltpu.CompilerParams(vmem_limit_bytes=VMEM_LIMIT),
        name="moe_sort",
    )(cls2d)


def _row_copy(src, src_row, dst, dst_row, sem, rows=1):
    return pltpu.make_async_copy(src.at[pl.ds(src_row, rows)], dst.at[pl.ds(dst_row, rows)], sem)


def _dispatch_kernel(pos_ref, ends_ref, h2_ref, xs_hbm, zero_scr, sem, *, ch, tile):
    g = pl.program_id(0)

    @pl.when(g == 0)
    def _zero_tails():
        zero_scr[...] = jnp.zeros_like(zero_scr)
        n_tiles = xs_hbm.shape[0] // tile
        min_used = n_tiles - N_CLASSES
        total = ends_ref[0, N_CLASSES - 1]
        for phase in ("start", "wait"):
            def zero_tile(row0):
                cp = pltpu.make_async_copy(zero_scr, xs_hbm.at[pl.ds(pl.multiple_of(row0, tile), tile)], sem)
                cp.start() if phase == "start" else cp.wait()

            for c in range(N_CLASSES):
                end = ends_ref[0, c]
                prev = ends_ref[0, c - 1] if c else 0
                pl.when(end > prev)(functools.partial(zero_tile, end - tile))
                pl.when((min_used + c) * tile >= total)(functools.partial(zero_tile, (min_used + c) * tile))

    for k in range(ch):
        _row_copy(h2_ref, k, xs_hbm, pos_ref[0, 0, k], sem).start(priority=k % DMA_THREADS)
    _row_copy(h2_ref, 0, xs_hbm, 0, sem, rows=ch).wait()


def _dispatch(pos3, ends, h2ext, *, n_rows, tile):
    n_chunks, _, ch = pos3.shape
    return pl.pallas_call(
        functools.partial(_dispatch_kernel, ch=ch, tile=tile),
        grid=(n_chunks,),
        in_specs=[
            pl.BlockSpec((1, 1, ch), lambda g: (g, 0, 0), memory_space=pltpu.SMEM),
            pl.BlockSpec(ends.shape, lambda g: (0, 0), memory_space=pltpu.SMEM),
            pl.BlockSpec((ch, EXT_WIDTH), lambda g: (g, 0)),
        ],
        out_specs=pl.BlockSpec(memory_space=pl.ANY),
        out_shape=jax.ShapeDtypeStruct((n_rows, EXT_WIDTH), F32),
        scratch_shapes=[pltpu.VMEM((tile, EXT_WIDTH), F32), pltpu.SemaphoreType.DMA(())],
        compiler_params=pltpu.CompilerParams(dimension_semantics=("arbitrary",), vmem_limit_bytes=VMEM_LIMIT),
        name="moe_dispatch",
    )(pos3, ends, h2ext)


def _experts_kernel(ea_ref, eb_ref, used_ref, xs_ref, wga, wua, wda, wgb, wub, wdb, ys_ref):
    del ea_ref, eb_ref

    @pl.when(pl.program_id(0) < used_ref[0])
    def _():
        x = xs_ref[:, 0:D_MODEL].astype(BF16)
        wts = xs_ref[:, D_MODEL:EXT_WIDTH]
        y = None
        for col, (wg, wu, wd) in enumerate(((wga, wua, wda), (wgb, wub, wdb))):
            gate = jnp.dot(x, wg[0], preferred_element_type=F32)
            up = jnp.dot(x, wu[0], preferred_element_type=F32)
            he = (gate / (1.0 + jnp.exp(-gate)) * up * wts[:, col:col + 1]).astype(BF16)
            t = jnp.dot(he, wd[0], preferred_element_type=F32)
            y = t if y is None else y + t
        ys_ref[...] = y

    @pl.when(pl.program_id(0) >= used_ref[0])
    def _():
        ys_ref[...] = jnp.zeros_like(ys_ref)


def _experts(ea, eb, used, xs, wg, wu, wd, *, tile):
    n_tiles = xs.shape[0] // tile
    rows = lambda j, ea, eb, used: (jnp.minimum(j, used[0] - 1), 0)
    first = lambda j, ea, eb, used: (ea[j], 0, 0)
    second = lambda j, ea, eb, used: (eb[j], 0, 0)
    up_spec = lambda m: pl.BlockSpec((1, D_MODEL, D_EXPERT), m)
    down_spec = lambda m: pl.BlockSpec((1, D_EXPERT, D_MODEL), m)
    return pl.pallas_call(
        _experts_kernel,
        grid_spec=pltpu.PrefetchScalarGridSpec(
            num_scalar_prefetch=3,
            grid=(n_tiles,),
            in_specs=[pl.BlockSpec((tile, EXT_WIDTH), rows),
                      up_spec(first), up_spec(first), down_spec(first),
                      up_spec(second), up_spec(second), down_spec(second)],
            out_specs=pl.BlockSpec((tile, D_MODEL), lambda j, ea, eb, used: (j, 0)),
        ),
        out_shape=jax.ShapeDtypeStruct((xs.shape[0], D_MODEL), F32),
        compiler_params=pltpu.CompilerParams(dimension_semantics=("arbitrary",), vmem_limit_bytes=VMEM_LIMIT),
        name="moe_experts",
    )(ea, eb, used, xs, wg, wu, wd, wg, wu, wd)


def _combine_kernel(pos_ref, next_pos_ref, x1_ref, g_ref, ys_hbm, out_ref, y_scr, sem):
    i = pl.program_id(0)
    tm = x1_ref.shape[0]
    slot = i & 1

    def gather(p_ref, s):
        for k in range(tm):
            _row_copy(ys_hbm, p_ref[0, 0, k], y_scr.at[s], k, sem.at[s]).start(priority=k % DMA_THREADS)

    @pl.when(i == 0)
    def _():
        gather(pos_ref, 0)

    @pl.when(i + 1 < pl.num_programs(0))
    def _():
        gather(next_pos_ref, 1 - slot)

    _row_copy(ys_hbm, 0, y_scr.at[slot], 0, sem.at[slot], rows=tm).wait()
    x2 = x1_ref[...] + y_scr[slot]
    ms = jnp.mean(x2 * x2, axis=-1, keepdims=True)
    out_ref[...] = x2 * lax.rsqrt(ms + NORM_EPS) * g_ref[...]


def _combine(pos3, x1, g, ys):
    n_chunks, _, tm = pos3.shape
    return pl.pallas_call(
        _combine_kernel,
        grid=(n_chunks,),
        in_specs=[
            pl.BlockSpec((1, 1, tm), lambda i: (i, 0, 0), memory_space=pltpu.SMEM),
            pl.BlockSpec((1, 1, tm), lambda i: (jnp.minimum(i + 1, n_chunks - 1), 0, 0), memory_space=pltpu.SMEM),
            pl.BlockSpec((tm, D_MODEL), lambda i: (i, 0)),
            pl.BlockSpec((1, D_MODEL), lambda i: (0, 0)),
            pl.BlockSpec(memory_space=pl.ANY),
        ],
        out_specs=pl.BlockSpec((tm, D_MODEL), lambda i: (i, 0)),
        out_shape=jax.ShapeDtypeStruct(x1.shape, F32),
        scratch_shapes=[pltpu.VMEM((2, tm, D_MODEL), F32), pltpu.SemaphoreType.DMA((2,))],
        compiler_params=pltpu.CompilerParams(dimension_semantics=("arbitrary",), vmem_limit_bytes=VMEM_LIMIT),
        name="moe_combine",
    )(pos3, pos3, x1, g, ys)


def _moe(h2ext, cls3, x1, wg, wu, wd, g, *, tile=MOE_TILE, dispatch_chunk=2048, combine_chunk=1024):
    n = x1.shape[0]
    n_tiles = n // tile + N_CLASSES
    assert n_tiles <= LANES
    pos, tile_cls, ends = _sort(cls3.reshape(n // LANES, LANES), tile=tile)
    xs = _dispatch(pos.reshape(n // dispatch_chunk, 1, dispatch_chunk), ends, h2ext, n_rows=n_tiles * tile, tile=tile)
    used = ends[0, N_CLASSES - 1:N_CLASSES] // tile
    tc = tile_cls[0, :n_tiles]
    tc = jnp.where(jnp.arange(n_tiles) < used[0], tc, tc[jnp.maximum(used[0] - 1, 0)])
    pair_a = jnp.array([p[0] for p in PAIR_ORDER], jnp.int32)
    pair_b = jnp.array([p[1] for p in PAIR_ORDER], jnp.int32)
    ea = (tc // N_PAIRS) * EXPERTS_PER_GROUP + pair_a[tc % N_PAIRS]
    eb = (tc // N_PAIRS) * EXPERTS_PER_GROUP + pair_b[tc % N_PAIRS]
    ys = _experts(ea, eb, used, xs, wg, wu, wd, tile=tile)
    return _combine(pos.reshape(n // combine_chunk, 1, combine_chunk), x1, g, ys)


def _rotary_tables(seq):
    pos = np.arange(seq, dtype=np.float64)
    inv_freq = 1.0 / (ROPE_THETA ** (np.arange(0, ROT_DIM, 2, dtype=np.float64) / ROT_DIM))
    ang = pos[:, None] * inv_freq[None, :]
    cos, sin = np.cos(ang), np.sin(ang)
    zeros = np.zeros((seq, HEAD_DIM - ROT_DIM))
    zh = np.zeros((seq, ROT_HALF))
    cos_h = np.concatenate([cos, cos, np.ones((seq, HEAD_DIM - ROT_DIM))], axis=1)
    sa_h = np.concatenate([-sin, zh, zeros], axis=1)
    sb_h = np.concatenate([zh, sin, zeros], axis=1)
    tile = lambda t: jnp.asarray(np.concatenate([t, t], axis=1), F32)
    return tile(cos_h), tile(sa_h), tile(sb_h)


def kernel(x, attn_norm, w_in, b_forget, w_out, ffn_norm, w_group, w_expert, w_gate_e, w_up_e, w_down_e, final_norm):
    b, seq, d = x.shape
    assert d == D_MODEL and w_in.shape[0] == 1, "single-layer block"
    n = b * seq
    scale = HEAD_DIM ** -0.5 * LOG2E
    col_scale = np.ones((2 * QKV_WIDTH + LANES,), np.float32)
    col_scale[0:WIDTH_A] = scale
    col_scale[QKV_WIDTH:QKV_WIDTH + WIDTH_B] = scale
    w = (jnp.pad(w_in[0], ((0, 0), (0, LANES - HEADS_B))) * col_scale[None, :]).astype(BF16)
    bf = jnp.pad(b_forget[0].astype(F32), (0, LANES - HEADS_B))[None, :]
    cosv, sa, sb = _rotary_tables(seq)
    x2 = x.reshape(n, d)

    qkva, qkvb, c = _inproj(x2, attn_norm[0][None, :], w, cosv, sa, sb, bf, seq=seq, tm=512)
    out_a = _dilated(qkva.reshape(b, seq, QKV_WIDTH))
    out_b, wg, wu, wd = _fox(qkvb.reshape(b, seq, QKV_WIDTH), c.reshape(b, seq, LANES),
                             w_gate_e[0], w_up_e[0], w_down_e[0], tq=1024)

    wr = jnp.pad(jnp.concatenate([w_group[0], w_expert[0]], axis=1).astype(F32),
                 ((0, 0), (0, LANES - N_GROUPS - N_EXPERTS)))
    wr_hi = wr.astype(BF16)
    wr = jnp.concatenate([wr_hi, (wr - wr_hi.astype(F32)).astype(BF16)], axis=1)
    x1, h2ext, cls3 = _outproj(out_a.reshape(n, WIDTH_A), out_b.reshape(n, WIDTH_B), x2,
                               w_out[0].astype(BF16), ffn_norm[0][None, :], wr, tm=512)
    out = _moe(h2ext, cls3, x1, wg, wu, wd, final_norm[None, :])
    return out.reshape(b, seq, d)
```

```python
import functools

import jax
import jax.numpy as jnp
import numpy as np
from jax import lax
from jax.experimental import pallas as pl
from jax.experimental.pallas import tpu as pltpu

F32 = jnp.float32
BF16 = jnp.bfloat16

D_MODEL = 1024
HEAD_DIM = 64
HEADS_A = 8
HEADS_B = 8
WIDTH_A = HEADS_A * HEAD_DIM
WIDTH_B = HEADS_B * HEAD_DIM
QKV_WIDTH = 3 * WIDTH_A
DILATIONS = ((128, 1), (512, 4), (2048, 16))
ROT_DIM = HEAD_DIM // 4
ROT_HALF = ROT_DIM // 2
ROPE_THETA = 500000.0
N_GROUPS = 4
EXPERTS_PER_GROUP = 4
N_EXPERTS = N_GROUPS * EXPERTS_PER_GROUP
D_EXPERT = 512
NORM_EPS = 1e-6
NEG_INF = -1e30

LANES = 128
HEAD_PAIRS = WIDTH_A // LANES
WIN = 128
MAX_DIL = 16
PERM = MAX_DIL * MAX_DIL
KV_BLOCK = 256
N_PAIRS = EXPERTS_PER_GROUP * (EXPERTS_PER_GROUP - 1) // 2
N_CLASSES = N_GROUPS * N_PAIRS
PAIR_ORDER = ((0, 1), (0, 2), (1, 2), (1, 3), (0, 3), (2, 3))
assert len(PAIR_ORDER) == N_PAIRS
MOE_TILE = 256
EXT_WIDTH = D_MODEL + LANES
LOGIT_MAX = 60.0
BRANCH_UNROLL = 8
DMA_THREADS = 2
ROUTER_OFF = N_GROUPS
VMEM_LIMIT = 56 * 1024 * 1024
LOG2E = 1.4426950408889634
BOUND_SLACK = 1.0 + 2.0 ** -6
BOUND_EPS = 2.0 ** -7


def _split3(x):
    hi = x.astype(BF16)
    r1 = x - hi.astype(F32)
    mid = r1.astype(BF16)
    lo = (r1 - mid.astype(F32)).astype(BF16)
    return hi, mid, lo


def _dot_nt(a, b):
    return lax.dot_general(a, b, (((1,), (1,)), ((), ())), preferred_element_type=F32)


def _inproj_kernel(x_ref, g_ref, w_ref, cos_ref, sa_ref, sb_ref, bf_ref,
                   qkva_ref, qkvb_ref, c_ref, h_scr, carry_scr, *, tiles_per_seq):
    i = pl.program_id(0)
    tm = x_ref.shape[0]
    x = x_ref[...]
    ms = jnp.mean(x * x, axis=-1, keepdims=True)
    h_scr[...] = (x * lax.rsqrt(ms + NORM_EPS) * g_ref[...]).astype(BF16)
    cosv = cos_ref[...]
    sa = sa_ref[...]
    sb = sb_ref[...]
    for g in range(6):
        p = _dot_nt(h_scr[...], w_ref[g * WIDTH_A:(g + 1) * WIDTH_A, :])
        dst = qkva_ref if g < 3 else qkvb_ref
        col0 = (g % 3) * WIDTH_A
        if g < 2:
            for k in range(HEAD_PAIRS):
                t = p[:, k * LANES:(k + 1) * LANES]
                t = (t * cosv + pltpu.roll(t, LANES - ROT_HALF, 1) * sa + pltpu.roll(t, ROT_HALF, 1) * sb)
                dst[:, col0 + k * LANES:col0 + (k + 1) * LANES] = t.astype(BF16)
        else:
            dst[:, col0:col0 + WIDTH_A] = p.astype(BF16)
    z = _dot_nt(h_scr[...], w_ref[2 * QKV_WIDTH:2 * QKV_WIDTH + LANES, :]) + bf_ref[...]
    lf = jnp.minimum(z, 0.0) - jnp.log1p(jnp.exp(-jnp.abs(z)))
    lane = lax.broadcasted_iota(jnp.int32, (tm, LANES), 1)
    lf = jnp.where(lane < HEADS_B, lf * LOG2E, 0.0)
    hi, mid, lo = _split3(lf)
    tri = (lax.broadcasted_iota(jnp.int32, (tm, tm), 0) >= lax.broadcasted_iota(jnp.int32, (tm, tm), 1)).astype(BF16)
    cs = (jnp.dot(tri, hi, preferred_element_type=F32) + jnp.dot(tri, mid, preferred_element_type=F32)
          + jnp.dot(tri, lo, preferred_element_type=F32))

    @pl.when(i % tiles_per_seq == 0)
    def _():
        carry_scr[...] = jnp.zeros_like(carry_scr)

    c = cs + carry_scr[0:1, :]
    c_ref[...] = c
    carry_scr[...] = jnp.broadcast_to(c[tm - 1:tm, :], carry_scr.shape)


def _inproj(x2, g, w, cosv, sa, sb, bf, *, seq, tm):
    n = x2.shape[0]
    kern = functools.partial(_inproj_kernel, tiles_per_seq=seq // tm)
    tps = seq // tm
    return pl.pallas_call(
        kern,
        grid=(n // tm,),
        in_specs=[
            pl.BlockSpec((tm, D_MODEL), lambda i: (i, 0)),
            pl.BlockSpec((1, D_MODEL), lambda i: (0, 0)),
            pl.BlockSpec(w.shape, lambda i: (0, 0)),
            pl.BlockSpec((tm, LANES), lambda i: (i % tps, 0)),
            pl.BlockSpec((tm, LANES), lambda i: (i % tps, 0)),
            pl.BlockSpec((tm, LANES), lambda i: (i % tps, 0)),
            pl.BlockSpec((1, LANES), lambda i: (0, 0)),
        ],
        out_specs=[
            pl.BlockSpec((tm, QKV_WIDTH), lambda i: (i, 0)),
            pl.BlockSpec((tm, QKV_WIDTH), lambda i: (i, 0)),
            pl.BlockSpec((tm, LANES), lambda i: (i, 0)),
        ],
        out_shape=[
            jax.ShapeDtypeStruct((n, QKV_WIDTH), BF16),
            jax.ShapeDtypeStruct((n, QKV_WIDTH), BF16),
            jax.ShapeDtypeStruct((n, LANES), F32),
        ],
        scratch_shapes=[pltpu.VMEM((tm, D_MODEL), BF16), pltpu.VMEM((8, LANES), F32)],
        compiler_params=pltpu.CompilerParams(dimension_semantics=("arbitrary",), vmem_limit_bytes=VMEM_LIMIT),
        name="inproj",
    )(x2, g, w, cosv, sa, sb, bf)


def _head_column(c_all, lane, head):
    return jnp.sum(jnp.where(lane == head, c_all, 0.0), axis=1, keepdims=True)


def _own_lanes(lane, hl):
    return (lane < HEAD_DIM) if hl == 0 else (lane >= HEAD_DIM)


def _head_sum_selector():
    lane = lax.broadcasted_iota(jnp.int32, (LANES, LANES), 0)
    col = lax.broadcasted_iota(jnp.int32, (LANES, LANES), 1)
    return jnp.where(((col == 0) & (lane < HEAD_DIM)) | ((col == 1) & (lane >= HEAD_DIM)), 1.0, 0.0).astype(BF16)


def _max_sq_norm(t, selector):
    return jnp.max(jnp.dot((t * t).astype(BF16), selector, preferred_element_type=F32), axis=0, keepdims=True)


def _norm_bound(qsq, ksq_max):
    return jnp.sqrt(qsq * ksq_max) * BOUND_SLACK + BOUND_EPS


def _fox_kernel(q_ref, k_ref, v_ref, c_ref, wg_ref, wu_ref, wd_ref, o_ref, wg_out, wu_out, wd_out,
                q0_scr, q1_scr, k0_scr, k1_scr, v0_scr, v1_scr, bound_scr, acc0_scr, acc1_scr, s_scr,
                *, tq, seq, chunk):
    hp = pl.program_id(1)
    qi = pl.program_id(2)
    for src, dst in ((wg_ref, wg_out), (wu_ref, wu_out), (wd_ref, wd_out)):
        dst[...] = src[...].astype(dst.dtype)
    q_scr = (q0_scr, q1_scr)
    k_scr = (k0_scr, k1_scr)
    v_scr = (v0_scr, v1_scr)
    acc_scr = (acc0_scr, acc1_scr)

    def augmented(vals, cterm, lane, hl, key_side):
        hi, mid, lo = (t.astype(F32) for t in _split3(cterm))
        base = HEAD_DIM if hl == 0 else 0
        ones = (lane >= base + (0 if key_side else 3)) & (lane < base + (3 if key_side else 6))
        o = 3 if key_side else 0
        sgn = -1.0 if key_side else 1.0
        aug = jnp.where(lane == base + o, sgn * hi,
                        jnp.where(lane == base + o + 1, sgn * mid,
                                  jnp.where(lane == base + o + 2, sgn * lo,
                                            jnp.where(ones, 1.0, 0.0))))
        return jnp.where(_own_lanes(lane, hl), vals, aug).astype(BF16)

    @pl.when(qi == 0)
    def _build():
        lane = lax.broadcasted_iota(jnp.int32, (chunk, LANES), 1)

        def body(ci, sq_max):
            rows = pl.ds(pl.multiple_of(ci * chunk, chunk), chunk)
            qc = q_ref[0, rows, :].astype(F32)
            kc = k_ref[0, rows, :].astype(F32)
            vc = v_ref[0, rows, :].astype(F32)
            cc = c_ref[0, rows, :]
            for hl in range(2):
                ccol = _head_column(cc, lane, 2 * hp + hl)
                q_scr[hl][rows, :] = augmented(qc, ccol, lane, hl, False)
                k_scr[hl][rows, :] = augmented(kc, ccol, lane, hl, True)
                one_lane = HEAD_DIM if hl == 0 else 0
                v_scr[hl][rows, :] = jnp.where(
                    _own_lanes(lane, hl), vc, jnp.where(lane == one_lane, 1.0, 0.0)).astype(BF16)
            return (jnp.maximum(sq_max[0], _max_sq_norm(qc, selector)),
                    jnp.maximum(sq_max[1], _max_sq_norm(kc, selector)))

        selector = _head_sum_selector()
        sq_max = lax.fori_loop(0, seq // chunk, body, (jnp.zeros((1, LANES), F32),) * 2)
        bound_scr[...] = jnp.broadcast_to(_norm_bound(sq_max[0], sq_max[1]), bound_scr.shape)

    lane_q = lax.broadcasted_iota(jnp.int32, (tq, LANES), 1)
    q_rows = pl.ds(pl.multiple_of(qi * tq, tq), tq)
    worst = jnp.max(bound_scr[...])
    n_diag = tq // KV_BLOCK
    n_full = qi * n_diag
    row_minus_col = (lax.broadcasted_iota(jnp.int32, (tq, KV_BLOCK), 0)
                     - lax.broadcasted_iota(jnp.int32, (tq, KV_BLOCK), 1))

    def causal_mask(s, d, top=0):
        return jnp.where(row_minus_col[top:tq, :] >= d * KV_BLOCK, s, NEG_INF)

    def logits(hl, j, top=0):
        r0 = pl.multiple_of(j * KV_BLOCK, KV_BLOCK)
        rows = q_rows if top == 0 else pl.ds(pl.multiple_of(qi * tq + top, KV_BLOCK), tq - top)
        return lax.dot_general(q_scr[hl][rows, :], k_scr[hl][pl.ds(r0, KV_BLOCK), :], (((1,), (1,)), ((), ())),
                               preferred_element_type=F32)

    def values(hl, j):
        return v_scr[hl][pl.ds(pl.multiple_of(j * KV_BLOCK, KV_BLOCK), KV_BLOCK), :]

    def finish(a0, a1):
        l0 = a0[:, HEAD_DIM:HEAD_DIM + 1]
        l1 = a1[:, 0:1]
        o_ref[0] = jnp.where(lane_q < HEAD_DIM, a0 / l0, a1 / l1).astype(o_ref.dtype)

    @pl.when(worst <= LOGIT_MAX)
    def _fast():
        acc0_scr[...] = jnp.zeros_like(acc0_scr)
        acc1_scr[...] = jnp.zeros_like(acc1_scr)

        def produce(j, slot, top=0):
            for hl in range(2):
                s_scr[slot, hl, top:tq, :] = logits(hl, j, top)

        def consume(j, slot, diag):
            top = 0 if diag is None else diag * KV_BLOCK
            for hl in range(2):
                s = s_scr[slot, hl, top:tq, :]
                if diag is not None:
                    s = causal_mask(s, diag, top)
                acc_scr[hl][top:tq, :] += jnp.dot(jnp.exp2(s).astype(BF16), values(hl, j),
                                                  preferred_element_type=F32)

        def produce_diagonal(d):
            produce(n_full + d, d, d * KV_BLOCK)

        @pl.when(qi == 0)
        def _():
            for d in range(n_diag):
                produce_diagonal(d)

        @pl.when(qi > 0)
        def _():
            for d in range(n_diag):
                produce(d, d)

            def body(i, carry):
                for d in range(n_diag):
                    j = i * n_diag + d
                    consume(j, d, None)
                    produce(j + n_diag, d)
                return carry

            lax.fori_loop(0, qi - 1, body, 0)
            for d in range(n_diag):
                consume(n_full - n_diag + d, d, None)
                produce_diagonal(d)

        for d in range(n_diag):
            consume(n_full + d, d, d)
        finish(acc0_scr[...], acc1_scr[...])

    @pl.when(worst > LOGIT_MAX)
    def _general():
        def step(j, carry, diag):
            new = []
            for hl in range(2):
                m, a = carry[2 * hl:2 * hl + 2]
                s = logits(hl, j)
                if diag is not None:
                    s = causal_mask(s, diag)
                mn = jnp.maximum(m, jnp.max(s, axis=1, keepdims=True))
                a = jnp.exp2(m - mn) * a + jnp.dot(jnp.exp2(s - mn).astype(BF16), values(hl, j),
                                                  preferred_element_type=F32)
                new += [mn, a]
            return tuple(new)

        carry = (jnp.full((tq, 1), NEG_INF, F32), jnp.zeros((tq, LANES), F32)) * 2
        carry = lax.fori_loop(0, n_full, lambda j, c: step(j, c, None), carry)
        for d in range(n_diag):
            carry = step(n_full + d, carry, d)
        finish(carry[1], carry[3])


def _fox(qkvb, c, wg, wu, wd, *, tq, chunk=512):
    b, seq, _ = qkvb.shape
    nq = seq // tq
    steps = b * HEAD_PAIRS * nq
    parts = steps // N_EXPERTS
    assert parts * N_EXPERTS == steps and D_MODEL % parts == 0 and D_EXPERT % parts == 0

    def slice_spec(rows, cols):
        def index(bi, hp, qi):
            step = (bi * HEAD_PAIRS + hp) * nq + qi
            return (step // parts, step % parts, 0)
        return pl.BlockSpec((1, rows // parts, cols), index)

    w_specs = [slice_spec(D_MODEL, D_EXPERT), slice_spec(D_MODEL, D_EXPERT), slice_spec(D_EXPERT, D_MODEL)]
    kern = functools.partial(_fox_kernel, tq=tq, seq=seq, chunk=chunk)
    return pl.pallas_call(
        kern,
        grid=(b, HEAD_PAIRS, nq),
        in_specs=[
            pl.BlockSpec((1, seq, LANES), lambda bi, hp, qi: (bi, 0, hp)),
            pl.BlockSpec((1, seq, LANES), lambda bi, hp, qi: (bi, 0, HEAD_PAIRS + hp)),
            pl.BlockSpec((1, seq, LANES), lambda bi, hp, qi: (bi, 0, 2 * HEAD_PAIRS + hp)),
            pl.BlockSpec((1, seq, LANES), lambda bi, hp, qi: (bi, 0, 0)),
        ] + w_specs,
        out_specs=[pl.BlockSpec((1, tq, LANES), lambda bi, hp, qi: (bi, qi, hp))] + w_specs,
        out_shape=[jax.ShapeDtypeStruct((b, seq, WIDTH_B), BF16)]
        + [jax.ShapeDtypeStruct(w.shape, BF16) for w in (wg, wu, wd)],
        scratch_shapes=[pltpu.VMEM((seq, LANES), BF16)] * 6 + [
                        pltpu.VMEM((8, LANES), F32),
                        pltpu.VMEM((tq, LANES), F32), pltpu.VMEM((tq, LANES), F32),
                        pltpu.VMEM((tq // KV_BLOCK, 2, tq, KV_BLOCK), F32)],
        compiler_params=pltpu.CompilerParams(
            dimension_semantics=("arbitrary", "arbitrary", "arbitrary"), vmem_limit_bytes=VMEM_LIMIT),
        name="fox",
    )(qkvb, qkvb, qkvb, c, wg, wu, wd)


def _dilated_general(q_ref, k_ref, v_ref, o_ref, q16, k16, v16, m_s, l_s, acc_s, on_s, lse_s, *, seq):
    sub = seq // MAX_DIL
    own0 = lax.broadcasted_iota(jnp.int32, (WIN, LANES), 1) < HEAD_DIM
    ri = lax.broadcasted_iota(jnp.int32, (PERM, PERM), 0)
    ci = lax.broadcasted_iota(jnp.int32, (PERM, PERM), 1)
    perm = (((ri >> 4) == (ci & 15)) & ((ri & 15) == (ci >> 4))).astype(BF16)

    qa = lax.broadcasted_iota(jnp.int32, (WIN, 2 * WIN), 0)
    kb_i = lax.broadcasted_iota(jnp.int32, (WIN, 2 * WIN), 1)
    cur = kb_i >= WIN
    kin = kb_i & (WIN - 1)
    dist_nat = qa - kin + jnp.where(cur, 0, WIN)
    band_nat = (dist_nat >= 0) & (dist_nat <= WIN)
    sq = ((qa & 31) << 2) + (qa >> 5)
    sk = ((kin & 31) << 2) + (kin >> 5)
    dist_4 = sq - sk + jnp.where(cur, 0, WIN)
    band_4 = (dist_4 >= 0) & (dist_4 <= WIN)

    def attend(qb, kb, vb, mask, state):
        qf = qb.astype(F32)
        new = []
        for hl in range(2):
            m, l, a = state[3 * hl:3 * hl + 3]
            own = own0 if hl == 0 else jnp.logical_not(own0)
            qh = jnp.where(own, qf, 0.0).astype(BF16)
            s = lax.dot_general(qh, kb, (((1,), (1,)), ((), ())), preferred_element_type=F32)
            s = jnp.where(mask, s, NEG_INF)
            mn = jnp.maximum(m, jnp.max(s, axis=1, keepdims=True))
            alpha = jnp.exp2(m - mn)
            p = jnp.exp2(s - mn)
            l = alpha * l + jnp.sum(p, axis=1, keepdims=True)
            a = alpha * a + jnp.dot(p.astype(BF16), vb, preferred_element_type=F32)
            new += [mn, l, a]
        return new

    def pack_state(st):
        m0, l0, a0, m1, l1, a1 = st
        return (jnp.where(own0, m0, m1), jnp.where(own0, l0, l1), jnp.where(own0, a0, a1))

    def deint(blk, carry):
        r0 = pl.multiple_of(blk * PERM, PERM)
        j0 = pl.multiple_of(blk * MAX_DIL, MAX_DIL)
        for src, dst in ((q_ref, q16), (k_ref, k16), (v_ref, v16)):
            y = jnp.dot(perm, src[0, pl.ds(r0, PERM), :], preferred_element_type=F32).astype(BF16)
            for r in range(MAX_DIL):
                dst[r, pl.ds(j0, MAX_DIL), :] = y[r * MAX_DIL:(r + 1) * MAX_DIL, :]
        return carry

    lax.fori_loop(0, seq // PERM, deint, 0)

    fresh = [jnp.full((WIN, 1), NEG_INF, F32), jnp.zeros((WIN, 1), F32), jnp.zeros((WIN, LANES), F32)] * 2
    nb16 = sub // WIN

    def d16_body(t, carry):
        r = t // nb16
        n = t % nb16
        c0 = pl.multiple_of(n * WIN, WIN)
        p0 = pl.multiple_of(jnp.maximum(n - 1, 0) * WIN, WIN)
        qb = q16[r, pl.ds(c0, WIN), :]
        kb = jnp.concatenate([k16[r, pl.ds(p0, WIN), :], k16[r, pl.ds(c0, WIN), :]], axis=0)
        vb = jnp.concatenate([v16[r, pl.ds(p0, WIN), :], v16[r, pl.ds(c0, WIN), :]], axis=0)
        mask = band_nat & (cur | (n > 0))
        mm, ll, aa = pack_state(attend(qb, kb, vb, mask, fresh))
        row = pl.multiple_of(r * sub + c0, WIN)
        m_s[pl.ds(row, WIN), :] = mm
        l_s[pl.ds(row, WIN), :] = ll
        acc_s[pl.ds(row, WIN), :] = aa
        return carry

    lax.fori_loop(0, MAX_DIL * nb16, d16_body, 0)

    nb4 = (seq // 4) // WIN
    ch = WIN // 4

    def d4_body(t, carry):
        r4 = t // nb4
        n = t % nb4
        c0 = pl.multiple_of(n * ch, ch)
        p0 = pl.multiple_of(jnp.maximum(n - 1, 0) * ch, ch)

        def gather(ref, j0):
            return [ref[r4 + 4 * q, pl.ds(j0, ch), :] for q in range(4)]

        qb = jnp.concatenate(gather(q16, c0), axis=0)
        kb = jnp.concatenate(gather(k16, p0) + gather(k16, c0), axis=0)
        vb = jnp.concatenate(gather(v16, p0) + gather(v16, c0), axis=0)
        rows = [pl.multiple_of((r4 + 4 * q) * sub + c0, ch) for q in range(4)]
        mm = jnp.concatenate([m_s[pl.ds(rw, ch), :] for rw in rows], axis=0)
        ll = jnp.concatenate([l_s[pl.ds(rw, ch), :] for rw in rows], axis=0)
        aa = jnp.concatenate([acc_s[pl.ds(rw, ch), :] for rw in rows], axis=0)
        state = [mm[:, 0:1], ll[:, 0:1], aa, mm[:, HEAD_DIM:HEAD_DIM + 1], ll[:, HEAD_DIM:HEAD_DIM + 1], aa]
        mask = band_4 & (cur | (n > 0))
        mm, ll, aa = pack_state(attend(qb, kb, vb, mask, state))
        for q, rw in enumerate(rows):
            m_s[pl.ds(rw, ch), :] = mm[q * ch:(q + 1) * ch, :]
            l_s[pl.ds(rw, ch), :] = ll[q * ch:(q + 1) * ch, :]
            acc_s[pl.ds(rw, ch), :] = aa[q * ch:(q + 1) * ch, :]
        return carry

    lax.fori_loop(0, 4 * nb4, d4_body, 0)

    def renat(blk, carry):
        j0 = pl.multiple_of(blk * MAX_DIL, MAX_DIL)
        r0 = pl.multiple_of(blk * PERM, PERM)
        rows = [pl.multiple_of(r * sub + j0, MAX_DIL) for r in range(MAX_DIL)]
        mm = jnp.concatenate([m_s[pl.ds(rw, MAX_DIL), :] for rw in rows], axis=0)
        ll = jnp.concatenate([l_s[pl.ds(rw, MAX_DIL), :] for rw in rows], axis=0)
        aa = jnp.concatenate([acc_s[pl.ds(rw, MAX_DIL), :] for rw in rows], axis=0)
        o = (aa / ll).astype(BF16)
        hi, mid, lo = _split3(mm + jnp.log2(ll))
        on_s[pl.ds(r0, PERM), :] = jnp.dot(perm, o, preferred_element_type=F32)
        lse_s[pl.ds(r0, PERM), :] = (jnp.dot(perm, hi, preferred_element_type=F32)
                                     + jnp.dot(perm, mid, preferred_element_type=F32)
                                     + jnp.dot(perm, lo, preferred_element_type=F32))
        return carry

    lax.fori_loop(0, seq // PERM, renat, 0)

    def d1_body(n, carry):
        c0 = pl.multiple_of(n * WIN, WIN)
        p0 = pl.multiple_of(jnp.maximum(n - 1, 0) * WIN, WIN)
        qb = q_ref[0, pl.ds(c0, WIN), :]
        kb = jnp.concatenate([k_ref[0, pl.ds(p0, WIN), :], k_ref[0, pl.ds(c0, WIN), :]], axis=0)
        vb = jnp.concatenate([v_ref[0, pl.ds(p0, WIN), :], v_ref[0, pl.ds(c0, WIN), :]], axis=0)
        lse = lse_s[pl.ds(c0, WIN), :]
        o = on_s[pl.ds(c0, WIN), :]
        one = jnp.ones((WIN, 1), F32)
        state = [lse[:, 0:1], one, o, lse[:, HEAD_DIM:HEAD_DIM + 1], one, o]
        mask = band_nat & (cur | (n > 0))
        m0, l0, a0, m1, l1, a1 = attend(qb, kb, vb, mask, state)
        o_ref[0, pl.ds(c0, WIN), :] = jnp.where(own0, a0 / l0, a1 / l1).astype(o_ref.dtype)
        return carry

    lax.fori_loop(0, seq // WIN, d1_body, 0)


def _dilated_fast(o_ref, qn, kn, vn, qr, kr, vr, fr, fn, s_scr, mask_scr, *, seq):
    sub = seq // MAX_DIL
    own0 = lax.broadcasted_iota(jnp.int32, (WIN, LANES), 1) < HEAD_DIM
    head_mask = [jnp.where(own0, 1.0, 0.0).astype(BF16), jnp.where(own0, 0.0, 1.0).astype(BF16)]
    ri = lax.broadcasted_iota(jnp.int32, (PERM, PERM), 0)
    ci = lax.broadcasted_iota(jnp.int32, (PERM, PERM), 1)
    perm = (((ri >> 4) == (ci & 15)) & ((ri & 15) == (ci >> 4))).astype(BF16)

    qa = lax.broadcasted_iota(jnp.int32, (WIN, 2 * WIN), 0)
    kb_i = lax.broadcasted_iota(jnp.int32, (WIN, 2 * WIN), 1)
    cur = kb_i >= WIN
    kin = kb_i & (WIN - 1)
    dist_nat = qa - kin + jnp.where(cur, 0, WIN)
    band_nat = (dist_nat >= 0) & (dist_nat <= WIN)
    sq = ((qa & 31) << 2) + (qa >> 5)
    sk = ((kin & 31) << 2) + (kin >> 5)
    dist_4 = sq - sk + jnp.where(cur, 0, WIN)
    band_4 = (dist_4 >= 0) & (dist_4 <= WIN)
    for i, band in enumerate((band_nat, band_4)):
        mask_scr[2 * i] = jnp.where(band & cur, 1.0, 0.0).astype(BF16)
        mask_scr[2 * i + 1] = jnp.where(band, 1.0, 0.0).astype(BF16)

    def run_branch(nblocks, block_in_seq, fetch, mask_base, sink, before_trip=None):
        def produce(t, pair, u):
            for hl in range(2):
                qb, kb = fetch(t, hl, True)
                s_scr[pair, u, hl] = lax.dot_general(qb * head_mask[hl], kb, (((1,), (1,)), ((), ())),
                                                     preferred_element_type=F32)

        def consume(t, pair, u):
            mk = mask_scr[mask_base + jnp.minimum(block_in_seq(t), 1)]
            pv = []
            for hl in range(2):
                p = jnp.exp2(s_scr[pair, u, hl]).astype(BF16) * mk
                pv.append(jnp.dot(p, fetch(t, hl, False), preferred_element_type=F32))
            sink(t, pv)

        for u in range(BRANCH_UNROLL):
            produce(u, 0, u)

        def body(i, carry):
            pair = i & 1
            if before_trip is not None:
                before_trip(i)
            for u in range(BRANCH_UNROLL):
                consume(BRANCH_UNROLL * i + u, pair, u)
            for u in range(BRANCH_UNROLL):
                produce(jnp.minimum(BRANCH_UNROLL * (i + 1) + u, nblocks - 1), 1 - pair, u)
            return carry

        lax.fori_loop(0, nblocks // BRANCH_UNROLL, body, 0)

    def prev_cur(ref_block, n, size):
        c0 = pl.multiple_of(n * size, size)
        p0 = pl.multiple_of(jnp.maximum(n - 1, 0) * size, size)
        return ref_block(p0) + ref_block(c0)

    nb16 = sub // WIN

    def fetch16(t, hl, qk):
        r, n = t // nb16, t % nb16
        if qk:
            return (qr[hl][r, pl.ds(pl.multiple_of(n * WIN, WIN), WIN), :],
                    jnp.concatenate(prev_cur(lambda s: [kr[hl][r, pl.ds(s, WIN), :]], n, WIN), axis=0))
        return jnp.concatenate(prev_cur(lambda s: [vr[hl][r, pl.ds(s, WIN), :]], n, WIN), axis=0)

    def sink16(t, pv):
        row = pl.multiple_of(t * WIN, WIN)
        for hl in range(2):
            fr[hl][pl.ds(row, WIN), :] = pv[hl]

    run_branch(MAX_DIL * nb16, lambda t: t % nb16, fetch16, 0, sink16)

    nb4 = (seq // 4) // WIN
    ch = WIN // 4

    def fetch4(t, hl, qk):
        r4, n = t // nb4, t % nb4
        gather = lambda ref: (lambda s: [ref[r4 + 4 * q, pl.ds(s, ch), :] for q in range(4)])
        if qk:
            return (jnp.concatenate(gather(qr[hl])(pl.multiple_of(n * ch, ch)), axis=0),
                    jnp.concatenate(prev_cur(gather(kr[hl]), n, ch), axis=0))
        return jnp.concatenate(prev_cur(gather(vr[hl]), n, ch), axis=0)

    def sink4(t, pv):
        r4, n = t // nb4, t % nb4
        for q in range(4):
            row = pl.multiple_of((r4 + 4 * q) * sub + n * ch, ch)
            for hl in range(2):
                fr[hl][pl.ds(row, ch), :] += pv[hl][q * ch:(q + 1) * ch, :]

    run_branch(4 * nb4, lambda t: t % nb4, fetch4, 2, sink4)

    def renat(i):
        for u in range(BRANCH_UNROLL * WIN // PERM):
            blk = i * (BRANCH_UNROLL * WIN // PERM) + u
            j0 = pl.multiple_of(blk * MAX_DIL, MAX_DIL)
            r0 = pl.multiple_of(blk * PERM, PERM)
            for hl in range(2):
                a = jnp.concatenate([fr[hl][pl.ds(pl.multiple_of(r * sub + j0, MAX_DIL), MAX_DIL), :]
                                     for r in range(MAX_DIL)], axis=0)
                hi = a.astype(BF16)
                lo = (a - hi.astype(F32)).astype(BF16)
                y = jnp.dot(perm, jnp.concatenate([hi, lo], axis=1), preferred_element_type=F32)
                fn[hl][pl.ds(r0, PERM), :] = y[:, 0:LANES] + y[:, LANES:2 * LANES]

    def fetch1(t, hl, qk):
        if qk:
            return (qn[hl][pl.ds(pl.multiple_of(t * WIN, WIN), WIN), :],
                    jnp.concatenate(prev_cur(lambda s: [kn[hl][pl.ds(s, WIN), :]], t, WIN), axis=0))
        return jnp.concatenate(prev_cur(lambda s: [vn[hl][pl.ds(s, WIN), :]], t, WIN), axis=0)

    def sink1(t, pv):
        row = pl.multiple_of(t * WIN, WIN)
        t0 = fn[0][pl.ds(row, WIN), :] + pv[0]
        t1 = fn[1][pl.ds(row, WIN), :] + pv[1]
        o_ref[0, pl.ds(row, WIN), :] = jnp.where(own0, t0 / t0[:, HEAD_DIM:HEAD_DIM + 1],
                                                 t1 / t1[:, 0:1]).astype(o_ref.dtype)

    run_branch(seq // WIN, lambda t: t, fetch1, 0, sink1, before_trip=renat)


def _dilated_kernel(q_ref, k_ref, v_ref, o_ref, vn0, vn1, qr, kr, vr0, vr1,
                    f0, f1, f2, f3, f4, s_scr, mask_scr, *, seq, chunk):
    lane = lax.broadcasted_iota(jnp.int32, (chunk, LANES), 1)
    vn = (vn0, vn1)

    ri = lax.broadcasted_iota(jnp.int32, (PERM, PERM), 0)
    ci_ = lax.broadcasted_iota(jnp.int32, (PERM, PERM), 1)
    perm = (((ri >> 4) == (ci_ & 15)) & ((ri & 15) == (ci_ >> 4))).astype(BF16)

    def stats(ci, sq_max):
        rows = pl.ds(pl.multiple_of(ci * chunk, chunk), chunk)
        qb = q_ref[0, rows, :]
        kb = k_ref[0, rows, :]
        vc = v_ref[0, rows, :].astype(F32)
        vb = []
        for hl in range(2):
            ones_lane = HEAD_DIM if hl == 0 else 0
            vb.append(jnp.where(_own_lanes(lane, hl), vc, jnp.where(lane == ones_lane, 1.0, 0.0)).astype(BF16))
            vn[hl][rows, :] = vb[hl]
        for u in range(chunk // PERM):
            j0 = pl.multiple_of((ci * (chunk // PERM) + u) * MAX_DIL, MAX_DIL)
            for srcs, dsts in (((qb, kb), (qr, kr)), (vb, (vr0, vr1))):
                both = jnp.concatenate([src[u * PERM:(u + 1) * PERM, :] for src in srcs], axis=1)
                y = jnp.dot(perm, both, preferred_element_type=F32).astype(BF16)
                for r in range(MAX_DIL):
                    for half, dst in enumerate(dsts):
                        dst[r, pl.ds(j0, MAX_DIL), :] = y[r * MAX_DIL:(r + 1) * MAX_DIL, half * LANES:(half + 1) * LANES]
        return (jnp.maximum(sq_max[0], _max_sq_norm(qb.astype(F32), selector)),
                jnp.maximum(sq_max[1], _max_sq_norm(kb.astype(F32), selector)))

    selector = _head_sum_selector()
    sq_max = lax.fori_loop(0, seq // chunk, stats, (jnp.zeros((1, LANES), F32),) * 2)
    worst = jnp.max(_norm_bound(sq_max[0], sq_max[1]))

    @pl.when(worst <= LOGIT_MAX)
    def _fast():
        qn, kn = q_ref.at[0], k_ref.at[0]
        _dilated_fast(o_ref, (qn, qn), (kn, kn), vn, (qr, qr), (kr, kr), (vr0, vr1), (f0, f1), (f2, f3),
                      s_scr, mask_scr, seq=seq)

    @pl.when(worst > LOGIT_MAX)
    def _general():
        _dilated_general(q_ref, k_ref, v_ref, o_ref, qr, kr, vr0, f0, f1, f2, f3, f4, seq=seq)


def _dilated(qkva, *, chunk=512):
    b, seq, _ = qkva.shape
    sub = seq // MAX_DIL
    kern = functools.partial(_dilated_kernel, seq=seq, chunk=chunk)
    return pl.pallas_call(
        kern,
        grid=(b, HEAD_PAIRS),
        in_specs=[
            pl.BlockSpec((1, seq, LANES), lambda bi, hp: (bi, 0, hp)),
            pl.BlockSpec((1, seq, LANES), lambda bi, hp: (bi, 0, HEAD_PAIRS + hp)),
            pl.BlockSpec((1, seq, LANES), lambda bi, hp: (bi, 0, 2 * HEAD_PAIRS + hp)),
        ],
        out_specs=pl.BlockSpec((1, seq, LANES), lambda bi, hp: (bi, 0, hp)),
        out_shape=jax.ShapeDtypeStruct((b, seq, WIDTH_A), BF16),
        scratch_shapes=(
            [pltpu.VMEM((seq, LANES), BF16)] * 2
            + [pltpu.VMEM((MAX_DIL, sub, LANES), BF16)] * 4
            + [pltpu.VMEM((seq, LANES), F32)] * 5
            + [pltpu.VMEM((2, BRANCH_UNROLL, 2, WIN, 2 * WIN), F32), pltpu.VMEM((4, WIN, 2 * WIN), BF16)]
        ),
        compiler_params=pltpu.CompilerParams(
            dimension_semantics=("arbitrary", "arbitrary"), vmem_limit_bytes=VMEM_LIMIT),
        name="dilated",
    )(qkva, qkva, qkva)


def _outproj_kernel(oa_ref, ob_ref, x_ref, wo_ref, g_ref, wr_ref, x1_ref, h2_ref, cls_ref):
    tm = x_ref.shape[0]
    x1 = (x_ref[...]
          + jnp.dot(oa_ref[...], wo_ref[0:WIDTH_A, :], preferred_element_type=F32)
          + jnp.dot(ob_ref[...], wo_ref[WIDTH_A:WIDTH_A + WIDTH_B, :], preferred_element_type=F32))
    x1_ref[...] = x1
    ms = jnp.mean(x1 * x1, axis=-1, keepdims=True)
    h2 = x1 * lax.rsqrt(ms + NORM_EPS) * g_ref[...]
    h2_ref[:, 0:D_MODEL] = h2
    h2_hi = h2.astype(BF16)
    h2_lo = (h2 - h2_hi.astype(F32)).astype(BF16)
    both = jnp.dot(h2_hi, wr_ref[...], preferred_element_type=F32)
    logits = (both[:, 0:LANES] + both[:, LANES:2 * LANES]
              + jnp.dot(h2_lo, wr_ref[:, 0:LANES], preferred_element_type=F32))
    lane = lax.broadcasted_iota(jnp.int32, (tm, LANES), 1)
    lane_f = lane.astype(F32)
    big = float(LANES)
    gmask = lane < N_GROUPS
    gl = jnp.where(gmask, logits, NEG_INF)
    gmax = jnp.max(gl, axis=1, keepdims=True)
    gsum = jnp.sum(jnp.where(gmask, jnp.exp(gl - gmax), 0.0), axis=1, keepdims=True)
    p_top = 1.0 / gsum
    g_star = jnp.min(jnp.where(gmask & (gl == gmax), lane_f, big), axis=1, keepdims=True)
    lo_lane = ROUTER_OFF + EXPERTS_PER_GROUP * g_star
    emask = (lane_f >= lo_lane) & (lane_f < lo_lane + EXPERTS_PER_GROUP)
    sel = jnp.where(emask, logits, NEG_INF)
    v1 = jnp.max(sel, axis=1, keepdims=True)
    i1 = jnp.min(jnp.where(emask & (sel == v1), lane_f, big), axis=1, keepdims=True)
    rest = emask & (lane_f != i1)
    sel2 = jnp.where(rest, logits, NEG_INF)
    v2 = jnp.max(sel2, axis=1, keepdims=True)
    i2 = jnp.min(jnp.where(rest & (sel2 == v2), lane_f, big), axis=1, keepdims=True)
    e2 = jnp.exp(v2 - v1)
    w1 = p_top / (1.0 + e2)
    w2 = p_top * e2 / (1.0 + e2)
    e1 = i1 - lo_lane
    e2x = i2 - lo_lane
    a = jnp.minimum(e1, e2x)
    b = jnp.maximum(e1, e2x)
    pair = jnp.zeros_like(a)
    for idx, (pa, pb) in enumerate(PAIR_ORDER):
        pair = jnp.where((a == pa) & (b == pb), float(idx), pair)
    cls = g_star * N_PAIRS + pair
    wa = jnp.where(e1 < e2x, w1, w2)
    wb = jnp.where(e1 < e2x, w2, w1)
    h2_ref[:, D_MODEL:EXT_WIDTH] = jnp.where(lane == 0, wa, jnp.where(lane == 1, wb, 0.0))
    row = lax.broadcasted_iota(jnp.int32, (tm, LANES), 0)
    spread = jnp.where(lane == (row & (LANES - 1)), cls, 0.0)
    cls_ref[0] = jnp.sum(spread.reshape(tm // LANES, LANES, LANES), axis=1).astype(jnp.int32)


def _outproj(oa, ob, x2, wo, g, wr, *, tm):
    n = x2.shape[0]
    return pl.pallas_call(
        _outproj_kernel,
        grid=(n // tm,),
        in_specs=[
            pl.BlockSpec((tm, WIDTH_A), lambda i: (i, 0)),
            pl.BlockSpec((tm, WIDTH_B), lambda i: (i, 0)),
            pl.BlockSpec((tm, D_MODEL), lambda i: (i, 0)),
            pl.BlockSpec(wo.shape, lambda i: (0, 0)),
            pl.BlockSpec((1, D_MODEL), lambda i: (0, 0)),
            pl.BlockSpec(wr.shape, lambda i: (0, 0)),
        ],
        out_specs=[
            pl.BlockSpec((tm, D_MODEL), lambda i: (i, 0)),
            pl.BlockSpec((tm, EXT_WIDTH), lambda i: (i, 0)),
            pl.BlockSpec((1, tm // LANES, LANES), lambda i: (i, 0, 0)),
        ],
        out_shape=[
            jax.ShapeDtypeStruct((n, D_MODEL), F32),
            jax.ShapeDtypeStruct((n, EXT_WIDTH), F32),
            jax.ShapeDtypeStruct((n // tm, tm // LANES, LANES), jnp.int32),
        ],
        compiler_params=pltpu.CompilerParams(dimension_semantics=("arbitrary",), vmem_limit_bytes=VMEM_LIMIT),
        name="outproj",
    )(oa, ob, x2, wo, g, wr)


def _sort_kernel(cls_ref, pos_ref, tile_ref, ends_ref, *, tile):
    cls = cls_ref[...]
    rows = cls.shape[0]
    upper = (lax.broadcasted_iota(jnp.int32, (LANES, LANES), 0)
             <= lax.broadcasted_iota(jnp.int32, (LANES, LANES), 1)).astype(BF16)
    earlier_rows = (lax.broadcasted_iota(jnp.int32, (rows, rows), 0)
                    > lax.broadcasted_iota(jnp.int32, (rows, rows), 1)).astype(BF16)
    ones = jnp.ones((LANES, LANES), BF16)
    lane8 = lax.broadcasted_iota(jnp.int32, tile_ref.shape, 1)
    tile_start = lane8.astype(F32) * tile
    off = jnp.zeros((1, 1), F32)
    pos = jnp.zeros(cls.shape, F32)
    tile_cls = jnp.zeros(tile_ref.shape, F32)
    ends = jnp.zeros(tile_ref.shape, F32)
    for c in range(N_CLASSES):
        hit = cls == c
        hot = jnp.where(hit, 1.0, 0.0).astype(BF16)
        in_row = jnp.dot(hot, upper, preferred_element_type=F32)
        row_tot = jnp.dot(hot, ones, preferred_element_type=F32)
        before = jnp.dot(earlier_rows, row_tot.astype(BF16), preferred_element_type=F32)
        count = before[rows - 1:rows, 0:1] + row_tot[rows - 1:rows, 0:1]
        pos = pos + jnp.where(hit, in_row - 1.0 + before + off, 0.0)
        off = off + jnp.ceil(count / tile) * tile
        tile_cls = tile_cls + jnp.where(tile_start >= off, 1.0, 0.0)
        ends = jnp.where(lane8 == c, off, ends)
    pos_ref[...] = pos.astype(jnp.int32)
    tile_ref[...] = tile_cls.astype(jnp.int32)
    ends_ref[...] = ends.astype(jnp.int32)


def _sort(cls2d, *, tile):
    meta = jax.ShapeDtypeStruct((8, LANES), jnp.int32)
    return pl.pallas_call(
        functools.partial(_sort_kernel, tile=tile),
        out_shape=[jax.ShapeDtypeStruct(cls2d.shape, jnp.int32), meta, meta],
        compiler_params=pltpu.CompilerParams(vmem_limit_bytes=VMEM_LIMIT),
        name="moe_sort",
    )(cls2d)


def _row_copy(src, src_row, dst, dst_row, sem, rows=1):
    return pltpu.make_async_copy(src.at[pl.ds(src_row, rows)], dst.at[pl.ds(dst_row, rows)], sem)


def _dispatch_kernel(pos_ref, ends_ref, h2_ref, xs_hbm, zero_scr, sem, *, ch, tile):
    g = pl.program_id(0)

    @pl.when(g == 0)
    def _zero_tails():
        zero_scr[...] = jnp.zeros_like(zero_scr)
        n_tiles = xs_hbm.shape[0] // tile
        min_used = n_tiles - N_CLASSES
        total = ends_ref[0, N_CLASSES - 1]
        for phase in ("start", "wait"):
            def zero_tile(row0):
                cp = pltpu.make_async_copy(zero_scr, xs_hbm.at[pl.ds(pl.multiple_of(row0, tile), tile)], sem)
                cp.start() if phase == "start" else cp.wait()

            for c in range(N_CLASSES):
                end = ends_ref[0, c]
                prev = ends_ref[0, c - 1] if c else 0
                pl.when(end > prev)(functools.partial(zero_tile, end - tile))
                pl.when((min_used + c) * tile >= total)(functools.partial(zero_tile, (min_used + c) * tile))

    for k in range(ch):
        _row_copy(h2_ref, k, xs_hbm, pos_ref[0, 0, k], sem).start(priority=k % DMA_THREADS)
    _row_copy(h2_ref, 0, xs_hbm, 0, sem, rows=ch).wait()


def _dispatch(pos3, ends, h2ext, *, n_rows, tile):
    n_chunks, _, ch = pos3.shape
    return pl.pallas_call(
        functools.partial(_dispatch_kernel, ch=ch, tile=tile),
        grid=(n_chunks,),
        in_specs=[
            pl.BlockSpec((1, 1, ch), lambda g: (g, 0, 0), memory_space=pltpu.SMEM),
            pl.BlockSpec(ends.shape, lambda g: (0, 0), memory_space=pltpu.SMEM),
            pl.BlockSpec((ch, EXT_WIDTH), lambda g: (g, 0)),
        ],
        out_specs=pl.BlockSpec(memory_space=pl.ANY),
        out_shape=jax.ShapeDtypeStruct((n_rows, EXT_WIDTH), F32),
        scratch_shapes=[pltpu.VMEM((tile, EXT_WIDTH), F32), pltpu.SemaphoreType.DMA(())],
        compiler_params=pltpu.CompilerParams(dimension_semantics=("arbitrary",), vmem_limit_bytes=VMEM_LIMIT),
        name="moe_dispatch",
    )(pos3, ends, h2ext)


def _experts_kernel(ea_ref, eb_ref, used_ref, xs_ref, wga, wua, wda, wgb, wub, wdb, ys_ref):
    del ea_ref, eb_ref

    @pl.when(pl.program_id(0) < used_ref[0])
    def _():
        x = xs_ref[:, 0:D_MODEL].astype(BF16)
        wts = xs_ref[:, D_MODEL:EXT_WIDTH]
        y = None
        for col, (wg, wu, wd) in enumerate(((wga, wua, wda), (wgb, wub, wdb))):
            gate = jnp.dot(x, wg[0], preferred_element_type=F32)
            up = jnp.dot(x, wu[0], preferred_element_type=F32)
            he = (gate / (1.0 + jnp.exp(-gate)) * up * wts[:, col:col + 1]).astype(BF16)
            t = jnp.dot(he, wd[0], preferred_element_type=F32)
            y = t if y is None else y + t
        ys_ref[...] = y

    @pl.when(pl.program_id(0) >= used_ref[0])
    def _():
        ys_ref[...] = jnp.zeros_like(ys_ref)


def _experts(ea, eb, used, xs, wg, wu, wd, *, tile):
    n_tiles = xs.shape[0] // tile
    rows = lambda j, ea, eb, used: (jnp.minimum(j, used[0] - 1), 0)
    first = lambda j, ea, eb, used: (ea[j], 0, 0)
    second = lambda j, ea, eb, used: (eb[j], 0, 0)
    up_spec = lambda m: pl.BlockSpec((1, D_MODEL, D_EXPERT), m)
    down_spec = lambda m: pl.BlockSpec((1, D_EXPERT, D_MODEL), m)
    return pl.pallas_call(
        _experts_kernel,
        grid_spec=pltpu.PrefetchScalarGridSpec(
            num_scalar_prefetch=3,
            grid=(n_tiles,),
            in_specs=[pl.BlockSpec((tile, EXT_WIDTH), rows),
                      up_spec(first), up_spec(first), down_spec(first),
                      up_spec(second), up_spec(second), down_spec(second)],
            out_specs=pl.BlockSpec((tile, D_MODEL), lambda j, ea, eb, used: (j, 0)),
        ),
        out_shape=jax.ShapeDtypeStruct((xs.shape[0], D_MODEL), F32),
        compiler_params=pltpu.CompilerParams(dimension_semantics=("arbitrary",), vmem_limit_bytes=VMEM_LIMIT),
        name="moe_experts",
    )(ea, eb, used, xs, wg, wu, wd, wg, wu, wd)


def _combine_kernel(pos_ref, next_pos_ref, x1_ref, g_ref, ys_hbm, out_ref, y_scr, sem):
    i = pl.program_id(0)
    tm = x1_ref.shape[0]
    slot = i & 1

    def gather(p_ref, s):
        for k in range(tm):
            _row_copy(ys_hbm, p_ref[0, 0, k], y_scr.at[s], k, sem.at[s]).start(priority=k % DMA_THREADS)

    @pl.when(i == 0)
    def _():
        gather(pos_ref, 0)

    @pl.when(i + 1 < pl.num_programs(0))
    def _():
        gather(next_pos_ref, 1 - slot)

    _row_copy(ys_hbm, 0, y_scr.at[slot], 0, sem.at[slot], rows=tm).wait()
    x2 = x1_ref[...] + y_scr[slot]
    ms = jnp.mean(x2 * x2, axis=-1, keepdims=True)
    out_ref[...] = x2 * lax.rsqrt(ms + NORM_EPS) * g_ref[...]


def _combine(pos3, x1, g, ys):
    n_chunks, _, tm = pos3.shape
    return pl.pallas_call(
        _combine_kernel,
        grid=(n_chunks,),
        in_specs=[
            pl.BlockSpec((1, 1, tm), lambda i: (i, 0, 0), memory_space=pltpu.SMEM),
            pl.BlockSpec((1, 1, tm), lambda i: (jnp.minimum(i + 1, n_chunks - 1), 0, 0), memory_space=pltpu.SMEM),
            pl.BlockSpec((tm, D_MODEL), lambda i: (i, 0)),
            pl.BlockSpec((1, D_MODEL), lambda i: (0, 0)),
            pl.BlockSpec(memory_space=pl.ANY),
        ],
        out_specs=pl.BlockSpec((tm, D_MODEL), lambda i: (i, 0)),
        out_shape=jax.ShapeDtypeStruct(x1.shape, F32),
        scratch_shapes=[pltpu.VMEM((2, tm, D_MODEL), F32), pltpu.SemaphoreType.DMA((2,))],
        compiler_params=pltpu.CompilerParams(dimension_semantics=("arbitrary",), vmem_limit_bytes=VMEM_LIMIT),
        name="moe_combine",
    )(pos3, pos3, x1, g, ys)


def _moe(h2ext, cls3, x1, wg, wu, wd, g, *, tile=MOE_TILE, dispatch_chunk=2048, combine_chunk=512):
    n = x1.shape[0]
    n_tiles = n // tile + N_CLASSES
    assert n_tiles <= LANES
    pos, tile_cls, ends = _sort(cls3.reshape(n // LANES, LANES), tile=tile)
    xs = _dispatch(pos.reshape(n // dispatch_chunk, 1, dispatch_chunk), ends, h2ext, n_rows=n_tiles * tile, tile=tile)
    used = ends[0, N_CLASSES - 1:N_CLASSES] // tile
    tc = tile_cls[0, :n_tiles]
    tc = jnp.where(jnp.arange(n_tiles) < used[0], tc, tc[jnp.maximum(used[0] - 1, 0)])
    pair_a = jnp.array([p[0] for p in PAIR_ORDER], jnp.int32)
    pair_b = jnp.array([p[1] for p in PAIR_ORDER], jnp.int32)
    ea = (tc // N_PAIRS) * EXPERTS_PER_GROUP + pair_a[tc % N_PAIRS]
    eb = (tc // N_PAIRS) * EXPERTS_PER_GROUP + pair_b[tc % N_PAIRS]
    ys = _experts(ea, eb, used, xs, wg, wu, wd, tile=tile)
    return _combine(pos.reshape(n // combine_chunk, 1, combine_chunk), x1, g, ys)


def _rotary_tables(seq):
    pos = np.arange(seq, dtype=np.float64)
    inv_freq = 1.0 / (ROPE_THETA ** (np.arange(0, ROT_DIM, 2, dtype=np.float64) / ROT_DIM))
    ang = pos[:, None] * inv_freq[None, :]
    cos, sin = np.cos(ang), np.sin(ang)
    zeros = np.zeros((seq, HEAD_DIM - ROT_DIM))
    zh = np.zeros((seq, ROT_HALF))
    cos_h = np.concatenate([cos, cos, np.ones((seq, HEAD_DIM - ROT_DIM))], axis=1)
    sa_h = np.concatenate([-sin, zh, zeros], axis=1)
    sb_h = np.concatenate([zh, sin, zeros], axis=1)
    tile = lambda t: jnp.asarray(np.concatenate([t, t], axis=1), F32)
    return tile(cos_h), tile(sa_h), tile(sb_h)


def kernel(x, attn_norm, w_in, b_forget, w_out, ffn_norm, w_group, w_expert, w_gate_e, w_up_e, w_down_e, final_norm):
    b, seq, d = x.shape
    assert d == D_MODEL and w_in.shape[0] == 1, "single-layer block"
    n = b * seq
    scale = HEAD_DIM ** -0.5 * LOG2E
    col_scale = np.ones((2 * QKV_WIDTH + LANES,), np.float32)
    col_scale[0:WIDTH_A] = scale
    col_scale[QKV_WIDTH:QKV_WIDTH + WIDTH_B] = scale
    w = (jnp.pad(w_in[0].T, ((0, LANES - HEADS_B), (0, 0))) * col_scale[:, None]).astype(BF16)
    bf = jnp.pad(b_forget[0].astype(F32), (0, LANES - HEADS_B))[None, :]
    cosv, sa, sb = _rotary_tables(seq)
    x2 = x.reshape(n, d)

    qkva, qkvb, c = _inproj(x2, attn_norm[0][None, :], w, cosv, sa, sb, bf, seq=seq, tm=512)
    out_a = _dilated(qkva.reshape(b, seq, QKV_WIDTH))
    out_b, wg, wu, wd = _fox(qkvb.reshape(b, seq, QKV_WIDTH), c.reshape(b, seq, LANES),
                             w_gate_e[0], w_up_e[0], w_down_e[0], tq=1024)

    wr = jnp.pad(jnp.concatenate([w_group[0], w_expert[0]], axis=1).astype(F32),
                 ((0, 0), (0, LANES - N_GROUPS - N_EXPERTS)))
    wr_hi = wr.astype(BF16)
    wr = jnp.concatenate([wr_hi, (wr - wr_hi.astype(F32)).astype(BF16)], axis=1)
    x1, h2ext, cls3 = _outproj(out_a.reshape(n, WIDTH_A), out_b.reshape(n, WIDTH_B), x2,
                               w_out[0].astype(BF16), ffn_norm[0][None, :], wr, tm=512)
    out = _moe(h2ext, cls3, x1, wg, wu, wd, final_norm[None, :])
    return out.reshape(b, seq, d)
```

```python
import functools

import jax
import jax.numpy as jnp
import numpy as np
from jax import lax
from jax.experimental import pallas as pl
from jax.experimental.pallas import tpu as pltpu

F32 = jnp.float32
BF16 = jnp.bfloat16

D_MODEL = 1024
HEAD_DIM = 64
HEADS_A = 8
HEADS_B = 8
WIDTH_A = HEADS_A * HEAD_DIM
WIDTH_B = HEADS_B * HEAD_DIM
QKV_WIDTH = 3 * WIDTH_A
DILATIONS = ((128, 1), (512, 4), (2048, 16))
ROT_DIM = HEAD_DIM // 4
ROT_HALF = ROT_DIM // 2
ROPE_THETA = 500000.0
N_GROUPS = 4
EXPERTS_PER_GROUP = 4
N_EXPERTS = N_GROUPS * EXPERTS_PER_GROUP
D_EXPERT = 512
NORM_EPS = 1e-6
NEG_INF = -1e30

LANES = 128
HEAD_PAIRS = WIDTH_A // LANES
WIN = 128
MAX_DIL = 16
MID_DIL = 4
N_MID = MAX_DIL // MID_DIL
assert sorted(d for _, d in DILATIONS) == [1, MID_DIL, MAX_DIL] and all(w // d == WIN for w, d in DILATIONS)
PERM = MAX_DIL * MAX_DIL
KV_BLOCK = 256
N_PAIRS = EXPERTS_PER_GROUP * (EXPERTS_PER_GROUP - 1) // 2
N_CLASSES = N_GROUPS * N_PAIRS
PAIR_ORDER = ((0, 1), (0, 2), (1, 2), (1, 3), (0, 3), (2, 3))
assert len(PAIR_ORDER) == N_PAIRS
MOE_TILE = 256
EXT_WIDTH = D_MODEL + LANES
LOGIT_MAX = 60.0
BRANCH_UNROLL = 8
DMA_THREADS = 2
ROUTER_OFF = N_GROUPS
VMEM_LIMIT = 56 * 1024 * 1024
LOG2E = 1.4426950408889634
BOUND_SLACK = 1.0 + 2.0 ** -6
BOUND_EPS = 2.0 ** -7


def _split3(x):
    hi = x.astype(BF16)
    r1 = x - hi.astype(F32)
    mid = r1.astype(BF16)
    lo = (r1 - mid.astype(F32)).astype(BF16)
    return hi, mid, lo


def _dot_nt(a, b):
    return lax.dot_general(a, b, (((1,), (1,)), ((), ())), preferred_element_type=F32)


def _inproj_kernel(x_ref, g_ref, w_ref, cos_ref, sa_ref, sb_ref, bf_ref,
                   qkva_ref, qkvb_ref, c_ref, h_scr, carry_scr, *, tiles_per_seq):
    i = pl.program_id(0)
    tm = x_ref.shape[0]
    x = x_ref[...]
    ms = jnp.mean(x * x, axis=-1, keepdims=True)
    h_scr[...] = (x * lax.rsqrt(ms + NORM_EPS) * g_ref[...]).astype(BF16)
    cosv = cos_ref[...]
    sa = sa_ref[...]
    sb = sb_ref[...]
    for g in range(6):
        p = _dot_nt(h_scr[...], w_ref[g * WIDTH_A:(g + 1) * WIDTH_A, :])
        dst = qkva_ref if g < 3 else qkvb_ref
        col0 = (g % 3) * WIDTH_A
        if g < 2:
            for k in range(HEAD_PAIRS):
                t = p[:, k * LANES:(k + 1) * LANES]
                t = (t * cosv + pltpu.roll(t, LANES - ROT_HALF, 1) * sa + pltpu.roll(t, ROT_HALF, 1) * sb)
                dst[:, col0 + k * LANES:col0 + (k + 1) * LANES] = t.astype(BF16)
        else:
            dst[:, col0:col0 + WIDTH_A] = p.astype(BF16)
    z = _dot_nt(h_scr[...], w_ref[2 * QKV_WIDTH:2 * QKV_WIDTH + LANES, :]) + bf_ref[...]
    lf = jnp.minimum(z, 0.0) - jnp.log1p(jnp.exp(-jnp.abs(z)))
    lane = lax.broadcasted_iota(jnp.int32, (tm, LANES), 1)
    lf = jnp.where(lane < HEADS_B, lf * LOG2E, 0.0)
    hi, mid, lo = _split3(lf)
    tri = (lax.broadcasted_iota(jnp.int32, (tm, tm), 0) >= lax.broadcasted_iota(jnp.int32, (tm, tm), 1)).astype(BF16)
    cs = (jnp.dot(tri, hi, preferred_element_type=F32) + jnp.dot(tri, mid, preferred_element_type=F32)
          + jnp.dot(tri, lo, preferred_element_type=F32))

    @pl.when(i % tiles_per_seq == 0)
    def _():
        carry_scr[...] = jnp.zeros_like(carry_scr)

    c = cs + carry_scr[0:1, :]
    c_ref[...] = c
    carry_scr[...] = jnp.broadcast_to(c[tm - 1:tm, :], carry_scr.shape)


def _inproj(x2, g, w, cosv, sa, sb, bf, *, seq, tm):
    n = x2.shape[0]
    kern = functools.partial(_inproj_kernel, tiles_per_seq=seq // tm)
    tps = seq // tm
    return pl.pallas_call(
        kern,
        grid=(n // tm,),
        in_specs=[
            pl.BlockSpec((tm, D_MODEL), lambda i: (i, 0)),
            pl.BlockSpec((1, D_MODEL), lambda i: (0, 0)),
            pl.BlockSpec(w.shape, lambda i: (0, 0)),
            pl.BlockSpec((tm, LANES), lambda i: (i % tps, 0)),
            pl.BlockSpec((tm, LANES), lambda i: (i % tps, 0)),
            pl.BlockSpec((tm, LANES), lambda i: (i % tps, 0)),
            pl.BlockSpec((1, LANES), lambda i: (0, 0)),
        ],
        out_specs=[
            pl.BlockSpec((tm, QKV_WIDTH), lambda i: (i, 0)),
            pl.BlockSpec((tm, QKV_WIDTH), lambda i: (i, 0)),
            pl.BlockSpec((tm, LANES), lambda i: (i, 0)),
        ],
        out_shape=[
            jax.ShapeDtypeStruct((n, QKV_WIDTH), BF16),
            jax.ShapeDtypeStruct((n, QKV_WIDTH), BF16),
            jax.ShapeDtypeStruct((n, LANES), F32),
        ],
        scratch_shapes=[pltpu.VMEM((tm, D_MODEL), BF16), pltpu.VMEM((8, LANES), F32)],
        compiler_params=pltpu.CompilerParams(dimension_semantics=("arbitrary",), vmem_limit_bytes=VMEM_LIMIT),
        name="inproj",
    )(x2, g, w, cosv, sa, sb, bf)


def _head_column(c_all, lane, head):
    return jnp.sum(jnp.where(lane == head, c_all, 0.0), axis=1, keepdims=True)


def _own_lanes(lane, hl):
    return (lane < HEAD_DIM) if hl == 0 else (lane >= HEAD_DIM)


def _head_sum_selector():
    lane = lax.broadcasted_iota(jnp.int32, (LANES, LANES), 0)
    col = lax.broadcasted_iota(jnp.int32, (LANES, LANES), 1)
    return jnp.where(((col == 0) & (lane < HEAD_DIM)) | ((col == 1) & (lane >= HEAD_DIM)), 1.0, 0.0).astype(BF16)


def _max_sq_norm(t, selector):
    return jnp.max(jnp.dot((t * t).astype(BF16), selector, preferred_element_type=F32), axis=0, keepdims=True)


def _norm_bound(qsq, ksq_max):
    return jnp.sqrt(qsq * ksq_max) * BOUND_SLACK + BOUND_EPS


def _fox_kernel(q_ref, k_ref, v_ref, c_ref, wg_ref, wu_ref, wd_ref, o_ref, wg_out, wu_out, wd_out,
                q0_scr, q1_scr, k0_scr, k1_scr, v0_scr, v1_scr, bound_scr, acc0_scr, acc1_scr, s_scr,
                *, tq, seq, chunk):
    hp = pl.program_id(1)
    qi = pl.program_id(2)
    for src, dst in ((wg_ref, wg_out), (wu_ref, wu_out), (wd_ref, wd_out)):
        dst[...] = src[...].astype(dst.dtype)
    q_scr = (q0_scr, q1_scr)
    k_scr = (k0_scr, k1_scr)
    v_scr = (v0_scr, v1_scr)
    acc_scr = (acc0_scr, acc1_scr)

    def augmented(vals, cterm, lane, hl, key_side):
        hi, mid, lo = (t.astype(F32) for t in _split3(cterm))
        base = HEAD_DIM if hl == 0 else 0
        ones = (lane >= base + (0 if key_side else 3)) & (lane < base + (3 if key_side else 6))
        o = 3 if key_side else 0
        sgn = -1.0 if key_side else 1.0
        aug = jnp.where(lane == base + o, sgn * hi,
                        jnp.where(lane == base + o + 1, sgn * mid,
                                  jnp.where(lane == base + o + 2, sgn * lo,
                                            jnp.where(ones, 1.0, 0.0))))
        return jnp.where(_own_lanes(lane, hl), vals, aug).astype(BF16)

    @pl.when(qi == 0)
    def _build():
        lane = lax.broadcasted_iota(jnp.int32, (chunk, LANES), 1)

        def body(ci, sq_max):
            rows = pl.ds(pl.multiple_of(ci * chunk, chunk), chunk)
            qc = q_ref[0, rows, :].astype(F32)
            kc = k_ref[0, rows, :].astype(F32)
            vc = v_ref[0, rows, :].astype(F32)
            cc = c_ref[0, rows, :]
            for hl in range(2):
                ccol = _head_column(cc, lane, 2 * hp + hl)
                q_scr[hl][rows, :] = augmented(qc, ccol, lane, hl, False)
                k_scr[hl][rows, :] = augmented(kc, ccol, lane, hl, True)
                one_lane = HEAD_DIM if hl == 0 else 0
                v_scr[hl][rows, :] = jnp.where(
                    _own_lanes(lane, hl), vc, jnp.where(lane == one_lane, 1.0, 0.0)).astype(BF16)
            return (jnp.maximum(sq_max[0], _max_sq_norm(qc, selector)),
                    jnp.maximum(sq_max[1], _max_sq_norm(kc, selector)))

        selector = _head_sum_selector()
        sq_max = lax.fori_loop(0, seq // chunk, body, (jnp.zeros((1, LANES), F32),) * 2)
        bound_scr[...] = jnp.broadcast_to(_norm_bound(sq_max[0], sq_max[1]), bound_scr.shape)

    lane_q = lax.broadcasted_iota(jnp.int32, (tq, LANES), 1)
    q_rows = pl.ds(pl.multiple_of(qi * tq, tq), tq)
    worst = jnp.max(bound_scr[...])
    n_diag = tq // KV_BLOCK
    n_full = qi * n_diag
    row_minus_col = (lax.broadcasted_iota(jnp.int32, (tq, KV_BLOCK), 0)
                     - lax.broadcasted_iota(jnp.int32, (tq, KV_BLOCK), 1))

    def causal_mask(s, d, top=0):
        return jnp.where(row_minus_col[top:tq, :] >= d * KV_BLOCK, s, NEG_INF)

    def logits(hl, j, top=0):
        r0 = pl.multiple_of(j * KV_BLOCK, KV_BLOCK)
        rows = q_rows if top == 0 else pl.ds(pl.multiple_of(qi * tq + top, KV_BLOCK), tq - top)
        return lax.dot_general(q_scr[hl][rows, :], k_scr[hl][pl.ds(r0, KV_BLOCK), :], (((1,), (1,)), ((), ())),
                               preferred_element_type=F32)

    def values(hl, j):
        return v_scr[hl][pl.ds(pl.multiple_of(j * KV_BLOCK, KV_BLOCK), KV_BLOCK), :]

    def finish(a0, a1):
        l0 = a0[:, HEAD_DIM:HEAD_DIM + 1]
        l1 = a1[:, 0:1]
        o_ref[0] = jnp.where(lane_q < HEAD_DIM, a0 / l0, a1 / l1).astype(o_ref.dtype)

    @pl.when(worst <= LOGIT_MAX)
    def _fast():
        acc0_scr[...] = jnp.zeros_like(acc0_scr)
        acc1_scr[...] = jnp.zeros_like(acc1_scr)

        def produce(j, slot, top=0):
            for hl in range(2):
                s_scr[slot, hl, top:tq, :] = logits(hl, j, top)

        def consume(j, slot, diag):
            top = 0 if diag is None else diag * KV_BLOCK
            for hl in range(2):
                s = s_scr[slot, hl, top:tq, :]
                if diag is not None:
                    s = causal_mask(s, diag, top)
                acc_scr[hl][top:tq, :] += jnp.dot(jnp.exp2(s).astype(BF16), values(hl, j),
                                                  preferred_element_type=F32)

        def produce_diagonal(d):
            produce(n_full + d, d, d * KV_BLOCK)

        @pl.when(qi == 0)
        def _():
            for d in range(n_diag):
                produce_diagonal(d)

        @pl.when(qi > 0)
        def _():
            for d in range(n_diag):
                produce(d, d)

            def body(i, carry):
                for d in range(n_diag):
                    j = i * n_diag + d
                    consume(j, d, None)
                    produce(j + n_diag, d)
                return carry

            lax.fori_loop(0, qi - 1, body, 0)
            for d in range(n_diag):
                consume(n_full - n_diag + d, d, None)
                produce_diagonal(d)

        for d in range(n_diag):
            consume(n_full + d, d, d)
        finish(acc0_scr[...], acc1_scr[...])

    @pl.when(worst > LOGIT_MAX)
    def _general():
        def step(j, carry, diag):
            new = []
            for hl in range(2):
                m, a = carry[2 * hl:2 * hl + 2]
                s = logits(hl, j)
                if diag is not None:
                    s = causal_mask(s, diag)
                mn = jnp.maximum(m, jnp.max(s, axis=1, keepdims=True))
                a = jnp.exp2(m - mn) * a + jnp.dot(jnp.exp2(s - mn).astype(BF16), values(hl, j),
                                                  preferred_element_type=F32)
                new += [mn, a]
            return tuple(new)

        carry = (jnp.full((tq, 1), NEG_INF, F32), jnp.zeros((tq, LANES), F32)) * 2
        carry = lax.fori_loop(0, n_full, lambda j, c: step(j, c, None), carry)
        for d in range(n_diag):
            carry = step(n_full + d, carry, d)
        finish(carry[1], carry[3])


def _fox(qkvb, c, wg, wu, wd, *, tq, chunk=512):
    b, seq, _ = qkvb.shape
    nq = seq // tq
    steps = b * HEAD_PAIRS * nq
    parts = steps // N_EXPERTS
    assert parts * N_EXPERTS == steps and D_MODEL % parts == 0 and D_EXPERT % parts == 0

    def slice_spec(rows, cols):
        def index(bi, hp, qi):
            step = (bi * HEAD_PAIRS + hp) * nq + qi
            return (step // parts, step % parts, 0)
        return pl.BlockSpec((1, rows // parts, cols), index)

    w_specs = [slice_spec(D_MODEL, D_EXPERT), slice_spec(D_MODEL, D_EXPERT), slice_spec(D_EXPERT, D_MODEL)]
    kern = functools.partial(_fox_kernel, tq=tq, seq=seq, chunk=chunk)
    return pl.pallas_call(
        kern,
        grid=(b, HEAD_PAIRS, nq),
        in_specs=[
            pl.BlockSpec((1, seq, LANES), lambda bi, hp, qi: (bi, 0, hp)),
            pl.BlockSpec((1, seq, LANES), lambda bi, hp, qi: (bi, 0, HEAD_PAIRS + hp)),
            pl.BlockSpec((1, seq, LANES), lambda bi, hp, qi: (bi, 0, 2 * HEAD_PAIRS + hp)),
            pl.BlockSpec((1, seq, LANES), lambda bi, hp, qi: (bi, 0, 0)),
        ] + w_specs,
        out_specs=[pl.BlockSpec((1, tq, LANES), lambda bi, hp, qi: (bi, qi, hp))] + w_specs,
        out_shape=[jax.ShapeDtypeStruct((b, seq, WIDTH_B), BF16)]
        + [jax.ShapeDtypeStruct(w.shape, BF16) for w in (wg, wu, wd)],
        scratch_shapes=[pltpu.VMEM((seq, LANES), BF16)] * 6 + [
                        pltpu.VMEM((8, LANES), F32),
                        pltpu.VMEM((tq, LANES), F32), pltpu.VMEM((tq, LANES), F32),
                        pltpu.VMEM((tq // KV_BLOCK, 2, tq, KV_BLOCK), F32)],
        compiler_params=pltpu.CompilerParams(
            dimension_semantics=("arbitrary", "arbitrary", "arbitrary"), vmem_limit_bytes=VMEM_LIMIT),
        name="fox",
    )(qkvb, qkvb, qkvb, c, wg, wu, wd)


def _log2(n):
    assert n & (n - 1) == 0
    return n.bit_length() - 1


def _permutation():
    ri = lax.broadcasted_iota(jnp.int32, (PERM, PERM), 0)
    ci = lax.broadcasted_iota(jnp.int32, (PERM, PERM), 1)
    sh, lo = _log2(MAX_DIL), MAX_DIL - 1
    return (((ri >> sh) == (ci & lo)) & ((ri & lo) == (ci >> sh))).astype(BF16)


def _band_masks():
    qa = lax.broadcasted_iota(jnp.int32, (WIN, 2 * WIN), 0)
    kb = lax.broadcasted_iota(jnp.int32, (WIN, 2 * WIN), 1)
    cur = kb >= WIN
    kin = kb & (WIN - 1)
    prev_shift = jnp.where(cur, 0, WIN)
    dist = qa - kin + prev_shift
    chunk = WIN // N_MID
    offset = lambda a: ((a & (chunk - 1)) << _log2(N_MID)) + (a >> _log2(chunk))
    dist_mid = offset(qa) - offset(kin) + prev_shift
    return (dist >= 0) & (dist <= WIN), (dist_mid >= 0) & (dist_mid <= WIN), cur


def _dilated_general(q_ref, k_ref, v_ref, o_ref, q16, k16, v16, m_s, l_s, acc_s, on_s, lse_s, *, seq):
    sub = seq // MAX_DIL
    own0 = lax.broadcasted_iota(jnp.int32, (WIN, LANES), 1) < HEAD_DIM
    perm = _permutation()
    band_nat, band_4, cur = _band_masks()

    def attend(qb, kb, vb, mask, state):
        qf = qb.astype(F32)
        new = []
        for hl in range(2):
            m, l, a = state[3 * hl:3 * hl + 3]
            own = own0 if hl == 0 else jnp.logical_not(own0)
            qh = jnp.where(own, qf, 0.0).astype(BF16)
            s = lax.dot_general(qh, kb, (((1,), (1,)), ((), ())), preferred_element_type=F32)
            s = jnp.where(mask, s, NEG_INF)
            mn = jnp.maximum(m, jnp.max(s, axis=1, keepdims=True))
            alpha = jnp.exp2(m - mn)
            p = jnp.exp2(s - mn)
            l = alpha * l + jnp.sum(p, axis=1, keepdims=True)
            a = alpha * a + jnp.dot(p.astype(BF16), vb, preferred_element_type=F32)
            new += [mn, l, a]
        return new

    def pack_state(st):
        m0, l0, a0, m1, l1, a1 = st
        return (jnp.where(own0, m0, m1), jnp.where(own0, l0, l1), jnp.where(own0, a0, a1))

    def deint(blk, carry):
        r0 = pl.multiple_of(blk * PERM, PERM)
        j0 = pl.multiple_of(blk * MAX_DIL, MAX_DIL)
        for src, dst in ((q_ref, q16), (k_ref, k16), (v_ref, v16)):
            y = jnp.dot(perm, src[0, pl.ds(r0, PERM), :], preferred_element_type=F32).astype(BF16)
            for r in range(MAX_DIL):
                dst[r, pl.ds(j0, MAX_DIL), :] = y[r * MAX_DIL:(r + 1) * MAX_DIL, :]
        return carry

    lax.fori_loop(0, seq // PERM, deint, 0)

    fresh = [jnp.full((WIN, 1), NEG_INF, F32), jnp.zeros((WIN, 1), F32), jnp.zeros((WIN, LANES), F32)] * 2
    nb16 = sub // WIN

    def d16_body(t, carry):
        r = t // nb16
        n = t % nb16
        c0 = pl.multiple_of(n * WIN, WIN)
        p0 = pl.multiple_of(jnp.maximum(n - 1, 0) * WIN, WIN)
        qb = q16[r, pl.ds(c0, WIN), :]
        kb = jnp.concatenate([k16[r, pl.ds(p0, WIN), :], k16[r, pl.ds(c0, WIN), :]], axis=0)
        vb = jnp.concatenate([v16[r, pl.ds(p0, WIN), :], v16[r, pl.ds(c0, WIN), :]], axis=0)
        mask = band_nat & (cur | (n > 0))
        mm, ll, aa = pack_state(attend(qb, kb, vb, mask, fresh))
        row = pl.multiple_of(r * sub + c0, WIN)
        m_s[pl.ds(row, WIN), :] = mm
        l_s[pl.ds(row, WIN), :] = ll
        acc_s[pl.ds(row, WIN), :] = aa
        return carry

    lax.fori_loop(0, MAX_DIL * nb16, d16_body, 0)

    nb4 = (seq // MID_DIL) // WIN
    ch = WIN // N_MID

    def d4_body(t, carry):
        r4 = t // nb4
        n = t % nb4
        c0 = pl.multiple_of(n * ch, ch)
        p0 = pl.multiple_of(jnp.maximum(n - 1, 0) * ch, ch)

        def gather(ref, j0):
            return [ref[r4 + MID_DIL * q, pl.ds(j0, ch), :] for q in range(N_MID)]

        qb = jnp.concatenate(gather(q16, c0), axis=0)
        kb = jnp.concatenate(gather(k16, p0) + gather(k16, c0), axis=0)
        vb = jnp.concatenate(gather(v16, p0) + gather(v16, c0), axis=0)
        rows = [pl.multiple_of((r4 + MID_DIL * q) * sub + c0, ch) for q in range(N_MID)]
        mm = jnp.concatenate([m_s[pl.ds(rw, ch), :] for rw in rows], axis=0)
        ll = jnp.concatenate([l_s[pl.ds(rw, ch), :] for rw in rows], axis=0)
        aa = jnp.concatenate([acc_s[pl.ds(rw, ch), :] for rw in rows], axis=0)
        state = [mm[:, 0:1], ll[:, 0:1], aa, mm[:, HEAD_DIM:HEAD_DIM + 1], ll[:, HEAD_DIM:HEAD_DIM + 1], aa]
        mask = band_4 & (cur | (n > 0))
        mm, ll, aa = pack_state(attend(qb, kb, vb, mask, state))
        for q, rw in enumerate(rows):
            m_s[pl.ds(rw, ch), :] = mm[q * ch:(q + 1) * ch, :]
            l_s[pl.ds(rw, ch), :] = ll[q * ch:(q + 1) * ch, :]
            acc_s[pl.ds(rw, ch), :] = aa[q * ch:(q + 1) * ch, :]
        return carry

    lax.fori_loop(0, MID_DIL * nb4, d4_body, 0)

    def renat(blk, carry):
        j0 = pl.multiple_of(blk * MAX_DIL, MAX_DIL)
        r0 = pl.multiple_of(blk * PERM, PERM)
        rows = [pl.multiple_of(r * sub + j0, MAX_DIL) for r in range(MAX_DIL)]
        mm = jnp.concatenate([m_s[pl.ds(rw, MAX_DIL), :] for rw in rows], axis=0)
        ll = jnp.concatenate([l_s[pl.ds(rw, MAX_DIL), :] for rw in rows], axis=0)
        aa = jnp.concatenate([acc_s[pl.ds(rw, MAX_DIL), :] for rw in rows], axis=0)
        o = (aa / ll).astype(BF16)
        hi, mid, lo = _split3(mm + jnp.log2(ll))
        on_s[pl.ds(r0, PERM), :] = jnp.dot(perm, o, preferred_element_type=F32)
        lse_s[pl.ds(r0, PERM), :] = (jnp.dot(perm, hi, preferred_element_type=F32)
                                     + jnp.dot(perm, mid, preferred_element_type=F32)
                                     + jnp.dot(perm, lo, preferred_element_type=F32))
        return carry

    lax.fori_loop(0, seq // PERM, renat, 0)

    def d1_body(n, carry):
        c0 = pl.multiple_of(n * WIN, WIN)
        p0 = pl.multiple_of(jnp.maximum(n - 1, 0) * WIN, WIN)
        qb = q_ref[0, pl.ds(c0, WIN), :]
        kb = jnp.concatenate([k_ref[0, pl.ds(p0, WIN), :], k_ref[0, pl.ds(c0, WIN), :]], axis=0)
        vb = jnp.concatenate([v_ref[0, pl.ds(p0, WIN), :], v_ref[0, pl.ds(c0, WIN), :]], axis=0)
        lse = lse_s[pl.ds(c0, WIN), :]
        o = on_s[pl.ds(c0, WIN), :]
        one = jnp.ones((WIN, 1), F32)
        state = [lse[:, 0:1], one, o, lse[:, HEAD_DIM:HEAD_DIM + 1], one, o]
        mask = band_nat & (cur | (n > 0))
        m0, l0, a0, m1, l1, a1 = attend(qb, kb, vb, mask, state)
        o_ref[0, pl.ds(c0, WIN), :] = jnp.where(own0, a0 / l0, a1 / l1).astype(o_ref.dtype)
        return carry

    lax.fori_loop(0, seq // WIN, d1_body, 0)


def _dilated_fast(o_ref, qn, kn, vn, qr, kr, vr, fr, fn, s_scr, mask_scr, *, seq):
    sub = seq // MAX_DIL
    own0 = lax.broadcasted_iota(jnp.int32, (WIN, LANES), 1) < HEAD_DIM
    head_mask = [jnp.where(own0, 1.0, 0.0).astype(BF16), jnp.where(own0, 0.0, 1.0).astype(BF16)]
    perm = _permutation()

    band_nat, band_4, cur = _band_masks()
    for i, band in enumerate((band_nat, band_4)):
        mask_scr[2 * i] = jnp.where(band & cur, 1.0, 0.0).astype(BF16)
        mask_scr[2 * i + 1] = jnp.where(band, 1.0, 0.0).astype(BF16)

    def run_branch(nblocks, block_in_seq, fetch, mask_base, sink, before_trip=None):
        def produce(t, pair, u):
            for hl in range(2):
                qb, kb = fetch(t, hl, True)
                s_scr[pair, u, hl] = lax.dot_general(qb * head_mask[hl], kb, (((1,), (1,)), ((), ())),
                                                     preferred_element_type=F32)

        def consume(t, pair, u):
            mk = mask_scr[mask_base + jnp.minimum(block_in_seq(t), 1)]
            pv = []
            for hl in range(2):
                p = jnp.exp2(s_scr[pair, u, hl]).astype(BF16) * mk
                pv.append(jnp.dot(p, fetch(t, hl, False), preferred_element_type=F32))
            sink(t, pv)

        for u in range(BRANCH_UNROLL):
            produce(u, 0, u)

        def trip(i, last):
            pair = i & 1
            if before_trip is not None:
                before_trip(i)
            for u in range(BRANCH_UNROLL):
                consume(BRANCH_UNROLL * i + u, pair, u)
            if not last:
                for u in range(BRANCH_UNROLL):
                    produce(BRANCH_UNROLL * (i + 1) + u, 1 - pair, u)

        def body(i, carry):
            trip(i, False)
            return carry

        trips = nblocks // BRANCH_UNROLL
        lax.fori_loop(0, trips - 1, body, 0)
        trip(jnp.int32(trips - 1), True)

    def prev_cur(ref_block, n, size):
        c0 = pl.multiple_of(n * size, size)
        p0 = pl.multiple_of(jnp.maximum(n - 1, 0) * size, size)
        return ref_block(p0) + ref_block(c0)

    nb16 = sub // WIN

    def fetch16(t, hl, qk):
        r, n = t // nb16, t % nb16
        if qk:
            return (qr[hl][r, pl.ds(pl.multiple_of(n * WIN, WIN), WIN), :],
                    jnp.concatenate(prev_cur(lambda s: [kr[hl][r, pl.ds(s, WIN), :]], n, WIN), axis=0))
        return jnp.concatenate(prev_cur(lambda s: [vr[hl][r, pl.ds(s, WIN), :]], n, WIN), axis=0)

    def sink16(t, pv):
        row = pl.multiple_of(t * WIN, WIN)
        for hl in range(2):
            fr[hl][pl.ds(row, WIN), :] = pv[hl]

    run_branch(MAX_DIL * nb16, lambda t: t % nb16, fetch16, 0, sink16)

    nb4 = (seq // MID_DIL) // WIN
    ch = WIN // N_MID

    def fetch4(t, hl, qk):
        r4, n = t // nb4, t % nb4
        gather = lambda ref: (lambda s: [ref[r4 + MID_DIL * q, pl.ds(s, ch), :] for q in range(N_MID)])
        if qk:
            return (jnp.concatenate(gather(qr[hl])(pl.multiple_of(n * ch, ch)), axis=0),
                    jnp.concatenate(prev_cur(gather(kr[hl]), n, ch), axis=0))
        return jnp.concatenate(prev_cur(gather(vr[hl]), n, ch), axis=0)

    def sink4(t, pv):
        r4, n = t // nb4, t % nb4
        for q in range(N_MID):
            row = pl.multiple_of((r4 + MID_DIL * q) * sub + n * ch, ch)
            for hl in range(2):
                fr[hl][pl.ds(row, ch), :] += pv[hl][q * ch:(q + 1) * ch, :]

    run_branch(MID_DIL * nb4, lambda t: t % nb4, fetch4, 2, sink4)

    def renat(i):
        for u in range(BRANCH_UNROLL * WIN // PERM):
            blk = i * (BRANCH_UNROLL * WIN // PERM) + u
            j0 = pl.multiple_of(blk * MAX_DIL, MAX_DIL)
            r0 = pl.multiple_of(blk * PERM, PERM)
            for hl in range(2):
                a = jnp.concatenate([fr[hl][pl.ds(pl.multiple_of(r * sub + j0, MAX_DIL), MAX_DIL), :]
                                     for r in range(MAX_DIL)], axis=0)
                hi = a.astype(BF16)
                lo = (a - hi.astype(F32)).astype(BF16)
                y = jnp.dot(perm, jnp.concatenate([hi, lo], axis=1), preferred_element_type=F32)
                fn[hl][pl.ds(r0, PERM), :] = y[:, 0:LANES] + y[:, LANES:2 * LANES]

    def fetch1(t, hl, qk):
        if qk:
            return (qn[hl][pl.ds(pl.multiple_of(t * WIN, WIN), WIN), :],
                    jnp.concatenate(prev_cur(lambda s: [kn[hl][pl.ds(s, WIN), :]], t, WIN), axis=0))
        return jnp.concatenate(prev_cur(lambda s: [vn[hl][pl.ds(s, WIN), :]], t, WIN), axis=0)

    def sink1(t, pv):
        row = pl.multiple_of(t * WIN, WIN)
        t0 = fn[0][pl.ds(row, WIN), :] + pv[0]
        t1 = fn[1][pl.ds(row, WIN), :] + pv[1]
        o_ref[0, pl.ds(row, WIN), :] = jnp.where(own0, t0 / t0[:, HEAD_DIM:HEAD_DIM + 1],
                                                 t1 / t1[:, 0:1]).astype(o_ref.dtype)

    run_branch(seq // WIN, lambda t: t, fetch1, 0, sink1, before_trip=renat)


def _dilated_kernel(q_ref, k_ref, v_ref, o_ref, vn0, vn1, qr, kr, vr0, vr1,
                    f0, f1, f2, f3, f4, s_scr, mask_scr, *, seq, chunk):
    lane = lax.broadcasted_iota(jnp.int32, (chunk, LANES), 1)
    vn = (vn0, vn1)

    perm = _permutation()

    def stats(ci, sq_max):
        rows = pl.ds(pl.multiple_of(ci * chunk, chunk), chunk)
        qb = q_ref[0, rows, :]
        kb = k_ref[0, rows, :]
        vc = v_ref[0, rows, :].astype(F32)
        vb = []
        for hl in range(2):
            ones_lane = HEAD_DIM if hl == 0 else 0
            vb.append(jnp.where(_own_lanes(lane, hl), vc, jnp.where(lane == ones_lane, 1.0, 0.0)).astype(BF16))
            vn[hl][rows, :] = vb[hl]
        for u in range(chunk // PERM):
            j0 = pl.multiple_of((ci * (chunk // PERM) + u) * MAX_DIL, MAX_DIL)
            for srcs, dsts in (((qb, kb), (qr, kr)), (vb, (vr0, vr1))):
                both = jnp.concatenate([src[u * PERM:(u + 1) * PERM, :] for src in srcs], axis=1)
                y = jnp.dot(perm, both, preferred_element_type=F32).astype(BF16)
                for r in range(MAX_DIL):
                    for half, dst in enumerate(dsts):
                        dst[r, pl.ds(j0, MAX_DIL), :] = y[r * MAX_DIL:(r + 1) * MAX_DIL, half * LANES:(half + 1) * LANES]
        return (jnp.maximum(sq_max[0], _max_sq_norm(qb.astype(F32), selector)),
                jnp.maximum(sq_max[1], _max_sq_norm(kb.astype(F32), selector)))

    selector = _head_sum_selector()
    sq_max = lax.fori_loop(0, seq // chunk, stats, (jnp.zeros((1, LANES), F32),) * 2)
    worst = jnp.max(_norm_bound(sq_max[0], sq_max[1]))

    @pl.when(worst <= LOGIT_MAX)
    def _fast():
        qn, kn = q_ref.at[0], k_ref.at[0]
        _dilated_fast(o_ref, (qn, qn), (kn, kn), vn, (qr, qr), (kr, kr), (vr0, vr1), (f0, f1), (f2, f3),
                      s_scr, mask_scr, seq=seq)

    @pl.when(worst > LOGIT_MAX)
    def _general():
        _dilated_general(q_ref, k_ref, v_ref, o_ref, qr, kr, vr0, f0, f1, f2, f3, f4, seq=seq)


def _dilated(qkva, *, chunk=512):
    b, seq, _ = qkva.shape
    sub = seq // MAX_DIL
    kern = functools.partial(_dilated_kernel, seq=seq, chunk=chunk)
    return pl.pallas_call(
        kern,
        grid=(b, HEAD_PAIRS),
        in_specs=[
            pl.BlockSpec((1, seq, LANES), lambda bi, hp: (bi, 0, hp)),
            pl.BlockSpec((1, seq, LANES), lambda bi, hp: (bi, 0, HEAD_PAIRS + hp)),
            pl.BlockSpec((1, seq, LANES), lambda bi, hp: (bi, 0, 2 * HEAD_PAIRS + hp)),
        ],
        out_specs=pl.BlockSpec((1, seq, LANES), lambda bi, hp: (bi, 0, hp)),
        out_shape=jax.ShapeDtypeStruct((b, seq, WIDTH_A), BF16),
        scratch_shapes=(
            [pltpu.VMEM((seq, LANES), BF16)] * 2
            + [pltpu.VMEM((MAX_DIL, sub, LANES), BF16)] * 4
            + [pltpu.VMEM((seq, LANES), F32)] * 5
            + [pltpu.VMEM((2, BRANCH_UNROLL, 2, WIN, 2 * WIN), F32), pltpu.VMEM((4, WIN, 2 * WIN), BF16)]
        ),
        compiler_params=pltpu.CompilerParams(
            dimension_semantics=("arbitrary", "arbitrary"), vmem_limit_bytes=VMEM_LIMIT),
        name="dilated",
    )(qkva, qkva, qkva)


def _outproj_kernel(oa_ref, ob_ref, x_ref, wo_ref, g_ref, wr_ref, x1_ref, h2_ref, cls_ref):
    tm = x_ref.shape[0]
    x1 = (x_ref[...]
          + jnp.dot(oa_ref[...], wo_ref[0:WIDTH_A, :], preferred_element_type=F32)
          + jnp.dot(ob_ref[...], wo_ref[WIDTH_A:WIDTH_A + WIDTH_B, :], preferred_element_type=F32))
    x1_ref[...] = x1
    ms = jnp.mean(x1 * x1, axis=-1, keepdims=True)
    h2 = x1 * lax.rsqrt(ms + NORM_EPS) * g_ref[...]
    h2_ref[:, 0:D_MODEL] = h2
    h2_hi = h2.astype(BF16)
    h2_lo = (h2 - h2_hi.astype(F32)).astype(BF16)
    both = jnp.dot(h2_hi, wr_ref[...], preferred_element_type=F32)
    logits = (both[:, 0:LANES] + both[:, LANES:2 * LANES]
              + jnp.dot(h2_lo, wr_ref[:, 0:LANES], preferred_element_type=F32))
    lane = lax.broadcasted_iota(jnp.int32, (tm, LANES), 1)
    lane_f = lane.astype(F32)
    big = float(LANES)
    gmask = lane < N_GROUPS
    gl = jnp.where(gmask, logits, NEG_INF)
    gmax = jnp.max(gl, axis=1, keepdims=True)
    gsum = jnp.sum(jnp.where(gmask, jnp.exp(gl - gmax), 0.0), axis=1, keepdims=True)
    p_top = 1.0 / gsum
    g_star = jnp.min(jnp.where(gmask & (gl == gmax), lane_f, big), axis=1, keepdims=True)
    lo_lane = ROUTER_OFF + EXPERTS_PER_GROUP * g_star
    emask = (lane_f >= lo_lane) & (lane_f < lo_lane + EXPERTS_PER_GROUP)
    sel = jnp.where(emask, logits, NEG_INF)
    v1 = jnp.max(sel, axis=1, keepdims=True)
    i1 = jnp.min(jnp.where(emask & (sel == v1), lane_f, big), axis=1, keepdims=True)
    rest = emask & (lane_f != i1)
    sel2 = jnp.where(rest, logits, NEG_INF)
    v2 = jnp.max(sel2, axis=1, keepdims=True)
    i2 = jnp.min(jnp.where(rest & (sel2 == v2), lane_f, big), axis=1, keepdims=True)
    e2 = jnp.exp(v2 - v1)
    w1 = p_top / (1.0 + e2)
    w2 = p_top * e2 / (1.0 + e2)
    e1 = i1 - lo_lane
    e2x = i2 - lo_lane
    a = jnp.minimum(e1, e2x)
    b = jnp.maximum(e1, e2x)
    pair = jnp.zeros_like(a)
    for idx, (pa, pb) in enumerate(PAIR_ORDER):
        pair = jnp.where((a == pa) & (b == pb), float(idx), pair)
    cls = g_star * N_PAIRS + pair
    wa = jnp.where(e1 < e2x, w1, w2)
    wb = jnp.where(e1 < e2x, w2, w1)
    h2_ref[:, D_MODEL:EXT_WIDTH] = jnp.where(lane == 0, wa, jnp.where(lane == 1, wb, 0.0))
    row = lax.broadcasted_iota(jnp.int32, (tm, LANES), 0)
    spread = jnp.where(lane == (row & (LANES - 1)), cls, 0.0)
    cls_ref[0] = jnp.sum(spread.reshape(tm // LANES, LANES, LANES), axis=1).astype(jnp.int32)


def _outproj(oa, ob, x2, wo, g, wr, *, tm):
    n = x2.shape[0]
    return pl.pallas_call(
        _outproj_kernel,
        grid=(n // tm,),
        in_specs=[
            pl.BlockSpec((tm, WIDTH_A), lambda i: (i, 0)),
            pl.BlockSpec((tm, WIDTH_B), lambda i: (i, 0)),
            pl.BlockSpec((tm, D_MODEL), lambda i: (i, 0)),
            pl.BlockSpec(wo.shape, lambda i: (0, 0)),
            pl.BlockSpec((1, D_MODEL), lambda i: (0, 0)),
            pl.BlockSpec(wr.shape, lambda i: (0, 0)),
        ],
        out_specs=[
            pl.BlockSpec((tm, D_MODEL), lambda i: (i, 0)),
            pl.BlockSpec((tm, EXT_WIDTH), lambda i: (i, 0)),
            pl.BlockSpec((1, tm // LANES, LANES), lambda i: (i, 0, 0)),
        ],
        out_shape=[
            jax.ShapeDtypeStruct((n, D_MODEL), F32),
            jax.ShapeDtypeStruct((n, EXT_WIDTH), F32),
            jax.ShapeDtypeStruct((n // tm, tm // LANES, LANES), jnp.int32),
        ],
        compiler_params=pltpu.CompilerParams(dimension_semantics=("arbitrary",), vmem_limit_bytes=VMEM_LIMIT),
        name="outproj",
    )(oa, ob, x2, wo, g, wr)


def _sort_kernel(cls_ref, pos_ref, tile_ref, ends_ref, *, tile):
    cls = cls_ref[...]
    rows = cls.shape[0]
    upper = (lax.broadcasted_iota(jnp.int32, (LANES, LANES), 0)
             <= lax.broadcasted_iota(jnp.int32, (LANES, LANES), 1)).astype(BF16)
    earlier_rows = (lax.broadcasted_iota(jnp.int32, (rows, rows), 0)
                    > lax.broadcasted_iota(jnp.int32, (rows, rows), 1)).astype(BF16)
    ones = jnp.ones((LANES, LANES), BF16)
    lane8 = lax.broadcasted_iota(jnp.int32, tile_ref.shape, 1)
    tile_start = lane8.astype(F32) * tile
    off = jnp.zeros((1, 1), F32)
    pos = jnp.zeros(cls.shape, F32)
    tile_cls = jnp.zeros(tile_ref.shape, F32)
    ends = jnp.zeros(tile_ref.shape, F32)
    for c in range(N_CLASSES):
        hit = cls == c
        hot = jnp.where(hit, 1.0, 0.0).astype(BF16)
        in_row = jnp.dot(hot, upper, preferred_element_type=F32)
        row_tot = jnp.dot(hot, ones, preferred_element_type=F32)
        before = jnp.dot(earlier_rows, row_tot.astype(BF16), preferred_element_type=F32)
        count = before[rows - 1:rows, 0:1] + row_tot[rows - 1:rows, 0:1]
        pos = pos + jnp.where(hit, in_row - 1.0 + before + off, 0.0)
        off = off + jnp.ceil(count / tile) * tile
        tile_cls = tile_cls + jnp.where(tile_start >= off, 1.0, 0.0)
        ends = jnp.where(lane8 == c, off, ends)
    pos_ref[...] = pos.astype(jnp.int32)
    tile_ref[...] = tile_cls.astype(jnp.int32)
    ends_ref[...] = ends.astype(jnp.int32)


def _sort(cls2d, *, tile):
    meta = jax.ShapeDtypeStruct((8, LANES), jnp.int32)
    return pl.pallas_call(
        functools.partial(_sort_kernel, tile=tile),
        out_shape=[jax.ShapeDtypeStruct(cls2d.shape, jnp.int32), meta, meta],
        compiler_params=pltpu.CompilerParams(vmem_limit_bytes=VMEM_LIMIT),
        name="moe_sort",
    )(cls2d)


def _row_copy(src, src_row, dst, dst_row, sem, rows=1):
    return pltpu.make_async_copy(src.at[pl.ds(src_row, rows)], dst.at[pl.ds(dst_row, rows)], sem)


def _dispatch_kernel(pos_ref, ends_ref, h2_ref, xs_hbm, zero_scr, sem, *, ch, tile):
    g = pl.program_id(0)

    @pl.when(g == 0)
    def _zero_tails():
        zero_scr[...] = jnp.zeros_like(zero_scr)
        n_tiles = xs_hbm.shape[0] // tile
        min_used = n_tiles - N_CLASSES
        total = ends_ref[0, N_CLASSES - 1]
        for phase in ("start", "wait"):
            def zero_tile(row0):
                cp = pltpu.make_async_copy(zero_scr, xs_hbm.at[pl.ds(pl.multiple_of(row0, tile), tile)], sem)
                cp.start() if phase == "start" else cp.wait()

            for c in range(N_CLASSES):
                end = ends_ref[0, c]
                prev = ends_ref[0, c - 1] if c else 0
                pl.when(end > prev)(functools.partial(zero_tile, end - tile))
                pl.when((min_used + c) * tile >= total)(functools.partial(zero_tile, (min_used + c) * tile))

    for k in range(ch):
        _row_copy(h2_ref, k, xs_hbm, pos_ref[0, 0, k], sem).start(priority=k % DMA_THREADS)
    _row_copy(h2_ref, 0, xs_hbm, 0, sem, rows=ch).wait()


def _dispatch(pos3, ends, h2ext, *, n_rows, tile):
    n_chunks, _, ch = pos3.shape
    return pl.pallas_call(
        functools.partial(_dispatch_kernel, ch=ch, tile=tile),
        grid=(n_chunks,),
        in_specs=[
            pl.BlockSpec((1, 1, ch), lambda g: (g, 0, 0), memory_space=pltpu.SMEM),
            pl.BlockSpec(ends.shape, lambda g: (0, 0), memory_space=pltpu.SMEM),
            pl.BlockSpec((ch, EXT_WIDTH), lambda g: (g, 0)),
        ],
        out_specs=pl.BlockSpec(memory_space=pl.ANY),
        out_shape=jax.ShapeDtypeStruct((n_rows, EXT_WIDTH), F32),
        scratch_shapes=[pltpu.VMEM((tile, EXT_WIDTH), F32), pltpu.SemaphoreType.DMA(())],
        compiler_params=pltpu.CompilerParams(dimension_semantics=("arbitrary",), vmem_limit_bytes=VMEM_LIMIT),
        name="moe_dispatch",
    )(pos3, ends, h2ext)


def _experts_kernel(ea_ref, eb_ref, used_ref, xs_ref, wga, wua, wda, wgb, wub, wdb, ys_ref):
    del ea_ref, eb_ref

    @pl.when(pl.program_id(0) < used_ref[0])
    def _():
        x = xs_ref[:, 0:D_MODEL].astype(BF16)
        wts = xs_ref[:, D_MODEL:EXT_WIDTH]
        y = None
        for col, (wg, wu, wd) in enumerate(((wga, wua, wda), (wgb, wub, wdb))):
            gate = jnp.dot(x, wg[0], preferred_element_type=F32)
            up = jnp.dot(x, wu[0], preferred_element_type=F32)
            he = (gate / (1.0 + jnp.exp(-gate)) * up * wts[:, col:col + 1]).astype(BF16)
            t = jnp.dot(he, wd[0], preferred_element_type=F32)
            y = t if y is None else y + t
        ys_ref[...] = y

    @pl.when(pl.program_id(0) >= used_ref[0])
    def _():
        ys_ref[...] = jnp.zeros_like(ys_ref)


def _experts(ea, eb, used, xs, wg, wu, wd, *, tile):
    n_tiles = xs.shape[0] // tile
    rows = lambda j, ea, eb, used: (jnp.minimum(j, used[0] - 1), 0)
    first = lambda j, ea, eb, used: (ea[j], 0, 0)
    second = lambda j, ea, eb, used: (eb[j], 0, 0)
    up_spec = lambda m: pl.BlockSpec((1, D_MODEL, D_EXPERT), m)
    down_spec = lambda m: pl.BlockSpec((1, D_EXPERT, D_MODEL), m)
    return pl.pallas_call(
        _experts_kernel,
        grid_spec=pltpu.PrefetchScalarGridSpec(
            num_scalar_prefetch=3,
            grid=(n_tiles,),
            in_specs=[pl.BlockSpec((tile, EXT_WIDTH), rows),
                      up_spec(first), up_spec(first), down_spec(first),
                      up_spec(second), up_spec(second), down_spec(second)],
            out_specs=pl.BlockSpec((tile, D_MODEL), lambda j, ea, eb, used: (j, 0)),
        ),
        out_shape=jax.ShapeDtypeStruct((xs.shape[0], D_MODEL), F32),
        compiler_params=pltpu.CompilerParams(dimension_semantics=("arbitrary",), vmem_limit_bytes=VMEM_LIMIT),
        name="moe_experts",
    )(ea, eb, used, xs, wg, wu, wd, wg, wu, wd)


def _combine_kernel(pos_ref, next_pos_ref, x1_ref, g_ref, ys_hbm, out_ref, y_scr, sem):
    i = pl.program_id(0)
    tm = x1_ref.shape[0]
    slot = i & 1

    def gather(p_ref, s):
        for k in range(tm):
            _row_copy(ys_hbm, p_ref[0, 0, k], y_scr.at[s], k, sem.at[s]).start(priority=k % DMA_THREADS)

    @pl.when(i == 0)
    def _():
        gather(pos_ref, 0)

    @pl.when(i + 1 < pl.num_programs(0))
    def _():
        gather(next_pos_ref, 1 - slot)

    _row_copy(ys_hbm, 0, y_scr.at[slot], 0, sem.at[slot], rows=tm).wait()
    x2 = x1_ref[...] + y_scr[slot]
    ms = jnp.mean(x2 * x2, axis=-1, keepdims=True)
    out_ref[...] = x2 * lax.rsqrt(ms + NORM_EPS) * g_ref[...]


def _combine(pos3, x1, g, ys):
    n_chunks, _, tm = pos3.shape
    return pl.pallas_call(
        _combine_kernel,
        grid=(n_chunks,),
        in_specs=[
            pl.BlockSpec((1, 1, tm), lambda i: (i, 0, 0), memory_space=pltpu.SMEM),
            pl.BlockSpec((1, 1, tm), lambda i: (jnp.minimum(i + 1, n_chunks - 1), 0, 0), memory_space=pltpu.SMEM),
            pl.BlockSpec((tm, D_MODEL), lambda i: (i, 0)),
            pl.BlockSpec((1, D_MODEL), lambda i: (0, 0)),
            pl.BlockSpec(memory_space=pl.ANY),
        ],
        out_specs=pl.BlockSpec((tm, D_MODEL), lambda i: (i, 0)),
        out_shape=jax.ShapeDtypeStruct(x1.shape, F32),
        scratch_shapes=[pltpu.VMEM((2, tm, D_MODEL), F32), pltpu.SemaphoreType.DMA((2,))],
        compiler_params=pltpu.CompilerParams(dimension_semantics=("arbitrary",), vmem_limit_bytes=VMEM_LIMIT),
        name="moe_combine",
    )(pos3, pos3, x1, g, ys)


def _moe(h2ext, cls3, x1, wg, wu, wd, g, *, tile=MOE_TILE, dispatch_chunk=2048, combine_chunk=512):
    n = x1.shape[0]
    n_tiles = n // tile + N_CLASSES
    assert n_tiles <= LANES
    pos, tile_cls, ends = _sort(cls3.reshape(n // LANES, LANES), tile=tile)
    xs = _dispatch(pos.reshape(n // dispatch_chunk, 1, dispatch_chunk), ends, h2ext, n_rows=n_tiles * tile, tile=tile)
    used = ends[0, N_CLASSES - 1:N_CLASSES] // tile
    tc = tile_cls[0, :n_tiles]
    tc = jnp.where(jnp.arange(n_tiles) < used[0], tc, tc[jnp.maximum(used[0] - 1, 0)])
    pair_a = jnp.array([p[0] for p in PAIR_ORDER], jnp.int32)
    pair_b = jnp.array([p[1] for p in PAIR_ORDER], jnp.int32)
    ea = (tc // N_PAIRS) * EXPERTS_PER_GROUP + pair_a[tc % N_PAIRS]
    eb = (tc // N_PAIRS) * EXPERTS_PER_GROUP + pair_b[tc % N_PAIRS]
    ys = _experts(ea, eb, used, xs, wg, wu, wd, tile=tile)
    return _combine(pos.reshape(n // combine_chunk, 1, combine_chunk), x1, g, ys)


def _rotary_tables(seq):
    pos = np.arange(seq, dtype=np.float64)
    inv_freq = 1.0 / (ROPE_THETA ** (np.arange(0, ROT_DIM, 2, dtype=np.float64) / ROT_DIM))
    ang = pos[:, None] * inv_freq[None, :]
    cos, sin = np.cos(ang), np.sin(ang)
    zeros = np.zeros((seq, HEAD_DIM - ROT_DIM))
    zh = np.zeros((seq, ROT_HALF))
    cos_h = np.concatenate([cos, cos, np.ones((seq, HEAD_DIM - ROT_DIM))], axis=1)
    sa_h = np.concatenate([-sin, zh, zeros], axis=1)
    sb_h = np.concatenate([zh, sin, zeros], axis=1)
    tile = lambda t: jnp.asarray(np.concatenate([t, t], axis=1), F32)
    return tile(cos_h), tile(sa_h), tile(sb_h)


def kernel(x, attn_norm, w_in, b_forget, w_out, ffn_norm, w_group, w_expert, w_gate_e, w_up_e, w_down_e, final_norm):
    b, seq, d = x.shape
    assert d == D_MODEL and w_in.shape[0] == 1, "single-layer block"
    n = b * seq
    scale = HEAD_DIM ** -0.5 * LOG2E
    col_scale = np.ones((2 * QKV_WIDTH + LANES,), np.float32)
    col_scale[0:WIDTH_A] = scale
    col_scale[QKV_WIDTH:QKV_WIDTH + WIDTH_B] = scale
    w = (jnp.pad(w_in[0].T, ((0, LANES - HEADS_B), (0, 0))) * col_scale[:, None]).astype(BF16)
    bf = jnp.pad(b_forget[0].astype(F32), (0, LANES - HEADS_B))[None, :]
    cosv, sa, sb = _rotary_tables(seq)
    x2 = x.reshape(n, d)

    qkva, qkvb, c = _inproj(x2, attn_norm[0][None, :], w, cosv, sa, sb, bf, seq=seq, tm=512)
    out_a = _dilated(qkva.reshape(b, seq, QKV_WIDTH))
    out_b, wg, wu, wd = _fox(qkvb.reshape(b, seq, QKV_WIDTH), c.reshape(b, seq, LANES),
                             w_gate_e[0], w_up_e[0], w_down_e[0], tq=1024)

    wr = jnp.pad(jnp.concatenate([w_group[0], w_expert[0]], axis=1).astype(F32),
                 ((0, 0), (0, LANES - N_GROUPS - N_EXPERTS)))
    wr_hi = wr.astype(BF16)
    wr = jnp.concatenate([wr_hi, (wr - wr_hi.astype(F32)).astype(BF16)], axis=1)
    x1, h2ext, cls3 = _outproj(out_a.reshape(n, WIDTH_A), out_b.reshape(n, WIDTH_B), x2,
                               w_out[0].astype(BF16), ffn_norm[0][None, :], wr, tm=512)
    out = _moe(h2ext, cls3, x1, wg, wu, wd, final_norm[None, :])
    return out.reshape(b, seq, d)
```

```python
import functools

import jax
import jax.numpy as jnp
import numpy as np
from jax import lax
from jax.experimental import pallas as pl
from jax.experimental.pallas import tpu as pltpu

F32 = jnp.float32
BF16 = jnp.bfloat16

D_MODEL = 1024
HEAD_DIM = 64
HEADS_A = 8
HEADS_B = 8
WIDTH_A = HEADS_A * HEAD_DIM
WIDTH_B = HEADS_B * HEAD_DIM
QKV_WIDTH = 3 * WIDTH_A
DILATIONS = ((128, 1), (512, 4), (2048, 16))
ROT_DIM = HEAD_DIM // 4
ROT_HALF = ROT_DIM // 2
ROPE_THETA = 500000.0
N_GROUPS = 4
EXPERTS_PER_GROUP = 4
N_EXPERTS = N_GROUPS * EXPERTS_PER_GROUP
D_EXPERT = 512
NORM_EPS = 1e-6
NEG_INF = -1e30

LANES = 128
HEAD_PAIRS = WIDTH_A // LANES
WIN = 128
MAX_DIL = 16
MID_DIL = 4
N_MID = MAX_DIL // MID_DIL
assert sorted(d for _, d in DILATIONS) == [1, MID_DIL, MAX_DIL] and all(w // d == WIN for w, d in DILATIONS)
PERM = MAX_DIL * MAX_DIL
KV_BLOCK = 256
N_PAIRS = EXPERTS_PER_GROUP * (EXPERTS_PER_GROUP - 1) // 2
N_CLASSES = N_GROUPS * N_PAIRS
PAIR_ORDER = ((0, 1), (0, 2), (1, 2), (1, 3), (0, 3), (2, 3))
assert len(PAIR_ORDER) == N_PAIRS
MOE_TILE = 256
EXT_WIDTH = D_MODEL + LANES
LOGIT_MAX = 60.0
BRANCH_UNROLL = 8
DMA_THREADS = 2
ROUTER_OFF = N_GROUPS
VMEM_LIMIT = 56 * 1024 * 1024
LOG2E = 1.4426950408889634
BOUND_SLACK = 1.0 + 2.0 ** -6
BOUND_EPS = 2.0 ** -7


N_PARTS = 3


def _split3(x):
    hi = x.astype(BF16)
    r1 = x - hi.astype(F32)
    mid = r1.astype(BF16)
    lo = (r1 - mid.astype(F32)).astype(BF16)
    return hi, mid, lo


def _dot_nt(a, b):
    return lax.dot_general(a, b, (((1,), (1,)), ((), ())), preferred_element_type=F32)


def _inproj_kernel(x_ref, g_ref, w_ref, cos_ref, sa_ref, sb_ref, bf_ref,
                   qkva_ref, qkvb_ref, c_ref, h_scr, carry_scr, *, tiles_per_seq):
    i = pl.program_id(0)
    tm = x_ref.shape[0]
    x = x_ref[...]
    ms = jnp.mean(x * x, axis=-1, keepdims=True)
    h_scr[...] = (x * lax.rsqrt(ms + NORM_EPS) * g_ref[...]).astype(BF16)
    cosv = cos_ref[...]
    sa = sa_ref[...]
    sb = sb_ref[...]
    for g in range(6):
        p = _dot_nt(h_scr[...], w_ref[g * WIDTH_A:(g + 1) * WIDTH_A, :])
        dst = qkva_ref if g < 3 else qkvb_ref
        col0 = (g % 3) * WIDTH_A
        if g < 2:
            for k in range(HEAD_PAIRS):
                t = p[:, k * LANES:(k + 1) * LANES]
                t = (t * cosv + pltpu.roll(t, LANES - ROT_HALF, 1) * sa + pltpu.roll(t, ROT_HALF, 1) * sb)
                dst[:, col0 + k * LANES:col0 + (k + 1) * LANES] = t.astype(BF16)
        else:
            dst[:, col0:col0 + WIDTH_A] = p.astype(BF16)
    z = _dot_nt(h_scr[...], w_ref[2 * QKV_WIDTH:2 * QKV_WIDTH + LANES, :]) + bf_ref[...]
    lf = jnp.minimum(z, 0.0) - jnp.log1p(jnp.exp(-jnp.abs(z)))
    lane = lax.broadcasted_iota(jnp.int32, (tm, LANES), 1)
    lf = jnp.where(lane < HEADS_B, lf * LOG2E, 0.0)
    hi, mid, lo = _split3(lf)
    tri = (lax.broadcasted_iota(jnp.int32, (tm, tm), 0) >= lax.broadcasted_iota(jnp.int32, (tm, tm), 1)).astype(BF16)
    two = jnp.dot(tri, jnp.concatenate([hi, mid], axis=1), preferred_element_type=F32)
    cs = two[:, 0:LANES] + two[:, LANES:2 * LANES] + jnp.dot(tri, lo, preferred_element_type=F32)

    @pl.when(i % tiles_per_seq == 0)
    def _():
        carry_scr[...] = jnp.zeros_like(carry_scr)

    c = cs + carry_scr[0:1, :]
    c_ref[...] = c
    carry_scr[...] = jnp.broadcast_to(c[tm - 1:tm, :], carry_scr.shape)


def _inproj(x2, g, w, cosv, sa, sb, bf, *, seq, tm):
    n = x2.shape[0]
    kern = functools.partial(_inproj_kernel, tiles_per_seq=seq // tm)
    tps = seq // tm
    return pl.pallas_call(
        kern,
        grid=(n // tm,),
        in_specs=[
            pl.BlockSpec((tm, D_MODEL), lambda i: (i, 0)),
            pl.BlockSpec((1, D_MODEL), lambda i: (0, 0)),
            pl.BlockSpec(w.shape, lambda i: (0, 0)),
            pl.BlockSpec((tm, LANES), lambda i: (i % tps, 0)),
            pl.BlockSpec((tm, LANES), lambda i: (i % tps, 0)),
            pl.BlockSpec((tm, LANES), lambda i: (i % tps, 0)),
            pl.BlockSpec((1, LANES), lambda i: (0, 0)),
        ],
        out_specs=[
            pl.BlockSpec((tm, QKV_WIDTH), lambda i: (i, 0)),
            pl.BlockSpec((tm, QKV_WIDTH), lambda i: (i, 0)),
            pl.BlockSpec((tm, LANES), lambda i: (i, 0)),
        ],
        out_shape=[
            jax.ShapeDtypeStruct((n, QKV_WIDTH), BF16),
            jax.ShapeDtypeStruct((n, QKV_WIDTH), BF16),
            jax.ShapeDtypeStruct((n, LANES), F32),
        ],
        scratch_shapes=[pltpu.VMEM((tm, D_MODEL), BF16), pltpu.VMEM((8, LANES), F32)],
        compiler_params=pltpu.CompilerParams(dimension_semantics=("arbitrary",), vmem_limit_bytes=VMEM_LIMIT),
        name="inproj",
    )(x2, g, w, cosv, sa, sb, bf)


def _own_lanes(lane, hl):
    return (lane < HEAD_DIM) if hl == 0 else (lane >= HEAD_DIM)


def _head_sum_selector():
    lane = lax.broadcasted_iota(jnp.int32, (LANES, LANES), 0)
    col = lax.broadcasted_iota(jnp.int32, (LANES, LANES), 1)
    return jnp.where(((col == 0) & (lane < HEAD_DIM)) | ((col == 1) & (lane >= HEAD_DIM)), 1.0, 0.0).astype(BF16)


def _max_sq_norm(t, selector):
    return jnp.max(jnp.dot((t * t).astype(BF16), selector, preferred_element_type=F32), axis=0, keepdims=True)


def _norm_bound(qsq, ksq_max):
    return jnp.sqrt(qsq * ksq_max) * BOUND_SLACK + BOUND_EPS


def _fox_kernel(q_ref, k_ref, v_ref, c_ref, wg_ref, wu_ref, wd_ref, o_ref, wg_out, wu_out, wd_out,
                q0_scr, q1_scr, k0_scr, k1_scr, v0_scr, v1_scr, bound_scr, acc0_scr, acc1_scr, s_scr,
                *, tq, seq, chunk):
    hp = pl.program_id(1)
    qi = pl.program_id(2)
    for src, dst in ((wg_ref, wg_out), (wu_ref, wu_out), (wd_ref, wd_out)):
        dst[...] = src[...].astype(dst.dtype)
    q_scr = (q0_scr, q1_scr)
    k_scr = (k0_scr, k1_scr)
    v_scr = (v0_scr, v1_scr)
    acc_scr = (acc0_scr, acc1_scr)

    def placement(part):
        row = lax.broadcasted_iota(jnp.int32, (LANES, 2 * LANES), 0)
        col = lax.broadcasted_iota(jnp.int32, (LANES, 2 * LANES), 1)
        m = jnp.zeros((LANES, 2 * LANES), F32)
        for hl in range(2):
            first = HEAD_DIM if hl == 0 else 0
            src = row == 2 * hp + hl
            m = jnp.where(src & (col == first + part), 1.0, m)
            m = jnp.where(src & (col == LANES + first + N_PARTS + part), -1.0, m)
            if part == 0:
                ones_q = (col >= first + N_PARTS) & (col < first + 2 * N_PARTS)
                ones_k = (col >= LANES + first) & (col < LANES + first + N_PARTS)
                m = jnp.where((row == LANES - 1) & (ones_q | ones_k), 1.0, m)
        return m.astype(BF16)

    @pl.when(qi == 0)
    def _build():
        lane = lax.broadcasted_iota(jnp.int32, (chunk, LANES), 1)
        place_hi_mid = jnp.concatenate([placement(0), placement(1)], axis=0)
        place_lo = placement(2)

        def body(ci, sq_max):
            rows = pl.ds(pl.multiple_of(ci * chunk, chunk), chunk)
            qc = q_ref[0, rows, :].astype(F32)
            kc = k_ref[0, rows, :].astype(F32)
            vc = v_ref[0, rows, :].astype(F32)
            hi, mid, lo = _split3(jnp.where(lane == LANES - 1, 1.0, c_ref[0, rows, :]))
            aug = (jnp.dot(jnp.concatenate([hi, mid], axis=1), place_hi_mid, preferred_element_type=F32)
                   + jnp.dot(lo, place_lo, preferred_element_type=F32))
            for hl in range(2):
                own = _own_lanes(lane, hl)
                q_scr[hl][rows, :] = jnp.where(own, qc, aug[:, 0:LANES]).astype(BF16)
                k_scr[hl][rows, :] = jnp.where(own, kc, aug[:, LANES:2 * LANES]).astype(BF16)
                one_lane = HEAD_DIM if hl == 0 else 0
                v_scr[hl][rows, :] = jnp.where(own, vc, jnp.where(lane == one_lane, 1.0, 0.0)).astype(BF16)
            return (jnp.maximum(sq_max[0], _max_sq_norm(qc, selector)),
                    jnp.maximum(sq_max[1], _max_sq_norm(kc, selector)))

        selector = _head_sum_selector()
        sq_max = lax.fori_loop(0, seq // chunk, body, (jnp.zeros((1, LANES), F32),) * 2)
        bound_scr[...] = jnp.broadcast_to(_norm_bound(sq_max[0], sq_max[1]), bound_scr.shape)

    lane_q = lax.broadcasted_iota(jnp.int32, (tq, LANES), 1)
    q_rows = pl.ds(pl.multiple_of(qi * tq, tq), tq)
    worst = jnp.max(bound_scr[...])
    n_diag = tq // KV_BLOCK
    n_full = qi * n_diag
    row_minus_col = (lax.broadcasted_iota(jnp.int32, (tq, KV_BLOCK), 0)
                     - lax.broadcasted_iota(jnp.int32, (tq, KV_BLOCK), 1))

    def causal_mask(s, d, top=0):
        return jnp.where(row_minus_col[top:tq, :] >= d * KV_BLOCK, s, NEG_INF)

    def logits(hl, j, top=0):
        r0 = pl.multiple_of(j * KV_BLOCK, KV_BLOCK)
        rows = q_rows if top == 0 else pl.ds(pl.multiple_of(qi * tq + top, KV_BLOCK), tq - top)
        return lax.dot_general(q_scr[hl][rows, :], k_scr[hl][pl.ds(r0, KV_BLOCK), :], (((1,), (1,)), ((), ())),
                               preferred_element_type=F32)

    def values(hl, j):
        return v_scr[hl][pl.ds(pl.multiple_of(j * KV_BLOCK, KV_BLOCK), KV_BLOCK), :]

    def finish(a0, a1):
        l0 = a0[:, HEAD_DIM:HEAD_DIM + 1]
        l1 = a1[:, 0:1]
        o_ref[0] = jnp.where(lane_q < HEAD_DIM, a0 / l0, a1 / l1).astype(o_ref.dtype)

    @pl.when(worst <= LOGIT_MAX)
    def _fast():
        acc0_scr[...] = jnp.zeros_like(acc0_scr)
        acc1_scr[...] = jnp.zeros_like(acc1_scr)

        def produce(j, slot, top=0):
            for hl in range(2):
                s_scr[slot, hl, top:tq, :] = logits(hl, j, top)

        def consume(j, slot, diag):
            top = 0 if diag is None else diag * KV_BLOCK
            for hl in range(2):
                s = s_scr[slot, hl, top:tq, :]
                if diag is not None:
                    s = causal_mask(s, diag, top)
                acc_scr[hl][top:tq, :] += jnp.dot(jnp.exp2(s).astype(BF16), values(hl, j),
                                                  preferred_element_type=F32)

        def produce_diagonal(d):
            produce(n_full + d, d, d * KV_BLOCK)

        @pl.when(qi == 0)
        def _():
            for d in range(n_diag):
                produce_diagonal(d)

        @pl.when(qi > 0)
        def _():
            for d in range(n_diag):
                produce(d, d)

            def body(i, carry):
                for d in range(n_diag):
                    j = i * n_diag + d
                    consume(j, d, None)
                    produce(j + n_diag, d)
                return carry

            lax.fori_loop(0, qi - 1, body, 0)
            for d in range(n_diag):
                consume(n_full - n_diag + d, d, None)
                produce_diagonal(d)

        for d in range(n_diag):
            consume(n_full + d, d, d)
        finish(acc0_scr[...], acc1_scr[...])

    @pl.when(worst > LOGIT_MAX)
    def _general():
        def step(j, carry, diag):
            new = []
            for hl in range(2):
                m, a = carry[2 * hl:2 * hl + 2]
                s = logits(hl, j)
                if diag is not None:
                    s = causal_mask(s, diag)
                mn = jnp.maximum(m, jnp.max(s, axis=1, keepdims=True))
                a = jnp.exp2(m - mn) * a + jnp.dot(jnp.exp2(s - mn).astype(BF16), values(hl, j),
                                                  preferred_element_type=F32)
                new += [mn, a]
            return tuple(new)

        carry = (jnp.full((tq, 1), NEG_INF, F32), jnp.zeros((tq, LANES), F32)) * 2
        carry = lax.fori_loop(0, n_full, lambda j, c: step(j, c, None), carry)
        for d in range(n_diag):
            carry = step(n_full + d, carry, d)
        finish(carry[1], carry[3])


def _fox(qkvb, c, wg, wu, wd, *, tq, chunk=512):
    b, seq, _ = qkvb.shape
    nq = seq // tq
    steps = b * HEAD_PAIRS * nq
    parts = steps // N_EXPERTS
    assert parts * N_EXPERTS == steps and D_MODEL % parts == 0 and D_EXPERT % parts == 0

    def slice_spec(rows, cols):
        def index(bi, hp, qi):
            step = (bi * HEAD_PAIRS + hp) * nq + qi
            return (step // parts, step % parts, 0)
        return pl.BlockSpec((1, rows // parts, cols), index)

    w_specs = [slice_spec(D_MODEL, D_EXPERT), slice_spec(D_MODEL, D_EXPERT), slice_spec(D_EXPERT, D_MODEL)]
    kern = functools.partial(_fox_kernel, tq=tq, seq=seq, chunk=chunk)
    return pl.pallas_call(
        kern,
        grid=(b, HEAD_PAIRS, nq),
        in_specs=[
            pl.BlockSpec((1, seq, LANES), lambda bi, hp, qi: (bi, 0, hp)),
            pl.BlockSpec((1, seq, LANES), lambda bi, hp, qi: (bi, 0, HEAD_PAIRS + hp)),
            pl.BlockSpec((1, seq, LANES), lambda bi, hp, qi: (bi, 0, 2 * HEAD_PAIRS + hp)),
            pl.BlockSpec((1, seq, LANES), lambda bi, hp, qi: (bi, 0, 0)),
        ] + w_specs,
        out_specs=[pl.BlockSpec((1, tq, LANES), lambda bi, hp, qi: (bi, qi, hp))] + w_specs,
        out_shape=[jax.ShapeDtypeStruct((b, seq, WIDTH_B), BF16)]
        + [jax.ShapeDtypeStruct(w.shape, BF16) for w in (wg, wu, wd)],
        scratch_shapes=[pltpu.VMEM((seq, LANES), BF16)] * 6 + [
                        pltpu.VMEM((8, LANES), F32),
                        pltpu.VMEM((tq, LANES), F32), pltpu.VMEM((tq, LANES), F32),
                        pltpu.VMEM((tq // KV_BLOCK, 2, tq, KV_BLOCK), F32)],
        compiler_params=pltpu.CompilerParams(
            dimension_semantics=("arbitrary", "arbitrary", "arbitrary"), vmem_limit_bytes=VMEM_LIMIT),
        name="fox",
    )(qkvb, qkvb, qkvb, c, wg, wu, wd)


def _log2(n):
    assert n & (n - 1) == 0
    return n.bit_length() - 1


def _permutation():
    ri = lax.broadcasted_iota(jnp.int32, (PERM, PERM), 0)
    ci = lax.broadcasted_iota(jnp.int32, (PERM, PERM), 1)
    sh, lo = _log2(MAX_DIL), MAX_DIL - 1
    return (((ri >> sh) == (ci & lo)) & ((ri & lo) == (ci >> sh))).astype(BF16)


def _band_masks():
    qa = lax.broadcasted_iota(jnp.int32, (WIN, 2 * WIN), 0)
    kb = lax.broadcasted_iota(jnp.int32, (WIN, 2 * WIN), 1)
    cur = kb >= WIN
    kin = kb & (WIN - 1)
    prev_shift = jnp.where(cur, 0, WIN)
    dist = qa - kin + prev_shift
    chunk = WIN // N_MID
    offset = lambda a: ((a & (chunk - 1)) << _log2(N_MID)) + (a >> _log2(chunk))
    dist_mid = offset(qa) - offset(kin) + prev_shift
    return (dist >= 0) & (dist <= WIN), (dist_mid >= 0) & (dist_mid <= WIN), cur


def _dilated_general(q_ref, k_ref, v_ref, o_ref, q16, k16, v16, m_s, l_s, acc_s, on_s, lse_s, *, seq):
    sub = seq // MAX_DIL
    own0 = lax.broadcasted_iota(jnp.int32, (WIN, LANES), 1) < HEAD_DIM
    perm = _permutation()
    band_nat, band_4, cur = _band_masks()

    def attend(qb, kb, vb, mask, state):
        qf = qb.astype(F32)
        new = []
        for hl in range(2):
            m, l, a = state[3 * hl:3 * hl + 3]
            own = own0 if hl == 0 else jnp.logical_not(own0)
            qh = jnp.where(own, qf, 0.0).astype(BF16)
            s = lax.dot_general(qh, kb, (((1,), (1,)), ((), ())), preferred_element_type=F32)
            s = jnp.where(mask, s, NEG_INF)
            mn = jnp.maximum(m, jnp.max(s, axis=1, keepdims=True))
            alpha = jnp.exp2(m - mn)
            p = jnp.exp2(s - mn)
            l = alpha * l + jnp.sum(p, axis=1, keepdims=True)
            a = alpha * a + jnp.dot(p.astype(BF16), vb, preferred_element_type=F32)
            new += [mn, l, a]
        return new

    def pack_state(st):
        m0, l0, a0, m1, l1, a1 = st
        return (jnp.where(own0, m0, m1), jnp.where(own0, l0, l1), jnp.where(own0, a0, a1))

    def deint(blk, carry):
        r0 = pl.multiple_of(blk * PERM, PERM)
        j0 = pl.multiple_of(blk * MAX_DIL, MAX_DIL)
        for src, dst in ((q_ref, q16), (k_ref, k16), (v_ref, v16)):
            y = jnp.dot(perm, src[0, pl.ds(r0, PERM), :], preferred_element_type=F32).astype(BF16)
            for r in range(MAX_DIL):
                dst[r, pl.ds(j0, MAX_DIL), :] = y[r * MAX_DIL:(r + 1) * MAX_DIL, :]
        return carry

    lax.fori_loop(0, seq // PERM, deint, 0)

    fresh = [jnp.full((WIN, 1), NEG_INF, F32), jnp.zeros((WIN, 1), F32), jnp.zeros((WIN, LANES), F32)] * 2
    nb16 = sub // WIN

    def d16_body(t, carry):
        r = t // nb16
        n = t % nb16
        c0 = pl.multiple_of(n * WIN, WIN)
        p0 = pl.multiple_of(jnp.maximum(n - 1, 0) * WIN, WIN)
        qb = q16[r, pl.ds(c0, WIN), :]
        kb = jnp.concatenate([k16[r, pl.ds(p0, WIN), :], k16[r, pl.ds(c0, WIN), :]], axis=0)
        vb = jnp.concatenate([v16[r, pl.ds(p0, WIN), :], v16[r, pl.ds(c0, WIN), :]], axis=0)
        mask = band_nat & (cur | (n > 0))
        mm, ll, aa = pack_state(attend(qb, kb, vb, mask, fresh))
        row = pl.multiple_of(r * sub + c0, WIN)
        m_s[pl.ds(row, WIN), :] = mm
        l_s[pl.ds(row, WIN), :] = ll
        acc_s[pl.ds(row, WIN), :] = aa
        return carry

    lax.fori_loop(0, MAX_DIL * nb16, d16_body, 0)

    nb4 = (seq // MID_DIL) // WIN
    ch = WIN // N_MID

    def d4_body(t, carry):
        r4 = t // nb4
        n = t % nb4
        c0 = pl.multiple_of(n * ch, ch)
        p0 = pl.multiple_of(jnp.maximum(n - 1, 0) * ch, ch)

        def gather(ref, j0):
            return [ref[r4 + MID_DIL * q, pl.ds(j0, ch), :] for q in range(N_MID)]

        qb = jnp.concatenate(gather(q16, c0), axis=0)
        kb = jnp.concatenate(gather(k16, p0) + gather(k16, c0), axis=0)
        vb = jnp.concatenate(gather(v16, p0) + gather(v16, c0), axis=0)
        rows = [pl.multiple_of((r4 + MID_DIL * q) * sub + c0, ch) for q in range(N_MID)]
        mm = jnp.concatenate([m_s[pl.ds(rw, ch), :] for rw in rows], axis=0)
        ll = jnp.concatenate([l_s[pl.ds(rw, ch), :] for rw in rows], axis=0)
        aa = jnp.concatenate([acc_s[pl.ds(rw, ch), :] for rw in rows], axis=0)
        state = [mm[:, 0:1], ll[:, 0:1], aa, mm[:, HEAD_DIM:HEAD_DIM + 1], ll[:, HEAD_DIM:HEAD_DIM + 1], aa]
        mask = band_4 & (cur | (n > 0))
        mm, ll, aa = pack_state(attend(qb, kb, vb, mask, state))
        for q, rw in enumerate(rows):
            m_s[pl.ds(rw, ch), :] = mm[q * ch:(q + 1) * ch, :]
            l_s[pl.ds(rw, ch), :] = ll[q * ch:(q + 1) * ch, :]
            acc_s[pl.ds(rw, ch), :] = aa[q * ch:(q + 1) * ch, :]
        return carry

    lax.fori_loop(0, MID_DIL * nb4, d4_body, 0)

    def renat(blk, carry):
        j0 = pl.multiple_of(blk * MAX_DIL, MAX_DIL)
        r0 = pl.multiple_of(blk * PERM, PERM)
        rows = [pl.multiple_of(r * sub + j0, MAX_DIL) for r in range(MAX_DIL)]
        mm = jnp.concatenate([m_s[pl.ds(rw, MAX_DIL), :] for rw in rows], axis=0)
        ll = jnp.concatenate([l_s[pl.ds(rw, MAX_DIL), :] for rw in rows], axis=0)
        aa = jnp.concatenate([acc_s[pl.ds(rw, MAX_DIL), :] for rw in rows], axis=0)
        o = (aa / ll).astype(BF16)
        hi, mid, lo = _split3(mm + jnp.log2(ll))
        on_s[pl.ds(r0, PERM), :] = jnp.dot(perm, o, preferred_element_type=F32)
        lse_s[pl.ds(r0, PERM), :] = (jnp.dot(perm, hi, preferred_element_type=F32)
                                     + jnp.dot(perm, mid, preferred_element_type=F32)
                                     + jnp.dot(perm, lo, preferred_element_type=F32))
        return carry

    lax.fori_loop(0, seq // PERM, renat, 0)

    def d1_body(n, carry):
        c0 = pl.multiple_of(n * WIN, WIN)
        p0 = pl.multiple_of(jnp.maximum(n - 1, 0) * WIN, WIN)
        qb = q_ref[0, pl.ds(c0, WIN), :]
        kb = jnp.concatenate([k_ref[0, pl.ds(p0, WIN), :], k_ref[0, pl.ds(c0, WIN), :]], axis=0)
        vb = jnp.concatenate([v_ref[0, pl.ds(p0, WIN), :], v_ref[0, pl.ds(c0, WIN), :]], axis=0)
        lse = lse_s[pl.ds(c0, WIN), :]
        o = on_s[pl.ds(c0, WIN), :]
        one = jnp.ones((WIN, 1), F32)
        state = [lse[:, 0:1], one, o, lse[:, HEAD_DIM:HEAD_DIM + 1], one, o]
        mask = band_nat & (cur | (n > 0))
        m0, l0, a0, m1, l1, a1 = attend(qb, kb, vb, mask, state)
        o_ref[0, pl.ds(c0, WIN), :] = jnp.where(own0, a0 / l0, a1 / l1).astype(o_ref.dtype)
        return carry

    lax.fori_loop(0, seq // WIN, d1_body, 0)


def _dilated_fast(o_ref, qn, kn, vn, qr, kr, vr, fr, fn, s_scr, mask_scr, *, seq):
    sub = seq // MAX_DIL
    own0 = lax.broadcasted_iota(jnp.int32, (WIN, LANES), 1) < HEAD_DIM
    head_mask = [jnp.where(own0, 1.0, 0.0).astype(BF16), jnp.where(own0, 0.0, 1.0).astype(BF16)]
    perm = _permutation()

    band_nat, band_4, cur = _band_masks()
    for i, band in enumerate((band_nat, band_4)):
        mask_scr[2 * i] = jnp.where(band & cur, 1.0, 0.0).astype(BF16)
        mask_scr[2 * i + 1] = jnp.where(band, 1.0, 0.0).astype(BF16)

    def run_branch(nblocks, block_in_seq, fetch, mask_base, sink, before_trip=None):
        def produce(t, pair, u):
            for hl in range(2):
                qb, kb = fetch(t, hl, True)
                s_scr[pair, u, hl] = lax.dot_general(qb * head_mask[hl], kb, (((1,), (1,)), ((), ())),
                                                     preferred_element_type=F32)

        def consume(t, pair, u):
            mk = mask_scr[mask_base + jnp.minimum(block_in_seq(t), 1)]
            pv = []
            for hl in range(2):
                p = jnp.exp2(s_scr[pair, u, hl]).astype(BF16) * mk
                pv.append(jnp.dot(p, fetch(t, hl, False), preferred_element_type=F32))
            sink(t, pv)

        for u in range(BRANCH_UNROLL):
            produce(u, 0, u)

        def trip(i, last):
            pair = i & 1
            if before_trip is not None:
                before_trip(i)
            for u in range(BRANCH_UNROLL):
                consume(BRANCH_UNROLL * i + u, pair, u)
            if not last:
                for u in range(BRANCH_UNROLL):
                    produce(BRANCH_UNROLL * (i + 1) + u, 1 - pair, u)

        def body(i, carry):
            trip(i, False)
            return carry

        trips = nblocks // BRANCH_UNROLL
        lax.fori_loop(0, trips - 1, body, 0)
        trip(jnp.int32(trips - 1), True)

    def prev_cur(ref_block, n, size):
        c0 = pl.multiple_of(n * size, size)
        p0 = pl.multiple_of(jnp.maximum(n - 1, 0) * size, size)
        return ref_block(p0) + ref_block(c0)

    nb16 = sub // WIN

    def fetch16(t, hl, qk):
        r, n = t // nb16, t % nb16
        if qk:
            return (qr[hl][r, pl.ds(pl.multiple_of(n * WIN, WIN), WIN), :],
                    jnp.concatenate(prev_cur(lambda s: [kr[hl][r, pl.ds(s, WIN), :]], n, WIN), axis=0))
        return jnp.concatenate(prev_cur(lambda s: [vr[hl][r, pl.ds(s, WIN), :]], n, WIN), axis=0)

    def sink16(t, pv):
        row = pl.multiple_of(t * WIN, WIN)
        for hl in range(2):
            fr[hl][pl.ds(row, WIN), :] = pv[hl]

    run_branch(MAX_DIL * nb16, lambda t: t % nb16, fetch16, 0, sink16)

    nb4 = (seq // MID_DIL) // WIN
    ch = WIN // N_MID

    def fetch4(t, hl, qk):
        r4, n = t // nb4, t % nb4
        gather = lambda ref: (lambda s: [ref[r4 + MID_DIL * q, pl.ds(s, ch), :] for q in range(N_MID)])
        if qk:
            return (jnp.concatenate(gather(qr[hl])(pl.multiple_of(n * ch, ch)), axis=0),
                    jnp.concatenate(prev_cur(gather(kr[hl]), n, ch), axis=0))
        return jnp.concatenate(prev_cur(gather(vr[hl]), n, ch), axis=0)

    def sink4(t, pv):
        r4, n = t // nb4, t % nb4
        for q in range(N_MID):
            row = pl.multiple_of((r4 + MID_DIL * q) * sub + n * ch, ch)
            for hl in range(2):
                fr[hl][pl.ds(row, ch), :] += pv[hl][q * ch:(q + 1) * ch, :]

    run_branch(MID_DIL * nb4, lambda t: t % nb4, fetch4, 2, sink4)

    def renat(i):
        for u in range(BRANCH_UNROLL * WIN // PERM):
            blk = i * (BRANCH_UNROLL * WIN // PERM) + u
            j0 = pl.multiple_of(blk * MAX_DIL, MAX_DIL)
            r0 = pl.multiple_of(blk * PERM, PERM)
            for hl in range(2):
                a = jnp.concatenate([fr[hl][pl.ds(pl.multiple_of(r * sub + j0, MAX_DIL), MAX_DIL), :]
                                     for r in range(MAX_DIL)], axis=0)
                hi = a.astype(BF16)
                lo = (a - hi.astype(F32)).astype(BF16)
                y = jnp.dot(perm, jnp.concatenate([hi, lo], axis=1), preferred_element_type=F32)
                fn[hl][pl.ds(r0, PERM), :] = y[:, 0:LANES] + y[:, LANES:2 * LANES]

    def fetch1(t, hl, qk):
        if qk:
            return (qn[hl][pl.ds(pl.multiple_of(t * WIN, WIN), WIN), :],
                    jnp.concatenate(prev_cur(lambda s: [kn[hl][pl.ds(s, WIN), :]], t, WIN), axis=0))
        return jnp.concatenate(prev_cur(lambda s: [vn[hl][pl.ds(s, WIN), :]], t, WIN), axis=0)

    def sink1(t, pv):
        row = pl.multiple_of(t * WIN, WIN)
        t0 = fn[0][pl.ds(row, WIN), :] + pv[0]
        t1 = fn[1][pl.ds(row, WIN), :] + pv[1]
        o_ref[0, pl.ds(row, WIN), :] = jnp.where(own0, t0 / t0[:, HEAD_DIM:HEAD_DIM + 1],
                                                 t1 / t1[:, 0:1]).astype(o_ref.dtype)

    run_branch(seq // WIN, lambda t: t, fetch1, 0, sink1, before_trip=renat)


def _dilated_kernel(q_ref, k_ref, v_ref, o_ref, vn0, vn1, qr, kr, vr0, vr1,
                    f0, f1, f2, f3, f4, s_scr, mask_scr, *, seq, chunk):
    lane = lax.broadcasted_iota(jnp.int32, (chunk, LANES), 1)
    vn = (vn0, vn1)

    perm = _permutation()

    def stats(ci, sq_max):
        rows = pl.ds(pl.multiple_of(ci * chunk, chunk), chunk)
        qb = q_ref[0, rows, :]
        kb = k_ref[0, rows, :]
        vc = v_ref[0, rows, :].astype(F32)
        vb = []
        for hl in range(2):
            ones_lane = HEAD_DIM if hl == 0 else 0
            vb.append(jnp.where(_own_lanes(lane, hl), vc, jnp.where(lane == ones_lane, 1.0, 0.0)).astype(BF16))
            vn[hl][rows, :] = vb[hl]
        for u in range(chunk // PERM):
            j0 = pl.multiple_of((ci * (chunk // PERM) + u) * MAX_DIL, MAX_DIL)
            for srcs, dsts in (((qb, kb), (qr, kr)), (vb, (vr0, vr1))):
                both = jnp.concatenate([src[u * PERM:(u + 1) * PERM, :] for src in srcs], axis=1)
                y = jnp.dot(perm, both, preferred_element_type=F32).astype(BF16)
                for r in range(MAX_DIL):
                    for half, dst in enumerate(dsts):
                        dst[r, pl.ds(j0, MAX_DIL), :] = y[r * MAX_DIL:(r + 1) * MAX_DIL, half * LANES:(half + 1) * LANES]
        return (jnp.maximum(sq_max[0], _max_sq_norm(qb.astype(F32), selector)),
                jnp.maximum(sq_max[1], _max_sq_norm(kb.astype(F32), selector)))

    selector = _head_sum_selector()
    sq_max = lax.fori_loop(0, seq // chunk, stats, (jnp.zeros((1, LANES), F32),) * 2)
    worst = jnp.max(_norm_bound(sq_max[0], sq_max[1]))

    @pl.when(worst <= LOGIT_MAX)
    def _fast():
        qn, kn = q_ref.at[0], k_ref.at[0]
        _dilated_fast(o_ref, (qn, qn), (kn, kn), vn, (qr, qr), (kr, kr), (vr0, vr1), (f0, f1), (f2, f3),
                      s_scr, mask_scr, seq=seq)

    @pl.when(worst > LOGIT_MAX)
    def _general():
        _dilated_general(q_ref, k_ref, v_ref, o_ref, qr, kr, vr0, f0, f1, f2, f3, f4, seq=seq)


def _dilated(qkva, *, chunk=512):
    b, seq, _ = qkva.shape
    sub = seq // MAX_DIL
    kern = functools.partial(_dilated_kernel, seq=seq, chunk=chunk)
    return pl.pallas_call(
        kern,
        grid=(b, HEAD_PAIRS),
        in_specs=[
            pl.BlockSpec((1, seq, LANES), lambda bi, hp: (bi, 0, hp)),
            pl.BlockSpec((1, seq, LANES), lambda bi, hp: (bi, 0, HEAD_PAIRS + hp)),
            pl.BlockSpec((1, seq, LANES), lambda bi, hp: (bi, 0, 2 * HEAD_PAIRS + hp)),
        ],
        out_specs=pl.BlockSpec((1, seq, LANES), lambda bi, hp: (bi, 0, hp)),
        out_shape=jax.ShapeDtypeStruct((b, seq, WIDTH_A), BF16),
        scratch_shapes=(
            [pltpu.VMEM((seq, LANES), BF16)] * 2
            + [pltpu.VMEM((MAX_DIL, sub, LANES), BF16)] * 4
            + [pltpu.VMEM((seq, LANES), F32)] * 5
            + [pltpu.VMEM((2, BRANCH_UNROLL, 2, WIN, 2 * WIN), F32), pltpu.VMEM((4, WIN, 2 * WIN), BF16)]
        ),
        compiler_params=pltpu.CompilerParams(
            dimension_semantics=("arbitrary", "arbitrary"), vmem_limit_bytes=VMEM_LIMIT),
        name="dilated",
    )(qkva, qkva, qkva)


def _outproj_kernel(oa_ref, ob_ref, x_ref, wo_ref, g_ref, wr_ref, x1_ref, h2_ref, cls_ref):
    tm = x_ref.shape[0]
    x1 = (x_ref[...]
          + jnp.dot(oa_ref[...], wo_ref[0:WIDTH_A, :], preferred_element_type=F32)
          + jnp.dot(ob_ref[...], wo_ref[WIDTH_A:WIDTH_A + WIDTH_B, :], preferred_element_type=F32))
    x1_ref[...] = x1
    ms = jnp.mean(x1 * x1, axis=-1, keepdims=True)
    h2 = x1 * lax.rsqrt(ms + NORM_EPS) * g_ref[...]
    h2_ref[:, 0:D_MODEL] = h2
    h2_hi = h2.astype(BF16)
    h2_lo = (h2 - h2_hi.astype(F32)).astype(BF16)
    both = jnp.dot(h2_hi, wr_ref[...], preferred_element_type=F32)
    logits = (both[:, 0:LANES] + both[:, LANES:2 * LANES]
              + jnp.dot(h2_lo, wr_ref[:, 0:LANES], preferred_element_type=F32))
    lane = lax.broadcasted_iota(jnp.int32, (tm, LANES), 1)
    lane_f = lane.astype(F32)
    big = float(LANES)
    gmask = lane < N_GROUPS
    gl = jnp.where(gmask, logits, NEG_INF)
    gmax = jnp.max(gl, axis=1, keepdims=True)
    gsum = jnp.sum(jnp.where(gmask, jnp.exp(gl - gmax), 0.0), axis=1, keepdims=True)
    p_top = 1.0 / gsum
    g_star = jnp.min(jnp.where(gmask & (gl == gmax), lane_f, big), axis=1, keepdims=True)
    lo_lane = ROUTER_OFF + EXPERTS_PER_GROUP * g_star
    emask = (lane_f >= lo_lane) & (lane_f < lo_lane + EXPERTS_PER_GROUP)
    sel = jnp.where(emask, logits, NEG_INF)
    v1 = jnp.max(sel, axis=1, keepdims=True)
    i1 = jnp.min(jnp.where(emask & (sel == v1), lane_f, big), axis=1, keepdims=True)
    rest = emask & (lane_f != i1)
    sel2 = jnp.where(rest, logits, NEG_INF)
    v2 = jnp.max(sel2, axis=1, keepdims=True)
    i2 = jnp.min(jnp.where(rest & (sel2 == v2), lane_f, big), axis=1, keepdims=True)
    e2 = jnp.exp(v2 - v1)
    w1 = p_top / (1.0 + e2)
    w2 = p_top * e2 / (1.0 + e2)
    e1 = i1 - lo_lane
    e2x = i2 - lo_lane
    a = jnp.minimum(e1, e2x)
    b = jnp.maximum(e1, e2x)
    pair = jnp.zeros_like(a)
    for idx, (pa, pb) in enumerate(PAIR_ORDER):
        pair = jnp.where((a == pa) & (b == pb), float(idx), pair)
    cls = g_star * N_PAIRS + pair
    wa = jnp.where(e1 < e2x, w1, w2)
    wb = jnp.where(e1 < e2x, w2, w1)
    h2_ref[:, D_MODEL:EXT_WIDTH] = jnp.where(lane == 0, wa, jnp.where(lane == 1, wb, 0.0))
    row = lax.broadcasted_iota(jnp.int32, (tm, LANES), 0)
    spread = jnp.where(lane == (row & (LANES - 1)), cls, 0.0)
    cls_ref[0] = jnp.sum(spread.reshape(tm // LANES, LANES, LANES), axis=1).astype(jnp.int32)


def _outproj(oa, ob, x2, wo, g, wr, *, tm):
    n = x2.shape[0]
    return pl.pallas_call(
        _outproj_kernel,
        grid=(n // tm,),
        in_specs=[
            pl.BlockSpec((tm, WIDTH_A), lambda i: (i, 0)),
            pl.BlockSpec((tm, WIDTH_B), lambda i: (i, 0)),
            pl.BlockSpec((tm, D_MODEL), lambda i: (i, 0)),
            pl.BlockSpec(wo.shape, lambda i: (0, 0)),
            pl.BlockSpec((1, D_MODEL), lambda i: (0, 0)),
            pl.BlockSpec(wr.shape, lambda i: (0, 0)),
        ],
        out_specs=[
            pl.BlockSpec((tm, D_MODEL), lambda i: (i, 0)),
            pl.BlockSpec((tm, EXT_WIDTH), lambda i: (i, 0)),
            pl.BlockSpec((1, tm // LANES, LANES), lambda i: (i, 0, 0)),
        ],
        out_shape=[
            jax.ShapeDtypeStruct((n, D_MODEL), F32),
            jax.ShapeDtypeStruct((n, EXT_WIDTH), F32),
            jax.ShapeDtypeStruct((n // tm, tm // LANES, LANES), jnp.int32),
        ],
        compiler_params=pltpu.CompilerParams(dimension_semantics=("arbitrary",), vmem_limit_bytes=VMEM_LIMIT),
        name="outproj",
    )(oa, ob, x2, wo, g, wr)


def _sort_kernel(cls_ref, pos_ref, tile_ref, ends_ref, *, tile):
    cls = cls_ref[...]
    rows = cls.shape[0]
    upper = (lax.broadcasted_iota(jnp.int32, (LANES, LANES), 0)
             <= lax.broadcasted_iota(jnp.int32, (LANES, LANES), 1)).astype(BF16)
    earlier_rows = (lax.broadcasted_iota(jnp.int32, (rows, rows), 0)
                    > lax.broadcasted_iota(jnp.int32, (rows, rows), 1)).astype(BF16)
    ones = jnp.ones((LANES, LANES), BF16)
    lane8 = lax.broadcasted_iota(jnp.int32, tile_ref.shape, 1)
    tile_start = lane8.astype(F32) * tile
    off = jnp.zeros((1, 1), F32)
    pos = jnp.zeros(cls.shape, F32)
    tile_cls = jnp.zeros(tile_ref.shape, F32)
    ends = jnp.zeros(tile_ref.shape, F32)
    for c in range(N_CLASSES):
        hit = cls == c
        hot = jnp.where(hit, 1.0, 0.0).astype(BF16)
        in_row = jnp.dot(hot, upper, preferred_element_type=F32)
        row_tot = jnp.dot(hot, ones, preferred_element_type=F32)
        before = jnp.dot(earlier_rows, row_tot.astype(BF16), preferred_element_type=F32)
        count = before[rows - 1:rows, 0:1] + row_tot[rows - 1:rows, 0:1]
        pos = pos + jnp.where(hit, in_row - 1.0 + before + off, 0.0)
        off = off + jnp.ceil(count / tile) * tile
        tile_cls = tile_cls + jnp.where(tile_start >= off, 1.0, 0.0)
        ends = jnp.where(lane8 == c, off, ends)
    pos_ref[...] = pos.astype(jnp.int32)
    tile_ref[...] = tile_cls.astype(jnp.int32)
    ends_ref[...] = ends.astype(jnp.int32)


def _sort(cls2d, *, tile):
    meta = jax.ShapeDtypeStruct((8, LANES), jnp.int32)
    return pl.pallas_call(
        functools.partial(_sort_kernel, tile=tile),
        out_shape=[jax.ShapeDtypeStruct(cls2d.shape, jnp.int32), meta, meta],
        compiler_params=pltpu.CompilerParams(vmem_limit_bytes=VMEM_LIMIT),
        name="moe_sort",
    )(cls2d)


def _row_copy(src, src_row, dst, dst_row, sem, rows=1):
    return pltpu.make_async_copy(src.at[pl.ds(src_row, rows)], dst.at[pl.ds(dst_row, rows)], sem)


def _dispatch_kernel(pos_ref, ends_ref, h2_ref, xs_hbm, zero_scr, sem, *, ch, tile):
    g = pl.program_id(0)

    @pl.when(g == 0)
    def _zero_tails():
        zero_scr[...] = jnp.zeros_like(zero_scr)
        n_tiles = xs_hbm.shape[0] // tile
        min_used = n_tiles - N_CLASSES
        total = ends_ref[0, N_CLASSES - 1]
        for phase in ("start", "wait"):
            def zero_tile(row0):
                cp = pltpu.make_async_copy(zero_scr, xs_hbm.at[pl.ds(pl.multiple_of(row0, tile), tile)], sem)
                cp.start() if phase == "start" else cp.wait()

            for c in range(N_CLASSES):
                end = ends_ref[0, c]
                prev = ends_ref[0, c - 1] if c else 0
                pl.when(end > prev)(functools.partial(zero_tile, end - tile))
                pl.when((min_used + c) * tile >= total)(functools.partial(zero_tile, (min_used + c) * tile))

    for k in range(ch):
        _row_copy(h2_ref, k, xs_hbm, pos_ref[0, 0, k], sem).start(priority=k % DMA_THREADS)
    _row_copy(h2_ref, 0, xs_hbm, 0, sem, rows=ch).wait()


def _dispatch(pos3, ends, h2ext, *, n_rows, tile):
    n_chunks, _, ch = pos3.shape
    return pl.pallas_call(
        functools.partial(_dispatch_kernel, ch=ch, tile=tile),
        grid=(n_chunks,),
        in_specs=[
            pl.BlockSpec((1, 1, ch), lambda g: (g, 0, 0), memory_space=pltpu.SMEM),
            pl.BlockSpec(ends.shape, lambda g: (0, 0), memory_space=pltpu.SMEM),
            pl.BlockSpec((ch, EXT_WIDTH), lambda g: (g, 0)),
        ],
        out_specs=pl.BlockSpec(memory_space=pl.ANY),
        out_shape=jax.ShapeDtypeStruct((n_rows, EXT_WIDTH), F32),
        scratch_shapes=[pltpu.VMEM((tile, EXT_WIDTH), F32), pltpu.SemaphoreType.DMA(())],
        compiler_params=pltpu.CompilerParams(dimension_semantics=("arbitrary",), vmem_limit_bytes=VMEM_LIMIT),
        name="moe_dispatch",
    )(pos3, ends, h2ext)


def _experts_kernel(ea_ref, eb_ref, used_ref, xs_ref, wga, wua, wda, wgb, wub, wdb, ys_ref):
    del ea_ref, eb_ref

    @pl.when(pl.program_id(0) < used_ref[0])
    def _():
        x = xs_ref[:, 0:D_MODEL].astype(BF16)
        wts = xs_ref[:, D_MODEL:EXT_WIDTH]
        y = None
        for col, (wg, wu, wd) in enumerate(((wga, wua, wda), (wgb, wub, wdb))):
            gate = jnp.dot(x, wg[0], preferred_element_type=F32)
            up = jnp.dot(x, wu[0], preferred_element_type=F32)
            he = (gate / (1.0 + jnp.exp(-gate)) * up * wts[:, col:col + 1]).astype(BF16)
            t = jnp.dot(he, wd[0], preferred_element_type=F32)
            y = t if y is None else y + t
        ys_ref[...] = y

    @pl.when(pl.program_id(0) >= used_ref[0])
    def _():
        ys_ref[...] = jnp.zeros_like(ys_ref)


def _experts(ea, eb, used, xs, wg, wu, wd, *, tile):
    n_tiles = xs.shape[0] // tile
    rows = lambda j, ea, eb, used: (jnp.minimum(j, used[0] - 1), 0)
    first = lambda j, ea, eb, used: (ea[j], 0, 0)
    second = lambda j, ea, eb, used: (eb[j], 0, 0)
    up_spec = lambda m: pl.BlockSpec((1, D_MODEL, D_EXPERT), m)
    down_spec = lambda m: pl.BlockSpec((1, D_EXPERT, D_MODEL), m)
    return pl.pallas_call(
        _experts_kernel,
        grid_spec=pltpu.PrefetchScalarGridSpec(
            num_scalar_prefetch=3,
            grid=(n_tiles,),
            in_specs=[pl.BlockSpec((tile, EXT_WIDTH), rows),
                      up_spec(first), up_spec(first), down_spec(first),
                      up_spec(second), up_spec(second), down_spec(second)],
            out_specs=pl.BlockSpec((tile, D_MODEL), lambda j, ea, eb, used: (j, 0)),
        ),
        out_shape=jax.ShapeDtypeStruct((xs.shape[0], D_MODEL), F32),
        compiler_params=pltpu.CompilerParams(dimension_semantics=("arbitrary",), vmem_limit_bytes=VMEM_LIMIT),
        name="moe_experts",
    )(ea, eb, used, xs, wg, wu, wd, wg, wu, wd)


def _combine_kernel(pos_ref, next_pos_ref, x1_ref, g_ref, ys_hbm, out_ref, y_scr, sem):
    i = pl.program_id(0)
    tm = x1_ref.shape[0]
    slot = i & 1

    def gather(p_ref, s):
        for k in range(tm):
            _row_copy(ys_hbm, p_ref[0, 0, k], y_scr.at[s], k, sem.at[s]).start(priority=k % DMA_THREADS)

    @pl.when(i == 0)
    def _():
        gather(pos_ref, 0)

    @pl.when(i + 1 < pl.num_programs(0))
    def _():
        gather(next_pos_ref, 1 - slot)

    _row_copy(ys_hbm, 0, y_scr.at[slot], 0, sem.at[slot], rows=tm).wait()
    x2 = x1_ref[...] + y_scr[slot]
    ms = jnp.mean(x2 * x2, axis=-1, keepdims=True)
    out_ref[...] = x2 * lax.rsqrt(ms + NORM_EPS) * g_ref[...]


def _combine(pos3, x1, g, ys):
    n_chunks, _, tm = pos3.shape
    return pl.pallas_call(
        _combine_kernel,
        grid=(n_chunks,),
        in_specs=[
            pl.BlockSpec((1, 1, tm), lambda i: (i, 0, 0), memory_space=pltpu.SMEM),
            pl.BlockSpec((1, 1, tm), lambda i: (jnp.minimum(i + 1, n_chunks - 1), 0, 0), memory_space=pltpu.SMEM),
            pl.BlockSpec((tm, D_MODEL), lambda i: (i, 0)),
            pl.BlockSpec((1, D_MODEL), lambda i: (0, 0)),
            pl.BlockSpec(memory_space=pl.ANY),
        ],
        out_specs=pl.BlockSpec((tm, D_MODEL), lambda i: (i, 0)),
        out_shape=jax.ShapeDtypeStruct(x1.shape, F32),
        scratch_shapes=[pltpu.VMEM((2, tm, D_MODEL), F32), pltpu.SemaphoreType.DMA((2,))],
        compiler_params=pltpu.CompilerParams(dimension_semantics=("arbitrary",), vmem_limit_bytes=VMEM_LIMIT),
        name="moe_combine",
    )(pos3, pos3, x1, g, ys)


def _moe(h2ext, cls3, x1, wg, wu, wd, g, *, tile=MOE_TILE, dispatch_chunk=2048, combine_chunk=512):
    n = x1.shape[0]
    n_tiles = n // tile + N_CLASSES
    assert n_tiles <= LANES
    pos, tile_cls, ends = _sort(cls3.reshape(n // LANES, LANES), tile=tile)
    xs = _dispatch(pos.reshape(n // dispatch_chunk, 1, dispatch_chunk), ends, h2ext, n_rows=n_tiles * tile, tile=tile)
    used = ends[0, N_CLASSES - 1:N_CLASSES] // tile
    tc = tile_cls[0, :n_tiles]
    tc = jnp.where(jnp.arange(n_tiles) < used[0], tc, tc[jnp.maximum(used[0] - 1, 0)])
    pair_a = jnp.array([p[0] for p in PAIR_ORDER], jnp.int32)
    pair_b = jnp.array([p[1] for p in PAIR_ORDER], jnp.int32)
    ea = (tc // N_PAIRS) * EXPERTS_PER_GROUP + pair_a[tc % N_PAIRS]
    eb = (tc // N_PAIRS) * EXPERTS_PER_GROUP + pair_b[tc % N_PAIRS]
    ys = _experts(ea, eb, used, xs, wg, wu, wd, tile=tile)
    return _combine(pos.reshape(n // combine_chunk, 1, combine_chunk), x1, g, ys)


def _rotary_tables(seq):
    pos = np.arange(seq, dtype=np.float64)
    inv_freq = 1.0 / (ROPE_THETA ** (np.arange(0, ROT_DIM, 2, dtype=np.float64) / ROT_DIM))
    ang = pos[:, None] * inv_freq[None, :]
    cos, sin = np.cos(ang), np.sin(ang)
    zeros = np.zeros((seq, HEAD_DIM - ROT_DIM))
    zh = np.zeros((seq, ROT_HALF))
    cos_h = np.concatenate([cos, cos, np.ones((seq, HEAD_DIM - ROT_DIM))], axis=1)
    sa_h = np.concatenate([-sin, zh, zeros], axis=1)
    sb_h = np.concatenate([zh, sin, zeros], axis=1)
    tile = lambda t: jnp.asarray(np.concatenate([t, t], axis=1), F32)
    return tile(cos_h), tile(sa_h), tile(sb_h)


def kernel(x, attn_norm, w_in, b_forget, w_out, ffn_norm, w_group, w_expert, w_gate_e, w_up_e, w_down_e, final_norm):
    b, seq, d = x.shape
    assert d == D_MODEL and w_in.shape[0] == 1, "single-layer block"
    n = b * seq
    scale = HEAD_DIM ** -0.5 * LOG2E
    col_scale = np.ones((2 * QKV_WIDTH + LANES,), np.float32)
    col_scale[0:WIDTH_A] = scale
    col_scale[QKV_WIDTH:QKV_WIDTH + WIDTH_B] = scale
    w = (jnp.pad(w_in[0].T, ((0, LANES - HEADS_B), (0, 0))) * col_scale[:, None]).astype(BF16)
    bf = jnp.pad(b_forget[0].astype(F32), (0, LANES - HEADS_B))[None, :]
    cosv, sa, sb = _rotary_tables(seq)
    x2 = x.reshape(n, d)

    qkva, qkvb, c = _inproj(x2, attn_norm[0][None, :], w, cosv, sa, sb, bf, seq=seq, tm=512)
    out_a = _dilated(qkva.reshape(b, seq, QKV_WIDTH))
    out_b, wg, wu, wd = _fox(qkvb.reshape(b, seq, QKV_WIDTH), c.reshape(b, seq, LANES),
                             w_gate_e[0], w_up_e[0], w_down_e[0], tq=1024)

    wr = jnp.pad(jnp.concatenate([w_group[0], w_expert[0]], axis=1).astype(F32),
                 ((0, 0), (0, LANES - N_GROUPS - N_EXPERTS)))
    wr_hi = wr.astype(BF16)
    wr = jnp.concatenate([wr_hi, (wr - wr_hi.astype(F32)).astype(BF16)], axis=1)
    x1, h2ext, cls3 = _outproj(out_a.reshape(n, WIDTH_A), out_b.reshape(n, WIDTH_B), x2,
                               w_out[0].astype(BF16), ffn_norm[0][None, :], wr, tm=512)
    out = _moe(h2ext, cls3, x1, wg, wu, wd, final_norm[None, :])
    return out.reshape(b, seq, d)
```

```python
import functools

import jax
import jax.numpy as jnp
import numpy as np
from jax import lax
from jax.experimental import pallas as pl
from jax.experimental.pallas import tpu as pltpu

F32 = jnp.float32
BF16 = jnp.bfloat16

D_MODEL = 1024
HEAD_DIM = 64
HEADS_A = 8
HEADS_B = 8
WIDTH_A = HEADS_A * HEAD_DIM
WIDTH_B = HEADS_B * HEAD_DIM
QKV_WIDTH = 3 * WIDTH_A
DILATIONS = ((128, 1), (512, 4), (2048, 16))
ROT_DIM = HEAD_DIM // 4
ROT_HALF = ROT_DIM // 2
ROPE_THETA = 500000.0
N_GROUPS = 4
EXPERTS_PER_GROUP = 4
N_EXPERTS = N_GROUPS * EXPERTS_PER_GROUP
D_EXPERT = 512
NORM_EPS = 1e-6
NEG_INF = -1e30

LANES = 128
HEAD_PAIRS = WIDTH_A // LANES
WIN = 128
MAX_DIL = 16
MID_DIL = 4
N_MID = MAX_DIL // MID_DIL
assert sorted(d for _, d in DILATIONS) == [1, MID_DIL, MAX_DIL] and all(w // d == WIN for w, d in DILATIONS)
PERM = MAX_DIL * MAX_DIL
KV_BLOCK = 256
N_PAIRS = EXPERTS_PER_GROUP * (EXPERTS_PER_GROUP - 1) // 2
N_CLASSES = N_GROUPS * N_PAIRS
PAIR_ORDER = ((0, 1), (0, 2), (1, 2), (1, 3), (0, 3), (2, 3))
assert len(PAIR_ORDER) == N_PAIRS
MOE_TILE = 256
EXT_WIDTH = D_MODEL + LANES
LOGIT_MAX = 60.0
BRANCH_UNROLL = 16
DMA_THREADS = 2
ROUTER_OFF = N_GROUPS
VMEM_LIMIT = 56 * 1024 * 1024
LOG2E = 1.4426950408889634
BOUND_SLACK = 1.0 + 2.0 ** -6
BOUND_EPS = 2.0 ** -7


N_PARTS = 3


def _split3(x):
    hi = x.astype(BF16)
    r1 = x - hi.astype(F32)
    mid = r1.astype(BF16)
    lo = (r1 - mid.astype(F32)).astype(BF16)
    return hi, mid, lo


def _dot_nt(a, b):
    return lax.dot_general(a, b, (((1,), (1,)), ((), ())), preferred_element_type=F32)


def _inproj_kernel(x_ref, g_ref, w_ref, cos_ref, sa_ref, sb_ref, bf_ref,
                   qkva_ref, qkvb_ref, c_ref, h_scr, carry_scr, *, tiles_per_seq):
    i = pl.program_id(0)
    tm = x_ref.shape[0]
    x = x_ref[...]
    ms = jnp.mean(x * x, axis=-1, keepdims=True)
    h_scr[...] = (x * lax.rsqrt(ms + NORM_EPS) * g_ref[...]).astype(BF16)
    cosv = cos_ref[...]
    sa = sa_ref[...]
    sb = sb_ref[...]
    for g in range(6):
        p = _dot_nt(h_scr[...], w_ref[g * WIDTH_A:(g + 1) * WIDTH_A, :])
        dst = qkva_ref if g < 3 else qkvb_ref
        col0 = (g % 3) * WIDTH_A
        if g < 2:
            for k in range(HEAD_PAIRS):
                t = p[:, k * LANES:(k + 1) * LANES]
                t = (t * cosv + pltpu.roll(t, LANES - ROT_HALF, 1) * sa + pltpu.roll(t, ROT_HALF, 1) * sb)
                dst[:, col0 + k * LANES:col0 + (k + 1) * LANES] = t.astype(BF16)
        else:
            dst[:, col0:col0 + WIDTH_A] = p.astype(BF16)
    z = _dot_nt(h_scr[...], w_ref[2 * QKV_WIDTH:2 * QKV_WIDTH + LANES, :]) + bf_ref[...]
    lf = jnp.minimum(z, 0.0) - jnp.log1p(jnp.exp(-jnp.abs(z)))
    lane = lax.broadcasted_iota(jnp.int32, (tm, LANES), 1)
    lf = jnp.where(lane < HEADS_B, lf * LOG2E, 0.0)
    hi, mid, lo = _split3(lf)
    tri = (lax.broadcasted_iota(jnp.int32, (tm, tm), 0) >= lax.broadcasted_iota(jnp.int32, (tm, tm), 1)).astype(BF16)
    two = jnp.dot(tri, jnp.concatenate([hi, mid], axis=1), preferred_element_type=F32)
    cs = two[:, 0:LANES] + two[:, LANES:2 * LANES] + jnp.dot(tri, lo, preferred_element_type=F32)

    @pl.when(i % tiles_per_seq == 0)
    def _():
        carry_scr[...] = jnp.zeros_like(carry_scr)

    c = cs + carry_scr[0:1, :]
    c_ref[...] = c
    carry_scr[...] = jnp.broadcast_to(c[tm - 1:tm, :], carry_scr.shape)


def _inproj(x2, g, w, cosv, sa, sb, bf, *, seq, tm):
    n = x2.shape[0]
    kern = functools.partial(_inproj_kernel, tiles_per_seq=seq // tm)
    tps = seq // tm
    return pl.pallas_call(
        kern,
        grid=(n // tm,),
        in_specs=[
            pl.BlockSpec((tm, D_MODEL), lambda i: (i, 0)),
            pl.BlockSpec((1, D_MODEL), lambda i: (0, 0)),
            pl.BlockSpec(w.shape, lambda i: (0, 0)),
            pl.BlockSpec((tm, LANES), lambda i: (i % tps, 0)),
            pl.BlockSpec((tm, LANES), lambda i: (i % tps, 0)),
            pl.BlockSpec((tm, LANES), lambda i: (i % tps, 0)),
            pl.BlockSpec((1, LANES), lambda i: (0, 0)),
        ],
        out_specs=[
            pl.BlockSpec((tm, QKV_WIDTH), lambda i: (i, 0)),
            pl.BlockSpec((tm, QKV_WIDTH), lambda i: (i, 0)),
            pl.BlockSpec((tm, LANES), lambda i: (i, 0)),
        ],
        out_shape=[
            jax.ShapeDtypeStruct((n, QKV_WIDTH), BF16),
            jax.ShapeDtypeStruct((n, QKV_WIDTH), BF16),
            jax.ShapeDtypeStruct((n, LANES), F32),
        ],
        scratch_shapes=[pltpu.VMEM((tm, D_MODEL), BF16), pltpu.VMEM((8, LANES), F32)],
        compiler_params=pltpu.CompilerParams(dimension_semantics=("arbitrary",), vmem_limit_bytes=VMEM_LIMIT),
        name="inproj",
    )(x2, g, w, cosv, sa, sb, bf)


def _own_lanes(lane, hl):
    return (lane < HEAD_DIM) if hl == 0 else (lane >= HEAD_DIM)


def _head_sum_selector():
    lane = lax.broadcasted_iota(jnp.int32, (LANES, LANES), 0)
    col = lax.broadcasted_iota(jnp.int32, (LANES, LANES), 1)
    return jnp.where(((col == 0) & (lane < HEAD_DIM)) | ((col == 1) & (lane >= HEAD_DIM)), 1.0, 0.0).astype(BF16)


def _max_sq_norm(t, selector):
    return jnp.max(jnp.dot((t * t).astype(BF16), selector, preferred_element_type=F32), axis=0, keepdims=True)


def _norm_bound(qsq, ksq_max):
    return jnp.sqrt(qsq * ksq_max) * BOUND_SLACK + BOUND_EPS


def _fox_kernel(q_ref, k_ref, v_ref, c_ref, wg_ref, wu_ref, wd_ref, o_ref, wg_out, wu_out, wd_out,
                q0_scr, q1_scr, k0_scr, k1_scr, v0_scr, v1_scr, bound_scr, acc0_scr, acc1_scr, s_scr,
                *, tq, seq, chunk):
    hp = pl.program_id(1)
    qi = pl.program_id(2)
    for src, dst in ((wg_ref, wg_out), (wu_ref, wu_out), (wd_ref, wd_out)):
        dst[...] = src[...].astype(dst.dtype)
    q_scr = (q0_scr, q1_scr)
    k_scr = (k0_scr, k1_scr)
    v_scr = (v0_scr, v1_scr)
    acc_scr = (acc0_scr, acc1_scr)

    def placement(part):
        row = lax.broadcasted_iota(jnp.int32, (LANES, 2 * LANES), 0)
        col = lax.broadcasted_iota(jnp.int32, (LANES, 2 * LANES), 1)
        m = jnp.zeros((LANES, 2 * LANES), F32)
        for hl in range(2):
            first = HEAD_DIM if hl == 0 else 0
            src = row == 2 * hp + hl
            m = jnp.where(src & (col == first + part), 1.0, m)
            m = jnp.where(src & (col == LANES + first + N_PARTS + part), -1.0, m)
            if part == 0:
                ones_q = (col >= first + N_PARTS) & (col < first + 2 * N_PARTS)
                ones_k = (col >= LANES + first) & (col < LANES + first + N_PARTS)
                m = jnp.where((row == LANES - 1) & (ones_q | ones_k), 1.0, m)
        return m.astype(BF16)

    @pl.when(qi == 0)
    def _build():
        lane = lax.broadcasted_iota(jnp.int32, (chunk, LANES), 1)
        place_hi_mid = jnp.concatenate([placement(0), placement(1)], axis=0)
        place_lo = placement(2)

        def body(ci, sq_max):
            rows = pl.ds(pl.multiple_of(ci * chunk, chunk), chunk)
            qc = q_ref[0, rows, :].astype(F32)
            kc = k_ref[0, rows, :].astype(F32)
            vc = v_ref[0, rows, :].astype(F32)
            hi, mid, lo = _split3(jnp.where(lane == LANES - 1, 1.0, c_ref[0, rows, :]))
            aug = (jnp.dot(jnp.concatenate([hi, mid], axis=1), place_hi_mid, preferred_element_type=F32)
                   + jnp.dot(lo, place_lo, preferred_element_type=F32))
            for hl in range(2):
                own = _own_lanes(lane, hl)
                q_scr[hl][rows, :] = jnp.where(own, qc, aug[:, 0:LANES]).astype(BF16)
                k_scr[hl][rows, :] = jnp.where(own, kc, aug[:, LANES:2 * LANES]).astype(BF16)
                one_lane = HEAD_DIM if hl == 0 else 0
                v_scr[hl][rows, :] = jnp.where(own, vc, jnp.where(lane == one_lane, 1.0, 0.0)).astype(BF16)
            return (jnp.maximum(sq_max[0], _max_sq_norm(qc, selector)),
                    jnp.maximum(sq_max[1], _max_sq_norm(kc, selector)))

        selector = _head_sum_selector()
        sq_max = lax.fori_loop(0, seq // chunk, body, (jnp.zeros((1, LANES), F32),) * 2)
        bound_scr[...] = jnp.broadcast_to(_norm_bound(sq_max[0], sq_max[1]), bound_scr.shape)

    lane_q = lax.broadcasted_iota(jnp.int32, (tq, LANES), 1)
    q_rows = pl.ds(pl.multiple_of(qi * tq, tq), tq)
    worst = jnp.max(bound_scr[...])
    n_diag = tq // KV_BLOCK
    n_full = qi * n_diag
    row_minus_col = (lax.broadcasted_iota(jnp.int32, (tq, KV_BLOCK), 0)
                     - lax.broadcasted_iota(jnp.int32, (tq, KV_BLOCK), 1))

    def causal_mask(s, d, top=0):
        return jnp.where(row_minus_col[top:tq, :] >= d * KV_BLOCK, s, NEG_INF)

    def logits(hl, j, top=0):
        r0 = pl.multiple_of(j * KV_BLOCK, KV_BLOCK)
        rows = q_rows if top == 0 else pl.ds(pl.multiple_of(qi * tq + top, KV_BLOCK), tq - top)
        return lax.dot_general(q_scr[hl][rows, :], k_scr[hl][pl.ds(r0, KV_BLOCK), :], (((1,), (1,)), ((), ())),
                               preferred_element_type=F32)

    def values(hl, j):
        return v_scr[hl][pl.ds(pl.multiple_of(j * KV_BLOCK, KV_BLOCK), KV_BLOCK), :]

    def finish(a0, a1):
        l0 = a0[:, HEAD_DIM:HEAD_DIM + 1]
        l1 = a1[:, 0:1]
        o_ref[0] = jnp.where(lane_q < HEAD_DIM, a0 / l0, a1 / l1).astype(o_ref.dtype)

    @pl.when(worst <= LOGIT_MAX)
    def _fast():
        acc0_scr[...] = jnp.zeros_like(acc0_scr)
        acc1_scr[...] = jnp.zeros_like(acc1_scr)

        def produce(j, slot, top=0):
            for hl in range(2):
                s_scr[slot, hl, top:tq, :] = logits(hl, j, top)

        def consume(j, slot, diag):
            top = 0 if diag is None else diag * KV_BLOCK
            for hl in range(2):
                s = s_scr[slot, hl, top:tq, :]
                if diag is not None:
                    s = causal_mask(s, diag, top)
                acc_scr[hl][top:tq, :] += jnp.dot(jnp.exp2(s).astype(BF16), values(hl, j),
                                                  preferred_element_type=F32)

        def produce_diagonal(d):
            produce(n_full + d, d, d * KV_BLOCK)

        @pl.when(qi == 0)
        def _():
            for d in range(n_diag):
                produce_diagonal(d)

        @pl.when(qi > 0)
        def _():
            for d in range(n_diag):
                produce(d, d)

            def body(i, carry):
                for d in range(n_diag):
                    j = i * n_diag + d
                    consume(j, d, None)
                    produce(j + n_diag, d)
                return carry

            lax.fori_loop(0, qi - 1, body, 0)
            for d in range(n_diag):
                consume(n_full - n_diag + d, d, None)
                produce_diagonal(d)

        for d in range(n_diag):
            consume(n_full + d, d, d)
        finish(acc0_scr[...], acc1_scr[...])

    @pl.when(worst > LOGIT_MAX)
    def _general():
        def step(j, carry, diag):
            new = []
            for hl in range(2):
                m, a = carry[2 * hl:2 * hl + 2]
                s = logits(hl, j)
                if diag is not None:
                    s = causal_mask(s, diag)
                mn = jnp.maximum(m, jnp.max(s, axis=1, keepdims=True))
                a = jnp.exp2(m - mn) * a + jnp.dot(jnp.exp2(s - mn).astype(BF16), values(hl, j),
                                                  preferred_element_type=F32)
                new += [mn, a]
            return tuple(new)

        carry = (jnp.full((tq, 1), NEG_INF, F32), jnp.zeros((tq, LANES), F32)) * 2
        carry = lax.fori_loop(0, n_full, lambda j, c: step(j, c, None), carry)
        for d in range(n_diag):
            carry = step(n_full + d, carry, d)
        finish(carry[1], carry[3])


def _fox(qkvb, c, wg, wu, wd, *, tq, chunk=512):
    b, seq, _ = qkvb.shape
    nq = seq // tq
    steps = b * HEAD_PAIRS * nq
    parts = steps // N_EXPERTS
    assert parts * N_EXPERTS == steps and D_MODEL % parts == 0 and D_EXPERT % parts == 0

    def slice_spec(rows, cols):
        def index(bi, hp, qi):
            step = (bi * HEAD_PAIRS + hp) * nq + qi
            return (step // parts, step % parts, 0)
        return pl.BlockSpec((1, rows // parts, cols), index)

    w_specs = [slice_spec(D_MODEL, D_EXPERT), slice_spec(D_MODEL, D_EXPERT), slice_spec(D_EXPERT, D_MODEL)]
    kern = functools.partial(_fox_kernel, tq=tq, seq=seq, chunk=chunk)
    return pl.pallas_call(
        kern,
        grid=(b, HEAD_PAIRS, nq),
        in_specs=[
            pl.BlockSpec((1, seq, LANES), lambda bi, hp, qi: (bi, 0, hp)),
            pl.BlockSpec((1, seq, LANES), lambda bi, hp, qi: (bi, 0, HEAD_PAIRS + hp)),
            pl.BlockSpec((1, seq, LANES), lambda bi, hp, qi: (bi, 0, 2 * HEAD_PAIRS + hp)),
            pl.BlockSpec((1, seq, LANES), lambda bi, hp, qi: (bi, 0, 0)),
        ] + w_specs,
        out_specs=[pl.BlockSpec((1, tq, LANES), lambda bi, hp, qi: (bi, qi, hp))] + w_specs,
        out_shape=[jax.ShapeDtypeStruct((b, seq, WIDTH_B), BF16)]
        + [jax.ShapeDtypeStruct(w.shape, BF16) for w in (wg, wu, wd)],
        scratch_shapes=[pltpu.VMEM((seq, LANES), BF16)] * 6 + [
                        pltpu.VMEM((8, LANES), F32),
                        pltpu.VMEM((tq, LANES), F32), pltpu.VMEM((tq, LANES), F32),
                        pltpu.VMEM((tq // KV_BLOCK, 2, tq, KV_BLOCK), F32)],
        compiler_params=pltpu.CompilerParams(
            dimension_semantics=("arbitrary", "arbitrary", "arbitrary"), vmem_limit_bytes=VMEM_LIMIT),
        name="fox",
    )(qkvb, qkvb, qkvb, c, wg, wu, wd)


def _log2(n):
    assert n & (n - 1) == 0
    return n.bit_length() - 1


def _permutation():
    ri = lax.broadcasted_iota(jnp.int32, (PERM, PERM), 0)
    ci = lax.broadcasted_iota(jnp.int32, (PERM, PERM), 1)
    sh, lo = _log2(MAX_DIL), MAX_DIL - 1
    return (((ri >> sh) == (ci & lo)) & ((ri & lo) == (ci >> sh))).astype(BF16)


def _band_masks():
    qa = lax.broadcasted_iota(jnp.int32, (WIN, 2 * WIN), 0)
    kb = lax.broadcasted_iota(jnp.int32, (WIN, 2 * WIN), 1)
    cur = kb >= WIN
    kin = kb & (WIN - 1)
    prev_shift = jnp.where(cur, 0, WIN)
    dist = qa - kin + prev_shift
    chunk = WIN // N_MID
    offset = lambda a: ((a & (chunk - 1)) << _log2(N_MID)) + (a >> _log2(chunk))
    dist_mid = offset(qa) - offset(kin) + prev_shift
    return (dist >= 0) & (dist <= WIN), (dist_mid >= 0) & (dist_mid <= WIN), cur


def _dilated_general(q_ref, k_ref, v_ref, o_ref, q16, k16, v16, m_s, l_s, acc_s, on_s, lse_s, *, seq):
    sub = seq // MAX_DIL
    own0 = lax.broadcasted_iota(jnp.int32, (WIN, LANES), 1) < HEAD_DIM
    perm = _permutation()
    band_nat, band_4, cur = _band_masks()

    def attend(qb, kb, vb, mask, state):
        qf = qb.astype(F32)
        new = []
        for hl in range(2):
            m, l, a = state[3 * hl:3 * hl + 3]
            own = own0 if hl == 0 else jnp.logical_not(own0)
            qh = jnp.where(own, qf, 0.0).astype(BF16)
            s = lax.dot_general(qh, kb, (((1,), (1,)), ((), ())), preferred_element_type=F32)
            s = jnp.where(mask, s, NEG_INF)
            mn = jnp.maximum(m, jnp.max(s, axis=1, keepdims=True))
            alpha = jnp.exp2(m - mn)
            p = jnp.exp2(s - mn)
            l = alpha * l + jnp.sum(p, axis=1, keepdims=True)
            a = alpha * a + jnp.dot(p.astype(BF16), vb, preferred_element_type=F32)
            new += [mn, l, a]
        return new

    def pack_state(st):
        m0, l0, a0, m1, l1, a1 = st
        return (jnp.where(own0, m0, m1), jnp.where(own0, l0, l1), jnp.where(own0, a0, a1))

    def deint(blk, carry):
        r0 = pl.multiple_of(blk * PERM, PERM)
        j0 = pl.multiple_of(blk * MAX_DIL, MAX_DIL)
        for src, dst in ((q_ref, q16), (k_ref, k16), (v_ref, v16)):
            y = jnp.dot(perm, src[0, pl.ds(r0, PERM), :], preferred_element_type=F32).astype(BF16)
            for r in range(MAX_DIL):
                dst[r, pl.ds(j0, MAX_DIL), :] = y[r * MAX_DIL:(r + 1) * MAX_DIL, :]
        return carry

    lax.fori_loop(0, seq // PERM, deint, 0)

    fresh = [jnp.full((WIN, 1), NEG_INF, F32), jnp.zeros((WIN, 1), F32), jnp.zeros((WIN, LANES), F32)] * 2
    nb16 = sub // WIN

    def d16_body(t, carry):
        r = t // nb16
        n = t % nb16
        c0 = pl.multiple_of(n * WIN, WIN)
        p0 = pl.multiple_of(jnp.maximum(n - 1, 0) * WIN, WIN)
        qb = q16[r, pl.ds(c0, WIN), :]
        kb = jnp.concatenate([k16[r, pl.ds(p0, WIN), :], k16[r, pl.ds(c0, WIN), :]], axis=0)
        vb = jnp.concatenate([v16[r, pl.ds(p0, WIN), :], v16[r, pl.ds(c0, WIN), :]], axis=0)
        mask = band_nat & (cur | (n > 0))
        mm, ll, aa = pack_state(attend(qb, kb, vb, mask, fresh))
        row = pl.multiple_of(r * sub + c0, WIN)
        m_s[pl.ds(row, WIN), :] = mm
        l_s[pl.ds(row, WIN), :] = ll
        acc_s[pl.ds(row, WIN), :] = aa
        return carry

    lax.fori_loop(0, MAX_DIL * nb16, d16_body, 0)

    nb4 = (seq // MID_DIL) // WIN
    ch = WIN // N_MID

    def d4_body(t, carry):
        r4 = t // nb4
        n = t % nb4
        c0 = pl.multiple_of(n * ch, ch)
        p0 = pl.multiple_of(jnp.maximum(n - 1, 0) * ch, ch)

        def gather(ref, j0):
            return [ref[r4 + MID_DIL * q, pl.ds(j0, ch), :] for q in range(N_MID)]

        qb = jnp.concatenate(gather(q16, c0), axis=0)
        kb = jnp.concatenate(gather(k16, p0) + gather(k16, c0), axis=0)
        vb = jnp.concatenate(gather(v16, p0) + gather(v16, c0), axis=0)
        rows = [pl.multiple_of((r4 + MID_DIL * q) * sub + c0, ch) for q in range(N_MID)]
        mm = jnp.concatenate([m_s[pl.ds(rw, ch), :] for rw in rows], axis=0)
        ll = jnp.concatenate([l_s[pl.ds(rw, ch), :] for rw in rows], axis=0)
        aa = jnp.concatenate([acc_s[pl.ds(rw, ch), :] for rw in rows], axis=0)
        state = [mm[:, 0:1], ll[:, 0:1], aa, mm[:, HEAD_DIM:HEAD_DIM + 1], ll[:, HEAD_DIM:HEAD_DIM + 1], aa]
        mask = band_4 & (cur | (n > 0))
        mm, ll, aa = pack_state(attend(qb, kb, vb, mask, state))
        for q, rw in enumerate(rows):
            m_s[pl.ds(rw, ch), :] = mm[q * ch:(q + 1) * ch, :]
            l_s[pl.ds(rw, ch), :] = ll[q * ch:(q + 1) * ch, :]
            acc_s[pl.ds(rw, ch), :] = aa[q * ch:(q + 1) * ch, :]
        return carry

    lax.fori_loop(0, MID_DIL * nb4, d4_body, 0)

    def renat(blk, carry):
        j0 = pl.multiple_of(blk * MAX_DIL, MAX_DIL)
        r0 = pl.multiple_of(blk * PERM, PERM)
        rows = [pl.multiple_of(r * sub + j0, MAX_DIL) for r in range(MAX_DIL)]
        mm = jnp.concatenate([m_s[pl.ds(rw, MAX_DIL), :] for rw in rows], axis=0)
        ll = jnp.concatenate([l_s[pl.ds(rw, MAX_DIL), :] for rw in rows], axis=0)
        aa = jnp.concatenate([acc_s[pl.ds(rw, MAX_DIL), :] for rw in rows], axis=0)
        o = (aa / ll).astype(BF16)
        hi, mid, lo = _split3(mm + jnp.log2(ll))
        on_s[pl.ds(r0, PERM), :] = jnp.dot(perm, o, preferred_element_type=F32)
        lse_s[pl.ds(r0, PERM), :] = (jnp.dot(perm, hi, preferred_element_type=F32)
                                     + jnp.dot(perm, mid, preferred_element_type=F32)
                                     + jnp.dot(perm, lo, preferred_element_type=F32))
        return carry

    lax.fori_loop(0, seq // PERM, renat, 0)

    def d1_body(n, carry):
        c0 = pl.multiple_of(n * WIN, WIN)
        p0 = pl.multiple_of(jnp.maximum(n - 1, 0) * WIN, WIN)
        qb = q_ref[0, pl.ds(c0, WIN), :]
        kb = jnp.concatenate([k_ref[0, pl.ds(p0, WIN), :], k_ref[0, pl.ds(c0, WIN), :]], axis=0)
        vb = jnp.concatenate([v_ref[0, pl.ds(p0, WIN), :], v_ref[0, pl.ds(c0, WIN), :]], axis=0)
        lse = lse_s[pl.ds(c0, WIN), :]
        o = on_s[pl.ds(c0, WIN), :]
        one = jnp.ones((WIN, 1), F32)
        state = [lse[:, 0:1], one, o, lse[:, HEAD_DIM:HEAD_DIM + 1], one, o]
        mask = band_nat & (cur | (n > 0))
        m0, l0, a0, m1, l1, a1 = attend(qb, kb, vb, mask, state)
        o_ref[0, pl.ds(c0, WIN), :] = jnp.where(own0, a0 / l0, a1 / l1).astype(o_ref.dtype)
        return carry

    lax.fori_loop(0, seq // WIN, d1_body, 0)


def _dilated_fast(o_ref, qn, kn, vn, qr, kr, vr, fr, fn, s_scr, mask_scr, *, seq):
    sub = seq // MAX_DIL
    own0 = lax.broadcasted_iota(jnp.int32, (WIN, LANES), 1) < HEAD_DIM
    head_mask = [jnp.where(own0, 1.0, 0.0).astype(BF16), jnp.where(own0, 0.0, 1.0).astype(BF16)]
    perm = _permutation()

    band_nat, band_4, cur = _band_masks()
    for i, band in enumerate((band_nat, band_4)):
        mask_scr[2 * i] = jnp.where(band & cur, 1.0, 0.0).astype(BF16)
        mask_scr[2 * i + 1] = jnp.where(band, 1.0, 0.0).astype(BF16)

    def run_branch(nblocks, block_in_seq, fetch, mask_base, sink, before_trip=None):
        def produce(t, pair, u):
            for hl in range(2):
                qb, kb = fetch(t, hl, True)
                s_scr[pair, u, hl] = lax.dot_general(qb * head_mask[hl], kb, (((1,), (1,)), ((), ())),
                                                     preferred_element_type=F32)

        def consume(t, pair, u):
            mk = mask_scr[mask_base + jnp.minimum(block_in_seq(t), 1)]
            pv = []
            for hl in range(2):
                p = jnp.exp2(s_scr[pair, u, hl]).astype(BF16) * mk
                pv.append(jnp.dot(p, fetch(t, hl, False), preferred_element_type=F32))
            sink(t, pv)

        for u in range(BRANCH_UNROLL):
            produce(u, 0, u)

        def trip(i, last):
            pair = i & 1
            if before_trip is not None:
                before_trip(i)
            for u in range(BRANCH_UNROLL):
                consume(BRANCH_UNROLL * i + u, pair, u)
            if not last:
                for u in range(BRANCH_UNROLL):
                    produce(BRANCH_UNROLL * (i + 1) + u, 1 - pair, u)

        def body(i, carry):
            trip(i, False)
            return carry

        trips = nblocks // BRANCH_UNROLL
        lax.fori_loop(0, trips - 1, body, 0)
        trip(jnp.int32(trips - 1), True)

    def prev_cur(ref_block, n, size):
        c0 = pl.multiple_of(n * size, size)
        p0 = pl.multiple_of(jnp.maximum(n - 1, 0) * size, size)
        return ref_block(p0) + ref_block(c0)

    nb16 = sub // WIN

    def fetch16(t, hl, qk):
        r, n = t // nb16, t % nb16
        if qk:
            return (qr[hl][r, pl.ds(pl.multiple_of(n * WIN, WIN), WIN), :],
                    jnp.concatenate(prev_cur(lambda s: [kr[hl][r, pl.ds(s, WIN), :]], n, WIN), axis=0))
        return jnp.concatenate(prev_cur(lambda s: [vr[hl][r, pl.ds(s, WIN), :]], n, WIN), axis=0)

    def sink16(t, pv):
        row = pl.multiple_of(t * WIN, WIN)
        for hl in range(2):
            fr[hl][pl.ds(row, WIN), :] = pv[hl]

    run_branch(MAX_DIL * nb16, lambda t: t % nb16, fetch16, 0, sink16)

    nb4 = (seq // MID_DIL) // WIN
    ch = WIN // N_MID

    def fetch4(t, hl, qk):
        r4, n = t // nb4, t % nb4
        gather = lambda ref: (lambda s: [ref[r4 + MID_DIL * q, pl.ds(s, ch), :] for q in range(N_MID)])
        if qk:
            return (jnp.concatenate(gather(qr[hl])(pl.multiple_of(n * ch, ch)), axis=0),
                    jnp.concatenate(prev_cur(gather(kr[hl]), n, ch), axis=0))
        return jnp.concatenate(prev_cur(gather(vr[hl]), n, ch), axis=0)

    def sink4(t, pv):
        r4, n = t // nb4, t % nb4
        for q in range(N_MID):
            row = pl.multiple_of((r4 + MID_DIL * q) * sub + n * ch, ch)
            for hl in range(2):
                fr[hl][pl.ds(row, ch), :] += pv[hl][q * ch:(q + 1) * ch, :]

    run_branch(MID_DIL * nb4, lambda t: t % nb4, fetch4, 2, sink4)

    def renat(i):
        for u in range(BRANCH_UNROLL * WIN // PERM):
            blk = i * (BRANCH_UNROLL * WIN // PERM) + u
            j0 = pl.multiple_of(blk * MAX_DIL, MAX_DIL)
            r0 = pl.multiple_of(blk * PERM, PERM)
            for hl in range(2):
                a = jnp.concatenate([fr[hl][pl.ds(pl.multiple_of(r * sub + j0, MAX_DIL), MAX_DIL), :]
                                     for r in range(MAX_DIL)], axis=0)
                hi = a.astype(BF16)
                lo = (a - hi.astype(F32)).astype(BF16)
                y = jnp.dot(perm, jnp.concatenate([hi, lo], axis=1), preferred_element_type=F32)
                fn[hl][pl.ds(r0, PERM), :] = y[:, 0:LANES] + y[:, LANES:2 * LANES]

    def fetch1(t, hl, qk):
        if qk:
            return (qn[hl][pl.ds(pl.multiple_of(t * WIN, WIN), WIN), :],
                    jnp.concatenate(prev_cur(lambda s: [kn[hl][pl.ds(s, WIN), :]], t, WIN), axis=0))
        return jnp.concatenate(prev_cur(lambda s: [vn[hl][pl.ds(s, WIN), :]], t, WIN), axis=0)

    def sink1(t, pv):
        row = pl.multiple_of(t * WIN, WIN)
        t0 = fn[0][pl.ds(row, WIN), :] + pv[0]
        t1 = fn[1][pl.ds(row, WIN), :] + pv[1]
        o_ref[0, pl.ds(row, WIN), :] = jnp.where(own0, t0 / t0[:, HEAD_DIM:HEAD_DIM + 1],
                                                 t1 / t1[:, 0:1]).astype(o_ref.dtype)

    run_branch(seq // WIN, lambda t: t, fetch1, 0, sink1, before_trip=renat)


def _dilated_kernel(q_ref, k_ref, v_ref, o_ref, vn0, vn1, qr, kr, vr0, vr1,
                    f0, f1, f2, f3, f4, s_scr, mask_scr, *, seq, chunk):
    lane = lax.broadcasted_iota(jnp.int32, (chunk, LANES), 1)
    vn = (vn0, vn1)

    perm = _permutation()

    def stats(ci, sq_max):
        rows = pl.ds(pl.multiple_of(ci * chunk, chunk), chunk)
        qb = q_ref[0, rows, :]
        kb = k_ref[0, rows, :]
        vc = v_ref[0, rows, :].astype(F32)
        vb = []
        for hl in range(2):
            ones_lane = HEAD_DIM if hl == 0 else 0
            vb.append(jnp.where(_own_lanes(lane, hl), vc, jnp.where(lane == ones_lane, 1.0, 0.0)).astype(BF16))
            vn[hl][rows, :] = vb[hl]
        for u in range(chunk // PERM):
            j0 = pl.multiple_of((ci * (chunk // PERM) + u) * MAX_DIL, MAX_DIL)
            for srcs, dsts in (((qb, kb), (qr, kr)), (vb, (vr0, vr1))):
                both = jnp.concatenate([src[u * PERM:(u + 1) * PERM, :] for src in srcs], axis=1)
                y = jnp.dot(perm, both, preferred_element_type=F32).astype(BF16)
                for r in range(MAX_DIL):
                    for half, dst in enumerate(dsts):
                        dst[r, pl.ds(j0, MAX_DIL), :] = y[r * MAX_DIL:(r + 1) * MAX_DIL, half * LANES:(half + 1) * LANES]
        return (jnp.maximum(sq_max[0], _max_sq_norm(qb.astype(F32), selector)),
                jnp.maximum(sq_max[1], _max_sq_norm(kb.astype(F32), selector)))

    selector = _head_sum_selector()
    sq_max = lax.fori_loop(0, seq // chunk, stats, (jnp.zeros((1, LANES), F32),) * 2)
    worst = jnp.max(_norm_bound(sq_max[0], sq_max[1]))

    @pl.when(worst <= LOGIT_MAX)
    def _fast():
        qn, kn = q_ref.at[0], k_ref.at[0]
        _dilated_fast(o_ref, (qn, qn), (kn, kn), vn, (qr, qr), (kr, kr), (vr0, vr1), (f0, f1), (f2, f3),
                      s_scr, mask_scr, seq=seq)

    @pl.when(worst > LOGIT_MAX)
    def _general():
        _dilated_general(q_ref, k_ref, v_ref, o_ref, qr, kr, vr0, f0, f1, f2, f3, f4, seq=seq)


def _dilated(qkva, *, chunk=512):
    b, seq, _ = qkva.shape
    sub = seq // MAX_DIL
    kern = functools.partial(_dilated_kernel, seq=seq, chunk=chunk)
    return pl.pallas_call(
        kern,
        grid=(b, HEAD_PAIRS),
        in_specs=[
            pl.BlockSpec((1, seq, LANES), lambda bi, hp: (bi, 0, hp)),
            pl.BlockSpec((1, seq, LANES), lambda bi, hp: (bi, 0, HEAD_PAIRS + hp)),
            pl.BlockSpec((1, seq, LANES), lambda bi, hp: (bi, 0, 2 * HEAD_PAIRS + hp)),
        ],
        out_specs=pl.BlockSpec((1, seq, LANES), lambda bi, hp: (bi, 0, hp)),
        out_shape=jax.ShapeDtypeStruct((b, seq, WIDTH_A), BF16),
        scratch_shapes=(
            [pltpu.VMEM((seq, LANES), BF16)] * 2
            + [pltpu.VMEM((MAX_DIL, sub, LANES), BF16)] * 4
            + [pltpu.VMEM((seq, LANES), F32)] * 5
            + [pltpu.VMEM((2, BRANCH_UNROLL, 2, WIN, 2 * WIN), F32), pltpu.VMEM((4, WIN, 2 * WIN), BF16)]
        ),
        compiler_params=pltpu.CompilerParams(
            dimension_semantics=("arbitrary", "arbitrary"), vmem_limit_bytes=VMEM_LIMIT),
        name="dilated",
    )(qkva, qkva, qkva)


def _outproj_kernel(oa_ref, ob_ref, x_ref, wo_ref, g_ref, wr_ref, x1_ref, h2_ref, cls_ref):
    tm = x_ref.shape[0]
    x1 = (x_ref[...]
          + jnp.dot(oa_ref[...], wo_ref[0:WIDTH_A, :], preferred_element_type=F32)
          + jnp.dot(ob_ref[...], wo_ref[WIDTH_A:WIDTH_A + WIDTH_B, :], preferred_element_type=F32))
    x1_ref[...] = x1
    ms = jnp.mean(x1 * x1, axis=-1, keepdims=True)
    h2 = x1 * lax.rsqrt(ms + NORM_EPS) * g_ref[...]
    h2_ref[:, 0:D_MODEL] = h2
    h2_hi = h2.astype(BF16)
    h2_lo = (h2 - h2_hi.astype(F32)).astype(BF16)
    both = jnp.dot(h2_hi, wr_ref[...], preferred_element_type=F32)
    logits = (both[:, 0:LANES] + both[:, LANES:2 * LANES]
              + jnp.dot(h2_lo, wr_ref[:, 0:LANES], preferred_element_type=F32))
    lane = lax.broadcasted_iota(jnp.int32, (tm, LANES), 1)
    lane_f = lane.astype(F32)
    big = float(LANES)
    gmask = lane < N_GROUPS
    gl = jnp.where(gmask, logits, NEG_INF)
    gmax = jnp.max(gl, axis=1, keepdims=True)
    gsum = jnp.sum(jnp.where(gmask, jnp.exp(gl - gmax), 0.0), axis=1, keepdims=True)
    p_top = 1.0 / gsum
    g_star = jnp.min(jnp.where(gmask & (gl == gmax), lane_f, big), axis=1, keepdims=True)
    lo_lane = ROUTER_OFF + EXPERTS_PER_GROUP * g_star
    emask = (lane_f >= lo_lane) & (lane_f < lo_lane + EXPERTS_PER_GROUP)
    sel = jnp.where(emask, logits, NEG_INF)
    v1 = jnp.max(sel, axis=1, keepdims=True)
    i1 = jnp.min(jnp.where(emask & (sel == v1), lane_f, big), axis=1, keepdims=True)
    rest = emask & (lane_f != i1)
    sel2 = jnp.where(rest, logits, NEG_INF)
    v2 = jnp.max(sel2, axis=1, keepdims=True)
    i2 = jnp.min(jnp.where(rest & (sel2 == v2), lane_f, big), axis=1, keepdims=True)
    e2 = jnp.exp(v2 - v1)
    w1 = p_top / (1.0 + e2)
    w2 = p_top * e2 / (1.0 + e2)
    e1 = i1 - lo_lane
    e2x = i2 - lo_lane
    a = jnp.minimum(e1, e2x)
    b = jnp.maximum(e1, e2x)
    pair = jnp.zeros_like(a)
    for idx, (pa, pb) in enumerate(PAIR_ORDER):
        pair = jnp.where((a == pa) & (b == pb), float(idx), pair)
    cls = g_star * N_PAIRS + pair
    wa = jnp.where(e1 < e2x, w1, w2)
    wb = jnp.where(e1 < e2x, w2, w1)
    h2_ref[:, D_MODEL:EXT_WIDTH] = jnp.where(lane == 0, wa, jnp.where(lane == 1, wb, 0.0))
    row = lax.broadcasted_iota(jnp.int32, (tm, LANES), 0)
    spread = jnp.where(lane == (row & (LANES - 1)), cls, 0.0)
    cls_ref[0] = jnp.sum(spread.reshape(tm // LANES, LANES, LANES), axis=1).astype(jnp.int32)


def _outproj(oa, ob, x2, wo, g, wr, *, tm):
    n = x2.shape[0]
    return pl.pallas_call(
        _outproj_kernel,
        grid=(n // tm,),
        in_specs=[
            pl.BlockSpec((tm, WIDTH_A), lambda i: (i, 0)),
            pl.BlockSpec((tm, WIDTH_B), lambda i: (i, 0)),
            pl.BlockSpec((tm, D_MODEL), lambda i: (i, 0)),
            pl.BlockSpec(wo.shape, lambda i: (0, 0)),
            pl.BlockSpec((1, D_MODEL), lambda i: (0, 0)),
            pl.BlockSpec(wr.shape, lambda i: (0, 0)),
        ],
        out_specs=[
            pl.BlockSpec((tm, D_MODEL), lambda i: (i, 0)),
            pl.BlockSpec((tm, EXT_WIDTH), lambda i: (i, 0)),
            pl.BlockSpec((1, tm // LANES, LANES), lambda i: (i, 0, 0)),
        ],
        out_shape=[
            jax.ShapeDtypeStruct((n, D_MODEL), F32),
            jax.ShapeDtypeStruct((n, EXT_WIDTH), F32),
            jax.ShapeDtypeStruct((n // tm, tm // LANES, LANES), jnp.int32),
        ],
        compiler_params=pltpu.CompilerParams(dimension_semantics=("arbitrary",), vmem_limit_bytes=VMEM_LIMIT),
        name="outproj",
    )(oa, ob, x2, wo, g, wr)


def _sort_kernel(cls_ref, pos_ref, tile_ref, ends_ref, *, tile):
    cls = cls_ref[...]
    rows = cls.shape[0]
    upper = (lax.broadcasted_iota(jnp.int32, (LANES, LANES), 0)
             <= lax.broadcasted_iota(jnp.int32, (LANES, LANES), 1)).astype(BF16)
    earlier_rows = (lax.broadcasted_iota(jnp.int32, (rows, rows), 0)
                    > lax.broadcasted_iota(jnp.int32, (rows, rows), 1)).astype(BF16)
    ones = jnp.ones((LANES, LANES), BF16)
    lane8 = lax.broadcasted_iota(jnp.int32, tile_ref.shape, 1)
    tile_start = lane8.astype(F32) * tile
    off = jnp.zeros((1, 1), F32)
    pos = jnp.zeros(cls.shape, F32)
    tile_cls = jnp.zeros(tile_ref.shape, F32)
    ends = jnp.zeros(tile_ref.shape, F32)
    for c in range(N_CLASSES):
        hit = cls == c
        hot = jnp.where(hit, 1.0, 0.0).astype(BF16)
        in_row = jnp.dot(hot, upper, preferred_element_type=F32)
        row_tot = jnp.dot(hot, ones, preferred_element_type=F32)
        before = jnp.dot(earlier_rows, row_tot.astype(BF16), preferred_element_type=F32)
        count = before[rows - 1:rows, 0:1] + row_tot[rows - 1:rows, 0:1]
        pos = pos + jnp.where(hit, in_row - 1.0 + before + off, 0.0)
        off = off + jnp.ceil(count / tile) * tile
        tile_cls = tile_cls + jnp.where(tile_start >= off, 1.0, 0.0)
        ends = jnp.where(lane8 == c, off, ends)
    pos_ref[...] = pos.astype(jnp.int32)
    tile_ref[...] = tile_cls.astype(jnp.int32)
    ends_ref[...] = ends.astype(jnp.int32)


def _sort(cls2d, *, tile):
    meta = jax.ShapeDtypeStruct((8, LANES), jnp.int32)
    return pl.pallas_call(
        functools.partial(_sort_kernel, tile=tile),
        out_shape=[jax.ShapeDtypeStruct(cls2d.shape, jnp.int32), meta, meta],
        compiler_params=pltpu.CompilerParams(vmem_limit_bytes=VMEM_LIMIT),
        name="moe_sort",
    )(cls2d)


def _row_copy(src, src_row, dst, dst_row, sem, rows=1):
    return pltpu.make_async_copy(src.at[pl.ds(src_row, rows)], dst.at[pl.ds(dst_row, rows)], sem)


def _dispatch_kernel(pos_ref, ends_ref, h2_ref, xs_hbm, zero_scr, sem, *, ch, tile):
    g = pl.program_id(0)

    @pl.when(g == 0)
    def _zero_tails():
        zero_scr[...] = jnp.zeros_like(zero_scr)
        n_tiles = xs_hbm.shape[0] // tile
        min_used = n_tiles - N_CLASSES
        total = ends_ref[0, N_CLASSES - 1]
        for phase in ("start", "wait"):
            def zero_tile(row0):
                cp = pltpu.make_async_copy(zero_scr, xs_hbm.at[pl.ds(pl.multiple_of(row0, tile), tile)], sem)
                cp.start() if phase == "start" else cp.wait()

            for c in range(N_CLASSES):
                end = ends_ref[0, c]
                prev = ends_ref[0, c - 1] if c else 0
                pl.when(end > prev)(functools.partial(zero_tile, end - tile))
                pl.when((min_used + c) * tile >= total)(functools.partial(zero_tile, (min_used + c) * tile))

    for k in range(ch):
        _row_copy(h2_ref, k, xs_hbm, pos_ref[0, 0, k], sem).start(priority=k % DMA_THREADS)
    _row_copy(h2_ref, 0, xs_hbm, 0, sem, rows=ch).wait()


def _dispatch(pos3, ends, h2ext, *, n_rows, tile):
    n_chunks, _, ch = pos3.shape
    return pl.pallas_call(
        functools.partial(_dispatch_kernel, ch=ch, tile=tile),
        grid=(n_chunks,),
        in_specs=[
            pl.BlockSpec((1, 1, ch), lambda g: (g, 0, 0), memory_space=pltpu.SMEM),
            pl.BlockSpec(ends.shape, lambda g: (0, 0), memory_space=pltpu.SMEM),
            pl.BlockSpec((ch, EXT_WIDTH), lambda g: (g, 0)),
        ],
        out_specs=pl.BlockSpec(memory_space=pl.ANY),
        out_shape=jax.ShapeDtypeStruct((n_rows, EXT_WIDTH), F32),
        scratch_shapes=[pltpu.VMEM((tile, EXT_WIDTH), F32), pltpu.SemaphoreType.DMA(())],
        compiler_params=pltpu.CompilerParams(dimension_semantics=("arbitrary",), vmem_limit_bytes=VMEM_LIMIT),
        name="moe_dispatch",
    )(pos3, ends, h2ext)


def _experts_kernel(ea_ref, eb_ref, used_ref, xs_ref, wga, wua, wda, wgb, wub, wdb, ys_ref):
    del ea_ref, eb_ref

    @pl.when(pl.program_id(0) < used_ref[0])
    def _():
        x = xs_ref[:, 0:D_MODEL].astype(BF16)
        wts = xs_ref[:, D_MODEL:EXT_WIDTH]
        y = None
        for col, (wg, wu, wd) in enumerate(((wga, wua, wda), (wgb, wub, wdb))):
            gate = jnp.dot(x, wg[0], preferred_element_type=F32)
            up = jnp.dot(x, wu[0], preferred_element_type=F32)
            he = (gate / (1.0 + jnp.exp(-gate)) * up * wts[:, col:col + 1]).astype(BF16)
            t = jnp.dot(he, wd[0], preferred_element_type=F32)
            y = t if y is None else y + t
        ys_ref[...] = y

    @pl.when(pl.program_id(0) >= used_ref[0])
    def _():
        ys_ref[...] = jnp.zeros_like(ys_ref)


def _experts(ea, eb, used, xs, wg, wu, wd, *, tile):
    n_tiles = xs.shape[0] // tile
    rows = lambda j, ea, eb, used: (jnp.minimum(j, used[0] - 1), 0)
    first = lambda j, ea, eb, used: (ea[j], 0, 0)
    second = lambda j, ea, eb, used: (eb[j], 0, 0)
    up_spec = lambda m: pl.BlockSpec((1, D_MODEL, D_EXPERT), m)
    down_spec = lambda m: pl.BlockSpec((1, D_EXPERT, D_MODEL), m)
    return pl.pallas_call(
        _experts_kernel,
        grid_spec=pltpu.PrefetchScalarGridSpec(
            num_scalar_prefetch=3,
            grid=(n_tiles,),
            in_specs=[pl.BlockSpec((tile, EXT_WIDTH), rows),
                      up_spec(first), up_spec(first), down_spec(first),
                      up_spec(second), up_spec(second), down_spec(second)],
            out_specs=pl.BlockSpec((tile, D_MODEL), lambda j, ea, eb, used: (j, 0)),
        ),
        out_shape=jax.ShapeDtypeStruct((xs.shape[0], D_MODEL), F32),
        compiler_params=pltpu.CompilerParams(dimension_semantics=("arbitrary",), vmem_limit_bytes=VMEM_LIMIT),
        name="moe_experts",
    )(ea, eb, used, xs, wg, wu, wd, wg, wu, wd)


def _combine_kernel(pos_ref, next_pos_ref, x1_ref, g_ref, ys_hbm, out_ref, y_scr, sem):
    i = pl.program_id(0)
    tm = x1_ref.shape[0]
    slot = i & 1

    def gather(p_ref, s):
        for k in range(tm):
            _row_copy(ys_hbm, p_ref[0, 0, k], y_scr.at[s], k, sem.at[s]).start(priority=k % DMA_THREADS)

    @pl.when(i == 0)
    def _():
        gather(pos_ref, 0)

    @pl.when(i + 1 < pl.num_programs(0))
    def _():
        gather(next_pos_ref, 1 - slot)

    _row_copy(ys_hbm, 0, y_scr.at[slot], 0, sem.at[slot], rows=tm).wait()
    x2 = x1_ref[...] + y_scr[slot]
    ms = jnp.mean(x2 * x2, axis=-1, keepdims=True)
    out_ref[...] = x2 * lax.rsqrt(ms + NORM_EPS) * g_ref[...]


def _combine(pos3, x1, g, ys):
    n_chunks, _, tm = pos3.shape
    return pl.pallas_call(
        _combine_kernel,
        grid=(n_chunks,),
        in_specs=[
            pl.BlockSpec((1, 1, tm), lambda i: (i, 0, 0), memory_space=pltpu.SMEM),
            pl.BlockSpec((1, 1, tm), lambda i: (jnp.minimum(i + 1, n_chunks - 1), 0, 0), memory_space=pltpu.SMEM),
            pl.BlockSpec((tm, D_MODEL), lambda i: (i, 0)),
            pl.BlockSpec((1, D_MODEL), lambda i: (0, 0)),
            pl.BlockSpec(memory_space=pl.ANY),
        ],
        out_specs=pl.BlockSpec((tm, D_MODEL), lambda i: (i, 0)),
        out_shape=jax.ShapeDtypeStruct(x1.shape, F32),
        scratch_shapes=[pltpu.VMEM((2, tm, D_MODEL), F32), pltpu.SemaphoreType.DMA((2,))],
        compiler_params=pltpu.CompilerParams(dimension_semantics=("arbitrary",), vmem_limit_bytes=VMEM_LIMIT),
        name="moe_combine",
    )(pos3, pos3, x1, g, ys)


def _moe(h2ext, cls3, x1, wg, wu, wd, g, *, tile=MOE_TILE, dispatch_chunk=2048, combine_chunk=512):
    n = x1.shape[0]
    n_tiles = n // tile + N_CLASSES
    assert n_tiles <= LANES
    pos, tile_cls, ends = _sort(cls3.reshape(n // LANES, LANES), tile=tile)
    xs = _dispatch(pos.reshape(n // dispatch_chunk, 1, dispatch_chunk), ends, h2ext, n_rows=n_tiles * tile, tile=tile)
    used = ends[0, N_CLASSES - 1:N_CLASSES] // tile
    tc = tile_cls[0, :n_tiles]
    tc = jnp.where(jnp.arange(n_tiles) < used[0], tc, tc[jnp.maximum(used[0] - 1, 0)])
    pair_a = jnp.array([p[0] for p in PAIR_ORDER], jnp.int32)
    pair_b = jnp.array([p[1] for p in PAIR_ORDER], jnp.int32)
    ea = (tc // N_PAIRS) * EXPERTS_PER_GROUP + pair_a[tc % N_PAIRS]
    eb = (tc // N_PAIRS) * EXPERTS_PER_GROUP + pair_b[tc % N_PAIRS]
    ys = _experts(ea, eb, used, xs, wg, wu, wd, tile=tile)
    return _combine(pos.reshape(n // combine_chunk, 1, combine_chunk), x1, g, ys)


def _rotary_tables(seq):
    pos = np.arange(seq, dtype=np.float64)
    inv_freq = 1.0 / (ROPE_THETA ** (np.arange(0, ROT_DIM, 2, dtype=np.float64) / ROT_DIM))
    ang = pos[:, None] * inv_freq[None, :]
    cos, sin = np.cos(ang), np.sin(ang)
    zeros = np.zeros((seq, HEAD_DIM - ROT_DIM))
    zh = np.zeros((seq, ROT_HALF))
    cos_h = np.concatenate([cos, cos, np.ones((seq, HEAD_DIM - ROT_DIM))], axis=1)
    sa_h = np.concatenate([-sin, zh, zeros], axis=1)
    sb_h = np.concatenate([zh, sin, zeros], axis=1)
    tile = lambda t: jnp.asarray(np.concatenate([t, t], axis=1), F32)
    return tile(cos_h), tile(sa_h), tile(sb_h)


def kernel(x, attn_norm, w_in, b_forget, w_out, ffn_norm, w_group, w_expert, w_gate_e, w_up_e, w_down_e, final_norm):
    b, seq, d = x.shape
    assert d == D_MODEL and w_in.shape[0] == 1, "single-layer block"
    n = b * seq
    scale = HEAD_DIM ** -0.5 * LOG2E
    col_scale = np.ones((2 * QKV_WIDTH + LANES,), np.float32)
    col_scale[0:WIDTH_A] = scale
    col_scale[QKV_WIDTH:QKV_WIDTH + WIDTH_B] = scale
    w = (jnp.pad(w_in[0].T, ((0, LANES - HEADS_B), (0, 0))) * col_scale[:, None]).astype(BF16)
    bf = jnp.pad(b_forget[0].astype(F32), (0, LANES - HEADS_B))[None, :]
    cosv, sa, sb = _rotary_tables(seq)
    x2 = x.reshape(n, d)

    qkva, qkvb, c = _inproj(x2, attn_norm[0][None, :], w, cosv, sa, sb, bf, seq=seq, tm=512)
    out_a = _dilated(qkva.reshape(b, seq, QKV_WIDTH))
    out_b, wg, wu, wd = _fox(qkvb.reshape(b, seq, QKV_WIDTH), c.reshape(b, seq, LANES),
                             w_gate_e[0], w_up_e[0], w_down_e[0], tq=1024)

    wr = jnp.pad(jnp.concatenate([w_group[0], w_expert[0]], axis=1).astype(F32),
                 ((0, 0), (0, LANES - N_GROUPS - N_EXPERTS)))
    wr_hi = wr.astype(BF16)
    wr = jnp.concatenate([wr_hi, (wr - wr_hi.astype(F32)).astype(BF16)], axis=1)
    x1, h2ext, cls3 = _outproj(out_a.reshape(n, WIDTH_A), out_b.reshape(n, WIDTH_B), x2,
                               w_out[0].astype(BF16), ffn_norm[0][None, :], wr, tm=512)
    out = _moe(h2ext, cls3, x1, wg, wu, wd, final_norm[None, :])
    return out.reshape(b, seq, d)
```

```python
import functools

import jax
import jax.numpy as jnp
import numpy as np
from jax import lax
from jax.experimental import pallas as pl
from jax.experimental.pallas import tpu as pltpu

F32 = jnp.float32
BF16 = jnp.bfloat16

D_MODEL = 1024
HEAD_DIM = 64
HEADS_A = 8
HEADS_B = 8
WIDTH_A = HEADS_A * HEAD_DIM
WIDTH_B = HEADS_B * HEAD_DIM
QKV_WIDTH = 3 * WIDTH_A
DILATIONS = ((128, 1), (512, 4), (2048, 16))
ROT_DIM = HEAD_DIM // 4
ROT_HALF = ROT_DIM // 2
ROPE_THETA = 500000.0
N_GROUPS = 4
EXPERTS_PER_GROUP = 4
N_EXPERTS = N_GROUPS * EXPERTS_PER_GROUP
D_EXPERT = 512
NORM_EPS = 1e-6
NEG_INF = -1e30

LANES = 128
HEAD_PAIRS = WIDTH_A // LANES
WIN = 128
MAX_DIL = 16
MID_DIL = 4
N_MID = MAX_DIL // MID_DIL
assert sorted(d for _, d in DILATIONS) == [1, MID_DIL, MAX_DIL] and all(w // d == WIN for w, d in DILATIONS)
PERM = MAX_DIL * MAX_DIL
KV_BLOCK = 256
N_PAIRS = EXPERTS_PER_GROUP * (EXPERTS_PER_GROUP - 1) // 2
N_CLASSES = N_GROUPS * N_PAIRS
PAIR_ORDER = ((0, 1), (0, 2), (1, 2), (1, 3), (0, 3), (2, 3))
assert len(PAIR_ORDER) == N_PAIRS
MOE_TILE = 256
EXT_WIDTH = D_MODEL + LANES
LOGIT_MAX = 60.0
BRANCH_UNROLL = 16
DMA_THREADS = 2
ROUTER_OFF = N_GROUPS
VMEM_LIMIT = 56 * 1024 * 1024
LOG2E = 1.4426950408889634
BOUND_SLACK = 1.0 + 2.0 ** -6
BOUND_EPS = 2.0 ** -7


N_PARTS = 3


def _split3(x):
    hi = x.astype(BF16)
    r1 = x - hi.astype(F32)
    mid = r1.astype(BF16)
    lo = (r1 - mid.astype(F32)).astype(BF16)
    return hi, mid, lo


def _dot_nt(a, b):
    return lax.dot_general(a, b, (((1,), (1,)), ((), ())), preferred_element_type=F32)


def _inproj_kernel(x_ref, g_ref, w_ref, cos_ref, sa_ref, sb_ref, bf_ref,
                   qkva_ref, qkvb_ref, c_ref, h_scr, carry_scr, *, tiles_per_seq):
    i = pl.program_id(0)
    tm = x_ref.shape[0]
    x = x_ref[...]
    ms = jnp.mean(x * x, axis=-1, keepdims=True)
    h_scr[...] = (x * lax.rsqrt(ms + NORM_EPS) * g_ref[...]).astype(BF16)
    cosv = cos_ref[...]
    sa = sa_ref[...]
    sb = sb_ref[...]
    for g in range(6):
        p = _dot_nt(h_scr[...], w_ref[g * WIDTH_A:(g + 1) * WIDTH_A, :])
        dst = qkva_ref if g < 3 else qkvb_ref
        col0 = (g % 3) * WIDTH_A
        if g < 2:
            for k in range(HEAD_PAIRS):
                t = p[:, k * LANES:(k + 1) * LANES]
                t = (t * cosv + pltpu.roll(t, LANES - ROT_HALF, 1) * sa + pltpu.roll(t, ROT_HALF, 1) * sb)
                dst[:, col0 + k * LANES:col0 + (k + 1) * LANES] = t.astype(BF16)
        else:
            dst[:, col0:col0 + WIDTH_A] = p.astype(BF16)
    z = _dot_nt(h_scr[...], w_ref[2 * QKV_WIDTH:2 * QKV_WIDTH + LANES, :]) + bf_ref[...]
    lf = jnp.minimum(z, 0.0) - jnp.log1p(jnp.exp(-jnp.abs(z)))
    lane = lax.broadcasted_iota(jnp.int32, (tm, LANES), 1)
    lf = jnp.where(lane < HEADS_B, lf * LOG2E, 0.0)
    hi, mid, lo = _split3(lf)
    tri = (lax.broadcasted_iota(jnp.int32, (tm, tm), 0) >= lax.broadcasted_iota(jnp.int32, (tm, tm), 1)).astype(BF16)
    two = jnp.dot(tri, jnp.concatenate([hi, mid], axis=1), preferred_element_type=F32)
    cs = two[:, 0:LANES] + two[:, LANES:2 * LANES] + jnp.dot(tri, lo, preferred_element_type=F32)

    @pl.when(i % tiles_per_seq == 0)
    def _():
        carry_scr[...] = jnp.zeros_like(carry_scr)

    c = cs + carry_scr[0:1, :]
    c_ref[...] = c
    carry_scr[...] = jnp.broadcast_to(c[tm - 1:tm, :], carry_scr.shape)


def _inproj(x2, g, w, cosv, sa, sb, bf, *, seq, tm):
    n = x2.shape[0]
    kern = functools.partial(_inproj_kernel, tiles_per_seq=seq // tm)
    tps = seq // tm
    return pl.pallas_call(
        kern,
        grid=(n // tm,),
        in_specs=[
            pl.BlockSpec((tm, D_MODEL), lambda i: (i, 0)),
            pl.BlockSpec((1, D_MODEL), lambda i: (0, 0)),
            pl.BlockSpec(w.shape, lambda i: (0, 0)),
            pl.BlockSpec((tm, LANES), lambda i: (i % tps, 0)),
            pl.BlockSpec((tm, LANES), lambda i: (i % tps, 0)),
            pl.BlockSpec((tm, LANES), lambda i: (i % tps, 0)),
            pl.BlockSpec((1, LANES), lambda i: (0, 0)),
        ],
        out_specs=[
            pl.BlockSpec((tm, QKV_WIDTH), lambda i: (i, 0)),
            pl.BlockSpec((tm, QKV_WIDTH), lambda i: (i, 0)),
            pl.BlockSpec((tm, LANES), lambda i: (i, 0)),
        ],
        out_shape=[
            jax.ShapeDtypeStruct((n, QKV_WIDTH), BF16),
            jax.ShapeDtypeStruct((n, QKV_WIDTH), BF16),
            jax.ShapeDtypeStruct((n, LANES), F32),
        ],
        scratch_shapes=[pltpu.VMEM((tm, D_MODEL), BF16), pltpu.VMEM((8, LANES), F32)],
        compiler_params=pltpu.CompilerParams(dimension_semantics=("arbitrary",), vmem_limit_bytes=VMEM_LIMIT),
        name="inproj",
    )(x2, g, w, cosv, sa, sb, bf)


def _own_lanes(lane, hl):
    return (lane < HEAD_DIM) if hl == 0 else (lane >= HEAD_DIM)


def _head_sum_selector():
    lane = lax.broadcasted_iota(jnp.int32, (LANES, LANES), 0)
    col = lax.broadcasted_iota(jnp.int32, (LANES, LANES), 1)
    return jnp.where(((col == 0) & (lane < HEAD_DIM)) | ((col == 1) & (lane >= HEAD_DIM)), 1.0, 0.0).astype(BF16)


def _max_sq_norm(t, selector):
    return jnp.max(jnp.dot((t * t).astype(BF16), selector, preferred_element_type=F32), axis=0, keepdims=True)


def _norm_bound(qsq, ksq_max):
    return jnp.sqrt(qsq * ksq_max) * BOUND_SLACK + BOUND_EPS


def _fox_kernel(q_ref, k_ref, v_ref, c_ref, wg_ref, wu_ref, wd_ref, o_ref, wg_out, wu_out, wd_out,
                q0_scr, q1_scr, k0_scr, k1_scr, v0_scr, v1_scr, acc0_scr, acc1_scr, s_scr,
                *, tq, seq, chunk):
    hp = pl.program_id(1)
    for src, dst in ((wg_ref, wg_out), (wu_ref, wu_out), (wd_ref, wd_out)):
        dst[...] = src[...].astype(dst.dtype)
    q_scr = (q0_scr, q1_scr)
    k_scr = (k0_scr, k1_scr)
    v_scr = (v0_scr, v1_scr)
    acc_scr = (acc0_scr, acc1_scr)

    def placement(part):
        row = lax.broadcasted_iota(jnp.int32, (LANES, 2 * LANES), 0)
        col = lax.broadcasted_iota(jnp.int32, (LANES, 2 * LANES), 1)
        m = jnp.zeros((LANES, 2 * LANES), F32)
        for hl in range(2):
            first = HEAD_DIM if hl == 0 else 0
            src = row == 2 * hp + hl
            m = jnp.where(src & (col == first + part), 1.0, m)
            m = jnp.where(src & (col == LANES + first + N_PARTS + part), -1.0, m)
            if part == 0:
                ones_q = (col >= first + N_PARTS) & (col < first + 2 * N_PARTS)
                ones_k = (col >= LANES + first) & (col < LANES + first + N_PARTS)
                m = jnp.where((row == LANES - 1) & (ones_q | ones_k), 1.0, m)
        return m.astype(BF16)

    def build():
        lane = lax.broadcasted_iota(jnp.int32, (chunk, LANES), 1)
        place_hi_mid = jnp.concatenate([placement(0), placement(1)], axis=0)
        place_lo = placement(2)

        def body(ci, sq_max):
            rows = pl.ds(pl.multiple_of(ci * chunk, chunk), chunk)
            qc = q_ref[0, rows, :].astype(F32)
            kc = k_ref[0, rows, :].astype(F32)
            vc = v_ref[0, rows, :].astype(F32)
            hi, mid, lo = _split3(jnp.where(lane == LANES - 1, 1.0, c_ref[0, rows, :]))
            aug = (jnp.dot(jnp.concatenate([hi, mid], axis=1), place_hi_mid, preferred_element_type=F32)
                   + jnp.dot(lo, place_lo, preferred_element_type=F32))
            for hl in range(2):
                own = _own_lanes(lane, hl)
                q_scr[hl][rows, :] = jnp.where(own, qc, aug[:, 0:LANES]).astype(BF16)
                k_scr[hl][rows, :] = jnp.where(own, kc, aug[:, LANES:2 * LANES]).astype(BF16)
                one_lane = HEAD_DIM if hl == 0 else 0
                v_scr[hl][rows, :] = jnp.where(own, vc, jnp.where(lane == one_lane, 1.0, 0.0)).astype(BF16)
            return (jnp.maximum(sq_max[0], _max_sq_norm(qc, selector)),
                    jnp.maximum(sq_max[1], _max_sq_norm(kc, selector)))

        selector = _head_sum_selector()
        sq_max = lax.fori_loop(0, seq // chunk, body, (jnp.zeros((1, LANES), F32),) * 2)
        return jnp.max(_norm_bound(sq_max[0], sq_max[1]))

    worst = build()
    lane_q = lax.broadcasted_iota(jnp.int32, (tq, LANES), 1)
    n_diag = tq // KV_BLOCK
    row_minus_col = (lax.broadcasted_iota(jnp.int32, (tq, KV_BLOCK), 0)
                     - lax.broadcasted_iota(jnp.int32, (tq, KV_BLOCK), 1))

    def block_rows(j):
        start = j * KV_BLOCK
        return pl.ds(start if isinstance(start, int) else pl.multiple_of(start, KV_BLOCK), KV_BLOCK)

    def causal_mask(s, d, top=0):
        return jnp.where(row_minus_col[top:tq, :] >= d * KV_BLOCK, s, NEG_INF)

    def logits(qi, hl, j, top=0):
        return lax.dot_general(q_scr[hl][pl.ds(qi * tq + top, tq - top), :], k_scr[hl][block_rows(j), :],
                               (((1,), (1,)), ((), ())), preferred_element_type=F32)

    def values(hl, j):
        return v_scr[hl][block_rows(j), :]

    def finish(qi, a0, a1):
        l0 = a0[:, HEAD_DIM:HEAD_DIM + 1]
        l1 = a1[:, 0:1]
        o_ref[0, pl.ds(qi * tq, tq), :] = jnp.where(lane_q < HEAD_DIM, a0 / l0, a1 / l1).astype(o_ref.dtype)

    def fast_tile(qi):
        n_full = qi * n_diag
        acc0_scr[...] = jnp.zeros_like(acc0_scr)
        acc1_scr[...] = jnp.zeros_like(acc1_scr)

        def produce(j, slot, top=0):
            for hl in range(2):
                s_scr[slot, hl, top:tq, :] = logits(qi, hl, j, top)

        def consume(j, slot, diag):
            top = 0 if diag is None else diag * KV_BLOCK
            for hl in range(2):
                s = s_scr[slot, hl, top:tq, :]
                if diag is not None:
                    s = causal_mask(s, diag, top)
                acc_scr[hl][top:tq, :] += jnp.dot(jnp.exp2(s).astype(BF16), values(hl, j),
                                                  preferred_element_type=F32)

        for j in range(min(n_diag, n_full + n_diag)):
            produce(j, j % n_diag, 0 if j < n_full else (j - n_full) * KV_BLOCK)
        for j in range(n_full + n_diag):
            consume(j, j % n_diag, None if j < n_full else j - n_full)
            nxt = j + n_diag
            if nxt < n_full + n_diag:
                produce(nxt, nxt % n_diag, 0 if nxt < n_full else (nxt - n_full) * KV_BLOCK)
        finish(qi, acc0_scr[...], acc1_scr[...])

    def general_tile(qi):
        n_full = qi * n_diag

        def step(j, carry, diag):
            new = []
            for hl in range(2):
                m, a = carry[2 * hl:2 * hl + 2]
                s = logits(qi, hl, j)
                if diag is not None:
                    s = causal_mask(s, diag)
                mn = jnp.maximum(m, jnp.max(s, axis=1, keepdims=True))
                a = jnp.exp2(m - mn) * a + jnp.dot(jnp.exp2(s - mn).astype(BF16), values(hl, j),
                                                  preferred_element_type=F32)
                new += [mn, a]
            return tuple(new)

        carry = (jnp.full((tq, 1), NEG_INF, F32), jnp.zeros((tq, LANES), F32)) * 2
        carry = lax.fori_loop(0, n_full, lambda j, c: step(j, c, None), carry)
        for d in range(n_diag):
            carry = step(n_full + d, carry, d)
        finish(qi, carry[1], carry[3])

    for tile_fn, cond in ((fast_tile, worst <= LOGIT_MAX), (general_tile, worst > LOGIT_MAX)):
        @pl.when(cond)
        def _():
            for qi in range(seq // tq):
                tile_fn(qi)


def _fox(qkvb, c, wg, wu, wd, *, tq, chunk=512):
    b, seq, _ = qkvb.shape
    steps = b * HEAD_PAIRS
    parts = max(steps // N_EXPERTS, 1)
    per = max(N_EXPERTS // steps, 1)
    assert parts * N_EXPERTS == steps * per and D_MODEL % parts == 0 and D_EXPERT % parts == 0

    def slice_spec(rows, cols):
        def index(bi, hp):
            step = bi * HEAD_PAIRS + hp
            return (step // parts, step % parts, 0)
        return pl.BlockSpec((per, rows // parts, cols), index)

    w_specs = [slice_spec(D_MODEL, D_EXPERT), slice_spec(D_MODEL, D_EXPERT), slice_spec(D_EXPERT, D_MODEL)]
    kern = functools.partial(_fox_kernel, tq=tq, seq=seq, chunk=chunk)
    return pl.pallas_call(
        kern,
        grid=(b, HEAD_PAIRS),
        in_specs=[
            pl.BlockSpec((1, seq, LANES), lambda bi, hp: (bi, 0, hp)),
            pl.BlockSpec((1, seq, LANES), lambda bi, hp: (bi, 0, HEAD_PAIRS + hp)),
            pl.BlockSpec((1, seq, LANES), lambda bi, hp: (bi, 0, 2 * HEAD_PAIRS + hp)),
            pl.BlockSpec((1, seq, LANES), lambda bi, hp: (bi, 0, 0)),
        ] + w_specs,
        out_specs=[pl.BlockSpec((1, seq, LANES), lambda bi, hp: (bi, 0, hp))] + w_specs,
        out_shape=[jax.ShapeDtypeStruct((b, seq, WIDTH_B), BF16)]
        + [jax.ShapeDtypeStruct(w.shape, BF16) for w in (wg, wu, wd)],
        scratch_shapes=[pltpu.VMEM((seq, LANES), BF16)] * 6 + [
                        pltpu.VMEM((tq, LANES), F32), pltpu.VMEM((tq, LANES), F32),
                        pltpu.VMEM((tq // KV_BLOCK, 2, tq, KV_BLOCK), F32)],
        compiler_params=pltpu.CompilerParams(
            dimension_semantics=("arbitrary", "arbitrary"), vmem_limit_bytes=VMEM_LIMIT),
        name="fox",
    )(qkvb, qkvb, qkvb, c, wg, wu, wd)


def _log2(n):
    assert n & (n - 1) == 0
    return n.bit_length() - 1


def _permutation():
    ri = lax.broadcasted_iota(jnp.int32, (PERM, PERM), 0)
    ci = lax.broadcasted_iota(jnp.int32, (PERM, PERM), 1)
    sh, lo = _log2(MAX_DIL), MAX_DIL - 1
    return (((ri >> sh) == (ci & lo)) & ((ri & lo) == (ci >> sh))).astype(BF16)


def _band_masks():
    qa = lax.broadcasted_iota(jnp.int32, (WIN, 2 * WIN), 0)
    kb = lax.broadcasted_iota(jnp.int32, (WIN, 2 * WIN), 1)
    cur = kb >= WIN
    kin = kb & (WIN - 1)
    prev_shift = jnp.where(cur, 0, WIN)
    dist = qa - kin + prev_shift
    chunk = WIN // N_MID
    offset = lambda a: ((a & (chunk - 1)) << _log2(N_MID)) + (a >> _log2(chunk))
    dist_mid = offset(qa) - offset(kin) + prev_shift
    return (dist >= 0) & (dist <= WIN), (dist_mid >= 0) & (dist_mid <= WIN), cur


def _dilated_general(q_ref, k_ref, v_ref, o_ref, q16, k16, v16, m_s, l_s, acc_s, on_s, lse_s, *, seq):
    sub = seq // MAX_DIL
    own0 = lax.broadcasted_iota(jnp.int32, (WIN, LANES), 1) < HEAD_DIM
    perm = _permutation()
    band_nat, band_4, cur = _band_masks()

    def attend(qb, kb, vb, mask, state):
        qf = qb.astype(F32)
        new = []
        for hl in range(2):
            m, l, a = state[3 * hl:3 * hl + 3]
            own = own0 if hl == 0 else jnp.logical_not(own0)
            qh = jnp.where(own, qf, 0.0).astype(BF16)
            s = lax.dot_general(qh, kb, (((1,), (1,)), ((), ())), preferred_element_type=F32)
            s = jnp.where(mask, s, NEG_INF)
            mn = jnp.maximum(m, jnp.max(s, axis=1, keepdims=True))
            alpha = jnp.exp2(m - mn)
            p = jnp.exp2(s - mn)
            l = alpha * l + jnp.sum(p, axis=1, keepdims=True)
            a = alpha * a + jnp.dot(p.astype(BF16), vb, preferred_element_type=F32)
            new += [mn, l, a]
        return new

    def pack_state(st):
        m0, l0, a0, m1, l1, a1 = st
        return (jnp.where(own0, m0, m1), jnp.where(own0, l0, l1), jnp.where(own0, a0, a1))

    def deint(blk, carry):
        r0 = pl.multiple_of(blk * PERM, PERM)
        j0 = pl.multiple_of(blk * MAX_DIL, MAX_DIL)
        for src, dst in ((q_ref, q16), (k_ref, k16), (v_ref, v16)):
            y = jnp.dot(perm, src[0, pl.ds(r0, PERM), :], preferred_element_type=F32).astype(BF16)
            for r in range(MAX_DIL):
                dst[r, pl.ds(j0, MAX_DIL), :] = y[r * MAX_DIL:(r + 1) * MAX_DIL, :]
        return carry

    lax.fori_loop(0, seq // PERM, deint, 0)

    fresh = [jnp.full((WIN, 1), NEG_INF, F32), jnp.zeros((WIN, 1), F32), jnp.zeros((WIN, LANES), F32)] * 2
    nb16 = sub // WIN

    def d16_body(t, carry):
        r = t // nb16
        n = t % nb16
        c0 = pl.multiple_of(n * WIN, WIN)
        p0 = pl.multiple_of(jnp.maximum(n - 1, 0) * WIN, WIN)
        qb = q16[r, pl.ds(c0, WIN), :]
        kb = jnp.concatenate([k16[r, pl.ds(p0, WIN), :], k16[r, pl.ds(c0, WIN), :]], axis=0)
        vb = jnp.concatenate([v16[r, pl.ds(p0, WIN), :], v16[r, pl.ds(c0, WIN), :]], axis=0)
        mask = band_nat & (cur | (n > 0))
        mm, ll, aa = pack_state(attend(qb, kb, vb, mask, fresh))
        row = pl.multiple_of(r * sub + c0, WIN)
        m_s[pl.ds(row, WIN), :] = mm
        l_s[pl.ds(row, WIN), :] = ll
        acc_s[pl.ds(row, WIN), :] = aa
        return carry

    lax.fori_loop(0, MAX_DIL * nb16, d16_body, 0)

    nb4 = (seq // MID_DIL) // WIN
    ch = WIN // N_MID

    def d4_body(t, carry):
        r4 = t // nb4
        n = t % nb4
        c0 = pl.multiple_of(n * ch, ch)
        p0 = pl.multiple_of(jnp.maximum(n - 1, 0) * ch, ch)

        def gather(ref, j0):
            return [ref[r4 + MID_DIL * q, pl.ds(j0, ch), :] for q in range(N_MID)]

        qb = jnp.concatenate(gather(q16, c0), axis=0)
        kb = jnp.concatenate(gather(k16, p0) + gather(k16, c0), axis=0)
        vb = jnp.concatenate(gather(v16, p0) + gather(v16, c0), axis=0)
        rows = [pl.multiple_of((r4 + MID_DIL * q) * sub + c0, ch) for q in range(N_MID)]
        mm = jnp.concatenate([m_s[pl.ds(rw, ch), :] for rw in rows], axis=0)
        ll = jnp.concatenate([l_s[pl.ds(rw, ch), :] for rw in rows], axis=0)
        aa = jnp.concatenate([acc_s[pl.ds(rw, ch), :] for rw in rows], axis=0)
        state = [mm[:, 0:1], ll[:, 0:1], aa, mm[:, HEAD_DIM:HEAD_DIM + 1], ll[:, HEAD_DIM:HEAD_DIM + 1], aa]
        mask = band_4 & (cur | (n > 0))
        mm, ll, aa = pack_state(attend(qb, kb, vb, mask, state))
        for q, rw in enumerate(rows):
            m_s[pl.ds(rw, ch), :] = mm[q * ch:(q + 1) * ch, :]
            l_s[pl.ds(rw, ch), :] = ll[q * ch:(q + 1) * ch, :]
            acc_s[pl.ds(rw, ch), :] = aa[q * ch:(q + 1) * ch, :]
        return carry

    lax.fori_loop(0, MID_DIL * nb4, d4_body, 0)

    def renat(blk, carry):
        j0 = pl.multiple_of(blk * MAX_DIL, MAX_DIL)
        r0 = pl.multiple_of(blk * PERM, PERM)
        rows = [pl.multiple_of(r * sub + j0, MAX_DIL) for r in range(MAX_DIL)]
        mm = jnp.concatenate([m_s[pl.ds(rw, MAX_DIL), :] for rw in rows], axis=0)
        ll = jnp.concatenate([l_s[pl.ds(rw, MAX_DIL), :] for rw in rows], axis=0)
        aa = jnp.concatenate([acc_s[pl.ds(rw, MAX_DIL), :] for rw in rows], axis=0)
        o = (aa / ll).astype(BF16)
        hi, mid, lo = _split3(mm + jnp.log2(ll))
        on_s[pl.ds(r0, PERM), :] = jnp.dot(perm, o, preferred_element_type=F32)
        lse_s[pl.ds(r0, PERM), :] = (jnp.dot(perm, hi, preferred_element_type=F32)
                                     + jnp.dot(perm, mid, preferred_element_type=F32)
                                     + jnp.dot(perm, lo, preferred_element_type=F32))
        return carry

    lax.fori_loop(0, seq // PERM, renat, 0)

    def d1_body(n, carry):
        c0 = pl.multiple_of(n * WIN, WIN)
        p0 = pl.multiple_of(jnp.maximum(n - 1, 0) * WIN, WIN)
        qb = q_ref[0, pl.ds(c0, WIN), :]
        kb = jnp.concatenate([k_ref[0, pl.ds(p0, WIN), :], k_ref[0, pl.ds(c0, WIN), :]], axis=0)
        vb = jnp.concatenate([v_ref[0, pl.ds(p0, WIN), :], v_ref[0, pl.ds(c0, WIN), :]], axis=0)
        lse = lse_s[pl.ds(c0, WIN), :]
        o = on_s[pl.ds(c0, WIN), :]
        one = jnp.ones((WIN, 1), F32)
        state = [lse[:, 0:1], one, o, lse[:, HEAD_DIM:HEAD_DIM + 1], one, o]
        mask = band_nat & (cur | (n > 0))
        m0, l0, a0, m1, l1, a1 = attend(qb, kb, vb, mask, state)
        o_ref[0, pl.ds(c0, WIN), :] = jnp.where(own0, a0 / l0, a1 / l1).astype(o_ref.dtype)
        return carry

    lax.fori_loop(0, seq // WIN, d1_body, 0)


def _dilated_fast(o_ref, qn, kn, vn, qr, kr, vr, fr, fn, s_scr, mask_scr, *, seq):
    sub = seq // MAX_DIL
    own0 = lax.broadcasted_iota(jnp.int32, (WIN, LANES), 1) < HEAD_DIM
    head_mask = [jnp.where(own0, 1.0, 0.0).astype(BF16), jnp.where(own0, 0.0, 1.0).astype(BF16)]
    perm = _permutation()

    band_nat, band_4, cur = _band_masks()
    for i, band in enumerate((band_nat, band_4)):
        mask_scr[2 * i] = jnp.where(band & cur, 1.0, 0.0).astype(BF16)
        mask_scr[2 * i + 1] = jnp.where(band, 1.0, 0.0).astype(BF16)

    def run_branch(nblocks, block_in_seq, fetch, mask_base, sink, before_trip=None):
        def produce(t, pair, u):
            for hl in range(2):
                qb, kb = fetch(t, hl, True)
                s_scr[pair, u, hl] = lax.dot_general(qb * head_mask[hl], kb, (((1,), (1,)), ((), ())),
                                                     preferred_element_type=F32)

        def consume(t, pair, u):
            mk = mask_scr[mask_base + jnp.minimum(block_in_seq(t), 1)]
            pv = []
            for hl in range(2):
                p = jnp.exp2(s_scr[pair, u, hl]).astype(BF16) * mk
                pv.append(jnp.dot(p, fetch(t, hl, False), preferred_element_type=F32))
            sink(t, pv)

        for u in range(BRANCH_UNROLL):
            produce(u, 0, u)

        def trip(i, last):
            pair = i & 1
            if before_trip is not None:
                before_trip(i)
            for u in range(BRANCH_UNROLL):
                consume(BRANCH_UNROLL * i + u, pair, u)
            if not last:
                for u in range(BRANCH_UNROLL):
                    produce(BRANCH_UNROLL * (i + 1) + u, 1 - pair, u)

        def body(i, carry):
            trip(i, False)
            return carry

        trips = nblocks // BRANCH_UNROLL
        lax.fori_loop(0, trips - 1, body, 0)
        trip(jnp.int32(trips - 1), True)

    def prev_cur(ref_block, n, size):
        c0 = pl.multiple_of(n * size, size)
        p0 = pl.multiple_of(jnp.maximum(n - 1, 0) * size, size)
        return ref_block(p0) + ref_block(c0)

    nb16 = sub // WIN

    def fetch16(t, hl, qk):
        r, n = t // nb16, t % nb16
        if qk:
            return (qr[hl][r, pl.ds(pl.multiple_of(n * WIN, WIN), WIN), :],
                    jnp.concatenate(prev_cur(lambda s: [kr[hl][r, pl.ds(s, WIN), :]], n, WIN), axis=0))
        return jnp.concatenate(prev_cur(lambda s: [vr[hl][r, pl.ds(s, WIN), :]], n, WIN), axis=0)

    def sink16(t, pv):
        row = pl.multiple_of(t * WIN, WIN)
        for hl in range(2):
            fr[hl][pl.ds(row, WIN), :] = pv[hl]

    run_branch(MAX_DIL * nb16, lambda t: t % nb16, fetch16, 0, sink16)

    nb4 = (seq // MID_DIL) // WIN
    ch = WIN // N_MID

    def fetch4(t, hl, qk):
        r4, n = t // nb4, t % nb4
        gather = lambda ref: (lambda s: [ref[r4 + MID_DIL * q, pl.ds(s, ch), :] for q in range(N_MID)])
        if qk:
            return (jnp.concatenate(gather(qr[hl])(pl.multiple_of(n * ch, ch)), axis=0),
                    jnp.concatenate(prev_cur(gather(kr[hl]), n, ch), axis=0))
        return jnp.concatenate(prev_cur(gather(vr[hl]), n, ch), axis=0)

    def sink4(t, pv):
        r4, n = t // nb4, t % nb4
        for q in range(N_MID):
            row = pl.multiple_of((r4 + MID_DIL * q) * sub + n * ch, ch)
            for hl in range(2):
                fr[hl][pl.ds(row, ch), :] += pv[hl][q * ch:(q + 1) * ch, :]

    run_branch(MID_DIL * nb4, lambda t: t % nb4, fetch4, 2, sink4)

    def renat(i):
        for u in range(BRANCH_UNROLL * WIN // PERM):
            blk = i * (BRANCH_UNROLL * WIN // PERM) + u
            j0 = pl.multiple_of(blk * MAX_DIL, MAX_DIL)
            r0 = pl.multiple_of(blk * PERM, PERM)
            for hl in range(2):
                a = jnp.concatenate([fr[hl][pl.ds(pl.multiple_of(r * sub + j0, MAX_DIL), MAX_DIL), :]
                                     for r in range(MAX_DIL)], axis=0)
                hi = a.astype(BF16)
                lo = (a - hi.astype(F32)).astype(BF16)
                y = jnp.dot(perm, jnp.concatenate([hi, lo], axis=1), preferred_element_type=F32)
                fn[hl][pl.ds(r0, PERM), :] = y[:, 0:LANES] + y[:, LANES:2 * LANES]

    def fetch1(t, hl, qk):
        if qk:
            return (qn[hl][pl.ds(pl.multiple_of(t * WIN, WIN), WIN), :],
                    jnp.concatenate(prev_cur(lambda s: [kn[hl][pl.ds(s, WIN), :]], t, WIN), axis=0))
        return jnp.concatenate(prev_cur(lambda s: [vn[hl][pl.ds(s, WIN), :]], t, WIN), axis=0)

    def sink1(t, pv):
        row = pl.multiple_of(t * WIN, WIN)
        t0 = fn[0][pl.ds(row, WIN), :] + pv[0]
        t1 = fn[1][pl.ds(row, WIN), :] + pv[1]
        o_ref[0, pl.ds(row, WIN), :] = jnp.where(own0, t0 / t0[:, HEAD_DIM:HEAD_DIM + 1],
                                                 t1 / t1[:, 0:1]).astype(o_ref.dtype)

    run_branch(seq // WIN, lambda t: t, fetch1, 0, sink1, before_trip=renat)


def _dilated_kernel(q_ref, k_ref, v_ref, o_ref, vn0, vn1, qr, kr, vr0, vr1,
                    f0, f1, f2, f3, f4, s_scr, mask_scr, *, seq, chunk):
    lane = lax.broadcasted_iota(jnp.int32, (chunk, LANES), 1)
    vn = (vn0, vn1)

    perm = _permutation()

    def stats(ci, sq_max):
        rows = pl.ds(pl.multiple_of(ci * chunk, chunk), chunk)
        qb = q_ref[0, rows, :]
        kb = k_ref[0, rows, :]
        vc = v_ref[0, rows, :].astype(F32)
        vb = []
        for hl in range(2):
            ones_lane = HEAD_DIM if hl == 0 else 0
            vb.append(jnp.where(_own_lanes(lane, hl), vc, jnp.where(lane == ones_lane, 1.0, 0.0)).astype(BF16))
            vn[hl][rows, :] = vb[hl]
        for u in range(chunk // PERM):
            j0 = pl.multiple_of((ci * (chunk // PERM) + u) * MAX_DIL, MAX_DIL)
            for srcs, dsts in (((qb, kb), (qr, kr)), (vb, (vr0, vr1))):
                both = jnp.concatenate([src[u * PERM:(u + 1) * PERM, :] for src in srcs], axis=1)
                y = jnp.dot(perm, both, preferred_element_type=F32).astype(BF16)
                for r in range(MAX_DIL):
                    for half, dst in enumerate(dsts):
                        dst[r, pl.ds(j0, MAX_DIL), :] = y[r * MAX_DIL:(r + 1) * MAX_DIL, half * LANES:(half + 1) * LANES]
        return (jnp.maximum(sq_max[0], _max_sq_norm(qb.astype(F32), selector)),
                jnp.maximum(sq_max[1], _max_sq_norm(kb.astype(F32), selector)))

    selector = _head_sum_selector()
    sq_max = lax.fori_loop(0, seq // chunk, stats, (jnp.zeros((1, LANES), F32),) * 2)
    worst = jnp.max(_norm_bound(sq_max[0], sq_max[1]))

    @pl.when(worst <= LOGIT_MAX)
    def _fast():
        qn, kn = q_ref.at[0], k_ref.at[0]
        _dilated_fast(o_ref, (qn, qn), (kn, kn), vn, (qr, qr), (kr, kr), (vr0, vr1), (f0, f1), (f2, f3),
                      s_scr, mask_scr, seq=seq)

    @pl.when(worst > LOGIT_MAX)
    def _general():
        _dilated_general(q_ref, k_ref, v_ref, o_ref, qr, kr, vr0, f0, f1, f2, f3, f4, seq=seq)


def _dilated(qkva, *, chunk=512):
    b, seq, _ = qkva.shape
    sub = seq // MAX_DIL
    kern = functools.partial(_dilated_kernel, seq=seq, chunk=chunk)
    return pl.pallas_call(
        kern,
        grid=(b, HEAD_PAIRS),
        in_specs=[
            pl.BlockSpec((1, seq, LANES), lambda bi, hp: (bi, 0, hp)),
            pl.BlockSpec((1, seq, LANES), lambda bi, hp: (bi, 0, HEAD_PAIRS + hp)),
            pl.BlockSpec((1, seq, LANES), lambda bi, hp: (bi, 0, 2 * HEAD_PAIRS + hp)),
        ],
        out_specs=pl.BlockSpec((1, seq, LANES), lambda bi, hp: (bi, 0, hp)),
        out_shape=jax.ShapeDtypeStruct((b, seq, WIDTH_A), BF16),
        scratch_shapes=(
            [pltpu.VMEM((seq, LANES), BF16)] * 2
            + [pltpu.VMEM((MAX_DIL, sub, LANES), BF16)] * 4
            + [pltpu.VMEM((seq, LANES), F32)] * 5
            + [pltpu.VMEM((2, BRANCH_UNROLL, 2, WIN, 2 * WIN), F32), pltpu.VMEM((4, WIN, 2 * WIN), BF16)]
        ),
        compiler_params=pltpu.CompilerParams(
            dimension_semantics=("arbitrary", "arbitrary"), vmem_limit_bytes=VMEM_LIMIT),
        name="dilated",
    )(qkva, qkva, qkva)


def _outproj_kernel(oa_ref, ob_ref, x_ref, wo_ref, g_ref, wr_ref, x1_ref, h2_ref, cls_ref):
    tm = x_ref.shape[0]
    x1 = (x_ref[...]
          + jnp.dot(oa_ref[...], wo_ref[0:WIDTH_A, :], preferred_element_type=F32)
          + jnp.dot(ob_ref[...], wo_ref[WIDTH_A:WIDTH_A + WIDTH_B, :], preferred_element_type=F32))
    x1_ref[...] = x1
    ms = jnp.mean(x1 * x1, axis=-1, keepdims=True)
    h2 = x1 * lax.rsqrt(ms + NORM_EPS) * g_ref[...]
    h2_ref[:, 0:D_MODEL] = h2
    h2_hi = h2.astype(BF16)
    h2_lo = (h2 - h2_hi.astype(F32)).astype(BF16)
    both = jnp.dot(h2_hi, wr_ref[...], preferred_element_type=F32)
    logits = (both[:, 0:LANES] + both[:, LANES:2 * LANES]
              + jnp.dot(h2_lo, wr_ref[:, 0:LANES], preferred_element_type=F32))
    lane = lax.broadcasted_iota(jnp.int32, (tm, LANES), 1)
    lane_f = lane.astype(F32)
    big = float(LANES)
    gmask = lane < N_GROUPS
    gl = jnp.where(gmask, logits, NEG_INF)
    gmax = jnp.max(gl, axis=1, keepdims=True)
    gsum = jnp.sum(jnp.where(gmask, jnp.exp(gl - gmax), 0.0), axis=1, keepdims=True)
    p_top = 1.0 / gsum
    g_star = jnp.min(jnp.where(gmask & (gl == gmax), lane_f, big), axis=1, keepdims=True)
    lo_lane = ROUTER_OFF + EXPERTS_PER_GROUP * g_star
    emask = (lane_f >= lo_lane) & (lane_f < lo_lane + EXPERTS_PER_GROUP)
    sel = jnp.where(emask, logits, NEG_INF)
    v1 = jnp.max(sel, axis=1, keepdims=True)
    i1 = jnp.min(jnp.where(emask & (sel == v1), lane_f, big), axis=1, keepdims=True)
    rest = emask & (lane_f != i1)
    sel2 = jnp.where(rest, logits, NEG_INF)
    v2 = jnp.max(sel2, axis=1, keepdims=True)
    i2 = jnp.min(jnp.where(rest & (sel2 == v2), lane_f, big), axis=1, keepdims=True)
    e2 = jnp.exp(v2 - v1)
    w1 = p_top / (1.0 + e2)
    w2 = p_top * e2 / (1.0 + e2)
    e1 = i1 - lo_lane
    e2x = i2 - lo_lane
    a = jnp.minimum(e1, e2x)
    b = jnp.maximum(e1, e2x)
    pair = jnp.zeros_like(a)
    for idx, (pa, pb) in enumerate(PAIR_ORDER):
        pair = jnp.where((a == pa) & (b == pb), float(idx), pair)
    cls = g_star * N_PAIRS + pair
    wa = jnp.where(e1 < e2x, w1, w2)
    wb = jnp.where(e1 < e2x, w2, w1)
    h2_ref[:, D_MODEL:EXT_WIDTH] = jnp.where(lane == 0, wa, jnp.where(lane == 1, wb, 0.0))
    row = lax.broadcasted_iota(jnp.int32, (tm, LANES), 0)
    spread = jnp.where(lane == (row & (LANES - 1)), cls, 0.0)
    cls_ref[0] = jnp.sum(spread.reshape(tm // LANES, LANES, LANES), axis=1).astype(jnp.int32)


def _outproj(oa, ob, x2, wo, g, wr, *, tm):
    n = x2.shape[0]
    return pl.pallas_call(
        _outproj_kernel,
        grid=(n // tm,),
        in_specs=[
            pl.BlockSpec((tm, WIDTH_A), lambda i: (i, 0)),
            pl.BlockSpec((tm, WIDTH_B), lambda i: (i, 0)),
            pl.BlockSpec((tm, D_MODEL), lambda i: (i, 0)),
            pl.BlockSpec(wo.shape, lambda i: (0, 0)),
            pl.BlockSpec((1, D_MODEL), lambda i: (0, 0)),
            pl.BlockSpec(wr.shape, lambda i: (0, 0)),
        ],
        out_specs=[
            pl.BlockSpec((tm, D_MODEL), lambda i: (i, 0)),
            pl.BlockSpec((tm, EXT_WIDTH), lambda i: (i, 0)),
            pl.BlockSpec((1, tm // LANES, LANES), lambda i: (i, 0, 0)),
        ],
        out_shape=[
            jax.ShapeDtypeStruct((n, D_MODEL), F32),
            jax.ShapeDtypeStruct((n, EXT_WIDTH), F32),
            jax.ShapeDtypeStruct((n // tm, tm // LANES, LANES), jnp.int32),
        ],
        compiler_params=pltpu.CompilerParams(dimension_semantics=("arbitrary",), vmem_limit_bytes=VMEM_LIMIT),
        name="outproj",
    )(oa, ob, x2, wo, g, wr)


def _sort_kernel(cls_ref, pos_ref, tile_ref, ends_ref, *, tile):
    cls = cls_ref[...]
    rows = cls.shape[0]
    upper = (lax.broadcasted_iota(jnp.int32, (LANES, LANES), 0)
             <= lax.broadcasted_iota(jnp.int32, (LANES, LANES), 1)).astype(BF16)
    earlier_rows = (lax.broadcasted_iota(jnp.int32, (rows, rows), 0)
                    > lax.broadcasted_iota(jnp.int32, (rows, rows), 1)).astype(BF16)
    ones = jnp.ones((LANES, LANES), BF16)
    lane8 = lax.broadcasted_iota(jnp.int32, tile_ref.shape, 1)
    tile_start = lane8.astype(F32) * tile
    off = jnp.zeros((1, 1), F32)
    pos = jnp.zeros(cls.shape, F32)
    tile_cls = jnp.zeros(tile_ref.shape, F32)
    ends = jnp.zeros(tile_ref.shape, F32)
    for c in range(N_CLASSES):
        hit = cls == c
        hot = jnp.where(hit, 1.0, 0.0).astype(BF16)
        in_row = jnp.dot(hot, upper, preferred_element_type=F32)
        row_tot = jnp.dot(hot, ones, preferred_element_type=F32)
        before = jnp.dot(earlier_rows, row_tot.astype(BF16), preferred_element_type=F32)
        count = before[rows - 1:rows, 0:1] + row_tot[rows - 1:rows, 0:1]
        pos = pos + jnp.where(hit, in_row - 1.0 + before + off, 0.0)
        off = off + jnp.ceil(count / tile) * tile
        tile_cls = tile_cls + jnp.where(tile_start >= off, 1.0, 0.0)
        ends = jnp.where(lane8 == c, off, ends)
    pos_ref[...] = pos.astype(jnp.int32)
    tile_ref[...] = tile_cls.astype(jnp.int32)
    ends_ref[...] = ends.astype(jnp.int32)


def _sort(cls2d, *, tile):
    meta = jax.ShapeDtypeStruct((8, LANES), jnp.int32)
    return pl.pallas_call(
        functools.partial(_sort_kernel, tile=tile),
        out_shape=[jax.ShapeDtypeStruct(cls2d.shape, jnp.int32), meta, meta],
        compiler_params=pltpu.CompilerParams(vmem_limit_bytes=VMEM_LIMIT),
        name="moe_sort",
    )(cls2d)


def _row_copy(src, src_row, dst, dst_row, sem, rows=1):
    return pltpu.make_async_copy(src.at[pl.ds(src_row, rows)], dst.at[pl.ds(dst_row, rows)], sem)


def _dispatch_kernel(pos_ref, ends_ref, h2_ref, xs_hbm, zero_scr, sem, *, ch, tile):
    g = pl.program_id(0)

    @pl.when(g == 0)
    def _zero_tails():
        zero_scr[...] = jnp.zeros_like(zero_scr)
        n_tiles = xs_hbm.shape[0] // tile
        min_used = n_tiles - N_CLASSES
        total = ends_ref[0, N_CLASSES - 1]
        for phase in ("start", "wait"):
            def zero_tile(row0):
                cp = pltpu.make_async_copy(zero_scr, xs_hbm.at[pl.ds(pl.multiple_of(row0, tile), tile)], sem)
                cp.start() if phase == "start" else cp.wait()

            for c in range(N_CLASSES):
                end = ends_ref[0, c]
                prev = ends_ref[0, c - 1] if c else 0
                pl.when(end > prev)(functools.partial(zero_tile, end - tile))
                pl.when((min_used + c) * tile >= total)(functools.partial(zero_tile, (min_used + c) * tile))

    for k in range(ch):
        _row_copy(h2_ref, k, xs_hbm, pos_ref[0, 0, k], sem).start(priority=k % DMA_THREADS)
    _row_copy(h2_ref, 0, xs_hbm, 0, sem, rows=ch).wait()


def _dispatch(pos3, ends, h2ext, *, n_rows, tile):
    n_chunks, _, ch = pos3.shape
    return pl.pallas_call(
        functools.partial(_dispatch_kernel, ch=ch, tile=tile),
        grid=(n_chunks,),
        in_specs=[
            pl.BlockSpec((1, 1, ch), lambda g: (g, 0, 0), memory_space=pltpu.SMEM),
            pl.BlockSpec(ends.shape, lambda g: (0, 0), memory_space=pltpu.SMEM),
            pl.BlockSpec((ch, EXT_WIDTH), lambda g: (g, 0)),
        ],
        out_specs=pl.BlockSpec(memory_space=pl.ANY),
        out_shape=jax.ShapeDtypeStruct((n_rows, EXT_WIDTH), F32),
        scratch_shapes=[pltpu.VMEM((tile, EXT_WIDTH), F32), pltpu.SemaphoreType.DMA(())],
        compiler_params=pltpu.CompilerParams(dimension_semantics=("arbitrary",), vmem_limit_bytes=VMEM_LIMIT),
        name="moe_dispatch",
    )(pos3, ends, h2ext)


def _experts_kernel(ea_ref, eb_ref, used_ref, xs_ref, wga, wua, wda, wgb, wub, wdb, ys_ref):
    del ea_ref, eb_ref

    @pl.when(pl.program_id(0) < used_ref[0])
    def _():
        x = xs_ref[:, 0:D_MODEL].astype(BF16)
        wts = xs_ref[:, D_MODEL:EXT_WIDTH]
        y = None
        for col, (wg, wu, wd) in enumerate(((wga, wua, wda), (wgb, wub, wdb))):
            gate = jnp.dot(x, wg[0], preferred_element_type=F32)
            up = jnp.dot(x, wu[0], preferred_element_type=F32)
            he = (gate / (1.0 + jnp.exp(-gate)) * up * wts[:, col:col + 1]).astype(BF16)
            t = jnp.dot(he, wd[0], preferred_element_type=F32)
            y = t if y is None else y + t
        ys_ref[...] = y

    @pl.when(pl.program_id(0) >= used_ref[0])
    def _():
        ys_ref[...] = jnp.zeros_like(ys_ref)


def _experts(ea, eb, used, xs, wg, wu, wd, *, tile):
    n_tiles = xs.shape[0] // tile
    rows = lambda j, ea, eb, used: (jnp.minimum(j, used[0] - 1), 0)
    first = lambda j, ea, eb, used: (ea[j], 0, 0)
    second = lambda j, ea, eb, used: (eb[j], 0, 0)
    up_spec = lambda m: pl.BlockSpec((1, D_MODEL, D_EXPERT), m)
    down_spec = lambda m: pl.BlockSpec((1, D_EXPERT, D_MODEL), m)
    return pl.pallas_call(
        _experts_kernel,
        grid_spec=pltpu.PrefetchScalarGridSpec(
            num_scalar_prefetch=3,
            grid=(n_tiles,),
            in_specs=[pl.BlockSpec((tile, EXT_WIDTH), rows),
                      up_spec(first), up_spec(first), down_spec(first),
                      up_spec(second), up_spec(second), down_spec(second)],
            out_specs=pl.BlockSpec((tile, D_MODEL), lambda j, ea, eb, used: (j, 0)),
        ),
        out_shape=jax.ShapeDtypeStruct((xs.shape[0], D_MODEL), F32),
        compiler_params=pltpu.CompilerParams(dimension_semantics=("arbitrary",), vmem_limit_bytes=VMEM_LIMIT),
        name="moe_experts",
    )(ea, eb, used, xs, wg, wu, wd, wg, wu, wd)


def _combine_kernel(pos_ref, next_pos_ref, x1_ref, g_ref, ys_hbm, out_ref, y_scr, sem):
    i = pl.program_id(0)
    tm = x1_ref.shape[0]
    slot = i & 1

    def gather(p_ref, s):
        for k in range(tm):
            _row_copy(ys_hbm, p_ref[0, 0, k], y_scr.at[s], k, sem.at[s]).start(priority=k % DMA_THREADS)

    @pl.when(i == 0)
    def _():
        gather(pos_ref, 0)

    @pl.when(i + 1 < pl.num_programs(0))
    def _():
        gather(next_pos_ref, 1 - slot)

    _row_copy(ys_hbm, 0, y_scr.at[slot], 0, sem.at[slot], rows=tm).wait()
    x2 = x1_ref[...] + y_scr[slot]
    ms = jnp.mean(x2 * x2, axis=-1, keepdims=True)
    out_ref[...] = x2 * lax.rsqrt(ms + NORM_EPS) * g_ref[...]


def _combine(pos3, x1, g, ys):
    n_chunks, _, tm = pos3.shape
    return pl.pallas_call(
        _combine_kernel,
        grid=(n_chunks,),
        in_specs=[
            pl.BlockSpec((1, 1, tm), lambda i: (i, 0, 0), memory_space=pltpu.SMEM),
            pl.BlockSpec((1, 1, tm), lambda i: (jnp.minimum(i + 1, n_chunks - 1), 0, 0), memory_space=pltpu.SMEM),
            pl.BlockSpec((tm, D_MODEL), lambda i: (i, 0)),
            pl.BlockSpec((1, D_MODEL), lambda i: (0, 0)),
            pl.BlockSpec(memory_space=pl.ANY),
        ],
        out_specs=pl.BlockSpec((tm, D_MODEL), lambda i: (i, 0)),
        out_shape=jax.ShapeDtypeStruct(x1.shape, F32),
        scratch_shapes=[pltpu.VMEM((2, tm, D_MODEL), F32), pltpu.SemaphoreType.DMA((2,))],
        compiler_params=pltpu.CompilerParams(dimension_semantics=("arbitrary",), vmem_limit_bytes=VMEM_LIMIT),
        name="moe_combine",
    )(pos3, pos3, x1, g, ys)


def _moe(h2ext, cls3, x1, wg, wu, wd, g, *, tile=MOE_TILE, dispatch_chunk=2048, combine_chunk=512):
    n = x1.shape[0]
    n_tiles = n // tile + N_CLASSES
    assert n_tiles <= LANES
    pos, tile_cls, ends = _sort(cls3.reshape(n // LANES, LANES), tile=tile)
    xs = _dispatch(pos.reshape(n // dispatch_chunk, 1, dispatch_chunk), ends, h2ext, n_rows=n_tiles * tile, tile=tile)
    used = ends[0, N_CLASSES - 1:N_CLASSES] // tile
    tc = tile_cls[0, :n_tiles]
    tc = jnp.where(jnp.arange(n_tiles) < used[0], tc, tc[jnp.maximum(used[0] - 1, 0)])
    pair_a = jnp.array([p[0] for p in PAIR_ORDER], jnp.int32)
    pair_b = jnp.array([p[1] for p in PAIR_ORDER], jnp.int32)
    ea = (tc // N_PAIRS) * EXPERTS_PER_GROUP + pair_a[tc % N_PAIRS]
    eb = (tc // N_PAIRS) * EXPERTS_PER_GROUP + pair_b[tc % N_PAIRS]
    ys = _experts(ea, eb, used, xs, wg, wu, wd, tile=tile)
    return _combine(pos.reshape(n // combine_chunk, 1, combine_chunk), x1, g, ys)


def _rotary_tables(seq):
    pos = np.arange(seq, dtype=np.float64)
    inv_freq = 1.0 / (ROPE_THETA ** (np.arange(0, ROT_DIM, 2, dtype=np.float64) / ROT_DIM))
    ang = pos[:, None] * inv_freq[None, :]
    cos, sin = np.cos(ang), np.sin(ang)
    zeros = np.zeros((seq, HEAD_DIM - ROT_DIM))
    zh = np.zeros((seq, ROT_HALF))
    cos_h = np.concatenate([cos, cos, np.ones((seq, HEAD_DIM - ROT_DIM))], axis=1)
    sa_h = np.concatenate([-sin, zh, zeros], axis=1)
    sb_h = np.concatenate([zh, sin, zeros], axis=1)
    tile = lambda t: jnp.asarray(np.concatenate([t, t], axis=1), F32)
    return tile(cos_h), tile(sa_h), tile(sb_h)


def kernel(x, attn_norm, w_in, b_forget, w_out, ffn_norm, w_group, w_expert, w_gate_e, w_up_e, w_down_e, final_norm):
    b, seq, d = x.shape
    assert d == D_MODEL and w_in.shape[0] == 1, "single-layer block"
    n = b * seq
    scale = HEAD_DIM ** -0.5 * LOG2E
    col_scale = np.ones((2 * QKV_WIDTH + LANES,), np.float32)
    col_scale[0:WIDTH_A] = scale
    col_scale[QKV_WIDTH:QKV_WIDTH + WIDTH_B] = scale
    w = (jnp.pad(w_in[0].T, ((0, LANES - HEADS_B), (0, 0))) * col_scale[:, None]).astype(BF16)
    bf = jnp.pad(b_forget[0].astype(F32), (0, LANES - HEADS_B))[None, :]
    cosv, sa, sb = _rotary_tables(seq)
    x2 = x.reshape(n, d)

    qkva, qkvb, c = _inproj(x2, attn_norm[0][None, :], w, cosv, sa, sb, bf, seq=seq, tm=512)
    out_a = _dilated(qkva.reshape(b, seq, QKV_WIDTH))
    out_b, wg, wu, wd = _fox(qkvb.reshape(b, seq, QKV_WIDTH), c.reshape(b, seq, LANES),
                             w_gate_e[0], w_up_e[0], w_down_e[0], tq=1024)

    wr = jnp.pad(jnp.concatenate([w_group[0], w_expert[0]], axis=1).astype(F32),
                 ((0, 0), (0, LANES - N_GROUPS - N_EXPERTS)))
    wr_hi = wr.astype(BF16)
    wr = jnp.concatenate([wr_hi, (wr - wr_hi.astype(F32)).astype(BF16)], axis=1)
    x1, h2ext, cls3 = _outproj(out_a.reshape(n, WIDTH_A), out_b.reshape(n, WIDTH_B), x2,
                               w_out[0].astype(BF16), ffn_norm[0][None, :], wr, tm=512)
    out = _moe(h2ext, cls3, x1, wg, wu, wd, final_norm[None, :])
    return out.reshape(b, seq, d)
```

```python
import functools

import jax
import jax.numpy as jnp
import numpy as np
from jax import lax
from jax.experimental import pallas as pl
from jax.experimental.pallas import tpu as pltpu

F32 = jnp.float32
BF16 = jnp.bfloat16

D_MODEL = 1024
HEAD_DIM = 64
HEADS_A = 8
HEADS_B = 8
WIDTH_A = HEADS_A * HEAD_DIM
WIDTH_B = HEADS_B * HEAD_DIM
QKV_WIDTH = 3 * WIDTH_A
DILATIONS = ((128, 1), (512, 4), (2048, 16))
ROT_DIM = HEAD_DIM // 4
ROT_HALF = ROT_DIM // 2
ROPE_THETA = 500000.0
N_GROUPS = 4
EXPERTS_PER_GROUP = 4
N_EXPERTS = N_GROUPS * EXPERTS_PER_GROUP
D_EXPERT = 512
NORM_EPS = 1e-6
NEG_INF = -1e30

LANES = 128
HEAD_PAIRS = WIDTH_A // LANES
WIN = 128
MAX_DIL = 16
MID_DIL = 4
N_MID = MAX_DIL // MID_DIL
assert sorted(d for _, d in DILATIONS) == [1, MID_DIL, MAX_DIL] and all(w // d == WIN for w, d in DILATIONS)
PERM = MAX_DIL * MAX_DIL
KV_BLOCK = 256
N_PAIRS = EXPERTS_PER_GROUP * (EXPERTS_PER_GROUP - 1) // 2
N_CLASSES = N_GROUPS * N_PAIRS
PAIR_ORDER = ((0, 1), (0, 2), (1, 2), (1, 3), (0, 3), (2, 3))
assert len(PAIR_ORDER) == N_PAIRS
MOE_TILE = 256
EXT_WIDTH = D_MODEL + LANES
LOGIT_MAX = 60.0
BRANCH_UNROLL = 16
DMA_THREADS = 2
ROUTER_OFF = N_GROUPS
VMEM_LIMIT = 56 * 1024 * 1024
LOG2E = 1.4426950408889634
BOUND_SLACK = 1.0 + 2.0 ** -6
BOUND_EPS = 2.0 ** -7


N_PARTS = 3


def _split3(x):
    hi = x.astype(BF16)
    r1 = x - hi.astype(F32)
    mid = r1.astype(BF16)
    lo = (r1 - mid.astype(F32)).astype(BF16)
    return hi, mid, lo


def _dot_nt(a, b):
    return lax.dot_general(a, b, (((1,), (1,)), ((), ())), preferred_element_type=F32)


def _inproj_kernel(x_ref, g_ref, w_ref, cos_ref, sa_ref, sb_ref, bf_ref,
                   qkva_ref, qkvb_ref, c_ref, h_scr, carry_scr, *, tiles_per_seq):
    i = pl.program_id(0)
    tm = x_ref.shape[0]
    x = x_ref[...]
    ms = jnp.mean(x * x, axis=-1, keepdims=True)
    h_scr[...] = (x * lax.rsqrt(ms + NORM_EPS) * g_ref[...]).astype(BF16)
    cosv = cos_ref[...]
    sa = sa_ref[...]
    sb = sb_ref[...]
    for g in range(6):
        p = _dot_nt(h_scr[...], w_ref[g * WIDTH_A:(g + 1) * WIDTH_A, :])
        dst = qkva_ref if g < 3 else qkvb_ref
        col0 = (g % 3) * WIDTH_A
        if g < 2:
            for k in range(HEAD_PAIRS):
                t = p[:, k * LANES:(k + 1) * LANES]
                t = (t * cosv + pltpu.roll(t, LANES - ROT_HALF, 1) * sa + pltpu.roll(t, ROT_HALF, 1) * sb)
                dst[:, col0 + k * LANES:col0 + (k + 1) * LANES] = t.astype(BF16)
        else:
            dst[:, col0:col0 + WIDTH_A] = p.astype(BF16)
    z = _dot_nt(h_scr[...], w_ref[2 * QKV_WIDTH:2 * QKV_WIDTH + LANES, :]) + bf_ref[...]
    lf = jnp.minimum(z, 0.0) - jnp.log1p(jnp.exp(-jnp.abs(z)))
    lane = lax.broadcasted_iota(jnp.int32, (tm, LANES), 1)
    lf = jnp.where(lane < HEADS_B, lf * LOG2E, 0.0)
    hi, mid, lo = _split3(lf)
    tri = (lax.broadcasted_iota(jnp.int32, (tm, tm), 0) >= lax.broadcasted_iota(jnp.int32, (tm, tm), 1)).astype(BF16)
    two = jnp.dot(tri, jnp.concatenate([hi, mid], axis=1), preferred_element_type=F32)
    cs = two[:, 0:LANES] + two[:, LANES:2 * LANES] + jnp.dot(tri, lo, preferred_element_type=F32)

    @pl.when(i % tiles_per_seq == 0)
    def _():
        carry_scr[...] = jnp.zeros_like(carry_scr)

    c = cs + carry_scr[0:1, :]
    c_ref[...] = c
    carry_scr[...] = jnp.broadcast_to(c[tm - 1:tm, :], carry_scr.shape)


def _inproj(x2, g, w, cosv, sa, sb, bf, *, seq, tm):
    n = x2.shape[0]
    kern = functools.partial(_inproj_kernel, tiles_per_seq=seq // tm)
    tps = seq // tm
    return pl.pallas_call(
        kern,
        grid=(n // tm,),
        in_specs=[
            pl.BlockSpec((tm, D_MODEL), lambda i: (i, 0)),
            pl.BlockSpec((1, D_MODEL), lambda i: (0, 0)),
            pl.BlockSpec(w.shape, lambda i: (0, 0)),
            pl.BlockSpec((tm, LANES), lambda i: (i % tps, 0)),
            pl.BlockSpec((tm, LANES), lambda i: (i % tps, 0)),
            pl.BlockSpec((tm, LANES), lambda i: (i % tps, 0)),
            pl.BlockSpec((1, LANES), lambda i: (0, 0)),
        ],
        out_specs=[
            pl.BlockSpec((tm, QKV_WIDTH), lambda i: (i, 0)),
            pl.BlockSpec((tm, QKV_WIDTH), lambda i: (i, 0)),
            pl.BlockSpec((tm, LANES), lambda i: (i, 0)),
        ],
        out_shape=[
            jax.ShapeDtypeStruct((n, QKV_WIDTH), BF16),
            jax.ShapeDtypeStruct((n, QKV_WIDTH), BF16),
            jax.ShapeDtypeStruct((n, LANES), F32),
        ],
        scratch_shapes=[pltpu.VMEM((tm, D_MODEL), BF16), pltpu.VMEM((8, LANES), F32)],
        compiler_params=pltpu.CompilerParams(dimension_semantics=("arbitrary",), vmem_limit_bytes=VMEM_LIMIT),
        name="inproj",
    )(x2, g, w, cosv, sa, sb, bf)


def _own_lanes(lane, hl):
    return (lane < HEAD_DIM) if hl == 0 else (lane >= HEAD_DIM)


def _head_sum_selector():
    lane = lax.broadcasted_iota(jnp.int32, (LANES, LANES), 0)
    col = lax.broadcasted_iota(jnp.int32, (LANES, LANES), 1)
    return jnp.where(((col == 0) & (lane < HEAD_DIM)) | ((col == 1) & (lane >= HEAD_DIM)), 1.0, 0.0).astype(BF16)


def _max_sq_norm(t, selector):
    return jnp.max(jnp.dot((t * t).astype(BF16), selector, preferred_element_type=F32), axis=0, keepdims=True)


def _norm_bound(qsq, ksq_max):
    return jnp.sqrt(qsq * ksq_max) * BOUND_SLACK + BOUND_EPS


def _fox_kernel(q_ref, k_ref, v_ref, c_ref, wg_ref, wu_ref, wd_ref, o_ref, wg_out, wu_out, wd_out,
                q0_scr, q1_scr, k0_scr, k1_scr, v0_scr, v1_scr, acc0_scr, acc1_scr, s_scr,
                *, tq, seq, chunk):
    hp = pl.program_id(1)
    for src, dst in ((wg_ref, wg_out), (wu_ref, wu_out), (wd_ref, wd_out)):
        dst[...] = src[...].astype(dst.dtype)
    q_scr = (q0_scr, q1_scr)
    k_scr = (k0_scr, k1_scr)
    v_scr = (v0_scr, v1_scr)
    acc_scr = (acc0_scr, acc1_scr)

    def placement(part):
        row = lax.broadcasted_iota(jnp.int32, (LANES, 2 * LANES), 0)
        col = lax.broadcasted_iota(jnp.int32, (LANES, 2 * LANES), 1)
        m = jnp.zeros((LANES, 2 * LANES), F32)
        for hl in range(2):
            first = HEAD_DIM if hl == 0 else 0
            src = row == 2 * hp + hl
            m = jnp.where(src & (col == first + part), 1.0, m)
            m = jnp.where(src & (col == LANES + first + N_PARTS + part), -1.0, m)
            if part == 0:
                ones_q = (col >= first + N_PARTS) & (col < first + 2 * N_PARTS)
                ones_k = (col >= LANES + first) & (col < LANES + first + N_PARTS)
                m = jnp.where((row == LANES - 1) & (ones_q | ones_k), 1.0, m)
        return m.astype(BF16)

    def build():
        lane = lax.broadcasted_iota(jnp.int32, (chunk, LANES), 1)
        place_hi_mid = jnp.concatenate([placement(0), placement(1)], axis=0)
        place_lo = placement(2)

        def body(ci, sq_max):
            rows = pl.ds(ci * chunk, chunk)
            qc = q_ref[0, rows, :].astype(F32)
            kc = k_ref[0, rows, :].astype(F32)
            vc = v_ref[0, rows, :].astype(F32)
            hi, mid, lo = _split3(jnp.where(lane == LANES - 1, 1.0, c_ref[0, rows, :]))
            aug = (jnp.dot(jnp.concatenate([hi, mid], axis=1), place_hi_mid, preferred_element_type=F32)
                   + jnp.dot(lo, place_lo, preferred_element_type=F32))
            for hl in range(2):
                own = _own_lanes(lane, hl)
                q_scr[hl][rows, :] = jnp.where(own, qc, aug[:, 0:LANES]).astype(BF16)
                k_scr[hl][rows, :] = jnp.where(own, kc, aug[:, LANES:2 * LANES]).astype(BF16)
                one_lane = HEAD_DIM if hl == 0 else 0
                v_scr[hl][rows, :] = jnp.where(own, vc, jnp.where(lane == one_lane, 1.0, 0.0)).astype(BF16)
            return (jnp.maximum(sq_max[0], _max_sq_norm(qc, selector)),
                    jnp.maximum(sq_max[1], _max_sq_norm(kc, selector)))

        selector = _head_sum_selector()
        sq_max = (jnp.zeros((1, LANES), F32),) * 2
        for ci in range(seq // chunk):
            sq_max = body(ci, sq_max)
        return jnp.max(_norm_bound(sq_max[0], sq_max[1]))

    worst = build()
    lane_q = lax.broadcasted_iota(jnp.int32, (tq, LANES), 1)
    n_diag = tq // KV_BLOCK
    row_minus_col = (lax.broadcasted_iota(jnp.int32, (tq, KV_BLOCK), 0)
                     - lax.broadcasted_iota(jnp.int32, (tq, KV_BLOCK), 1))

    def block_rows(j):
        start = j * KV_BLOCK
        return pl.ds(start if isinstance(start, int) else pl.multiple_of(start, KV_BLOCK), KV_BLOCK)

    def causal_mask(s, d, top=0):
        return jnp.where(row_minus_col[top:tq, :] >= d * KV_BLOCK, s, NEG_INF)

    def logits(qi, hl, j, top=0):
        return lax.dot_general(q_scr[hl][pl.ds(qi * tq + top, tq - top), :], k_scr[hl][block_rows(j), :],
                               (((1,), (1,)), ((), ())), preferred_element_type=F32)

    def values(hl, j):
        return v_scr[hl][block_rows(j), :]

    def finish(qi, a0, a1):
        l0 = a0[:, HEAD_DIM:HEAD_DIM + 1]
        l1 = a1[:, 0:1]
        o_ref[0, pl.ds(qi * tq, tq), :] = jnp.where(lane_q < HEAD_DIM, a0 / l0, a1 / l1).astype(o_ref.dtype)

    def fast_tile(qi):
        n_full = qi * n_diag
        acc0_scr[...] = jnp.zeros_like(acc0_scr)
        acc1_scr[...] = jnp.zeros_like(acc1_scr)

        def produce(j, slot, top=0):
            for hl in range(2):
                s_scr[slot, hl, top:tq, :] = logits(qi, hl, j, top)

        def consume(j, slot, diag):
            top = 0 if diag is None else diag * KV_BLOCK
            for hl in range(2):
                s = s_scr[slot, hl, top:tq, :]
                if diag is not None:
                    s = causal_mask(s, diag, top)
                acc_scr[hl][top:tq, :] += jnp.dot(jnp.exp2(s).astype(BF16), values(hl, j),
                                                  preferred_element_type=F32)

        for j in range(min(n_diag, n_full + n_diag)):
            produce(j, j % n_diag, 0 if j < n_full else (j - n_full) * KV_BLOCK)
        for j in range(n_full + n_diag):
            consume(j, j % n_diag, None if j < n_full else j - n_full)
            nxt = j + n_diag
            if nxt < n_full + n_diag:
                produce(nxt, nxt % n_diag, 0 if nxt < n_full else (nxt - n_full) * KV_BLOCK)
        finish(qi, acc0_scr[...], acc1_scr[...])

    def general_tile(qi):
        n_full = qi * n_diag

        def step(j, carry, diag):
            new = []
            for hl in range(2):
                m, a = carry[2 * hl:2 * hl + 2]
                s = logits(qi, hl, j)
                if diag is not None:
                    s = causal_mask(s, diag)
                mn = jnp.maximum(m, jnp.max(s, axis=1, keepdims=True))
                a = jnp.exp2(m - mn) * a + jnp.dot(jnp.exp2(s - mn).astype(BF16), values(hl, j),
                                                  preferred_element_type=F32)
                new += [mn, a]
            return tuple(new)

        carry = (jnp.full((tq, 1), NEG_INF, F32), jnp.zeros((tq, LANES), F32)) * 2
        carry = lax.fori_loop(0, n_full, lambda j, c: step(j, c, None), carry)
        for d in range(n_diag):
            carry = step(n_full + d, carry, d)
        finish(qi, carry[1], carry[3])

    for tile_fn, cond in ((fast_tile, worst <= LOGIT_MAX), (general_tile, worst > LOGIT_MAX)):
        @pl.when(cond)
        def _():
            for qi in range(seq // tq):
                tile_fn(qi)


def _fox(qkvb, c, wg, wu, wd, *, tq, chunk=512):
    b, seq, _ = qkvb.shape
    steps = b * HEAD_PAIRS
    parts = max(steps // N_EXPERTS, 1)
    per = max(N_EXPERTS // steps, 1)
    assert parts * N_EXPERTS == steps * per and D_MODEL % parts == 0 and D_EXPERT % parts == 0

    def slice_spec(rows, cols):
        def index(bi, hp):
            step = bi * HEAD_PAIRS + hp
            return (step // parts, step % parts, 0)
        return pl.BlockSpec((per, rows // parts, cols), index)

    w_specs = [slice_spec(D_MODEL, D_EXPERT), slice_spec(D_MODEL, D_EXPERT), slice_spec(D_EXPERT, D_MODEL)]
    kern = functools.partial(_fox_kernel, tq=tq, seq=seq, chunk=chunk)
    return pl.pallas_call(
        kern,
        grid=(b, HEAD_PAIRS),
        in_specs=[
            pl.BlockSpec((1, seq, LANES), lambda bi, hp: (bi, 0, hp)),
            pl.BlockSpec((1, seq, LANES), lambda bi, hp: (bi, 0, HEAD_PAIRS + hp)),
            pl.BlockSpec((1, seq, LANES), lambda bi, hp: (bi, 0, 2 * HEAD_PAIRS + hp)),
            pl.BlockSpec((1, seq, LANES), lambda bi, hp: (bi, 0, 0)),
        ] + w_specs,
        out_specs=[pl.BlockSpec((1, seq, LANES), lambda bi, hp: (bi, 0, hp))] + w_specs,
        out_shape=[jax.ShapeDtypeStruct((b, seq, WIDTH_B), BF16)]
        + [jax.ShapeDtypeStruct(w.shape, BF16) for w in (wg, wu, wd)],
        scratch_shapes=[pltpu.VMEM((seq, LANES), BF16)] * 6 + [
                        pltpu.VMEM((tq, LANES), F32), pltpu.VMEM((tq, LANES), F32),
                        pltpu.VMEM((tq // KV_BLOCK, 2, tq, KV_BLOCK), F32)],
        compiler_params=pltpu.CompilerParams(
            dimension_semantics=("arbitrary", "arbitrary"), vmem_limit_bytes=VMEM_LIMIT),
        name="fox",
    )(qkvb, qkvb, qkvb, c, wg, wu, wd)


def _log2(n):
    assert n & (n - 1) == 0
    return n.bit_length() - 1


def _permutation():
    ri = lax.broadcasted_iota(jnp.int32, (PERM, PERM), 0)
    ci = lax.broadcasted_iota(jnp.int32, (PERM, PERM), 1)
    sh, lo = _log2(MAX_DIL), MAX_DIL - 1
    return (((ri >> sh) == (ci & lo)) & ((ri & lo) == (ci >> sh))).astype(BF16)


def _band_masks():
    qa = lax.broadcasted_iota(jnp.int32, (WIN, 2 * WIN), 0)
    kb = lax.broadcasted_iota(jnp.int32, (WIN, 2 * WIN), 1)
    cur = kb >= WIN
    kin = kb & (WIN - 1)
    prev_shift = jnp.where(cur, 0, WIN)
    dist = qa - kin + prev_shift
    chunk = WIN // N_MID
    offset = lambda a: ((a & (chunk - 1)) << _log2(N_MID)) + (a >> _log2(chunk))
    dist_mid = offset(qa) - offset(kin) + prev_shift
    return (dist >= 0) & (dist <= WIN), (dist_mid >= 0) & (dist_mid <= WIN), cur


def _dilated_general(q_ref, k_ref, v_ref, o_ref, q16, k16, v16, m_s, l_s, acc_s, on_s, lse_s, *, seq):
    sub = seq // MAX_DIL
    own0 = lax.broadcasted_iota(jnp.int32, (WIN, LANES), 1) < HEAD_DIM
    perm = _permutation()
    band_nat, band_4, cur = _band_masks()

    def attend(qb, kb, vb, mask, state):
        qf = qb.astype(F32)
        new = []
        for hl in range(2):
            m, l, a = state[3 * hl:3 * hl + 3]
            own = own0 if hl == 0 else jnp.logical_not(own0)
            qh = jnp.where(own, qf, 0.0).astype(BF16)
            s = lax.dot_general(qh, kb, (((1,), (1,)), ((), ())), preferred_element_type=F32)
            s = jnp.where(mask, s, NEG_INF)
            mn = jnp.maximum(m, jnp.max(s, axis=1, keepdims=True))
            alpha = jnp.exp2(m - mn)
            p = jnp.exp2(s - mn)
            l = alpha * l + jnp.sum(p, axis=1, keepdims=True)
            a = alpha * a + jnp.dot(p.astype(BF16), vb, preferred_element_type=F32)
            new += [mn, l, a]
        return new

    def pack_state(st):
        m0, l0, a0, m1, l1, a1 = st
        return (jnp.where(own0, m0, m1), jnp.where(own0, l0, l1), jnp.where(own0, a0, a1))

    def deint(blk, carry):
        r0 = pl.multiple_of(blk * PERM, PERM)
        j0 = pl.multiple_of(blk * MAX_DIL, MAX_DIL)
        for src, dst in ((q_ref, q16), (k_ref, k16), (v_ref, v16)):
            y = jnp.dot(perm, src[0, pl.ds(r0, PERM), :], preferred_element_type=F32).astype(BF16)
            for r in range(MAX_DIL):
                dst[r, pl.ds(j0, MAX_DIL), :] = y[r * MAX_DIL:(r + 1) * MAX_DIL, :]
        return carry

    lax.fori_loop(0, seq // PERM, deint, 0)

    fresh = [jnp.full((WIN, 1), NEG_INF, F32), jnp.zeros((WIN, 1), F32), jnp.zeros((WIN, LANES), F32)] * 2
    nb16 = sub // WIN

    def d16_body(t, carry):
        r = t // nb16
        n = t % nb16
        c0 = pl.multiple_of(n * WIN, WIN)
        p0 = pl.multiple_of(jnp.maximum(n - 1, 0) * WIN, WIN)
        qb = q16[r, pl.ds(c0, WIN), :]
        kb = jnp.concatenate([k16[r, pl.ds(p0, WIN), :], k16[r, pl.ds(c0, WIN), :]], axis=0)
        vb = jnp.concatenate([v16[r, pl.ds(p0, WIN), :], v16[r, pl.ds(c0, WIN), :]], axis=0)
        mask = band_nat & (cur | (n > 0))
        mm, ll, aa = pack_state(attend(qb, kb, vb, mask, fresh))
        row = pl.multiple_of(r * sub + c0, WIN)
        m_s[pl.ds(row, WIN), :] = mm
        l_s[pl.ds(row, WIN), :] = ll
        acc_s[pl.ds(row, WIN), :] = aa
        return carry

    lax.fori_loop(0, MAX_DIL * nb16, d16_body, 0)

    nb4 = (seq // MID_DIL) // WIN
    ch = WIN // N_MID

    def d4_body(t, carry):
        r4 = t // nb4
        n = t % nb4
        c0 = pl.multiple_of(n * ch, ch)
        p0 = pl.multiple_of(jnp.maximum(n - 1, 0) * ch, ch)

        def gather(ref, j0):
            return [ref[r4 + MID_DIL * q, pl.ds(j0, ch), :] for q in range(N_MID)]

        qb = jnp.concatenate(gather(q16, c0), axis=0)
        kb = jnp.concatenate(gather(k16, p0) + gather(k16, c0), axis=0)
        vb = jnp.concatenate(gather(v16, p0) + gather(v16, c0), axis=0)
        rows = [pl.multiple_of((r4 + MID_DIL * q) * sub + c0, ch) for q in range(N_MID)]
        mm = jnp.concatenate([m_s[pl.ds(rw, ch), :] for rw in rows], axis=0)
        ll = jnp.concatenate([l_s[pl.ds(rw, ch), :] for rw in rows], axis=0)
        aa = jnp.concatenate([acc_s[pl.ds(rw, ch), :] for rw in rows], axis=0)
        state = [mm[:, 0:1], ll[:, 0:1], aa, mm[:, HEAD_DIM:HEAD_DIM + 1], ll[:, HEAD_DIM:HEAD_DIM + 1], aa]
        mask = band_4 & (cur | (n > 0))
        mm, ll, aa = pack_state(attend(qb, kb, vb, mask, state))
        for q, rw in enumerate(rows):
            m_s[pl.ds(rw, ch), :] = mm[q * ch:(q + 1) * ch, :]
            l_s[pl.ds(rw, ch), :] = ll[q * ch:(q + 1) * ch, :]
            acc_s[pl.ds(rw, ch), :] = aa[q * ch:(q + 1) * ch, :]
        return carry

    lax.fori_loop(0, MID_DIL * nb4, d4_body, 0)

    def renat(blk, carry):
        j0 = pl.multiple_of(blk * MAX_DIL, MAX_DIL)
        r0 = pl.multiple_of(blk * PERM, PERM)
        rows = [pl.multiple_of(r * sub + j0, MAX_DIL) for r in range(MAX_DIL)]
        mm = jnp.concatenate([m_s[pl.ds(rw, MAX_DIL), :] for rw in rows], axis=0)
        ll = jnp.concatenate([l_s[pl.ds(rw, MAX_DIL), :] for rw in rows], axis=0)
        aa = jnp.concatenate([acc_s[pl.ds(rw, MAX_DIL), :] for rw in rows], axis=0)
        o = (aa / ll).astype(BF16)
        hi, mid, lo = _split3(mm + jnp.log2(ll))
        on_s[pl.ds(r0, PERM), :] = jnp.dot(perm, o, preferred_element_type=F32)
        lse_s[pl.ds(r0, PERM), :] = (jnp.dot(perm, hi, preferred_element_type=F32)
                                     + jnp.dot(perm, mid, preferred_element_type=F32)
                                     + jnp.dot(perm, lo, preferred_element_type=F32))
        return carry

    lax.fori_loop(0, seq // PERM, renat, 0)

    def d1_body(n, carry):
        c0 = pl.multiple_of(n * WIN, WIN)
        p0 = pl.multiple_of(jnp.maximum(n - 1, 0) * WIN, WIN)
        qb = q_ref[0, pl.ds(c0, WIN), :]
        kb = jnp.concatenate([k_ref[0, pl.ds(p0, WIN), :], k_ref[0, pl.ds(c0, WIN), :]], axis=0)
        vb = jnp.concatenate([v_ref[0, pl.ds(p0, WIN), :], v_ref[0, pl.ds(c0, WIN), :]], axis=0)
        lse = lse_s[pl.ds(c0, WIN), :]
        o = on_s[pl.ds(c0, WIN), :]
        one = jnp.ones((WIN, 1), F32)
        state = [lse[:, 0:1], one, o, lse[:, HEAD_DIM:HEAD_DIM + 1], one, o]
        mask = band_nat & (cur | (n > 0))
        m0, l0, a0, m1, l1, a1 = attend(qb, kb, vb, mask, state)
        o_ref[0, pl.ds(c0, WIN), :] = jnp.where(own0, a0 / l0, a1 / l1).astype(o_ref.dtype)
        return carry

    lax.fori_loop(0, seq // WIN, d1_body, 0)


def _dilated_fast(o_ref, qn, kn, vn, qr, kr, vr, fr, fn, s_scr, mask_scr, *, seq):
    sub = seq // MAX_DIL
    own0 = lax.broadcasted_iota(jnp.int32, (WIN, LANES), 1) < HEAD_DIM
    head_mask = [jnp.where(own0, 1.0, 0.0).astype(BF16), jnp.where(own0, 0.0, 1.0).astype(BF16)]
    perm = _permutation()

    band_nat, band_4, cur = _band_masks()
    for i, band in enumerate((band_nat, band_4)):
        mask_scr[2 * i] = jnp.where(band & cur, 1.0, 0.0).astype(BF16)
        mask_scr[2 * i + 1] = jnp.where(band, 1.0, 0.0).astype(BF16)

    def run_branch(nblocks, block_in_seq, fetch, mask_base, sink, before_trip=None):
        def produce(t, pair, u):
            for hl in range(2):
                qb, kb = fetch(t, hl, True)
                s_scr[pair, u, hl] = lax.dot_general(qb * head_mask[hl], kb, (((1,), (1,)), ((), ())),
                                                     preferred_element_type=F32)

        def consume(t, pair, u):
            mk = mask_scr[mask_base + jnp.minimum(block_in_seq(t), 1)]
            pv = []
            for hl in range(2):
                p = jnp.exp2(s_scr[pair, u, hl]).astype(BF16) * mk
                pv.append(jnp.dot(p, fetch(t, hl, False), preferred_element_type=F32))
            sink(t, pv)

        for u in range(BRANCH_UNROLL):
            produce(u, 0, u)

        def trip(i, last):
            pair = i & 1
            if before_trip is not None:
                before_trip(i)
            for u in range(BRANCH_UNROLL):
                consume(BRANCH_UNROLL * i + u, pair, u)
            if not last:
                for u in range(BRANCH_UNROLL):
                    produce(BRANCH_UNROLL * (i + 1) + u, 1 - pair, u)

        def body(i, carry):
            trip(i, False)
            return carry

        trips = nblocks // BRANCH_UNROLL
        lax.fori_loop(0, trips - 1, body, 0)
        trip(jnp.int32(trips - 1), True)

    def prev_cur(ref_block, n, size):
        c0 = pl.multiple_of(n * size, size)
        p0 = pl.multiple_of(jnp.maximum(n - 1, 0) * size, size)
        return ref_block(p0) + ref_block(c0)

    nb16 = sub // WIN

    def fetch16(t, hl, qk):
        r, n = t // nb16, t % nb16
        if qk:
            return (qr[hl][r, pl.ds(pl.multiple_of(n * WIN, WIN), WIN), :],
                    jnp.concatenate(prev_cur(lambda s: [kr[hl][r, pl.ds(s, WIN), :]], n, WIN), axis=0))
        return jnp.concatenate(prev_cur(lambda s: [vr[hl][r, pl.ds(s, WIN), :]], n, WIN), axis=0)

    def sink16(t, pv):
        row = pl.multiple_of(t * WIN, WIN)
        for hl in range(2):
            fr[hl][pl.ds(row, WIN), :] = pv[hl]

    run_branch(MAX_DIL * nb16, lambda t: t % nb16, fetch16, 0, sink16)

    nb4 = (seq // MID_DIL) // WIN
    ch = WIN // N_MID

    def fetch4(t, hl, qk):
        r4, n = t // nb4, t % nb4
        gather = lambda ref: (lambda s: [ref[r4 + MID_DIL * q, pl.ds(s, ch), :] for q in range(N_MID)])
        if qk:
            return (jnp.concatenate(gather(qr[hl])(pl.multiple_of(n * ch, ch)), axis=0),
                    jnp.concatenate(prev_cur(gather(kr[hl]), n, ch), axis=0))
        return jnp.concatenate(prev_cur(gather(vr[hl]), n, ch), axis=0)

    def sink4(t, pv):
        r4, n = t // nb4, t % nb4
        for q in range(N_MID):
            row = pl.multiple_of((r4 + MID_DIL * q) * sub + n * ch, ch)
            for hl in range(2):
                fr[hl][pl.ds(row, ch), :] += pv[hl][q * ch:(q + 1) * ch, :]

    run_branch(MID_DIL * nb4, lambda t: t % nb4, fetch4, 2, sink4)

    def renat(i):
        for u in range(BRANCH_UNROLL * WIN // PERM):
            blk = i * (BRANCH_UNROLL * WIN // PERM) + u
            j0 = pl.multiple_of(blk * MAX_DIL, MAX_DIL)
            r0 = pl.multiple_of(blk * PERM, PERM)
            for hl in range(2):
                a = jnp.concatenate([fr[hl][pl.ds(pl.multiple_of(r * sub + j0, MAX_DIL), MAX_DIL), :]
                                     for r in range(MAX_DIL)], axis=0)
                hi = a.astype(BF16)
                lo = (a - hi.astype(F32)).astype(BF16)
                y = jnp.dot(perm, jnp.concatenate([hi, lo], axis=1), preferred_element_type=F32)
                fn[hl][pl.ds(r0, PERM), :] = y[:, 0:LANES] + y[:, LANES:2 * LANES]

    def fetch1(t, hl, qk):
        if qk:
            return (qn[hl][pl.ds(pl.multiple_of(t * WIN, WIN), WIN), :],
                    jnp.concatenate(prev_cur(lambda s: [kn[hl][pl.ds(s, WIN), :]], t, WIN), axis=0))
        return jnp.concatenate(prev_cur(lambda s: [vn[hl][pl.ds(s, WIN), :]], t, WIN), axis=0)

    def sink1(t, pv):
        row = pl.multiple_of(t * WIN, WIN)
        t0 = fn[0][pl.ds(row, WIN), :] + pv[0]
        t1 = fn[1][pl.ds(row, WIN), :] + pv[1]
        o_ref[0, pl.ds(row, WIN), :] = jnp.where(own0, t0 / t0[:, HEAD_DIM:HEAD_DIM + 1],
                                                 t1 / t1[:, 0:1]).astype(o_ref.dtype)

    run_branch(seq // WIN, lambda t: t, fetch1, 0, sink1, before_trip=renat)


def _dilated_kernel(q_ref, k_ref, v_ref, o_ref, vn0, vn1, qr, kr, vr0, vr1,
                    f0, f1, f2, f3, f4, s_scr, mask_scr, *, seq, chunk):
    lane = lax.broadcasted_iota(jnp.int32, (chunk, LANES), 1)
    vn = (vn0, vn1)

    perm = _permutation()

    def stats(ci, sq_max):
        rows = pl.ds(ci * chunk, chunk)
        qb = q_ref[0, rows, :]
        kb = k_ref[0, rows, :]
        vc = v_ref[0, rows, :].astype(F32)
        vb = []
        for hl in range(2):
            ones_lane = HEAD_DIM if hl == 0 else 0
            vb.append(jnp.where(_own_lanes(lane, hl), vc, jnp.where(lane == ones_lane, 1.0, 0.0)).astype(BF16))
            vn[hl][rows, :] = vb[hl]
        for u in range(chunk // PERM):
            j0 = (ci * (chunk // PERM) + u) * MAX_DIL
            for srcs, dsts in (((qb, kb), (qr, kr)), (vb, (vr0, vr1))):
                both = jnp.concatenate([src[u * PERM:(u + 1) * PERM, :] for src in srcs], axis=1)
                y = jnp.dot(perm, both, preferred_element_type=F32).astype(BF16)
                for r in range(MAX_DIL):
                    for half, dst in enumerate(dsts):
                        dst[r, pl.ds(j0, MAX_DIL), :] = y[r * MAX_DIL:(r + 1) * MAX_DIL, half * LANES:(half + 1) * LANES]
        return (jnp.maximum(sq_max[0], _max_sq_norm(qb.astype(F32), selector)),
                jnp.maximum(sq_max[1], _max_sq_norm(kb.astype(F32), selector)))

    selector = _head_sum_selector()
    sq_max = (jnp.zeros((1, LANES), F32),) * 2
    for ci in range(seq // chunk):
        sq_max = stats(ci, sq_max)
    worst = jnp.max(_norm_bound(sq_max[0], sq_max[1]))

    @pl.when(worst <= LOGIT_MAX)
    def _fast():
        qn, kn = q_ref.at[0], k_ref.at[0]
        _dilated_fast(o_ref, (qn, qn), (kn, kn), vn, (qr, qr), (kr, kr), (vr0, vr1), (f0, f1), (f2, f3),
                      s_scr, mask_scr, seq=seq)

    @pl.when(worst > LOGIT_MAX)
    def _general():
        _dilated_general(q_ref, k_ref, v_ref, o_ref, qr, kr, vr0, f0, f1, f2, f3, f4, seq=seq)


def _dilated(qkva, *, chunk=512):
    b, seq, _ = qkva.shape
    sub = seq // MAX_DIL
    kern = functools.partial(_dilated_kernel, seq=seq, chunk=chunk)
    return pl.pallas_call(
        kern,
        grid=(b, HEAD_PAIRS),
        in_specs=[
            pl.BlockSpec((1, seq, LANES), lambda bi, hp: (bi, 0, hp)),
            pl.BlockSpec((1, seq, LANES), lambda bi, hp: (bi, 0, HEAD_PAIRS + hp)),
            pl.BlockSpec((1, seq, LANES), lambda bi, hp: (bi, 0, 2 * HEAD_PAIRS + hp)),
        ],
        out_specs=pl.BlockSpec((1, seq, LANES), lambda bi, hp: (bi, 0, hp)),
        out_shape=jax.ShapeDtypeStruct((b, seq, WIDTH_A), BF16),
        scratch_shapes=(
            [pltpu.VMEM((seq, LANES), BF16)] * 2
            + [pltpu.VMEM((MAX_DIL, sub, LANES), BF16)] * 4
            + [pltpu.VMEM((seq, LANES), F32)] * 5
            + [pltpu.VMEM((2, BRANCH_UNROLL, 2, WIN, 2 * WIN), F32), pltpu.VMEM((4, WIN, 2 * WIN), BF16)]
        ),
        compiler_params=pltpu.CompilerParams(
            dimension_semantics=("arbitrary", "arbitrary"), vmem_limit_bytes=VMEM_LIMIT),
        name="dilated",
    )(qkva, qkva, qkva)


def _outproj_kernel(oa_ref, ob_ref, x_ref, wo_ref, g_ref, wr_ref, x1_ref, h2_ref, cls_ref):
    tm = x_ref.shape[0]
    x1 = (x_ref[...]
          + jnp.dot(oa_ref[...], wo_ref[0:WIDTH_A, :], preferred_element_type=F32)
          + jnp.dot(ob_ref[...], wo_ref[WIDTH_A:WIDTH_A + WIDTH_B, :], preferred_element_type=F32))
    x1_ref[...] = x1
    ms = jnp.mean(x1 * x1, axis=-1, keepdims=True)
    h2 = x1 * lax.rsqrt(ms + NORM_EPS) * g_ref[...]
    h2_ref[:, 0:D_MODEL] = h2
    h2_hi = h2.astype(BF16)
    h2_lo = (h2 - h2_hi.astype(F32)).astype(BF16)
    both = jnp.dot(h2_hi, wr_ref[...], preferred_element_type=F32)
    logits = (both[:, 0:LANES] + both[:, LANES:2 * LANES]
              + jnp.dot(h2_lo, wr_ref[:, 0:LANES], preferred_element_type=F32))
    lane = lax.broadcasted_iota(jnp.int32, (tm, LANES), 1)
    lane_f = lane.astype(F32)
    big = float(LANES)
    gmask = lane < N_GROUPS
    gl = jnp.where(gmask, logits, NEG_INF)
    gmax = jnp.max(gl, axis=1, keepdims=True)
    gsum = jnp.sum(jnp.where(gmask, jnp.exp(gl - gmax), 0.0), axis=1, keepdims=True)
    p_top = 1.0 / gsum
    g_star = jnp.min(jnp.where(gmask & (gl == gmax), lane_f, big), axis=1, keepdims=True)
    lo_lane = ROUTER_OFF + EXPERTS_PER_GROUP * g_star
    emask = (lane_f >= lo_lane) & (lane_f < lo_lane + EXPERTS_PER_GROUP)
    sel = jnp.where(emask, logits, NEG_INF)
    v1 = jnp.max(sel, axis=1, keepdims=True)
    i1 = jnp.min(jnp.where(emask & (sel == v1), lane_f, big), axis=1, keepdims=True)
    rest = emask & (lane_f != i1)
    sel2 = jnp.where(rest, logits, NEG_INF)
    v2 = jnp.max(sel2, axis=1, keepdims=True)
    i2 = jnp.min(jnp.where(rest & (sel2 == v2), lane_f, big), axis=1, keepdims=True)
    e2 = jnp.exp(v2 - v1)
    w1 = p_top / (1.0 + e2)
    w2 = p_top * e2 / (1.0 + e2)
    e1 = i1 - lo_lane
    e2x = i2 - lo_lane
    a = jnp.minimum(e1, e2x)
    b = jnp.maximum(e1, e2x)
    pair = jnp.zeros_like(a)
    for idx, (pa, pb) in enumerate(PAIR_ORDER):
        pair = jnp.where((a == pa) & (b == pb), float(idx), pair)
    cls = g_star * N_PAIRS + pair
    wa = jnp.where(e1 < e2x, w1, w2)
    wb = jnp.where(e1 < e2x, w2, w1)
    h2_ref[:, D_MODEL:EXT_WIDTH] = jnp.where(lane == 0, wa, jnp.where(lane == 1, wb, 0.0))
    row = lax.broadcasted_iota(jnp.int32, (tm, LANES), 0)
    spread = jnp.where(lane == (row & (LANES - 1)), cls, 0.0)
    cls_ref[0] = jnp.sum(spread.reshape(tm // LANES, LANES, LANES), axis=1).astype(jnp.int32)


def _outproj(oa, ob, x2, wo, g, wr, *, tm):
    n = x2.shape[0]
    return pl.pallas_call(
        _outproj_kernel,
        grid=(n // tm,),
        in_specs=[
            pl.BlockSpec((tm, WIDTH_A), lambda i: (i, 0)),
            pl.BlockSpec((tm, WIDTH_B), lambda i: (i, 0)),
            pl.BlockSpec((tm, D_MODEL), lambda i: (i, 0)),
            pl.BlockSpec(wo.shape, lambda i: (0, 0)),
            pl.BlockSpec((1, D_MODEL), lambda i: (0, 0)),
            pl.BlockSpec(wr.shape, lambda i: (0, 0)),
        ],
        out_specs=[
            pl.BlockSpec((tm, D_MODEL), lambda i: (i, 0)),
            pl.BlockSpec((tm, EXT_WIDTH), lambda i: (i, 0)),
            pl.BlockSpec((1, tm // LANES, LANES), lambda i: (i, 0, 0)),
        ],
        out_shape=[
            jax.ShapeDtypeStruct((n, D_MODEL), F32),
            jax.ShapeDtypeStruct((n, EXT_WIDTH), F32),
            jax.ShapeDtypeStruct((n // tm, tm // LANES, LANES), jnp.int32),
        ],
        compiler_params=pltpu.CompilerParams(dimension_semantics=("arbitrary",), vmem_limit_bytes=VMEM_LIMIT),
        name="outproj",
    )(oa, ob, x2, wo, g, wr)


def _sort_kernel(cls_ref, pos_ref, tile_ref, ends_ref, *, tile):
    cls = cls_ref[...]
    rows = cls.shape[0]
    upper = (lax.broadcasted_iota(jnp.int32, (LANES, LANES), 0)
             <= lax.broadcasted_iota(jnp.int32, (LANES, LANES), 1)).astype(BF16)
    earlier_rows = (lax.broadcasted_iota(jnp.int32, (rows, rows), 0)
                    > lax.broadcasted_iota(jnp.int32, (rows, rows), 1)).astype(BF16)
    ones = jnp.ones((LANES, LANES), BF16)
    lane8 = lax.broadcasted_iota(jnp.int32, tile_ref.shape, 1)
    tile_start = lane8.astype(F32) * tile
    off = jnp.zeros((1, 1), F32)
    pos = jnp.zeros(cls.shape, F32)
    tile_cls = jnp.zeros(tile_ref.shape, F32)
    ends = jnp.zeros(tile_ref.shape, F32)
    for c in range(N_CLASSES):
        hit = cls == c
        hot = jnp.where(hit, 1.0, 0.0).astype(BF16)
        in_row = jnp.dot(hot, upper, preferred_element_type=F32)
        row_tot = jnp.dot(hot, ones, preferred_element_type=F32)
        before = jnp.dot(earlier_rows, row_tot.astype(BF16), preferred_element_type=F32)
        count = before[rows - 1:rows, 0:1] + row_tot[rows - 1:rows, 0:1]
        pos = pos + jnp.where(hit, in_row - 1.0 + before + off, 0.0)
        off = off + jnp.ceil(count / tile) * tile
        tile_cls = tile_cls + jnp.where(tile_start >= off, 1.0, 0.0)
        ends = jnp.where(lane8 == c, off, ends)
    pos_ref[...] = pos.astype(jnp.int32)
    tile_ref[...] = tile_cls.astype(jnp.int32)
    ends_ref[...] = ends.astype(jnp.int32)


def _sort(cls2d, *, tile):
    meta = jax.ShapeDtypeStruct((8, LANES), jnp.int32)
    return pl.pallas_call(
        functools.partial(_sort_kernel, tile=tile),
        out_shape=[jax.ShapeDtypeStruct(cls2d.shape, jnp.int32), meta, meta],
        compiler_params=pltpu.CompilerParams(vmem_limit_bytes=VMEM_LIMIT),
        name="moe_sort",
    )(cls2d)


def _row_copy(src, src_row, dst, dst_row, sem, rows=1):
    return pltpu.make_async_copy(src.at[pl.ds(src_row, rows)], dst.at[pl.ds(dst_row, rows)], sem)


def _dispatch_kernel(pos_ref, ends_ref, h2_ref, xs_hbm, zero_scr, sem, *, ch, tile):
    g = pl.program_id(0)

    @pl.when(g == 0)
    def _zero_tails():
        zero_scr[...] = jnp.zeros_like(zero_scr)
        n_tiles = xs_hbm.shape[0] // tile
        min_used = n_tiles - N_CLASSES
        total = ends_ref[0, N_CLASSES - 1]
        for phase in ("start", "wait"):
            def zero_tile(row0):
                cp = pltpu.make_async_copy(zero_scr, xs_hbm.at[pl.ds(pl.multiple_of(row0, tile), tile)], sem)
                cp.start() if phase == "start" else cp.wait()

            for c in range(N_CLASSES):
                end = ends_ref[0, c]
                prev = ends_ref[0, c - 1] if c else 0
                pl.when(end > prev)(functools.partial(zero_tile, end - tile))
                pl.when((min_used + c) * tile >= total)(functools.partial(zero_tile, (min_used + c) * tile))

    for k in range(ch):
        _row_copy(h2_ref, k, xs_hbm, pos_ref[0, 0, k], sem).start(priority=k % DMA_THREADS)
    _row_copy(h2_ref, 0, xs_hbm, 0, sem, rows=ch).wait()


def _dispatch(pos3, ends, h2ext, *, n_rows, tile):
    n_chunks, _, ch = pos3.shape
    return pl.pallas_call(
        functools.partial(_dispatch_kernel, ch=ch, tile=tile),
        grid=(n_chunks,),
        in_specs=[
            pl.BlockSpec((1, 1, ch), lambda g: (g, 0, 0), memory_space=pltpu.SMEM),
            pl.BlockSpec(ends.shape, lambda g: (0, 0), memory_space=pltpu.SMEM),
            pl.BlockSpec((ch, EXT_WIDTH), lambda g: (g, 0)),
        ],
        out_specs=pl.BlockSpec(memory_space=pl.ANY),
        out_shape=jax.ShapeDtypeStruct((n_rows, EXT_WIDTH), F32),
        scratch_shapes=[pltpu.VMEM((tile, EXT_WIDTH), F32), pltpu.SemaphoreType.DMA(())],
        compiler_params=pltpu.CompilerParams(dimension_semantics=("arbitrary",), vmem_limit_bytes=VMEM_LIMIT),
        name="moe_dispatch",
    )(pos3, ends, h2ext)


def _experts_kernel(ea_ref, eb_ref, used_ref, xs_ref, wga, wua, wda, wgb, wub, wdb, ys_ref):
    del ea_ref, eb_ref

    @pl.when(pl.program_id(0) < used_ref[0])
    def _():
        x = xs_ref[:, 0:D_MODEL].astype(BF16)
        wts = xs_ref[:, D_MODEL:EXT_WIDTH]
        y = None
        for col, (wg, wu, wd) in enumerate(((wga, wua, wda), (wgb, wub, wdb))):
            gate = jnp.dot(x, wg[0], preferred_element_type=F32)
            up = jnp.dot(x, wu[0], preferred_element_type=F32)
            he = (gate / (1.0 + jnp.exp(-gate)) * up * wts[:, col:col + 1]).astype(BF16)
            t = jnp.dot(he, wd[0], preferred_element_type=F32)
            y = t if y is None else y + t
        ys_ref[...] = y

    @pl.when(pl.program_id(0) >= used_ref[0])
    def _():
        ys_ref[...] = jnp.zeros_like(ys_ref)


def _experts(ea, eb, used, xs, wg, wu, wd, *, tile):
    n_tiles = xs.shape[0] // tile
    rows = lambda j, ea, eb, used: (jnp.minimum(j, used[0] - 1), 0)
    first = lambda j, ea, eb, used: (ea[j], 0, 0)
    second = lambda j, ea, eb, used: (eb[j], 0, 0)
    up_spec = lambda m: pl.BlockSpec((1, D_MODEL, D_EXPERT), m)
    down_spec = lambda m: pl.BlockSpec((1, D_EXPERT, D_MODEL), m)
    return pl.pallas_call(
        _experts_kernel,
        grid_spec=pltpu.PrefetchScalarGridSpec(
            num_scalar_prefetch=3,
            grid=(n_tiles,),
            in_specs=[pl.BlockSpec((tile, EXT_WIDTH), rows),
                      up_spec(first), up_spec(first), down_spec(first),
                      up_spec(second), up_spec(second), down_spec(second)],
            out_specs=pl.BlockSpec((tile, D_MODEL), lambda j, ea, eb, used: (j, 0)),
        ),
        out_shape=jax.ShapeDtypeStruct((xs.shape[0], D_MODEL), F32),
        compiler_params=pltpu.CompilerParams(dimension_semantics=("arbitrary",), vmem_limit_bytes=VMEM_LIMIT),
        name="moe_experts",
    )(ea, eb, used, xs, wg, wu, wd, wg, wu, wd)


def _combine_kernel(pos_ref, next_pos_ref, x1_ref, g_ref, ys_hbm, out_ref, y_scr, sem):
    i = pl.program_id(0)
    tm = x1_ref.shape[0]
    slot = i & 1

    def gather(p_ref, s):
        for k in range(tm):
            _row_copy(ys_hbm, p_ref[0, 0, k], y_scr.at[s], k, sem.at[s]).start(priority=k % DMA_THREADS)

    @pl.when(i == 0)
    def _():
        gather(pos_ref, 0)

    @pl.when(i + 1 < pl.num_programs(0))
    def _():
        gather(next_pos_ref, 1 - slot)

    _row_copy(ys_hbm, 0, y_scr.at[slot], 0, sem.at[slot], rows=tm).wait()
    x2 = x1_ref[...] + y_scr[slot]
    ms = jnp.mean(x2 * x2, axis=-1, keepdims=True)
    out_ref[...] = x2 * lax.rsqrt(ms + NORM_EPS) * g_ref[...]


def _combine(pos3, x1, g, ys):
    n_chunks, _, tm = pos3.shape
    return pl.pallas_call(
        _combine_kernel,
        grid=(n_chunks,),
        in_specs=[
            pl.BlockSpec((1, 1, tm), lambda i: (i, 0, 0), memory_space=pltpu.SMEM),
            pl.BlockSpec((1, 1, tm), lambda i: (jnp.minimum(i + 1, n_chunks - 1), 0, 0), memory_space=pltpu.SMEM),
            pl.BlockSpec((tm, D_MODEL), lambda i: (i, 0)),
            pl.BlockSpec((1, D_MODEL), lambda i: (0, 0)),
            pl.BlockSpec(memory_space=pl.ANY),
        ],
        out_specs=pl.BlockSpec((tm, D_MODEL), lambda i: (i, 0)),
        out_shape=jax.ShapeDtypeStruct(x1.shape, F32),
        scratch_shapes=[pltpu.VMEM((2, tm, D_MODEL), F32), pltpu.SemaphoreType.DMA((2,))],
        compiler_params=pltpu.CompilerParams(dimension_semantics=("arbitrary",), vmem_limit_bytes=VMEM_LIMIT),
        name="moe_combine",
    )(pos3, pos3, x1, g, ys)


def _moe(h2ext, cls3, x1, wg, wu, wd, g, *, tile=MOE_TILE, dispatch_chunk=2048, combine_chunk=512):
    n = x1.shape[0]
    n_tiles = n // tile + N_CLASSES
    assert n_tiles <= LANES
    pos, tile_cls, ends = _sort(cls3.reshape(n // LANES, LANES), tile=tile)
    xs = _dispatch(pos.reshape(n // dispatch_chunk, 1, dispatch_chunk), ends, h2ext, n_rows=n_tiles * tile, tile=tile)
    used = ends[0, N_CLASSES - 1:N_CLASSES] // tile
    tc = tile_cls[0, :n_tiles]
    tc = jnp.where(jnp.arange(n_tiles) < used[0], tc, tc[jnp.maximum(used[0] - 1, 0)])
    pair_a = jnp.array([p[0] for p in PAIR_ORDER], jnp.int32)
    pair_b = jnp.array([p[1] for p in PAIR_ORDER], jnp.int32)
    ea = (tc // N_PAIRS) * EXPERTS_PER_GROUP + pair_a[tc % N_PAIRS]
    eb = (tc // N_PAIRS) * EXPERTS_PER_GROUP + pair_b[tc % N_PAIRS]
    ys = _experts(ea, eb, used, xs, wg, wu, wd, tile=tile)
    return _combine(pos.reshape(n // combine_chunk, 1, combine_chunk), x1, g, ys)


def _rotary_tables(seq):
    pos = np.arange(seq, dtype=np.float64)
    inv_freq = 1.0 / (ROPE_THETA ** (np.arange(0, ROT_DIM, 2, dtype=np.float64) / ROT_DIM))
    ang = pos[:, None] * inv_freq[None, :]
    cos, sin = np.cos(ang), np.sin(ang)
    zeros = np.zeros((seq, HEAD_DIM - ROT_DIM))
    zh = np.zeros((seq, ROT_HALF))
    cos_h = np.concatenate([cos, cos, np.ones((seq, HEAD_DIM - ROT_DIM))], axis=1)
    sa_h = np.concatenate([-sin, zh, zeros], axis=1)
    sb_h = np.concatenate([zh, sin, zeros], axis=1)
    tile = lambda t: jnp.asarray(np.concatenate([t, t], axis=1), F32)
    return tile(cos_h), tile(sa_h), tile(sb_h)


def kernel(x, attn_norm, w_in, b_forget, w_out, ffn_norm, w_group, w_expert, w_gate_e, w_up_e, w_down_e, final_norm):
    b, seq, d = x.shape
    assert d == D_MODEL and w_in.shape[0] == 1, "single-layer block"
    n = b * seq
    scale = HEAD_DIM ** -0.5 * LOG2E
    col_scale = np.ones((2 * QKV_WIDTH + LANES,), np.float32)
    col_scale[0:WIDTH_A] = scale
    col_scale[QKV_WIDTH:QKV_WIDTH + WIDTH_B] = scale
    w = (jnp.pad(w_in[0].T, ((0, LANES - HEADS_B), (0, 0))) * col_scale[:, None]).astype(BF16)
    bf = jnp.pad(b_forget[0].astype(F32), (0, LANES - HEADS_B))[None, :]
    cosv, sa, sb = _rotary_tables(seq)
    x2 = x.reshape(n, d)

    qkva, qkvb, c = _inproj(x2, attn_norm[0][None, :], w, cosv, sa, sb, bf, seq=seq, tm=512)
    out_a = _dilated(qkva.reshape(b, seq, QKV_WIDTH))
    out_b, wg, wu, wd = _fox(qkvb.reshape(b, seq, QKV_WIDTH), c.reshape(b, seq, LANES),
                             w_gate_e[0], w_up_e[0], w_down_e[0], tq=1024)

    wr = jnp.pad(jnp.concatenate([w_group[0], w_expert[0]], axis=1).astype(F32),
                 ((0, 0), (0, LANES - N_GROUPS - N_EXPERTS)))
    wr_hi = wr.astype(BF16)
    wr = jnp.concatenate([wr_hi, (wr - wr_hi.astype(F32)).astype(BF16)], axis=1)
    x1, h2ext, cls3 = _outproj(out_a.reshape(n, WIDTH_A), out_b.reshape(n, WIDTH_B), x2,
                               w_out[0].astype(BF16), ffn_norm[0][None, :], wr, tm=512)
    out = _moe(h2ext, cls3, x1, wg, wu, wd, final_norm[None, :])
    return out.reshape(b, seq, d)
```

```python
import functools

import jax
import jax.numpy as jnp
import numpy as np
from jax import lax
from jax.experimental import pallas as pl
from jax.experimental.pallas import tpu as pltpu

F32 = jnp.float32
BF16 = jnp.bfloat16

D_MODEL = 1024
HEAD_DIM = 64
HEADS_A = 8
HEADS_B = 8
WIDTH_A = HEADS_A * HEAD_DIM
WIDTH_B = HEADS_B * HEAD_DIM
QKV_WIDTH = 3 * WIDTH_A
DILATIONS = ((128, 1), (512, 4), (2048, 16))
ROT_DIM = HEAD_DIM // 4
ROT_HALF = ROT_DIM // 2
ROPE_THETA = 500000.0
N_GROUPS = 4
EXPERTS_PER_GROUP = 4
N_EXPERTS = N_GROUPS * EXPERTS_PER_GROUP
D_EXPERT = 512
NORM_EPS = 1e-6
NEG_INF = -1e30

LANES = 128
HEAD_PAIRS = WIDTH_A // LANES
WIN = 128
MAX_DIL = 16
MID_DIL = 4
N_MID = MAX_DIL // MID_DIL
assert sorted(d for _, d in DILATIONS) == [1, MID_DIL, MAX_DIL] and all(w // d == WIN for w, d in DILATIONS)
PERM = MAX_DIL * MAX_DIL
KV_BLOCK = 256
N_PAIRS = EXPERTS_PER_GROUP * (EXPERTS_PER_GROUP - 1) // 2
N_CLASSES = N_GROUPS * N_PAIRS
PAIR_ORDER = ((0, 1), (0, 2), (1, 2), (1, 3), (0, 3), (2, 3))
assert len(PAIR_ORDER) == N_PAIRS
MOE_TILE = 256
EXT_WIDTH = D_MODEL + LANES
LOGIT_MAX = 60.0
BRANCH_UNROLL = 16
DMA_THREADS = 2
ROUTER_OFF = N_GROUPS
VMEM_LIMIT = 56 * 1024 * 1024
LOG2E = 1.4426950408889634
BOUND_SLACK = 1.0 + 2.0 ** -6
BOUND_EPS = 2.0 ** -7


N_PARTS = 3


def _split3(x):
    hi = x.astype(BF16)
    r1 = x - hi.astype(F32)
    mid = r1.astype(BF16)
    lo = (r1 - mid.astype(F32)).astype(BF16)
    return hi, mid, lo


def _dot_nt(a, b):
    return lax.dot_general(a, b, (((1,), (1,)), ((), ())), preferred_element_type=F32)


def _inproj_kernel(x_ref, g_ref, w_ref, cos_ref, sa_ref, sb_ref, bf_ref,
                   qkva_ref, qkvb_ref, c_ref, h_scr, carry_scr, *, tiles_per_seq):
    i = pl.program_id(0)
    tm = x_ref.shape[0]
    x = x_ref[...]
    ms = jnp.mean(x * x, axis=-1, keepdims=True)
    h_scr[...] = (x * lax.rsqrt(ms + NORM_EPS) * g_ref[...]).astype(BF16)
    cosv = cos_ref[...]
    sa = sa_ref[...]
    sb = sb_ref[...]
    for g in range(6):
        p = _dot_nt(h_scr[...], w_ref[g * WIDTH_A:(g + 1) * WIDTH_A, :])
        dst = qkva_ref if g < 3 else qkvb_ref
        col0 = (g % 3) * WIDTH_A
        if g < 2:
            for k in range(HEAD_PAIRS):
                t = p[:, k * LANES:(k + 1) * LANES]
                t = (t * cosv + pltpu.roll(t, LANES - ROT_HALF, 1) * sa + pltpu.roll(t, ROT_HALF, 1) * sb)
                dst[:, col0 + k * LANES:col0 + (k + 1) * LANES] = t.astype(BF16)
        else:
            dst[:, col0:col0 + WIDTH_A] = p.astype(BF16)
    z = _dot_nt(h_scr[...], w_ref[2 * QKV_WIDTH:2 * QKV_WIDTH + LANES, :]) + bf_ref[...]
    lf = jnp.minimum(z, 0.0) - jnp.log1p(jnp.exp(-jnp.abs(z)))
    lane = lax.broadcasted_iota(jnp.int32, (tm, LANES), 1)
    lf = jnp.where(lane < HEADS_B, lf * LOG2E, 0.0)
    hi, mid, lo = _split3(lf)
    tri = (lax.broadcasted_iota(jnp.int32, (tm, tm), 0) >= lax.broadcasted_iota(jnp.int32, (tm, tm), 1)).astype(BF16)
    two = jnp.dot(tri, jnp.concatenate([hi, mid], axis=1), preferred_element_type=F32)
    cs = two[:, 0:LANES] + two[:, LANES:2 * LANES] + jnp.dot(tri, lo, preferred_element_type=F32)

    @pl.when(i % tiles_per_seq == 0)
    def _():
        carry_scr[...] = jnp.zeros_like(carry_scr)

    c = cs + carry_scr[0:1, :]
    c_ref[...] = c
    carry_scr[...] = jnp.broadcast_to(c[tm - 1:tm, :], carry_scr.shape)


def _inproj(x2, g, w, cosv, sa, sb, bf, *, seq, tm):
    n = x2.shape[0]
    kern = functools.partial(_inproj_kernel, tiles_per_seq=seq // tm)
    tps = seq // tm
    return pl.pallas_call(
        kern,
        grid=(n // tm,),
        in_specs=[
            pl.BlockSpec((tm, D_MODEL), lambda i: (i, 0)),
            pl.BlockSpec((1, D_MODEL), lambda i: (0, 0)),
            pl.BlockSpec(w.shape, lambda i: (0, 0)),
            pl.BlockSpec((tm, LANES), lambda i: (i % tps, 0)),
            pl.BlockSpec((tm, LANES), lambda i: (i % tps, 0)),
            pl.BlockSpec((tm, LANES), lambda i: (i % tps, 0)),
            pl.BlockSpec((1, LANES), lambda i: (0, 0)),
        ],
        out_specs=[
            pl.BlockSpec((tm, QKV_WIDTH), lambda i: (i, 0)),
            pl.BlockSpec((tm, QKV_WIDTH), lambda i: (i, 0)),
            pl.BlockSpec((tm, LANES), lambda i: (i, 0)),
        ],
        out_shape=[
            jax.ShapeDtypeStruct((n, QKV_WIDTH), BF16),
            jax.ShapeDtypeStruct((n, QKV_WIDTH), BF16),
            jax.ShapeDtypeStruct((n, LANES), F32),
        ],
        scratch_shapes=[pltpu.VMEM((tm, D_MODEL), BF16), pltpu.VMEM((8, LANES), F32)],
        compiler_params=pltpu.CompilerParams(dimension_semantics=("arbitrary",), vmem_limit_bytes=VMEM_LIMIT),
        name="inproj",
    )(x2, g, w, cosv, sa, sb, bf)


def _own_lanes(lane, hl):
    return (lane < HEAD_DIM) if hl == 0 else (lane >= HEAD_DIM)


def _head_sum_selector():
    lane = lax.broadcasted_iota(jnp.int32, (LANES, LANES), 0)
    col = lax.broadcasted_iota(jnp.int32, (LANES, LANES), 1)
    return jnp.where(((col == 0) & (lane < HEAD_DIM)) | ((col == 1) & (lane >= HEAD_DIM)), 1.0, 0.0).astype(BF16)


def _max_sq_norm(t, selector):
    return jnp.max(jnp.dot((t * t).astype(BF16), selector, preferred_element_type=F32), axis=0, keepdims=True)


def _norm_bound(qsq, ksq_max):
    return jnp.sqrt(qsq * ksq_max) * BOUND_SLACK + BOUND_EPS


def _fox_kernel(q_ref, k_ref, v_ref, c_ref, wg_ref, wu_ref, wd_ref, o_ref, wg_out, wu_out, wd_out,
                q0_scr, q1_scr, k0_scr, k1_scr, v0_scr, v1_scr, acc0_scr, acc1_scr, s_scr,
                *, tq, seq, chunk):
    hp = pl.program_id(1)
    for src, dst in ((wg_ref, wg_out), (wu_ref, wu_out), (wd_ref, wd_out)):
        dst[...] = src[...].astype(dst.dtype)
    q_scr = (q0_scr, q1_scr)
    k_scr = (k0_scr, k1_scr)
    v_scr = (v0_scr, v1_scr)
    acc_scr = (acc0_scr, acc1_scr)

    def placement(part):
        row = lax.broadcasted_iota(jnp.int32, (LANES, 2 * LANES), 0)
        col = lax.broadcasted_iota(jnp.int32, (LANES, 2 * LANES), 1)
        m = jnp.zeros((LANES, 2 * LANES), F32)
        for hl in range(2):
            first = HEAD_DIM if hl == 0 else 0
            src = row == 2 * hp + hl
            m = jnp.where(src & (col == first + part), 1.0, m)
            m = jnp.where(src & (col == LANES + first + N_PARTS + part), -1.0, m)
            if part == 0:
                ones_q = (col >= first + N_PARTS) & (col < first + 2 * N_PARTS)
                ones_k = (col >= LANES + first) & (col < LANES + first + N_PARTS)
                m = jnp.where((row == LANES - 1) & (ones_q | ones_k), 1.0, m)
        return m.astype(BF16)

    def build():
        lane = lax.broadcasted_iota(jnp.int32, (chunk, LANES), 1)
        place_hi_mid = jnp.concatenate([placement(0), placement(1)], axis=0)
        place_lo = placement(2)

        def body(ci, sq_max):
            rows = pl.ds(pl.multiple_of(ci * chunk, chunk), chunk)
            qc = q_ref[0, rows, :].astype(F32)
            kc = k_ref[0, rows, :].astype(F32)
            vc = v_ref[0, rows, :].astype(F32)
            hi, mid, lo = _split3(jnp.where(lane == LANES - 1, 1.0, c_ref[0, rows, :]))
            aug = (jnp.dot(jnp.concatenate([hi, mid], axis=1), place_hi_mid, preferred_element_type=F32)
                   + jnp.dot(lo, place_lo, preferred_element_type=F32))
            for hl in range(2):
                own = _own_lanes(lane, hl)
                q_scr[hl][rows, :] = jnp.where(own, qc, aug[:, 0:LANES]).astype(BF16)
                k_scr[hl][rows, :] = jnp.where(own, kc, aug[:, LANES:2 * LANES]).astype(BF16)
                one_lane = HEAD_DIM if hl == 0 else 0
                v_scr[hl][rows, :] = jnp.where(own, vc, jnp.where(lane == one_lane, 1.0, 0.0)).astype(BF16)
            return (jnp.maximum(sq_max[0], _max_sq_norm(qc, selector)),
                    jnp.maximum(sq_max[1], _max_sq_norm(kc, selector)))

        selector = _head_sum_selector()
        sq_max = lax.fori_loop(0, seq // chunk, body, (jnp.zeros((1, LANES), F32),) * 2)
        return jnp.max(_norm_bound(sq_max[0], sq_max[1]))

    worst = build()
    lane_q = lax.broadcasted_iota(jnp.int32, (tq, LANES), 1)
    n_diag = tq // KV_BLOCK
    row_minus_col = (lax.broadcasted_iota(jnp.int32, (tq, KV_BLOCK), 0)
                     - lax.broadcasted_iota(jnp.int32, (tq, KV_BLOCK), 1))

    def block_rows(j):
        start = j * KV_BLOCK
        return pl.ds(start if isinstance(start, int) else pl.multiple_of(start, KV_BLOCK), KV_BLOCK)

    def causal_mask(s, d, top=0):
        return jnp.where(row_minus_col[top:tq, :] >= d * KV_BLOCK, s, NEG_INF)

    def logits(qi, hl, j, top=0):
        return lax.dot_general(q_scr[hl][pl.ds(qi * tq + top, tq - top), :], k_scr[hl][block_rows(j), :],
                               (((1,), (1,)), ((), ())), preferred_element_type=F32)

    def values(hl, j):
        return v_scr[hl][block_rows(j), :]

    def finish(qi, a0, a1):
        l0 = a0[:, HEAD_DIM:HEAD_DIM + 1]
        l1 = a1[:, 0:1]
        o_ref[0, pl.ds(qi * tq, tq), :] = jnp.where(lane_q < HEAD_DIM, a0 / l0, a1 / l1).astype(o_ref.dtype)

    def fast_tile(qi):
        n_full = qi * n_diag
        acc0_scr[...] = jnp.zeros_like(acc0_scr)
        acc1_scr[...] = jnp.zeros_like(acc1_scr)

        def produce(j, slot, top=0):
            for hl in range(2):
                s_scr[slot, hl, top:tq, :] = logits(qi, hl, j, top)

        def consume(j, slot, diag):
            top = 0 if diag is None else diag * KV_BLOCK
            for hl in range(2):
                s = s_scr[slot, hl, top:tq, :]
                if diag is not None:
                    s = causal_mask(s, diag, top)
                acc_scr[hl][top:tq, :] += jnp.dot(jnp.exp2(s).astype(BF16), values(hl, j),
                                                  preferred_element_type=F32)

        for j in range(min(n_diag, n_full + n_diag)):
            produce(j, j % n_diag, 0 if j < n_full else (j - n_full) * KV_BLOCK)
        for j in range(n_full + n_diag):
            consume(j, j % n_diag, None if j < n_full else j - n_full)
            nxt = j + n_diag
            if nxt < n_full + n_diag:
                produce(nxt, nxt % n_diag, 0 if nxt < n_full else (nxt - n_full) * KV_BLOCK)
        finish(qi, acc0_scr[...], acc1_scr[...])

    def general_tile(qi):
        n_full = qi * n_diag

        def step(j, carry, diag):
            new = []
            for hl in range(2):
                m, a = carry[2 * hl:2 * hl + 2]
                s = logits(qi, hl, j)
                if diag is not None:
                    s = causal_mask(s, diag)
                mn = jnp.maximum(m, jnp.max(s, axis=1, keepdims=True))
                a = jnp.exp2(m - mn) * a + jnp.dot(jnp.exp2(s - mn).astype(BF16), values(hl, j),
                                                  preferred_element_type=F32)
                new += [mn, a]
            return tuple(new)

        carry = (jnp.full((tq, 1), NEG_INF, F32), jnp.zeros((tq, LANES), F32)) * 2
        carry = lax.fori_loop(0, n_full, lambda j, c: step(j, c, None), carry)
        for d in range(n_diag):
            carry = step(n_full + d, carry, d)
        finish(qi, carry[1], carry[3])

    for tile_fn, cond in ((fast_tile, worst <= LOGIT_MAX), (general_tile, worst > LOGIT_MAX)):
        @pl.when(cond)
        def _():
            for qi in range(seq // tq):
                tile_fn(qi)


def _fox(qkvb, c, wg, wu, wd, *, tq, chunk=512):
    b, seq, _ = qkvb.shape
    steps = b * HEAD_PAIRS
    parts = max(steps // N_EXPERTS, 1)
    per = max(N_EXPERTS // steps, 1)
    assert parts * N_EXPERTS == steps * per and D_MODEL % parts == 0 and D_EXPERT % parts == 0

    def slice_spec(rows, cols):
        def index(bi, hp):
            step = bi * HEAD_PAIRS + hp
            return (step // parts, step % parts, 0)
        return pl.BlockSpec((per, rows // parts, cols), index)

    w_specs = [slice_spec(D_MODEL, D_EXPERT), slice_spec(D_MODEL, D_EXPERT), slice_spec(D_EXPERT, D_MODEL)]
    kern = functools.partial(_fox_kernel, tq=tq, seq=seq, chunk=chunk)
    return pl.pallas_call(
        kern,
        grid=(b, HEAD_PAIRS),
        in_specs=[
            pl.BlockSpec((1, seq, LANES), lambda bi, hp: (bi, 0, hp)),
            pl.BlockSpec((1, seq, LANES), lambda bi, hp: (bi, 0, HEAD_PAIRS + hp)),
            pl.BlockSpec((1, seq, LANES), lambda bi, hp: (bi, 0, 2 * HEAD_PAIRS + hp)),
            pl.BlockSpec((1, seq, LANES), lambda bi, hp: (bi, 0, 0)),
        ] + w_specs,
        out_specs=[pl.BlockSpec((1, seq, LANES), lambda bi, hp: (bi, 0, hp))] + w_specs,
        out_shape=[jax.ShapeDtypeStruct((b, seq, WIDTH_B), BF16)]
        + [jax.ShapeDtypeStruct(w.shape, BF16) for w in (wg, wu, wd)],
        scratch_shapes=[pltpu.VMEM((seq, LANES), BF16)] * 6 + [
                        pltpu.VMEM((tq, LANES), F32), pltpu.VMEM((tq, LANES), F32),
                        pltpu.VMEM((tq // KV_BLOCK, 2, tq, KV_BLOCK), F32)],
        compiler_params=pltpu.CompilerParams(
            dimension_semantics=("arbitrary", "arbitrary"), vmem_limit_bytes=VMEM_LIMIT),
        name="fox",
    )(qkvb, qkvb, qkvb, c, wg, wu, wd)


def _log2(n):
    assert n & (n - 1) == 0
    return n.bit_length() - 1


def _permutation():
    ri = lax.broadcasted_iota(jnp.int32, (PERM, PERM), 0)
    ci = lax.broadcasted_iota(jnp.int32, (PERM, PERM), 1)
    sh, lo = _log2(MAX_DIL), MAX_DIL - 1
    return (((ri >> sh) == (ci & lo)) & ((ri & lo) == (ci >> sh))).astype(BF16)


def _band_masks():
    qa = lax.broadcasted_iota(jnp.int32, (WIN, 2 * WIN), 0)
    kb = lax.broadcasted_iota(jnp.int32, (WIN, 2 * WIN), 1)
    cur = kb >= WIN
    kin = kb & (WIN - 1)
    prev_shift = jnp.where(cur, 0, WIN)
    dist = qa - kin + prev_shift
    chunk = WIN // N_MID
    offset = lambda a: ((a & (chunk - 1)) << _log2(N_MID)) + (a >> _log2(chunk))
    dist_mid = offset(qa) - offset(kin) + prev_shift
    return (dist >= 0) & (dist <= WIN), (dist_mid >= 0) & (dist_mid <= WIN), cur


def _dilated_general(q_ref, k_ref, v_ref, o_ref, q16, k16, v16, m_s, l_s, acc_s, on_s, lse_s, *, seq):
    sub = seq // MAX_DIL
    own0 = lax.broadcasted_iota(jnp.int32, (WIN, LANES), 1) < HEAD_DIM
    perm = _permutation()
    band_nat, band_4, cur = _band_masks()

    def attend(qb, kb, vb, mask, state):
        qf = qb.astype(F32)
        new = []
        for hl in range(2):
            m, l, a = state[3 * hl:3 * hl + 3]
            own = own0 if hl == 0 else jnp.logical_not(own0)
            qh = jnp.where(own, qf, 0.0).astype(BF16)
            s = lax.dot_general(qh, kb, (((1,), (1,)), ((), ())), preferred_element_type=F32)
            s = jnp.where(mask, s, NEG_INF)
            mn = jnp.maximum(m, jnp.max(s, axis=1, keepdims=True))
            alpha = jnp.exp2(m - mn)
            p = jnp.exp2(s - mn)
            l = alpha * l + jnp.sum(p, axis=1, keepdims=True)
            a = alpha * a + jnp.dot(p.astype(BF16), vb, preferred_element_type=F32)
            new += [mn, l, a]
        return new

    def pack_state(st):
        m0, l0, a0, m1, l1, a1 = st
        return (jnp.where(own0, m0, m1), jnp.where(own0, l0, l1), jnp.where(own0, a0, a1))

    def deint(blk, carry):
        r0 = pl.multiple_of(blk * PERM, PERM)
        j0 = pl.multiple_of(blk * MAX_DIL, MAX_DIL)
        for src, dst in ((q_ref, q16), (k_ref, k16), (v_ref, v16)):
            y = jnp.dot(perm, src[0, pl.ds(r0, PERM), :], preferred_element_type=F32).astype(BF16)
            for r in range(MAX_DIL):
                dst[r, pl.ds(j0, MAX_DIL), :] = y[r * MAX_DIL:(r + 1) * MAX_DIL, :]
        return carry

    lax.fori_loop(0, seq // PERM, deint, 0)

    fresh = [jnp.full((WIN, 1), NEG_INF, F32), jnp.zeros((WIN, 1), F32), jnp.zeros((WIN, LANES), F32)] * 2
    nb16 = sub // WIN

    def d16_body(t, carry):
        r = t // nb16
        n = t % nb16
        c0 = pl.multiple_of(n * WIN, WIN)
        p0 = pl.multiple_of(jnp.maximum(n - 1, 0) * WIN, WIN)
        qb = q16[r, pl.ds(c0, WIN), :]
        kb = jnp.concatenate([k16[r, pl.ds(p0, WIN), :], k16[r, pl.ds(c0, WIN), :]], axis=0)
        vb = jnp.concatenate([v16[r, pl.ds(p0, WIN), :], v16[r, pl.ds(c0, WIN), :]], axis=0)
        mask = band_nat & (cur | (n > 0))
        mm, ll, aa = pack_state(attend(qb, kb, vb, mask, fresh))
        row = pl.multiple_of(r * sub + c0, WIN)
        m_s[pl.ds(row, WIN), :] = mm
        l_s[pl.ds(row, WIN), :] = ll
        acc_s[pl.ds(row, WIN), :] = aa
        return carry

    lax.fori_loop(0, MAX_DIL * nb16, d16_body, 0)

    nb4 = (seq // MID_DIL) // WIN
    ch = WIN // N_MID

    def d4_body(t, carry):
        r4 = t // nb4
        n = t % nb4
        c0 = pl.multiple_of(n * ch, ch)
        p0 = pl.multiple_of(jnp.maximum(n - 1, 0) * ch, ch)

        def gather(ref, j0):
            return [ref[r4 + MID_DIL * q, pl.ds(j0, ch), :] for q in range(N_MID)]

        qb = jnp.concatenate(gather(q16, c0), axis=0)
        kb = jnp.concatenate(gather(k16, p0) + gather(k16, c0), axis=0)
        vb = jnp.concatenate(gather(v16, p0) + gather(v16, c0), axis=0)
        rows = [pl.multiple_of((r4 + MID_DIL * q) * sub + c0, ch) for q in range(N_MID)]
        mm = jnp.concatenate([m_s[pl.ds(rw, ch), :] for rw in rows], axis=0)
        ll = jnp.concatenate([l_s[pl.ds(rw, ch), :] for rw in rows], axis=0)
        aa = jnp.concatenate([acc_s[pl.ds(rw, ch), :] for rw in rows], axis=0)
        state = [mm[:, 0:1], ll[:, 0:1], aa, mm[:, HEAD_DIM:HEAD_DIM + 1], ll[:, HEAD_DIM:HEAD_DIM + 1], aa]
        mask = band_4 & (cur | (n > 0))
        mm, ll, aa = pack_state(attend(qb, kb, vb, mask, state))
        for q, rw in enumerate(rows):
            m_s[pl.ds(rw, ch), :] = mm[q * ch:(q + 1) * ch, :]
            l_s[pl.ds(rw, ch), :] = ll[q * ch:(q + 1) * ch, :]
            acc_s[pl.ds(rw, ch), :] = aa[q * ch:(q + 1) * ch, :]
        return carry

    lax.fori_loop(0, MID_DIL * nb4, d4_body, 0)

    def renat(blk, carry):
        j0 = pl.multiple_of(blk * MAX_DIL, MAX_DIL)
        r0 = pl.multiple_of(blk * PERM, PERM)
        rows = [pl.multiple_of(r * sub + j0, MAX_DIL) for r in range(MAX_DIL)]
        mm = jnp.concatenate([m_s[pl.ds(rw, MAX_DIL), :] for rw in rows], axis=0)
        ll = jnp.concatenate([l_s[pl.ds(rw, MAX_DIL), :] for rw in rows], axis=0)
        aa = jnp.concatenate([acc_s[pl.ds(rw, MAX_DIL), :] for rw in rows], axis=0)
        o = (aa / ll).astype(BF16)
        hi, mid, lo = _split3(mm + jnp.log2(ll))
        on_s[pl.ds(r0, PERM), :] = jnp.dot(perm, o, preferred_element_type=F32)
        lse_s[pl.ds(r0, PERM), :] = (jnp.dot(perm, hi, preferred_element_type=F32)
                                     + jnp.dot(perm, mid, preferred_element_type=F32)
                                     + jnp.dot(perm, lo, preferred_element_type=F32))
        return carry

    lax.fori_loop(0, seq // PERM, renat, 0)

    def d1_body(n, carry):
        c0 = pl.multiple_of(n * WIN, WIN)
        p0 = pl.multiple_of(jnp.maximum(n - 1, 0) * WIN, WIN)
        qb = q_ref[0, pl.ds(c0, WIN), :]
        kb = jnp.concatenate([k_ref[0, pl.ds(p0, WIN), :], k_ref[0, pl.ds(c0, WIN), :]], axis=0)
        vb = jnp.concatenate([v_ref[0, pl.ds(p0, WIN), :], v_ref[0, pl.ds(c0, WIN), :]], axis=0)
        lse = lse_s[pl.ds(c0, WIN), :]
        o = on_s[pl.ds(c0, WIN), :]
        one = jnp.ones((WIN, 1), F32)
        state = [lse[:, 0:1], one, o, lse[:, HEAD_DIM:HEAD_DIM + 1], one, o]
        mask = band_nat & (cur | (n > 0))
        m0, l0, a0, m1, l1, a1 = attend(qb, kb, vb, mask, state)
        o_ref[0, pl.ds(c0, WIN), :] = jnp.where(own0, a0 / l0, a1 / l1).astype(o_ref.dtype)
        return carry

    lax.fori_loop(0, seq // WIN, d1_body, 0)


def _dilated_fast(o_ref, qn, kn, vn, qr, kr, vr, fr, fn, s_scr, mask_scr, *, seq):
    sub = seq // MAX_DIL
    own0 = lax.broadcasted_iota(jnp.int32, (WIN, LANES), 1) < HEAD_DIM
    head_mask = [jnp.where(own0, 1.0, 0.0).astype(BF16), jnp.where(own0, 0.0, 1.0).astype(BF16)]
    perm = _permutation()

    band_nat, band_4, cur = _band_masks()
    for i, band in enumerate((band_nat, band_4)):
        mask_scr[2 * i] = jnp.where(band & cur, 1.0, 0.0).astype(BF16)
        mask_scr[2 * i + 1] = jnp.where(band, 1.0, 0.0).astype(BF16)

    def run_branch(nblocks, block_in_seq, fetch, mask_base, sink, before_trip=None):
        def produce(t, pair, u):
            for hl in range(2):
                qb, kb = fetch(t, hl, True)
                s_scr[pair, u, hl] = lax.dot_general(qb * head_mask[hl], kb, (((1,), (1,)), ((), ())),
                                                     preferred_element_type=F32)

        def consume(t, pair, u):
            mk = mask_scr[mask_base + jnp.minimum(block_in_seq(t), 1)]
            pv = []
            for hl in range(2):
                p = jnp.exp2(s_scr[pair, u, hl]).astype(BF16) * mk
                pv.append(jnp.dot(p, fetch(t, hl, False), preferred_element_type=F32))
            sink(t, pv)

        for u in range(BRANCH_UNROLL):
            produce(u, 0, u)

        def trip(i, last):
            pair = i & 1
            if before_trip is not None:
                before_trip(i)
            for u in range(BRANCH_UNROLL):
                consume(BRANCH_UNROLL * i + u, pair, u)
            if not last:
                for u in range(BRANCH_UNROLL):
                    produce(BRANCH_UNROLL * (i + 1) + u, 1 - pair, u)

        def body(i, carry):
            trip(i, False)
            return carry

        trips = nblocks // BRANCH_UNROLL
        lax.fori_loop(0, trips - 1, body, 0)
        trip(jnp.int32(trips - 1), True)

    def prev_cur(ref_block, n, size):
        c0 = pl.multiple_of(n * size, size)
        p0 = pl.multiple_of(jnp.maximum(n - 1, 0) * size, size)
        return ref_block(p0) + ref_block(c0)

    nb16 = sub // WIN

    def fetch16(t, hl, qk):
        r, n = t // nb16, t % nb16
        if qk:
            return (qr[hl][r, pl.ds(pl.multiple_of(n * WIN, WIN), WIN), :],
                    jnp.concatenate(prev_cur(lambda s: [kr[hl][r, pl.ds(s, WIN), :]], n, WIN), axis=0))
        return jnp.concatenate(prev_cur(lambda s: [vr[hl][r, pl.ds(s, WIN), :]], n, WIN), axis=0)

    def sink16(t, pv):
        row = pl.multiple_of(t * WIN, WIN)
        for hl in range(2):
            fr[hl][pl.ds(row, WIN), :] = pv[hl]

    run_branch(MAX_DIL * nb16, lambda t: t % nb16, fetch16, 0, sink16)

    nb4 = (seq // MID_DIL) // WIN
    ch = WIN // N_MID

    def fetch4(t, hl, qk):
        r4, n = t // nb4, t % nb4
        gather = lambda ref: (lambda s: [ref[r4 + MID_DIL * q, pl.ds(s, ch), :] for q in range(N_MID)])
        if qk:
            return (jnp.concatenate(gather(qr[hl])(pl.multiple_of(n * ch, ch)), axis=0),
                    jnp.concatenate(prev_cur(gather(kr[hl]), n, ch), axis=0))
        return jnp.concatenate(prev_cur(gather(vr[hl]), n, ch), axis=0)

    def sink4(t, pv):
        r4, n = t // nb4, t % nb4
        for q in range(N_MID):
            row = pl.multiple_of((r4 + MID_DIL * q) * sub + n * ch, ch)
            for hl in range(2):
                fr[hl][pl.ds(row, ch), :] += pv[hl][q * ch:(q + 1) * ch, :]

    run_branch(MID_DIL * nb4, lambda t: t % nb4, fetch4, 2, sink4)

    def renat(i):
        for u in range(BRANCH_UNROLL * WIN // PERM):
            blk = i * (BRANCH_UNROLL * WIN // PERM) + u
            j0 = pl.multiple_of(blk * MAX_DIL, MAX_DIL)
            r0 = pl.multiple_of(blk * PERM, PERM)
            for hl in range(2):
                a = jnp.concatenate([fr[hl][pl.ds(pl.multiple_of(r * sub + j0, MAX_DIL), MAX_DIL), :]
                                     for r in range(MAX_DIL)], axis=0)
                hi = a.astype(BF16)
                lo = (a - hi.astype(F32)).astype(BF16)
                y = jnp.dot(perm, jnp.concatenate([hi, lo], axis=1), preferred_element_type=F32)
                fn[hl][pl.ds(r0, PERM), :] = y[:, 0:LANES] + y[:, LANES:2 * LANES]

    def fetch1(t, hl, qk):
        if qk:
            return (qn[hl][pl.ds(pl.multiple_of(t * WIN, WIN), WIN), :],
                    jnp.concatenate(prev_cur(lambda s: [kn[hl][pl.ds(s, WIN), :]], t, WIN), axis=0))
        return jnp.concatenate(prev_cur(lambda s: [vn[hl][pl.ds(s, WIN), :]], t, WIN), axis=0)

    def sink1(t, pv):
        row = pl.multiple_of(t * WIN, WIN)
        t0 = fn[0][pl.ds(row, WIN), :] + pv[0]
        t1 = fn[1][pl.ds(row, WIN), :] + pv[1]
        o_ref[0, pl.ds(row, WIN), :] = jnp.where(own0, t0 / t0[:, HEAD_DIM:HEAD_DIM + 1],
                                                 t1 / t1[:, 0:1]).astype(o_ref.dtype)

    run_branch(seq // WIN, lambda t: t, fetch1, 0, sink1, before_trip=renat)


def _dilated_kernel(q_ref, k_ref, v_ref, o_ref, vn0, vn1, qr, kr, vr0, vr1,
                    f0, f1, f2, f3, f4, s_scr, mask_scr, *, seq, chunk):
    lane = lax.broadcasted_iota(jnp.int32, (chunk, LANES), 1)
    vn = (vn0, vn1)

    perm = _permutation()

    def stats(ci, sq_max):
        rows = pl.ds(ci * chunk, chunk)
        qb = q_ref[0, rows, :]
        kb = k_ref[0, rows, :]
        vc = v_ref[0, rows, :].astype(F32)
        vb = []
        for hl in range(2):
            ones_lane = HEAD_DIM if hl == 0 else 0
            vb.append(jnp.where(_own_lanes(lane, hl), vc, jnp.where(lane == ones_lane, 1.0, 0.0)).astype(BF16))
            vn[hl][rows, :] = vb[hl]
        for u in range(chunk // PERM):
            j0 = (ci * (chunk // PERM) + u) * MAX_DIL
            for srcs, dsts in (((qb, kb), (qr, kr)), (vb, (vr0, vr1))):
                both = jnp.concatenate([src[u * PERM:(u + 1) * PERM, :] for src in srcs], axis=1)
                y = jnp.dot(perm, both, preferred_element_type=F32).astype(BF16)
                for r in range(MAX_DIL):
                    for half, dst in enumerate(dsts):
                        dst[r, pl.ds(j0, MAX_DIL), :] = y[r * MAX_DIL:(r + 1) * MAX_DIL, half * LANES:(half + 1) * LANES]
        return (jnp.maximum(sq_max[0], _max_sq_norm(qb.astype(F32), selector)),
                jnp.maximum(sq_max[1], _max_sq_norm(kb.astype(F32), selector)))

    selector = _head_sum_selector()
    sq_max = (jnp.zeros((1, LANES), F32),) * 2
    for ci in range(seq // chunk):
        sq_max = stats(ci, sq_max)
    worst = jnp.max(_norm_bound(sq_max[0], sq_max[1]))

    @pl.when(worst <= LOGIT_MAX)
    def _fast():
        qn, kn = q_ref.at[0], k_ref.at[0]
        _dilated_fast(o_ref, (qn, qn), (kn, kn), vn, (qr, qr), (kr, kr), (vr0, vr1), (f0, f1), (f2, f3),
                      s_scr, mask_scr, seq=seq)

    @pl.when(worst > LOGIT_MAX)
    def _general():
        _dilated_general(q_ref, k_ref, v_ref, o_ref, qr, kr, vr0, f0, f1, f2, f3, f4, seq=seq)


def _dilated(qkva, *, chunk=512):
    b, seq, _ = qkva.shape
    sub = seq // MAX_DIL
    kern = functools.partial(_dilated_kernel, seq=seq, chunk=chunk)
    return pl.pallas_call(
        kern,
        grid=(b, HEAD_PAIRS),
        in_specs=[
            pl.BlockSpec((1, seq, LANES), lambda bi, hp: (bi, 0, hp)),
            pl.BlockSpec((1, seq, LANES), lambda bi, hp: (bi, 0, HEAD_PAIRS + hp)),
            pl.BlockSpec((1, seq, LANES), lambda bi, hp: (bi, 0, 2 * HEAD_PAIRS + hp)),
        ],
        out_specs=pl.BlockSpec((1, seq, LANES), lambda bi, hp: (bi, 0, hp)),
        out_shape=jax.ShapeDtypeStruct((b, seq, WIDTH_A), BF16),
        scratch_shapes=(
            [pltpu.VMEM((seq, LANES), BF16)] * 2
            + [pltpu.VMEM((MAX_DIL, sub, LANES), BF16)] * 4
            + [pltpu.VMEM((seq, LANES), F32)] * 5
            + [pltpu.VMEM((2, BRANCH_UNROLL, 2, WIN, 2 * WIN), F32), pltpu.VMEM((4, WIN, 2 * WIN), BF16)]
        ),
        compiler_params=pltpu.CompilerParams(
            dimension_semantics=("arbitrary", "arbitrary"), vmem_limit_bytes=VMEM_LIMIT),
        name="dilated",
    )(qkva, qkva, qkva)


def _outproj_kernel(oa_ref, ob_ref, x_ref, wo_ref, g_ref, wr_ref, x1_ref, h2_ref, cls_ref):
    tm = x_ref.shape[0]
    x1 = (x_ref[...]
          + jnp.dot(oa_ref[...], wo_ref[0:WIDTH_A, :], preferred_element_type=F32)
          + jnp.dot(ob_ref[...], wo_ref[WIDTH_A:WIDTH_A + WIDTH_B, :], preferred_element_type=F32))
    x1_ref[...] = x1
    ms = jnp.mean(x1 * x1, axis=-1, keepdims=True)
    h2 = x1 * lax.rsqrt(ms + NORM_EPS) * g_ref[...]
    h2_ref[:, 0:D_MODEL] = h2
    h2_hi = h2.astype(BF16)
    h2_lo = (h2 - h2_hi.astype(F32)).astype(BF16)
    both = jnp.dot(h2_hi, wr_ref[...], preferred_element_type=F32)
    logits = (both[:, 0:LANES] + both[:, LANES:2 * LANES]
              + jnp.dot(h2_lo, wr_ref[:, 0:LANES], preferred_element_type=F32))
    lane = lax.broadcasted_iota(jnp.int32, (tm, LANES), 1)
    lane_f = lane.astype(F32)
    big = float(LANES)
    gmask = lane < N_GROUPS
    gl = jnp.where(gmask, logits, NEG_INF)
    gmax = jnp.max(gl, axis=1, keepdims=True)
    gsum = jnp.sum(jnp.where(gmask, jnp.exp(gl - gmax), 0.0), axis=1, keepdims=True)
    p_top = 1.0 / gsum
    g_star = jnp.min(jnp.where(gmask & (gl == gmax), lane_f, big), axis=1, keepdims=True)
    lo_lane = ROUTER_OFF + EXPERTS_PER_GROUP * g_star
    emask = (lane_f >= lo_lane) & (lane_f < lo_lane + EXPERTS_PER_GROUP)
    sel = jnp.where(emask, logits, NEG_INF)
    v1 = jnp.max(sel, axis=1, keepdims=True)
    i1 = jnp.min(jnp.where(emask & (sel == v1), lane_f, big), axis=1, keepdims=True)
    rest = emask & (lane_f != i1)
    sel2 = jnp.where(rest, logits, NEG_INF)
    v2 = jnp.max(sel2, axis=1, keepdims=True)
    i2 = jnp.min(jnp.where(rest & (sel2 == v2), lane_f, big), axis=1, keepdims=True)
    e2 = jnp.exp(v2 - v1)
    w1 = p_top / (1.0 + e2)
    w2 = p_top * e2 / (1.0 + e2)
    e1 = i1 - lo_lane
    e2x = i2 - lo_lane
    a = jnp.minimum(e1, e2x)
    b = jnp.maximum(e1, e2x)
    pair = jnp.zeros_like(a)
    for idx, (pa, pb) in enumerate(PAIR_ORDER):
        pair = jnp.where((a == pa) & (b == pb), float(idx), pair)
    cls = g_star * N_PAIRS + pair
    wa = jnp.where(e1 < e2x, w1, w2)
    wb = jnp.where(e1 < e2x, w2, w1)
    h2_ref[:, D_MODEL:EXT_WIDTH] = jnp.where(lane == 0, wa, jnp.where(lane == 1, wb, 0.0))
    row = lax.broadcasted_iota(jnp.int32, (tm, LANES), 0)
    spread = jnp.where(lane == (row & (LANES - 1)), cls, 0.0)
    cls_ref[0] = jnp.sum(spread.reshape(tm // LANES, LANES, LANES), axis=1).astype(jnp.int32)


def _outproj(oa, ob, x2, wo, g, wr, *, tm):
    n = x2.shape[0]
    return pl.pallas_call(
        _outproj_kernel,
        grid=(n // tm,),
        in_specs=[
            pl.BlockSpec((tm, WIDTH_A), lambda i: (i, 0)),
            pl.BlockSpec((tm, WIDTH_B), lambda i: (i, 0)),
            pl.BlockSpec((tm, D_MODEL), lambda i: (i, 0)),
            pl.BlockSpec(wo.shape, lambda i: (0, 0)),
            pl.BlockSpec((1, D_MODEL), lambda i: (0, 0)),
            pl.BlockSpec(wr.shape, lambda i: (0, 0)),
        ],
        out_specs=[
            pl.BlockSpec((tm, D_MODEL), lambda i: (i, 0)),
            pl.BlockSpec((tm, EXT_WIDTH), lambda i: (i, 0)),
            pl.BlockSpec((1, tm // LANES, LANES), lambda i: (i, 0, 0)),
        ],
        out_shape=[
            jax.ShapeDtypeStruct((n, D_MODEL), F32),
            jax.ShapeDtypeStruct((n, EXT_WIDTH), F32),
            jax.ShapeDtypeStruct((n // tm, tm // LANES, LANES), jnp.int32),
        ],
        compiler_params=pltpu.CompilerParams(dimension_semantics=("arbitrary",), vmem_limit_bytes=VMEM_LIMIT),
        name="outproj",
    )(oa, ob, x2, wo, g, wr)


def _sort_kernel(cls_ref, pos_ref, tile_ref, ends_ref, *, tile):
    cls = cls_ref[...]
    rows = cls.shape[0]
    upper = (lax.broadcasted_iota(jnp.int32, (LANES, LANES), 0)
             <= lax.broadcasted_iota(jnp.int32, (LANES, LANES), 1)).astype(BF16)
    earlier_rows = (lax.broadcasted_iota(jnp.int32, (rows, rows), 0)
                    > lax.broadcasted_iota(jnp.int32, (rows, rows), 1)).astype(BF16)
    ones = jnp.ones((LANES, LANES), BF16)
    lane8 = lax.broadcasted_iota(jnp.int32, tile_ref.shape, 1)
    tile_start = lane8.astype(F32) * tile
    off = jnp.zeros((1, 1), F32)
    pos = jnp.zeros(cls.shape, F32)
    tile_cls = jnp.zeros(tile_ref.shape, F32)
    ends = jnp.zeros(tile_ref.shape, F32)
    for c in range(N_CLASSES):
        hit = cls == c
        hot = jnp.where(hit, 1.0, 0.0).astype(BF16)
        in_row = jnp.dot(hot, upper, preferred_element_type=F32)
        row_tot = jnp.dot(hot, ones, preferred_element_type=F32)
        before = jnp.dot(earlier_rows, row_tot.astype(BF16), preferred_element_type=F32)
        count = before[rows - 1:rows, 0:1] + row_tot[rows - 1:rows, 0:1]
        pos = pos + jnp.where(hit, in_row - 1.0 + before + off, 0.0)
        off = off + jnp.ceil(count / tile) * tile
        tile_cls = tile_cls + jnp.where(tile_start >= off, 1.0, 0.0)
        ends = jnp.where(lane8 == c, off, ends)
    pos_ref[...] = pos.astype(jnp.int32)
    tile_ref[...] = tile_cls.astype(jnp.int32)
    ends_ref[...] = ends.astype(jnp.int32)


def _sort(cls2d, *, tile):
    meta = jax.ShapeDtypeStruct((8, LANES), jnp.int32)
    return pl.pallas_call(
        functools.partial(_sort_kernel, tile=tile),
        out_shape=[jax.ShapeDtypeStruct(cls2d.shape, jnp.int32), meta, meta],
        compiler_params=pltpu.CompilerParams(vmem_limit_bytes=VMEM_LIMIT),
        name="moe_sort",
    )(cls2d)


def _row_copy(src, src_row, dst, dst_row, sem, rows=1):
    return pltpu.make_async_copy(src.at[pl.ds(src_row, rows)], dst.at[pl.ds(dst_row, rows)], sem)


def _dispatch_kernel(pos_ref, ends_ref, h2_ref, xs_hbm, zero_scr, sem, *, ch, tile):
    g = pl.program_id(0)

    @pl.when(g == 0)
    def _zero_tails():
        zero_scr[...] = jnp.zeros_like(zero_scr)
        n_tiles = xs_hbm.shape[0] // tile
        min_used = n_tiles - N_CLASSES
        total = ends_ref[0, N_CLASSES - 1]
        for phase in ("start", "wait"):
            def zero_tile(row0):
                cp = pltpu.make_async_copy(zero_scr, xs_hbm.at[pl.ds(pl.multiple_of(row0, tile), tile)], sem)
                cp.start() if phase == "start" else cp.wait()

            for c in range(N_CLASSES):
                end = ends_ref[0, c]
                prev = ends_ref[0, c - 1] if c else 0
                pl.when(end > prev)(functools.partial(zero_tile, end - tile))
                pl.when((min_used + c) * tile >= total)(functools.partial(zero_tile, (min_used + c) * tile))

    for k in range(ch):
        _row_copy(h2_ref, k, xs_hbm, pos_ref[0, 0, k], sem).start(priority=k % DMA_THREADS)
    _row_copy(h2_ref, 0, xs_hbm, 0, sem, rows=ch).wait()


def _dispatch(pos3, ends, h2ext, *, n_rows, tile):
    n_chunks, _, ch = pos3.shape
    return pl.pallas_call(
        functools.partial(_dispatch_kernel, ch=ch, tile=tile),
        grid=(n_chunks,),
        in_specs=[
            pl.BlockSpec((1, 1, ch), lambda g: (g, 0, 0), memory_space=pltpu.SMEM),
            pl.BlockSpec(ends.shape, lambda g: (0, 0), memory_space=pltpu.SMEM),
            pl.BlockSpec((ch, EXT_WIDTH), lambda g: (g, 0)),
        ],
        out_specs=pl.BlockSpec(memory_space=pl.ANY),
        out_shape=jax.ShapeDtypeStruct((n_rows, EXT_WIDTH), F32),
        scratch_shapes=[pltpu.VMEM((tile, EXT_WIDTH), F32), pltpu.SemaphoreType.DMA(())],
        compiler_params=pltpu.CompilerParams(dimension_semantics=("arbitrary",), vmem_limit_bytes=VMEM_LIMIT),
        name="moe_dispatch",
    )(pos3, ends, h2ext)


def _experts_kernel(ea_ref, eb_ref, used_ref, xs_ref, wga, wua, wda, wgb, wub, wdb, ys_ref):
    del ea_ref, eb_ref

    @pl.when(pl.program_id(0) < used_ref[0])
    def _():
        x = xs_ref[:, 0:D_MODEL].astype(BF16)
        wts = xs_ref[:, D_MODEL:EXT_WIDTH]
        y = None
        for col, (wg, wu, wd) in enumerate(((wga, wua, wda), (wgb, wub, wdb))):
            gate = jnp.dot(x, wg[0], preferred_element_type=F32)
            up = jnp.dot(x, wu[0], preferred_element_type=F32)
            he = (gate / (1.0 + jnp.exp(-gate)) * up * wts[:, col:col + 1]).astype(BF16)
            t = jnp.dot(he, wd[0], preferred_element_type=F32)
            y = t if y is None else y + t
        ys_ref[...] = y

    @pl.when(pl.program_id(0) >= used_ref[0])
    def _():
        ys_ref[...] = jnp.zeros_like(ys_ref)


def _experts(ea, eb, used, xs, wg, wu, wd, *, tile):
    n_tiles = xs.shape[0] // tile
    rows = lambda j, ea, eb, used: (jnp.minimum(j, used[0] - 1), 0)
    first = lambda j, ea, eb, used: (ea[j], 0, 0)
    second = lambda j, ea, eb, used: (eb[j], 0, 0)
    up_spec = lambda m: pl.BlockSpec((1, D_MODEL, D_EXPERT), m)
    down_spec = lambda m: pl.BlockSpec((1, D_EXPERT, D_MODEL), m)
    return pl.pallas_call(
        _experts_kernel,
        grid_spec=pltpu.PrefetchScalarGridSpec(
            num_scalar_prefetch=3,
            grid=(n_tiles,),
            in_specs=[pl.BlockSpec((tile, EXT_WIDTH), rows),
                      up_spec(first), up_spec(first), down_spec(first),
                      up_spec(second), up_spec(second), down_spec(second)],
            out_specs=pl.BlockSpec((tile, D_MODEL), lambda j, ea, eb, used: (j, 0)),
        ),
        out_shape=jax.ShapeDtypeStruct((xs.shape[0], D_MODEL), F32),
        compiler_params=pltpu.CompilerParams(dimension_semantics=("arbitrary",), vmem_limit_bytes=VMEM_LIMIT),
        name="moe_experts",
    )(ea, eb, used, xs, wg, wu, wd, wg, wu, wd)


def _combine_kernel(pos_ref, next_pos_ref, x1_ref, g_ref, ys_hbm, out_ref, y_scr, sem):
    i = pl.program_id(0)
    tm = x1_ref.shape[0]
    slot = i & 1

    def gather(p_ref, s):
        for k in range(tm):
            _row_copy(ys_hbm, p_ref[0, 0, k], y_scr.at[s], k, sem.at[s]).start(priority=k % DMA_THREADS)

    @pl.when(i == 0)
    def _():
        gather(pos_ref, 0)

    @pl.when(i + 1 < pl.num_programs(0))
    def _():
        gather(next_pos_ref, 1 - slot)

    _row_copy(ys_hbm, 0, y_scr.at[slot], 0, sem.at[slot], rows=tm).wait()
    x2 = x1_ref[...] + y_scr[slot]
    ms = jnp.mean(x2 * x2, axis=-1, keepdims=True)
    out_ref[...] = x2 * lax.rsqrt(ms + NORM_EPS) * g_ref[...]


def _combine(pos3, x1, g, ys):
    n_chunks, _, tm = pos3.shape
    return pl.pallas_call(
        _combine_kernel,
        grid=(n_chunks,),
        in_specs=[
            pl.BlockSpec((1, 1, tm), lambda i: (i, 0, 0), memory_space=pltpu.SMEM),
            pl.BlockSpec((1, 1, tm), lambda i: (jnp.minimum(i + 1, n_chunks - 1), 0, 0), memory_space=pltpu.SMEM),
            pl.BlockSpec((tm, D_MODEL), lambda i: (i, 0)),
            pl.BlockSpec((1, D_MODEL), lambda i: (0, 0)),
            pl.BlockSpec(memory_space=pl.ANY),
        ],
        out_specs=pl.BlockSpec((tm, D_MODEL), lambda i: (i, 0)),
        out_shape=jax.ShapeDtypeStruct(x1.shape, F32),
        scratch_shapes=[pltpu.VMEM((2, tm, D_MODEL), F32), pltpu.SemaphoreType.DMA((2,))],
        compiler_params=pltpu.CompilerParams(dimension_semantics=("arbitrary",), vmem_limit_bytes=VMEM_LIMIT),
        name="moe_combine",
    )(pos3, pos3, x1, g, ys)


def _moe(h2ext, cls3, x1, wg, wu, wd, g, *, tile=MOE_TILE, dispatch_chunk=2048, combine_chunk=512):
    n = x1.shape[0]
    n_tiles = n // tile + N_CLASSES
    assert n_tiles <= LANES
    pos, tile_cls, ends = _sort(cls3.reshape(n // LANES, LANES), tile=tile)
    xs = _dispatch(pos.reshape(n // dispatch_chunk, 1, dispatch_chunk), ends, h2ext, n_rows=n_tiles * tile, tile=tile)
    used = ends[0, N_CLASSES - 1:N_CLASSES] // tile
    tc = tile_cls[0, :n_tiles]
    tc = jnp.where(jnp.arange(n_tiles) < used[0], tc, tc[jnp.maximum(used[0] - 1, 0)])
    pair_a = jnp.array([p[0] for p in PAIR_ORDER], jnp.int32)
    pair_b = jnp.array([p[1] for p in PAIR_ORDER], jnp.int32)
    ea = (tc // N_PAIRS) * EXPERTS_PER_GROUP + pair_a[tc % N_PAIRS]
    eb = (tc // N_PAIRS) * EXPERTS_PER_GROUP + pair_b[tc % N_PAIRS]
    ys = _experts(ea, eb, used, xs, wg, wu, wd, tile=tile)
    return _combine(pos.reshape(n // combine_chunk, 1, combine_chunk), x1, g, ys)


def _rotary_tables(seq):
    pos = np.arange(seq, dtype=np.float64)
    inv_freq = 1.0 / (ROPE_THETA ** (np.arange(0, ROT_DIM, 2, dtype=np.float64) / ROT_DIM))
    ang = pos[:, None] * inv_freq[None, :]
    cos, sin = np.cos(ang), np.sin(ang)
    zeros = np.zeros((seq, HEAD_DIM - ROT_DIM))
    zh = np.zeros((seq, ROT_HALF))
    cos_h = np.concatenate([cos, cos, np.ones((seq, HEAD_DIM - ROT_DIM))], axis=1)
    sa_h = np.concatenate([-sin, zh, zeros], axis=1)
    sb_h = np.concatenate([zh, sin, zeros], axis=1)
    tile = lambda t: jnp.asarray(np.concatenate([t, t], axis=1), F32)
    return tile(cos_h), tile(sa_h), tile(sb_h)


def kernel(x, attn_norm, w_in, b_forget, w_out, ffn_norm, w_group, w_expert, w_gate_e, w_up_e, w_down_e, final_norm):
    b, seq, d = x.shape
    assert d == D_MODEL and w_in.shape[0] == 1, "single-layer block"
    n = b * seq
    scale = HEAD_DIM ** -0.5 * LOG2E
    col_scale = np.ones((2 * QKV_WIDTH + LANES,), np.float32)
    col_scale[0:WIDTH_A] = scale
    col_scale[QKV_WIDTH:QKV_WIDTH + WIDTH_B] = scale
    w = (jnp.pad(w_in[0].T, ((0, LANES - HEADS_B), (0, 0))) * col_scale[:, None]).astype(BF16)
    bf = jnp.pad(b_forget[0].astype(F32), (0, LANES - HEADS_B))[None, :]
    cosv, sa, sb = _rotary_tables(seq)
    x2 = x.reshape(n, d)

    qkva, qkvb, c = _inproj(x2, attn_norm[0][None, :], w, cosv, sa, sb, bf, seq=seq, tm=512)
    out_a = _dilated(qkva.reshape(b, seq, QKV_WIDTH))
    out_b, wg, wu, wd = _fox(qkvb.reshape(b, seq, QKV_WIDTH), c.reshape(b, seq, LANES),
                             w_gate_e[0], w_up_e[0], w_down_e[0], tq=1024)

    wr = jnp.pad(jnp.concatenate([w_group[0], w_expert[0]], axis=1).astype(F32),
                 ((0, 0), (0, LANES - N_GROUPS - N_EXPERTS)))
    wr_hi = wr.astype(BF16)
    wr = jnp.concatenate([wr_hi, (wr - wr_hi.astype(F32)).astype(BF16)], axis=1)
    x1, h2ext, cls3 = _outproj(out_a.reshape(n, WIDTH_A), out_b.reshape(n, WIDTH_B), x2,
                               w_out[0].astype(BF16), ffn_norm[0][None, :], wr, tm=512)
    out = _moe(h2ext, cls3, x1, wg, wu, wd, final_norm[None, :])
    return out.reshape(b, seq, d)
```
